```python
import jax, jax.numpy as jnp
from jax import lax
import numpy as np

D_MODEL = 2048
BATCH = 8
SEQ = 4096
DEPTH = 2

SWA_Q_HEADS = 16
SWA_KV_HEADS = 2
SWA_HEAD_DIM = 64
SWA_WINDOW = 128
SWA_BLOCK = 128
MLA_HEADS = 16
MLA_Q_RANK = 512
MLA_KV_RANK = 512
MLA_NOPE_DIM = 128
MLA_ROPE_DIM = 64
MLA_V_DIM = 128
MLA_BLOCK = 128
ROPE_THETA = 10000.0
SGU_GROUPS = 8
SGU_GROUP_DIM = 128
SGU_CHUNK = 128
SGU_WIDTH = SGU_GROUPS * SGU_GROUP_DIM
D_FF = 5632
CONV_WIDTH = 3
N_BRANCHES = 3
EPS = 1e-5
MASK_VALUE = -1e30
DN_ALPHA = (2 * DEPTH) ** 0.25
DN_BETA = (8 * DEPTH) ** -0.25

A_Q = SWA_Q_HEADS * SWA_HEAD_DIM
A_KV = SWA_KV_HEADS * SWA_HEAD_DIM
B_OUT = MLA_HEADS * MLA_V_DIM
N_IN = A_Q + 2 * A_KV + MLA_Q_RANK + MLA_KV_RANK + MLA_ROPE_DIM + 2 * SGU_WIDTH + N_BRANCHES * D_MODEL

kernel_name = "hybrid_swa_mla_sgu_deepnorm"


def _layer_norm(x, g, b):
    xf = x.astype(jnp.float32)
    mu = xf.mean(-1, keepdims=True)
    var = jnp.mean(jnp.square(xf - mu), -1, keepdims=True)
    y = (xf - mu) * lax.rsqrt(var + EPS) * g.astype(jnp.float32) + b.astype(jnp.float32)
    return y.astype(x.dtype)


def _rms_norm(x, g):
    xf = x.astype(jnp.float32)
    y = xf * lax.rsqrt(jnp.mean(jnp.square(xf), -1, keepdims=True) + EPS) * g.astype(jnp.float32)
    return y.astype(x.dtype)


def _rope(x, cos, sin):
    x1, x2 = jnp.split(x, 2, axis=-1)
    return jnp.concatenate([x1 * cos - x2 * sin, x2 * cos + x1 * sin], axis=-1)


def _sliding_window_gqa(q, k, v, sinks):
    B, S = q.shape[:2]
    nb = S // SWA_BLOCK
    G = SWA_Q_HEADS // SWA_KV_HEADS
    qb = q.reshape(B, nb, SWA_BLOCK, SWA_KV_HEADS, G, SWA_HEAD_DIM)
    kb = k.reshape(B, nb, SWA_BLOCK, SWA_KV_HEADS, SWA_HEAD_DIM)
    vb = v.reshape(B, nb, SWA_BLOCK, SWA_KV_HEADS, SWA_HEAD_DIM)

    def with_prev(t):
        prev = jnp.pad(t[:, :-1], ((0, 0), (1, 0), (0, 0), (0, 0), (0, 0)))
        return jnp.concatenate([prev, t], axis=2)

    kw, vw = with_prev(kb), with_prev(vb)
    scores = jnp.einsum('bnqhgd,bnkhd->bnhgqk', qb, kw,
                        preferred_element_type=jnp.float32) * (SWA_HEAD_DIM ** -0.5)
    q_off = jnp.arange(SWA_BLOCK)[:, None] + SWA_BLOCK
    k_off = jnp.arange(2 * SWA_BLOCK)[None, :]
    rel = q_off - k_off
    band = (rel >= 0) & (rel < SWA_WINDOW)
    not_first = (jnp.arange(nb) > 0)[:, None, None]
    valid = band[None] & (not_first | (k_off >= SWA_BLOCK)[None])
    scores = jnp.where(valid[None, :, None, None], scores, MASK_VALUE)
    sink = sinks.astype(jnp.float32).reshape(SWA_KV_HEADS, G)[None, None, :, :, None, None]
    m = jnp.maximum(scores.max(-1, keepdims=True), sink)
    p = jnp.exp(scores - m)
    p = (p / (p.sum(-1, keepdims=True) + jnp.exp(sink - m))).astype(v.dtype)
    out = jnp.einsum('bnhgqk,bnkhd->bnqhgd', p, vw)
    return out.reshape(B, S, A_Q)


def _mla(c_q, c_kv, k_rope, cos, sin, q_norm_g, kv_norm_g, w_uq, w_ukv):
    B, S = c_q.shape[:2]
    q = (_rms_norm(c_q, q_norm_g) @ w_uq).reshape(B, S, MLA_HEADS, MLA_NOPE_DIM + MLA_ROPE_DIM)
    q_nope = q[..., :MLA_NOPE_DIM]
    q_rope = _rope(q[..., MLA_NOPE_DIM:], cos[:, :, None], sin[:, :, None])
    kv = (_rms_norm(c_kv, kv_norm_g) @ w_ukv).reshape(B, S, MLA_HEADS, MLA_NOPE_DIM + MLA_V_DIM)
    k_nope, v = kv[..., :MLA_NOPE_DIM], kv[..., MLA_NOPE_DIM:]
    k_r = _rope(k_rope, cos, sin)
    nb = S // MLA_BLOCK
    scale = (MLA_NOPE_DIM + MLA_ROPE_DIM) ** -0.5
    qn_b = q_nope.reshape(B, nb, MLA_BLOCK, MLA_HEADS, MLA_NOPE_DIM).transpose(1, 0, 2, 3, 4)
    qr_b = q_rope.reshape(B, nb, MLA_BLOCK, MLA_HEADS, MLA_ROPE_DIM).transpose(1, 0, 2, 3, 4)
    key_idx = jnp.arange(S)

    def block(args):
        qn, qr, i = args
        s = (jnp.einsum('bqhd,bkhd->bhqk', qn, k_nope, preferred_element_type=jnp.float32)
             + jnp.einsum('bqhr,bkr->bhqk', qr, k_r, preferred_element_type=jnp.float32)) * scale
        q_idx = i * MLA_BLOCK + jnp.arange(MLA_BLOCK)
        s = jnp.where((key_idx[None, :] <= q_idx[:, None])[None, None], s, MASK_VALUE)
        p = jax.nn.softmax(s, axis=-1).astype(v.dtype)
        return jnp.einsum('bhqk,bkhd->bqhd', p, v)

    out = lax.map(block, (qn_b, qr_b, jnp.arange(nb)))
    return out.transpose(1, 0, 2, 3, 4).reshape(B, S, B_OUT)


def _chunked_sgu(u, v, ln_g, ln_b, w_s, b_s):
    B, S = u.shape[:2]
    nc = S // SGU_CHUNK
    vn = _layer_norm(v, ln_g, ln_b).reshape(B, nc, SGU_CHUNK, SGU_GROUPS, SGU_GROUP_DIM)
    causal = jnp.tril(jnp.ones((SGU_CHUNK, SGU_CHUNK), dtype=bool))
    w = jnp.where(causal[None], w_s, 0.0)
    mixed = jnp.einsum('gts,bnsgc->bntgc', w, vn) + b_s.T[None, None, :, :, None]
    return u * mixed.reshape(B, S, SGU_WIDTH)


def _fwd_setup_inputs(seed: int = 0) -> dict:
    key = jax.random.key(seed)
    ks = jax.random.split(key, 26)
    L, D = DEPTH, D_MODEL
    f32 = jnp.float32
    nrm = lambda k, shape, s: jax.random.normal(k, shape, f32) * s
    x = jax.random.normal(ks[0], (BATCH, SEQ, D), f32)
    offset = jax.random.randint(ks[1], (BATCH, 1), 0, 1024, dtype=jnp.int32)
    positions = (offset + jnp.arange(SEQ, dtype=jnp.int32)[None, :]).astype(jnp.int32)
    return {
        "x": x,
        "positions": positions,
        "w_in": nrm(ks[2], (L, D, N_IN), D ** -0.5),
        "b_gate": nrm(ks[3], (L, N_BRANCHES, D), 0.1),
        "sinks": nrm(ks[4], (L, SWA_Q_HEADS), 0.5),
        "q_norm_g": 1.0 + nrm(ks[5], (L, MLA_Q_RANK), 0.02),
        "kv_norm_g": 1.0 + nrm(ks[6], (L, MLA_KV_RANK), 0.02),
        "w_uq": nrm(ks[7], (L, MLA_Q_RANK, MLA_HEADS * (MLA_NOPE_DIM + MLA_ROPE_DIM)), MLA_Q_RANK ** -0.5),
        "w_ukv": nrm(ks[8], (L, MLA_KV_RANK, MLA_HEADS * (MLA_NOPE_DIM + MLA_V_DIM)), MLA_KV_RANK ** -0.5),
        "sgu_ln_g": 1.0 + nrm(ks[9], (L, SGU_WIDTH), 0.02),
        "sgu_ln_b": nrm(ks[10], (L, SGU_WIDTH), 0.02),
        "sgu_w": nrm(ks[11], (L, SGU_GROUPS, SGU_CHUNK, SGU_CHUNK), SGU_CHUNK ** -0.5),
        "sgu_b": 1.0 + nrm(ks[12], (L, SGU_GROUPS, SGU_CHUNK), 0.02),
        "w_proj_a": nrm(ks[13], (L, A_Q, D), A_Q ** -0.5),
        "w_proj_b": nrm(ks[14], (L, B_OUT, D), B_OUT ** -0.5),
        "w_proj_c": nrm(ks[15], (L, SGU_WIDTH, D), SGU_WIDTH ** -0.5),
        "w_o": nrm(ks[16], (L, D, D), DN_BETA * D ** -0.5),
        "ln1_g": 1.0 + nrm(ks[17], (L, D), 0.02),
        "ln1_b": nrm(ks[18], (L, D), 0.02),
        "w_up": nrm(ks[19], (L, D, 2 * D_FF), D ** -0.5),
        "conv_w": nrm(ks[20], (L, CONV_WIDTH, 2 * D_FF), CONV_WIDTH ** -0.5),
        "conv_b": nrm(ks[21], (L, 2 * D_FF), 0.02),
        "w_down": nrm(ks[22], (L, D_FF, D), DN_BETA * D_FF ** -0.5),
        "ln2_g": 1.0 + nrm(ks[23], (L, D), 0.02),
        "ln2_b": nrm(ks[24], (L, D), 0.02),
    }


def _fwd_reference(x, positions, w_in, b_gate, sinks, q_norm_g, kv_norm_g, w_uq, w_ukv,
              sgu_ln_g, sgu_ln_b, sgu_w, sgu_b, w_proj_a, w_proj_b, w_proj_c, w_o,
              ln1_g, ln1_b, w_up, conv_w, conv_b, w_down, ln2_g, ln2_b):
    B, S, D = x.shape
    inv_freq = ROPE_THETA ** (-jnp.arange(0, MLA_ROPE_DIM, 2, dtype=jnp.float32) / MLA_ROPE_DIM)
    ang = positions.astype(jnp.float32)[..., None] * inv_freq
    cos, sin = jnp.cos(ang).astype(x.dtype), jnp.sin(ang).astype(x.dtype)
    split_at = [A_Q, A_Q + A_KV, A_Q + 2 * A_KV]
    split_at += [split_at[-1] + MLA_Q_RANK]
    split_at += [split_at[-1] + MLA_KV_RANK]
    split_at += [split_at[-1] + MLA_ROPE_DIM]
    split_at += [split_at[-1] + SGU_WIDTH]
    split_at += [split_at[-1] + SGU_WIDTH]

    for l in range(DEPTH):
        h = x @ w_in[l]
        qa, ka, va, c_q, c_kv, k_rope, hu, hv, g_logit = jnp.split(h, split_at, axis=-1)
        y_a = _sliding_window_gqa(qa, ka, va, sinks[l])
        y_b = _mla(c_q, c_kv, k_rope, cos, sin, q_norm_g[l], kv_norm_g[l], w_uq[l], w_ukv[l])
        y_c = _chunked_sgu(jax.nn.gelu(hu, approximate=False), jax.nn.gelu(hv, approximate=False),
                           sgu_ln_g[l], sgu_ln_b[l], sgu_w[l], sgu_b[l])
        gates = jax.nn.sigmoid((g_logit.reshape(B, S, N_BRANCHES, D) + b_gate[l]).astype(jnp.float32)).astype(x.dtype)
        merged = (gates[:, :, 0] * (y_a @ w_proj_a[l])
                  + gates[:, :, 1] * (y_b @ w_proj_b[l])
                  + gates[:, :, 2] * (y_c @ w_proj_c[l]))
        x = _layer_norm(DN_ALPHA * x + merged @ w_o[l], ln1_g[l], ln1_b[l])

        up = x @ w_up[l]
        up_pad = jnp.pad(up, ((0, 0), (CONV_WIDTH - 1, 0), (0, 0)))
        conv = conv_b[l] + sum(up_pad[:, j:j + S] * conv_w[l, j] for j in range(CONV_WIDTH))
        gate, val = jnp.split(conv, 2, axis=-1)
        x = _layer_norm(DN_ALPHA * x + (jax.nn.silu(gate) * val) @ w_down[l], ln2_g[l], ln2_b[l])
    return x


import jax as _jax
import jax.numpy as _jnp

TWIN_FORMAT = 'train_step'
FWD_PARAMS = ['x', 'positions', 'w_in', 'b_gate', 'sinks', 'q_norm_g', 'kv_norm_g', 'w_uq', 'w_ukv', 'sgu_ln_g', 'sgu_ln_b', 'sgu_w', 'sgu_b', 'w_proj_a', 'w_proj_b', 'w_proj_c', 'w_o', 'ln1_g', 'ln1_b', 'w_up', 'conv_w', 'conv_b', 'w_down', 'ln2_g', 'ln2_b']
TWIN_WEIGHTS = ['w_in', 'b_gate', 'sinks', 'q_norm_g', 'kv_norm_g', 'w_uq', 'w_ukv', 'sgu_ln_g', 'sgu_ln_b', 'sgu_w', 'sgu_b', 'w_proj_a', 'w_proj_b', 'w_proj_c', 'w_o', 'ln1_g', 'ln1_b', 'w_up', 'conv_w', 'conv_b', 'w_down', 'ln2_g', 'ln2_b']
TWIN_DIFF_INPUT = 'x'
TWIN_INPUTS = ['x', 'positions', 'w_in', 'b_gate', 'sinks', 'q_norm_g', 'kv_norm_g', 'w_uq', 'w_ukv', 'sgu_ln_g', 'sgu_ln_b', 'sgu_w', 'sgu_b', 'w_proj_a', 'w_proj_b', 'w_proj_c', 'w_o', 'ln1_g', 'ln1_b', 'w_up', 'conv_w', 'conv_b', 'w_down', 'ln2_g', 'ln2_b', 'loss_target', 'm_w_in', 'm_b_gate', 'm_sinks', 'm_q_norm_g', 'm_kv_norm_g', 'm_w_uq', 'm_w_ukv', 'm_sgu_ln_g', 'm_sgu_ln_b', 'm_sgu_w', 'm_sgu_b', 'm_w_proj_a', 'm_w_proj_b', 'm_w_proj_c', 'm_w_o', 'm_ln1_g', 'm_ln1_b', 'm_w_up', 'm_conv_w', 'm_conv_b', 'm_w_down', 'm_ln2_g', 'm_ln2_b', 'v_w_in', 'v_b_gate', 'v_sinks', 'v_q_norm_g', 'v_kv_norm_g', 'v_w_uq', 'v_w_ukv', 'v_sgu_ln_g', 'v_sgu_ln_b', 'v_sgu_w', 'v_sgu_b', 'v_w_proj_a', 'v_w_proj_b', 'v_w_proj_c', 'v_w_o', 'v_ln1_g', 'v_ln1_b', 'v_w_up', 'v_conv_w', 'v_conv_b', 'v_w_down', 'v_ln2_g', 'v_ln2_b']
TWIN_OUTPUTS = ['loss', 'grad_x', 'grad_w_in', 'grad_b_gate', 'grad_sinks', 'grad_q_norm_g', 'grad_kv_norm_g', 'grad_w_uq', 'grad_w_ukv', 'grad_sgu_ln_g', 'grad_sgu_ln_b', 'grad_sgu_w', 'grad_sgu_b', 'grad_w_proj_a', 'grad_w_proj_b', 'grad_w_proj_c', 'grad_w_o', 'grad_ln1_g', 'grad_ln1_b', 'grad_w_up', 'grad_conv_w', 'grad_conv_b', 'grad_w_down', 'grad_ln2_g', 'grad_ln2_b', 'delta_w_in', 'delta_b_gate', 'delta_sinks', 'delta_q_norm_g', 'delta_kv_norm_g', 'delta_w_uq', 'delta_w_ukv', 'delta_sgu_ln_g', 'delta_sgu_ln_b', 'delta_sgu_w', 'delta_sgu_b', 'delta_w_proj_a', 'delta_w_proj_b', 'delta_w_proj_c', 'delta_w_o', 'delta_ln1_g', 'delta_ln1_b', 'delta_w_up', 'delta_conv_w', 'delta_conv_b', 'delta_w_down', 'delta_ln2_g', 'delta_ln2_b', 'new_m_w_in', 'new_m_b_gate', 'new_m_sinks', 'new_m_q_norm_g', 'new_m_kv_norm_g', 'new_m_w_uq', 'new_m_w_ukv', 'new_m_sgu_ln_g', 'new_m_sgu_ln_b', 'new_m_sgu_w', 'new_m_sgu_b', 'new_m_w_proj_a', 'new_m_w_proj_b', 'new_m_w_proj_c', 'new_m_w_o', 'new_m_ln1_g', 'new_m_ln1_b', 'new_m_w_up', 'new_m_conv_w', 'new_m_conv_b', 'new_m_w_down', 'new_m_ln2_g', 'new_m_ln2_b', 'new_v_w_in', 'new_v_b_gate', 'new_v_sinks', 'new_v_q_norm_g', 'new_v_kv_norm_g', 'new_v_w_uq', 'new_v_w_ukv', 'new_v_sgu_ln_g', 'new_v_sgu_ln_b', 'new_v_sgu_w', 'new_v_sgu_b', 'new_v_w_proj_a', 'new_v_w_proj_b', 'new_v_w_proj_c', 'new_v_w_o', 'new_v_ln1_g', 'new_v_ln1_b', 'new_v_w_up', 'new_v_conv_w', 'new_v_conv_b', 'new_v_w_down', 'new_v_ln2_g', 'new_v_ln2_b']
TWIN_LEAF_KINDS = {'loss': 'loss', 'grad_x': 'grad_x', 'grad_w_in': 'grad_w', 'grad_b_gate': 'grad_w', 'grad_sinks': 'grad_w', 'grad_q_norm_g': 'grad_w', 'grad_kv_norm_g': 'grad_w', 'grad_w_uq': 'grad_w', 'grad_w_ukv': 'grad_w', 'grad_sgu_ln_g': 'grad_w', 'grad_sgu_ln_b': 'grad_w', 'grad_sgu_w': 'grad_w', 'grad_sgu_b': 'grad_w', 'grad_w_proj_a': 'grad_w', 'grad_w_proj_b': 'grad_w', 'grad_w_proj_c': 'grad_w', 'grad_w_o': 'grad_w', 'grad_ln1_g': 'grad_w', 'grad_ln1_b': 'grad_w', 'grad_w_up': 'grad_w', 'grad_conv_w': 'grad_w', 'grad_conv_b': 'grad_w', 'grad_w_down': 'grad_w', 'grad_ln2_g': 'grad_w', 'grad_ln2_b': 'grad_w', 'delta_w_in': 'delta_w', 'delta_b_gate': 'delta_w', 'delta_sinks': 'delta_w', 'delta_q_norm_g': 'delta_w', 'delta_kv_norm_g': 'delta_w', 'delta_w_uq': 'delta_w', 'delta_w_ukv': 'delta_w', 'delta_sgu_ln_g': 'delta_w', 'delta_sgu_ln_b': 'delta_w', 'delta_sgu_w': 'delta_w', 'delta_sgu_b': 'delta_w', 'delta_w_proj_a': 'delta_w', 'delta_w_proj_b': 'delta_w', 'delta_w_proj_c': 'delta_w', 'delta_w_o': 'delta_w', 'delta_ln1_g': 'delta_w', 'delta_ln1_b': 'delta_w', 'delta_w_up': 'delta_w', 'delta_conv_w': 'delta_w', 'delta_conv_b': 'delta_w', 'delta_w_down': 'delta_w', 'delta_ln2_g': 'delta_w', 'delta_ln2_b': 'delta_w', 'new_m_w_in': 'new_m', 'new_m_b_gate': 'new_m', 'new_m_sinks': 'new_m', 'new_m_q_norm_g': 'new_m', 'new_m_kv_norm_g': 'new_m', 'new_m_w_uq': 'new_m', 'new_m_w_ukv': 'new_m', 'new_m_sgu_ln_g': 'new_m', 'new_m_sgu_ln_b': 'new_m', 'new_m_sgu_w': 'new_m', 'new_m_sgu_b': 'new_m', 'new_m_w_proj_a': 'new_m', 'new_m_w_proj_b': 'new_m', 'new_m_w_proj_c': 'new_m', 'new_m_w_o': 'new_m', 'new_m_ln1_g': 'new_m', 'new_m_ln1_b': 'new_m', 'new_m_w_up': 'new_m', 'new_m_conv_w': 'new_m', 'new_m_conv_b': 'new_m', 'new_m_w_down': 'new_m', 'new_m_ln2_g': 'new_m', 'new_m_ln2_b': 'new_m', 'new_v_w_in': 'new_v', 'new_v_b_gate': 'new_v', 'new_v_sinks': 'new_v', 'new_v_q_norm_g': 'new_v', 'new_v_kv_norm_g': 'new_v', 'new_v_w_uq': 'new_v', 'new_v_w_ukv': 'new_v', 'new_v_sgu_ln_g': 'new_v', 'new_v_sgu_ln_b': 'new_v', 'new_v_sgu_w': 'new_v', 'new_v_sgu_b': 'new_v', 'new_v_w_proj_a': 'new_v', 'new_v_w_proj_b': 'new_v', 'new_v_w_proj_c': 'new_v', 'new_v_w_o': 'new_v', 'new_v_ln1_g': 'new_v', 'new_v_ln1_b': 'new_v', 'new_v_w_up': 'new_v', 'new_v_conv_w': 'new_v', 'new_v_conv_b': 'new_v', 'new_v_w_down': 'new_v', 'new_v_ln2_g': 'new_v', 'new_v_ln2_b': 'new_v'}


def _forward(args):
    return _fwd_reference(*[args[k] for k in FWD_PARAMS])


def _output_shape():
    def fwd():
        inp = _fwd_setup_inputs(0)
        return _fwd_reference(*[inp[k] for k in FWD_PARAMS])
    out = _jax.eval_shape(fwd)
    return out.shape, out.dtype

N_MICROBATCH = 1
ADAM_LR = 0.001
ADAM_B1 = 0.9
ADAM_B2 = 0.999
ADAM_EPS = 1e-08
ADAM_WD = 0.01
ADAM_STEP = 10
PER_EXAMPLE_BATCH_AXIS = {'x': 0, 'positions': 0, 'loss_target': 0}
SHARED_INPUTS = []
_WEIGHT_DTYPES = {'w_in': _jnp.float32, 'b_gate': _jnp.float32, 'sinks': _jnp.float32, 'q_norm_g': _jnp.float32, 'kv_norm_g': _jnp.float32, 'w_uq': _jnp.float32, 'w_ukv': _jnp.float32, 'sgu_ln_g': _jnp.float32, 'sgu_ln_b': _jnp.float32, 'sgu_w': _jnp.float32, 'sgu_b': _jnp.float32, 'w_proj_a': _jnp.float32, 'w_proj_b': _jnp.float32, 'w_proj_c': _jnp.float32, 'w_o': _jnp.float32, 'ln1_g': _jnp.float32, 'ln1_b': _jnp.float32, 'w_up': _jnp.float32, 'conv_w': _jnp.float32, 'conv_b': _jnp.float32, 'w_down': _jnp.float32, 'ln2_g': _jnp.float32, 'ln2_b': _jnp.float32}
MOMENT_SCALE = {'w_in': 8.735020e-03, 'b_gate': 4.157005e-03, 'sinks': 4.712302e-03, 'q_norm_g': 6.155214e-03, 'kv_norm_g': 9.245017e-03, 'w_uq': 2.528920e-03, 'w_ukv': 3.320432e-03, 'sgu_ln_g': 1.155210e-02, 'sgu_ln_b': 1.198670e-02, 'sgu_w': 1.140251e-02, 'sgu_b': 1.605459e-02, 'w_proj_a': 3.729987e-03, 'w_proj_b': 3.944137e-03, 'w_proj_c': 1.775765e-02, 'w_o': 3.519926e-02, 'ln1_g': 5.676171e-01, 'ln1_b': 2.787325e-01, 'w_up': 1.178181e-02, 'conv_w': 1.190622e-02, 'conv_b': 1.377563e-02, 'w_down': 3.851216e-02, 'ln2_g': 1.133938e+01, 'ln2_b': 8.619498e-01}


def _to_microbatches(a, axis):
    t = _jnp.moveaxis(a, axis, 0)
    t = t.reshape((N_MICROBATCH, t.shape[0] // N_MICROBATCH) + t.shape[1:])
    return _jnp.moveaxis(t, 1, axis + 1)


def setup_inputs(seed: int = 0) -> dict:
    inp = _fwd_setup_inputs(seed)
    key = _jax.random.fold_in(_jax.random.key(seed), 7919)
    shape, _ = _output_shape()
    out = dict(inp)
    out["loss_target"] = _jax.random.normal(_jax.random.fold_in(key, 0), shape, _jnp.float32)
    for i, name in enumerate(TWIN_WEIGHTS):
        w = inp[name].astype(_jnp.float32)
        if MOMENT_SCALE is None:
            s = _jnp.sqrt(_jnp.mean(_jnp.square(w)) + 1e-30)
        else:
            s = MOMENT_SCALE[name]
        km, kv = _jax.random.split(_jax.random.fold_in(key, i + 1))
        out[name] = w
        out["m_" + name] = s * _jax.random.normal(km, w.shape, _jnp.float32)
        out["v_" + name] = (s * s) * _jax.random.uniform(kv, w.shape, _jnp.float32, 0.5, 1.5)
    if N_MICROBATCH > 1:
        for name, axis in PER_EXAMPLE_BATCH_AXIS.items():
            out[name] = _to_microbatches(out[name], axis)
    return {'x': out['x'], 'positions': out['positions'], 'w_in': out['w_in'], 'b_gate': out['b_gate'], 'sinks': out['sinks'], 'q_norm_g': out['q_norm_g'], 'kv_norm_g': out['kv_norm_g'], 'w_uq': out['w_uq'], 'w_ukv': out['w_ukv'], 'sgu_ln_g': out['sgu_ln_g'], 'sgu_ln_b': out['sgu_ln_b'], 'sgu_w': out['sgu_w'], 'sgu_b': out['sgu_b'], 'w_proj_a': out['w_proj_a'], 'w_proj_b': out['w_proj_b'], 'w_proj_c': out['w_proj_c'], 'w_o': out['w_o'], 'ln1_g': out['ln1_g'], 'ln1_b': out['ln1_b'], 'w_up': out['w_up'], 'conv_w': out['conv_w'], 'conv_b': out['conv_b'], 'w_down': out['w_down'], 'ln2_g': out['ln2_g'], 'ln2_b': out['ln2_b'], 'loss_target': out['loss_target'], 'm_w_in': out['m_w_in'], 'm_b_gate': out['m_b_gate'], 'm_sinks': out['m_sinks'], 'm_q_norm_g': out['m_q_norm_g'], 'm_kv_norm_g': out['m_kv_norm_g'], 'm_w_uq': out['m_w_uq'], 'm_w_ukv': out['m_w_ukv'], 'm_sgu_ln_g': out['m_sgu_ln_g'], 'm_sgu_ln_b': out['m_sgu_ln_b'], 'm_sgu_w': out['m_sgu_w'], 'm_sgu_b': out['m_sgu_b'], 'm_w_proj_a': out['m_w_proj_a'], 'm_w_proj_b': out['m_w_proj_b'], 'm_w_proj_c': out['m_w_proj_c'], 'm_w_o': out['m_w_o'], 'm_ln1_g': out['m_ln1_g'], 'm_ln1_b': out['m_ln1_b'], 'm_w_up': out['m_w_up'], 'm_conv_w': out['m_conv_w'], 'm_conv_b': out['m_conv_b'], 'm_w_down': out['m_w_down'], 'm_ln2_g': out['m_ln2_g'], 'm_ln2_b': out['m_ln2_b'], 'v_w_in': out['v_w_in'], 'v_b_gate': out['v_b_gate'], 'v_sinks': out['v_sinks'], 'v_q_norm_g': out['v_q_norm_g'], 'v_kv_norm_g': out['v_kv_norm_g'], 'v_w_uq': out['v_w_uq'], 'v_w_ukv': out['v_w_ukv'], 'v_sgu_ln_g': out['v_sgu_ln_g'], 'v_sgu_ln_b': out['v_sgu_ln_b'], 'v_sgu_w': out['v_sgu_w'], 'v_sgu_b': out['v_sgu_b'], 'v_w_proj_a': out['v_w_proj_a'], 'v_w_proj_b': out['v_w_proj_b'], 'v_w_proj_c': out['v_w_proj_c'], 'v_w_o': out['v_w_o'], 'v_ln1_g': out['v_ln1_g'], 'v_ln1_b': out['v_ln1_b'], 'v_w_up': out['v_w_up'], 'v_conv_w': out['v_conv_w'], 'v_conv_b': out['v_conv_b'], 'v_w_down': out['v_w_down'], 'v_ln2_g': out['v_ln2_g'], 'v_ln2_b': out['v_ln2_b']}


def _loss(weights, diff, rest, loss_target):
    with _jax.named_scope("forward"):
        args = {**rest, TWIN_DIFF_INPUT: diff, **{k: w.astype(_WEIGHT_DTYPES[k]) for k, w in weights.items()}}
        y = _forward(args)
    with _jax.named_scope("loss_head"):
        err = _jnp.square(y.astype(_jnp.float32) - loss_target)
        return 0.5 * _jnp.sum(_jnp.mean(err, axis=-1)) if err.ndim else 0.5 * err


def _adamw(w, g, m, v):
    m = ADAM_B1 * m + (1.0 - ADAM_B1) * g
    v = ADAM_B2 * v + (1.0 - ADAM_B2) * _jnp.square(g)
    m_hat = m / (1.0 - ADAM_B1 ** ADAM_STEP)
    v_hat = v / (1.0 - ADAM_B2 ** ADAM_STEP)
    delta = -ADAM_LR * (m_hat / (_jnp.sqrt(v_hat) + ADAM_EPS) + ADAM_WD * w)
    return delta, m, v


def reference(x, positions, w_in, b_gate, sinks, q_norm_g, kv_norm_g, w_uq, w_ukv, sgu_ln_g, sgu_ln_b, sgu_w, sgu_b, w_proj_a, w_proj_b, w_proj_c, w_o, ln1_g, ln1_b, w_up, conv_w, conv_b, w_down, ln2_g, ln2_b, loss_target, m_w_in, m_b_gate, m_sinks, m_q_norm_g, m_kv_norm_g, m_w_uq, m_w_ukv, m_sgu_ln_g, m_sgu_ln_b, m_sgu_w, m_sgu_b, m_w_proj_a, m_w_proj_b, m_w_proj_c, m_w_o, m_ln1_g, m_ln1_b, m_w_up, m_conv_w, m_conv_b, m_w_down, m_ln2_g, m_ln2_b, v_w_in, v_b_gate, v_sinks, v_q_norm_g, v_kv_norm_g, v_w_uq, v_w_ukv, v_sgu_ln_g, v_sgu_ln_b, v_sgu_w, v_sgu_b, v_w_proj_a, v_w_proj_b, v_w_proj_c, v_w_o, v_ln1_g, v_ln1_b, v_w_up, v_conv_w, v_conv_b, v_w_down, v_ln2_g, v_ln2_b):
    given = dict(x=x, positions=positions, w_in=w_in, b_gate=b_gate, sinks=sinks, q_norm_g=q_norm_g, kv_norm_g=kv_norm_g, w_uq=w_uq, w_ukv=w_ukv, sgu_ln_g=sgu_ln_g, sgu_ln_b=sgu_ln_b, sgu_w=sgu_w, sgu_b=sgu_b, w_proj_a=w_proj_a, w_proj_b=w_proj_b, w_proj_c=w_proj_c, w_o=w_o, ln1_g=ln1_g, ln1_b=ln1_b, w_up=w_up, conv_w=conv_w, conv_b=conv_b, w_down=w_down, ln2_g=ln2_g, ln2_b=ln2_b, loss_target=loss_target, m_w_in=m_w_in, m_b_gate=m_b_gate, m_sinks=m_sinks, m_q_norm_g=m_q_norm_g, m_kv_norm_g=m_kv_norm_g, m_w_uq=m_w_uq, m_w_ukv=m_w_ukv, m_sgu_ln_g=m_sgu_ln_g, m_sgu_ln_b=m_sgu_ln_b, m_sgu_w=m_sgu_w, m_sgu_b=m_sgu_b, m_w_proj_a=m_w_proj_a, m_w_proj_b=m_w_proj_b, m_w_proj_c=m_w_proj_c, m_w_o=m_w_o, m_ln1_g=m_ln1_g, m_ln1_b=m_ln1_b, m_w_up=m_w_up, m_conv_w=m_conv_w, m_conv_b=m_conv_b, m_w_down=m_w_down, m_ln2_g=m_ln2_g, m_ln2_b=m_ln2_b, v_w_in=v_w_in, v_b_gate=v_b_gate, v_sinks=v_sinks, v_q_norm_g=v_q_norm_g, v_kv_norm_g=v_kv_norm_g, v_w_uq=v_w_uq, v_w_ukv=v_w_ukv, v_sgu_ln_g=v_sgu_ln_g, v_sgu_ln_b=v_sgu_ln_b, v_sgu_w=v_sgu_w, v_sgu_b=v_sgu_b, v_w_proj_a=v_w_proj_a, v_w_proj_b=v_w_proj_b, v_w_proj_c=v_w_proj_c, v_w_o=v_w_o, v_ln1_g=v_ln1_g, v_ln1_b=v_ln1_b, v_w_up=v_w_up, v_conv_w=v_conv_w, v_conv_b=v_conv_b, v_w_down=v_w_down, v_ln2_g=v_ln2_g, v_ln2_b=v_ln2_b)
    weights = {n: given[n] for n in TWIN_WEIGHTS}
    shared = {n: given[n] for n in SHARED_INPUTS}
    per_example = {n: given[n] for n in ['x', 'positions']}
    grad_fn = _jax.value_and_grad(_loss, argnums=(0, 1))

    def one_microbatch(ex, loss_target):
        ex = dict(ex)
        diff = ex.pop(TWIN_DIFF_INPUT)
        return grad_fn(weights, diff, {**shared, **ex}, loss_target)

    if N_MICROBATCH == 1:
        loss, (grad_w, grad_x) = one_microbatch(per_example, given["loss_target"])
    else:
        def body(carry, xs):
            loss_sum, grad_sum = carry
            l_k, (gw_k, gx_k) = one_microbatch(xs[0], xs[1])
            with _jax.named_scope("update"):
                return (loss_sum + l_k, _jax.tree.map(_jnp.add, grad_sum, gw_k)), gx_k

        init = (_jnp.zeros((), _jnp.float32), _jax.tree.map(_jnp.zeros_like, weights))
        (loss, grad_w), grad_x = _jax.lax.scan(body, init, (per_example, given["loss_target"]))
    with _jax.named_scope("update"):
        delta_w, new_m, new_v = {}, {}, {}
        for n in TWIN_WEIGHTS:
            delta_w[n], new_m[n], new_v[n] = _adamw(weights[n], grad_w[n], given["m_" + n], given["v_" + n])
    return (loss, grad_x, *[grad_w[n] for n in TWIN_WEIGHTS], *[delta_w[n] for n in TWIN_WEIGHTS],
            *[new_m[n] for n in TWIN_WEIGHTS], *[new_v[n] for n in TWIN_WEIGHTS])
```

```python
import math

import jax
import jax.numpy as jnp
import numpy as np
from jax import lax
from jax.experimental import pallas as pl
from jax.experimental.pallas import tpu as pltpu

F32, BF16 = jnp.float32, jnp.bfloat16
SDS = jax.ShapeDtypeStruct
MESH = pl.DeviceIdType.MESH

SWA_Q_HEADS, SWA_KV_HEADS, SWA_HEAD_DIM, SWA_BLOCK = 16, 2, 64, 128
MLA_HEADS, MLA_NOPE, MLA_ROPE, MLA_V = 16, 128, 64, 128
MLA_Q_RANK, MLA_KV_RANK = 512, 512
ROPE_THETA = 10000.0
SGU_GROUPS, SGU_GROUP_DIM, SGU_CHUNK = 8, 128, 128
SGU_WIDTH = SGU_GROUPS * SGU_GROUP_DIM
A_Q = SWA_Q_HEADS * SWA_HEAD_DIM
A_KV = SWA_KV_HEADS * SWA_HEAD_DIM
N_BRANCHES = 3
DEPTH = 2
EPS = 1e-5
MASK_VALUE = -1e30
DN_ALPHA = (2 * DEPTH) ** 0.25
ADAM_LR, ADAM_B1, ADAM_B2, ADAM_EPS, ADAM_WD, ADAM_STEP = 0.001, 0.9, 0.999, 1e-08, 0.01, 10
N_CHIPS = 4

LANES = 128
VMEM_LIMIT = 48 * 1024 * 1024

BIG = ["w_in", "w_uq", "w_ukv", "w_proj_a", "w_proj_b", "w_proj_c", "w_o", "w_up", "w_down"]
ROW_SHARDED = {"w_proj_b", "w_o", "w_down"}
SMALL = ["b_gate", "sinks", "q_norm_g", "kv_norm_g", "sgu_ln_g", "sgu_ln_b", "sgu_w", "sgu_b", "ln1_g", "ln1_b",
         "conv_w", "conv_b", "ln2_g", "ln2_b"]
SMALL_SHARDED = {"b_gate", "conv_w"}
WEIGHTS = ["w_in", "b_gate", "sinks", "q_norm_g", "kv_norm_g", "w_uq", "w_ukv", "sgu_ln_g", "sgu_ln_b", "sgu_w", "sgu_b",
           "w_proj_a", "w_proj_b", "w_proj_c", "w_o", "ln1_g", "ln1_b", "w_up", "conv_w", "conv_b", "w_down", "ln2_g", "ln2_b"]


def _pcall(body, **kw):
    return pl.pallas_call(body, **kw)


def _params(sem=None):
    return pltpu.CompilerParams(dimension_semantics=sem, vmem_limit_bytes=VMEM_LIMIT)


def _tile(dim, pref, align=LANES):
    t = (min(pref, dim) // align) * align
    while t >= align:
        if dim % t == 0:
            return t
        t -= align
    return dim


def _mm(a, b, mode, out_dtype, name, add=None, tm=1024, tn=512, tk=2048):
    if mode == "nn":
        (M, K), (K2, N) = a.shape, b.shape
    elif mode == "nt":
        (M, K), (N, K2) = a.shape, b.shape
    else:
        (K, M), (K2, N) = a.shape, b.shape
    assert K == K2, (a.shape, b.shape, mode)
    tm, tn, tk = _tile(M, tm), _tile(N, tn), _tile(K, tk)
    nk = K // tk
    if mode == "tn":
        a_spec = pl.BlockSpec((tk, tm), lambda i, j, k: (k, i))
    else:
        a_spec = pl.BlockSpec((tm, tk), lambda i, j, k: (i, k))
    if mode == "nt":
        b_spec = pl.BlockSpec((tn, tk), lambda i, j, k: (j, k))
    else:
        b_spec = pl.BlockSpec((tk, tn), lambda i, j, k: (k, j))
    dn = {"nn": (((1,), (0,)), ((), ())), "nt": (((1,), (1,)), ((), ())), "tn": (((0,), (0,)), ((), ()))}[mode]
    o_spec = pl.BlockSpec((tm, tn), lambda i, j, k: (i, j))
    has_add = add is not None

    def body(*refs):
        if has_add:
            a_ref, b_ref, add_ref, o_ref, acc_ref = refs
        else:
            a_ref, b_ref, o_ref, acc_ref = refs
        k = pl.program_id(2)

        @pl.when(k == 0)
        def _():
            acc_ref[...] = jnp.zeros_like(acc_ref)

        acc_ref[...] += lax.dot_general(a_ref[...].astype(BF16), b_ref[...].astype(BF16), dn,
                                        preferred_element_type=F32)

        @pl.when(k == nk - 1)
        def _():
            r = acc_ref[...]
            if has_add:
                r = r + add_ref[...].astype(F32)
            o_ref[...] = r.astype(o_ref.dtype)

    ins = [a, b] + ([add] if has_add else [])
    in_specs = [a_spec, b_spec] + ([o_spec] if has_add else [])
    return _pcall(body, name=name, out_shape=SDS((M, N), out_dtype), grid=(M // tm, N // tn, nk),
                  in_specs=in_specs, out_specs=o_spec, scratch_shapes=[pltpu.VMEM((tm, tn), F32)],
                  compiler_params=_params(("parallel", "parallel", "arbitrary")))(*ins)


def _rowwise(fn, rows, params, row_outs, acc_outs, tm, name):
    n_rows = rows[0][0].shape[0]
    assert n_rows % tm == 0
    nr, npar, no = len(rows), len(params), len(row_outs)

    def body(*refs):
        i = pl.program_id(0)
        r, p = refs[:nr], refs[nr:nr + npar]
        o, acc = refs[nr + npar:nr + npar + no], refs[nr + npar + no:]
        outs, sums = fn(i, [x[...] for x in r], [x[...] for x in p])
        for ref, val in zip(o, outs, strict=True):
            ref[...] = val.astype(ref.dtype)
        if acc:
            @pl.when(i == 0)
            def _():
                for ref in acc:
                    ref[...] = jnp.zeros_like(ref)
            for ref, val in zip(acc, sums, strict=True):
                ref[...] += val.astype(F32)

    def full(shape):
        nd = len(shape)
        return pl.BlockSpec(tuple(shape), lambda i: (0,) * nd)

    in_specs = [pl.BlockSpec((tm, w), (lambda i, cb=cb: (i, cb))) for (_, w, cb) in rows] + [full(p.shape) for p in params]
    out_specs = [pl.BlockSpec((tm, w), lambda i: (i, 0)) for (w, _) in row_outs] + [full(s) for s in acc_outs]
    out_shape = [SDS((n_rows, w), dt) for (w, dt) in row_outs] + [SDS(tuple(s), F32) for s in acc_outs]
    res = _pcall(body, name=name, out_shape=out_shape, grid=(n_rows // tm,), in_specs=in_specs, out_specs=out_specs,
                 compiler_params=_params(("arbitrary",)))(*[r[0] for r in rows], *params)
    return list(res[:no]), list(res[no:])


def _whole(a):
    return (a, a.shape[1], 0)


def _gelu(x):
    return 0.5 * x * (1.0 + lax.erf(x * (1.0 / math.sqrt(2.0))))


def _layer_norm(x, g, b):
    mu = x.mean(-1, keepdims=True)
    var = jnp.mean(jnp.square(x - mu), -1, keepdims=True)
    return (x - mu) * lax.rsqrt(var + EPS) * g + b


def _rms_norm(x, g):
    return x * lax.rsqrt(jnp.mean(jnp.square(x), -1, keepdims=True) + EPS) * g


def _sgu_math(hu, hv, ln_g, ln_b, ws, bs):
    u = _gelu(hu.astype(F32))
    vn = _layer_norm(_gelu(hv.astype(F32)), ln_g, ln_b)
    r = lax.broadcasted_iota(jnp.int32, (SGU_CHUNK, SGU_CHUNK), 0)
    c = lax.broadcasted_iota(jnp.int32, (SGU_CHUNK, SGU_CHUNK), 1)
    outs = []
    for g in range(SGU_GROUPS):
        w = jnp.where(r >= c, ws[g], 0.0).astype(BF16)
        vg = vn[:, g * SGU_GROUP_DIM:(g + 1) * SGU_GROUP_DIM].astype(BF16)
        outs.append(jnp.dot(w, vg, preferred_element_type=F32) + bs[g])
    return u * jnp.concatenate(outs, axis=1)


def _sgu_fwd(h, cu, cv, ln_g, ln_b, w, b3):
    def fn(i, rows, ps):
        g_, b_, w_, b3_ = ps
        y = _sgu_math(rows[0], rows[1], g_, b_, [w_[g] for g in range(SGU_GROUPS)], [b3_[g] for g in range(SGU_GROUPS)])
        return [y], []
    (y,), _ = _rowwise(fn, [(h, SGU_WIDTH, cu), (h, SGU_WIDTH, cv)], [ln_g, ln_b, w, b3], [(SGU_WIDTH, BF16)], [],
                       SGU_CHUNK, "sgu_fwd")
    return y


def _sgu_bwd(h, cu, cv, dy, ln_g, ln_b, w, b3):
    nd = 2 * SGU_WIDTH

    def body(hu_ref, hv_ref, dy_ref, g_ref, b_ref, w_ref, b3_ref, dh_ref, dg_ref, db_ref, dw_ref, db3_ref):
        i = pl.program_id(0)

        @pl.when(i == 0)
        def _():
            dg_ref[...] = jnp.zeros_like(dg_ref)
            db_ref[...] = jnp.zeros_like(db_ref)
            dw_ref[...] = jnp.zeros_like(dw_ref)
            db3_ref[...] = jnp.zeros_like(db3_ref)

        ws = [w_ref[g] for g in range(SGU_GROUPS)]
        bs = [b3_ref[g] for g in range(SGU_GROUPS)]
        _, vjp = jax.vjp(_sgu_math, hu_ref[...], hv_ref[...], g_ref[...], b_ref[...], ws, bs)
        dhu, dhv, dg, db, dws, dbs = vjp(dy_ref[...].astype(F32))
        dh_ref[...] = jnp.concatenate([dhu, dhv], axis=1).astype(dh_ref.dtype)
        dg_ref[...] += dg
        db_ref[...] += db
        for g in range(SGU_GROUPS):
            dw_ref[g] += dws[g]
            db3_ref[g] += dbs[g]

    n = h.shape[0]
    blk = lambda cb: pl.BlockSpec((SGU_CHUNK, SGU_WIDTH), lambda i, cb=cb: (i, cb))
    full = lambda s: pl.BlockSpec(tuple(s), lambda i: (0,) * len(s))
    return _pcall(
        body, name="sgu_bwd", grid=(n // SGU_CHUNK,),
        out_shape=[SDS((n, nd), BF16), SDS(ln_g.shape, F32), SDS(ln_b.shape, F32), SDS(w.shape, F32), SDS(b3.shape, F32)],
        in_specs=[blk(cu), blk(cv), blk(0), full(ln_g.shape), full(ln_b.shape), full(w.shape), full(b3.shape)],
        out_specs=[pl.BlockSpec((SGU_CHUNK, nd), lambda i: (i, 0)), full(ln_g.shape), full(ln_b.shape), full(w.shape),
                   full(b3.shape)],
        compiler_params=_params(("arbitrary",)))(h, h, dy, ln_g, ln_b, w, b3)


def _swa_math(q, kp, kc, vp, vc, sinks, not_first):
    kw = jnp.concatenate([kp, kc], axis=0).astype(BF16)
    vw = jnp.concatenate([vp, vc], axis=0).astype(BF16)
    qb = q.astype(BF16)
    q_off = lax.broadcasted_iota(jnp.int32, (SWA_BLOCK, 2 * SWA_BLOCK), 0) + SWA_BLOCK
    k_off = lax.broadcasted_iota(jnp.int32, (SWA_BLOCK, 2 * SWA_BLOCK), 1)
    rel = q_off - k_off
    valid = (rel >= 0) & (rel < SWA_BLOCK) & (not_first | (k_off >= SWA_BLOCK))
    G = SWA_Q_HEADS // SWA_KV_HEADS
    outs = []
    for head in range(SWA_Q_HEADS):
        hk = head // G
        qh = qb[:, head * SWA_HEAD_DIM:(head + 1) * SWA_HEAD_DIM]
        kh = kw[:, hk * SWA_HEAD_DIM:(hk + 1) * SWA_HEAD_DIM]
        vh = vw[:, hk * SWA_HEAD_DIM:(hk + 1) * SWA_HEAD_DIM]
        s = lax.dot_general(qh, kh, (((1,), (1,)), ((), ())), preferred_element_type=F32) * (SWA_HEAD_DIM ** -0.5)
        s = jnp.where(valid, s, MASK_VALUE)
        sink = sinks[:, head:head + 1]
        m = jnp.maximum(s.max(-1, keepdims=True), sink)
        p = jnp.exp(s - m)
        p = (p / (p.sum(-1, keepdims=True) + jnp.exp(sink - m))).astype(BF16)
        outs.append(jnp.dot(p, vh, preferred_element_type=F32))
    return jnp.concatenate(outs, axis=1)


def _swa_fwd(h, cq, ck, cv, sinks):
    n = h.shape[0]
    nb = n // SWA_BLOCK

    def body(q_ref, kp_ref, kc_ref, vp_ref, vc_ref, s_ref, o_ref):
        i = pl.program_id(0)
        f = lambda x: x[...].astype(F32)
        o_ref[...] = _swa_math(f(q_ref), f(kp_ref), f(kc_ref), f(vp_ref), f(vc_ref), s_ref[...], i > 0).astype(o_ref.dtype)

    prev = lambda cb: pl.BlockSpec((SWA_BLOCK, A_KV), lambda i, cb=cb: (jnp.maximum(i - 1, 0), cb))
    cur = lambda cb: pl.BlockSpec((SWA_BLOCK, A_KV), lambda i, cb=cb: (i, cb))
    return _pcall(body, name="swa_fwd", grid=(nb,), out_shape=SDS((n, A_Q), BF16),
                  in_specs=[pl.BlockSpec((SWA_BLOCK, A_Q), lambda i: (i, cq)), prev(ck), cur(ck), prev(cv), cur(cv),
                            pl.BlockSpec((1, SWA_Q_HEADS), lambda i: (0, 0))],
                  out_specs=pl.BlockSpec((SWA_BLOCK, A_Q), lambda i: (i, 0)),
                  compiler_params=_params(("arbitrary",)))(h, h, h, h, h, sinks)


def _swa_bwd(h, cq, ck, cv, sinks, dy):
    n = h.shape[0]
    nb = n // SWA_BLOCK

    def body(q_ref, kp_ref, kc_ref, vp_ref, vc_ref, s_ref, dy_ref, dq_ref, dk_ref, dv_ref, ds_ref, ck_ref, cv_ref):
        r = pl.program_id(0)
        blk = nb - 1 - r

        @pl.when(r == 0)
        def _():
            ds_ref[...] = jnp.zeros_like(ds_ref)
            ck_ref[...] = jnp.zeros_like(ck_ref)
            cv_ref[...] = jnp.zeros_like(cv_ref)

        f = lambda x: x[...].astype(F32)
        not_first = blk > 0
        _, vjp = jax.vjp(lambda q, kp, kc, vp, vc, s: _swa_math(q, kp, kc, vp, vc, s, not_first),
                         f(q_ref), f(kp_ref), f(kc_ref), f(vp_ref), f(vc_ref), s_ref[...])
        dq, dkp, dkc, dvp, dvc, dsk = vjp(f(dy_ref))
        dq_ref[...] = dq.astype(dq_ref.dtype)
        dk_ref[...] = (dkc + ck_ref[...]).astype(dk_ref.dtype)
        dv_ref[...] = (dvc + cv_ref[...]).astype(dv_ref.dtype)
        ck_ref[...] = dkp
        cv_ref[...] = dvp
        ds_ref[...] += dsk

    rev = lambda i: nb - 1 - i
    prev = lambda cb: pl.BlockSpec((SWA_BLOCK, A_KV), lambda i, cb=cb: (jnp.maximum(rev(i) - 1, 0), cb))
    cur = lambda cb: pl.BlockSpec((SWA_BLOCK, A_KV), lambda i, cb=cb: (rev(i), cb))
    return _pcall(
        body, name="swa_bwd", grid=(nb,),
        out_shape=[SDS((n, A_Q), BF16), SDS((n, A_KV), BF16), SDS((n, A_KV), BF16), SDS((1, SWA_Q_HEADS), F32)],
        in_specs=[pl.BlockSpec((SWA_BLOCK, A_Q), lambda i: (rev(i), cq)), prev(ck), cur(ck), prev(cv), cur(cv),
                  pl.BlockSpec((1, SWA_Q_HEADS), lambda i: (0, 0)), pl.BlockSpec((SWA_BLOCK, A_Q), lambda i: (rev(i), 0))],
        out_specs=[pl.BlockSpec((SWA_BLOCK, A_Q), lambda i: (rev(i), 0)), pl.BlockSpec((SWA_BLOCK, A_KV), lambda i: (rev(i), 0)),
                   pl.BlockSpec((SWA_BLOCK, A_KV), lambda i: (rev(i), 0)), pl.BlockSpec((1, SWA_Q_HEADS), lambda i: (0, 0))],
        scratch_shapes=[pltpu.VMEM((SWA_BLOCK, A_KV), F32), pltpu.VMEM((SWA_BLOCK, A_KV), F32)],
        compiler_params=_params(("arbitrary",)))(h, h, h, h, h, sinks, dy)


def _rope(x, cos, sin, sign):
    w = x.shape[1]
    reps = w // LANES
    ct = jnp.tile(cos, (1, reps)) if reps > 1 else cos
    st = jnp.tile(sin, (1, reps)) if reps > 1 else sin
    fwd = pltpu.roll(x, MLA_ROPE // 2, axis=1)
    bwd = pltpu.roll(x, w - MLA_ROPE // 2, axis=1)
    lane = lax.broadcasted_iota(jnp.int32, x.shape, 1) % LANES
    rot = jnp.where(lane < MLA_ROPE // 2, -bwd, fwd)
    return x * ct + sign * (rot * st)


def _rope_call(a, wa, ca, b, wb, cb, cos, sin, sign, tm, name):
    def fn(i, rows, ps):
        xa, xb, c_, s_ = rows
        return [_rope(xa.astype(F32), c_, s_, sign), _rope(xb.astype(F32), c_, s_, sign)], []
    (ra, rb), _ = _rowwise(fn, [(a, wa, ca), (b, wb, cb), _whole(cos), _whole(sin)], [], [(wa, BF16), (wb, BF16)], [], tm, name)
    return ra, rb


def _mla_scores(qn_ref, qr_ref, kn_ref, kr_ref, i, j, T):
    q = jnp.concatenate([qn_ref[...], qr_ref[...]], axis=1)
    k = jnp.concatenate([kn_ref[...], kr_ref[...]], axis=1)
    s = lax.dot_general(q, k, (((1,), (1,)), ((), ())), preferred_element_type=F32)
    s = s * ((MLA_NOPE + MLA_ROPE) ** -0.5)
    row = i * T + lax.broadcasted_iota(jnp.int32, (T, T), 0)
    col = j * T + lax.broadcasted_iota(jnp.int32, (T, T), 1)
    return jnp.where(col <= row, s, MASK_VALUE), q, k


def _mla_fwd(q_full, qr, kv, kr, T):
    n = q_full.shape[0]
    nq = n // T
    H = MLA_HEADS

    def body(qn_ref, qr_ref, kn_ref, v_ref, kr_ref, y_ref, lse_ref, m_ref, l_ref, acc_ref):
        i, j = pl.program_id(1), pl.program_id(2)

        @pl.when(j == 0)
        def _():
            m_ref[...] = jnp.full_like(m_ref, MASK_VALUE)
            l_ref[...] = jnp.zeros_like(l_ref)
            acc_ref[...] = jnp.zeros_like(acc_ref)

        @pl.when(j <= i)
        def _():
            s, _, _ = _mla_scores(qn_ref, qr_ref, kn_ref, kr_ref, i, j, T)
            m_prev = m_ref[...]
            m_new = jnp.maximum(m_prev, s.max(-1, keepdims=True))
            p = jnp.exp(s - m_new[:, :1])
            alpha = jnp.exp(m_prev - m_new)
            l_ref[...] = alpha * l_ref[...] + p.sum(-1, keepdims=True)
            acc_ref[...] = alpha * acc_ref[...] + jnp.dot(p.astype(BF16), v_ref[...], preferred_element_type=F32)
            m_ref[...] = m_new

        @pl.when(j == i)
        def _():
            y_ref[...] = (acc_ref[...] / l_ref[...]).astype(y_ref.dtype)
            lse_ref[0] = m_ref[...] + jnp.log(l_ref[...])

    qspec = lambda off: pl.BlockSpec((T, LANES), lambda h, i, j, off=off: (i, off + h))
    kspec = lambda off: pl.BlockSpec((T, LANES), lambda h, i, j, off=off: (jnp.minimum(j, i), off + h))
    return _pcall(
        body, name="mla_fwd", grid=(H, nq, nq),
        out_shape=[SDS((n, H * MLA_V), BF16), SDS((H, n, LANES), F32)],
        in_specs=[qspec(0), qspec(0), kspec(0), kspec(H), pl.BlockSpec((T, LANES), lambda h, i, j: (jnp.minimum(j, i), 0))],
        out_specs=[pl.BlockSpec((T, LANES), lambda h, i, j: (i, h)), pl.BlockSpec((1, T, LANES), lambda h, i, j: (h, i, 0))],
        scratch_shapes=[pltpu.VMEM((T, LANES), F32)] * 3,
        compiler_params=_params(("parallel", "parallel", "arbitrary")))(q_full, qr, kv, kv, kr)


def _mla_delta(dy, y, T):
    n = y.shape[0]
    H = MLA_HEADS

    def body(dy_ref, y_ref, d_ref):
        d = jnp.sum(dy_ref[...].astype(F32) * y_ref[...].astype(F32), axis=-1, keepdims=True)
        d_ref[0] = jnp.broadcast_to(d, (T, LANES))

    spec = pl.BlockSpec((T, LANES), lambda h, i: (i, h))
    return _pcall(body, name="mla_delta", grid=(H, n // T), out_shape=SDS((H, n, LANES), F32), in_specs=[spec, spec],
                  out_specs=pl.BlockSpec((1, T, LANES), lambda h, i: (h, i, 0)),
                  compiler_params=_params(("parallel", "parallel")))(dy, y)


def _mla_bwd_dq(q_full, qr, kv, kr, dy, lse, delta, T):
    n = q_full.shape[0]
    nq = n // T
    H = MLA_HEADS

    def body(qn_ref, qr_ref, kn_ref, v_ref, kr_ref, dy_ref, lse_ref, dl_ref, dqn_ref, dqr_ref, acc_ref):
        i, j = pl.program_id(1), pl.program_id(2)

        @pl.when(j == 0)
        def _():
            acc_ref[...] = jnp.zeros_like(acc_ref)

        @pl.when(j <= i)
        def _():
            s, _, k = _mla_scores(qn_ref, qr_ref, kn_ref, kr_ref, i, j, T)
            p = jnp.exp(s - lse_ref[0][:, :1])
            dp = lax.dot_general(dy_ref[...], v_ref[...], (((1,), (1,)), ((), ())), preferred_element_type=F32)
            ds = p * (dp - dl_ref[0][:, :1]) * ((MLA_NOPE + MLA_ROPE) ** -0.5)
            acc_ref[...] += jnp.dot(ds.astype(BF16), k, preferred_element_type=F32)

        @pl.when(j == i)
        def _():
            dqn_ref[...] = acc_ref[:, :LANES].astype(dqn_ref.dtype)
            dqr_ref[...] = acc_ref[:, LANES:].astype(dqr_ref.dtype)

    qspec = lambda off: pl.BlockSpec((T, LANES), lambda h, i, j, off=off: (i, off + h))
    kspec = lambda off: pl.BlockSpec((T, LANES), lambda h, i, j, off=off: (jnp.minimum(j, i), off + h))
    stat = pl.BlockSpec((1, T, LANES), lambda h, i, j: (h, i, 0))
    out = pl.BlockSpec((T, LANES), lambda h, i, j: (i, h))
    return _pcall(
        body, name="mla_bwd_dq", grid=(H, nq, nq),
        out_shape=[SDS((n, H * LANES), BF16), SDS((n, H * LANES), BF16)],
        in_specs=[qspec(0), qspec(0), kspec(0), kspec(H), pl.BlockSpec((T, LANES), lambda h, i, j: (jnp.minimum(j, i), 0)),
                  qspec(0), stat, stat],
        out_specs=[out, out], scratch_shapes=[pltpu.VMEM((T, 2 * LANES), F32)],
        compiler_params=_params(("parallel", "parallel", "arbitrary")))(q_full, qr, kv, kv, kr, dy, lse, delta)


def _mla_bwd_dkv(q_full, qr, kv, kr, dy, lse, delta, T):
    n = q_full.shape[0]
    nq = n // T
    H = MLA_HEADS

    def body(qn_ref, qr_ref, kn_ref, v_ref, kr_ref, dy_ref, lse_ref, dl_ref, dkn_ref, dv_ref, dkr_ref, dk_acc, dv_acc, dkr_acc):
        j, h, i = pl.program_id(0), pl.program_id(1), pl.program_id(2)

        @pl.when(i == 0)
        def _():
            dk_acc[...] = jnp.zeros_like(dk_acc)
            dv_acc[...] = jnp.zeros_like(dv_acc)

        @pl.when((i == 0) & (h == 0))
        def _():
            dkr_acc[...] = jnp.zeros_like(dkr_acc)

        @pl.when(i >= j)
        def _():
            s, q, _ = _mla_scores(qn_ref, qr_ref, kn_ref, kr_ref, i, j, T)
            p = jnp.exp(s - lse_ref[0][:, :1])
            dy = dy_ref[...]
            dv_acc[...] += lax.dot_general(p.astype(BF16), dy, (((0,), (0,)), ((), ())), preferred_element_type=F32)
            dp = lax.dot_general(dy, v_ref[...], (((1,), (1,)), ((), ())), preferred_element_type=F32)
            ds = p * (dp - dl_ref[0][:, :1]) * ((MLA_NOPE + MLA_ROPE) ** -0.5)
            dk_acc[...] += lax.dot_general(ds.astype(BF16), q, (((0,), (0,)), ((), ())), preferred_element_type=F32)

        @pl.when(i == nq - 1)
        def _():
            dkn_ref[...] = dk_acc[:, :LANES].astype(dkn_ref.dtype)
            dv_ref[...] = dv_acc[...].astype(dv_ref.dtype)
            dkr_acc[...] += dk_acc[:, LANES:]

        @pl.when((i == nq - 1) & (h == H - 1))
        def _():
            dkr_ref[...] = dkr_acc[...].astype(dkr_ref.dtype)

    qspec = lambda off: pl.BlockSpec((T, LANES), lambda j, h, i, off=off: (jnp.maximum(i, j), off + h))
    kspec = lambda off: pl.BlockSpec((T, LANES), lambda j, h, i, off=off: (j, off + h))
    stat = pl.BlockSpec((1, T, LANES), lambda j, h, i: (h, jnp.maximum(i, j), 0))
    out = pl.BlockSpec((T, LANES), lambda j, h, i: (j, h))
    return _pcall(
        body, name="mla_bwd_dkv", grid=(nq, H, nq),
        out_shape=[SDS((n, H * LANES), BF16), SDS((n, H * LANES), BF16), SDS((n, LANES), F32)],
        in_specs=[qspec(0), qspec(0), kspec(0), kspec(H), pl.BlockSpec((T, LANES), lambda j, h, i: (j, 0)), qspec(0), stat, stat],
        out_specs=[out, out, pl.BlockSpec((T, LANES), lambda j, h, i: (j, 0))],
        scratch_shapes=[pltpu.VMEM((T, 2 * LANES), F32), pltpu.VMEM((T, LANES), F32), pltpu.VMEM((T, LANES), F32)],
        compiler_params=_params(("arbitrary", "arbitrary", "arbitrary")))(q_full, qr, kv, kv, kr, dy, lse, delta)


def _shift_down(x, k):
    row = lax.broadcasted_iota(jnp.int32, x.shape, 0)
    return jnp.where(row >= k, pltpu.roll(x, k, axis=0), 0.0)


def _shift_up(x, k):
    n = x.shape[0]
    row = lax.broadcasted_iota(jnp.int32, x.shape, 0)
    return jnp.where(row < n - k, pltpu.roll(x, n - k, axis=0), 0.0)


def _conv_fwd(up, w, b):
    n, c = up.shape

    def body(u_ref, w_ref, b_ref, o_ref):
        u = u_ref[...].astype(F32)
        wv = w_ref[...]
        o_ref[...] = (b_ref[...] + wv[0:1] * _shift_down(u, 2) + wv[1:2] * _shift_down(u, 1) + wv[2:3] * u).astype(o_ref.dtype)

    return _pcall(body, name="conv_fwd", grid=(c // LANES,), out_shape=SDS((n, c), BF16),
                  in_specs=[pl.BlockSpec((n, LANES), lambda j: (0, j)), pl.BlockSpec((3, LANES), lambda j: (0, j)),
                            pl.BlockSpec((1, LANES), lambda j: (0, j))],
                  out_specs=pl.BlockSpec((n, LANES), lambda j: (0, j)), compiler_params=_params(("parallel",)))(up, w, b)


def _conv_bwd(up, dc, w):
    n, c = up.shape

    def body(u_ref, d_ref, w_ref, du_ref, dw_ref, db_ref):
        u = u_ref[...].astype(F32)
        d = d_ref[...].astype(F32)
        wv = w_ref[...]
        du_ref[...] = (wv[2:3] * d + wv[1:2] * _shift_up(d, 1) + wv[0:1] * _shift_up(d, 2)).astype(du_ref.dtype)
        dw_ref[0:1, :] = jnp.sum(d * _shift_down(u, 2), axis=0, keepdims=True)
        dw_ref[1:2, :] = jnp.sum(d * _shift_down(u, 1), axis=0, keepdims=True)
        dw_ref[2:3, :] = jnp.sum(d * u, axis=0, keepdims=True)
        db_ref[...] = jnp.sum(d, axis=0, keepdims=True)

    col = pl.BlockSpec((n, LANES), lambda j: (0, j))
    return _pcall(body, name="conv_bwd", grid=(c // LANES,),
                  out_shape=[SDS((n, c), BF16), SDS((3, c), F32), SDS((1, c), F32)],
                  in_specs=[col, col, pl.BlockSpec((3, LANES), lambda j: (0, j))],
                  out_specs=[col, pl.BlockSpec((3, LANES), lambda j: (0, j)), pl.BlockSpec((1, LANES), lambda j: (0, j))],
                  compiler_params=_params(("parallel",)))(up, dc, w)


def _adamw_math(w, g, m, v):
    m = ADAM_B1 * m + (1.0 - ADAM_B1) * g
    v = ADAM_B2 * v + (1.0 - ADAM_B2) * jnp.square(g)
    m_hat = m / (1.0 - ADAM_B1 ** ADAM_STEP)
    v_hat = v / (1.0 - ADAM_B2 ** ADAM_STEP)
    delta = -ADAM_LR * (m_hat / (jnp.sqrt(v_hat) + ADAM_EPS) + ADAM_WD * w)
    return delta, m, v


def _adamw(w, g, m, v, name):
    r, c = w.shape
    tr = r
    budget = max(8, (1 << 20) // (4 * c))
    t = (min(budget, r) // 8) * 8
    while t >= 8:
        if r % t == 0:
            tr = t
            break
        t -= 8

    def fn(i, rows, ps):
        return list(_adamw_math(*rows)), []
    (d, nm, nv), _ = _rowwise(fn, [_whole(w), _whole(g), _whole(m), _whole(v)], [], [(c, F32)] * 3, [], tr, name)
    return d, nm, nv


def _coords():
    return lax.axis_index("x"), lax.axis_index("y"), lax.axis_index("c")


def _other_chips(x, y):
    return [(1 - x, y), (x, 1 - y), (1 - x, 1 - y)]


HBM_SPEC = pl.BlockSpec(memory_space=pltpu.HBM)


def _all_gather_chips(own):
    _, R, C = own.shape

    def body(own_ref, out_ref, send_sems, recv_sems, local_sem):
        x, y, c = _coords()
        me = 2 * x + y
        sib = (x, y, 1 - c)
        chips = _other_chips(x, y)
        local = pltpu.make_async_copy(own_ref, out_ref.at[me], local_sem)
        local.start()

        def copy(k, chip_idx, half, to):
            return pltpu.make_async_remote_copy(
                src_ref=own_ref.at[half] if chip_idx is None else out_ref.at[chip_idx, half],
                dst_ref=out_ref.at[me if chip_idx is None else chip_idx, half],
                send_sem=send_sems.at[k], recv_sem=recv_sems.at[k], device_id=to, device_id_type=MESH)

        first = [copy(j, None, c, (px, py, c)) for j, (px, py) in enumerate(chips)]
        for cp in first:
            cp.start()
        passed = []
        for j, (px, py) in enumerate(chips):
            copy(j, 2 * px + py, c, (px, py, c)).wait_recv()
            fw = copy(3 + j, 2 * px + py, c, sib)
            fw.start()
            passed.append(fw)
        for j, (px, py) in enumerate(chips):
            copy(3 + j, 2 * px + py, 1 - c, sib).wait_recv()
        for cp in first + passed:
            cp.wait_send()
        local.wait()

    return _pcall(body, name="all_gather_weights", out_shape=SDS((N_CHIPS, 2, R, C), own.dtype),
                  in_specs=[HBM_SPEC], out_specs=HBM_SPEC,
                  scratch_shapes=[pltpu.SemaphoreType.DMA((6,)), pltpu.SemaphoreType.DMA((6,)), pltpu.SemaphoreType.DMA])(own)


def _sibling_exchange_half(g):
    _, N, R, C = g.shape

    def body(g_ref, out_ref, send_sem, recv_sem):
        x, y, c = _coords()
        cp = pltpu.make_async_remote_copy(src_ref=g_ref.at[1 - c], dst_ref=out_ref, send_sem=send_sem, recv_sem=recv_sem,
                                          device_id=(x, y, 1 - c), device_id_type=MESH)
        cp.start()
        cp.wait()

    return _pcall(body, name="rs_sibling_halves", out_shape=SDS((N, R, C), g.dtype), in_specs=[HBM_SPEC], out_specs=HBM_SPEC,
                  scratch_shapes=[pltpu.SemaphoreType.DMA, pltpu.SemaphoreType.DMA])(g)


def _scatter_to_chips(p):
    _, R, C = p.shape

    def body(p_ref, out_ref, send_sems, recv_sems):
        x, y, c = _coords()
        chips = _other_chips(x, y)
        cps = [pltpu.make_async_remote_copy(src_ref=p_ref.at[2 * px + py], dst_ref=out_ref.at[j], send_sem=send_sems.at[j],
                                            recv_sem=recv_sems.at[j], device_id=(px, py, c), device_id_type=MESH)
               for j, (px, py) in enumerate(chips)]
        for cp in cps:
            cp.start()
        for cp in cps:
            cp.wait()

    return _pcall(body, name="rs_scatter_chips", out_shape=SDS((3, R, C), p.dtype), in_specs=[HBM_SPEC], out_specs=HBM_SPEC,
                  scratch_shapes=[pltpu.SemaphoreType.DMA((3,)), pltpu.SemaphoreType.DMA((3,))])(p)


def _sibling_join(r):
    R, C = r.shape

    def body(r_ref, out_ref, send_sem, recv_sem, local_sem):
        x, y, c = _coords()
        local = pltpu.make_async_copy(r_ref, out_ref.at[c], local_sem)
        local.start()
        cp = pltpu.make_async_remote_copy(src_ref=r_ref, dst_ref=out_ref.at[c], send_sem=send_sem, recv_sem=recv_sem,
                                          device_id=(x, y, 1 - c), device_id_type=MESH)
        cp.start()
        cp.wait_send()
        pltpu.make_async_remote_copy(src_ref=r_ref, dst_ref=out_ref.at[1 - c], send_sem=send_sem, recv_sem=recv_sem,
                                     device_id=(x, y, 1 - c), device_id_type=MESH).wait_recv()
        local.wait()

    return _pcall(body, name="rs_sibling_join", out_shape=SDS((2, R, C), r.dtype), in_specs=[HBM_SPEC], out_specs=HBM_SPEC,
                  scratch_shapes=[pltpu.SemaphoreType.DMA, pltpu.SemaphoreType.DMA, pltpu.SemaphoreType.DMA])(r)


def _all_reduce_small(v):
    n = v.shape[0]

    def body(v_ref, out_ref, slots, send_sems, recv_sems):
        x, y, c = _coords()
        me = 4 * x + 2 * y + c
        cps = []
        for r in range(1, 8):
            t = (me + r) % 8
            cp = pltpu.make_async_remote_copy(src_ref=v_ref, dst_ref=slots.at[me], send_sem=send_sems.at[r - 1],
                                              recv_sem=recv_sems.at[me], device_id=(t // 4, (t // 2) % 2, t % 2),
                                              device_id_type=MESH)
            cp.start()
            cps.append(cp)
        slots[me] = v_ref[...]
        for r in range(1, 8):
            s = (me + r) % 8
            pltpu.make_async_remote_copy(src_ref=v_ref, dst_ref=slots.at[s], send_sem=send_sems.at[r - 1],
                                         recv_sem=recv_sems.at[s], device_id=(x, y, c), device_id_type=MESH).wait_recv()
        for cp in cps:
            cp.wait_send()
        acc = slots[0]
        for d in range(1, 8):
            acc = acc + slots[d]
        out_ref[...] = acc

    return _pcall(body, name="all_reduce_small", out_shape=SDS((n, LANES), F32),
                  in_specs=[pl.BlockSpec(memory_space=pltpu.VMEM)], out_specs=pl.BlockSpec(memory_space=pltpu.VMEM),
                  scratch_shapes=[pltpu.VMEM((8, n, LANES), F32), pltpu.SemaphoreType.DMA((7,)), pltpu.SemaphoreType.DMA((8,))],
                  compiler_params=pltpu.CompilerParams(vmem_limit_bytes=VMEM_LIMIT))(v)


def _add_halves(g, recv, c):
    _, n, C = g.shape
    tr = _tile(n, 512, 16)

    def body(c_ref, g_ref, r_ref, o32_ref, o16_ref):
        s = g_ref[0] + r_ref[...]
        o32_ref[...] = s
        o16_ref[...] = s.astype(BF16)

    gs = pltpu.PrefetchScalarGridSpec(
        num_scalar_prefetch=1, grid=(n // tr,),
        in_specs=[pl.BlockSpec((1, tr, C), lambda i, c_ref: (c_ref[0], i, 0)), pl.BlockSpec((tr, C), lambda i, c_ref: (i, 0))],
        out_specs=[pl.BlockSpec((tr, C), lambda i, c_ref: (i, 0)), pl.BlockSpec((tr, C), lambda i, c_ref: (i, 0))])
    return _pcall(body, name="rs_add_halves", grid_spec=gs, out_shape=[SDS((n, C), F32), SDS((n, C), BF16)],
                  compiler_params=_params(("arbitrary",)))(jnp.reshape(c, (1,)).astype(jnp.int32), g, recv)


def _sum_chunks(p32, recv, me):
    _, R, C = p32.shape
    tr = _tile(R, 512, 16)

    def body(me_ref, p_ref, r_ref, o_ref):
        o_ref[...] = ((p_ref[0] + r_ref[0].astype(F32)) + r_ref[1].astype(F32)) + r_ref[2].astype(F32)

    gs = pltpu.PrefetchScalarGridSpec(
        num_scalar_prefetch=1, grid=(R // tr,),
        in_specs=[pl.BlockSpec((1, tr, C), lambda i, m: (m[0], i, 0)), pl.BlockSpec((3, tr, C), lambda i, m: (0, i, 0))],
        out_specs=pl.BlockSpec((tr, C), lambda i, m: (i, 0)))
    return _pcall(body, name="rs_sum_chunks", grid_spec=gs, out_shape=SDS((R, C), F32),
                  compiler_params=_params(("arbitrary",)))(jnp.reshape(me, (1,)).astype(jnp.int32), p32, recv)


def _h_layout(D):
    G = N_BRANCHES * D
    off, o = {}, 0
    for name, w in [("g", G), ("qa", A_Q), ("cq", MLA_Q_RANK), ("ckv", MLA_KV_RANK), ("hu", SGU_WIDTH), ("hv", SGU_WIDTH),
                    ("ka", A_KV), ("va", A_KV), ("kr", LANES)]:
        assert o % w == 0, (name, o, w)
        off[name] = (o, w)
        o += w
    off["total"] = -(-o // 512) * 512
    return off


def _perm_w_in(w, lay):
    s = np.cumsum([0, A_Q, A_KV, A_KV, MLA_Q_RANK, MLA_KV_RANK, MLA_ROPE, SGU_WIDTH, SGU_WIDTH])
    qa, ka, va, cq, ckv, kr, hu, hv = [w[:, s[i]:s[i + 1]] for i in range(8)]
    g = w[:, s[8]:]
    pad = jnp.zeros((w.shape[0], lay["total"] - lay["kr"][0] - MLA_ROPE), w.dtype)
    return jnp.concatenate([g, qa, cq, ckv, hu, hv, ka, va, kr, pad], axis=1)


def _unperm_w_in(wp, lay, D):
    take = lambda n, width=None: wp[:, lay[n][0]:lay[n][0] + (width or lay[n][1])]
    return jnp.concatenate([take("qa"), take("ka"), take("va"), take("cq"), take("ckv"), take("kr", MLA_ROPE), take("hu"),
                            take("hv"), take("g")], axis=1)


def _perm_w_uq(w):
    r = w.shape[0]
    w3 = w.reshape(r, MLA_HEADS, MLA_NOPE + MLA_ROPE)
    nope = w3[:, :, :MLA_NOPE].reshape(r, MLA_HEADS * MLA_NOPE)
    rope = jnp.pad(w3[:, :, MLA_NOPE:], ((0, 0), (0, 0), (0, LANES - MLA_ROPE))).reshape(r, MLA_HEADS * LANES)
    return jnp.concatenate([nope, rope], axis=1)


def _unperm_w_uq(wp):
    r = wp.shape[0]
    nope = wp[:, :MLA_HEADS * MLA_NOPE].reshape(r, MLA_HEADS, MLA_NOPE)
    rope = wp[:, MLA_HEADS * MLA_NOPE:].reshape(r, MLA_HEADS, LANES)[:, :, :MLA_ROPE]
    return jnp.concatenate([nope, rope], axis=2).reshape(r, MLA_HEADS * (MLA_NOPE + MLA_ROPE))


def _perm_w_ukv(w):
    r = w.shape[0]
    w3 = w.reshape(r, MLA_HEADS, MLA_NOPE + MLA_V)
    return jnp.concatenate([w3[:, :, :MLA_NOPE].reshape(r, -1), w3[:, :, MLA_NOPE:].reshape(r, -1)], axis=1)


def _unperm_w_ukv(wp):
    r = wp.shape[0]
    k = wp[:, :MLA_HEADS * MLA_NOPE].reshape(r, MLA_HEADS, MLA_NOPE)
    v = wp[:, MLA_HEADS * MLA_NOPE:].reshape(r, MLA_HEADS, MLA_V)
    return jnp.concatenate([k, v], axis=2).reshape(r, -1)


FLAT_COLS = 1024


def _flat_len(shards):
    n = sum(int(np.prod(shards[k])) for k in BIG)
    unit = 2 * 16 * FLAT_COLS
    return n, -(-n // unit) * unit


def _flatten(parts, total):
    flat = jnp.concatenate([parts[k].reshape(-1) for k in BIG])
    return jnp.pad(flat, (0, total - flat.shape[0]))


def _unflatten(flat, shards):
    out, o = {}, 0
    for k in BIG:
        n = int(np.prod(shards[k]))
        out[k] = flat[o:o + n].reshape(shards[k])
        o += n
    return out


def kernel(x, positions, w_in, b_gate, sinks, q_norm_g, kv_norm_g, w_uq, w_ukv, sgu_ln_g, sgu_ln_b, sgu_w, sgu_b, w_proj_a, w_proj_b, w_proj_c, w_o, ln1_g, ln1_b, w_up, conv_w, conv_b, w_down, ln2_g, ln2_b, loss_target, m_w_in, m_b_gate, m_sinks, m_q_norm_g, m_kv_norm_g, m_w_uq, m_w_ukv, m_sgu_ln_g, m_sgu_ln_b, m_sgu_w, m_sgu_b, m_w_proj_a, m_w_proj_b, m_w_proj_c, m_w_o, m_ln1_g, m_ln1_b, m_w_up, m_conv_w, m_conv_b, m_w_down, m_ln2_g, m_ln2_b, v_w_in, v_b_gate, v_sinks, v_q_norm_g, v_kv_norm_g, v_w_uq, v_w_ukv, v_sgu_ln_g, v_sgu_ln_b, v_sgu_w, v_sgu_b, v_w_proj_a, v_w_proj_b, v_w_proj_c, v_w_o, v_ln1_g, v_ln1_b, v_w_up, v_conv_w, v_conv_b, v_w_down, v_ln2_g, v_ln2_b):
    a = locals()
    W = {k: a[k] for k in WEIGHTS}
    Mo = {k: a["m_" + k] for k in WEIGHTS}
    Vo = {k: a["v_" + k] for k in WEIGHTS}
    S, D = x.shape[1], x.shape[2]
    FF2 = w_up.shape[2] * N_CHIPS
    FF = FF2 // 2
    L = DEPTH
    lay = _h_layout(D)
    NP = lay["total"]
    cx, cy, cc = _coords()
    chip = 2 * cx + cy
    T = _tile(S, 512)
    TM = _tile(S, 256, 16)
    TMW = _tile(S, 64, 16)

    shards = {k: tuple(W[k].shape) for k in BIG}
    n_flat, n_pad = _flat_len(shards)
    half_rows = n_pad // (2 * FLAT_COLS)
    own16 = _flatten({k: W[k].astype(BF16) for k in BIG}, n_pad).reshape(2, half_rows, FLAT_COLS)
    gathered = _all_gather_chips(own16).reshape(N_CHIPS, n_pad)
    full = {}
    for k in BIG:
        off = sum(int(np.prod(shards[j])) for j in BIG[:BIG.index(k)])
        n = int(np.prod(shards[k]))
        seg = gathered[:, off:off + n].reshape((N_CHIPS,) + shards[k])
        seg = jnp.transpose(seg, (1, 0, 2, 3))
        if k in ROW_SHARDED:
            full[k] = seg.reshape(L, N_CHIPS * shards[k][1], shards[k][2])
        else:
            full[k] = jnp.transpose(seg, (0, 2, 1, 3)).reshape(L, shards[k][1], N_CHIPS * shards[k][2])

    small_sharded_full = {k: tuple(W[k].shape[:-1]) + (W[k].shape[-1] * N_CHIPS,) for k in SMALL_SHARDED}
    placed = []
    for k in ("b_gate", "conv_w"):
        z = jnp.zeros(small_sharded_full[k], F32)
        z = lax.dynamic_update_slice_in_dim(z, W[k], chip * W[k].shape[-1], axis=-1)
        placed.append(jnp.where(cc == 0, z, 0.0).reshape(-1))
    pv = jnp.concatenate(placed)
    n_pv = pv.shape[0]
    pv = jnp.pad(pv, (0, -n_pv % (8 * LANES))).reshape(-1, LANES)
    pv = _all_reduce_small(pv).reshape(-1)
    nb_ = int(np.prod(small_sharded_full["b_gate"]))
    b_gate_full = pv[:nb_].reshape(small_sharded_full["b_gate"])
    conv_w_full = pv[nb_:n_pv].reshape(small_sharded_full["conv_w"])

    inv_freq = ROPE_THETA ** (-jnp.arange(0, MLA_ROPE, 2, dtype=F32) / MLA_ROPE)
    ang = positions[0].astype(F32)[:, None] * inv_freq
    cos, sin = jnp.cos(ang), jnp.sin(ang)
    cos_t = jnp.concatenate([cos, cos, jnp.ones((S, LANES - MLA_ROPE), F32)], axis=1)
    sin_t = jnp.concatenate([sin, sin, jnp.zeros((S, LANES - MLA_ROPE), F32)], axis=1)

    row = lambda v: v.reshape(1, -1)
    cb = lambda name: lay[name][0] // lay[name][1]

    xs = x[0]
    saved = []
    for l in range(L):
        p = dict(
            w_in=_perm_w_in(full["w_in"][l], lay), w_uq=_perm_w_uq(full["w_uq"][l]), w_ukv=_perm_w_ukv(full["w_ukv"][l]),
            w_pa=full["w_proj_a"][l], w_pb=full["w_proj_b"][l], w_pc=full["w_proj_c"][l], w_o=full["w_o"][l],
            w_up=full["w_up"][l], w_down=full["w_down"][l],
            sinks=row(sinks[l]), qg=row(q_norm_g[l]), kvg=row(kv_norm_g[l]), sg=row(sgu_ln_g[l]), sb=row(sgu_ln_b[l]),
            sw=sgu_w[l], sb3=sgu_b[l].reshape(SGU_GROUPS, SGU_CHUNK, 1),
            bg=b_gate_full[l], l1g=row(ln1_g[l]), l1b=row(ln1_b[l]), cw=conv_w_full[l], cbias=row(conv_b[l]),
            l2g=row(ln2_g[l]), l2b=row(ln2_b[l]))
        if l == 0:
            def fn_cast(i, rows, ps):
                return [rows[0]], []
            (xb,), _ = _rowwise(fn_cast, [_whole(xs)], [], [(D, BF16)], [], TM, "cast_x")
        h = _mm(xb, p["w_in"], "nn", BF16, "mm_h")
        y_a = _swa_fwd(h, cb("qa"), cb("ka"), cb("va"), p["sinks"])
        def fn_rms(i, rows, ps):
            return [_rms_norm(rows[0].astype(F32), ps[0]), _rms_norm(rows[1].astype(F32), ps[1])], []
        (cqn, ckvn), _ = _rowwise(fn_rms, [(h, MLA_Q_RANK, cb("cq")), (h, MLA_KV_RANK, cb("ckv"))], [p["qg"], p["kvg"]],
                                  [(MLA_Q_RANK, BF16), (MLA_KV_RANK, BF16)], [], TM, "mla_rms")
        q_full = _mm(cqn, p["w_uq"], "nn", BF16, "mm_q")
        kv = _mm(ckvn, p["w_ukv"], "nn", BF16, "mm_kv")
        qr, kr = _rope_call(q_full, MLA_HEADS * LANES, 1, h, LANES, cb("kr"), cos_t, sin_t, 1.0, TM, "rope_fwd")
        y_b, lse = _mla_fwd(q_full, qr, kv, kr, T)
        y_c = _sgu_fwd(h, cb("hu"), cb("hv"), p["sg"], p["sb"], p["sw"], p["sb3"])
        pa = _mm(y_a, p["w_pa"], "nn", F32, "mm_pa")
        pb = _mm(y_b, p["w_pb"], "nn", F32, "mm_pb")
        pc = _mm(y_c, p["w_pc"], "nn", F32, "mm_pc")

        def merge_math(pa_, pb_, pc_, g_, b0, b1, b2):
            out = 0.0
            for br, (pp, bb) in enumerate(zip((pa_, pb_, pc_), (b0, b1, b2))):
                gate = jax.nn.sigmoid(g_[:, br * D:(br + 1) * D].astype(F32) + bb)
                out = out + gate * pp
            return out

        def fn_merge(i, rows, ps):
            bgv = ps[0]
            return [merge_math(rows[0], rows[1], rows[2], rows[3], bgv[0:1], bgv[1:2], bgv[2:3])], []
        (merged,), _ = _rowwise(fn_merge, [_whole(pa), _whole(pb), _whole(pc), (h, N_BRANCHES * D, cb("g"))], [p["bg"]],
                                [(D, BF16)], [], TMW, "merge_fwd")
        o = _mm(merged, p["w_o"], "nn", F32, "mm_o")

        def ln_res_math(x_, o_, g_, b_):
            return _layer_norm(DN_ALPHA * x_ + o_, g_, b_)

        def fn_ln(i, rows, ps):
            y = ln_res_math(rows[0], rows[1], ps[0], ps[1])
            return [y, y], []
        (x1, x1b), _ = _rowwise(fn_ln, [_whole(xs), _whole(o)], [p["l1g"], p["l1b"]], [(D, F32), (D, BF16)], [], TM, "ln1_fwd")
        up = _mm(x1b, p["w_up"], "nn", BF16, "mm_up")
        cv_ = _conv_fwd(up, p["cw"], p["cbias"])

        def glu_math(cg, cvv):
            return jax.nn.silu(cg.astype(F32)) * cvv.astype(F32)

        def fn_glu(i, rows, ps):
            return [glu_math(rows[0], rows[1])], []
        (act,), _ = _rowwise(fn_glu, [(cv_, FF, 0), (cv_, FF, 1)], [], [(FF, BF16)], [], TMW, "glu_fwd")
        dn = _mm(act, p["w_down"], "nn", F32, "mm_down")
        (x2, x2b), _ = _rowwise(fn_ln, [_whole(x1), _whole(dn)], [p["l2g"], p["l2b"]], [(D, F32), (D, BF16)], [], TM, "ln2_fwd")
        saved.append(dict(p=p, x0=xs, x0b=xb, h=h, y_a=y_a, cqn=cqn, ckvn=ckvn, q_full=q_full, kv=kv, qr=qr, kr=kr, y_b=y_b,
                          lse=lse, y_c=y_c, pa=pa, pb=pb, pc=pc, merged=merged, o=o, x1=x1, x1b=x1b, up=up, cv=cv_, act=act,
                          dn=dn))
        xs, xb = x2, x2b

    def fn_loss(i, rows, ps):
        diff = rows[0] - rows[1]
        part = jnp.sum(jnp.mean(jnp.square(diff), axis=-1, keepdims=True), axis=0, keepdims=True)
        return [diff * (1.0 / D)], [jnp.broadcast_to(part, (8, LANES))]
    (dx,), (loss_acc,) = _rowwise(fn_loss, [_whole(xs), _whole(loss_target[0])], [], [(D, F32)], [(8, LANES)], TM, "loss")
    loss = lax.psum(0.5 * loss_acc[0, 0], ("x", "y", "c"))

    gbig = {k: [None] * L for k in BIG}
    gsmall = {k: [None] * L for k in SMALL}
    for l in reversed(range(L)):
        s = saved[l]
        p = s["p"]

        def fn_ln_bwd(i, rows, ps):
            _, vjp = jax.vjp(ln_res_math, rows[0], rows[1], ps[0], ps[1])
            dx_, do_, dg_, db_ = vjp(rows[2])
            return [dx_, do_], [dg_, db_]
        (dx1_res, ddn), (g_l2g, g_l2b) = _rowwise(fn_ln_bwd, [_whole(s["x1"]), _whole(s["dn"]), _whole(dx)], [p["l2g"], p["l2b"]],
                                                  [(D, F32), (D, BF16)], [(1, D), (1, D)], TM, "ln2_bwd")
        gsmall["ln2_g"][l], gsmall["ln2_b"][l] = g_l2g, g_l2b
        gbig["w_down"][l] = _mm(s["act"], ddn, "tn", F32, "mm_dw_down")
        dact = _mm(ddn, p["w_down"], "nt", BF16, "mm_dact")

        def fn_glu_bwd(i, rows, ps):
            _, vjp = jax.vjp(glu_math, rows[0], rows[1])
            dcg, dcv = vjp(rows[2].astype(F32))
            return [jnp.concatenate([dcg, dcv], axis=1)], []
        (dc,), _ = _rowwise(fn_glu_bwd, [(s["cv"], FF, 0), (s["cv"], FF, 1), _whole(dact)], [], [(FF2, BF16)], [], TMW, "glu_bwd")
        dup, g_cw, g_cb = _conv_bwd(s["up"], dc, p["cw"])
        gsmall["conv_w"][l], gsmall["conv_b"][l] = g_cw, g_cb
        gbig["w_up"][l] = _mm(s["x1b"], dup, "tn", F32, "mm_dw_up")
        dx1 = _mm(dup, p["w_up"], "nt", F32, "mm_dx1", add=dx1_res)
        (dx0_res, do_), (g_l1g, g_l1b) = _rowwise(fn_ln_bwd, [_whole(s["x0"]), _whole(s["o"]), _whole(dx1)], [p["l1g"], p["l1b"]],
                                                  [(D, F32), (D, BF16)], [(1, D), (1, D)], TM, "ln1_bwd")
        gsmall["ln1_g"][l], gsmall["ln1_b"][l] = g_l1g, g_l1b
        gbig["w_o"][l] = _mm(s["merged"], do_, "tn", F32, "mm_dw_o")
        dmerged = _mm(do_, p["w_o"], "nt", F32, "mm_dmerged")

        def fn_merge_bwd(i, rows, ps):
            bgv = ps[0]
            _, vjp = jax.vjp(merge_math, rows[0], rows[1], rows[2], rows[3], bgv[0:1], bgv[1:2], bgv[2:3])
            dpa, dpb, dpc, dg_, db0, db1, db2 = vjp(rows[4])
            return [dpa, dpb, dpc, dg_], [db0, db1, db2]
        (dpa, dpb, dpc, dgl), (db0, db1, db2) = _rowwise(
            fn_merge_bwd, [_whole(s["pa"]), _whole(s["pb"]), _whole(s["pc"]), (s["h"], N_BRANCHES * D, cb("g")), _whole(dmerged)],
            [p["bg"]], [(D, BF16), (D, BF16), (D, BF16), (N_BRANCHES * D, BF16)], [(1, D)] * 3, TMW, "merge_bwd")
        gsmall["b_gate"][l] = jnp.concatenate([db0, db1, db2], axis=0)
        gbig["w_proj_a"][l] = _mm(s["y_a"], dpa, "tn", F32, "mm_dw_pa")
        gbig["w_proj_b"][l] = _mm(s["y_b"], dpb, "tn", F32, "mm_dw_pb")
        gbig["w_proj_c"][l] = _mm(s["y_c"], dpc, "tn", F32, "mm_dw_pc")
        dy_a = _mm(dpa, p["w_pa"], "nt", BF16, "mm_dy_a")
        dy_b = _mm(dpb, p["w_pb"], "nt", BF16, "mm_dy_b")
        dy_c = _mm(dpc, p["w_pc"], "nt", BF16, "mm_dy_c")
        dh_c, g_sg, g_sb, g_sw, g_sb3 = _sgu_bwd(s["h"], cb("hu"), cb("hv"), dy_c, p["sg"], p["sb"], p["sw"], p["sb3"])
        gsmall["sgu_ln_g"][l], gsmall["sgu_ln_b"][l], gsmall["sgu_w"][l] = g_sg, g_sb, g_sw
        gsmall["sgu_b"][l] = g_sb3.reshape(SGU_GROUPS, SGU_CHUNK)
        dqa, dka, dva, g_sinks = _swa_bwd(s["h"], cb("qa"), cb("ka"), cb("va"), p["sinks"], dy_a)
        gsmall["sinks"][l] = g_sinks
        delta = _mla_delta(dy_b, s["y_b"], T)
        dqn, dqr = _mla_bwd_dq(s["q_full"], s["qr"], s["kv"], s["kr"], dy_b, s["lse"], delta, T)
        dkn, dv, dkr = _mla_bwd_dkv(s["q_full"], s["qr"], s["kv"], s["kr"], dy_b, s["lse"], delta, T)
        dqr_raw, dkr_raw = _rope_call(dqr, MLA_HEADS * LANES, 0, dkr, LANES, 0, cos_t, sin_t, -1.0, TM, "rope_bwd")
        dq_full = jnp.concatenate([dqn, dqr_raw], axis=1)
        dkv = jnp.concatenate([dkn, dv], axis=1)
        gbig["w_uq"][l] = _unperm_w_uq(_mm(s["cqn"], dq_full, "tn", F32, "mm_dw_uq"))
        gbig["w_ukv"][l] = _unperm_w_ukv(_mm(s["ckvn"], dkv, "tn", F32, "mm_dw_ukv"))
        dcqn = _mm(dq_full, p["w_uq"], "nt", F32, "mm_dcqn")
        dckvn = _mm(dkv, p["w_ukv"], "nt", F32, "mm_dckvn")

        def fn_rms_bwd(i, rows, ps):
            _, vjp1 = jax.vjp(lambda c_, g_: _rms_norm(c_.astype(F32), g_), rows[0], ps[0])
            _, vjp2 = jax.vjp(lambda c_, g_: _rms_norm(c_.astype(F32), g_), rows[1], ps[1])
            d1, dg1 = vjp1(rows[2])
            d2, dg2 = vjp2(rows[3])
            return [d1, d2], [dg1, dg2]
        (dcq, dckv), (g_qg, g_kvg) = _rowwise(
            fn_rms_bwd, [(s["h"], MLA_Q_RANK, cb("cq")), (s["h"], MLA_KV_RANK, cb("ckv")), _whole(dcqn), _whole(dckvn)],
            [p["qg"], p["kvg"]], [(MLA_Q_RANK, BF16), (MLA_KV_RANK, BF16)], [(1, MLA_Q_RANK), (1, MLA_KV_RANK)], TM, "mla_rms_bwd")
        gsmall["q_norm_g"][l], gsmall["kv_norm_g"][l] = g_qg, g_kvg
        tail = jnp.zeros((S, NP - lay["kr"][0] - LANES), BF16)
        dh = jnp.concatenate([dgl, dqa, dcq, dckv, dh_c, dka, dva, dkr_raw, tail], axis=1)
        gbig["w_in"][l] = _unperm_w_in(_mm(s["x0b"], dh, "tn", F32, "mm_dw_in"), lay, D)
        dx = _mm(dh, p["w_in"], "nt", F32, "mm_dx0", add=dx0_res)

    grad_x = dx.reshape(x.shape)

    chunks = []
    for k_ in range(N_CHIPS):
        parts = {}
        for k in BIG:
            g2 = jnp.stack(gbig[k])
            if k in ROW_SHARDED:
                r = shards[k][1]
                parts[k] = g2[:, k_ * r:(k_ + 1) * r, :]
            else:
                c_ = shards[k][2]
                parts[k] = g2[:, :, k_ * c_:(k_ + 1) * c_]
        chunks.append(_flatten(parts, n_pad).reshape(2, half_rows, FLAT_COLS))
    G = jnp.stack(chunks, axis=1)
    from_sibling = _sibling_exchange_half(G)
    p32, p16 = _add_halves(G.reshape(2, N_CHIPS * half_rows, FLAT_COLS), from_sibling.reshape(N_CHIPS * half_rows, FLAT_COLS), cc)
    arrived = _scatter_to_chips(p16.reshape(N_CHIPS, half_rows, FLAT_COLS))
    reduced_half = _sum_chunks(p32.reshape(N_CHIPS, half_rows, FLAT_COLS), arrived, chip)
    reduced = _sibling_join(reduced_half).reshape(-1)
    g_big = _unflatten(reduced, shards)

    small_shapes = {k: (small_sharded_full[k] if k in SMALL_SHARDED else tuple(W[k].shape)) for k in SMALL}
    sv = jnp.concatenate([jnp.stack(gsmall[k]).reshape(-1) for k in SMALL])
    n_sv = sv.shape[0]
    sv = jnp.pad(sv, (0, -n_sv % (8 * LANES))).reshape(-1, LANES)
    sv = _all_reduce_small(sv).reshape(-1)
    g_small, o_ = {}, 0
    for k in SMALL:
        n = int(np.prod(small_shapes[k]))
        g = sv[o_:o_ + n].reshape(small_shapes[k])
        if k in SMALL_SHARDED:
            g = lax.dynamic_slice_in_dim(g, chip * W[k].shape[-1], W[k].shape[-1], axis=-1)
        g_small[k] = g
        o_ += n

    delta, new_m, new_v = {}, {}, {}
    for k in BIG:
        shp = shards[k]
        v2 = lambda t: t.reshape(shp[0] * shp[1], shp[2])
        d_, m_, v_ = _adamw(v2(W[k]), v2(g_big[k]), v2(Mo[k]), v2(Vo[k]), "adamw_" + k)
        delta[k], new_m[k], new_v[k] = d_.reshape(shp), m_.reshape(shp), v_.reshape(shp)
    pack = lambda t: jnp.concatenate([t[k].reshape(-1) for k in SMALL])
    n_small = sum(int(np.prod(W[k].shape)) for k in SMALL)
    pad2 = lambda t: jnp.pad(t, (0, -n_small % (8 * LANES))).reshape(-1, LANES)
    d_, m_, v_ = _adamw(pad2(pack(W)), pad2(pack(g_small)), pad2(pack(Mo)), pad2(pack(Vo)), "adamw_small")
    o_ = 0
    for k in SMALL:
        n = int(np.prod(W[k].shape))
        take = lambda t: t.reshape(-1)[o_:o_ + n].reshape(W[k].shape)
        delta[k], new_m[k], new_v[k] = take(d_), take(m_), take(v_)
        o_ += n

    grads = {**g_big, **g_small}
    return (loss, grad_x, *[grads[k] for k in WEIGHTS], *[delta[k] for k in WEIGHTS], *[new_m[k] for k in WEIGHTS],
            *[new_v[k] for k in WEIGHTS])
```

```python
import math

import jax
import jax.numpy as jnp
import numpy as np
from jax import lax
from jax.experimental import pallas as pl
from jax.experimental.pallas import tpu as pltpu

F32, BF16 = jnp.float32, jnp.bfloat16
SDS = jax.ShapeDtypeStruct
MESH = pl.DeviceIdType.MESH

SWA_Q_HEADS, SWA_KV_HEADS, SWA_HEAD_DIM, SWA_BLOCK = 16, 2, 64, 128
MLA_HEADS, MLA_NOPE, MLA_ROPE, MLA_V = 16, 128, 64, 128
MLA_Q_RANK, MLA_KV_RANK = 512, 512
ROPE_THETA = 10000.0
SGU_GROUPS, SGU_GROUP_DIM, SGU_CHUNK = 8, 128, 128
SGU_WIDTH = SGU_GROUPS * SGU_GROUP_DIM
A_Q = SWA_Q_HEADS * SWA_HEAD_DIM
A_KV = SWA_KV_HEADS * SWA_HEAD_DIM
N_BRANCHES = 3
DEPTH = 2
EPS = 1e-5
MASK_VALUE = -1e30
DN_ALPHA = (2 * DEPTH) ** 0.25
ADAM_LR, ADAM_B1, ADAM_B2, ADAM_EPS, ADAM_WD, ADAM_STEP = 0.001, 0.9, 0.999, 1e-08, 0.01, 10
N_CHIPS = 4

LANES = 128
VMEM_LIMIT = 48 * 1024 * 1024

BIG = ["w_in", "w_uq", "w_ukv", "w_proj_a", "w_proj_b", "w_proj_c", "w_o", "w_up", "w_down"]
ROW_SHARDED = {"w_proj_b", "w_o", "w_down"}
SMALL = ["b_gate", "sinks", "q_norm_g", "kv_norm_g", "sgu_ln_g", "sgu_ln_b", "sgu_w", "sgu_b", "ln1_g", "ln1_b",
         "conv_w", "conv_b", "ln2_g", "ln2_b"]
SMALL_SHARDED = {"b_gate", "conv_w"}
WEIGHTS = ["w_in", "b_gate", "sinks", "q_norm_g", "kv_norm_g", "w_uq", "w_ukv", "sgu_ln_g", "sgu_ln_b", "sgu_w", "sgu_b",
           "w_proj_a", "w_proj_b", "w_proj_c", "w_o", "ln1_g", "ln1_b", "w_up", "conv_w", "conv_b", "w_down", "ln2_g", "ln2_b"]


def _pcall(body, **kw):
    return pl.pallas_call(body, **kw)


def _params(sem=None):
    return pltpu.CompilerParams(dimension_semantics=sem, vmem_limit_bytes=VMEM_LIMIT)


def _tile(dim, pref, align=LANES):
    t = (min(pref, dim) // align) * align
    while t >= align:
        if dim % t == 0:
            return t
        t -= align
    return dim


def _mm(a, b, mode, out_dtype, name, add=None, tm=1024, tn=512, tk=2048, col_chunks=1):
    if mode == "nn":
        (M, K), (K2, N) = a.shape, b.shape
    elif mode == "nt":
        (M, K), (N, K2) = a.shape, b.shape
    else:
        (K, M), (K2, N) = a.shape, b.shape
    assert K == K2, (a.shape, b.shape, mode)
    assert N % col_chunks == 0
    tm, tn, tk = _tile(M, tm), _tile(N // col_chunks, tn), _tile(K, tk)
    assert (N // col_chunks) % tn == 0
    per_chunk = (N // col_chunks) // tn
    nk = K // tk
    if mode == "tn":
        a_spec = pl.BlockSpec((tk, tm), lambda i, j, k: (k, i))
    else:
        a_spec = pl.BlockSpec((tm, tk), lambda i, j, k: (i, k))
    if mode == "nt":
        b_spec = pl.BlockSpec((tn, tk), lambda i, j, k: (j, k))
    else:
        b_spec = pl.BlockSpec((tk, tn), lambda i, j, k: (k, j))
    dn = {"nn": (((1,), (0,)), ((), ())), "nt": (((1,), (1,)), ((), ())), "tn": (((0,), (0,)), ((), ()))}[mode]
    chunked = col_chunks > 1
    if chunked:
        assert add is None
        o_spec = pl.BlockSpec((1, tm, tn), lambda i, j, k: (lax.div(j, per_chunk), i, lax.rem(j, per_chunk)))
        out_shape = SDS((col_chunks, M, N // col_chunks), out_dtype)
    else:
        o_spec = pl.BlockSpec((tm, tn), lambda i, j, k: (i, j))
        out_shape = SDS((M, N), out_dtype)
    has_add = add is not None

    def body(*refs):
        if has_add:
            a_ref, b_ref, add_ref, o_ref, acc_ref = refs
        else:
            a_ref, b_ref, o_ref, acc_ref = refs
        k = pl.program_id(2)

        @pl.when(k == 0)
        def _():
            acc_ref[...] = jnp.zeros_like(acc_ref)

        acc_ref[...] += lax.dot_general(a_ref[...].astype(BF16), b_ref[...].astype(BF16), dn,
                                        preferred_element_type=F32)

        @pl.when(k == nk - 1)
        def _():
            r = acc_ref[...]
            if has_add:
                r = r + add_ref[...].astype(F32)
            if chunked:
                o_ref[0] = r.astype(o_ref.dtype)
            else:
                o_ref[...] = r.astype(o_ref.dtype)

    ins = [a, b] + ([add] if has_add else [])
    in_specs = [a_spec, b_spec] + ([o_spec] if has_add else [])
    return _pcall(body, name=name, out_shape=out_shape, grid=(M // tm, N // tn, nk),
                  in_specs=in_specs, out_specs=o_spec, scratch_shapes=[pltpu.VMEM((tm, tn), F32)],
                  compiler_params=_params(("parallel", "parallel", "arbitrary")))(*ins)


def _rowwise(fn, rows, params, row_outs, acc_outs, tm, name):
    n_rows = rows[0][0].shape[0]
    assert n_rows % tm == 0
    nr, npar, no = len(rows), len(params), len(row_outs)

    def body(*refs):
        i = pl.program_id(0)
        r, p = refs[:nr], refs[nr:nr + npar]
        o, acc = refs[nr + npar:nr + npar + no], refs[nr + npar + no:]
        outs, sums = fn(i, [x[...] for x in r], [x[...] for x in p])
        for ref, val in zip(o, outs, strict=True):
            ref[...] = val.astype(ref.dtype)
        if acc:
            @pl.when(i == 0)
            def _():
                for ref in acc:
                    ref[...] = jnp.zeros_like(ref)
            for ref, val in zip(acc, sums, strict=True):
                ref[...] += val.astype(F32)

    def full(shape):
        nd = len(shape)
        return pl.BlockSpec(tuple(shape), lambda i: (0,) * nd)

    in_specs = [pl.BlockSpec((tm, w), (lambda i, cb=cb: (i, cb))) for (_, w, cb) in rows] + [full(p.shape) for p in params]
    out_specs = [pl.BlockSpec((tm, w), lambda i: (i, 0)) for (w, _) in row_outs] + [full(s) for s in acc_outs]
    out_shape = [SDS((n_rows, w), dt) for (w, dt) in row_outs] + [SDS(tuple(s), F32) for s in acc_outs]
    res = _pcall(body, name=name, out_shape=out_shape, grid=(n_rows // tm,), in_specs=in_specs, out_specs=out_specs,
                 compiler_params=_params(("arbitrary",)))(*[r[0] for r in rows], *params)
    return list(res[:no]), list(res[no:])


def _whole(a):
    return (a, a.shape[1], 0)


def _gelu(x):
    return 0.5 * x * (1.0 + lax.erf(x * (1.0 / math.sqrt(2.0))))


def _layer_norm(x, g, b):
    mu = x.mean(-1, keepdims=True)
    var = jnp.mean(jnp.square(x - mu), -1, keepdims=True)
    return (x - mu) * lax.rsqrt(var + EPS) * g + b


def _rms_norm(x, g):
    return x * lax.rsqrt(jnp.mean(jnp.square(x), -1, keepdims=True) + EPS) * g


def _sgu_math(hu, hv, ln_g, ln_b, ws, bs):
    u = _gelu(hu.astype(F32))
    vn = _layer_norm(_gelu(hv.astype(F32)), ln_g, ln_b)
    r = lax.broadcasted_iota(jnp.int32, (SGU_CHUNK, SGU_CHUNK), 0)
    c = lax.broadcasted_iota(jnp.int32, (SGU_CHUNK, SGU_CHUNK), 1)
    outs = []
    for g in range(SGU_GROUPS):
        w = jnp.where(r >= c, ws[g], 0.0).astype(BF16)
        vg = vn[:, g * SGU_GROUP_DIM:(g + 1) * SGU_GROUP_DIM].astype(BF16)
        outs.append(jnp.dot(w, vg, preferred_element_type=F32) + bs[g])
    return u * jnp.concatenate(outs, axis=1)


def _sgu_fwd(h, cu, cv, ln_g, ln_b, w, b3):
    def fn(i, rows, ps):
        g_, b_, w_, b3_ = ps
        y = _sgu_math(rows[0], rows[1], g_, b_, [w_[g] for g in range(SGU_GROUPS)], [b3_[g] for g in range(SGU_GROUPS)])
        return [y], []
    (y,), _ = _rowwise(fn, [(h, SGU_WIDTH, cu), (h, SGU_WIDTH, cv)], [ln_g, ln_b, w, b3], [(SGU_WIDTH, BF16)], [],
                       SGU_CHUNK, "sgu_fwd")
    return y


def _sgu_bwd(h, cu, cv, dy, ln_g, ln_b, w, b3):
    nd = 2 * SGU_WIDTH

    def body(hu_ref, hv_ref, dy_ref, g_ref, b_ref, w_ref, b3_ref, dh_ref, dg_ref, db_ref, dw_ref, db3_ref):
        i = pl.program_id(0)

        @pl.when(i == 0)
        def _():
            dg_ref[...] = jnp.zeros_like(dg_ref)
            db_ref[...] = jnp.zeros_like(db_ref)
            dw_ref[...] = jnp.zeros_like(dw_ref)
            db3_ref[...] = jnp.zeros_like(db3_ref)

        ws = [w_ref[g] for g in range(SGU_GROUPS)]
        bs = [b3_ref[g] for g in range(SGU_GROUPS)]
        _, vjp = jax.vjp(_sgu_math, hu_ref[...], hv_ref[...], g_ref[...], b_ref[...], ws, bs)
        dhu, dhv, dg, db, dws, dbs = vjp(dy_ref[...].astype(F32))
        dh_ref[...] = jnp.concatenate([dhu, dhv], axis=1).astype(dh_ref.dtype)
        dg_ref[...] += dg
        db_ref[...] += db
        for g in range(SGU_GROUPS):
            dw_ref[g] += dws[g]
            db3_ref[g] += dbs[g]

    n = h.shape[0]
    blk = lambda cb: pl.BlockSpec((SGU_CHUNK, SGU_WIDTH), lambda i, cb=cb: (i, cb))
    full = lambda s: pl.BlockSpec(tuple(s), lambda i: (0,) * len(s))
    return _pcall(
        body, name="sgu_bwd", grid=(n // SGU_CHUNK,),
        out_shape=[SDS((n, nd), BF16), SDS(ln_g.shape, F32), SDS(ln_b.shape, F32), SDS(w.shape, F32), SDS(b3.shape, F32)],
        in_specs=[blk(cu), blk(cv), blk(0), full(ln_g.shape), full(ln_b.shape), full(w.shape), full(b3.shape)],
        out_specs=[pl.BlockSpec((SGU_CHUNK, nd), lambda i: (i, 0)), full(ln_g.shape), full(ln_b.shape), full(w.shape),
                   full(b3.shape)],
        compiler_params=_params(("arbitrary",)))(h, h, dy, ln_g, ln_b, w, b3)


def _swa_math(q, kp, kc, vp, vc, sinks, not_first):
    kw = jnp.concatenate([kp, kc], axis=0).astype(BF16)
    vw = jnp.concatenate([vp, vc], axis=0).astype(BF16)
    qb = q.astype(BF16)
    q_off = lax.broadcasted_iota(jnp.int32, (SWA_BLOCK, 2 * SWA_BLOCK), 0) + SWA_BLOCK
    k_off = lax.broadcasted_iota(jnp.int32, (SWA_BLOCK, 2 * SWA_BLOCK), 1)
    rel = q_off - k_off
    valid = (rel >= 0) & (rel < SWA_BLOCK) & (not_first | (k_off >= SWA_BLOCK))
    G = SWA_Q_HEADS // SWA_KV_HEADS
    outs = []
    for head in range(SWA_Q_HEADS):
        hk = head // G
        qh = qb[:, head * SWA_HEAD_DIM:(head + 1) * SWA_HEAD_DIM]
        kh = kw[:, hk * SWA_HEAD_DIM:(hk + 1) * SWA_HEAD_DIM]
        vh = vw[:, hk * SWA_HEAD_DIM:(hk + 1) * SWA_HEAD_DIM]
        s = lax.dot_general(qh, kh, (((1,), (1,)), ((), ())), preferred_element_type=F32) * (SWA_HEAD_DIM ** -0.5)
        s = jnp.where(valid, s, MASK_VALUE)
        sink = sinks[:, head:head + 1]
        m = jnp.maximum(s.max(-1, keepdims=True), sink)
        p = jnp.exp(s - m)
        p = (p / (p.sum(-1, keepdims=True) + jnp.exp(sink - m))).astype(BF16)
        outs.append(jnp.dot(p, vh, preferred_element_type=F32))
    return jnp.concatenate(outs, axis=1)


def _swa_fwd(h, cq, ck, cv, sinks):
    n = h.shape[0]
    nb = n // SWA_BLOCK

    def body(q_ref, kp_ref, kc_ref, vp_ref, vc_ref, s_ref, o_ref):
        i = pl.program_id(0)
        f = lambda x: x[...].astype(F32)
        o_ref[...] = _swa_math(f(q_ref), f(kp_ref), f(kc_ref), f(vp_ref), f(vc_ref), s_ref[...], i > 0).astype(o_ref.dtype)

    prev = lambda cb: pl.BlockSpec((SWA_BLOCK, A_KV), lambda i, cb=cb: (jnp.maximum(i - 1, 0), cb))
    cur = lambda cb: pl.BlockSpec((SWA_BLOCK, A_KV), lambda i, cb=cb: (i, cb))
    return _pcall(body, name="swa_fwd", grid=(nb,), out_shape=SDS((n, A_Q), BF16),
                  in_specs=[pl.BlockSpec((SWA_BLOCK, A_Q), lambda i: (i, cq)), prev(ck), cur(ck), prev(cv), cur(cv),
                            pl.BlockSpec((1, SWA_Q_HEADS), lambda i: (0, 0))],
                  out_specs=pl.BlockSpec((SWA_BLOCK, A_Q), lambda i: (i, 0)),
                  compiler_params=_params(("arbitrary",)))(h, h, h, h, h, sinks)


def _swa_bwd(h, cq, ck, cv, sinks, dy):
    n = h.shape[0]
    nb = n // SWA_BLOCK

    def body(q_ref, kp_ref, kc_ref, vp_ref, vc_ref, s_ref, dy_ref, dq_ref, dk_ref, dv_ref, ds_ref, ck_ref, cv_ref):
        r = pl.program_id(0)
        blk = nb - 1 - r

        @pl.when(r == 0)
        def _():
            ds_ref[...] = jnp.zeros_like(ds_ref)
            ck_ref[...] = jnp.zeros_like(ck_ref)
            cv_ref[...] = jnp.zeros_like(cv_ref)

        f = lambda x: x[...].astype(F32)
        not_first = blk > 0
        _, vjp = jax.vjp(lambda q, kp, kc, vp, vc, s: _swa_math(q, kp, kc, vp, vc, s, not_first),
                         f(q_ref), f(kp_ref), f(kc_ref), f(vp_ref), f(vc_ref), s_ref[...])
        dq, dkp, dkc, dvp, dvc, dsk = vjp(f(dy_ref))
        dq_ref[...] = dq.astype(dq_ref.dtype)
        dk_ref[...] = (dkc + ck_ref[...]).astype(dk_ref.dtype)
        dv_ref[...] = (dvc + cv_ref[...]).astype(dv_ref.dtype)
        ck_ref[...] = dkp
        cv_ref[...] = dvp
        ds_ref[...] += dsk

    rev = lambda i: nb - 1 - i
    prev = lambda cb: pl.BlockSpec((SWA_BLOCK, A_KV), lambda i, cb=cb: (jnp.maximum(rev(i) - 1, 0), cb))
    cur = lambda cb: pl.BlockSpec((SWA_BLOCK, A_KV), lambda i, cb=cb: (rev(i), cb))
    return _pcall(
        body, name="swa_bwd", grid=(nb,),
        out_shape=[SDS((n, A_Q), BF16), SDS((n, A_KV), BF16), SDS((n, A_KV), BF16), SDS((1, SWA_Q_HEADS), F32)],
        in_specs=[pl.BlockSpec((SWA_BLOCK, A_Q), lambda i: (rev(i), cq)), prev(ck), cur(ck), prev(cv), cur(cv),
                  pl.BlockSpec((1, SWA_Q_HEADS), lambda i: (0, 0)), pl.BlockSpec((SWA_BLOCK, A_Q), lambda i: (rev(i), 0))],
        out_specs=[pl.BlockSpec((SWA_BLOCK, A_Q), lambda i: (rev(i), 0)), pl.BlockSpec((SWA_BLOCK, A_KV), lambda i: (rev(i), 0)),
                   pl.BlockSpec((SWA_BLOCK, A_KV), lambda i: (rev(i), 0)), pl.BlockSpec((1, SWA_Q_HEADS), lambda i: (0, 0))],
        scratch_shapes=[pltpu.VMEM((SWA_BLOCK, A_KV), F32), pltpu.VMEM((SWA_BLOCK, A_KV), F32)],
        compiler_params=_params(("arbitrary",)))(h, h, h, h, h, sinks, dy)


def _rope(x, cos, sin, sign):
    w = x.shape[1]
    reps = w // LANES
    ct = jnp.tile(cos, (1, reps)) if reps > 1 else cos
    st = jnp.tile(sin, (1, reps)) if reps > 1 else sin
    fwd = pltpu.roll(x, MLA_ROPE // 2, axis=1)
    bwd = pltpu.roll(x, w - MLA_ROPE // 2, axis=1)
    lane = lax.broadcasted_iota(jnp.int32, x.shape, 1) % LANES
    rot = jnp.where(lane < MLA_ROPE // 2, -bwd, fwd)
    return x * ct + sign * (rot * st)


def _rope_call(a, wa, ca, b, wb, cb, cos, sin, sign, tm, name):
    def fn(i, rows, ps):
        xa, xb, c_, s_ = rows
        return [_rope(xa.astype(F32), c_, s_, sign), _rope(xb.astype(F32), c_, s_, sign)], []
    (ra, rb), _ = _rowwise(fn, [(a, wa, ca), (b, wb, cb), _whole(cos), _whole(sin)], [], [(wa, BF16), (wb, BF16)], [], tm, name)
    return ra, rb


def _mla_scores(qn_ref, qr_ref, kn_ref, kr_ref, i, j, T):
    q = jnp.concatenate([qn_ref[...], qr_ref[...]], axis=1)
    k = jnp.concatenate([kn_ref[...], kr_ref[...]], axis=1)
    s = lax.dot_general(q, k, (((1,), (1,)), ((), ())), preferred_element_type=F32)
    s = s * ((MLA_NOPE + MLA_ROPE) ** -0.5)
    row = i * T + lax.broadcasted_iota(jnp.int32, (T, T), 0)
    col = j * T + lax.broadcasted_iota(jnp.int32, (T, T), 1)
    return jnp.where(col <= row, s, MASK_VALUE), q, k


def _mla_fwd(q_full, qr, kv, kr, T):
    n = q_full.shape[0]
    nq = n // T
    H = MLA_HEADS

    def body(qn_ref, qr_ref, kn_ref, v_ref, kr_ref, y_ref, lse_ref, m_ref, l_ref, acc_ref):
        i, j = pl.program_id(1), pl.program_id(2)

        @pl.when(j == 0)
        def _():
            m_ref[...] = jnp.full_like(m_ref, MASK_VALUE)
            l_ref[...] = jnp.zeros_like(l_ref)
            acc_ref[...] = jnp.zeros_like(acc_ref)

        @pl.when(j <= i)
        def _():
            s, _, _ = _mla_scores(qn_ref, qr_ref, kn_ref, kr_ref, i, j, T)
            m_prev = m_ref[...]
            m_new = jnp.maximum(m_prev, s.max(-1, keepdims=True))
            p = jnp.exp(s - m_new[:, :1])
            alpha = jnp.exp(m_prev - m_new)
            l_ref[...] = alpha * l_ref[...] + p.sum(-1, keepdims=True)
            acc_ref[...] = alpha * acc_ref[...] + jnp.dot(p.astype(BF16), v_ref[...], preferred_element_type=F32)
            m_ref[...] = m_new

        @pl.when(j == i)
        def _():
            y_ref[...] = (acc_ref[...] / l_ref[...]).astype(y_ref.dtype)
            lse_ref[0] = m_ref[...] + jnp.log(l_ref[...])

    qspec = lambda off: pl.BlockSpec((T, LANES), lambda h, i, j, off=off: (i, off + h))
    kspec = lambda off: pl.BlockSpec((T, LANES), lambda h, i, j, off=off: (jnp.minimum(j, i), off + h))
    return _pcall(
        body, name="mla_fwd", grid=(H, nq, nq),
        out_shape=[SDS((n, H * MLA_V), BF16), SDS((H, n, LANES), F32)],
        in_specs=[qspec(0), qspec(0), kspec(0), kspec(H), pl.BlockSpec((T, LANES), lambda h, i, j: (jnp.minimum(j, i), 0))],
        out_specs=[pl.BlockSpec((T, LANES), lambda h, i, j: (i, h)), pl.BlockSpec((1, T, LANES), lambda h, i, j: (h, i, 0))],
        scratch_shapes=[pltpu.VMEM((T, LANES), F32)] * 3,
        compiler_params=_params(("parallel", "parallel", "arbitrary")))(q_full, qr, kv, kv, kr)


def _mla_delta(dy, y, T):
    n = y.shape[0]
    H = MLA_HEADS

    def body(dy_ref, y_ref, d_ref):
        d = jnp.sum(dy_ref[...].astype(F32) * y_ref[...].astype(F32), axis=-1, keepdims=True)
        d_ref[0] = jnp.broadcast_to(d, (T, LANES))

    spec = pl.BlockSpec((T, LANES), lambda h, i: (i, h))
    return _pcall(body, name="mla_delta", grid=(H, n // T), out_shape=SDS((H, n, LANES), F32), in_specs=[spec, spec],
                  out_specs=pl.BlockSpec((1, T, LANES), lambda h, i: (h, i, 0)),
                  compiler_params=_params(("parallel", "parallel")))(dy, y)


def _mla_bwd_dq(q_full, qr, kv, kr, dy, lse, delta, T):
    n = q_full.shape[0]
    nq = n // T
    H = MLA_HEADS

    def body(qn_ref, qr_ref, kn_ref, v_ref, kr_ref, dy_ref, lse_ref, dl_ref, dqn_ref, dqr_ref, acc_ref):
        i, j = pl.program_id(1), pl.program_id(2)

        @pl.when(j == 0)
        def _():
            acc_ref[...] = jnp.zeros_like(acc_ref)

        @pl.when(j <= i)
        def _():
            s, _, k = _mla_scores(qn_ref, qr_ref, kn_ref, kr_ref, i, j, T)
            p = jnp.exp(s - lse_ref[0][:, :1])
            dp = lax.dot_general(dy_ref[...], v_ref[...], (((1,), (1,)), ((), ())), preferred_element_type=F32)
            ds = p * (dp - dl_ref[0][:, :1]) * ((MLA_NOPE + MLA_ROPE) ** -0.5)
            acc_ref[...] += jnp.dot(ds.astype(BF16), k, preferred_element_type=F32)

        @pl.when(j == i)
        def _():
            dqn_ref[...] = acc_ref[:, :LANES].astype(dqn_ref.dtype)
            dqr_ref[...] = acc_ref[:, LANES:].astype(dqr_ref.dtype)

    qspec = lambda off: pl.BlockSpec((T, LANES), lambda h, i, j, off=off: (i, off + h))
    kspec = lambda off: pl.BlockSpec((T, LANES), lambda h, i, j, off=off: (jnp.minimum(j, i), off + h))
    stat = pl.BlockSpec((1, T, LANES), lambda h, i, j: (h, i, 0))
    out = pl.BlockSpec((T, LANES), lambda h, i, j: (i, h))
    return _pcall(
        body, name="mla_bwd_dq", grid=(H, nq, nq),
        out_shape=[SDS((n, H * LANES), BF16), SDS((n, H * LANES), BF16)],
        in_specs=[qspec(0), qspec(0), kspec(0), kspec(H), pl.BlockSpec((T, LANES), lambda h, i, j: (jnp.minimum(j, i), 0)),
                  qspec(0), stat, stat],
        out_specs=[out, out], scratch_shapes=[pltpu.VMEM((T, 2 * LANES), F32)],
        compiler_params=_params(("parallel", "parallel", "arbitrary")))(q_full, qr, kv, kv, kr, dy, lse, delta)


def _mla_bwd_dkv(q_full, qr, kv, kr, dy, lse, delta, T):
    n = q_full.shape[0]
    nq = n // T
    H = MLA_HEADS

    def body(qn_ref, qr_ref, kn_ref, v_ref, kr_ref, dy_ref, lse_ref, dl_ref, dkn_ref, dv_ref, dkr_ref, dk_acc, dv_acc, dkr_acc):
        j, h, i = pl.program_id(0), pl.program_id(1), pl.program_id(2)

        @pl.when(i == 0)
        def _():
            dk_acc[...] = jnp.zeros_like(dk_acc)
            dv_acc[...] = jnp.zeros_like(dv_acc)

        @pl.when((i == 0) & (h == 0))
        def _():
            dkr_acc[...] = jnp.zeros_like(dkr_acc)

        @pl.when(i >= j)
        def _():
            s, q, _ = _mla_scores(qn_ref, qr_ref, kn_ref, kr_ref, i, j, T)
            p = jnp.exp(s - lse_ref[0][:, :1])
            dy = dy_ref[...]
            dv_acc[...] += lax.dot_general(p.astype(BF16), dy, (((0,), (0,)), ((), ())), preferred_element_type=F32)
            dp = lax.dot_general(dy, v_ref[...], (((1,), (1,)), ((), ())), preferred_element_type=F32)
            ds = p * (dp - dl_ref[0][:, :1]) * ((MLA_NOPE + MLA_ROPE) ** -0.5)
            dk_acc[...] += lax.dot_general(ds.astype(BF16), q, (((0,), (0,)), ((), ())), preferred_element_type=F32)

        @pl.when(i == nq - 1)
        def _():
            dkn_ref[...] = dk_acc[:, :LANES].astype(dkn_ref.dtype)
            dv_ref[...] = dv_acc[...].astype(dv_ref.dtype)
            dkr_acc[...] += dk_acc[:, LANES:]

        @pl.when((i == nq - 1) & (h == H - 1))
        def _():
            dkr_ref[...] = dkr_acc[...].astype(dkr_ref.dtype)

    qspec = lambda off: pl.BlockSpec((T, LANES), lambda j, h, i, off=off: (jnp.maximum(i, j), off + h))
    kspec = lambda off: pl.BlockSpec((T, LANES), lambda j, h, i, off=off: (j, off + h))
    stat = pl.BlockSpec((1, T, LANES), lambda j, h, i: (h, jnp.maximum(i, j), 0))
    out = pl.BlockSpec((T, LANES), lambda j, h, i: (j, h))
    return _pcall(
        body, name="mla_bwd_dkv", grid=(nq, H, nq),
        out_shape=[SDS((n, H * LANES), BF16), SDS((n, H * LANES), BF16), SDS((n, LANES), F32)],
        in_specs=[qspec(0), qspec(0), kspec(0), kspec(H), pl.BlockSpec((T, LANES), lambda j, h, i: (j, 0)), qspec(0), stat, stat],
        out_specs=[out, out, pl.BlockSpec((T, LANES), lambda j, h, i: (j, 0))],
        scratch_shapes=[pltpu.VMEM((T, 2 * LANES), F32), pltpu.VMEM((T, LANES), F32), pltpu.VMEM((T, LANES), F32)],
        compiler_params=_params(("arbitrary", "arbitrary", "arbitrary")))(q_full, qr, kv, kv, kr, dy, lse, delta)


def _shift_down(x, k):
    row = lax.broadcasted_iota(jnp.int32, x.shape, 0)
    return jnp.where(row >= k, pltpu.roll(x, k, axis=0), 0.0)


def _shift_up(x, k):
    n = x.shape[0]
    row = lax.broadcasted_iota(jnp.int32, x.shape, 0)
    return jnp.where(row < n - k, pltpu.roll(x, n - k, axis=0), 0.0)


def _conv_fwd(up, w, b):
    n, c = up.shape

    def body(u_ref, w_ref, b_ref, o_ref):
        u = u_ref[...].astype(F32)
        wv = w_ref[...]
        o_ref[...] = (b_ref[...] + wv[0:1] * _shift_down(u, 2) + wv[1:2] * _shift_down(u, 1) + wv[2:3] * u).astype(o_ref.dtype)

    return _pcall(body, name="conv_fwd", grid=(c // LANES,), out_shape=SDS((n, c), BF16),
                  in_specs=[pl.BlockSpec((n, LANES), lambda j: (0, j)), pl.BlockSpec((3, LANES), lambda j: (0, j)),
                            pl.BlockSpec((1, LANES), lambda j: (0, j))],
                  out_specs=pl.BlockSpec((n, LANES), lambda j: (0, j)), compiler_params=_params(("parallel",)))(up, w, b)


def _conv_bwd(up, dc, w):
    n, c = up.shape

    def body(u_ref, d_ref, w_ref, du_ref, dw_ref, db_ref):
        u = u_ref[...].astype(F32)
        d = d_ref[...].astype(F32)
        wv = w_ref[...]
        du_ref[...] = (wv[2:3] * d + wv[1:2] * _shift_up(d, 1) + wv[0:1] * _shift_up(d, 2)).astype(du_ref.dtype)
        dw_ref[0:1, :] = jnp.sum(d * _shift_down(u, 2), axis=0, keepdims=True)
        dw_ref[1:2, :] = jnp.sum(d * _shift_down(u, 1), axis=0, keepdims=True)
        dw_ref[2:3, :] = jnp.sum(d * u, axis=0, keepdims=True)
        db_ref[...] = jnp.sum(d, axis=0, keepdims=True)

    col = pl.BlockSpec((n, LANES), lambda j: (0, j))
    return _pcall(body, name="conv_bwd", grid=(c // LANES,),
                  out_shape=[SDS((n, c), BF16), SDS((3, c), F32), SDS((1, c), F32)],
                  in_specs=[col, col, pl.BlockSpec((3, LANES), lambda j: (0, j))],
                  out_specs=[col, pl.BlockSpec((3, LANES), lambda j: (0, j)), pl.BlockSpec((1, LANES), lambda j: (0, j))],
                  compiler_params=_params(("parallel",)))(up, dc, w)


def _adamw_math(w, g, m, v):
    m = ADAM_B1 * m + (1.0 - ADAM_B1) * g
    v = ADAM_B2 * v + (1.0 - ADAM_B2) * jnp.square(g)
    m_hat = m / (1.0 - ADAM_B1 ** ADAM_STEP)
    v_hat = v / (1.0 - ADAM_B2 ** ADAM_STEP)
    delta = -ADAM_LR * (m_hat / (jnp.sqrt(v_hat) + ADAM_EPS) + ADAM_WD * w)
    return delta, m, v


def _adamw(w, g, m, v, name):
    r, c = w.shape
    tr = r
    budget = max(8, (1 << 20) // (4 * c))
    t = (min(budget, r) // 8) * 8
    while t >= 8:
        if r % t == 0:
            tr = t
            break
        t -= 8

    def fn(i, rows, ps):
        return list(_adamw_math(*rows)), []
    (d, nm, nv), _ = _rowwise(fn, [_whole(w), _whole(g), _whole(m), _whole(v)], [], [(c, F32)] * 3, [], tr, name)
    return d, nm, nv


def _coords():
    return lax.axis_index("x"), lax.axis_index("y"), lax.axis_index("c")


def _other_chips(x, y):
    return [(1 - x, y), (x, 1 - y), (1 - x, 1 - y)]


HBM_SPEC = pl.BlockSpec(memory_space=pltpu.HBM)


def _half(c, rows):
    return pl.ds(pl.multiple_of(c * rows, 16), rows)


def _rows_tile(rows, cols, budget_bytes=2 << 20, align=16):
    t = (min(max(align, budget_bytes // (4 * cols)), rows) // align) * align
    while t >= align:
        if rows % t == 0:
            return t
        t -= align
    return rows


def _scalar(v):
    return jnp.reshape(jnp.asarray(v, jnp.int32), (1,))


def _cast_into_slot(w3, layer, slot, name):
    _, R, C = w3.shape
    tr = _rows_tile(R, C)

    def body(s_ref, w_ref, o_ref):
        o_ref[0] = w_ref[0].astype(BF16)

    gs = pltpu.PrefetchScalarGridSpec(
        num_scalar_prefetch=1, grid=(R // tr,),
        in_specs=[pl.BlockSpec((1, tr, C), lambda i, s: (layer, i, 0))],
        out_specs=pl.BlockSpec((1, tr, C), lambda i, s: (s[0], i, 0)))
    return _pcall(body, name=name, grid_spec=gs, out_shape=SDS((N_CHIPS, R, C), BF16),
                  compiler_params=_params(("arbitrary",)))(_scalar(slot), w3)


def _gather_layer(bufs, name):
    n = len(bufs)

    def body(*refs):
        outs, send_sems, recv_sems = refs[n:2 * n], refs[2 * n], refs[2 * n + 1]
        x, y, c = _coords()
        me = 2 * x + y
        sib = (x, y, 1 - c)
        chips = _other_chips(x, y)

        def copy(w, k, chip_idx, half, to):
            blk = outs[w].at[chip_idx, _half(half, outs[w].shape[1] // 2), :]
            return pltpu.make_async_remote_copy(src_ref=blk, dst_ref=blk, send_sem=send_sems.at[6 * w + k],
                                                recv_sem=recv_sems.at[6 * w + k], device_id=to, device_id_type=MESH)

        first = [copy(w, j, me, c, (px, py, c)) for w in range(n) for j, (px, py) in enumerate(chips)]
        for cp in first:
            cp.start()
        passed = []
        for w in range(n):
            for j, (px, py) in enumerate(chips):
                copy(w, j, 2 * px + py, c, (px, py, c)).wait_recv()
                fw = copy(w, 3 + j, 2 * px + py, c, sib)
                fw.start()
                passed.append(fw)
        for w in range(n):
            for j, (px, py) in enumerate(chips):
                copy(w, 3 + j, 2 * px + py, 1 - c, sib).wait_recv()
        for cp in first + passed:
            cp.wait_send()

    return _pcall(body, name=name, out_shape=[SDS(b.shape, b.dtype) for b in bufs],
                  in_specs=[HBM_SPEC] * n, out_specs=[HBM_SPEC] * n, input_output_aliases={i: i for i in range(n)},
                  scratch_shapes=[pltpu.SemaphoreType.DMA((6 * n,)), pltpu.SemaphoreType.DMA((6 * n,))])(*bufs)


def _swap_halves(gs_, name):
    n = len(gs_)

    def body(*refs):
        ins, outs, send_sems, recv_sems = refs[:n], refs[n:2 * n], refs[2 * n], refs[2 * n + 1]
        x, y, c = _coords()
        cps = []
        for w in range(n):
            rows = ins[w].shape[1] // 2
            cps.append(pltpu.make_async_remote_copy(src_ref=ins[w].at[:, _half(1 - c, rows), :], dst_ref=outs[w],
                                                    send_sem=send_sems.at[w], recv_sem=recv_sems.at[w],
                                                    device_id=(x, y, 1 - c), device_id_type=MESH))
        for cp in cps:
            cp.start()
        for cp in cps:
            cp.wait()

    return _pcall(body, name=name, out_shape=[SDS((N_CHIPS, g.shape[1] // 2, g.shape[2]), g.dtype) for g in gs_],
                  in_specs=[HBM_SPEC] * n, out_specs=[HBM_SPEC] * n,
                  scratch_shapes=[pltpu.SemaphoreType.DMA((n,)), pltpu.SemaphoreType.DMA((n,))])(*gs_)


def _scatter_to_chips(ps, name):
    n = len(ps)

    def body(*refs):
        ins, outs, send_sems, recv_sems = refs[:n], refs[n:2 * n], refs[2 * n], refs[2 * n + 1]
        x, y, c = _coords()
        chips = _other_chips(x, y)
        cps = [pltpu.make_async_remote_copy(src_ref=ins[w].at[2 * px + py], dst_ref=outs[w].at[j], send_sem=send_sems.at[3 * w + j],
                                            recv_sem=recv_sems.at[3 * w + j], device_id=(px, py, c), device_id_type=MESH)
               for w in range(n) for j, (px, py) in enumerate(chips)]
        for cp in cps:
            cp.start()
        for cp in cps:
            cp.wait()

    return _pcall(body, name=name, out_shape=[SDS((3,) + p.shape[1:], p.dtype) for p in ps],
                  in_specs=[HBM_SPEC] * n, out_specs=[HBM_SPEC] * n,
                  scratch_shapes=[pltpu.SemaphoreType.DMA((3 * n,)), pltpu.SemaphoreType.DMA((3 * n,))])(*ps)


def _join_halves(bufs, name):
    n = len(bufs)
    L = bufs[0].shape[0]

    def body(*refs):
        outs, send_sems, recv_sems = refs[n:2 * n], refs[2 * n], refs[2 * n + 1]
        x, y, c = _coords()

        def copy(w, l, half):
            blk = outs[w].at[l, _half(half, outs[w].shape[1] // 2), :]
            return pltpu.make_async_remote_copy(src_ref=blk, dst_ref=blk, send_sem=send_sems.at[L * w + l],
                                                recv_sem=recv_sems.at[L * w + l], device_id=(x, y, 1 - c), device_id_type=MESH)

        cps = [copy(w, l, c) for w in range(n) for l in range(L)]
        for cp in cps:
            cp.start()
        for w in range(n):
            for l in range(L):
                copy(w, l, 1 - c).wait_recv()
        for cp in cps:
            cp.wait_send()

    return _pcall(body, name=name, out_shape=[SDS(b.shape, b.dtype) for b in bufs],
                  in_specs=[HBM_SPEC] * n, out_specs=[HBM_SPEC] * n, input_output_aliases={i: i for i in range(n)},
                  scratch_shapes=[pltpu.SemaphoreType.DMA((L * n,)), pltpu.SemaphoreType.DMA((L * n,))])(*bufs)


def _all_reduce_small(v):
    n = v.shape[0]

    def body(v_ref, out_ref, slots, send_sems, recv_sems):
        x, y, c = _coords()
        me = 4 * x + 2 * y + c
        cps = []
        for r in range(1, 8):
            t = (me + r) % 8
            cp = pltpu.make_async_remote_copy(src_ref=v_ref, dst_ref=slots.at[me], send_sem=send_sems.at[r - 1],
                                              recv_sem=recv_sems.at[me], device_id=(t // 4, (t // 2) % 2, t % 2),
                                              device_id_type=MESH)
            cp.start()
            cps.append(cp)
        slots[me] = v_ref[...]
        for r in range(1, 8):
            s = (me + r) % 8
            pltpu.make_async_remote_copy(src_ref=v_ref, dst_ref=slots.at[s], send_sem=send_sems.at[r - 1],
                                         recv_sem=recv_sems.at[s], device_id=(x, y, c), device_id_type=MESH).wait_recv()
        for cp in cps:
            cp.wait_send()
        acc = slots[0]
        for d in range(1, 8):
            acc = acc + slots[d]
        out_ref[...] = acc

    return _pcall(body, name="all_reduce_small", out_shape=SDS((n, LANES), F32),
                  in_specs=[pl.BlockSpec(memory_space=pltpu.VMEM)], out_specs=pl.BlockSpec(memory_space=pltpu.VMEM),
                  scratch_shapes=[pltpu.VMEM((8, n, LANES), F32), pltpu.SemaphoreType.DMA((7,)), pltpu.SemaphoreType.DMA((8,))],
                  compiler_params=pltpu.CompilerParams(vmem_limit_bytes=VMEM_LIMIT))(v)


def _add_half(g, recv, c, name):
    _, R, C = g.shape
    rows = R // 2
    tr = _rows_tile(rows, C, 1 << 20)
    nb = rows // tr

    def body(s_ref, g_ref, r_ref, o32_ref, o16_ref):
        s = g_ref[...] + r_ref[...]
        o32_ref[...] = s
        o16_ref[...] = s.astype(BF16)

    blk = lambda k, i, s: (k, i, 0)
    gs = pltpu.PrefetchScalarGridSpec(
        num_scalar_prefetch=1, grid=(N_CHIPS, nb),
        in_specs=[pl.BlockSpec((1, tr, C), lambda k, i, s: (k, s[0] * nb + i, 0)), pl.BlockSpec((1, tr, C), blk)],
        out_specs=[pl.BlockSpec((1, tr, C), blk), pl.BlockSpec((1, tr, C), blk)])
    return _pcall(body, name=name, grid_spec=gs, out_shape=[SDS((N_CHIPS, rows, C), F32), SDS((N_CHIPS, rows, C), BF16)],
                  compiler_params=_params(("arbitrary", "arbitrary")))(_scalar(c), g, recv)


def _sum_into(p32, arrived, chip, c, layer, n_layers, prev, name):
    _, rows, C = p32.shape
    tr = _rows_tile(rows, C, 1 << 20)
    nb = rows // tr

    def body(chip_ref, c_ref, p_ref, a_ref, *rest):
        o_ref = rest[-1]
        o_ref[0] = ((p_ref[0] + a_ref[0].astype(F32)) + a_ref[1].astype(F32)) + a_ref[2].astype(F32)

    in_specs = [pl.BlockSpec((1, tr, C), lambda i, chip_ref, c_ref: (chip_ref[0], i, 0)),
                pl.BlockSpec((3, tr, C), lambda i, chip_ref, c_ref: (0, i, 0))]
    ins = [p32, arrived]
    aliases = {}
    if prev is not None:
        in_specs.append(pl.BlockSpec(memory_space=pl.ANY))
        ins.append(prev)
        aliases = {4: 0}
    gs = pltpu.PrefetchScalarGridSpec(
        num_scalar_prefetch=2, grid=(nb,), in_specs=in_specs,
        out_specs=pl.BlockSpec((1, tr, C), lambda i, chip_ref, c_ref: (layer, c_ref[0] * nb + i, 0)))
    return _pcall(body, name=name, grid_spec=gs, out_shape=SDS((n_layers, 2 * rows, C), F32), input_output_aliases=aliases,
                  compiler_params=_params(("arbitrary",)))(_scalar(chip), _scalar(c), *ins)


def _h_layout(D):
    G = N_BRANCHES * D
    off, o = {}, 0
    for name, w in [("g", G), ("qa", A_Q), ("cq", MLA_Q_RANK), ("ckv", MLA_KV_RANK), ("hu", SGU_WIDTH), ("hv", SGU_WIDTH),
                    ("ka", A_KV), ("va", A_KV), ("kr", LANES)]:
        assert o % w == 0, (name, o, w)
        off[name] = (o, w)
        o += w
    off["total"] = -(-o // 512) * 512
    return off


def _perm_w_in(w, lay):
    s = np.cumsum([0, A_Q, A_KV, A_KV, MLA_Q_RANK, MLA_KV_RANK, MLA_ROPE, SGU_WIDTH, SGU_WIDTH])
    qa, ka, va, cq, ckv, kr, hu, hv = [w[:, s[i]:s[i + 1]] for i in range(8)]
    g = w[:, s[8]:]
    pad = jnp.zeros((w.shape[0], lay["total"] - lay["kr"][0] - MLA_ROPE), w.dtype)
    return jnp.concatenate([g, qa, cq, ckv, hu, hv, ka, va, kr, pad], axis=1)


def _unperm_w_in(wp, lay, D):
    take = lambda n, width=None: wp[:, lay[n][0]:lay[n][0] + (width or lay[n][1])]
    return jnp.concatenate([take("qa"), take("ka"), take("va"), take("cq"), take("ckv"), take("kr", MLA_ROPE), take("hu"),
                            take("hv"), take("g")], axis=1)


def _perm_w_uq(w):
    r = w.shape[0]
    w3 = w.reshape(r, MLA_HEADS, MLA_NOPE + MLA_ROPE)
    nope = w3[:, :, :MLA_NOPE].reshape(r, MLA_HEADS * MLA_NOPE)
    rope = jnp.pad(w3[:, :, MLA_NOPE:], ((0, 0), (0, 0), (0, LANES - MLA_ROPE))).reshape(r, MLA_HEADS * LANES)
    return jnp.concatenate([nope, rope], axis=1)


def _unperm_w_uq(wp):
    r = wp.shape[0]
    nope = wp[:, :MLA_HEADS * MLA_NOPE].reshape(r, MLA_HEADS, MLA_NOPE)
    rope = wp[:, MLA_HEADS * MLA_NOPE:].reshape(r, MLA_HEADS, LANES)[:, :, :MLA_ROPE]
    return jnp.concatenate([nope, rope], axis=2).reshape(r, MLA_HEADS * (MLA_NOPE + MLA_ROPE))


def _perm_w_ukv(w):
    r = w.shape[0]
    w3 = w.reshape(r, MLA_HEADS, MLA_NOPE + MLA_V)
    return jnp.concatenate([w3[:, :, :MLA_NOPE].reshape(r, -1), w3[:, :, MLA_NOPE:].reshape(r, -1)], axis=1)


def _unperm_w_ukv(wp):
    r = wp.shape[0]
    k = wp[:, :MLA_HEADS * MLA_NOPE].reshape(r, MLA_HEADS, MLA_NOPE)
    v = wp[:, MLA_HEADS * MLA_NOPE:].reshape(r, MLA_HEADS, MLA_V)
    return jnp.concatenate([k, v], axis=2).reshape(r, -1)


def _col_chunks(g):
    r, c4 = g.shape
    return jnp.transpose(g.reshape(r, N_CHIPS, c4 // N_CHIPS), (1, 0, 2))


def kernel(x, positions, w_in, b_gate, sinks, q_norm_g, kv_norm_g, w_uq, w_ukv, sgu_ln_g, sgu_ln_b, sgu_w, sgu_b, w_proj_a, w_proj_b, w_proj_c, w_o, ln1_g, ln1_b, w_up, conv_w, conv_b, w_down, ln2_g, ln2_b, loss_target, m_w_in, m_b_gate, m_sinks, m_q_norm_g, m_kv_norm_g, m_w_uq, m_w_ukv, m_sgu_ln_g, m_sgu_ln_b, m_sgu_w, m_sgu_b, m_w_proj_a, m_w_proj_b, m_w_proj_c, m_w_o, m_ln1_g, m_ln1_b, m_w_up, m_conv_w, m_conv_b, m_w_down, m_ln2_g, m_ln2_b, v_w_in, v_b_gate, v_sinks, v_q_norm_g, v_kv_norm_g, v_w_uq, v_w_ukv, v_sgu_ln_g, v_sgu_ln_b, v_sgu_w, v_sgu_b, v_w_proj_a, v_w_proj_b, v_w_proj_c, v_w_o, v_ln1_g, v_ln1_b, v_w_up, v_conv_w, v_conv_b, v_w_down, v_ln2_g, v_ln2_b):
    a = locals()
    W = {k: a[k] for k in WEIGHTS}
    Mo = {k: a["m_" + k] for k in WEIGHTS}
    Vo = {k: a["v_" + k] for k in WEIGHTS}
    S, D = x.shape[1], x.shape[2]
    FF2 = w_up.shape[2] * N_CHIPS
    FF = FF2 // 2
    L = DEPTH
    lay = _h_layout(D)
    NP = lay["total"]
    cx, cy, cc = _coords()
    chip = 2 * cx + cy
    T = _tile(S, 512)
    TM = _tile(S, 256, 16)
    TMW = _tile(S, 64, 16)

    shards = {k: tuple(W[k].shape) for k in BIG}
    full = {k: [None] * L for k in BIG}
    for l in range(L):
        bufs = [_cast_into_slot(W[k], l, chip, f"cast_{k}_l{l}") for k in BIG]
        for k, g in zip(BIG, _gather_layer(bufs, f"gather_weights_l{l}")):
            _, r, c_ = shards[k]
            full[k][l] = g.reshape(N_CHIPS * r, c_) if k in ROW_SHARDED else jnp.transpose(g, (1, 0, 2)).reshape(r, N_CHIPS * c_)

    small_sharded_full = {k: tuple(W[k].shape[:-1]) + (W[k].shape[-1] * N_CHIPS,) for k in SMALL_SHARDED}
    placed = []
    for k in ("b_gate", "conv_w"):
        z = jnp.zeros(small_sharded_full[k], F32)
        z = lax.dynamic_update_slice_in_dim(z, W[k], chip * W[k].shape[-1], axis=-1)
        placed.append(jnp.where(cc == 0, z, 0.0).reshape(-1))
    pv = jnp.concatenate(placed)
    n_pv = pv.shape[0]
    pv = jnp.pad(pv, (0, -n_pv % (8 * LANES))).reshape(-1, LANES)
    pv = _all_reduce_small(pv).reshape(-1)
    nb_ = int(np.prod(small_sharded_full["b_gate"]))
    b_gate_full = pv[:nb_].reshape(small_sharded_full["b_gate"])
    conv_w_full = pv[nb_:n_pv].reshape(small_sharded_full["conv_w"])

    inv_freq = ROPE_THETA ** (-jnp.arange(0, MLA_ROPE, 2, dtype=F32) / MLA_ROPE)
    ang = positions[0].astype(F32)[:, None] * inv_freq
    cos, sin = jnp.cos(ang), jnp.sin(ang)
    cos_t = jnp.concatenate([cos, cos, jnp.ones((S, LANES - MLA_ROPE), F32)], axis=1)
    sin_t = jnp.concatenate([sin, sin, jnp.zeros((S, LANES - MLA_ROPE), F32)], axis=1)

    row = lambda v: v.reshape(1, -1)
    cb = lambda name: lay[name][0] // lay[name][1]

    xs = x[0]
    saved = []
    for l in range(L):
        p = dict(
            w_in=_perm_w_in(full["w_in"][l], lay), w_uq=_perm_w_uq(full["w_uq"][l]), w_ukv=_perm_w_ukv(full["w_ukv"][l]),
            w_pa=full["w_proj_a"][l], w_pb=full["w_proj_b"][l], w_pc=full["w_proj_c"][l], w_o=full["w_o"][l],
            w_up=full["w_up"][l], w_down=full["w_down"][l],
            sinks=row(sinks[l]), qg=row(q_norm_g[l]), kvg=row(kv_norm_g[l]), sg=row(sgu_ln_g[l]), sb=row(sgu_ln_b[l]),
            sw=sgu_w[l], sb3=sgu_b[l].reshape(SGU_GROUPS, SGU_CHUNK, 1),
            bg=b_gate_full[l], l1g=row(ln1_g[l]), l1b=row(ln1_b[l]), cw=conv_w_full[l], cbias=row(conv_b[l]),
            l2g=row(ln2_g[l]), l2b=row(ln2_b[l]))
        if l == 0:
            def fn_cast(i, rows, ps):
                return [rows[0]], []
            (xb,), _ = _rowwise(fn_cast, [_whole(xs)], [], [(D, BF16)], [], TM, "cast_x")
        h = _mm(xb, p["w_in"], "nn", BF16, "mm_h")
        y_a = _swa_fwd(h, cb("qa"), cb("ka"), cb("va"), p["sinks"])
        def fn_rms(i, rows, ps):
            return [_rms_norm(rows[0].astype(F32), ps[0]), _rms_norm(rows[1].astype(F32), ps[1])], []
        (cqn, ckvn), _ = _rowwise(fn_rms, [(h, MLA_Q_RANK, cb("cq")), (h, MLA_KV_RANK, cb("ckv"))], [p["qg"], p["kvg"]],
                                  [(MLA_Q_RANK, BF16), (MLA_KV_RANK, BF16)], [], TM, "mla_rms")
        q_full = _mm(cqn, p["w_uq"], "nn", BF16, "mm_q")
        kv = _mm(ckvn, p["w_ukv"], "nn", BF16, "mm_kv")
        qr, kr = _rope_call(q_full, MLA_HEADS * LANES, 1, h, LANES, cb("kr"), cos_t, sin_t, 1.0, TM, "rope_fwd")
        y_b, lse = _mla_fwd(q_full, qr, kv, kr, T)
        y_c = _sgu_fwd(h, cb("hu"), cb("hv"), p["sg"], p["sb"], p["sw"], p["sb3"])
        pa = _mm(y_a, p["w_pa"], "nn", F32, "mm_pa")
        pb = _mm(y_b, p["w_pb"], "nn", F32, "mm_pb")
        pc = _mm(y_c, p["w_pc"], "nn", F32, "mm_pc")

        def merge_math(pa_, pb_, pc_, g_, b0, b1, b2):
            out = 0.0
            for br, (pp, bb) in enumerate(zip((pa_, pb_, pc_), (b0, b1, b2))):
                gate = jax.nn.sigmoid(g_[:, br * D:(br + 1) * D].astype(F32) + bb)
                out = out + gate * pp
            return out

        def fn_merge(i, rows, ps):
            bgv = ps[0]
            return [merge_math(rows[0], rows[1], rows[2], rows[3], bgv[0:1], bgv[1:2], bgv[2:3])], []
        (merged,), _ = _rowwise(fn_merge, [_whole(pa), _whole(pb), _whole(pc), (h, N_BRANCHES * D, cb("g"))], [p["bg"]],
                                [(D, BF16)], [], TMW, "merge_fwd")
        o = _mm(merged, p["w_o"], "nn", F32, "mm_o")

        def ln_res_math(x_, o_, g_, b_):
            return _layer_norm(DN_ALPHA * x_ + o_, g_, b_)

        def fn_ln(i, rows, ps):
            y = ln_res_math(rows[0], rows[1], ps[0], ps[1])
            return [y, y], []
        (x1, x1b), _ = _rowwise(fn_ln, [_whole(xs), _whole(o)], [p["l1g"], p["l1b"]], [(D, F32), (D, BF16)], [], TM, "ln1_fwd")
        up = _mm(x1b, p["w_up"], "nn", BF16, "mm_up")
        cv_ = _conv_fwd(up, p["cw"], p["cbias"])

        def glu_math(cg, cvv):
            return jax.nn.silu(cg.astype(F32)) * cvv.astype(F32)

        def fn_glu(i, rows, ps):
            return [glu_math(rows[0], rows[1])], []
        (act,), _ = _rowwise(fn_glu, [(cv_, FF, 0), (cv_, FF, 1)], [], [(FF, BF16)], [], TMW, "glu_fwd")
        dn = _mm(act, p["w_down"], "nn", F32, "mm_down")
        (x2, x2b), _ = _rowwise(fn_ln, [_whole(x1), _whole(dn)], [p["l2g"], p["l2b"]], [(D, F32), (D, BF16)], [], TM, "ln2_fwd")
        saved.append(dict(p=p, x0=xs, x0b=xb, h=h, y_a=y_a, cqn=cqn, ckvn=ckvn, q_full=q_full, kv=kv, qr=qr, kr=kr, y_b=y_b,
                          lse=lse, y_c=y_c, pa=pa, pb=pb, pc=pc, merged=merged, o=o, x1=x1, x1b=x1b, up=up, cv=cv_, act=act,
                          dn=dn))
        xs, xb = x2, x2b

    def fn_loss(i, rows, ps):
        diff = rows[0] - rows[1]
        part = jnp.sum(jnp.mean(jnp.square(diff), axis=-1, keepdims=True), axis=0, keepdims=True)
        return [diff * (1.0 / D)], [jnp.broadcast_to(part, (8, LANES))]
    (dx,), (loss_acc,) = _rowwise(fn_loss, [_whole(xs), _whole(loss_target[0])], [], [(D, F32)], [(8, LANES)], TM, "loss")
    loss = lax.psum(0.5 * loss_acc[0, 0], ("x", "y", "c"))

    gbig = {k: [None] * L for k in BIG}
    gsmall = {k: [None] * L for k in SMALL}
    reduced = {}
    for l in reversed(range(L)):
        s = saved[l]
        p = s["p"]

        def fn_ln_bwd(i, rows, ps):
            _, vjp = jax.vjp(ln_res_math, rows[0], rows[1], ps[0], ps[1])
            dx_, do_, dg_, db_ = vjp(rows[2])
            return [dx_, do_], [dg_, db_]
        (dx1_res, ddn), (g_l2g, g_l2b) = _rowwise(fn_ln_bwd, [_whole(s["x1"]), _whole(s["dn"]), _whole(dx)], [p["l2g"], p["l2b"]],
                                                  [(D, F32), (D, BF16)], [(1, D), (1, D)], TM, "ln2_bwd")
        gsmall["ln2_g"][l], gsmall["ln2_b"][l] = g_l2g, g_l2b
        row_chunks = lambda g: g.reshape(N_CHIPS, g.shape[0] // N_CHIPS, g.shape[1])
        gbig["w_down"][l] = row_chunks(_mm(s["act"], ddn, "tn", F32, "mm_dw_down"))
        dact = _mm(ddn, p["w_down"], "nt", BF16, "mm_dact")

        def fn_glu_bwd(i, rows, ps):
            _, vjp = jax.vjp(glu_math, rows[0], rows[1])
            dcg, dcv = vjp(rows[2].astype(F32))
            return [jnp.concatenate([dcg, dcv], axis=1)], []
        (dc,), _ = _rowwise(fn_glu_bwd, [(s["cv"], FF, 0), (s["cv"], FF, 1), _whole(dact)], [], [(FF2, BF16)], [], TMW, "glu_bwd")
        dup, g_cw, g_cb = _conv_bwd(s["up"], dc, p["cw"])
        gsmall["conv_w"][l], gsmall["conv_b"][l] = g_cw, g_cb
        gbig["w_up"][l] = _mm(s["x1b"], dup, "tn", F32, "mm_dw_up", col_chunks=N_CHIPS)
        dx1 = _mm(dup, p["w_up"], "nt", F32, "mm_dx1", add=dx1_res)
        (dx0_res, do_), (g_l1g, g_l1b) = _rowwise(fn_ln_bwd, [_whole(s["x0"]), _whole(s["o"]), _whole(dx1)], [p["l1g"], p["l1b"]],
                                                  [(D, F32), (D, BF16)], [(1, D), (1, D)], TM, "ln1_bwd")
        gsmall["ln1_g"][l], gsmall["ln1_b"][l] = g_l1g, g_l1b
        gbig["w_o"][l] = row_chunks(_mm(s["merged"], do_, "tn", F32, "mm_dw_o"))
        dmerged = _mm(do_, p["w_o"], "nt", F32, "mm_dmerged")

        def fn_merge_bwd(i, rows, ps):
            bgv = ps[0]
            _, vjp = jax.vjp(merge_math, rows[0], rows[1], rows[2], rows[3], bgv[0:1], bgv[1:2], bgv[2:3])
            dpa, dpb, dpc, dg_, db0, db1, db2 = vjp(rows[4])
            return [dpa, dpb, dpc, dg_], [db0, db1, db2]
        (dpa, dpb, dpc, dgl), (db0, db1, db2) = _rowwise(
            fn_merge_bwd, [_whole(s["pa"]), _whole(s["pb"]), _whole(s["pc"]), (s["h"], N_BRANCHES * D, cb("g")), _whole(dmerged)],
            [p["bg"]], [(D, BF16), (D, BF16), (D, BF16), (N_BRANCHES * D, BF16)], [(1, D)] * 3, TMW, "merge_bwd")
        gsmall["b_gate"][l] = jnp.concatenate([db0, db1, db2], axis=0)
        gbig["w_proj_a"][l] = _mm(s["y_a"], dpa, "tn", F32, "mm_dw_pa", col_chunks=N_CHIPS)
        gbig["w_proj_b"][l] = row_chunks(_mm(s["y_b"], dpb, "tn", F32, "mm_dw_pb"))
        gbig["w_proj_c"][l] = _mm(s["y_c"], dpc, "tn", F32, "mm_dw_pc", col_chunks=N_CHIPS)
        dy_a = _mm(dpa, p["w_pa"], "nt", BF16, "mm_dy_a")
        dy_b = _mm(dpb, p["w_pb"], "nt", BF16, "mm_dy_b")
        dy_c = _mm(dpc, p["w_pc"], "nt", BF16, "mm_dy_c")
        dh_c, g_sg, g_sb, g_sw, g_sb3 = _sgu_bwd(s["h"], cb("hu"), cb("hv"), dy_c, p["sg"], p["sb"], p["sw"], p["sb3"])
        gsmall["sgu_ln_g"][l], gsmall["sgu_ln_b"][l], gsmall["sgu_w"][l] = g_sg, g_sb, g_sw
        gsmall["sgu_b"][l] = g_sb3.reshape(SGU_GROUPS, SGU_CHUNK)
        dqa, dka, dva, g_sinks = _swa_bwd(s["h"], cb("qa"), cb("ka"), cb("va"), p["sinks"], dy_a)
        gsmall["sinks"][l] = g_sinks
        delta = _mla_delta(dy_b, s["y_b"], T)
        dqn, dqr = _mla_bwd_dq(s["q_full"], s["qr"], s["kv"], s["kr"], dy_b, s["lse"], delta, T)
        dkn, dv, dkr = _mla_bwd_dkv(s["q_full"], s["qr"], s["kv"], s["kr"], dy_b, s["lse"], delta, T)
        dqr_raw, dkr_raw = _rope_call(dqr, MLA_HEADS * LANES, 0, dkr, LANES, 0, cos_t, sin_t, -1.0, TM, "rope_bwd")
        dq_full = jnp.concatenate([dqn, dqr_raw], axis=1)
        dkv = jnp.concatenate([dkn, dv], axis=1)
        gbig["w_uq"][l] = _col_chunks(_unperm_w_uq(_mm(s["cqn"], dq_full, "tn", F32, "mm_dw_uq")))
        gbig["w_ukv"][l] = _col_chunks(_unperm_w_ukv(_mm(s["ckvn"], dkv, "tn", F32, "mm_dw_ukv")))
        dcqn = _mm(dq_full, p["w_uq"], "nt", F32, "mm_dcqn")
        dckvn = _mm(dkv, p["w_ukv"], "nt", F32, "mm_dckvn")

        def fn_rms_bwd(i, rows, ps):
            _, vjp1 = jax.vjp(lambda c_, g_: _rms_norm(c_.astype(F32), g_), rows[0], ps[0])
            _, vjp2 = jax.vjp(lambda c_, g_: _rms_norm(c_.astype(F32), g_), rows[1], ps[1])
            d1, dg1 = vjp1(rows[2])
            d2, dg2 = vjp2(rows[3])
            return [d1, d2], [dg1, dg2]
        (dcq, dckv), (g_qg, g_kvg) = _rowwise(
            fn_rms_bwd, [(s["h"], MLA_Q_RANK, cb("cq")), (s["h"], MLA_KV_RANK, cb("ckv")), _whole(dcqn), _whole(dckvn)],
            [p["qg"], p["kvg"]], [(MLA_Q_RANK, BF16), (MLA_KV_RANK, BF16)], [(1, MLA_Q_RANK), (1, MLA_KV_RANK)], TM, "mla_rms_bwd")
        gsmall["q_norm_g"][l], gsmall["kv_norm_g"][l] = g_qg, g_kvg
        tail = jnp.zeros((S, NP - lay["kr"][0] - LANES), BF16)
        dh = jnp.concatenate([dgl, dqa, dcq, dckv, dh_c, dka, dva, dkr_raw, tail], axis=1)
        gbig["w_in"][l] = _col_chunks(_unperm_w_in(_mm(s["x0b"], dh, "tn", F32, "mm_dw_in"), lay, D))
        dx = _mm(dh, p["w_in"], "nt", F32, "mm_dx0", add=dx0_res)

        local = [gbig[k][l] for k in BIG]
        from_sibling = _swap_halves(local, f"rs_swap_halves_l{l}")
        partial = [_add_half(g, r, cc, f"rs_add_{k}_l{l}") for k, g, r in zip(BIG, local, from_sibling)]
        arrived = _scatter_to_chips([p16 for _, p16 in partial], f"rs_scatter_chips_l{l}")
        for k, (p32, _), arr in zip(BIG, partial, arrived):
            reduced[k] = _sum_into(p32, arr, chip, cc, l, L, reduced.get(k), f"rs_sum_{k}_l{l}")

    grad_x = dx.reshape(x.shape)
    g_big = dict(zip(BIG, _join_halves([reduced[k] for k in BIG], "rs_join_halves")))

    small_shapes = {k: (small_sharded_full[k] if k in SMALL_SHARDED else tuple(W[k].shape)) for k in SMALL}
    sv = jnp.concatenate([jnp.stack(gsmall[k]).reshape(-1) for k in SMALL])
    n_sv = sv.shape[0]
    sv = jnp.pad(sv, (0, -n_sv % (8 * LANES))).reshape(-1, LANES)
    sv = _all_reduce_small(sv).reshape(-1)
    g_small, o_ = {}, 0
    for k in SMALL:
        n = int(np.prod(small_shapes[k]))
        g = sv[o_:o_ + n].reshape(small_shapes[k])
        if k in SMALL_SHARDED:
            g = lax.dynamic_slice_in_dim(g, chip * W[k].shape[-1], W[k].shape[-1], axis=-1)
        g_small[k] = g
        o_ += n

    delta, new_m, new_v = {}, {}, {}
    for k in BIG:
        shp = shards[k]
        v2 = lambda t: t.reshape(shp[0] * shp[1], shp[2])
        d_, m_, v_ = _adamw(v2(W[k]), v2(g_big[k]), v2(Mo[k]), v2(Vo[k]), "adamw_" + k)
        delta[k], new_m[k], new_v[k] = d_.reshape(shp), m_.reshape(shp), v_.reshape(shp)
    pack = lambda t: jnp.concatenate([t[k].reshape(-1) for k in SMALL])
    n_small = sum(int(np.prod(W[k].shape)) for k in SMALL)
    pad2 = lambda t: jnp.pad(t, (0, -n_small % (8 * LANES))).reshape(-1, LANES)
    d_, m_, v_ = _adamw(pad2(pack(W)), pad2(pack(g_small)), pad2(pack(Mo)), pad2(pack(Vo)), "adamw_small")
    o_ = 0
    for k in SMALL:
        n = int(np.prod(W[k].shape))
        take = lambda t: t.reshape(-1)[o_:o_ + n].reshape(W[k].shape)
        delta[k], new_m[k], new_v[k] = take(d_), take(m_), take(v_)
        o_ += n

    grads = {**g_big, **g_small}
    return (loss, grad_x, *[grads[k] for k in WEIGHTS], *[delta[k] for k in WEIGHTS], *[new_m[k] for k in WEIGHTS],
            *[new_v[k] for k in WEIGHTS])
```

```python
import math

import jax
import jax.numpy as jnp
import numpy as np
from jax import lax
from jax.experimental import pallas as pl
from jax.experimental.pallas import tpu as pltpu

F32, BF16 = jnp.float32, jnp.bfloat16
SDS = jax.ShapeDtypeStruct
MESH = pl.DeviceIdType.MESH

SWA_Q_HEADS, SWA_KV_HEADS, SWA_HEAD_DIM, SWA_BLOCK = 16, 2, 64, 128
MLA_HEADS, MLA_NOPE, MLA_ROPE, MLA_V = 16, 128, 64, 128
MLA_Q_RANK, MLA_KV_RANK = 512, 512
ROPE_THETA = 10000.0
SGU_GROUPS, SGU_GROUP_DIM, SGU_CHUNK = 8, 128, 128
SGU_WIDTH = SGU_GROUPS * SGU_GROUP_DIM
A_Q = SWA_Q_HEADS * SWA_HEAD_DIM
A_KV = SWA_KV_HEADS * SWA_HEAD_DIM
N_BRANCHES = 3
DEPTH = 2
EPS = 1e-5
MASK_VALUE = -1e30
DN_ALPHA = (2 * DEPTH) ** 0.25
ADAM_LR, ADAM_B1, ADAM_B2, ADAM_EPS, ADAM_WD, ADAM_STEP = 0.001, 0.9, 0.999, 1e-08, 0.01, 10
N_CHIPS = 4

LANES = 128
VMEM_LIMIT = 48 * 1024 * 1024

BIG = ["w_in", "w_uq", "w_ukv", "w_proj_a", "w_proj_b", "w_proj_c", "w_o", "w_up", "w_down"]
ROW_SHARDED = {"w_proj_b", "w_o", "w_down"}
SMALL = ["b_gate", "sinks", "q_norm_g", "kv_norm_g", "sgu_ln_g", "sgu_ln_b", "sgu_w", "sgu_b", "ln1_g", "ln1_b",
         "conv_w", "conv_b", "ln2_g", "ln2_b"]
SMALL_SHARDED = {"b_gate", "conv_w"}
WEIGHTS = ["w_in", "b_gate", "sinks", "q_norm_g", "kv_norm_g", "w_uq", "w_ukv", "sgu_ln_g", "sgu_ln_b", "sgu_w", "sgu_b",
           "w_proj_a", "w_proj_b", "w_proj_c", "w_o", "ln1_g", "ln1_b", "w_up", "conv_w", "conv_b", "w_down", "ln2_g", "ln2_b"]


def _pcall(body, **kw):
    return pl.pallas_call(body, **kw)


def _params(sem=None):
    return pltpu.CompilerParams(dimension_semantics=sem, vmem_limit_bytes=VMEM_LIMIT)


def _tile(dim, pref, align=LANES):
    t = (min(pref, dim) // align) * align
    while t >= align:
        if dim % t == 0:
            return t
        t -= align
    return dim


def _mm(a, b, mode, out_dtype, name, add=None, tm=1024, tn=512, tk=2048, col_chunks=1):
    if mode == "nn":
        (M, K), (K2, N) = a.shape, b.shape
    elif mode == "nt":
        (M, K), (N, K2) = a.shape, b.shape
    else:
        (K, M), (K2, N) = a.shape, b.shape
    assert K == K2, (a.shape, b.shape, mode)
    assert N % col_chunks == 0
    tm, tn, tk = _tile(M, tm), _tile(N // col_chunks, tn), _tile(K, tk)
    assert (N // col_chunks) % tn == 0
    per_chunk = (N // col_chunks) // tn
    nk = K // tk
    if mode == "tn":
        a_spec = pl.BlockSpec((tk, tm), lambda i, j, k: (k, i))
    else:
        a_spec = pl.BlockSpec((tm, tk), lambda i, j, k: (i, k))
    if mode == "nt":
        b_spec = pl.BlockSpec((tn, tk), lambda i, j, k: (j, k))
    else:
        b_spec = pl.BlockSpec((tk, tn), lambda i, j, k: (k, j))
    dn = {"nn": (((1,), (0,)), ((), ())), "nt": (((1,), (1,)), ((), ())), "tn": (((0,), (0,)), ((), ()))}[mode]
    chunked = col_chunks > 1
    if chunked:
        assert add is None
        o_spec = pl.BlockSpec((1, tm, tn), lambda i, j, k: (lax.div(j, per_chunk), i, lax.rem(j, per_chunk)))
        out_shape = SDS((col_chunks, M, N // col_chunks), out_dtype)
    else:
        o_spec = pl.BlockSpec((tm, tn), lambda i, j, k: (i, j))
        out_shape = SDS((M, N), out_dtype)
    has_add = add is not None

    def body(*refs):
        if has_add:
            a_ref, b_ref, add_ref, o_ref, acc_ref = refs
        else:
            a_ref, b_ref, o_ref, acc_ref = refs
        k = pl.program_id(2)

        @pl.when(k == 0)
        def _():
            acc_ref[...] = jnp.zeros_like(acc_ref)

        acc_ref[...] += lax.dot_general(a_ref[...].astype(BF16), b_ref[...].astype(BF16), dn,
                                        preferred_element_type=F32)

        @pl.when(k == nk - 1)
        def _():
            r = acc_ref[...]
            if has_add:
                r = r + add_ref[...].astype(F32)
            if chunked:
                o_ref[0] = r.astype(o_ref.dtype)
            else:
                o_ref[...] = r.astype(o_ref.dtype)

    ins = [a, b] + ([add] if has_add else [])
    in_specs = [a_spec, b_spec] + ([o_spec] if has_add else [])
    return _pcall(body, name=name, out_shape=out_shape, grid=(M // tm, N // tn, nk),
                  in_specs=in_specs, out_specs=o_spec, scratch_shapes=[pltpu.VMEM((tm, tn), F32)],
                  compiler_params=_params(("parallel", "parallel", "arbitrary")))(*ins)


def _rowwise(fn, rows, params, row_outs, acc_outs, tm, name):
    n_rows = rows[0][0].shape[0]
    assert n_rows % tm == 0
    nr, npar, no = len(rows), len(params), len(row_outs)

    def body(*refs):
        i = pl.program_id(0)
        r, p = refs[:nr], refs[nr:nr + npar]
        o, acc = refs[nr + npar:nr + npar + no], refs[nr + npar + no:]
        outs, sums = fn(i, [x[...] for x in r], [x[...] for x in p])
        for ref, val in zip(o, outs, strict=True):
            ref[...] = val.astype(ref.dtype)
        if acc:
            @pl.when(i == 0)
            def _():
                for ref in acc:
                    ref[...] = jnp.zeros_like(ref)
            for ref, val in zip(acc, sums, strict=True):
                ref[...] += val.astype(F32)

    def full(shape):
        nd = len(shape)
        return pl.BlockSpec(tuple(shape), lambda i: (0,) * nd)

    in_specs = [pl.BlockSpec((tm, w), (lambda i, cb=cb: (i, cb))) for (_, w, cb) in rows] + [full(p.shape) for p in params]
    out_specs = [pl.BlockSpec((tm, w), lambda i: (i, 0)) for (w, _) in row_outs] + [full(s) for s in acc_outs]
    out_shape = [SDS((n_rows, w), dt) for (w, dt) in row_outs] + [SDS(tuple(s), F32) for s in acc_outs]
    res = _pcall(body, name=name, out_shape=out_shape, grid=(n_rows // tm,), in_specs=in_specs, out_specs=out_specs,
                 compiler_params=_params(("arbitrary",)))(*[r[0] for r in rows], *params)
    return list(res[:no]), list(res[no:])


def _whole(a):
    return (a, a.shape[1], 0)


def _gelu(x):
    return 0.5 * x * (1.0 + lax.erf(x * (1.0 / math.sqrt(2.0))))


def _layer_norm(x, g, b):
    mu = x.mean(-1, keepdims=True)
    var = jnp.mean(jnp.square(x - mu), -1, keepdims=True)
    return (x - mu) * lax.rsqrt(var + EPS) * g + b


def _rms_norm(x, g):
    return x * lax.rsqrt(jnp.mean(jnp.square(x), -1, keepdims=True) + EPS) * g


def _sgu_math(hu, hv, ln_g, ln_b, ws, bs):
    u = _gelu(hu.astype(F32))
    vn = _layer_norm(_gelu(hv.astype(F32)), ln_g, ln_b)
    r = lax.broadcasted_iota(jnp.int32, (SGU_CHUNK, SGU_CHUNK), 0)
    c = lax.broadcasted_iota(jnp.int32, (SGU_CHUNK, SGU_CHUNK), 1)
    outs = []
    for g in range(SGU_GROUPS):
        w = jnp.where(r >= c, ws[g], 0.0).astype(BF16)
        vg = vn[:, g * SGU_GROUP_DIM:(g + 1) * SGU_GROUP_DIM].astype(BF16)
        outs.append(jnp.dot(w, vg, preferred_element_type=F32) + bs[g])
    return u * jnp.concatenate(outs, axis=1)


def _sgu_fwd(h, cu, cv, ln_g, ln_b, w, b3):
    def fn(i, rows, ps):
        g_, b_, w_, b3_ = ps
        y = _sgu_math(rows[0], rows[1], g_, b_, [w_[g] for g in range(SGU_GROUPS)], [b3_[g] for g in range(SGU_GROUPS)])
        return [y], []
    (y,), _ = _rowwise(fn, [(h, SGU_WIDTH, cu), (h, SGU_WIDTH, cv)], [ln_g, ln_b, w, b3], [(SGU_WIDTH, BF16)], [],
                       SGU_CHUNK, "sgu_fwd")
    return y


def _sgu_bwd(h, cu, cv, dy, ln_g, ln_b, w, b3):
    nd = 2 * SGU_WIDTH

    def body(hu_ref, hv_ref, dy_ref, g_ref, b_ref, w_ref, b3_ref, dh_ref, dg_ref, db_ref, dw_ref, db3_ref):
        i = pl.program_id(0)

        @pl.when(i == 0)
        def _():
            dg_ref[...] = jnp.zeros_like(dg_ref)
            db_ref[...] = jnp.zeros_like(db_ref)
            dw_ref[...] = jnp.zeros_like(dw_ref)
            db3_ref[...] = jnp.zeros_like(db3_ref)

        ws = [w_ref[g] for g in range(SGU_GROUPS)]
        bs = [b3_ref[g] for g in range(SGU_GROUPS)]
        _, vjp = jax.vjp(_sgu_math, hu_ref[...], hv_ref[...], g_ref[...], b_ref[...], ws, bs)
        dhu, dhv, dg, db, dws, dbs = vjp(dy_ref[...].astype(F32))
        dh_ref[...] = jnp.concatenate([dhu, dhv], axis=1).astype(dh_ref.dtype)
        dg_ref[...] += dg
        db_ref[...] += db
        for g in range(SGU_GROUPS):
            dw_ref[g] += dws[g]
            db3_ref[g] += dbs[g]

    n = h.shape[0]
    blk = lambda cb: pl.BlockSpec((SGU_CHUNK, SGU_WIDTH), lambda i, cb=cb: (i, cb))
    full = lambda s: pl.BlockSpec(tuple(s), lambda i: (0,) * len(s))
    return _pcall(
        body, name="sgu_bwd", grid=(n // SGU_CHUNK,),
        out_shape=[SDS((n, nd), BF16), SDS(ln_g.shape, F32), SDS(ln_b.shape, F32), SDS(w.shape, F32), SDS(b3.shape, F32)],
        in_specs=[blk(cu), blk(cv), blk(0), full(ln_g.shape), full(ln_b.shape), full(w.shape), full(b3.shape)],
        out_specs=[pl.BlockSpec((SGU_CHUNK, nd), lambda i: (i, 0)), full(ln_g.shape), full(ln_b.shape), full(w.shape),
                   full(b3.shape)],
        compiler_params=_params(("arbitrary",)))(h, h, dy, ln_g, ln_b, w, b3)


def _swa_math(q, kp, kc, vp, vc, sinks, not_first):
    kw = jnp.concatenate([kp, kc], axis=0).astype(BF16)
    vw = jnp.concatenate([vp, vc], axis=0).astype(BF16)
    qb = q.astype(BF16)
    q_off = lax.broadcasted_iota(jnp.int32, (SWA_BLOCK, 2 * SWA_BLOCK), 0) + SWA_BLOCK
    k_off = lax.broadcasted_iota(jnp.int32, (SWA_BLOCK, 2 * SWA_BLOCK), 1)
    rel = q_off - k_off
    valid = (rel >= 0) & (rel < SWA_BLOCK) & (not_first | (k_off >= SWA_BLOCK))
    G = SWA_Q_HEADS // SWA_KV_HEADS
    outs = []
    for head in range(SWA_Q_HEADS):
        hk = head // G
        qh = qb[:, head * SWA_HEAD_DIM:(head + 1) * SWA_HEAD_DIM]
        kh = kw[:, hk * SWA_HEAD_DIM:(hk + 1) * SWA_HEAD_DIM]
        vh = vw[:, hk * SWA_HEAD_DIM:(hk + 1) * SWA_HEAD_DIM]
        s = lax.dot_general(qh, kh, (((1,), (1,)), ((), ())), preferred_element_type=F32) * (SWA_HEAD_DIM ** -0.5)
        s = jnp.where(valid, s, MASK_VALUE)
        sink = sinks[:, head:head + 1]
        m = jnp.maximum(s.max(-1, keepdims=True), sink)
        p = jnp.exp(s - m)
        p = (p / (p.sum(-1, keepdims=True) + jnp.exp(sink - m))).astype(BF16)
        outs.append(jnp.dot(p, vh, preferred_element_type=F32))
    return jnp.concatenate(outs, axis=1)


def _swa_fwd(h, cq, ck, cv, sinks):
    n = h.shape[0]
    nb = n // SWA_BLOCK

    def body(q_ref, kp_ref, kc_ref, vp_ref, vc_ref, s_ref, o_ref):
        i = pl.program_id(0)
        f = lambda x: x[...].astype(F32)
        o_ref[...] = _swa_math(f(q_ref), f(kp_ref), f(kc_ref), f(vp_ref), f(vc_ref), s_ref[...], i > 0).astype(o_ref.dtype)

    prev = lambda cb: pl.BlockSpec((SWA_BLOCK, A_KV), lambda i, cb=cb: (jnp.maximum(i - 1, 0), cb))
    cur = lambda cb: pl.BlockSpec((SWA_BLOCK, A_KV), lambda i, cb=cb: (i, cb))
    return _pcall(body, name="swa_fwd", grid=(nb,), out_shape=SDS((n, A_Q), BF16),
                  in_specs=[pl.BlockSpec((SWA_BLOCK, A_Q), lambda i: (i, cq)), prev(ck), cur(ck), prev(cv), cur(cv),
                            pl.BlockSpec((1, SWA_Q_HEADS), lambda i: (0, 0))],
                  out_specs=pl.BlockSpec((SWA_BLOCK, A_Q), lambda i: (i, 0)),
                  compiler_params=_params(("arbitrary",)))(h, h, h, h, h, sinks)


def _swa_bwd(h, cq, ck, cv, sinks, dy):
    n = h.shape[0]
    nb = n // SWA_BLOCK

    def body(q_ref, kp_ref, kc_ref, vp_ref, vc_ref, s_ref, dy_ref, dq_ref, dk_ref, dv_ref, ds_ref, ck_ref, cv_ref):
        r = pl.program_id(0)
        blk = nb - 1 - r

        @pl.when(r == 0)
        def _():
            ds_ref[...] = jnp.zeros_like(ds_ref)
            ck_ref[...] = jnp.zeros_like(ck_ref)
            cv_ref[...] = jnp.zeros_like(cv_ref)

        f = lambda x: x[...].astype(F32)
        not_first = blk > 0
        _, vjp = jax.vjp(lambda q, kp, kc, vp, vc, s: _swa_math(q, kp, kc, vp, vc, s, not_first),
                         f(q_ref), f(kp_ref), f(kc_ref), f(vp_ref), f(vc_ref), s_ref[...])
        dq, dkp, dkc, dvp, dvc, dsk = vjp(f(dy_ref))
        dq_ref[...] = dq.astype(dq_ref.dtype)
        dk_ref[...] = (dkc + ck_ref[...]).astype(dk_ref.dtype)
        dv_ref[...] = (dvc + cv_ref[...]).astype(dv_ref.dtype)
        ck_ref[...] = dkp
        cv_ref[...] = dvp
        ds_ref[...] += dsk

    rev = lambda i: nb - 1 - i
    prev = lambda cb: pl.BlockSpec((SWA_BLOCK, A_KV), lambda i, cb=cb: (jnp.maximum(rev(i) - 1, 0), cb))
    cur = lambda cb: pl.BlockSpec((SWA_BLOCK, A_KV), lambda i, cb=cb: (rev(i), cb))
    return _pcall(
        body, name="swa_bwd", grid=(nb,),
        out_shape=[SDS((n, A_Q), BF16), SDS((n, A_KV), BF16), SDS((n, A_KV), BF16), SDS((1, SWA_Q_HEADS), F32)],
        in_specs=[pl.BlockSpec((SWA_BLOCK, A_Q), lambda i: (rev(i), cq)), prev(ck), cur(ck), prev(cv), cur(cv),
                  pl.BlockSpec((1, SWA_Q_HEADS), lambda i: (0, 0)), pl.BlockSpec((SWA_BLOCK, A_Q), lambda i: (rev(i), 0))],
        out_specs=[pl.BlockSpec((SWA_BLOCK, A_Q), lambda i: (rev(i), 0)), pl.BlockSpec((SWA_BLOCK, A_KV), lambda i: (rev(i), 0)),
                   pl.BlockSpec((SWA_BLOCK, A_KV), lambda i: (rev(i), 0)), pl.BlockSpec((1, SWA_Q_HEADS), lambda i: (0, 0))],
        scratch_shapes=[pltpu.VMEM((SWA_BLOCK, A_KV), F32), pltpu.VMEM((SWA_BLOCK, A_KV), F32)],
        compiler_params=_params(("arbitrary",)))(h, h, h, h, h, sinks, dy)


def _rope(x, cos, sin, sign):
    w = x.shape[1]
    reps = w // LANES
    ct = jnp.tile(cos, (1, reps)) if reps > 1 else cos
    st = jnp.tile(sin, (1, reps)) if reps > 1 else sin
    fwd = pltpu.roll(x, MLA_ROPE // 2, axis=1)
    bwd = pltpu.roll(x, w - MLA_ROPE // 2, axis=1)
    lane = lax.broadcasted_iota(jnp.int32, x.shape, 1) % LANES
    rot = jnp.where(lane < MLA_ROPE // 2, -bwd, fwd)
    return x * ct + sign * (rot * st)


def _rope_call(a, wa, ca, b, wb, cb, cos, sin, sign, tm, name):
    def fn(i, rows, ps):
        xa, xb, c_, s_ = rows
        return [_rope(xa.astype(F32), c_, s_, sign), _rope(xb.astype(F32), c_, s_, sign)], []
    (ra, rb), _ = _rowwise(fn, [(a, wa, ca), (b, wb, cb), _whole(cos), _whole(sin)], [], [(wa, BF16), (wb, BF16)], [], tm, name)
    return ra, rb


MLA_SCALE = (MLA_NOPE + MLA_ROPE) ** -0.5
LOG2E = math.log2(math.e)


def _mla_scores(qn_ref, qr_ref, kn_ref, kr_ref, masked):
    q = jnp.concatenate([qn_ref[...], qr_ref[...]], axis=1)
    k = jnp.concatenate([kn_ref[...], kr_ref[...]], axis=1)
    s = lax.dot_general(q, k, (((1,), (1,)), ((), ())), preferred_element_type=F32)
    if masked:
        row = lax.broadcasted_iota(jnp.int32, s.shape, 0)
        col = lax.broadcasted_iota(jnp.int32, s.shape, 1)
        s = jnp.where(col <= row, s, MASK_VALUE)
    return s, q, k


def _causal_pairs(nq, by_query):
    if by_query:
        pairs = [(i, j) for i in range(nq) for j in range(i + 1)]
    else:
        pairs = [(i, j) for j in range(nq) for i in range(j, nq)]
    return (jnp.asarray(np.array([p[0] for p in pairs], np.int32)), jnp.asarray(np.array([p[1] for p in pairs], np.int32)),
            len(pairs))


def _mla_fwd(q_full, qr, kv, kr, T):
    n = q_full.shape[0]
    nq = n // T
    H = MLA_HEADS
    qi, kj, npairs = _causal_pairs(nq, True)

    def body(qi_ref, kj_ref, qn_ref, qr_ref, kn_ref, v_ref, kr_ref, y_ref, lse_ref, m_ref, l_ref, acc_ref):
        t = pl.program_id(1)
        i, j = qi_ref[t], kj_ref[t]

        @pl.when(j == 0)
        def _():
            m_ref[...] = jnp.full_like(m_ref, MASK_VALUE)
            l_ref[...] = jnp.zeros_like(l_ref)
            acc_ref[...] = jnp.zeros_like(acc_ref)

        def update(masked):
            s, _, _ = _mla_scores(qn_ref, qr_ref, kn_ref, kr_ref, masked)
            m_prev = m_ref[...]
            m_new = jnp.maximum(m_prev, s.max(-1, keepdims=True))
            p = jnp.exp2((s - m_new[:, :1]) * (MLA_SCALE * LOG2E))
            alpha = jnp.exp2((m_prev - m_new) * (MLA_SCALE * LOG2E))
            l_ref[...] = alpha * l_ref[...] + p.sum(-1, keepdims=True)
            acc_ref[...] = alpha * acc_ref[...] + jnp.dot(p.astype(BF16), v_ref[...], preferred_element_type=F32)
            m_ref[...] = m_new

        @pl.when(j < i)
        def _():
            update(False)

        @pl.when(j == i)
        def _():
            update(True)
            y_ref[...] = (acc_ref[...] / l_ref[...]).astype(y_ref.dtype)
            lse_ref[0] = m_ref[...] * (MLA_SCALE * LOG2E) + jnp.log2(l_ref[...])

    qspec = lambda off: pl.BlockSpec((T, LANES), lambda h, t, qi, kj, off=off: (qi[t], off + h))
    kspec = lambda off: pl.BlockSpec((T, LANES), lambda h, t, qi, kj, off=off: (kj[t], off + h))
    gs = pltpu.PrefetchScalarGridSpec(
        num_scalar_prefetch=2, grid=(H, npairs),
        in_specs=[qspec(0), qspec(0), kspec(0), kspec(H), pl.BlockSpec((T, LANES), lambda h, t, qi, kj: (kj[t], 0))],
        out_specs=[pl.BlockSpec((T, LANES), lambda h, t, qi, kj: (qi[t], h)),
                   pl.BlockSpec((1, T, LANES), lambda h, t, qi, kj: (h, qi[t], 0))],
        scratch_shapes=[pltpu.VMEM((T, LANES), F32)] * 3)
    return _pcall(body, name="mla_fwd", grid_spec=gs, out_shape=[SDS((n, H * MLA_V), BF16), SDS((H, n, LANES), F32)],
                  compiler_params=_params(("parallel", "arbitrary")))(qi, kj, q_full, qr, kv, kv, kr)


def _mla_delta(dy, y, T):
    n = y.shape[0]
    H = MLA_HEADS

    def body(dy_ref, y_ref, d_ref):
        d = jnp.sum(dy_ref[...].astype(F32) * y_ref[...].astype(F32), axis=-1, keepdims=True)
        d_ref[0] = jnp.broadcast_to(d, (T, LANES))

    spec = pl.BlockSpec((T, LANES), lambda h, i: (i, h))
    return _pcall(body, name="mla_delta", grid=(H, n // T), out_shape=SDS((H, n, LANES), F32), in_specs=[spec, spec],
                  out_specs=pl.BlockSpec((1, T, LANES), lambda h, i: (h, i, 0)),
                  compiler_params=_params(("parallel", "parallel")))(dy, y)


def _mla_bwd_dq(q_full, qr, kv, kr, dy, lse, delta, T):
    n = q_full.shape[0]
    nq = n // T
    H = MLA_HEADS

    qi, kj, npairs = _causal_pairs(nq, True)

    def body(qi_ref, kj_ref, qn_ref, qr_ref, kn_ref, v_ref, kr_ref, dy_ref, lse_ref, dl_ref, dqn_ref, dqr_ref, acc_ref):
        t = pl.program_id(1)
        i, j = qi_ref[t], kj_ref[t]

        @pl.when(j == 0)
        def _():
            acc_ref[...] = jnp.zeros_like(acc_ref)

        def update(masked):
            s, _, k = _mla_scores(qn_ref, qr_ref, kn_ref, kr_ref, masked)
            p = jnp.exp2(s * (MLA_SCALE * LOG2E) - lse_ref[0][:, :1])
            dp = lax.dot_general(dy_ref[...], v_ref[...], (((1,), (1,)), ((), ())), preferred_element_type=F32)
            ds = p * (dp - dl_ref[0][:, :1])
            acc_ref[...] += jnp.dot(ds.astype(BF16), k, preferred_element_type=F32)

        @pl.when(j < i)
        def _():
            update(False)

        @pl.when(j == i)
        def _():
            update(True)
            dqn_ref[...] = (acc_ref[:, :LANES] * MLA_SCALE).astype(dqn_ref.dtype)
            dqr_ref[...] = (acc_ref[:, LANES:] * MLA_SCALE).astype(dqr_ref.dtype)

    qspec = lambda off: pl.BlockSpec((T, LANES), lambda h, t, qi, kj, off=off: (qi[t], off + h))
    kspec = lambda off: pl.BlockSpec((T, LANES), lambda h, t, qi, kj, off=off: (kj[t], off + h))
    stat = pl.BlockSpec((1, T, LANES), lambda h, t, qi, kj: (h, qi[t], 0))
    out = pl.BlockSpec((T, LANES), lambda h, t, qi, kj: (qi[t], h))
    gs = pltpu.PrefetchScalarGridSpec(
        num_scalar_prefetch=2, grid=(H, npairs),
        in_specs=[qspec(0), qspec(0), kspec(0), kspec(H), pl.BlockSpec((T, LANES), lambda h, t, qi, kj: (kj[t], 0)),
                  qspec(0), stat, stat],
        out_specs=[out, out], scratch_shapes=[pltpu.VMEM((T, 2 * LANES), F32)])
    return _pcall(body, name="mla_bwd_dq", grid_spec=gs, out_shape=[SDS((n, H * LANES), BF16), SDS((n, H * LANES), BF16)],
                  compiler_params=_params(("parallel", "arbitrary")))(qi, kj, q_full, qr, kv, kv, kr, dy, lse, delta)


def _mla_bwd_dkv(q_full, qr, kv, kr, dy, lse, delta, T):
    n = q_full.shape[0]
    nq = n // T
    H = MLA_HEADS

    qi, kj, npairs = _causal_pairs(nq, False)

    def body(qi_ref, kj_ref, qn_ref, qr_ref, kn_ref, v_ref, kr_ref, dy_ref, lse_ref, dl_ref, dkn_ref, dv_ref, dkr_ref,
             dk_acc, dv_acc):
        t = pl.program_id(1)
        i, j = qi_ref[t], kj_ref[t]

        @pl.when(i == j)
        def _():
            dk_acc[...] = jnp.zeros_like(dk_acc)
            dv_acc[...] = jnp.zeros_like(dv_acc)

        def update(masked):
            s, q, _ = _mla_scores(qn_ref, qr_ref, kn_ref, kr_ref, masked)
            p = jnp.exp2(s * (MLA_SCALE * LOG2E) - lse_ref[0][:, :1])
            dy = dy_ref[...]
            dv_acc[...] += lax.dot_general(p.astype(BF16), dy, (((0,), (0,)), ((), ())), preferred_element_type=F32)
            dp = lax.dot_general(dy, v_ref[...], (((1,), (1,)), ((), ())), preferred_element_type=F32)
            ds = p * (dp - dl_ref[0][:, :1])
            dk_acc[...] += lax.dot_general(ds.astype(BF16), q, (((0,), (0,)), ((), ())), preferred_element_type=F32)

        @pl.when(i == j)
        def _():
            update(True)

        @pl.when(i > j)
        def _():
            update(False)

        @pl.when(i == nq - 1)
        def _():
            dkn_ref[...] = (dk_acc[:, :LANES] * MLA_SCALE).astype(dkn_ref.dtype)
            dv_ref[...] = dv_acc[...].astype(dv_ref.dtype)
            dkr_ref[0] = dk_acc[:, LANES:] * MLA_SCALE

    qspec = lambda off: pl.BlockSpec((T, LANES), lambda h, t, qi, kj, off=off: (qi[t], off + h))
    kspec = lambda off: pl.BlockSpec((T, LANES), lambda h, t, qi, kj, off=off: (kj[t], off + h))
    stat = pl.BlockSpec((1, T, LANES), lambda h, t, qi, kj: (h, qi[t], 0))
    out = pl.BlockSpec((T, LANES), lambda h, t, qi, kj: (kj[t], h))
    gs = pltpu.PrefetchScalarGridSpec(
        num_scalar_prefetch=2, grid=(H, npairs),
        in_specs=[qspec(0), qspec(0), kspec(0), kspec(H), pl.BlockSpec((T, LANES), lambda h, t, qi, kj: (kj[t], 0)),
                  qspec(0), stat, stat],
        out_specs=[out, out, pl.BlockSpec((1, T, LANES), lambda h, t, qi, kj: (h, kj[t], 0))],
        scratch_shapes=[pltpu.VMEM((T, 2 * LANES), F32), pltpu.VMEM((T, LANES), F32)])
    return _pcall(body, name="mla_bwd_dkv", grid_spec=gs,
                  out_shape=[SDS((n, H * LANES), BF16), SDS((n, H * LANES), BF16), SDS((H, n, LANES), F32)],
                  compiler_params=_params(("parallel", "arbitrary")))(qi, kj, q_full, qr, kv, kv, kr, dy, lse, delta)


def _sum_heads(a, tm):
    H, n, _ = a.shape

    def body(a_ref, o_ref):
        o_ref[...] = jnp.sum(a_ref[...], axis=0)

    return _pcall(body, name="mla_sum_heads", grid=(n // tm,), out_shape=SDS((n, LANES), F32),
                  in_specs=[pl.BlockSpec((H, tm, LANES), lambda i: (0, i, 0))],
                  out_specs=pl.BlockSpec((tm, LANES), lambda i: (i, 0)), compiler_params=_params(("parallel",)))(a)


def _shift_down(x, k):
    row = lax.broadcasted_iota(jnp.int32, x.shape, 0)
    return jnp.where(row >= k, pltpu.roll(x, k, axis=0), 0.0)


def _shift_up(x, k):
    n = x.shape[0]
    row = lax.broadcasted_iota(jnp.int32, x.shape, 0)
    return jnp.where(row < n - k, pltpu.roll(x, n - k, axis=0), 0.0)


def _conv_fwd(up, w, b):
    n, c = up.shape

    def body(u_ref, w_ref, b_ref, o_ref):
        u = u_ref[...].astype(F32)
        wv = w_ref[...]
        o_ref[...] = (b_ref[...] + wv[0:1] * _shift_down(u, 2) + wv[1:2] * _shift_down(u, 1) + wv[2:3] * u).astype(o_ref.dtype)

    return _pcall(body, name="conv_fwd", grid=(c // LANES,), out_shape=SDS((n, c), BF16),
                  in_specs=[pl.BlockSpec((n, LANES), lambda j: (0, j)), pl.BlockSpec((3, LANES), lambda j: (0, j)),
                            pl.BlockSpec((1, LANES), lambda j: (0, j))],
                  out_specs=pl.BlockSpec((n, LANES), lambda j: (0, j)), compiler_params=_params(("parallel",)))(up, w, b)


def _conv_bwd(up, dc, w):
    n, c = up.shape

    def body(u_ref, d_ref, w_ref, du_ref, dw_ref, db_ref):
        u = u_ref[...].astype(F32)
        d = d_ref[...].astype(F32)
        wv = w_ref[...]
        du_ref[...] = (wv[2:3] * d + wv[1:2] * _shift_up(d, 1) + wv[0:1] * _shift_up(d, 2)).astype(du_ref.dtype)
        dw_ref[0:1, :] = jnp.sum(d * _shift_down(u, 2), axis=0, keepdims=True)
        dw_ref[1:2, :] = jnp.sum(d * _shift_down(u, 1), axis=0, keepdims=True)
        dw_ref[2:3, :] = jnp.sum(d * u, axis=0, keepdims=True)
        db_ref[...] = jnp.sum(d, axis=0, keepdims=True)

    col = pl.BlockSpec((n, LANES), lambda j: (0, j))
    return _pcall(body, name="conv_bwd", grid=(c // LANES,),
                  out_shape=[SDS((n, c), BF16), SDS((3, c), F32), SDS((1, c), F32)],
                  in_specs=[col, col, pl.BlockSpec((3, LANES), lambda j: (0, j))],
                  out_specs=[col, pl.BlockSpec((3, LANES), lambda j: (0, j)), pl.BlockSpec((1, LANES), lambda j: (0, j))],
                  compiler_params=_params(("parallel",)))(up, dc, w)


def _adamw_math(w, g, m, v):
    m = ADAM_B1 * m + (1.0 - ADAM_B1) * g
    v = ADAM_B2 * v + (1.0 - ADAM_B2) * jnp.square(g)
    m_hat = m / (1.0 - ADAM_B1 ** ADAM_STEP)
    v_hat = v / (1.0 - ADAM_B2 ** ADAM_STEP)
    delta = -ADAM_LR * (m_hat / (jnp.sqrt(v_hat) + ADAM_EPS) + ADAM_WD * w)
    return delta, m, v


def _adamw(w, g, m, v, name):
    r, c = w.shape
    tr = r
    budget = max(8, (1 << 20) // (4 * c))
    t = (min(budget, r) // 8) * 8
    while t >= 8:
        if r % t == 0:
            tr = t
            break
        t -= 8

    def fn(i, rows, ps):
        return list(_adamw_math(*rows)), []
    (d, nm, nv), _ = _rowwise(fn, [_whole(w), _whole(g), _whole(m), _whole(v)], [], [(c, F32)] * 3, [], tr, name)
    return d, nm, nv


def _coords():
    return lax.axis_index("x"), lax.axis_index("y"), lax.axis_index("c")


def _other_chips(x, y):
    return [(1 - x, y), (x, 1 - y), (1 - x, 1 - y)]


HBM_SPEC = pl.BlockSpec(memory_space=pltpu.HBM)


def _half(c, rows):
    return pl.ds(pl.multiple_of(c * rows, 16), rows)


def _rows_tile(rows, cols, budget_bytes=2 << 20, align=16):
    t = (min(max(align, budget_bytes // (4 * cols)), rows) // align) * align
    while t >= align:
        if rows % t == 0:
            return t
        t -= align
    return rows


def _scalar(v):
    return jnp.reshape(jnp.asarray(v, jnp.int32), (1,))


def _cast_into_slot(w3, layer, slot, name):
    _, R, C = w3.shape
    tr = _rows_tile(R, C)

    def body(s_ref, w_ref, o_ref):
        o_ref[0] = w_ref[0].astype(BF16)

    gs = pltpu.PrefetchScalarGridSpec(
        num_scalar_prefetch=1, grid=(R // tr,),
        in_specs=[pl.BlockSpec((1, tr, C), lambda i, s: (layer, i, 0))],
        out_specs=pl.BlockSpec((1, tr, C), lambda i, s: (s[0], i, 0)))
    return _pcall(body, name=name, grid_spec=gs, out_shape=SDS((N_CHIPS, R, C), BF16),
                  compiler_params=_params(("arbitrary",)))(_scalar(slot), w3)


def _gather_layer(bufs, name):
    n = len(bufs)

    def body(*refs):
        outs, send_sems, recv_sems = refs[n:2 * n], refs[2 * n], refs[2 * n + 1]
        x, y, c = _coords()
        me = 2 * x + y
        sib = (x, y, 1 - c)
        chips = _other_chips(x, y)

        def copy(w, k, chip_idx, half, to):
            blk = outs[w].at[chip_idx, _half(half, outs[w].shape[1] // 2), :]
            return pltpu.make_async_remote_copy(src_ref=blk, dst_ref=blk, send_sem=send_sems.at[6 * w + k],
                                                recv_sem=recv_sems.at[6 * w + k], device_id=to, device_id_type=MESH)

        first = [copy(w, j, me, c, (px, py, c)) for w in range(n) for j, (px, py) in enumerate(chips)]
        for cp in first:
            cp.start()
        passed = []
        for w in range(n):
            for j, (px, py) in enumerate(chips):
                copy(w, j, 2 * px + py, c, (px, py, c)).wait_recv()
                fw = copy(w, 3 + j, 2 * px + py, c, sib)
                fw.start()
                passed.append(fw)
        for w in range(n):
            for j, (px, py) in enumerate(chips):
                copy(w, 3 + j, 2 * px + py, 1 - c, sib).wait_recv()
        for cp in first + passed:
            cp.wait_send()

    return _pcall(body, name=name, out_shape=[SDS(b.shape, b.dtype) for b in bufs],
                  in_specs=[HBM_SPEC] * n, out_specs=[HBM_SPEC] * n, input_output_aliases={i: i for i in range(n)},
                  scratch_shapes=[pltpu.SemaphoreType.DMA((6 * n,)), pltpu.SemaphoreType.DMA((6 * n,))])(*bufs)


def _swap_halves(gs_, name):
    n = len(gs_)

    def body(*refs):
        ins, outs, send_sems, recv_sems = refs[:n], refs[n:2 * n], refs[2 * n], refs[2 * n + 1]
        x, y, c = _coords()
        cps = []
        for w in range(n):
            rows = ins[w].shape[1] // 2
            cps.append(pltpu.make_async_remote_copy(src_ref=ins[w].at[:, _half(1 - c, rows), :], dst_ref=outs[w],
                                                    send_sem=send_sems.at[w], recv_sem=recv_sems.at[w],
                                                    device_id=(x, y, 1 - c), device_id_type=MESH))
        for cp in cps:
            cp.start()
        for cp in cps:
            cp.wait()

    return _pcall(body, name=name, out_shape=[SDS((N_CHIPS, g.shape[1] // 2, g.shape[2]), g.dtype) for g in gs_],
                  in_specs=[HBM_SPEC] * n, out_specs=[HBM_SPEC] * n,
                  scratch_shapes=[pltpu.SemaphoreType.DMA((n,)), pltpu.SemaphoreType.DMA((n,))])(*gs_)


def _scatter_to_chips(ps, name):
    n = len(ps)

    def body(*refs):
        ins, outs, send_sems, recv_sems = refs[:n], refs[n:2 * n], refs[2 * n], refs[2 * n + 1]
        x, y, c = _coords()
        chips = _other_chips(x, y)
        cps = [pltpu.make_async_remote_copy(src_ref=ins[w].at[2 * px + py], dst_ref=outs[w].at[j], send_sem=send_sems.at[3 * w + j],
                                            recv_sem=recv_sems.at[3 * w + j], device_id=(px, py, c), device_id_type=MESH)
               for w in range(n) for j, (px, py) in enumerate(chips)]
        for cp in cps:
            cp.start()
        for cp in cps:
            cp.wait()

    return _pcall(body, name=name, out_shape=[SDS((3,) + p.shape[1:], p.dtype) for p in ps],
                  in_specs=[HBM_SPEC] * n, out_specs=[HBM_SPEC] * n,
                  scratch_shapes=[pltpu.SemaphoreType.DMA((3 * n,)), pltpu.SemaphoreType.DMA((3 * n,))])(*ps)


def _join_halves(bufs, name):
    n = len(bufs)
    L = bufs[0].shape[0]

    def body(*refs):
        outs, send_sems, recv_sems = refs[n:2 * n], refs[2 * n], refs[2 * n + 1]
        x, y, c = _coords()

        def copy(w, l, half):
            blk = outs[w].at[l, _half(half, outs[w].shape[1] // 2), :]
            return pltpu.make_async_remote_copy(src_ref=blk, dst_ref=blk, send_sem=send_sems.at[L * w + l],
                                                recv_sem=recv_sems.at[L * w + l], device_id=(x, y, 1 - c), device_id_type=MESH)

        cps = [copy(w, l, c) for w in range(n) for l in range(L)]
        for cp in cps:
            cp.start()
        for w in range(n):
            for l in range(L):
                copy(w, l, 1 - c).wait_recv()
        for cp in cps:
            cp.wait_send()

    return _pcall(body, name=name, out_shape=[SDS(b.shape, b.dtype) for b in bufs],
                  in_specs=[HBM_SPEC] * n, out_specs=[HBM_SPEC] * n, input_output_aliases={i: i for i in range(n)},
                  scratch_shapes=[pltpu.SemaphoreType.DMA((L * n,)), pltpu.SemaphoreType.DMA((L * n,))])(*bufs)


def _all_reduce_small(v):
    n = v.shape[0]

    def body(v_ref, out_ref, slots, send_sems, recv_sems):
        x, y, c = _coords()
        me = 4 * x + 2 * y + c
        cps = []
        for r in range(1, 8):
            t = (me + r) % 8
            cp = pltpu.make_async_remote_copy(src_ref=v_ref, dst_ref=slots.at[me], send_sem=send_sems.at[r - 1],
                                              recv_sem=recv_sems.at[me], device_id=(t // 4, (t // 2) % 2, t % 2),
                                              device_id_type=MESH)
            cp.start()
            cps.append(cp)
        slots[me] = v_ref[...]
        for r in range(1, 8):
            s = (me + r) % 8
            pltpu.make_async_remote_copy(src_ref=v_ref, dst_ref=slots.at[s], send_sem=send_sems.at[r - 1],
                                         recv_sem=recv_sems.at[s], device_id=(x, y, c), device_id_type=MESH).wait_recv()
        for cp in cps:
            cp.wait_send()
        acc = slots[0]
        for d in range(1, 8):
            acc = acc + slots[d]
        out_ref[...] = acc

    return _pcall(body, name="all_reduce_small", out_shape=SDS((n, LANES), F32),
                  in_specs=[pl.BlockSpec(memory_space=pltpu.VMEM)], out_specs=pl.BlockSpec(memory_space=pltpu.VMEM),
                  scratch_shapes=[pltpu.VMEM((8, n, LANES), F32), pltpu.SemaphoreType.DMA((7,)), pltpu.SemaphoreType.DMA((8,))],
                  compiler_params=pltpu.CompilerParams(vmem_limit_bytes=VMEM_LIMIT))(v)


def _add_half(g, recv, c, name):
    _, R, C = g.shape
    rows = R // 2
    tr = _rows_tile(rows, C, 1 << 20)
    nb = rows // tr

    def body(s_ref, g_ref, r_ref, o32_ref, o16_ref):
        s = g_ref[...] + r_ref[...]
        o32_ref[...] = s
        o16_ref[...] = s.astype(BF16)

    blk = lambda k, i, s: (k, i, 0)
    gs = pltpu.PrefetchScalarGridSpec(
        num_scalar_prefetch=1, grid=(N_CHIPS, nb),
        in_specs=[pl.BlockSpec((1, tr, C), lambda k, i, s: (k, s[0] * nb + i, 0)), pl.BlockSpec((1, tr, C), blk)],
        out_specs=[pl.BlockSpec((1, tr, C), blk), pl.BlockSpec((1, tr, C), blk)])
    return _pcall(body, name=name, grid_spec=gs, out_shape=[SDS((N_CHIPS, rows, C), F32), SDS((N_CHIPS, rows, C), BF16)],
                  compiler_params=_params(("arbitrary", "arbitrary")))(_scalar(c), g, recv)


def _sum_into(p32, arrived, chip, c, layer, n_layers, prev, name):
    _, rows, C = p32.shape
    tr = _rows_tile(rows, C, 1 << 20)
    nb = rows // tr

    def body(chip_ref, c_ref, p_ref, a_ref, *rest):
        o_ref = rest[-1]
        o_ref[0] = ((p_ref[0] + a_ref[0].astype(F32)) + a_ref[1].astype(F32)) + a_ref[2].astype(F32)

    in_specs = [pl.BlockSpec((1, tr, C), lambda i, chip_ref, c_ref: (chip_ref[0], i, 0)),
                pl.BlockSpec((3, tr, C), lambda i, chip_ref, c_ref: (0, i, 0))]
    ins = [p32, arrived]
    aliases = {}
    if prev is not None:
        in_specs.append(pl.BlockSpec(memory_space=pl.ANY))
        ins.append(prev)
        aliases = {4: 0}
    gs = pltpu.PrefetchScalarGridSpec(
        num_scalar_prefetch=2, grid=(nb,), in_specs=in_specs,
        out_specs=pl.BlockSpec((1, tr, C), lambda i, chip_ref, c_ref: (layer, c_ref[0] * nb + i, 0)))
    return _pcall(body, name=name, grid_spec=gs, out_shape=SDS((n_layers, 2 * rows, C), F32), input_output_aliases=aliases,
                  compiler_params=_params(("arbitrary",)))(_scalar(chip), _scalar(c), *ins)


def _h_layout(D):
    G = N_BRANCHES * D
    off, o = {}, 0
    for name, w in [("g", G), ("qa", A_Q), ("cq", MLA_Q_RANK), ("ckv", MLA_KV_RANK), ("hu", SGU_WIDTH), ("hv", SGU_WIDTH),
                    ("ka", A_KV), ("va", A_KV), ("kr", LANES)]:
        assert o % w == 0, (name, o, w)
        off[name] = (o, w)
        o += w
    off["total"] = -(-o // 512) * 512
    return off


def _perm_w_in(w, lay):
    s = np.cumsum([0, A_Q, A_KV, A_KV, MLA_Q_RANK, MLA_KV_RANK, MLA_ROPE, SGU_WIDTH, SGU_WIDTH])
    qa, ka, va, cq, ckv, kr, hu, hv = [w[:, s[i]:s[i + 1]] for i in range(8)]
    g = w[:, s[8]:]
    pad = jnp.zeros((w.shape[0], lay["total"] - lay["kr"][0] - MLA_ROPE), w.dtype)
    return jnp.concatenate([g, qa, cq, ckv, hu, hv, ka, va, kr, pad], axis=1)


def _unperm_w_in(wp, lay, D):
    take = lambda n, width=None: wp[:, lay[n][0]:lay[n][0] + (width or lay[n][1])]
    return jnp.concatenate([take("qa"), take("ka"), take("va"), take("cq"), take("ckv"), take("kr", MLA_ROPE), take("hu"),
                            take("hv"), take("g")], axis=1)


def _perm_w_uq(w):
    r = w.shape[0]
    w3 = w.reshape(r, MLA_HEADS, MLA_NOPE + MLA_ROPE)
    nope = w3[:, :, :MLA_NOPE].reshape(r, MLA_HEADS * MLA_NOPE)
    rope = jnp.pad(w3[:, :, MLA_NOPE:], ((0, 0), (0, 0), (0, LANES - MLA_ROPE))).reshape(r, MLA_HEADS * LANES)
    return jnp.concatenate([nope, rope], axis=1)


def _unperm_w_uq(wp):
    r = wp.shape[0]
    nope = wp[:, :MLA_HEADS * MLA_NOPE].reshape(r, MLA_HEADS, MLA_NOPE)
    rope = wp[:, MLA_HEADS * MLA_NOPE:].reshape(r, MLA_HEADS, LANES)[:, :, :MLA_ROPE]
    return jnp.concatenate([nope, rope], axis=2).reshape(r, MLA_HEADS * (MLA_NOPE + MLA_ROPE))


def _perm_w_ukv(w):
    r = w.shape[0]
    w3 = w.reshape(r, MLA_HEADS, MLA_NOPE + MLA_V)
    return jnp.concatenate([w3[:, :, :MLA_NOPE].reshape(r, -1), w3[:, :, MLA_NOPE:].reshape(r, -1)], axis=1)


def _unperm_w_ukv(wp):
    r = wp.shape[0]
    k = wp[:, :MLA_HEADS * MLA_NOPE].reshape(r, MLA_HEADS, MLA_NOPE)
    v = wp[:, MLA_HEADS * MLA_NOPE:].reshape(r, MLA_HEADS, MLA_V)
    return jnp.concatenate([k, v], axis=2).reshape(r, -1)


def _col_chunks(g):
    r, c4 = g.shape
    return jnp.transpose(g.reshape(r, N_CHIPS, c4 // N_CHIPS), (1, 0, 2))


def kernel(x, positions, w_in, b_gate, sinks, q_norm_g, kv_norm_g, w_uq, w_ukv, sgu_ln_g, sgu_ln_b, sgu_w, sgu_b, w_proj_a, w_proj_b, w_proj_c, w_o, ln1_g, ln1_b, w_up, conv_w, conv_b, w_down, ln2_g, ln2_b, loss_target, m_w_in, m_b_gate, m_sinks, m_q_norm_g, m_kv_norm_g, m_w_uq, m_w_ukv, m_sgu_ln_g, m_sgu_ln_b, m_sgu_w, m_sgu_b, m_w_proj_a, m_w_proj_b, m_w_proj_c, m_w_o, m_ln1_g, m_ln1_b, m_w_up, m_conv_w, m_conv_b, m_w_down, m_ln2_g, m_ln2_b, v_w_in, v_b_gate, v_sinks, v_q_norm_g, v_kv_norm_g, v_w_uq, v_w_ukv, v_sgu_ln_g, v_sgu_ln_b, v_sgu_w, v_sgu_b, v_w_proj_a, v_w_proj_b, v_w_proj_c, v_w_o, v_ln1_g, v_ln1_b, v_w_up, v_conv_w, v_conv_b, v_w_down, v_ln2_g, v_ln2_b):
    a = locals()
    W = {k: a[k] for k in WEIGHTS}
    Mo = {k: a["m_" + k] for k in WEIGHTS}
    Vo = {k: a["v_" + k] for k in WEIGHTS}
    S, D = x.shape[1], x.shape[2]
    FF2 = w_up.shape[2] * N_CHIPS
    FF = FF2 // 2
    L = DEPTH
    lay = _h_layout(D)
    NP = lay["total"]
    cx, cy, cc = _coords()
    chip = 2 * cx + cy
    T = _tile(S, 512)
    TM = _tile(S, 256, 16)
    TMW = _tile(S, 64, 16)

    shards = {k: tuple(W[k].shape) for k in BIG}
    full = {k: [None] * L for k in BIG}
    for l in range(L):
        bufs = [_cast_into_slot(W[k], l, chip, f"cast_{k}_l{l}") for k in BIG]
        for k, g in zip(BIG, _gather_layer(bufs, f"gather_weights_l{l}")):
            _, r, c_ = shards[k]
            full[k][l] = g.reshape(N_CHIPS * r, c_) if k in ROW_SHARDED else jnp.transpose(g, (1, 0, 2)).reshape(r, N_CHIPS * c_)

    small_sharded_full = {k: tuple(W[k].shape[:-1]) + (W[k].shape[-1] * N_CHIPS,) for k in SMALL_SHARDED}
    placed = []
    for k in ("b_gate", "conv_w"):
        z = jnp.zeros(small_sharded_full[k], F32)
        z = lax.dynamic_update_slice_in_dim(z, W[k], chip * W[k].shape[-1], axis=-1)
        placed.append(jnp.where(cc == 0, z, 0.0).reshape(-1))
    pv = jnp.concatenate(placed)
    n_pv = pv.shape[0]
    pv = jnp.pad(pv, (0, -n_pv % (8 * LANES))).reshape(-1, LANES)
    pv = _all_reduce_small(pv).reshape(-1)
    nb_ = int(np.prod(small_sharded_full["b_gate"]))
    b_gate_full = pv[:nb_].reshape(small_sharded_full["b_gate"])
    conv_w_full = pv[nb_:n_pv].reshape(small_sharded_full["conv_w"])

    inv_freq = ROPE_THETA ** (-jnp.arange(0, MLA_ROPE, 2, dtype=F32) / MLA_ROPE)
    ang = positions[0].astype(F32)[:, None] * inv_freq
    cos, sin = jnp.cos(ang), jnp.sin(ang)
    cos_t = jnp.concatenate([cos, cos, jnp.ones((S, LANES - MLA_ROPE), F32)], axis=1)
    sin_t = jnp.concatenate([sin, sin, jnp.zeros((S, LANES - MLA_ROPE), F32)], axis=1)

    row = lambda v: v.reshape(1, -1)
    cb = lambda name: lay[name][0] // lay[name][1]

    xs = x[0]
    saved = []
    for l in range(L):
        p = dict(
            w_in=_perm_w_in(full["w_in"][l], lay), w_uq=_perm_w_uq(full["w_uq"][l]), w_ukv=_perm_w_ukv(full["w_ukv"][l]),
            w_pa=full["w_proj_a"][l], w_pb=full["w_proj_b"][l], w_pc=full["w_proj_c"][l], w_o=full["w_o"][l],
            w_up=full["w_up"][l], w_down=full["w_down"][l],
            sinks=row(sinks[l]), qg=row(q_norm_g[l]), kvg=row(kv_norm_g[l]), sg=row(sgu_ln_g[l]), sb=row(sgu_ln_b[l]),
            sw=sgu_w[l], sb3=sgu_b[l].reshape(SGU_GROUPS, SGU_CHUNK, 1),
            bg=b_gate_full[l], l1g=row(ln1_g[l]), l1b=row(ln1_b[l]), cw=conv_w_full[l], cbias=row(conv_b[l]),
            l2g=row(ln2_g[l]), l2b=row(ln2_b[l]))
        if l == 0:
            def fn_cast(i, rows, ps):
                return [rows[0]], []
            (xb,), _ = _rowwise(fn_cast, [_whole(xs)], [], [(D, BF16)], [], TM, "cast_x")
        h = _mm(xb, p["w_in"], "nn", BF16, "mm_h")
        y_a = _swa_fwd(h, cb("qa"), cb("ka"), cb("va"), p["sinks"])
        def fn_rms(i, rows, ps):
            return [_rms_norm(rows[0].astype(F32), ps[0]), _rms_norm(rows[1].astype(F32), ps[1])], []
        (cqn, ckvn), _ = _rowwise(fn_rms, [(h, MLA_Q_RANK, cb("cq")), (h, MLA_KV_RANK, cb("ckv"))], [p["qg"], p["kvg"]],
                                  [(MLA_Q_RANK, BF16), (MLA_KV_RANK, BF16)], [], TM, "mla_rms")
        q_full = _mm(cqn, p["w_uq"], "nn", BF16, "mm_q")
        kv = _mm(ckvn, p["w_ukv"], "nn", BF16, "mm_kv")
        qr, kr = _rope_call(q_full, MLA_HEADS * LANES, 1, h, LANES, cb("kr"), cos_t, sin_t, 1.0, TM, "rope_fwd")
        y_b, lse = _mla_fwd(q_full, qr, kv, kr, T)
        y_c = _sgu_fwd(h, cb("hu"), cb("hv"), p["sg"], p["sb"], p["sw"], p["sb3"])
        pa = _mm(y_a, p["w_pa"], "nn", F32, "mm_pa")
        pb = _mm(y_b, p["w_pb"], "nn", F32, "mm_pb")
        pc = _mm(y_c, p["w_pc"], "nn", F32, "mm_pc")

        def merge_math(pa_, pb_, pc_, g_, b0, b1, b2):
            out = 0.0
            for br, (pp, bb) in enumerate(zip((pa_, pb_, pc_), (b0, b1, b2))):
                gate = jax.nn.sigmoid(g_[:, br * D:(br + 1) * D].astype(F32) + bb)
                out = out + gate * pp
            return out

        def fn_merge(i, rows, ps):
            bgv = ps[0]
            return [merge_math(rows[0], rows[1], rows[2], rows[3], bgv[0:1], bgv[1:2], bgv[2:3])], []
        (merged,), _ = _rowwise(fn_merge, [_whole(pa), _whole(pb), _whole(pc), (h, N_BRANCHES * D, cb("g"))], [p["bg"]],
                                [(D, BF16)], [], TMW, "merge_fwd")
        o = _mm(merged, p["w_o"], "nn", F32, "mm_o")

        def ln_res_math(x_, o_, g_, b_):
            return _layer_norm(DN_ALPHA * x_ + o_, g_, b_)

        def fn_ln(i, rows, ps):
            y = ln_res_math(rows[0], rows[1], ps[0], ps[1])
            return [y, y], []
        (x1, x1b), _ = _rowwise(fn_ln, [_whole(xs), _whole(o)], [p["l1g"], p["l1b"]], [(D, F32), (D, BF16)], [], TM, "ln1_fwd")
        up = _mm(x1b, p["w_up"], "nn", BF16, "mm_up")
        cv_ = _conv_fwd(up, p["cw"], p["cbias"])

        def glu_math(cg, cvv):
            return jax.nn.silu(cg.astype(F32)) * cvv.astype(F32)

        def fn_glu(i, rows, ps):
            return [glu_math(rows[0], rows[1])], []
        (act,), _ = _rowwise(fn_glu, [(cv_, FF, 0), (cv_, FF, 1)], [], [(FF, BF16)], [], TMW, "glu_fwd")
        dn = _mm(act, p["w_down"], "nn", F32, "mm_down")
        (x2, x2b), _ = _rowwise(fn_ln, [_whole(x1), _whole(dn)], [p["l2g"], p["l2b"]], [(D, F32), (D, BF16)], [], TM, "ln2_fwd")
        saved.append(dict(p=p, x0=xs, x0b=xb, h=h, y_a=y_a, cqn=cqn, ckvn=ckvn, q_full=q_full, kv=kv, qr=qr, kr=kr, y_b=y_b,
                          lse=lse, y_c=y_c, pa=pa, pb=pb, pc=pc, merged=merged, o=o, x1=x1, x1b=x1b, up=up, cv=cv_, act=act,
                          dn=dn))
        xs, xb = x2, x2b

    def fn_loss(i, rows, ps):
        diff = rows[0] - rows[1]
        part = jnp.sum(jnp.mean(jnp.square(diff), axis=-1, keepdims=True), axis=0, keepdims=True)
        return [diff * (1.0 / D)], [jnp.broadcast_to(part, (8, LANES))]
    (dx,), (loss_acc,) = _rowwise(fn_loss, [_whole(xs), _whole(loss_target[0])], [], [(D, F32)], [(8, LANES)], TM, "loss")
    loss = lax.psum(0.5 * loss_acc[0, 0], ("x", "y", "c"))

    gbig = {k: [None] * L for k in BIG}
    gsmall = {k: [None] * L for k in SMALL}
    reduced = {}
    for l in reversed(range(L)):
        s = saved[l]
        p = s["p"]

        def fn_ln_bwd(i, rows, ps):
            _, vjp = jax.vjp(ln_res_math, rows[0], rows[1], ps[0], ps[1])
            dx_, do_, dg_, db_ = vjp(rows[2])
            return [dx_, do_], [dg_, db_]
        (dx1_res, ddn), (g_l2g, g_l2b) = _rowwise(fn_ln_bwd, [_whole(s["x1"]), _whole(s["dn"]), _whole(dx)], [p["l2g"], p["l2b"]],
                                                  [(D, F32), (D, BF16)], [(1, D), (1, D)], TM, "ln2_bwd")
        gsmall["ln2_g"][l], gsmall["ln2_b"][l] = g_l2g, g_l2b
        row_chunks = lambda g: g.reshape(N_CHIPS, g.shape[0] // N_CHIPS, g.shape[1])
        gbig["w_down"][l] = row_chunks(_mm(s["act"], ddn, "tn", F32, "mm_dw_down"))
        dact = _mm(ddn, p["w_down"], "nt", BF16, "mm_dact")

        def fn_glu_bwd(i, rows, ps):
            _, vjp = jax.vjp(glu_math, rows[0], rows[1])
            dcg, dcv = vjp(rows[2].astype(F32))
            return [jnp.concatenate([dcg, dcv], axis=1)], []
        (dc,), _ = _rowwise(fn_glu_bwd, [(s["cv"], FF, 0), (s["cv"], FF, 1), _whole(dact)], [], [(FF2, BF16)], [], TMW, "glu_bwd")
        dup, g_cw, g_cb = _conv_bwd(s["up"], dc, p["cw"])
        gsmall["conv_w"][l], gsmall["conv_b"][l] = g_cw, g_cb
        gbig["w_up"][l] = _mm(s["x1b"], dup, "tn", F32, "mm_dw_up", col_chunks=N_CHIPS, tm=2048)
        dx1 = _mm(dup, p["w_up"], "nt", F32, "mm_dx1", add=dx1_res)
        (dx0_res, do_), (g_l1g, g_l1b) = _rowwise(fn_ln_bwd, [_whole(s["x0"]), _whole(s["o"]), _whole(dx1)], [p["l1g"], p["l1b"]],
                                                  [(D, F32), (D, BF16)], [(1, D), (1, D)], TM, "ln1_bwd")
        gsmall["ln1_g"][l], gsmall["ln1_b"][l] = g_l1g, g_l1b
        gbig["w_o"][l] = row_chunks(_mm(s["merged"], do_, "tn", F32, "mm_dw_o"))
        dmerged = _mm(do_, p["w_o"], "nt", F32, "mm_dmerged")

        def fn_merge_bwd(i, rows, ps):
            bgv = ps[0]
            _, vjp = jax.vjp(merge_math, rows[0], rows[1], rows[2], rows[3], bgv[0:1], bgv[1:2], bgv[2:3])
            dpa, dpb, dpc, dg_, db0, db1, db2 = vjp(rows[4])
            return [dpa, dpb, dpc, dg_], [db0, db1, db2]
        (dpa, dpb, dpc, dgl), (db0, db1, db2) = _rowwise(
            fn_merge_bwd, [_whole(s["pa"]), _whole(s["pb"]), _whole(s["pc"]), (s["h"], N_BRANCHES * D, cb("g")), _whole(dmerged)],
            [p["bg"]], [(D, BF16), (D, BF16), (D, BF16), (N_BRANCHES * D, BF16)], [(1, D)] * 3, TMW, "merge_bwd")
        gsmall["b_gate"][l] = jnp.concatenate([db0, db1, db2], axis=0)
        gbig["w_proj_a"][l] = _mm(s["y_a"], dpa, "tn", F32, "mm_dw_pa", col_chunks=N_CHIPS)
        gbig["w_proj_b"][l] = row_chunks(_mm(s["y_b"], dpb, "tn", F32, "mm_dw_pb"))
        gbig["w_proj_c"][l] = _mm(s["y_c"], dpc, "tn", F32, "mm_dw_pc", col_chunks=N_CHIPS)
        dy_a = _mm(dpa, p["w_pa"], "nt", BF16, "mm_dy_a")
        dy_b = _mm(dpb, p["w_pb"], "nt", BF16, "mm_dy_b")
        dy_c = _mm(dpc, p["w_pc"], "nt", BF16, "mm_dy_c")
        dh_c, g_sg, g_sb, g_sw, g_sb3 = _sgu_bwd(s["h"], cb("hu"), cb("hv"), dy_c, p["sg"], p["sb"], p["sw"], p["sb3"])
        gsmall["sgu_ln_g"][l], gsmall["sgu_ln_b"][l], gsmall["sgu_w"][l] = g_sg, g_sb, g_sw
        gsmall["sgu_b"][l] = g_sb3.reshape(SGU_GROUPS, SGU_CHUNK)
        dqa, dka, dva, g_sinks = _swa_bwd(s["h"], cb("qa"), cb("ka"), cb("va"), p["sinks"], dy_a)
        gsmall["sinks"][l] = g_sinks
        delta = _mla_delta(dy_b, s["y_b"], T)
        dqn, dqr = _mla_bwd_dq(s["q_full"], s["qr"], s["kv"], s["kr"], dy_b, s["lse"], delta, T)
        dkn, dv, dkr_heads = _mla_bwd_dkv(s["q_full"], s["qr"], s["kv"], s["kr"], dy_b, s["lse"], delta, T)
        dkr = _sum_heads(dkr_heads, TM)
        dqr_raw, dkr_raw = _rope_call(dqr, MLA_HEADS * LANES, 0, dkr, LANES, 0, cos_t, sin_t, -1.0, TM, "rope_bwd")
        dq_full = jnp.concatenate([dqn, dqr_raw], axis=1)
        dkv = jnp.concatenate([dkn, dv], axis=1)
        gbig["w_uq"][l] = _col_chunks(_unperm_w_uq(_mm(s["cqn"], dq_full, "tn", F32, "mm_dw_uq")))
        gbig["w_ukv"][l] = _col_chunks(_unperm_w_ukv(_mm(s["ckvn"], dkv, "tn", F32, "mm_dw_ukv")))
        dcqn = _mm(dq_full, p["w_uq"], "nt", F32, "mm_dcqn")
        dckvn = _mm(dkv, p["w_ukv"], "nt", F32, "mm_dckvn")

        def fn_rms_bwd(i, rows, ps):
            _, vjp1 = jax.vjp(lambda c_, g_: _rms_norm(c_.astype(F32), g_), rows[0], ps[0])
            _, vjp2 = jax.vjp(lambda c_, g_: _rms_norm(c_.astype(F32), g_), rows[1], ps[1])
            d1, dg1 = vjp1(rows[2])
            d2, dg2 = vjp2(rows[3])
            return [d1, d2], [dg1, dg2]
        (dcq, dckv), (g_qg, g_kvg) = _rowwise(
            fn_rms_bwd, [(s["h"], MLA_Q_RANK, cb("cq")), (s["h"], MLA_KV_RANK, cb("ckv")), _whole(dcqn), _whole(dckvn)],
            [p["qg"], p["kvg"]], [(MLA_Q_RANK, BF16), (MLA_KV_RANK, BF16)], [(1, MLA_Q_RANK), (1, MLA_KV_RANK)], TM, "mla_rms_bwd")
        gsmall["q_norm_g"][l], gsmall["kv_norm_g"][l] = g_qg, g_kvg
        tail = jnp.zeros((S, NP - lay["kr"][0] - LANES), BF16)
        dh = jnp.concatenate([dgl, dqa, dcq, dckv, dh_c, dka, dva, dkr_raw, tail], axis=1)
        gbig["w_in"][l] = _col_chunks(_unperm_w_in(_mm(s["x0b"], dh, "tn", F32, "mm_dw_in"), lay, D))
        dx = _mm(dh, p["w_in"], "nt", F32, "mm_dx0", add=dx0_res)

        local = [gbig[k][l] for k in BIG]
        from_sibling = _swap_halves(local, f"rs_swap_halves_l{l}")
        partial = [_add_half(g, r, cc, f"rs_add_{k}_l{l}") for k, g, r in zip(BIG, local, from_sibling)]
        arrived = _scatter_to_chips([p16 for _, p16 in partial], f"rs_scatter_chips_l{l}")
        for k, (p32, _), arr in zip(BIG, partial, arrived):
            reduced[k] = _sum_into(p32, arr, chip, cc, l, L, reduced.get(k), f"rs_sum_{k}_l{l}")

    grad_x = dx.reshape(x.shape)
    g_big = dict(zip(BIG, _join_halves([reduced[k] for k in BIG], "rs_join_halves")))

    small_shapes = {k: (small_sharded_full[k] if k in SMALL_SHARDED else tuple(W[k].shape)) for k in SMALL}
    sv = jnp.concatenate([jnp.stack(gsmall[k]).reshape(-1) for k in SMALL])
    n_sv = sv.shape[0]
    sv = jnp.pad(sv, (0, -n_sv % (8 * LANES))).reshape(-1, LANES)
    sv = _all_reduce_small(sv).reshape(-1)
    g_small, o_ = {}, 0
    for k in SMALL:
        n = int(np.prod(small_shapes[k]))
        g = sv[o_:o_ + n].reshape(small_shapes[k])
        if k in SMALL_SHARDED:
            g = lax.dynamic_slice_in_dim(g, chip * W[k].shape[-1], W[k].shape[-1], axis=-1)
        g_small[k] = g
        o_ += n

    delta, new_m, new_v = {}, {}, {}
    for k in BIG:
        shp = shards[k]
        v2 = lambda t: t.reshape(shp[0] * shp[1], shp[2])
        d_, m_, v_ = _adamw(v2(W[k]), v2(g_big[k]), v2(Mo[k]), v2(Vo[k]), "adamw_" + k)
        delta[k], new_m[k], new_v[k] = d_.reshape(shp), m_.reshape(shp), v_.reshape(shp)
    pack = lambda t: jnp.concatenate([t[k].reshape(-1) for k in SMALL])
    n_small = sum(int(np.prod(W[k].shape)) for k in SMALL)
    pad2 = lambda t: jnp.pad(t, (0, -n_small % (8 * LANES))).reshape(-1, LANES)
    d_, m_, v_ = _adamw(pad2(pack(W)), pad2(pack(g_small)), pad2(pack(Mo)), pad2(pack(Vo)), "adamw_small")
    o_ = 0
    for k in SMALL:
        n = int(np.prod(W[k].shape))
        take = lambda t: t.reshape(-1)[o_:o_ + n].reshape(W[k].shape)
        delta[k], new_m[k], new_v[k] = take(d_), take(m_), take(v_)
        o_ += n

    grads = {**g_big, **g_small}
    return (loss, grad_x, *[grads[k] for k in WEIGHTS], *[delta[k] for k in WEIGHTS], *[new_m[k] for k in WEIGHTS],
            *[new_v[k] for k in WEIGHTS])
```

```python
import functools
import math

import jax
import jax.numpy as jnp
import numpy as np
from jax import lax
from jax.experimental import pallas as pl
from jax.experimental.pallas import tpu as pltpu

F32, BF16 = jnp.float32, jnp.bfloat16
SDS = jax.ShapeDtypeStruct
MESH = pl.DeviceIdType.MESH

SWA_Q_HEADS, SWA_KV_HEADS, SWA_HEAD_DIM, SWA_BLOCK = 16, 2, 64, 128
MLA_HEADS, MLA_NOPE, MLA_ROPE, MLA_V = 16, 128, 64, 128
MLA_Q_RANK, MLA_KV_RANK = 512, 512
ROPE_THETA = 10000.0
SGU_GROUPS, SGU_GROUP_DIM, SGU_CHUNK = 8, 128, 128
SGU_WIDTH = SGU_GROUPS * SGU_GROUP_DIM
A_Q = SWA_Q_HEADS * SWA_HEAD_DIM
A_KV = SWA_KV_HEADS * SWA_HEAD_DIM
N_BRANCHES = 3
DEPTH = 2
EPS = 1e-5
MASK_VALUE = -1e30
DN_ALPHA = (2 * DEPTH) ** 0.25
ADAM_LR, ADAM_B1, ADAM_B2, ADAM_EPS, ADAM_WD, ADAM_STEP = 0.001, 0.9, 0.999, 1e-08, 0.01, 10
N_CHIPS = 4

LANES = 128
VMEM_LIMIT = 48 * 1024 * 1024

BIG = ["w_in", "w_uq", "w_ukv", "w_proj_a", "w_proj_b", "w_proj_c", "w_o", "w_up", "w_down"]
ROW_SHARDED = {"w_proj_b", "w_o", "w_down"}
SMALL = ["b_gate", "sinks", "q_norm_g", "kv_norm_g", "sgu_ln_g", "sgu_ln_b", "sgu_w", "sgu_b", "ln1_g", "ln1_b",
         "conv_w", "conv_b", "ln2_g", "ln2_b"]
SMALL_SHARDED = {"b_gate", "conv_w"}
WEIGHTS = ["w_in", "b_gate", "sinks", "q_norm_g", "kv_norm_g", "w_uq", "w_ukv", "sgu_ln_g", "sgu_ln_b", "sgu_w", "sgu_b",
           "w_proj_a", "w_proj_b", "w_proj_c", "w_o", "ln1_g", "ln1_b", "w_up", "conv_w", "conv_b", "w_down", "ln2_g", "ln2_b"]


def _pcall(body, **kw):
    return pl.pallas_call(body, **kw)


def _params(sem=None):
    return pltpu.CompilerParams(dimension_semantics=sem, vmem_limit_bytes=VMEM_LIMIT)


def _tile(dim, pref, align=LANES):
    t = (min(pref, dim) // align) * align
    while t >= align:
        if dim % t == 0:
            return t
        t -= align
    return dim


def _mm(a, b, mode, out_dtype, name, add=None, tm=1024, tn=512, tk=2048, col_chunks=1, carry=None):
    if mode == "nn":
        (M, K), (K2, N) = a.shape, b.shape
    elif mode == "nt":
        (M, K), (N, K2) = a.shape, b.shape
    else:
        (K, M), (K2, N) = a.shape, b.shape
    assert K == K2, (a.shape, b.shape, mode)
    assert N % col_chunks == 0
    tm, tn, tk = _tile(M, tm), _tile(N // col_chunks, tn), _tile(K, tk)
    assert (N // col_chunks) % tn == 0
    per_chunk = (N // col_chunks) // tn
    nk = K // tk
    if mode == "tn":
        a_spec = pl.BlockSpec((tk, tm), lambda i, j, k: (k, i))
    else:
        a_spec = pl.BlockSpec((tm, tk), lambda i, j, k: (i, k))
    if mode == "nt":
        b_spec = pl.BlockSpec((tn, tk), lambda i, j, k: (j, k))
    else:
        b_spec = pl.BlockSpec((tk, tn), lambda i, j, k: (k, j))
    dn = {"nn": (((1,), (0,)), ((), ())), "nt": (((1,), (1,)), ((), ())), "tn": (((0,), (0,)), ((), ()))}[mode]
    chunked = col_chunks > 1
    if chunked:
        assert add is None
        o_spec = pl.BlockSpec((1, tm, tn), lambda i, j, k: (lax.div(j, per_chunk), i, lax.rem(j, per_chunk)))
        out_shape = SDS((col_chunks, M, N // col_chunks), out_dtype)
    else:
        o_spec = pl.BlockSpec((tm, tn), lambda i, j, k: (i, j))
        out_shape = SDS((M, N), out_dtype)
    has_add = add is not None

    def body(*refs):
        if has_add:
            a_ref, b_ref, add_ref, o_ref, acc_ref = refs
        else:
            a_ref, b_ref, o_ref, acc_ref = refs
        k = pl.program_id(2)

        @pl.when(k == 0)
        def _():
            acc_ref[...] = jnp.zeros_like(acc_ref)

        acc_ref[...] += lax.dot_general(a_ref[...].astype(BF16), b_ref[...].astype(BF16), dn,
                                        preferred_element_type=F32)

        @pl.when(k == nk - 1)
        def _():
            r = acc_ref[...]
            if has_add:
                r = r + add_ref[...].astype(F32)
            if chunked:
                o_ref[0] = r.astype(o_ref.dtype)
            else:
                o_ref[...] = r.astype(o_ref.dtype)

    ins = [a, b] + ([add] if has_add else [])
    in_specs = [a_spec, b_spec] + ([o_spec] if has_add else [])
    (out,), carried = _call(body, name, (M // tm, N // tn, nk), 0, in_specs, [o_spec], [out_shape], [pltpu.VMEM((tm, tn), F32)],
                            ins, ("parallel", "parallel", "arbitrary"), carry)
    return out if carry is None else (out, carried)


def _rowwise(fn, rows, params, row_outs, acc_outs, tm, name):
    n_rows = rows[0][0].shape[0]
    assert n_rows % tm == 0
    nr, npar, no = len(rows), len(params), len(row_outs)

    def body(*refs):
        i = pl.program_id(0)
        r, p = refs[:nr], refs[nr:nr + npar]
        o, acc = refs[nr + npar:nr + npar + no], refs[nr + npar + no:]
        outs, sums = fn(i, [x[...] for x in r], [x[...] for x in p])
        for ref, val in zip(o, outs, strict=True):
            ref[...] = val.astype(ref.dtype)
        if acc:
            @pl.when(i == 0)
            def _():
                for ref in acc:
                    ref[...] = jnp.zeros_like(ref)
            for ref, val in zip(acc, sums, strict=True):
                ref[...] += val.astype(F32)

    def full(shape):
        nd = len(shape)
        return pl.BlockSpec(tuple(shape), lambda i: (0,) * nd)

    in_specs = [pl.BlockSpec((tm, w), (lambda i, cb=cb: (i, cb))) for (_, w, cb) in rows] + [full(p.shape) for p in params]
    out_specs = [pl.BlockSpec((tm, w), lambda i: (i, 0)) for (w, _) in row_outs] + [full(s) for s in acc_outs]
    out_shape = [SDS((n_rows, w), dt) for (w, dt) in row_outs] + [SDS(tuple(s), F32) for s in acc_outs]
    res = _pcall(body, name=name, out_shape=out_shape, grid=(n_rows // tm,), in_specs=in_specs, out_specs=out_specs,
                 compiler_params=_params(("arbitrary",)))(*[r[0] for r in rows], *params)
    return list(res[:no]), list(res[no:])


def _whole(a):
    return (a, a.shape[1], 0)


def _gelu(x):
    return 0.5 * x * (1.0 + lax.erf(x * (1.0 / math.sqrt(2.0))))


def _layer_norm(x, g, b):
    mu = x.mean(-1, keepdims=True)
    var = jnp.mean(jnp.square(x - mu), -1, keepdims=True)
    return (x - mu) * lax.rsqrt(var + EPS) * g + b


def _rms_norm(x, g):
    return x * lax.rsqrt(jnp.mean(jnp.square(x), -1, keepdims=True) + EPS) * g


def _sgu_math(hu, hv, ln_g, ln_b, ws, bs):
    u = _gelu(hu.astype(F32))
    vn = _layer_norm(_gelu(hv.astype(F32)), ln_g, ln_b)
    r = lax.broadcasted_iota(jnp.int32, (SGU_CHUNK, SGU_CHUNK), 0)
    c = lax.broadcasted_iota(jnp.int32, (SGU_CHUNK, SGU_CHUNK), 1)
    outs = []
    for g in range(SGU_GROUPS):
        w = jnp.where(r >= c, ws[g], 0.0).astype(BF16)
        vg = vn[:, g * SGU_GROUP_DIM:(g + 1) * SGU_GROUP_DIM].astype(BF16)
        outs.append(jnp.dot(w, vg, preferred_element_type=F32) + bs[g])
    return u * jnp.concatenate(outs, axis=1)


def _sgu_fwd(h, cu, cv, ln_g, ln_b, w, b3):
    def fn(i, rows, ps):
        g_, b_, w_, b3_ = ps
        y = _sgu_math(rows[0], rows[1], g_, b_, [w_[g] for g in range(SGU_GROUPS)], [b3_[g] for g in range(SGU_GROUPS)])
        return [y], []
    (y,), _ = _rowwise(fn, [(h, SGU_WIDTH, cu), (h, SGU_WIDTH, cv)], [ln_g, ln_b, w, b3], [(SGU_WIDTH, BF16)], [],
                       SGU_CHUNK, "sgu_fwd")
    return y


def _sgu_bwd(h, cu, cv, dy, ln_g, ln_b, w, b3):
    nd = 2 * SGU_WIDTH

    def body(hu_ref, hv_ref, dy_ref, g_ref, b_ref, w_ref, b3_ref, dh_ref, dg_ref, db_ref, dw_ref, db3_ref):
        i = pl.program_id(0)

        @pl.when(i == 0)
        def _():
            dg_ref[...] = jnp.zeros_like(dg_ref)
            db_ref[...] = jnp.zeros_like(db_ref)
            dw_ref[...] = jnp.zeros_like(dw_ref)
            db3_ref[...] = jnp.zeros_like(db3_ref)

        ws = [w_ref[g] for g in range(SGU_GROUPS)]
        bs = [b3_ref[g] for g in range(SGU_GROUPS)]
        _, vjp = jax.vjp(_sgu_math, hu_ref[...], hv_ref[...], g_ref[...], b_ref[...], ws, bs)
        dhu, dhv, dg, db, dws, dbs = vjp(dy_ref[...].astype(F32))
        dh_ref[...] = jnp.concatenate([dhu, dhv], axis=1).astype(dh_ref.dtype)
        dg_ref[...] += dg
        db_ref[...] += db
        for g in range(SGU_GROUPS):
            dw_ref[g] += dws[g]
            db3_ref[g] += dbs[g]

    n = h.shape[0]
    blk = lambda cb: pl.BlockSpec((SGU_CHUNK, SGU_WIDTH), lambda i, cb=cb: (i, cb))
    full = lambda s: pl.BlockSpec(tuple(s), lambda i: (0,) * len(s))
    return _pcall(
        body, name="sgu_bwd", grid=(n // SGU_CHUNK,),
        out_shape=[SDS((n, nd), BF16), SDS(ln_g.shape, F32), SDS(ln_b.shape, F32), SDS(w.shape, F32), SDS(b3.shape, F32)],
        in_specs=[blk(cu), blk(cv), blk(0), full(ln_g.shape), full(ln_b.shape), full(w.shape), full(b3.shape)],
        out_specs=[pl.BlockSpec((SGU_CHUNK, nd), lambda i: (i, 0)), full(ln_g.shape), full(ln_b.shape), full(w.shape),
                   full(b3.shape)],
        compiler_params=_params(("arbitrary",)))(h, h, dy, ln_g, ln_b, w, b3)


def _swa_math(q, kp, kc, vp, vc, sinks, not_first):
    kw = jnp.concatenate([kp, kc], axis=0).astype(BF16)
    vw = jnp.concatenate([vp, vc], axis=0).astype(BF16)
    qb = q.astype(BF16)
    q_off = lax.broadcasted_iota(jnp.int32, (SWA_BLOCK, 2 * SWA_BLOCK), 0) + SWA_BLOCK
    k_off = lax.broadcasted_iota(jnp.int32, (SWA_BLOCK, 2 * SWA_BLOCK), 1)
    rel = q_off - k_off
    valid = (rel >= 0) & (rel < SWA_BLOCK) & (not_first | (k_off >= SWA_BLOCK))
    G = SWA_Q_HEADS // SWA_KV_HEADS
    outs = []
    for head in range(SWA_Q_HEADS):
        hk = head // G
        qh = qb[:, head * SWA_HEAD_DIM:(head + 1) * SWA_HEAD_DIM]
        kh = kw[:, hk * SWA_HEAD_DIM:(hk + 1) * SWA_HEAD_DIM]
        vh = vw[:, hk * SWA_HEAD_DIM:(hk + 1) * SWA_HEAD_DIM]
        s = lax.dot_general(qh, kh, (((1,), (1,)), ((), ())), preferred_element_type=F32) * (SWA_HEAD_DIM ** -0.5)
        s = jnp.where(valid, s, MASK_VALUE)
        sink = sinks[:, head:head + 1]
        m = jnp.maximum(s.max(-1, keepdims=True), sink)
        p = jnp.exp(s - m)
        p = (p / (p.sum(-1, keepdims=True) + jnp.exp(sink - m))).astype(BF16)
        outs.append(jnp.dot(p, vh, preferred_element_type=F32))
    return jnp.concatenate(outs, axis=1)


def _swa_fwd(h, cq, ck, cv, sinks):
    n = h.shape[0]
    nb = n // SWA_BLOCK

    def body(q_ref, kp_ref, kc_ref, vp_ref, vc_ref, s_ref, o_ref):
        i = pl.program_id(0)
        f = lambda x: x[...].astype(F32)
        o_ref[...] = _swa_math(f(q_ref), f(kp_ref), f(kc_ref), f(vp_ref), f(vc_ref), s_ref[...], i > 0).astype(o_ref.dtype)

    prev = lambda cb: pl.BlockSpec((SWA_BLOCK, A_KV), lambda i, cb=cb: (jnp.maximum(i - 1, 0), cb))
    cur = lambda cb: pl.BlockSpec((SWA_BLOCK, A_KV), lambda i, cb=cb: (i, cb))
    return _pcall(body, name="swa_fwd", grid=(nb,), out_shape=SDS((n, A_Q), BF16),
                  in_specs=[pl.BlockSpec((SWA_BLOCK, A_Q), lambda i: (i, cq)), prev(ck), cur(ck), prev(cv), cur(cv),
                            pl.BlockSpec((1, SWA_Q_HEADS), lambda i: (0, 0))],
                  out_specs=pl.BlockSpec((SWA_BLOCK, A_Q), lambda i: (i, 0)),
                  compiler_params=_params(("arbitrary",)))(h, h, h, h, h, sinks)


def _swa_bwd(h, cq, ck, cv, sinks, dy):
    n = h.shape[0]
    nb = n // SWA_BLOCK

    def body(q_ref, kp_ref, kc_ref, vp_ref, vc_ref, s_ref, dy_ref, dq_ref, dk_ref, dv_ref, ds_ref, ck_ref, cv_ref):
        r = pl.program_id(0)
        blk = nb - 1 - r

        @pl.when(r == 0)
        def _():
            ds_ref[...] = jnp.zeros_like(ds_ref)
            ck_ref[...] = jnp.zeros_like(ck_ref)
            cv_ref[...] = jnp.zeros_like(cv_ref)

        f = lambda x: x[...].astype(F32)
        not_first = blk > 0
        _, vjp = jax.vjp(lambda q, kp, kc, vp, vc, s: _swa_math(q, kp, kc, vp, vc, s, not_first),
                         f(q_ref), f(kp_ref), f(kc_ref), f(vp_ref), f(vc_ref), s_ref[...])
        dq, dkp, dkc, dvp, dvc, dsk = vjp(f(dy_ref))
        dq_ref[...] = dq.astype(dq_ref.dtype)
        dk_ref[...] = (dkc + ck_ref[...]).astype(dk_ref.dtype)
        dv_ref[...] = (dvc + cv_ref[...]).astype(dv_ref.dtype)
        ck_ref[...] = dkp
        cv_ref[...] = dvp
        ds_ref[...] += dsk

    rev = lambda i: nb - 1 - i
    prev = lambda cb: pl.BlockSpec((SWA_BLOCK, A_KV), lambda i, cb=cb: (jnp.maximum(rev(i) - 1, 0), cb))
    cur = lambda cb: pl.BlockSpec((SWA_BLOCK, A_KV), lambda i, cb=cb: (rev(i), cb))
    return _pcall(
        body, name="swa_bwd", grid=(nb,),
        out_shape=[SDS((n, A_Q), BF16), SDS((n, A_KV), BF16), SDS((n, A_KV), BF16), SDS((1, SWA_Q_HEADS), F32)],
        in_specs=[pl.BlockSpec((SWA_BLOCK, A_Q), lambda i: (rev(i), cq)), prev(ck), cur(ck), prev(cv), cur(cv),
                  pl.BlockSpec((1, SWA_Q_HEADS), lambda i: (0, 0)), pl.BlockSpec((SWA_BLOCK, A_Q), lambda i: (rev(i), 0))],
        out_specs=[pl.BlockSpec((SWA_BLOCK, A_Q), lambda i: (rev(i), 0)), pl.BlockSpec((SWA_BLOCK, A_KV), lambda i: (rev(i), 0)),
                   pl.BlockSpec((SWA_BLOCK, A_KV), lambda i: (rev(i), 0)), pl.BlockSpec((1, SWA_Q_HEADS), lambda i: (0, 0))],
        scratch_shapes=[pltpu.VMEM((SWA_BLOCK, A_KV), F32), pltpu.VMEM((SWA_BLOCK, A_KV), F32)],
        compiler_params=_params(("arbitrary",)))(h, h, h, h, h, sinks, dy)


def _rope(x, cos, sin, sign):
    w = x.shape[1]
    reps = w // LANES
    ct = jnp.tile(cos, (1, reps)) if reps > 1 else cos
    st = jnp.tile(sin, (1, reps)) if reps > 1 else sin
    fwd = pltpu.roll(x, MLA_ROPE // 2, axis=1)
    bwd = pltpu.roll(x, w - MLA_ROPE // 2, axis=1)
    lane = lax.broadcasted_iota(jnp.int32, x.shape, 1) % LANES
    rot = jnp.where(lane < MLA_ROPE // 2, -bwd, fwd)
    return x * ct + sign * (rot * st)


def _rope_call(a, wa, ca, b, wb, cb, cos, sin, sign, tm, name):
    def fn(i, rows, ps):
        xa, xb, c_, s_ = rows
        return [_rope(xa.astype(F32), c_, s_, sign), _rope(xb.astype(F32), c_, s_, sign)], []
    (ra, rb), _ = _rowwise(fn, [(a, wa, ca), (b, wb, cb), _whole(cos), _whole(sin)], [], [(wa, BF16), (wb, BF16)], [], tm, name)
    return ra, rb


MLA_SCALE = (MLA_NOPE + MLA_ROPE) ** -0.5
LOG2E = math.log2(math.e)


def _mla_scores(qn_ref, qr_ref, kn_ref, kr_ref, masked):
    q = jnp.concatenate([qn_ref[...], qr_ref[...]], axis=1)
    k = jnp.concatenate([kn_ref[...], kr_ref[...]], axis=1)
    s = lax.dot_general(q, k, (((1,), (1,)), ((), ())), preferred_element_type=F32)
    if masked:
        row = lax.broadcasted_iota(jnp.int32, s.shape, 0)
        col = lax.broadcasted_iota(jnp.int32, s.shape, 1)
        s = jnp.where(col <= row, s, MASK_VALUE)
    return s, q, k


def _causal_pairs(nq, by_query):
    if by_query:
        pairs = [(i, j) for i in range(nq) for j in range(i + 1)]
    else:
        pairs = [(i, j) for j in range(nq) for i in range(j, nq)]
    return (jnp.asarray(np.array([p[0] for p in pairs], np.int32)), jnp.asarray(np.array([p[1] for p in pairs], np.int32)),
            len(pairs))


def _mla_fwd(q_full, qr, kv, kr, T, carry=None):
    n = q_full.shape[0]
    nq = n // T
    H = MLA_HEADS
    qi, kj, npairs = _causal_pairs(nq, True)

    def body(qi_ref, kj_ref, qn_ref, qr_ref, kn_ref, v_ref, kr_ref, y_ref, lse_ref, m_ref, l_ref, acc_ref):
        t = pl.program_id(1)
        i, j = qi_ref[t], kj_ref[t]

        @pl.when(j == 0)
        def _():
            m_ref[...] = jnp.full_like(m_ref, MASK_VALUE)
            l_ref[...] = jnp.zeros_like(l_ref)
            acc_ref[...] = jnp.zeros_like(acc_ref)

        def update(masked):
            s, _, _ = _mla_scores(qn_ref, qr_ref, kn_ref, kr_ref, masked)
            m_prev = m_ref[...]
            m_new = jnp.maximum(m_prev, s.max(-1, keepdims=True))
            p = jnp.exp2((s - m_new[:, :1]) * (MLA_SCALE * LOG2E))
            alpha = jnp.exp2((m_prev - m_new) * (MLA_SCALE * LOG2E))
            l_ref[...] = alpha * l_ref[...] + p.sum(-1, keepdims=True)
            acc_ref[...] = alpha * acc_ref[...] + jnp.dot(p.astype(BF16), v_ref[...], preferred_element_type=F32)
            m_ref[...] = m_new

        @pl.when(j < i)
        def _():
            update(False)

        @pl.when(j == i)
        def _():
            update(True)
            y_ref[...] = (acc_ref[...] / l_ref[...]).astype(y_ref.dtype)
            lse_ref[0] = m_ref[...] * (MLA_SCALE * LOG2E) + jnp.log2(l_ref[...])

    qspec = lambda off: pl.BlockSpec((T, LANES), lambda h, t, qi, kj, off=off: (qi[t], off + h))
    kspec = lambda off: pl.BlockSpec((T, LANES), lambda h, t, qi, kj, off=off: (kj[t], off + h))
    return _call(
        body, "mla_fwd", (H, npairs), 2,
        [qspec(0), qspec(0), kspec(0), kspec(H), pl.BlockSpec((T, LANES), lambda h, t, qi, kj: (kj[t], 0))],
        [pl.BlockSpec((T, LANES), lambda h, t, qi, kj: (qi[t], h)), pl.BlockSpec((1, T, LANES), lambda h, t, qi, kj: (h, qi[t], 0))],
        [SDS((n, H * MLA_V), BF16), SDS((H, n, LANES), F32)], [pltpu.VMEM((T, LANES), F32)] * 3,
        [qi, kj, q_full, qr, kv, kv, kr], ("parallel", "arbitrary"), carry)


def _mla_delta(dy, y, T):
    n = y.shape[0]
    H = MLA_HEADS

    def body(dy_ref, y_ref, d_ref):
        d = jnp.sum(dy_ref[...].astype(F32) * y_ref[...].astype(F32), axis=-1, keepdims=True)
        d_ref[0] = jnp.broadcast_to(d, (T, LANES))

    spec = pl.BlockSpec((T, LANES), lambda h, i: (i, h))
    return _pcall(body, name="mla_delta", grid=(H, n // T), out_shape=SDS((H, n, LANES), F32), in_specs=[spec, spec],
                  out_specs=pl.BlockSpec((1, T, LANES), lambda h, i: (h, i, 0)),
                  compiler_params=_params(("parallel", "parallel")))(dy, y)


def _mla_bwd_dq(q_full, qr, kv, kr, dy, lse, delta, T):
    n = q_full.shape[0]
    nq = n // T
    H = MLA_HEADS

    qi, kj, npairs = _causal_pairs(nq, True)

    def body(qi_ref, kj_ref, qn_ref, qr_ref, kn_ref, v_ref, kr_ref, dy_ref, lse_ref, dl_ref, dqn_ref, dqr_ref, acc_ref):
        t = pl.program_id(1)
        i, j = qi_ref[t], kj_ref[t]

        @pl.when(j == 0)
        def _():
            acc_ref[...] = jnp.zeros_like(acc_ref)

        def update(masked):
            s, _, k = _mla_scores(qn_ref, qr_ref, kn_ref, kr_ref, masked)
            p = jnp.exp2(s * (MLA_SCALE * LOG2E) - lse_ref[0][:, :1])
            dp = lax.dot_general(dy_ref[...], v_ref[...], (((1,), (1,)), ((), ())), preferred_element_type=F32)
            ds = p * (dp - dl_ref[0][:, :1])
            acc_ref[...] += jnp.dot(ds.astype(BF16), k, preferred_element_type=F32)

        @pl.when(j < i)
        def _():
            update(False)

        @pl.when(j == i)
        def _():
            update(True)
            dqn_ref[...] = (acc_ref[:, :LANES] * MLA_SCALE).astype(dqn_ref.dtype)
            dqr_ref[...] = (acc_ref[:, LANES:] * MLA_SCALE).astype(dqr_ref.dtype)

    qspec = lambda off: pl.BlockSpec((T, LANES), lambda h, t, qi, kj, off=off: (qi[t], off + h))
    kspec = lambda off: pl.BlockSpec((T, LANES), lambda h, t, qi, kj, off=off: (kj[t], off + h))
    stat = pl.BlockSpec((1, T, LANES), lambda h, t, qi, kj: (h, qi[t], 0))
    out = pl.BlockSpec((T, LANES), lambda h, t, qi, kj: (qi[t], h))
    gs = pltpu.PrefetchScalarGridSpec(
        num_scalar_prefetch=2, grid=(H, npairs),
        in_specs=[qspec(0), qspec(0), kspec(0), kspec(H), pl.BlockSpec((T, LANES), lambda h, t, qi, kj: (kj[t], 0)),
                  qspec(0), stat, stat],
        out_specs=[out, out], scratch_shapes=[pltpu.VMEM((T, 2 * LANES), F32)])
    return _pcall(body, name="mla_bwd_dq", grid_spec=gs, out_shape=[SDS((n, H * LANES), BF16), SDS((n, H * LANES), BF16)],
                  compiler_params=_params(("parallel", "arbitrary")))(qi, kj, q_full, qr, kv, kv, kr, dy, lse, delta)


def _mla_bwd_dkv(q_full, qr, kv, kr, dy, lse, delta, T, carry=None):
    n = q_full.shape[0]
    nq = n // T
    H = MLA_HEADS

    qi, kj, npairs = _causal_pairs(nq, False)

    def body(qi_ref, kj_ref, qn_ref, qr_ref, kn_ref, v_ref, kr_ref, dy_ref, lse_ref, dl_ref, dkn_ref, dv_ref, dkr_ref,
             dk_acc, dv_acc):
        t = pl.program_id(1)
        i, j = qi_ref[t], kj_ref[t]

        @pl.when(i == j)
        def _():
            dk_acc[...] = jnp.zeros_like(dk_acc)
            dv_acc[...] = jnp.zeros_like(dv_acc)

        def update(masked):
            s, q, _ = _mla_scores(qn_ref, qr_ref, kn_ref, kr_ref, masked)
            p = jnp.exp2(s * (MLA_SCALE * LOG2E) - lse_ref[0][:, :1])
            dy = dy_ref[...]
            dv_acc[...] += lax.dot_general(p.astype(BF16), dy, (((0,), (0,)), ((), ())), preferred_element_type=F32)
            dp = lax.dot_general(dy, v_ref[...], (((1,), (1,)), ((), ())), preferred_element_type=F32)
            ds = p * (dp - dl_ref[0][:, :1])
            dk_acc[...] += lax.dot_general(ds.astype(BF16), q, (((0,), (0,)), ((), ())), preferred_element_type=F32)

        @pl.when(i == j)
        def _():
            update(True)

        @pl.when(i > j)
        def _():
            update(False)

        @pl.when(i == nq - 1)
        def _():
            dkn_ref[...] = (dk_acc[:, :LANES] * MLA_SCALE).astype(dkn_ref.dtype)
            dv_ref[...] = dv_acc[...].astype(dv_ref.dtype)
            dkr_ref[0] = dk_acc[:, LANES:] * MLA_SCALE

    qspec = lambda off: pl.BlockSpec((T, LANES), lambda h, t, qi, kj, off=off: (qi[t], off + h))
    kspec = lambda off: pl.BlockSpec((T, LANES), lambda h, t, qi, kj, off=off: (kj[t], off + h))
    stat = pl.BlockSpec((1, T, LANES), lambda h, t, qi, kj: (h, qi[t], 0))
    out = pl.BlockSpec((T, LANES), lambda h, t, qi, kj: (kj[t], h))
    return _call(
        body, "mla_bwd_dkv", (H, npairs), 2,
        [qspec(0), qspec(0), kspec(0), kspec(H), pl.BlockSpec((T, LANES), lambda h, t, qi, kj: (kj[t], 0)), qspec(0), stat, stat],
        [out, out, pl.BlockSpec((1, T, LANES), lambda h, t, qi, kj: (h, kj[t], 0))],
        [SDS((n, H * LANES), BF16), SDS((n, H * LANES), BF16), SDS((H, n, LANES), F32)],
        [pltpu.VMEM((T, 2 * LANES), F32), pltpu.VMEM((T, LANES), F32)],
        [qi, kj, q_full, qr, kv, kv, kr, dy, lse, delta], ("parallel", "arbitrary"), carry)


def _sum_heads(a, tm):
    H, n, _ = a.shape

    def body(a_ref, o_ref):
        o_ref[...] = jnp.sum(a_ref[...], axis=0)

    return _pcall(body, name="mla_sum_heads", grid=(n // tm,), out_shape=SDS((n, LANES), F32),
                  in_specs=[pl.BlockSpec((H, tm, LANES), lambda i: (0, i, 0))],
                  out_specs=pl.BlockSpec((tm, LANES), lambda i: (i, 0)), compiler_params=_params(("parallel",)))(a)


def _shift_down(x, k):
    row = lax.broadcasted_iota(jnp.int32, x.shape, 0)
    return jnp.where(row >= k, pltpu.roll(x, k, axis=0), 0.0)


def _shift_up(x, k):
    n = x.shape[0]
    row = lax.broadcasted_iota(jnp.int32, x.shape, 0)
    return jnp.where(row < n - k, pltpu.roll(x, n - k, axis=0), 0.0)


def _conv_fwd(up, w, b):
    n, c = up.shape

    def body(u_ref, w_ref, b_ref, o_ref):
        u = u_ref[...].astype(F32)
        wv = w_ref[...]
        o_ref[...] = (b_ref[...] + wv[0:1] * _shift_down(u, 2) + wv[1:2] * _shift_down(u, 1) + wv[2:3] * u).astype(o_ref.dtype)

    return _pcall(body, name="conv_fwd", grid=(c // LANES,), out_shape=SDS((n, c), BF16),
                  in_specs=[pl.BlockSpec((n, LANES), lambda j: (0, j)), pl.BlockSpec((3, LANES), lambda j: (0, j)),
                            pl.BlockSpec((1, LANES), lambda j: (0, j))],
                  out_specs=pl.BlockSpec((n, LANES), lambda j: (0, j)), compiler_params=_params(("parallel",)))(up, w, b)


def _conv_bwd(up, dc, w):
    n, c = up.shape

    def body(u_ref, d_ref, w_ref, du_ref, dw_ref, db_ref):
        u = u_ref[...].astype(F32)
        d = d_ref[...].astype(F32)
        wv = w_ref[...]
        du_ref[...] = (wv[2:3] * d + wv[1:2] * _shift_up(d, 1) + wv[0:1] * _shift_up(d, 2)).astype(du_ref.dtype)
        dw_ref[0:1, :] = jnp.sum(d * _shift_down(u, 2), axis=0, keepdims=True)
        dw_ref[1:2, :] = jnp.sum(d * _shift_down(u, 1), axis=0, keepdims=True)
        dw_ref[2:3, :] = jnp.sum(d * u, axis=0, keepdims=True)
        db_ref[...] = jnp.sum(d, axis=0, keepdims=True)

    col = pl.BlockSpec((n, LANES), lambda j: (0, j))
    return _pcall(body, name="conv_bwd", grid=(c // LANES,),
                  out_shape=[SDS((n, c), BF16), SDS((3, c), F32), SDS((1, c), F32)],
                  in_specs=[col, col, pl.BlockSpec((3, LANES), lambda j: (0, j))],
                  out_specs=[col, pl.BlockSpec((3, LANES), lambda j: (0, j)), pl.BlockSpec((1, LANES), lambda j: (0, j))],
                  compiler_params=_params(("parallel",)))(up, dc, w)


def _adamw_math(w, g, m, v):
    m = ADAM_B1 * m + (1.0 - ADAM_B1) * g
    v = ADAM_B2 * v + (1.0 - ADAM_B2) * jnp.square(g)
    m_hat = m / (1.0 - ADAM_B1 ** ADAM_STEP)
    v_hat = v / (1.0 - ADAM_B2 ** ADAM_STEP)
    delta = -ADAM_LR * (m_hat / (jnp.sqrt(v_hat) + ADAM_EPS) + ADAM_WD * w)
    return delta, m, v


def _adamw(w, g, m, v, name):
    r, c = w.shape
    tr = r
    budget = max(8, (1 << 20) // (4 * c))
    t = (min(budget, r) // 8) * 8
    while t >= 8:
        if r % t == 0:
            tr = t
            break
        t -= 8

    def fn(i, rows, ps):
        return list(_adamw_math(*rows)), []
    (d, nm, nv), _ = _rowwise(fn, [_whole(w), _whole(g), _whole(m), _whole(v)], [], [(c, F32)] * 3, [], tr, name)
    return d, nm, nv


def _coords():
    return lax.axis_index("x"), lax.axis_index("y"), lax.axis_index("c")


def _other_chips(x, y):
    return [(1 - x, y), (x, 1 - y), (1 - x, 1 - y)]


HBM_SPEC = pl.BlockSpec(memory_space=pltpu.HBM)


def _half(c, rows):
    return pl.ds(pl.multiple_of(c * rows, 16), rows)


def _rows_tile(rows, cols, budget_bytes=2 << 20, align=16):
    t = (min(max(align, budget_bytes // (4 * cols)), rows) // align) * align
    while t >= align:
        if rows % t == 0:
            return t
        t -= align
    return rows


def _scalar(v):
    return jnp.reshape(jnp.asarray(v, jnp.int32), (1,))


def _cast_into_slot(w3, layer, slot, name):
    _, R, C = w3.shape
    tr = _rows_tile(R, C)

    def body(s_ref, w_ref, o_ref):
        o_ref[0] = w_ref[0].astype(BF16)

    gs = pltpu.PrefetchScalarGridSpec(
        num_scalar_prefetch=1, grid=(R // tr,),
        in_specs=[pl.BlockSpec((1, tr, C), lambda i, s: (layer, i, 0))],
        out_specs=pl.BlockSpec((1, tr, C), lambda i, s: (s[0], i, 0)))
    return _pcall(body, name=name, grid_spec=gs, out_shape=SDS((N_CHIPS, R, C), BF16),
                  compiler_params=_params(("arbitrary",)))(_scalar(slot), w3)


class _Stage:
    def __init__(self, ins, out_shapes, aliases, n_sems, copies):
        self.ins, self.out_shapes, self.aliases, self.n_sems, self.copies = list(ins), out_shapes, aliases, n_sems, copies

    def start(self, ins, outs, send_sems, recv_sems):
        for cp in self.copies(ins, outs, send_sems, recv_sems)[0]:
            cp.start()

    def finish(self, ins, outs, send_sems, recv_sems):
        sends, arrivals = self.copies(ins, outs, send_sems, recv_sems)
        for cp in arrivals:
            cp.wait_recv()
        for cp in sends:
            cp.wait_send()


def _remote(src, dst, send_sems, recv_sems, k, to):
    return pltpu.make_async_remote_copy(src_ref=src, dst_ref=dst, send_sem=send_sems.at[k], recv_sem=recv_sems.at[k],
                                        device_id=to, device_id_type=MESH)


def _gather_stages(bufs):
    n = len(bufs)
    shapes = [SDS(b.shape, b.dtype) for b in bufs]
    same = {i: i for i in range(n)}

    def over_ici(ins, outs, send_sems, recv_sems):
        x, y, c = _coords()
        blk = lambda w, chip: outs[w].at[chip, _half(c, outs[w].shape[1] // 2), :]
        sends, arrivals = [], []
        for w in range(n):
            for j, (px, py) in enumerate(_other_chips(x, y)):
                sends.append(_remote(blk(w, 2 * x + y), blk(w, 2 * x + y), send_sems, recv_sems, 3 * w + j, (px, py, c)))
                arrivals.append(_remote(blk(w, 2 * px + py), blk(w, 2 * px + py), send_sems, recv_sems, 3 * w + j, (px, py, c)))
        return sends, arrivals

    def to_sibling(ins, outs, send_sems, recv_sems):
        x, y, c = _coords()
        blk = lambda w, chip, half: outs[w].at[chip, _half(half, outs[w].shape[1] // 2), :]
        sends, arrivals = [], []
        for w in range(n):
            for j, (px, py) in enumerate(_other_chips(x, y)):
                k = 2 * px + py
                sends.append(_remote(blk(w, k, c), blk(w, k, c), send_sems, recv_sems, 3 * w + j, (x, y, 1 - c)))
                arrivals.append(_remote(blk(w, k, 1 - c), blk(w, k, 1 - c), send_sems, recv_sems, 3 * w + j, (x, y, 1 - c)))
        return sends, arrivals

    return (lambda b: _Stage(b, shapes, same, 3 * n, over_ici)), (lambda b: _Stage(b, shapes, same, 3 * n, to_sibling))


def _swap_stage(gs_):
    n = len(gs_)

    def copies(ins, outs, send_sems, recv_sems):
        x, y, c = _coords()
        cps = [_remote(ins[w].at[:, _half(1 - c, ins[w].shape[1] // 2), :], outs[w], send_sems, recv_sems, w, (x, y, 1 - c))
               for w in range(n)]
        return cps, cps

    return _Stage(gs_, [SDS((N_CHIPS, g.shape[1] // 2, g.shape[2]), g.dtype) for g in gs_], {}, n, copies)


def _scatter_stage(ps):
    n = len(ps)

    def copies(ins, outs, send_sems, recv_sems):
        x, y, c = _coords()
        cps = [_remote(ins[w].at[2 * px + py], outs[w].at[j], send_sems, recv_sems, 3 * w + j, (px, py, c))
               for w in range(n) for j, (px, py) in enumerate(_other_chips(x, y))]
        return cps, cps

    return _Stage(ps, [SDS((3,) + p.shape[1:], p.dtype) for p in ps], {}, 3 * n, copies)


def _join_stage(bufs):
    n = len(bufs)
    L = bufs[0].shape[0]

    def copies(ins, outs, send_sems, recv_sems):
        x, y, c = _coords()
        blk = lambda w, l, half: outs[w].at[l, _half(half, outs[w].shape[1] // 2), :]
        sends = [_remote(blk(w, l, c), blk(w, l, c), send_sems, recv_sems, L * w + l, (x, y, 1 - c))
                 for w in range(n) for l in range(L)]
        arrivals = [_remote(blk(w, l, 1 - c), blk(w, l, 1 - c), send_sems, recv_sems, L * w + l, (x, y, 1 - c))
                    for w in range(n) for l in range(L)]
        return sends, arrivals

    return _Stage(bufs, [SDS(b.shape, b.dtype) for b in bufs], {i: i for i in range(n)}, L * n, copies)


def _stage_scratch(stage):
    return [pltpu.SemaphoreType.DMA((stage.n_sems,)), pltpu.SemaphoreType.DMA((stage.n_sems,))]


def _run_stage(stage, name):
    n_in, n_out = len(stage.ins), len(stage.out_shapes)

    def body(*refs):
        ins, outs, send_sems, recv_sems = refs[:n_in], refs[n_in:n_in + n_out], refs[n_in + n_out], refs[n_in + n_out + 1]
        stage.start(ins, outs, send_sems, recv_sems)
        stage.finish(ins, outs, send_sems, recv_sems)

    return _pcall(body, name=name, out_shape=stage.out_shapes, in_specs=[HBM_SPEC] * n_in, out_specs=[HBM_SPEC] * n_out,
                  input_output_aliases=stage.aliases, scratch_shapes=_stage_scratch(stage))(*stage.ins)


def _call(body, name, grid, n_prefetch, in_specs, out_specs, out_shape, scratch, operands, semantics, carry=None):
    n_in, n_out, n_sc = len(in_specs), len(out_specs), len(scratch)
    if carry is None:
        gs = pltpu.PrefetchScalarGridSpec(num_scalar_prefetch=n_prefetch, grid=grid, in_specs=in_specs, out_specs=out_specs,
                                          scratch_shapes=scratch)
        res = _pcall(body, name=name, grid_spec=gs, out_shape=out_shape, compiler_params=_params(semantics))(*operands)
        return list(res), []
    s_in, s_out = len(carry.ins), len(carry.out_shapes)

    def carrying(*refs):
        o = n_prefetch
        pre, ins = refs[:o], refs[o:o + n_in]
        o += n_in
        sins = refs[o:o + s_in]
        o += s_in
        outs = refs[o:o + n_out]
        o += n_out
        souts = refs[o:o + s_out]
        o += s_out
        sc, send_sems, recv_sems = refs[o:o + n_sc], refs[o + n_sc], refs[o + n_sc + 1]
        first = functools.reduce(jnp.logical_and, [pl.program_id(a) == 0 for a in range(len(grid))])
        last = functools.reduce(jnp.logical_and, [pl.program_id(a) == g - 1 for a, g in enumerate(grid)])

        @pl.when(first)
        def _():
            carry.start(sins, souts, send_sems, recv_sems)

        body(*pre, *ins, *outs, *sc)

        @pl.when(last)
        def _():
            carry.finish(sins, souts, send_sems, recv_sems)

    gs = pltpu.PrefetchScalarGridSpec(
        num_scalar_prefetch=n_prefetch, grid=grid, in_specs=list(in_specs) + [HBM_SPEC] * s_in,
        out_specs=list(out_specs) + [HBM_SPEC] * s_out, scratch_shapes=list(scratch) + _stage_scratch(carry))
    aliases = {n_prefetch + n_in + a: n_out + b for a, b in carry.aliases.items()}
    res = _pcall(carrying, name=name, grid_spec=gs, out_shape=list(out_shape) + list(carry.out_shapes),
                 input_output_aliases=aliases, compiler_params=_params(("arbitrary",) * len(grid)))(*operands, *carry.ins)
    return list(res[:n_out]), list(res[n_out:])


def _all_reduce_small(v):
    n = v.shape[0]

    def body(v_ref, out_ref, slots, send_sems, recv_sems):
        x, y, c = _coords()
        me = 4 * x + 2 * y + c
        cps = []
        for r in range(1, 8):
            t = (me + r) % 8
            cp = pltpu.make_async_remote_copy(src_ref=v_ref, dst_ref=slots.at[me], send_sem=send_sems.at[r - 1],
                                              recv_sem=recv_sems.at[me], device_id=(t // 4, (t // 2) % 2, t % 2),
                                              device_id_type=MESH)
            cp.start()
            cps.append(cp)
        slots[me] = v_ref[...]
        for r in range(1, 8):
            s = (me + r) % 8
            pltpu.make_async_remote_copy(src_ref=v_ref, dst_ref=slots.at[s], send_sem=send_sems.at[r - 1],
                                         recv_sem=recv_sems.at[s], device_id=(x, y, c), device_id_type=MESH).wait_recv()
        for cp in cps:
            cp.wait_send()
        acc = slots[0]
        for d in range(1, 8):
            acc = acc + slots[d]
        out_ref[...] = acc

    return _pcall(body, name="all_reduce_small", out_shape=SDS((n, LANES), F32),
                  in_specs=[pl.BlockSpec(memory_space=pltpu.VMEM)], out_specs=pl.BlockSpec(memory_space=pltpu.VMEM),
                  scratch_shapes=[pltpu.VMEM((8, n, LANES), F32), pltpu.SemaphoreType.DMA((7,)), pltpu.SemaphoreType.DMA((8,))],
                  compiler_params=pltpu.CompilerParams(vmem_limit_bytes=VMEM_LIMIT))(v)


def _add_half(g, recv, c, name):
    _, R, C = g.shape
    rows = R // 2
    tr = _rows_tile(rows, C, 1 << 20)
    nb = rows // tr

    def body(s_ref, g_ref, r_ref, o32_ref, o16_ref):
        s = g_ref[...] + r_ref[...]
        o32_ref[...] = s
        o16_ref[...] = s.astype(BF16)

    blk = lambda k, i, s: (k, i, 0)
    gs = pltpu.PrefetchScalarGridSpec(
        num_scalar_prefetch=1, grid=(N_CHIPS, nb),
        in_specs=[pl.BlockSpec((1, tr, C), lambda k, i, s: (k, s[0] * nb + i, 0)), pl.BlockSpec((1, tr, C), blk)],
        out_specs=[pl.BlockSpec((1, tr, C), blk), pl.BlockSpec((1, tr, C), blk)])
    return _pcall(body, name=name, grid_spec=gs, out_shape=[SDS((N_CHIPS, rows, C), F32), SDS((N_CHIPS, rows, C), BF16)],
                  compiler_params=_params(("arbitrary", "arbitrary")))(_scalar(c), g, recv)


def _sum_into(p32, arrived, chip, c, layer, n_layers, prev, name):
    _, rows, C = p32.shape
    tr = _rows_tile(rows, C, 1 << 20)
    nb = rows // tr

    def body(chip_ref, c_ref, p_ref, a_ref, *rest):
        o_ref = rest[-1]
        o_ref[0] = ((p_ref[0] + a_ref[0].astype(F32)) + a_ref[1].astype(F32)) + a_ref[2].astype(F32)

    in_specs = [pl.BlockSpec((1, tr, C), lambda i, chip_ref, c_ref: (chip_ref[0], i, 0)),
                pl.BlockSpec((3, tr, C), lambda i, chip_ref, c_ref: (0, i, 0))]
    ins = [p32, arrived]
    aliases = {}
    if prev is not None:
        in_specs.append(pl.BlockSpec(memory_space=pl.ANY))
        ins.append(prev)
        aliases = {4: 0}
    gs = pltpu.PrefetchScalarGridSpec(
        num_scalar_prefetch=2, grid=(nb,), in_specs=in_specs,
        out_specs=pl.BlockSpec((1, tr, C), lambda i, chip_ref, c_ref: (layer, c_ref[0] * nb + i, 0)))
    return _pcall(body, name=name, grid_spec=gs, out_shape=SDS((n_layers, 2 * rows, C), F32), input_output_aliases=aliases,
                  compiler_params=_params(("arbitrary",)))(_scalar(chip), _scalar(c), *ins)


def _h_layout(D):
    G = N_BRANCHES * D
    off, o = {}, 0
    for name, w in [("g", G), ("qa", A_Q), ("cq", MLA_Q_RANK), ("ckv", MLA_KV_RANK), ("hu", SGU_WIDTH), ("hv", SGU_WIDTH),
                    ("ka", A_KV), ("va", A_KV), ("kr", LANES)]:
        assert o % w == 0, (name, o, w)
        off[name] = (o, w)
        o += w
    off["total"] = -(-o // 512) * 512
    return off


def _perm_w_in(w, lay):
    s = np.cumsum([0, A_Q, A_KV, A_KV, MLA_Q_RANK, MLA_KV_RANK, MLA_ROPE, SGU_WIDTH, SGU_WIDTH])
    qa, ka, va, cq, ckv, kr, hu, hv = [w[:, s[i]:s[i + 1]] for i in range(8)]
    g = w[:, s[8]:]
    pad = jnp.zeros((w.shape[0], lay["total"] - lay["kr"][0] - MLA_ROPE), w.dtype)
    return jnp.concatenate([g, qa, cq, ckv, hu, hv, ka, va, kr, pad], axis=1)


def _unperm_w_in(wp, lay, D):
    take = lambda n, width=None: wp[:, lay[n][0]:lay[n][0] + (width or lay[n][1])]
    return jnp.concatenate([take("qa"), take("ka"), take("va"), take("cq"), take("ckv"), take("kr", MLA_ROPE), take("hu"),
                            take("hv"), take("g")], axis=1)


def _perm_w_uq(w):
    r = w.shape[0]
    w3 = w.reshape(r, MLA_HEADS, MLA_NOPE + MLA_ROPE)
    nope = w3[:, :, :MLA_NOPE].reshape(r, MLA_HEADS * MLA_NOPE)
    rope = jnp.pad(w3[:, :, MLA_NOPE:], ((0, 0), (0, 0), (0, LANES - MLA_ROPE))).reshape(r, MLA_HEADS * LANES)
    return jnp.concatenate([nope, rope], axis=1)


def _unperm_w_uq(wp):
    r = wp.shape[0]
    nope = wp[:, :MLA_HEADS * MLA_NOPE].reshape(r, MLA_HEADS, MLA_NOPE)
    rope = wp[:, MLA_HEADS * MLA_NOPE:].reshape(r, MLA_HEADS, LANES)[:, :, :MLA_ROPE]
    return jnp.concatenate([nope, rope], axis=2).reshape(r, MLA_HEADS * (MLA_NOPE + MLA_ROPE))


def _perm_w_ukv(w):
    r = w.shape[0]
    w3 = w.reshape(r, MLA_HEADS, MLA_NOPE + MLA_V)
    return jnp.concatenate([w3[:, :, :MLA_NOPE].reshape(r, -1), w3[:, :, MLA_NOPE:].reshape(r, -1)], axis=1)


def _unperm_w_ukv(wp):
    r = wp.shape[0]
    k = wp[:, :MLA_HEADS * MLA_NOPE].reshape(r, MLA_HEADS, MLA_NOPE)
    v = wp[:, MLA_HEADS * MLA_NOPE:].reshape(r, MLA_HEADS, MLA_V)
    return jnp.concatenate([k, v], axis=2).reshape(r, -1)


def _col_chunks(g):
    r, c4 = g.shape
    return jnp.transpose(g.reshape(r, N_CHIPS, c4 // N_CHIPS), (1, 0, 2))


def kernel(x, positions, w_in, b_gate, sinks, q_norm_g, kv_norm_g, w_uq, w_ukv, sgu_ln_g, sgu_ln_b, sgu_w, sgu_b, w_proj_a, w_proj_b, w_proj_c, w_o, ln1_g, ln1_b, w_up, conv_w, conv_b, w_down, ln2_g, ln2_b, loss_target, m_w_in, m_b_gate, m_sinks, m_q_norm_g, m_kv_norm_g, m_w_uq, m_w_ukv, m_sgu_ln_g, m_sgu_ln_b, m_sgu_w, m_sgu_b, m_w_proj_a, m_w_proj_b, m_w_proj_c, m_w_o, m_ln1_g, m_ln1_b, m_w_up, m_conv_w, m_conv_b, m_w_down, m_ln2_g, m_ln2_b, v_w_in, v_b_gate, v_sinks, v_q_norm_g, v_kv_norm_g, v_w_uq, v_w_ukv, v_sgu_ln_g, v_sgu_ln_b, v_sgu_w, v_sgu_b, v_w_proj_a, v_w_proj_b, v_w_proj_c, v_w_o, v_ln1_g, v_ln1_b, v_w_up, v_conv_w, v_conv_b, v_w_down, v_ln2_g, v_ln2_b):
    a = locals()
    W = {k: a[k] for k in WEIGHTS}
    Mo = {k: a["m_" + k] for k in WEIGHTS}
    Vo = {k: a["v_" + k] for k in WEIGHTS}
    S, D = x.shape[1], x.shape[2]
    FF2 = w_up.shape[2] * N_CHIPS
    FF = FF2 // 2
    L = DEPTH
    lay = _h_layout(D)
    NP = lay["total"]
    cx, cy, cc = _coords()
    chip = 2 * cx + cy
    T = _tile(S, 512)
    TM = _tile(S, 256, 16)
    TMW = _tile(S, 64, 16)

    shards = {k: tuple(W[k].shape) for k in BIG}
    full = {k: [None] * L for k in BIG}
    own = [[_cast_into_slot(W[k], l, chip, f"cast_{k}_l{l}") for k in BIG] for l in range(L)]
    over_ici, to_sibling = _gather_stages(own[0])

    def use_gathered(l, gathered):
        for k, g in zip(BIG, gathered):
            _, r, c_ = shards[k]
            full[k][l] = g.reshape(N_CHIPS * r, c_) if k in ROW_SHARDED else jnp.transpose(g, (1, 0, 2)).reshape(r, N_CHIPS * c_)

    use_gathered(0, _run_stage(to_sibling(_run_stage(over_ici(own[0]), "gather_ici_l0")), "gather_sibling_l0"))

    small_sharded_full = {k: tuple(W[k].shape[:-1]) + (W[k].shape[-1] * N_CHIPS,) for k in SMALL_SHARDED}
    placed = []
    for k in ("b_gate", "conv_w"):
        z = jnp.zeros(small_sharded_full[k], F32)
        z = lax.dynamic_update_slice_in_dim(z, W[k], chip * W[k].shape[-1], axis=-1)
        placed.append(jnp.where(cc == 0, z, 0.0).reshape(-1))
    pv = jnp.concatenate(placed)
    n_pv = pv.shape[0]
    pv = jnp.pad(pv, (0, -n_pv % (8 * LANES))).reshape(-1, LANES)
    pv = _all_reduce_small(pv).reshape(-1)
    nb_ = int(np.prod(small_sharded_full["b_gate"]))
    b_gate_full = pv[:nb_].reshape(small_sharded_full["b_gate"])
    conv_w_full = pv[nb_:n_pv].reshape(small_sharded_full["conv_w"])

    inv_freq = ROPE_THETA ** (-jnp.arange(0, MLA_ROPE, 2, dtype=F32) / MLA_ROPE)
    ang = positions[0].astype(F32)[:, None] * inv_freq
    cos, sin = jnp.cos(ang), jnp.sin(ang)
    cos_t = jnp.concatenate([cos, cos, jnp.ones((S, LANES - MLA_ROPE), F32)], axis=1)
    sin_t = jnp.concatenate([sin, sin, jnp.zeros((S, LANES - MLA_ROPE), F32)], axis=1)

    row = lambda v: v.reshape(1, -1)
    cb = lambda name: lay[name][0] // lay[name][1]

    xs = x[0]
    saved = []
    for l in range(L):
        p = dict(
            w_in=_perm_w_in(full["w_in"][l], lay), w_uq=_perm_w_uq(full["w_uq"][l]), w_ukv=_perm_w_ukv(full["w_ukv"][l]),
            w_pa=full["w_proj_a"][l], w_pb=full["w_proj_b"][l], w_pc=full["w_proj_c"][l], w_o=full["w_o"][l],
            w_up=full["w_up"][l], w_down=full["w_down"][l],
            sinks=row(sinks[l]), qg=row(q_norm_g[l]), kvg=row(kv_norm_g[l]), sg=row(sgu_ln_g[l]), sb=row(sgu_ln_b[l]),
            sw=sgu_w[l], sb3=sgu_b[l].reshape(SGU_GROUPS, SGU_CHUNK, 1),
            bg=b_gate_full[l], l1g=row(ln1_g[l]), l1b=row(ln1_b[l]), cw=conv_w_full[l], cbias=row(conv_b[l]),
            l2g=row(ln2_g[l]), l2b=row(ln2_b[l]))
        if l == 0:
            def fn_cast(i, rows, ps):
                return [rows[0]], []
            (xb,), _ = _rowwise(fn_cast, [_whole(xs)], [], [(D, BF16)], [], TM, "cast_x")
        h = _mm(xb, p["w_in"], "nn", BF16, "mm_h")
        y_a = _swa_fwd(h, cb("qa"), cb("ka"), cb("va"), p["sinks"])
        def fn_rms(i, rows, ps):
            return [_rms_norm(rows[0].astype(F32), ps[0]), _rms_norm(rows[1].astype(F32), ps[1])], []
        (cqn, ckvn), _ = _rowwise(fn_rms, [(h, MLA_Q_RANK, cb("cq")), (h, MLA_KV_RANK, cb("ckv"))], [p["qg"], p["kvg"]],
                                  [(MLA_Q_RANK, BF16), (MLA_KV_RANK, BF16)], [], TM, "mla_rms")
        q_full = _mm(cqn, p["w_uq"], "nn", BF16, "mm_q")
        kv = _mm(ckvn, p["w_ukv"], "nn", BF16, "mm_kv")
        qr, kr = _rope_call(q_full, MLA_HEADS * LANES, 1, h, LANES, cb("kr"), cos_t, sin_t, 1.0, TM, "rope_fwd")
        (y_b, lse), landed = _mla_fwd(q_full, qr, kv, kr, T, carry=over_ici(own[l + 1]) if l + 1 < L else None)
        if l + 1 < L:
            use_gathered(l + 1, _run_stage(to_sibling(landed), f"gather_sibling_l{l + 1}"))
        y_c = _sgu_fwd(h, cb("hu"), cb("hv"), p["sg"], p["sb"], p["sw"], p["sb3"])
        pa = _mm(y_a, p["w_pa"], "nn", F32, "mm_pa")
        pb = _mm(y_b, p["w_pb"], "nn", F32, "mm_pb")
        pc = _mm(y_c, p["w_pc"], "nn", F32, "mm_pc")

        def merge_math(pa_, pb_, pc_, g_, b0, b1, b2):
            out = 0.0
            for br, (pp, bb) in enumerate(zip((pa_, pb_, pc_), (b0, b1, b2))):
                gate = jax.nn.sigmoid(g_[:, br * D:(br + 1) * D].astype(F32) + bb)
                out = out + gate * pp
            return out

        def fn_merge(i, rows, ps):
            bgv = ps[0]
            return [merge_math(rows[0], rows[1], rows[2], rows[3], bgv[0:1], bgv[1:2], bgv[2:3])], []
        (merged,), _ = _rowwise(fn_merge, [_whole(pa), _whole(pb), _whole(pc), (h, N_BRANCHES * D, cb("g"))], [p["bg"]],
                                [(D, BF16)], [], TMW, "merge_fwd")
        o = _mm(merged, p["w_o"], "nn", F32, "mm_o")

        def ln_res_math(x_, o_, g_, b_):
            return _layer_norm(DN_ALPHA * x_ + o_, g_, b_)

        def fn_ln(i, rows, ps):
            y = ln_res_math(rows[0], rows[1], ps[0], ps[1])
            return [y, y], []
        (x1, x1b), _ = _rowwise(fn_ln, [_whole(xs), _whole(o)], [p["l1g"], p["l1b"]], [(D, F32), (D, BF16)], [], TM, "ln1_fwd")
        up = _mm(x1b, p["w_up"], "nn", BF16, "mm_up")
        cv_ = _conv_fwd(up, p["cw"], p["cbias"])

        def glu_math(cg, cvv):
            return jax.nn.silu(cg.astype(F32)) * cvv.astype(F32)

        def fn_glu(i, rows, ps):
            return [glu_math(rows[0], rows[1])], []
        (act,), _ = _rowwise(fn_glu, [(cv_, FF, 0), (cv_, FF, 1)], [], [(FF, BF16)], [], TMW, "glu_fwd")
        dn = _mm(act, p["w_down"], "nn", F32, "mm_down")
        (x2, x2b), _ = _rowwise(fn_ln, [_whole(x1), _whole(dn)], [p["l2g"], p["l2b"]], [(D, F32), (D, BF16)], [], TM, "ln2_fwd")
        saved.append(dict(p=p, x0=xs, x0b=xb, h=h, y_a=y_a, cqn=cqn, ckvn=ckvn, q_full=q_full, kv=kv, qr=qr, kr=kr, y_b=y_b,
                          lse=lse, y_c=y_c, pa=pa, pb=pb, pc=pc, merged=merged, o=o, x1=x1, x1b=x1b, up=up, cv=cv_, act=act,
                          dn=dn))
        xs, xb = x2, x2b

    def fn_loss(i, rows, ps):
        diff = rows[0] - rows[1]
        part = jnp.sum(jnp.mean(jnp.square(diff), axis=-1, keepdims=True), axis=0, keepdims=True)
        return [diff * (1.0 / D)], [jnp.broadcast_to(part, (8, LANES))]
    (dx,), (loss_acc,) = _rowwise(fn_loss, [_whole(xs), _whole(loss_target[0])], [], [(D, F32)], [(8, LANES)], TM, "loss")
    loss = lax.psum(0.5 * loss_acc[0, 0], ("x", "y", "c"))

    gbig = {k: [None] * L for k in BIG}
    gsmall = {k: [None] * L for k in SMALL}
    reduced = {}
    pending = None
    for l in reversed(range(L)):
        s = saved[l]
        p = s["p"]

        def fn_ln_bwd(i, rows, ps):
            _, vjp = jax.vjp(ln_res_math, rows[0], rows[1], ps[0], ps[1])
            dx_, do_, dg_, db_ = vjp(rows[2])
            return [dx_, do_], [dg_, db_]
        (dx1_res, ddn), (g_l2g, g_l2b) = _rowwise(fn_ln_bwd, [_whole(s["x1"]), _whole(s["dn"]), _whole(dx)], [p["l2g"], p["l2b"]],
                                                  [(D, F32), (D, BF16)], [(1, D), (1, D)], TM, "ln2_bwd")
        gsmall["ln2_g"][l], gsmall["ln2_b"][l] = g_l2g, g_l2b
        row_chunks = lambda g: g.reshape(N_CHIPS, g.shape[0] // N_CHIPS, g.shape[1])
        gbig["w_down"][l] = row_chunks(_mm(s["act"], ddn, "tn", F32, "mm_dw_down"))
        dact = _mm(ddn, p["w_down"], "nt", BF16, "mm_dact")

        def fn_glu_bwd(i, rows, ps):
            _, vjp = jax.vjp(glu_math, rows[0], rows[1])
            dcg, dcv = vjp(rows[2].astype(F32))
            return [jnp.concatenate([dcg, dcv], axis=1)], []
        (dc,), _ = _rowwise(fn_glu_bwd, [(s["cv"], FF, 0), (s["cv"], FF, 1), _whole(dact)], [], [(FF2, BF16)], [], TMW, "glu_bwd")
        dup, g_cw, g_cb = _conv_bwd(s["up"], dc, p["cw"])
        gsmall["conv_w"][l], gsmall["conv_b"][l] = g_cw, g_cb
        if pending is None:
            gbig["w_up"][l] = _mm(s["x1b"], dup, "tn", F32, "mm_dw_up", col_chunks=N_CHIPS, tm=2048)
        else:
            gbig["w_up"][l], from_sibling = _mm(s["x1b"], dup, "tn", F32, "mm_dw_up", col_chunks=N_CHIPS, tm=2048,
                                                carry=_swap_stage(pending))
            partial = [_add_half(g, r, cc, f"rs_add_{k}_l{l + 1}") for k, g, r in zip(BIG, pending, from_sibling)]
        dx1 = _mm(dup, p["w_up"], "nt", F32, "mm_dx1", add=dx1_res)
        (dx0_res, do_), (g_l1g, g_l1b) = _rowwise(fn_ln_bwd, [_whole(s["x0"]), _whole(s["o"]), _whole(dx1)], [p["l1g"], p["l1b"]],
                                                  [(D, F32), (D, BF16)], [(1, D), (1, D)], TM, "ln1_bwd")
        gsmall["ln1_g"][l], gsmall["ln1_b"][l] = g_l1g, g_l1b
        gbig["w_o"][l] = row_chunks(_mm(s["merged"], do_, "tn", F32, "mm_dw_o"))
        dmerged = _mm(do_, p["w_o"], "nt", F32, "mm_dmerged")

        def fn_merge_bwd(i, rows, ps):
            bgv = ps[0]
            _, vjp = jax.vjp(merge_math, rows[0], rows[1], rows[2], rows[3], bgv[0:1], bgv[1:2], bgv[2:3])
            dpa, dpb, dpc, dg_, db0, db1, db2 = vjp(rows[4])
            return [dpa, dpb, dpc, dg_], [db0, db1, db2]
        (dpa, dpb, dpc, dgl), (db0, db1, db2) = _rowwise(
            fn_merge_bwd, [_whole(s["pa"]), _whole(s["pb"]), _whole(s["pc"]), (s["h"], N_BRANCHES * D, cb("g")), _whole(dmerged)],
            [p["bg"]], [(D, BF16), (D, BF16), (D, BF16), (N_BRANCHES * D, BF16)], [(1, D)] * 3, TMW, "merge_bwd")
        gsmall["b_gate"][l] = jnp.concatenate([db0, db1, db2], axis=0)
        gbig["w_proj_a"][l] = _mm(s["y_a"], dpa, "tn", F32, "mm_dw_pa", col_chunks=N_CHIPS)
        gbig["w_proj_b"][l] = row_chunks(_mm(s["y_b"], dpb, "tn", F32, "mm_dw_pb"))
        gbig["w_proj_c"][l] = _mm(s["y_c"], dpc, "tn", F32, "mm_dw_pc", col_chunks=N_CHIPS)
        dy_a = _mm(dpa, p["w_pa"], "nt", BF16, "mm_dy_a")
        dy_b = _mm(dpb, p["w_pb"], "nt", BF16, "mm_dy_b")
        dy_c = _mm(dpc, p["w_pc"], "nt", BF16, "mm_dy_c")
        dh_c, g_sg, g_sb, g_sw, g_sb3 = _sgu_bwd(s["h"], cb("hu"), cb("hv"), dy_c, p["sg"], p["sb"], p["sw"], p["sb3"])
        gsmall["sgu_ln_g"][l], gsmall["sgu_ln_b"][l], gsmall["sgu_w"][l] = g_sg, g_sb, g_sw
        gsmall["sgu_b"][l] = g_sb3.reshape(SGU_GROUPS, SGU_CHUNK)
        dqa, dka, dva, g_sinks = _swa_bwd(s["h"], cb("qa"), cb("ka"), cb("va"), p["sinks"], dy_a)
        gsmall["sinks"][l] = g_sinks
        delta = _mla_delta(dy_b, s["y_b"], T)
        dqn, dqr = _mla_bwd_dq(s["q_full"], s["qr"], s["kv"], s["kr"], dy_b, s["lse"], delta, T)
        (dkn, dv, dkr_heads), arrived = _mla_bwd_dkv(
            s["q_full"], s["qr"], s["kv"], s["kr"], dy_b, s["lse"], delta, T,
            carry=None if pending is None else _scatter_stage([p16 for _, p16 in partial]))
        if pending is not None:
            for k, (p32, _), arr in zip(BIG, partial, arrived):
                reduced[k] = _sum_into(p32, arr, chip, cc, l + 1, L, reduced.get(k), f"rs_sum_{k}_l{l + 1}")
        dkr = _sum_heads(dkr_heads, TM)
        dqr_raw, dkr_raw = _rope_call(dqr, MLA_HEADS * LANES, 0, dkr, LANES, 0, cos_t, sin_t, -1.0, TM, "rope_bwd")
        dq_full = jnp.concatenate([dqn, dqr_raw], axis=1)
        dkv = jnp.concatenate([dkn, dv], axis=1)
        gbig["w_uq"][l] = _col_chunks(_unperm_w_uq(_mm(s["cqn"], dq_full, "tn", F32, "mm_dw_uq")))
        gbig["w_ukv"][l] = _col_chunks(_unperm_w_ukv(_mm(s["ckvn"], dkv, "tn", F32, "mm_dw_ukv")))
        dcqn = _mm(dq_full, p["w_uq"], "nt", F32, "mm_dcqn")
        dckvn = _mm(dkv, p["w_ukv"], "nt", F32, "mm_dckvn")

        def fn_rms_bwd(i, rows, ps):
            _, vjp1 = jax.vjp(lambda c_, g_: _rms_norm(c_.astype(F32), g_), rows[0], ps[0])
            _, vjp2 = jax.vjp(lambda c_, g_: _rms_norm(c_.astype(F32), g_), rows[1], ps[1])
            d1, dg1 = vjp1(rows[2])
            d2, dg2 = vjp2(rows[3])
            return [d1, d2], [dg1, dg2]
        (dcq, dckv), (g_qg, g_kvg) = _rowwise(
            fn_rms_bwd, [(s["h"], MLA_Q_RANK, cb("cq")), (s["h"], MLA_KV_RANK, cb("ckv")), _whole(dcqn), _whole(dckvn)],
            [p["qg"], p["kvg"]], [(MLA_Q_RANK, BF16), (MLA_KV_RANK, BF16)], [(1, MLA_Q_RANK), (1, MLA_KV_RANK)], TM, "mla_rms_bwd")
        gsmall["q_norm_g"][l], gsmall["kv_norm_g"][l] = g_qg, g_kvg
        tail = jnp.zeros((S, NP - lay["kr"][0] - LANES), BF16)
        dh = jnp.concatenate([dgl, dqa, dcq, dckv, dh_c, dka, dva, dkr_raw, tail], axis=1)
        gbig["w_in"][l] = _col_chunks(_unperm_w_in(_mm(s["x0b"], dh, "tn", F32, "mm_dw_in"), lay, D))
        dx = _mm(dh, p["w_in"], "nt", F32, "mm_dx0", add=dx0_res)

        pending = [gbig[k][l] for k in BIG]

    from_sibling = _run_stage(_swap_stage(pending), "rs_swap_halves_l0")
    partial = [_add_half(g, r, cc, f"rs_add_{k}_l0") for k, g, r in zip(BIG, pending, from_sibling)]
    arrived = _run_stage(_scatter_stage([p16 for _, p16 in partial]), "rs_scatter_chips_l0")
    for k, (p32, _), arr in zip(BIG, partial, arrived):
        reduced[k] = _sum_into(p32, arr, chip, cc, 0, L, reduced.get(k), f"rs_sum_{k}_l0")

    grad_x = dx.reshape(x.shape)
    g_big = dict(zip(BIG, _run_stage(_join_stage([reduced[k] for k in BIG]), "rs_join_halves")))

    small_shapes = {k: (small_sharded_full[k] if k in SMALL_SHARDED else tuple(W[k].shape)) for k in SMALL}
    sv = jnp.concatenate([jnp.stack(gsmall[k]).reshape(-1) for k in SMALL])
    n_sv = sv.shape[0]
    sv = jnp.pad(sv, (0, -n_sv % (8 * LANES))).reshape(-1, LANES)
    sv = _all_reduce_small(sv).reshape(-1)
    g_small, o_ = {}, 0
    for k in SMALL:
        n = int(np.prod(small_shapes[k]))
        g = sv[o_:o_ + n].reshape(small_shapes[k])
        if k in SMALL_SHARDED:
            g = lax.dynamic_slice_in_dim(g, chip * W[k].shape[-1], W[k].shape[-1], axis=-1)
        g_small[k] = g
        o_ += n

    delta, new_m, new_v = {}, {}, {}
    for k in BIG:
        shp = shards[k]
        v2 = lambda t: t.reshape(shp[0] * shp[1], shp[2])
        d_, m_, v_ = _adamw(v2(W[k]), v2(g_big[k]), v2(Mo[k]), v2(Vo[k]), "adamw_" + k)
        delta[k], new_m[k], new_v[k] = d_.reshape(shp), m_.reshape(shp), v_.reshape(shp)
    pack = lambda t: jnp.concatenate([t[k].reshape(-1) for k in SMALL])
    n_small = sum(int(np.prod(W[k].shape)) for k in SMALL)
    pad2 = lambda t: jnp.pad(t, (0, -n_small % (8 * LANES))).reshape(-1, LANES)
    d_, m_, v_ = _adamw(pad2(pack(W)), pad2(pack(g_small)), pad2(pack(Mo)), pad2(pack(Vo)), "adamw_small")
    o_ = 0
    for k in SMALL:
        n = int(np.prod(W[k].shape))
        take = lambda t: t.reshape(-1)[o_:o_ + n].reshape(W[k].shape)
        delta[k], new_m[k], new_v[k] = take(d_), take(m_), take(v_)
        o_ += n

    grads = {**g_big, **g_small}
    return (loss, grad_x, *[grads[k] for k in WEIGHTS], *[delta[k] for k in WEIGHTS], *[new_m[k] for k in WEIGHTS],
            *[new_v[k] for k in WEIGHTS])
```

```python
import functools
import math

import jax
import jax.numpy as jnp
import numpy as np
from jax import lax
from jax.experimental import pallas as pl
from jax.experimental.pallas import tpu as pltpu

F32, BF16 = jnp.float32, jnp.bfloat16
SDS = jax.ShapeDtypeStruct
MESH = pl.DeviceIdType.MESH

SWA_Q_HEADS, SWA_KV_HEADS, SWA_HEAD_DIM, SWA_BLOCK = 16, 2, 64, 128
MLA_HEADS, MLA_NOPE, MLA_ROPE, MLA_V = 16, 128, 64, 128
MLA_Q_RANK, MLA_KV_RANK = 512, 512
ROPE_THETA = 10000.0
SGU_GROUPS, SGU_GROUP_DIM, SGU_CHUNK = 8, 128, 128
SGU_WIDTH = SGU_GROUPS * SGU_GROUP_DIM
A_Q = SWA_Q_HEADS * SWA_HEAD_DIM
A_KV = SWA_KV_HEADS * SWA_HEAD_DIM
N_BRANCHES = 3
DEPTH = 2
EPS = 1e-5
MASK_VALUE = -1e30
DN_ALPHA = (2 * DEPTH) ** 0.25
ADAM_LR, ADAM_B1, ADAM_B2, ADAM_EPS, ADAM_WD, ADAM_STEP = 0.001, 0.9, 0.999, 1e-08, 0.01, 10
N_CHIPS = 4

LANES = 128
VMEM_LIMIT = 48 * 1024 * 1024

BIG = ["w_in", "w_uq", "w_ukv", "w_proj_a", "w_proj_b", "w_proj_c", "w_o", "w_up", "w_down"]
ROW_SHARDED = {"w_proj_b", "w_o", "w_down"}
LATE = ["w_in", "w_uq", "w_ukv"]
EARLY = [k for k in BIG if k not in LATE]
SMALL = ["b_gate", "sinks", "q_norm_g", "kv_norm_g", "sgu_ln_g", "sgu_ln_b", "sgu_w", "sgu_b", "ln1_g", "ln1_b",
         "conv_w", "conv_b", "ln2_g", "ln2_b"]
SMALL_SHARDED = {"b_gate", "conv_w"}
WEIGHTS = ["w_in", "b_gate", "sinks", "q_norm_g", "kv_norm_g", "w_uq", "w_ukv", "sgu_ln_g", "sgu_ln_b", "sgu_w", "sgu_b",
           "w_proj_a", "w_proj_b", "w_proj_c", "w_o", "ln1_g", "ln1_b", "w_up", "conv_w", "conv_b", "w_down", "ln2_g", "ln2_b"]


def _pcall(body, **kw):
    return pl.pallas_call(body, **kw)


def _params(sem=None):
    return pltpu.CompilerParams(dimension_semantics=sem, vmem_limit_bytes=VMEM_LIMIT)


def _tile(dim, pref, align=LANES):
    t = (min(pref, dim) // align) * align
    while t >= align:
        if dim % t == 0:
            return t
        t -= align
    return dim


def _mm(a, b, mode, out_dtype, name, add=None, tm=1024, tn=512, tk=2048, col_chunks=1, carry=None):
    if mode == "nn":
        (M, K), (K2, N) = a.shape, b.shape
    elif mode == "nt":
        (M, K), (N, K2) = a.shape, b.shape
    else:
        (K, M), (K2, N) = a.shape, b.shape
    assert K == K2, (a.shape, b.shape, mode)
    assert N % col_chunks == 0
    tm, tn, tk = _tile(M, tm), _tile(N // col_chunks, tn), _tile(K, tk)
    assert (N // col_chunks) % tn == 0
    per_chunk = (N // col_chunks) // tn
    nk = K // tk
    if mode == "tn":
        a_spec = pl.BlockSpec((tk, tm), lambda i, j, k: (k, i))
    else:
        a_spec = pl.BlockSpec((tm, tk), lambda i, j, k: (i, k))
    if mode == "nt":
        b_spec = pl.BlockSpec((tn, tk), lambda i, j, k: (j, k))
    else:
        b_spec = pl.BlockSpec((tk, tn), lambda i, j, k: (k, j))
    dn = {"nn": (((1,), (0,)), ((), ())), "nt": (((1,), (1,)), ((), ())), "tn": (((0,), (0,)), ((), ()))}[mode]
    chunked = col_chunks > 1
    if chunked:
        assert add is None
        o_spec = pl.BlockSpec((1, tm, tn), lambda i, j, k: (lax.div(j, per_chunk), i, lax.rem(j, per_chunk)))
        out_shape = SDS((col_chunks, M, N // col_chunks), out_dtype)
    else:
        o_spec = pl.BlockSpec((tm, tn), lambda i, j, k: (i, j))
        out_shape = SDS((M, N), out_dtype)
    has_add = add is not None

    def body(*refs):
        if has_add:
            a_ref, b_ref, add_ref, o_ref, acc_ref = refs
        else:
            a_ref, b_ref, o_ref, acc_ref = refs
        k = pl.program_id(2)

        @pl.when(k == 0)
        def _():
            acc_ref[...] = jnp.zeros_like(acc_ref)

        acc_ref[...] += lax.dot_general(a_ref[...].astype(BF16), b_ref[...].astype(BF16), dn,
                                        preferred_element_type=F32)

        @pl.when(k == nk - 1)
        def _():
            r = acc_ref[...]
            if has_add:
                r = r + add_ref[...].astype(F32)
            if chunked:
                o_ref[0] = r.astype(o_ref.dtype)
            else:
                o_ref[...] = r.astype(o_ref.dtype)

    ins = [a, b] + ([add] if has_add else [])
    in_specs = [a_spec, b_spec] + ([o_spec] if has_add else [])
    (out,), carried = _call(body, name, (M // tm, N // tn, nk), 0, in_specs, [o_spec], [out_shape], [pltpu.VMEM((tm, tn), F32)],
                            ins, ("parallel", "parallel", "arbitrary"), carry)
    return out if carry is None else (out, carried)


def _rowwise(fn, rows, params, row_outs, acc_outs, tm, name):
    n_rows = rows[0][0].shape[0]
    assert n_rows % tm == 0
    nr, npar, no = len(rows), len(params), len(row_outs)

    def body(*refs):
        i = pl.program_id(0)
        r, p = refs[:nr], refs[nr:nr + npar]
        o, acc = refs[nr + npar:nr + npar + no], refs[nr + npar + no:]
        outs, sums = fn(i, [x[...] for x in r], [x[...] for x in p])
        for ref, val in zip(o, outs, strict=True):
            ref[...] = val.astype(ref.dtype)
        if acc:
            @pl.when(i == 0)
            def _():
                for ref in acc:
                    ref[...] = jnp.zeros_like(ref)
            for ref, val in zip(acc, sums, strict=True):
                ref[...] += val.astype(F32)

    def full(shape):
        nd = len(shape)
        return pl.BlockSpec(tuple(shape), lambda i: (0,) * nd)

    in_specs = [pl.BlockSpec((tm, w), (lambda i, cb=cb: (i, cb))) for (_, w, cb) in rows] + [full(p.shape) for p in params]
    out_specs = [pl.BlockSpec((tm, w), lambda i: (i, 0)) for (w, _) in row_outs] + [full(s) for s in acc_outs]
    out_shape = [SDS((n_rows, w), dt) for (w, dt) in row_outs] + [SDS(tuple(s), F32) for s in acc_outs]
    res = _pcall(body, name=name, out_shape=out_shape, grid=(n_rows // tm,), in_specs=in_specs, out_specs=out_specs,
                 compiler_params=_params(("arbitrary",)))(*[r[0] for r in rows], *params)
    return list(res[:no]), list(res[no:])


def _whole(a):
    return (a, a.shape[1], 0)


def _gelu(x):
    return 0.5 * x * (1.0 + lax.erf(x * (1.0 / math.sqrt(2.0))))


def _layer_norm(x, g, b):
    mu = x.mean(-1, keepdims=True)
    var = jnp.mean(jnp.square(x - mu), -1, keepdims=True)
    return (x - mu) * lax.rsqrt(var + EPS) * g + b


def _rms_norm(x, g):
    return x * lax.rsqrt(jnp.mean(jnp.square(x), -1, keepdims=True) + EPS) * g


def _sgu_math(hu, hv, ln_g, ln_b, ws, bs):
    u = _gelu(hu.astype(F32))
    vn = _layer_norm(_gelu(hv.astype(F32)), ln_g, ln_b)
    r = lax.broadcasted_iota(jnp.int32, (SGU_CHUNK, SGU_CHUNK), 0)
    c = lax.broadcasted_iota(jnp.int32, (SGU_CHUNK, SGU_CHUNK), 1)
    outs = []
    for g in range(SGU_GROUPS):
        w = jnp.where(r >= c, ws[g], 0.0).astype(BF16)
        vg = vn[:, g * SGU_GROUP_DIM:(g + 1) * SGU_GROUP_DIM].astype(BF16)
        outs.append(jnp.dot(w, vg, preferred_element_type=F32) + bs[g])
    return u * jnp.concatenate(outs, axis=1)


def _sgu_fwd(h, cu, cv, ln_g, ln_b, w, b3):
    def fn(i, rows, ps):
        g_, b_, w_, b3_ = ps
        y = _sgu_math(rows[0], rows[1], g_, b_, [w_[g] for g in range(SGU_GROUPS)], [b3_[g] for g in range(SGU_GROUPS)])
        return [y], []
    (y,), _ = _rowwise(fn, [(h, SGU_WIDTH, cu), (h, SGU_WIDTH, cv)], [ln_g, ln_b, w, b3], [(SGU_WIDTH, BF16)], [],
                       SGU_CHUNK, "sgu_fwd")
    return y


def _sgu_bwd(h, cu, cv, dy, ln_g, ln_b, w, b3):
    nd = 2 * SGU_WIDTH

    def body(hu_ref, hv_ref, dy_ref, g_ref, b_ref, w_ref, b3_ref, dh_ref, dg_ref, db_ref, dw_ref, db3_ref):
        i = pl.program_id(0)

        @pl.when(i == 0)
        def _():
            dg_ref[...] = jnp.zeros_like(dg_ref)
            db_ref[...] = jnp.zeros_like(db_ref)
            dw_ref[...] = jnp.zeros_like(dw_ref)
            db3_ref[...] = jnp.zeros_like(db3_ref)

        ws = [w_ref[g] for g in range(SGU_GROUPS)]
        bs = [b3_ref[g] for g in range(SGU_GROUPS)]
        _, vjp = jax.vjp(_sgu_math, hu_ref[...], hv_ref[...], g_ref[...], b_ref[...], ws, bs)
        dhu, dhv, dg, db, dws, dbs = vjp(dy_ref[...].astype(F32))
        dh_ref[...] = jnp.concatenate([dhu, dhv], axis=1).astype(dh_ref.dtype)
        dg_ref[...] += dg
        db_ref[...] += db
        for g in range(SGU_GROUPS):
            dw_ref[g] += dws[g]
            db3_ref[g] += dbs[g]

    n = h.shape[0]
    blk = lambda cb: pl.BlockSpec((SGU_CHUNK, SGU_WIDTH), lambda i, cb=cb: (i, cb))
    full = lambda s: pl.BlockSpec(tuple(s), lambda i: (0,) * len(s))
    return _pcall(
        body, name="sgu_bwd", grid=(n // SGU_CHUNK,),
        out_shape=[SDS((n, nd), BF16), SDS(ln_g.shape, F32), SDS(ln_b.shape, F32), SDS(w.shape, F32), SDS(b3.shape, F32)],
        in_specs=[blk(cu), blk(cv), blk(0), full(ln_g.shape), full(ln_b.shape), full(w.shape), full(b3.shape)],
        out_specs=[pl.BlockSpec((SGU_CHUNK, nd), lambda i: (i, 0)), full(ln_g.shape), full(ln_b.shape), full(w.shape),
                   full(b3.shape)],
        compiler_params=_params(("arbitrary",)))(h, h, dy, ln_g, ln_b, w, b3)


def _swa_math(q, kp, kc, vp, vc, sinks, not_first):
    kw = jnp.concatenate([kp, kc], axis=0).astype(BF16)
    vw = jnp.concatenate([vp, vc], axis=0).astype(BF16)
    qb = q.astype(BF16)
    q_off = lax.broadcasted_iota(jnp.int32, (SWA_BLOCK, 2 * SWA_BLOCK), 0) + SWA_BLOCK
    k_off = lax.broadcasted_iota(jnp.int32, (SWA_BLOCK, 2 * SWA_BLOCK), 1)
    rel = q_off - k_off
    valid = (rel >= 0) & (rel < SWA_BLOCK) & (not_first | (k_off >= SWA_BLOCK))
    G = SWA_Q_HEADS // SWA_KV_HEADS
    outs = []
    for head in range(SWA_Q_HEADS):
        hk = head // G
        qh = qb[:, head * SWA_HEAD_DIM:(head + 1) * SWA_HEAD_DIM]
        kh = kw[:, hk * SWA_HEAD_DIM:(hk + 1) * SWA_HEAD_DIM]
        vh = vw[:, hk * SWA_HEAD_DIM:(hk + 1) * SWA_HEAD_DIM]
        s = lax.dot_general(qh, kh, (((1,), (1,)), ((), ())), preferred_element_type=F32) * (SWA_HEAD_DIM ** -0.5)
        s = jnp.where(valid, s, MASK_VALUE)
        sink = sinks[:, head:head + 1]
        m = jnp.maximum(s.max(-1, keepdims=True), sink)
        p = jnp.exp(s - m)
        p = (p / (p.sum(-1, keepdims=True) + jnp.exp(sink - m))).astype(BF16)
        outs.append(jnp.dot(p, vh, preferred_element_type=F32))
    return jnp.concatenate(outs, axis=1)


def _swa_fwd(h, cq, ck, cv, sinks):
    n = h.shape[0]
    nb = n // SWA_BLOCK

    def body(q_ref, kp_ref, kc_ref, vp_ref, vc_ref, s_ref, o_ref):
        i = pl.program_id(0)
        f = lambda x: x[...].astype(F32)
        o_ref[...] = _swa_math(f(q_ref), f(kp_ref), f(kc_ref), f(vp_ref), f(vc_ref), s_ref[...], i > 0).astype(o_ref.dtype)

    prev = lambda cb: pl.BlockSpec((SWA_BLOCK, A_KV), lambda i, cb=cb: (jnp.maximum(i - 1, 0), cb))
    cur = lambda cb: pl.BlockSpec((SWA_BLOCK, A_KV), lambda i, cb=cb: (i, cb))
    return _pcall(body, name="swa_fwd", grid=(nb,), out_shape=SDS((n, A_Q), BF16),
                  in_specs=[pl.BlockSpec((SWA_BLOCK, A_Q), lambda i: (i, cq)), prev(ck), cur(ck), prev(cv), cur(cv),
                            pl.BlockSpec((1, SWA_Q_HEADS), lambda i: (0, 0))],
                  out_specs=pl.BlockSpec((SWA_BLOCK, A_Q), lambda i: (i, 0)),
                  compiler_params=_params(("arbitrary",)))(h, h, h, h, h, sinks)


def _swa_bwd(h, cq, ck, cv, sinks, dy, carry=None):
    n = h.shape[0]
    nb = n // SWA_BLOCK

    def body(q_ref, kp_ref, kc_ref, vp_ref, vc_ref, s_ref, dy_ref, dq_ref, dk_ref, dv_ref, ds_ref, ck_ref, cv_ref):
        r = pl.program_id(0)
        blk = nb - 1 - r

        @pl.when(r == 0)
        def _():
            ds_ref[...] = jnp.zeros_like(ds_ref)
            ck_ref[...] = jnp.zeros_like(ck_ref)
            cv_ref[...] = jnp.zeros_like(cv_ref)

        f = lambda x: x[...].astype(F32)
        not_first = blk > 0
        _, vjp = jax.vjp(lambda q, kp, kc, vp, vc, s: _swa_math(q, kp, kc, vp, vc, s, not_first),
                         f(q_ref), f(kp_ref), f(kc_ref), f(vp_ref), f(vc_ref), s_ref[...])
        dq, dkp, dkc, dvp, dvc, dsk = vjp(f(dy_ref))
        dq_ref[...] = dq.astype(dq_ref.dtype)
        dk_ref[...] = (dkc + ck_ref[...]).astype(dk_ref.dtype)
        dv_ref[...] = (dvc + cv_ref[...]).astype(dv_ref.dtype)
        ck_ref[...] = dkp
        cv_ref[...] = dvp
        ds_ref[...] += dsk

    rev = lambda i: nb - 1 - i
    prev = lambda cb: pl.BlockSpec((SWA_BLOCK, A_KV), lambda i, cb=cb: (jnp.maximum(rev(i) - 1, 0), cb))
    cur = lambda cb: pl.BlockSpec((SWA_BLOCK, A_KV), lambda i, cb=cb: (rev(i), cb))
    return _call(
        body, "swa_bwd", (nb,), 0,
        [pl.BlockSpec((SWA_BLOCK, A_Q), lambda i: (rev(i), cq)), prev(ck), cur(ck), prev(cv), cur(cv),
         pl.BlockSpec((1, SWA_Q_HEADS), lambda i: (0, 0)), pl.BlockSpec((SWA_BLOCK, A_Q), lambda i: (rev(i), 0))],
        [pl.BlockSpec((SWA_BLOCK, A_Q), lambda i: (rev(i), 0)), pl.BlockSpec((SWA_BLOCK, A_KV), lambda i: (rev(i), 0)),
         pl.BlockSpec((SWA_BLOCK, A_KV), lambda i: (rev(i), 0)), pl.BlockSpec((1, SWA_Q_HEADS), lambda i: (0, 0))],
        [SDS((n, A_Q), BF16), SDS((n, A_KV), BF16), SDS((n, A_KV), BF16), SDS((1, SWA_Q_HEADS), F32)],
        [pltpu.VMEM((SWA_BLOCK, A_KV), F32), pltpu.VMEM((SWA_BLOCK, A_KV), F32)],
        [h, h, h, h, h, sinks, dy], ("arbitrary",), carry)


def _rope(x, cos, sin, sign):
    w = x.shape[1]
    reps = w // LANES
    ct = jnp.tile(cos, (1, reps)) if reps > 1 else cos
    st = jnp.tile(sin, (1, reps)) if reps > 1 else sin
    fwd = pltpu.roll(x, MLA_ROPE // 2, axis=1)
    bwd = pltpu.roll(x, w - MLA_ROPE // 2, axis=1)
    lane = lax.broadcasted_iota(jnp.int32, x.shape, 1) % LANES
    rot = jnp.where(lane < MLA_ROPE // 2, -bwd, fwd)
    return x * ct + sign * (rot * st)


def _rope_call(a, wa, ca, b, wb, cb, cos, sin, sign, tm, name):
    def fn(i, rows, ps):
        xa, xb, c_, s_ = rows
        return [_rope(xa.astype(F32), c_, s_, sign), _rope(xb.astype(F32), c_, s_, sign)], []
    (ra, rb), _ = _rowwise(fn, [(a, wa, ca), (b, wb, cb), _whole(cos), _whole(sin)], [], [(wa, BF16), (wb, BF16)], [], tm, name)
    return ra, rb


MLA_SCALE = (MLA_NOPE + MLA_ROPE) ** -0.5
LOG2E = math.log2(math.e)


def _mla_scores(qn_ref, qr_ref, kn_ref, kr_ref, masked):
    q = jnp.concatenate([qn_ref[...], qr_ref[...]], axis=1)
    k = jnp.concatenate([kn_ref[...], kr_ref[...]], axis=1)
    s = lax.dot_general(q, k, (((1,), (1,)), ((), ())), preferred_element_type=F32)
    if masked:
        row = lax.broadcasted_iota(jnp.int32, s.shape, 0)
        col = lax.broadcasted_iota(jnp.int32, s.shape, 1)
        s = jnp.where(col <= row, s, MASK_VALUE)
    return s, q, k


def _causal_pairs(nq, by_query):
    if by_query:
        pairs = [(i, j) for i in range(nq) for j in range(i + 1)]
    else:
        pairs = [(i, j) for j in range(nq) for i in range(j, nq)]
    return (jnp.asarray(np.array([p[0] for p in pairs], np.int32)), jnp.asarray(np.array([p[1] for p in pairs], np.int32)),
            len(pairs))


def _mla_fwd(q_full, qr, kv, kr, T, carry=None):
    n = q_full.shape[0]
    nq = n // T
    H = MLA_HEADS
    qi, kj, npairs = _causal_pairs(nq, True)

    def body(qi_ref, kj_ref, qn_ref, qr_ref, kn_ref, v_ref, kr_ref, y_ref, lse_ref, m_ref, l_ref, acc_ref):
        t = pl.program_id(1)
        i, j = qi_ref[t], kj_ref[t]

        @pl.when(j == 0)
        def _():
            m_ref[...] = jnp.full_like(m_ref, MASK_VALUE)
            l_ref[...] = jnp.zeros_like(l_ref)
            acc_ref[...] = jnp.zeros_like(acc_ref)

        def update(masked):
            s, _, _ = _mla_scores(qn_ref, qr_ref, kn_ref, kr_ref, masked)
            m_prev = m_ref[...]
            m_new = jnp.maximum(m_prev, s.max(-1, keepdims=True))
            p = jnp.exp2((s - m_new[:, :1]) * (MLA_SCALE * LOG2E))
            alpha = jnp.exp2((m_prev - m_new) * (MLA_SCALE * LOG2E))
            l_ref[...] = alpha * l_ref[...] + p.sum(-1, keepdims=True)
            acc_ref[...] = alpha * acc_ref[...] + jnp.dot(p.astype(BF16), v_ref[...], preferred_element_type=F32)
            m_ref[...] = m_new

        @pl.when(j < i)
        def _():
            update(False)

        @pl.when(j == i)
        def _():
            update(True)
            y_ref[...] = (acc_ref[...] / l_ref[...]).astype(y_ref.dtype)
            lse_ref[0] = m_ref[...] * (MLA_SCALE * LOG2E) + jnp.log2(l_ref[...])

    qspec = lambda off: pl.BlockSpec((T, LANES), lambda h, t, qi, kj, off=off: (qi[t], off + h))
    kspec = lambda off: pl.BlockSpec((T, LANES), lambda h, t, qi, kj, off=off: (kj[t], off + h))
    return _call(
        body, "mla_fwd", (H, npairs), 2,
        [qspec(0), qspec(0), kspec(0), kspec(H), pl.BlockSpec((T, LANES), lambda h, t, qi, kj: (kj[t], 0))],
        [pl.BlockSpec((T, LANES), lambda h, t, qi, kj: (qi[t], h)), pl.BlockSpec((1, T, LANES), lambda h, t, qi, kj: (h, qi[t], 0))],
        [SDS((n, H * MLA_V), BF16), SDS((H, n, LANES), F32)], [pltpu.VMEM((T, LANES), F32)] * 3,
        [qi, kj, q_full, qr, kv, kv, kr], ("parallel", "arbitrary"), carry)


def _mla_delta(dy, y, T):
    n = y.shape[0]
    H = MLA_HEADS

    def body(dy_ref, y_ref, d_ref):
        d = jnp.sum(dy_ref[...].astype(F32) * y_ref[...].astype(F32), axis=-1, keepdims=True)
        d_ref[0] = jnp.broadcast_to(d, (T, LANES))

    spec = pl.BlockSpec((T, LANES), lambda h, i: (i, h))
    return _pcall(body, name="mla_delta", grid=(H, n // T), out_shape=SDS((H, n, LANES), F32), in_specs=[spec, spec],
                  out_specs=pl.BlockSpec((1, T, LANES), lambda h, i: (h, i, 0)),
                  compiler_params=_params(("parallel", "parallel")))(dy, y)


def _mla_bwd_dq(q_full, qr, kv, kr, dy, lse, delta, T, carry=None):
    n = q_full.shape[0]
    nq = n // T
    H = MLA_HEADS

    qi, kj, npairs = _causal_pairs(nq, True)

    def body(qi_ref, kj_ref, qn_ref, qr_ref, kn_ref, v_ref, kr_ref, dy_ref, lse_ref, dl_ref, dqn_ref, dqr_ref, acc_ref):
        t = pl.program_id(1)
        i, j = qi_ref[t], kj_ref[t]

        @pl.when(j == 0)
        def _():
            acc_ref[...] = jnp.zeros_like(acc_ref)

        def update(masked):
            s, _, k = _mla_scores(qn_ref, qr_ref, kn_ref, kr_ref, masked)
            p = jnp.exp2(s * (MLA_SCALE * LOG2E) - lse_ref[0][:, :1])
            dp = lax.dot_general(dy_ref[...], v_ref[...], (((1,), (1,)), ((), ())), preferred_element_type=F32)
            ds = p * (dp - dl_ref[0][:, :1])
            acc_ref[...] += jnp.dot(ds.astype(BF16), k, preferred_element_type=F32)

        @pl.when(j < i)
        def _():
            update(False)

        @pl.when(j == i)
        def _():
            update(True)
            dqn_ref[...] = (acc_ref[:, :LANES] * MLA_SCALE).astype(dqn_ref.dtype)
            dqr_ref[...] = (acc_ref[:, LANES:] * MLA_SCALE).astype(dqr_ref.dtype)

    qspec = lambda off: pl.BlockSpec((T, LANES), lambda h, t, qi, kj, off=off: (qi[t], off + h))
    kspec = lambda off: pl.BlockSpec((T, LANES), lambda h, t, qi, kj, off=off: (kj[t], off + h))
    stat = pl.BlockSpec((1, T, LANES), lambda h, t, qi, kj: (h, qi[t], 0))
    out = pl.BlockSpec((T, LANES), lambda h, t, qi, kj: (qi[t], h))
    return _call(
        body, "mla_bwd_dq", (H, npairs), 2,
        [qspec(0), qspec(0), kspec(0), kspec(H), pl.BlockSpec((T, LANES), lambda h, t, qi, kj: (kj[t], 0)), qspec(0), stat, stat],
        [out, out], [SDS((n, H * LANES), BF16), SDS((n, H * LANES), BF16)], [pltpu.VMEM((T, 2 * LANES), F32)],
        [qi, kj, q_full, qr, kv, kv, kr, dy, lse, delta], ("parallel", "arbitrary"), carry)


def _mla_bwd_dkv(q_full, qr, kv, kr, dy, lse, delta, T, carry=None):
    n = q_full.shape[0]
    nq = n // T
    H = MLA_HEADS

    qi, kj, npairs = _causal_pairs(nq, False)

    def body(qi_ref, kj_ref, qn_ref, qr_ref, kn_ref, v_ref, kr_ref, dy_ref, lse_ref, dl_ref, dkn_ref, dv_ref, dkr_ref,
             dk_acc, dv_acc):
        t = pl.program_id(1)
        i, j = qi_ref[t], kj_ref[t]

        @pl.when(i == j)
        def _():
            dk_acc[...] = jnp.zeros_like(dk_acc)
            dv_acc[...] = jnp.zeros_like(dv_acc)

        def update(masked):
            s, q, _ = _mla_scores(qn_ref, qr_ref, kn_ref, kr_ref, masked)
            p = jnp.exp2(s * (MLA_SCALE * LOG2E) - lse_ref[0][:, :1])
            dy = dy_ref[...]
            dv_acc[...] += lax.dot_general(p.astype(BF16), dy, (((0,), (0,)), ((), ())), preferred_element_type=F32)
            dp = lax.dot_general(dy, v_ref[...], (((1,), (1,)), ((), ())), preferred_element_type=F32)
            ds = p * (dp - dl_ref[0][:, :1])
            dk_acc[...] += lax.dot_general(ds.astype(BF16), q, (((0,), (0,)), ((), ())), preferred_element_type=F32)

        @pl.when(i == j)
        def _():
            update(True)

        @pl.when(i > j)
        def _():
            update(False)

        @pl.when(i == nq - 1)
        def _():
            dkn_ref[...] = (dk_acc[:, :LANES] * MLA_SCALE).astype(dkn_ref.dtype)
            dv_ref[...] = dv_acc[...].astype(dv_ref.dtype)
            dkr_ref[0] = dk_acc[:, LANES:] * MLA_SCALE

    qspec = lambda off: pl.BlockSpec((T, LANES), lambda h, t, qi, kj, off=off: (qi[t], off + h))
    kspec = lambda off: pl.BlockSpec((T, LANES), lambda h, t, qi, kj, off=off: (kj[t], off + h))
    stat = pl.BlockSpec((1, T, LANES), lambda h, t, qi, kj: (h, qi[t], 0))
    out = pl.BlockSpec((T, LANES), lambda h, t, qi, kj: (kj[t], h))
    return _call(
        body, "mla_bwd_dkv", (H, npairs), 2,
        [qspec(0), qspec(0), kspec(0), kspec(H), pl.BlockSpec((T, LANES), lambda h, t, qi, kj: (kj[t], 0)), qspec(0), stat, stat],
        [out, out, pl.BlockSpec((1, T, LANES), lambda h, t, qi, kj: (h, kj[t], 0))],
        [SDS((n, H * LANES), BF16), SDS((n, H * LANES), BF16), SDS((H, n, LANES), F32)],
        [pltpu.VMEM((T, 2 * LANES), F32), pltpu.VMEM((T, LANES), F32)],
        [qi, kj, q_full, qr, kv, kv, kr, dy, lse, delta], ("parallel", "arbitrary"), carry)


def _sum_heads(a, tm):
    H, n, _ = a.shape

    def body(a_ref, o_ref):
        o_ref[...] = jnp.sum(a_ref[...], axis=0)

    return _pcall(body, name="mla_sum_heads", grid=(n // tm,), out_shape=SDS((n, LANES), F32),
                  in_specs=[pl.BlockSpec((H, tm, LANES), lambda i: (0, i, 0))],
                  out_specs=pl.BlockSpec((tm, LANES), lambda i: (i, 0)), compiler_params=_params(("parallel",)))(a)


def _shift_down(x, k):
    row = lax.broadcasted_iota(jnp.int32, x.shape, 0)
    return jnp.where(row >= k, pltpu.roll(x, k, axis=0), 0.0)


def _shift_up(x, k):
    n = x.shape[0]
    row = lax.broadcasted_iota(jnp.int32, x.shape, 0)
    return jnp.where(row < n - k, pltpu.roll(x, n - k, axis=0), 0.0)


def _conv_fwd(up, w, b):
    n, c = up.shape

    def body(u_ref, w_ref, b_ref, o_ref):
        u = u_ref[...].astype(F32)
        wv = w_ref[...]
        o_ref[...] = (b_ref[...] + wv[0:1] * _shift_down(u, 2) + wv[1:2] * _shift_down(u, 1) + wv[2:3] * u).astype(o_ref.dtype)

    return _pcall(body, name="conv_fwd", grid=(c // LANES,), out_shape=SDS((n, c), BF16),
                  in_specs=[pl.BlockSpec((n, LANES), lambda j: (0, j)), pl.BlockSpec((3, LANES), lambda j: (0, j)),
                            pl.BlockSpec((1, LANES), lambda j: (0, j))],
                  out_specs=pl.BlockSpec((n, LANES), lambda j: (0, j)), compiler_params=_params(("parallel",)))(up, w, b)


def _conv_bwd(up, dc, w):
    n, c = up.shape

    def body(u_ref, d_ref, w_ref, du_ref, dw_ref, db_ref):
        u = u_ref[...].astype(F32)
        d = d_ref[...].astype(F32)
        wv = w_ref[...]
        du_ref[...] = (wv[2:3] * d + wv[1:2] * _shift_up(d, 1) + wv[0:1] * _shift_up(d, 2)).astype(du_ref.dtype)
        dw_ref[0:1, :] = jnp.sum(d * _shift_down(u, 2), axis=0, keepdims=True)
        dw_ref[1:2, :] = jnp.sum(d * _shift_down(u, 1), axis=0, keepdims=True)
        dw_ref[2:3, :] = jnp.sum(d * u, axis=0, keepdims=True)
        db_ref[...] = jnp.sum(d, axis=0, keepdims=True)

    col = pl.BlockSpec((n, LANES), lambda j: (0, j))
    return _pcall(body, name="conv_bwd", grid=(c // LANES,),
                  out_shape=[SDS((n, c), BF16), SDS((3, c), F32), SDS((1, c), F32)],
                  in_specs=[col, col, pl.BlockSpec((3, LANES), lambda j: (0, j))],
                  out_specs=[col, pl.BlockSpec((3, LANES), lambda j: (0, j)), pl.BlockSpec((1, LANES), lambda j: (0, j))],
                  compiler_params=_params(("parallel",)))(up, dc, w)


def _adamw_math(w, g, m, v):
    m = ADAM_B1 * m + (1.0 - ADAM_B1) * g
    v = ADAM_B2 * v + (1.0 - ADAM_B2) * jnp.square(g)
    m_hat = m / (1.0 - ADAM_B1 ** ADAM_STEP)
    v_hat = v / (1.0 - ADAM_B2 ** ADAM_STEP)
    delta = -ADAM_LR * (m_hat / (jnp.sqrt(v_hat) + ADAM_EPS) + ADAM_WD * w)
    return delta, m, v


def _adamw(w, g, m, v, name):
    r, c = w.shape
    tr = r
    budget = max(8, (1 << 20) // (4 * c))
    t = (min(budget, r) // 8) * 8
    while t >= 8:
        if r % t == 0:
            tr = t
            break
        t -= 8

    def fn(i, rows, ps):
        return list(_adamw_math(*rows)), []
    (d, nm, nv), _ = _rowwise(fn, [_whole(w), _whole(g), _whole(m), _whole(v)], [], [(c, F32)] * 3, [], tr, name)
    return d, nm, nv


def _coords():
    return lax.axis_index("x"), lax.axis_index("y"), lax.axis_index("c")


def _other_chips(x, y):
    return [(1 - x, y), (x, 1 - y), (1 - x, 1 - y)]


HBM_SPEC = pl.BlockSpec(memory_space=pltpu.HBM)


def _half(c, rows):
    return pl.ds(pl.multiple_of(c * rows, 16), rows)


def _rows_tile(rows, cols, budget_bytes=2 << 20, align=16):
    t = (min(max(align, budget_bytes // (4 * cols)), rows) // align) * align
    while t >= align:
        if rows % t == 0:
            return t
        t -= align
    return rows


def _scalar(v):
    return jnp.reshape(jnp.asarray(v, jnp.int32), (1,))


def _cast_into_slot(w3, layer, slot, name):
    _, R, C = w3.shape
    tr = _rows_tile(R, C)

    def body(s_ref, w_ref, o_ref):
        o_ref[0] = w_ref[0].astype(BF16)

    gs = pltpu.PrefetchScalarGridSpec(
        num_scalar_prefetch=1, grid=(R // tr,),
        in_specs=[pl.BlockSpec((1, tr, C), lambda i, s: (layer, i, 0))],
        out_specs=pl.BlockSpec((1, tr, C), lambda i, s: (s[0], i, 0)))
    return _pcall(body, name=name, grid_spec=gs, out_shape=SDS((N_CHIPS, R, C), BF16),
                  compiler_params=_params(("arbitrary",)))(_scalar(slot), w3)


class _Stage:
    def __init__(self, ins, out_shapes, aliases, n_sems, copies):
        self.ins, self.out_shapes, self.aliases, self.n_sems, self.copies = list(ins), out_shapes, aliases, n_sems, copies

    def start(self, ins, outs, send_sems, recv_sems):
        for cp in self.copies(ins, outs, send_sems, recv_sems)[0]:
            cp.start()

    def finish(self, ins, outs, send_sems, recv_sems):
        sends, arrivals = self.copies(ins, outs, send_sems, recv_sems)
        for cp in arrivals:
            cp.wait_recv()
        for cp in sends:
            cp.wait_send()


def _remote(src, dst, send_sems, recv_sems, k, to):
    return pltpu.make_async_remote_copy(src_ref=src, dst_ref=dst, send_sem=send_sems.at[k], recv_sem=recv_sems.at[k],
                                        device_id=to, device_id_type=MESH)


def _gather_stages(bufs):
    n = len(bufs)
    shapes = [SDS(b.shape, b.dtype) for b in bufs]
    same = {i: i for i in range(n)}

    def over_ici(ins, outs, send_sems, recv_sems):
        x, y, c = _coords()
        blk = lambda w, chip: outs[w].at[chip, _half(c, outs[w].shape[1] // 2), :]
        sends, arrivals = [], []
        for w in range(n):
            for j, (px, py) in enumerate(_other_chips(x, y)):
                sends.append(_remote(blk(w, 2 * x + y), blk(w, 2 * x + y), send_sems, recv_sems, 3 * w + j, (px, py, c)))
                arrivals.append(_remote(blk(w, 2 * px + py), blk(w, 2 * px + py), send_sems, recv_sems, 3 * w + j, (px, py, c)))
        return sends, arrivals

    def to_sibling(ins, outs, send_sems, recv_sems):
        x, y, c = _coords()
        blk = lambda w, chip, half: outs[w].at[chip, _half(half, outs[w].shape[1] // 2), :]
        sends, arrivals = [], []
        for w in range(n):
            for j, (px, py) in enumerate(_other_chips(x, y)):
                k = 2 * px + py
                sends.append(_remote(blk(w, k, c), blk(w, k, c), send_sems, recv_sems, 3 * w + j, (x, y, 1 - c)))
                arrivals.append(_remote(blk(w, k, 1 - c), blk(w, k, 1 - c), send_sems, recv_sems, 3 * w + j, (x, y, 1 - c)))
        return sends, arrivals

    return (lambda b: _Stage(b, shapes, same, 3 * n, over_ici)), (lambda b: _Stage(b, shapes, same, 3 * n, to_sibling))


def _swap_stage(gs_):
    n = len(gs_)

    def copies(ins, outs, send_sems, recv_sems):
        x, y, c = _coords()
        cps = [_remote(ins[w].at[:, _half(1 - c, ins[w].shape[1] // 2), :], outs[w], send_sems, recv_sems, w, (x, y, 1 - c))
               for w in range(n)]
        return cps, cps

    return _Stage(gs_, [SDS((N_CHIPS, g.shape[1] // 2, g.shape[2]), g.dtype) for g in gs_], {}, n, copies)


def _scatter_stage(ps):
    n = len(ps)

    def copies(ins, outs, send_sems, recv_sems):
        x, y, c = _coords()
        cps = [_remote(ins[w].at[2 * px + py], outs[w].at[j], send_sems, recv_sems, 3 * w + j, (px, py, c))
               for w in range(n) for j, (px, py) in enumerate(_other_chips(x, y))]
        return cps, cps

    return _Stage(ps, [SDS((3,) + p.shape[1:], p.dtype) for p in ps], {}, 3 * n, copies)


def _join_stage(bufs):
    n = len(bufs)
    L = bufs[0].shape[0]

    def copies(ins, outs, send_sems, recv_sems):
        x, y, c = _coords()
        blk = lambda w, l, half: outs[w].at[l, _half(half, outs[w].shape[1] // 2), :]
        sends = [_remote(blk(w, l, c), blk(w, l, c), send_sems, recv_sems, L * w + l, (x, y, 1 - c))
                 for w in range(n) for l in range(L)]
        arrivals = [_remote(blk(w, l, 1 - c), blk(w, l, 1 - c), send_sems, recv_sems, L * w + l, (x, y, 1 - c))
                    for w in range(n) for l in range(L)]
        return sends, arrivals

    return _Stage(bufs, [SDS(b.shape, b.dtype) for b in bufs], {i: i for i in range(n)}, L * n, copies)


def _stage_scratch(stage):
    return [pltpu.SemaphoreType.DMA((stage.n_sems,)), pltpu.SemaphoreType.DMA((stage.n_sems,))]


def _run_stage(stage, name):
    n_in, n_out = len(stage.ins), len(stage.out_shapes)

    def body(*refs):
        ins, outs, send_sems, recv_sems = refs[:n_in], refs[n_in:n_in + n_out], refs[n_in + n_out], refs[n_in + n_out + 1]
        stage.start(ins, outs, send_sems, recv_sems)
        stage.finish(ins, outs, send_sems, recv_sems)

    return _pcall(body, name=name, out_shape=stage.out_shapes, in_specs=[HBM_SPEC] * n_in, out_specs=[HBM_SPEC] * n_out,
                  input_output_aliases=stage.aliases, scratch_shapes=_stage_scratch(stage))(*stage.ins)


def _call(body, name, grid, n_prefetch, in_specs, out_specs, out_shape, scratch, operands, semantics, carry=None):
    n_in, n_out, n_sc = len(in_specs), len(out_specs), len(scratch)
    if carry is None:
        gs = pltpu.PrefetchScalarGridSpec(num_scalar_prefetch=n_prefetch, grid=grid, in_specs=in_specs, out_specs=out_specs,
                                          scratch_shapes=scratch)
        res = _pcall(body, name=name, grid_spec=gs, out_shape=out_shape, compiler_params=_params(semantics))(*operands)
        return list(res), []
    s_in, s_out = len(carry.ins), len(carry.out_shapes)

    def carrying(*refs):
        o = n_prefetch
        pre, ins = refs[:o], refs[o:o + n_in]
        o += n_in
        sins = refs[o:o + s_in]
        o += s_in
        outs = refs[o:o + n_out]
        o += n_out
        souts = refs[o:o + s_out]
        o += s_out
        sc, send_sems, recv_sems = refs[o:o + n_sc], refs[o + n_sc], refs[o + n_sc + 1]
        first = functools.reduce(jnp.logical_and, [pl.program_id(a) == 0 for a in range(len(grid))])
        last = functools.reduce(jnp.logical_and, [pl.program_id(a) == g - 1 for a, g in enumerate(grid)])

        @pl.when(first)
        def _():
            carry.start(sins, souts, send_sems, recv_sems)

        body(*pre, *ins, *outs, *sc)

        @pl.when(last)
        def _():
            carry.finish(sins, souts, send_sems, recv_sems)

    gs = pltpu.PrefetchScalarGridSpec(
        num_scalar_prefetch=n_prefetch, grid=grid, in_specs=list(in_specs) + [HBM_SPEC] * s_in,
        out_specs=list(out_specs) + [HBM_SPEC] * s_out, scratch_shapes=list(scratch) + _stage_scratch(carry))
    aliases = {n_prefetch + n_in + a: n_out + b for a, b in carry.aliases.items()}
    res = _pcall(carrying, name=name, grid_spec=gs, out_shape=list(out_shape) + list(carry.out_shapes),
                 input_output_aliases=aliases, compiler_params=_params(("arbitrary",) * len(grid)))(*operands, *carry.ins)
    return list(res[:n_out]), list(res[n_out:])


def _all_reduce_small(v):
    n = v.shape[0]

    def body(v_ref, out_ref, slots, send_sems, recv_sems):
        x, y, c = _coords()
        me = 4 * x + 2 * y + c
        cps = []
        for r in range(1, 8):
            t = (me + r) % 8
            cp = pltpu.make_async_remote_copy(src_ref=v_ref, dst_ref=slots.at[me], send_sem=send_sems.at[r - 1],
                                              recv_sem=recv_sems.at[me], device_id=(t // 4, (t // 2) % 2, t % 2),
                                              device_id_type=MESH)
            cp.start()
            cps.append(cp)
        slots[me] = v_ref[...]
        for r in range(1, 8):
            s = (me + r) % 8
            pltpu.make_async_remote_copy(src_ref=v_ref, dst_ref=slots.at[s], send_sem=send_sems.at[r - 1],
                                         recv_sem=recv_sems.at[s], device_id=(x, y, c), device_id_type=MESH).wait_recv()
        for cp in cps:
            cp.wait_send()
        acc = slots[0]
        for d in range(1, 8):
            acc = acc + slots[d]
        out_ref[...] = acc

    return _pcall(body, name="all_reduce_small", out_shape=SDS((n, LANES), F32),
                  in_specs=[pl.BlockSpec(memory_space=pltpu.VMEM)], out_specs=pl.BlockSpec(memory_space=pltpu.VMEM),
                  scratch_shapes=[pltpu.VMEM((8, n, LANES), F32), pltpu.SemaphoreType.DMA((7,)), pltpu.SemaphoreType.DMA((8,))],
                  compiler_params=pltpu.CompilerParams(vmem_limit_bytes=VMEM_LIMIT))(v)


def _add_half(g, recv, c, name):
    _, R, C = g.shape
    rows = R // 2
    tr = _rows_tile(rows, C, 1 << 20)
    nb = rows // tr

    def body(s_ref, g_ref, r_ref, o32_ref, o16_ref):
        s = g_ref[...] + r_ref[...]
        o32_ref[...] = s
        o16_ref[...] = s.astype(BF16)

    blk = lambda k, i, s: (k, i, 0)
    gs = pltpu.PrefetchScalarGridSpec(
        num_scalar_prefetch=1, grid=(N_CHIPS, nb),
        in_specs=[pl.BlockSpec((1, tr, C), lambda k, i, s: (k, s[0] * nb + i, 0)), pl.BlockSpec((1, tr, C), blk)],
        out_specs=[pl.BlockSpec((1, tr, C), blk), pl.BlockSpec((1, tr, C), blk)])
    return _pcall(body, name=name, grid_spec=gs, out_shape=[SDS((N_CHIPS, rows, C), F32), SDS((N_CHIPS, rows, C), BF16)],
                  compiler_params=_params(("arbitrary", "arbitrary")))(_scalar(c), g, recv)


def _sum_into(p32, arrived, chip, c, layer, n_layers, prev, name):
    _, rows, C = p32.shape
    tr = _rows_tile(rows, C, 1 << 20)
    nb = rows // tr

    def body(chip_ref, c_ref, p_ref, a_ref, *rest):
        o_ref = rest[-1]
        o_ref[0] = ((p_ref[0] + a_ref[0].astype(F32)) + a_ref[1].astype(F32)) + a_ref[2].astype(F32)

    in_specs = [pl.BlockSpec((1, tr, C), lambda i, chip_ref, c_ref: (chip_ref[0], i, 0)),
                pl.BlockSpec((3, tr, C), lambda i, chip_ref, c_ref: (0, i, 0))]
    ins = [p32, arrived]
    aliases = {}
    if prev is not None:
        in_specs.append(pl.BlockSpec(memory_space=pl.ANY))
        ins.append(prev)
        aliases = {4: 0}
    gs = pltpu.PrefetchScalarGridSpec(
        num_scalar_prefetch=2, grid=(nb,), in_specs=in_specs,
        out_specs=pl.BlockSpec((1, tr, C), lambda i, chip_ref, c_ref: (layer, c_ref[0] * nb + i, 0)))
    return _pcall(body, name=name, grid_spec=gs, out_shape=SDS((n_layers, 2 * rows, C), F32), input_output_aliases=aliases,
                  compiler_params=_params(("arbitrary",)))(_scalar(chip), _scalar(c), *ins)


def _h_layout(D):
    G = N_BRANCHES * D
    off, o = {}, 0
    for name, w in [("g", G), ("qa", A_Q), ("cq", MLA_Q_RANK), ("ckv", MLA_KV_RANK), ("hu", SGU_WIDTH), ("hv", SGU_WIDTH),
                    ("ka", A_KV), ("va", A_KV), ("kr", LANES)]:
        assert o % w == 0, (name, o, w)
        off[name] = (o, w)
        o += w
    off["total"] = -(-o // 512) * 512
    return off


def _perm_w_in(w, lay):
    s = np.cumsum([0, A_Q, A_KV, A_KV, MLA_Q_RANK, MLA_KV_RANK, MLA_ROPE, SGU_WIDTH, SGU_WIDTH])
    qa, ka, va, cq, ckv, kr, hu, hv = [w[:, s[i]:s[i + 1]] for i in range(8)]
    g = w[:, s[8]:]
    pad = jnp.zeros((w.shape[0], lay["total"] - lay["kr"][0] - MLA_ROPE), w.dtype)
    return jnp.concatenate([g, qa, cq, ckv, hu, hv, ka, va, kr, pad], axis=1)


def _unperm_w_in(wp, lay, D):
    take = lambda n, width=None: wp[:, lay[n][0]:lay[n][0] + (width or lay[n][1])]
    return jnp.concatenate([take("qa"), take("ka"), take("va"), take("cq"), take("ckv"), take("kr", MLA_ROPE), take("hu"),
                            take("hv"), take("g")], axis=1)


def _perm_w_uq(w):
    r = w.shape[0]
    w3 = w.reshape(r, MLA_HEADS, MLA_NOPE + MLA_ROPE)
    nope = w3[:, :, :MLA_NOPE].reshape(r, MLA_HEADS * MLA_NOPE)
    rope = jnp.pad(w3[:, :, MLA_NOPE:], ((0, 0), (0, 0), (0, LANES - MLA_ROPE))).reshape(r, MLA_HEADS * LANES)
    return jnp.concatenate([nope, rope], axis=1)


def _unperm_w_uq(wp):
    r = wp.shape[0]
    nope = wp[:, :MLA_HEADS * MLA_NOPE].reshape(r, MLA_HEADS, MLA_NOPE)
    rope = wp[:, MLA_HEADS * MLA_NOPE:].reshape(r, MLA_HEADS, LANES)[:, :, :MLA_ROPE]
    return jnp.concatenate([nope, rope], axis=2).reshape(r, MLA_HEADS * (MLA_NOPE + MLA_ROPE))


def _perm_w_ukv(w):
    r = w.shape[0]
    w3 = w.reshape(r, MLA_HEADS, MLA_NOPE + MLA_V)
    return jnp.concatenate([w3[:, :, :MLA_NOPE].reshape(r, -1), w3[:, :, MLA_NOPE:].reshape(r, -1)], axis=1)


def _unperm_w_ukv(wp):
    r = wp.shape[0]
    k = wp[:, :MLA_HEADS * MLA_NOPE].reshape(r, MLA_HEADS, MLA_NOPE)
    v = wp[:, MLA_HEADS * MLA_NOPE:].reshape(r, MLA_HEADS, MLA_V)
    return jnp.concatenate([k, v], axis=2).reshape(r, -1)


def _col_chunks(g):
    r, c4 = g.shape
    return jnp.transpose(g.reshape(r, N_CHIPS, c4 // N_CHIPS), (1, 0, 2))


def kernel(x, positions, w_in, b_gate, sinks, q_norm_g, kv_norm_g, w_uq, w_ukv, sgu_ln_g, sgu_ln_b, sgu_w, sgu_b, w_proj_a, w_proj_b, w_proj_c, w_o, ln1_g, ln1_b, w_up, conv_w, conv_b, w_down, ln2_g, ln2_b, loss_target, m_w_in, m_b_gate, m_sinks, m_q_norm_g, m_kv_norm_g, m_w_uq, m_w_ukv, m_sgu_ln_g, m_sgu_ln_b, m_sgu_w, m_sgu_b, m_w_proj_a, m_w_proj_b, m_w_proj_c, m_w_o, m_ln1_g, m_ln1_b, m_w_up, m_conv_w, m_conv_b, m_w_down, m_ln2_g, m_ln2_b, v_w_in, v_b_gate, v_sinks, v_q_norm_g, v_kv_norm_g, v_w_uq, v_w_ukv, v_sgu_ln_g, v_sgu_ln_b, v_sgu_w, v_sgu_b, v_w_proj_a, v_w_proj_b, v_w_proj_c, v_w_o, v_ln1_g, v_ln1_b, v_w_up, v_conv_w, v_conv_b, v_w_down, v_ln2_g, v_ln2_b):
    a = locals()
    W = {k: a[k] for k in WEIGHTS}
    Mo = {k: a["m_" + k] for k in WEIGHTS}
    Vo = {k: a["v_" + k] for k in WEIGHTS}
    S, D = x.shape[1], x.shape[2]
    FF2 = w_up.shape[2] * N_CHIPS
    FF = FF2 // 2
    L = DEPTH
    lay = _h_layout(D)
    NP = lay["total"]
    cx, cy, cc = _coords()
    chip = 2 * cx + cy
    T = _tile(S, 512)
    TM = _tile(S, 256, 16)
    TMW = _tile(S, 64, 16)

    shards = {k: tuple(W[k].shape) for k in BIG}
    full = {k: [None] * L for k in BIG}
    own = [[_cast_into_slot(W[k], l, chip, f"cast_{k}_l{l}") for k in BIG] for l in range(L)]
    over_ici, to_sibling = _gather_stages(own[0])

    def use_gathered(l, gathered):
        for k, g in zip(BIG, gathered):
            _, r, c_ = shards[k]
            full[k][l] = g.reshape(N_CHIPS * r, c_) if k in ROW_SHARDED else jnp.transpose(g, (1, 0, 2)).reshape(r, N_CHIPS * c_)

    use_gathered(0, _run_stage(to_sibling(_run_stage(over_ici(own[0]), "gather_ici_l0")), "gather_sibling_l0"))

    small_sharded_full = {k: tuple(W[k].shape[:-1]) + (W[k].shape[-1] * N_CHIPS,) for k in SMALL_SHARDED}
    placed = []
    for k in ("b_gate", "conv_w"):
        z = jnp.zeros(small_sharded_full[k], F32)
        z = lax.dynamic_update_slice_in_dim(z, W[k], chip * W[k].shape[-1], axis=-1)
        placed.append(jnp.where(cc == 0, z, 0.0).reshape(-1))
    pv = jnp.concatenate(placed)
    n_pv = pv.shape[0]
    pv = jnp.pad(pv, (0, -n_pv % (8 * LANES))).reshape(-1, LANES)
    pv = _all_reduce_small(pv).reshape(-1)
    nb_ = int(np.prod(small_sharded_full["b_gate"]))
    b_gate_full = pv[:nb_].reshape(small_sharded_full["b_gate"])
    conv_w_full = pv[nb_:n_pv].reshape(small_sharded_full["conv_w"])

    inv_freq = ROPE_THETA ** (-jnp.arange(0, MLA_ROPE, 2, dtype=F32) / MLA_ROPE)
    ang = positions[0].astype(F32)[:, None] * inv_freq
    cos, sin = jnp.cos(ang), jnp.sin(ang)
    cos_t = jnp.concatenate([cos, cos, jnp.ones((S, LANES - MLA_ROPE), F32)], axis=1)
    sin_t = jnp.concatenate([sin, sin, jnp.zeros((S, LANES - MLA_ROPE), F32)], axis=1)

    row = lambda v: v.reshape(1, -1)
    cb = lambda name: lay[name][0] // lay[name][1]

    xs = x[0]
    saved = []
    for l in range(L):
        p = dict(
            w_in=_perm_w_in(full["w_in"][l], lay), w_uq=_perm_w_uq(full["w_uq"][l]), w_ukv=_perm_w_ukv(full["w_ukv"][l]),
            w_pa=full["w_proj_a"][l], w_pb=full["w_proj_b"][l], w_pc=full["w_proj_c"][l], w_o=full["w_o"][l],
            w_up=full["w_up"][l], w_down=full["w_down"][l],
            sinks=row(sinks[l]), qg=row(q_norm_g[l]), kvg=row(kv_norm_g[l]), sg=row(sgu_ln_g[l]), sb=row(sgu_ln_b[l]),
            sw=sgu_w[l], sb3=sgu_b[l].reshape(SGU_GROUPS, SGU_CHUNK, 1),
            bg=b_gate_full[l], l1g=row(ln1_g[l]), l1b=row(ln1_b[l]), cw=conv_w_full[l], cbias=row(conv_b[l]),
            l2g=row(ln2_g[l]), l2b=row(ln2_b[l]))
        if l == 0:
            def fn_cast(i, rows, ps):
                return [rows[0]], []
            (xb,), _ = _rowwise(fn_cast, [_whole(xs)], [], [(D, BF16)], [], TM, "cast_x")
        h = _mm(xb, p["w_in"], "nn", BF16, "mm_h")
        y_a = _swa_fwd(h, cb("qa"), cb("ka"), cb("va"), p["sinks"])
        def fn_rms(i, rows, ps):
            return [_rms_norm(rows[0].astype(F32), ps[0]), _rms_norm(rows[1].astype(F32), ps[1])], []
        (cqn, ckvn), _ = _rowwise(fn_rms, [(h, MLA_Q_RANK, cb("cq")), (h, MLA_KV_RANK, cb("ckv"))], [p["qg"], p["kvg"]],
                                  [(MLA_Q_RANK, BF16), (MLA_KV_RANK, BF16)], [], TM, "mla_rms")
        q_full = _mm(cqn, p["w_uq"], "nn", BF16, "mm_q")
        kv = _mm(ckvn, p["w_ukv"], "nn", BF16, "mm_kv")
        qr, kr = _rope_call(q_full, MLA_HEADS * LANES, 1, h, LANES, cb("kr"), cos_t, sin_t, 1.0, TM, "rope_fwd")
        (y_b, lse), landed = _mla_fwd(q_full, qr, kv, kr, T, carry=over_ici(own[l + 1]) if l + 1 < L else None)
        if l + 1 < L:
            use_gathered(l + 1, _run_stage(to_sibling(landed), f"gather_sibling_l{l + 1}"))
        y_c = _sgu_fwd(h, cb("hu"), cb("hv"), p["sg"], p["sb"], p["sw"], p["sb3"])
        pa = _mm(y_a, p["w_pa"], "nn", F32, "mm_pa")
        pb = _mm(y_b, p["w_pb"], "nn", F32, "mm_pb")
        pc = _mm(y_c, p["w_pc"], "nn", F32, "mm_pc")

        def merge_math(pa_, pb_, pc_, g_, b0, b1, b2):
            out = 0.0
            for br, (pp, bb) in enumerate(zip((pa_, pb_, pc_), (b0, b1, b2))):
                gate = jax.nn.sigmoid(g_[:, br * D:(br + 1) * D].astype(F32) + bb)
                out = out + gate * pp
            return out

        def fn_merge(i, rows, ps):
            bgv = ps[0]
            return [merge_math(rows[0], rows[1], rows[2], rows[3], bgv[0:1], bgv[1:2], bgv[2:3])], []
        (merged,), _ = _rowwise(fn_merge, [_whole(pa), _whole(pb), _whole(pc), (h, N_BRANCHES * D, cb("g"))], [p["bg"]],
                                [(D, BF16)], [], TMW, "merge_fwd")
        o = _mm(merged, p["w_o"], "nn", F32, "mm_o")

        def ln_res_math(x_, o_, g_, b_):
            return _layer_norm(DN_ALPHA * x_ + o_, g_, b_)

        def fn_ln(i, rows, ps):
            y = ln_res_math(rows[0], rows[1], ps[0], ps[1])
            return [y, y], []
        (x1, x1b), _ = _rowwise(fn_ln, [_whole(xs), _whole(o)], [p["l1g"], p["l1b"]], [(D, F32), (D, BF16)], [], TM, "ln1_fwd")
        up = _mm(x1b, p["w_up"], "nn", BF16, "mm_up")
        cv_ = _conv_fwd(up, p["cw"], p["cbias"])

        def glu_math(cg, cvv):
            return jax.nn.silu(cg.astype(F32)) * cvv.astype(F32)

        def fn_glu(i, rows, ps):
            return [glu_math(rows[0], rows[1])], []
        (act,), _ = _rowwise(fn_glu, [(cv_, FF, 0), (cv_, FF, 1)], [], [(FF, BF16)], [], TMW, "glu_fwd")
        dn = _mm(act, p["w_down"], "nn", F32, "mm_down")
        (x2, x2b), _ = _rowwise(fn_ln, [_whole(x1), _whole(dn)], [p["l2g"], p["l2b"]], [(D, F32), (D, BF16)], [], TM, "ln2_fwd")
        saved.append(dict(p=p, x0=xs, x0b=xb, h=h, y_a=y_a, cqn=cqn, ckvn=ckvn, q_full=q_full, kv=kv, qr=qr, kr=kr, y_b=y_b,
                          lse=lse, y_c=y_c, pa=pa, pb=pb, pc=pc, merged=merged, o=o, x1=x1, x1b=x1b, up=up, cv=cv_, act=act,
                          dn=dn))
        xs, xb = x2, x2b

    def fn_loss(i, rows, ps):
        diff = rows[0] - rows[1]
        part = jnp.sum(jnp.mean(jnp.square(diff), axis=-1, keepdims=True), axis=0, keepdims=True)
        return [diff * (1.0 / D)], [jnp.broadcast_to(part, (8, LANES))]
    (dx,), (loss_acc,) = _rowwise(fn_loss, [_whole(xs), _whole(loss_target[0])], [], [(D, F32)], [(8, LANES)], TM, "loss")
    loss = lax.psum(0.5 * loss_acc[0, 0], ("x", "y", "c"))

    gbig = {k: [None] * L for k in BIG}
    gsmall = {k: [None] * L for k in SMALL}
    reduced = {}
    pending = None
    for l in reversed(range(L)):
        s = saved[l]
        p = s["p"]

        def fn_ln_bwd(i, rows, ps):
            _, vjp = jax.vjp(ln_res_math, rows[0], rows[1], ps[0], ps[1])
            dx_, do_, dg_, db_ = vjp(rows[2])
            return [dx_, do_], [dg_, db_]
        (dx1_res, ddn), (g_l2g, g_l2b) = _rowwise(fn_ln_bwd, [_whole(s["x1"]), _whole(s["dn"]), _whole(dx)], [p["l2g"], p["l2b"]],
                                                  [(D, F32), (D, BF16)], [(1, D), (1, D)], TM, "ln2_bwd")
        gsmall["ln2_g"][l], gsmall["ln2_b"][l] = g_l2g, g_l2b
        row_chunks = lambda g: g.reshape(N_CHIPS, g.shape[0] // N_CHIPS, g.shape[1])
        gbig["w_down"][l] = row_chunks(_mm(s["act"], ddn, "tn", F32, "mm_dw_down"))
        dact = _mm(ddn, p["w_down"], "nt", BF16, "mm_dact")

        def fn_glu_bwd(i, rows, ps):
            _, vjp = jax.vjp(glu_math, rows[0], rows[1])
            dcg, dcv = vjp(rows[2].astype(F32))
            return [jnp.concatenate([dcg, dcv], axis=1)], []
        (dc,), _ = _rowwise(fn_glu_bwd, [(s["cv"], FF, 0), (s["cv"], FF, 1), _whole(dact)], [], [(FF2, BF16)], [], TMW, "glu_bwd")
        dup, g_cw, g_cb = _conv_bwd(s["up"], dc, p["cw"])
        gsmall["conv_w"][l], gsmall["conv_b"][l] = g_cw, g_cb
        if pending is None:
            gbig["w_up"][l] = _mm(s["x1b"], dup, "tn", F32, "mm_dw_up", col_chunks=N_CHIPS, tm=2048)
        else:
            gbig["w_up"][l], from_sibling = _mm(s["x1b"], dup, "tn", F32, "mm_dw_up", col_chunks=N_CHIPS, tm=2048,
                                                carry=_swap_stage(pending))
            partial = [_add_half(g, r, cc, f"rs_add_{k}_l{l + 1}") for k, g, r in zip(BIG, pending, from_sibling)]
        dx1 = _mm(dup, p["w_up"], "nt", F32, "mm_dx1", add=dx1_res)
        (dx0_res, do_), (g_l1g, g_l1b) = _rowwise(fn_ln_bwd, [_whole(s["x0"]), _whole(s["o"]), _whole(dx1)], [p["l1g"], p["l1b"]],
                                                  [(D, F32), (D, BF16)], [(1, D), (1, D)], TM, "ln1_bwd")
        gsmall["ln1_g"][l], gsmall["ln1_b"][l] = g_l1g, g_l1b
        gbig["w_o"][l] = row_chunks(_mm(s["merged"], do_, "tn", F32, "mm_dw_o"))
        dmerged = _mm(do_, p["w_o"], "nt", F32, "mm_dmerged")

        def fn_merge_bwd(i, rows, ps):
            bgv = ps[0]
            _, vjp = jax.vjp(merge_math, rows[0], rows[1], rows[2], rows[3], bgv[0:1], bgv[1:2], bgv[2:3])
            dpa, dpb, dpc, dg_, db0, db1, db2 = vjp(rows[4])
            return [dpa, dpb, dpc, dg_], [db0, db1, db2]
        (dpa, dpb, dpc, dgl), (db0, db1, db2) = _rowwise(
            fn_merge_bwd, [_whole(s["pa"]), _whole(s["pb"]), _whole(s["pc"]), (s["h"], N_BRANCHES * D, cb("g")), _whole(dmerged)],
            [p["bg"]], [(D, BF16), (D, BF16), (D, BF16), (N_BRANCHES * D, BF16)], [(1, D)] * 3, TMW, "merge_bwd")
        gsmall["b_gate"][l] = jnp.concatenate([db0, db1, db2], axis=0)
        gbig["w_proj_a"][l] = _mm(s["y_a"], dpa, "tn", F32, "mm_dw_pa", col_chunks=N_CHIPS)
        gbig["w_proj_b"][l] = row_chunks(_mm(s["y_b"], dpb, "tn", F32, "mm_dw_pb"))
        gbig["w_proj_c"][l] = _mm(s["y_c"], dpc, "tn", F32, "mm_dw_pc", col_chunks=N_CHIPS)
        dy_a = _mm(dpa, p["w_pa"], "nt", BF16, "mm_dy_a")
        dy_b = _mm(dpb, p["w_pb"], "nt", BF16, "mm_dy_b")
        dy_c = _mm(dpc, p["w_pc"], "nt", BF16, "mm_dy_c")
        dh_c, g_sg, g_sb, g_sw, g_sb3 = _sgu_bwd(s["h"], cb("hu"), cb("hv"), dy_c, p["sg"], p["sb"], p["sw"], p["sb3"])
        gsmall["sgu_ln_g"][l], gsmall["sgu_ln_b"][l], gsmall["sgu_w"][l] = g_sg, g_sb, g_sw
        gsmall["sgu_b"][l] = g_sb3.reshape(SGU_GROUPS, SGU_CHUNK)
        hide_early = l == 0
        early = [gbig[k][l] for k in EARLY]
        (dqa, dka, dva, g_sinks), early_from_sibling = _swa_bwd(s["h"], cb("qa"), cb("ka"), cb("va"), p["sinks"], dy_a,
                                                                carry=_swap_stage(early) if hide_early else None)
        gsmall["sinks"][l] = g_sinks
        if hide_early:
            early_partial = [_add_half(g, r, cc, f"rs_add_{k}_l{l}") for k, g, r in zip(EARLY, early, early_from_sibling)]
        delta = _mla_delta(dy_b, s["y_b"], T)
        (dqn, dqr), early_arrived = _mla_bwd_dq(s["q_full"], s["qr"], s["kv"], s["kr"], dy_b, s["lse"], delta, T,
                                                carry=_scatter_stage([p16 for _, p16 in early_partial]) if hide_early else None)
        if hide_early:
            for k, (p32, _), arr in zip(EARLY, early_partial, early_arrived):
                reduced[k] = _sum_into(p32, arr, chip, cc, l, L, reduced.get(k), f"rs_sum_{k}_l{l}")
        (dkn, dv, dkr_heads), arrived = _mla_bwd_dkv(
            s["q_full"], s["qr"], s["kv"], s["kr"], dy_b, s["lse"], delta, T,
            carry=None if pending is None else _scatter_stage([p16 for _, p16 in partial]))
        if pending is not None:
            for k, (p32, _), arr in zip(BIG, partial, arrived):
                reduced[k] = _sum_into(p32, arr, chip, cc, l + 1, L, reduced.get(k), f"rs_sum_{k}_l{l + 1}")
        dkr = _sum_heads(dkr_heads, TM)
        dqr_raw, dkr_raw = _rope_call(dqr, MLA_HEADS * LANES, 0, dkr, LANES, 0, cos_t, sin_t, -1.0, TM, "rope_bwd")
        dq_full = jnp.concatenate([dqn, dqr_raw], axis=1)
        dkv = jnp.concatenate([dkn, dv], axis=1)
        gbig["w_uq"][l] = _col_chunks(_unperm_w_uq(_mm(s["cqn"], dq_full, "tn", F32, "mm_dw_uq")))
        gbig["w_ukv"][l] = _col_chunks(_unperm_w_ukv(_mm(s["ckvn"], dkv, "tn", F32, "mm_dw_ukv")))
        dcqn = _mm(dq_full, p["w_uq"], "nt", F32, "mm_dcqn")
        dckvn = _mm(dkv, p["w_ukv"], "nt", F32, "mm_dckvn")

        def fn_rms_bwd(i, rows, ps):
            _, vjp1 = jax.vjp(lambda c_, g_: _rms_norm(c_.astype(F32), g_), rows[0], ps[0])
            _, vjp2 = jax.vjp(lambda c_, g_: _rms_norm(c_.astype(F32), g_), rows[1], ps[1])
            d1, dg1 = vjp1(rows[2])
            d2, dg2 = vjp2(rows[3])
            return [d1, d2], [dg1, dg2]
        (dcq, dckv), (g_qg, g_kvg) = _rowwise(
            fn_rms_bwd, [(s["h"], MLA_Q_RANK, cb("cq")), (s["h"], MLA_KV_RANK, cb("ckv")), _whole(dcqn), _whole(dckvn)],
            [p["qg"], p["kvg"]], [(MLA_Q_RANK, BF16), (MLA_KV_RANK, BF16)], [(1, MLA_Q_RANK), (1, MLA_KV_RANK)], TM, "mla_rms_bwd")
        gsmall["q_norm_g"][l], gsmall["kv_norm_g"][l] = g_qg, g_kvg
        tail = jnp.zeros((S, NP - lay["kr"][0] - LANES), BF16)
        dh = jnp.concatenate([dgl, dqa, dcq, dckv, dh_c, dka, dva, dkr_raw, tail], axis=1)
        gbig["w_in"][l] = _col_chunks(_unperm_w_in(_mm(s["x0b"], dh, "tn", F32, "mm_dw_in"), lay, D))
        dx = _mm(dh, p["w_in"], "nt", F32, "mm_dx0", add=dx0_res)

        pending = [gbig[k][l] for k in BIG]

    late = [gbig[k][0] for k in LATE]
    from_sibling = _run_stage(_swap_stage(late), "rs_swap_halves_l0")
    partial = [_add_half(g, r, cc, f"rs_add_{k}_l0") for k, g, r in zip(LATE, late, from_sibling)]
    arrived = _run_stage(_scatter_stage([p16 for _, p16 in partial]), "rs_scatter_chips_l0")
    for k, (p32, _), arr in zip(LATE, partial, arrived):
        reduced[k] = _sum_into(p32, arr, chip, cc, 0, L, reduced.get(k), f"rs_sum_{k}_l0")

    grad_x = dx.reshape(x.shape)
    g_big = dict(zip(BIG, _run_stage(_join_stage([reduced[k] for k in BIG]), "rs_join_halves")))

    small_shapes = {k: (small_sharded_full[k] if k in SMALL_SHARDED else tuple(W[k].shape)) for k in SMALL}
    sv = jnp.concatenate([jnp.stack(gsmall[k]).reshape(-1) for k in SMALL])
    n_sv = sv.shape[0]
    sv = jnp.pad(sv, (0, -n_sv % (8 * LANES))).reshape(-1, LANES)
    sv = _all_reduce_small(sv).reshape(-1)
    g_small, o_ = {}, 0
    for k in SMALL:
        n = int(np.prod(small_shapes[k]))
        g = sv[o_:o_ + n].reshape(small_shapes[k])
        if k in SMALL_SHARDED:
            g = lax.dynamic_slice_in_dim(g, chip * W[k].shape[-1], W[k].shape[-1], axis=-1)
        g_small[k] = g
        o_ += n

    delta, new_m, new_v = {}, {}, {}
    for k in BIG:
        shp = shards[k]
        v2 = lambda t: t.reshape(shp[0] * shp[1], shp[2])
        d_, m_, v_ = _adamw(v2(W[k]), v2(g_big[k]), v2(Mo[k]), v2(Vo[k]), "adamw_" + k)
        delta[k], new_m[k], new_v[k] = d_.reshape(shp), m_.reshape(shp), v_.reshape(shp)
    pack = lambda t: jnp.concatenate([t[k].reshape(-1) for k in SMALL])
    n_small = sum(int(np.prod(W[k].shape)) for k in SMALL)
    pad2 = lambda t: jnp.pad(t, (0, -n_small % (8 * LANES))).reshape(-1, LANES)
    d_, m_, v_ = _adamw(pad2(pack(W)), pad2(pack(g_small)), pad2(pack(Mo)), pad2(pack(Vo)), "adamw_small")
    o_ = 0
    for k in SMALL:
        n = int(np.prod(W[k].shape))
        take = lambda t: t.reshape(-1)[o_:o_ + n].reshape(W[k].shape)
        delta[k], new_m[k], new_v[k] = take(d_), take(m_), take(v_)
        o_ += n

    grads = {**g_big, **g_small}
    return (loss, grad_x, *[grads[k] for k in WEIGHTS], *[delta[k] for k in WEIGHTS], *[new_m[k] for k in WEIGHTS],
            *[new_v[k] for k in WEIGHTS])
```

```python
import functools
import math

import jax
import jax.numpy as jnp
import numpy as np
from jax import lax
from jax.experimental import pallas as pl
from jax.experimental.pallas import tpu as pltpu

F32, BF16 = jnp.float32, jnp.bfloat16
SDS = jax.ShapeDtypeStruct
MESH = pl.DeviceIdType.MESH

SWA_Q_HEADS, SWA_KV_HEADS, SWA_HEAD_DIM, SWA_BLOCK = 16, 2, 64, 128
MLA_HEADS, MLA_NOPE, MLA_ROPE, MLA_V = 16, 128, 64, 128
MLA_Q_RANK, MLA_KV_RANK = 512, 512
ROPE_THETA = 10000.0
SGU_GROUPS, SGU_GROUP_DIM, SGU_CHUNK = 8, 128, 128
SGU_WIDTH = SGU_GROUPS * SGU_GROUP_DIM
A_Q = SWA_Q_HEADS * SWA_HEAD_DIM
A_KV = SWA_KV_HEADS * SWA_HEAD_DIM
N_BRANCHES = 3
DEPTH = 2
EPS = 1e-5
MASK_VALUE = -1e30
DN_ALPHA = (2 * DEPTH) ** 0.25
ADAM_LR, ADAM_B1, ADAM_B2, ADAM_EPS, ADAM_WD, ADAM_STEP = 0.001, 0.9, 0.999, 1e-08, 0.01, 10
N_CHIPS = 4

LANES = 128
VMEM_LIMIT = 48 * 1024 * 1024

BIG = ["w_in", "w_uq", "w_ukv", "w_proj_a", "w_proj_b", "w_proj_c", "w_o", "w_up", "w_down"]
ROW_SHARDED = {"w_proj_b", "w_o", "w_down"}
LATE = ["w_in", "w_uq", "w_ukv"]
EARLY = [k for k in BIG if k not in LATE]
SMALL = ["b_gate", "sinks", "q_norm_g", "kv_norm_g", "sgu_ln_g", "sgu_ln_b", "sgu_w", "sgu_b", "ln1_g", "ln1_b",
         "conv_w", "conv_b", "ln2_g", "ln2_b"]
SMALL_SHARDED = {"b_gate", "conv_w"}
WEIGHTS = ["w_in", "b_gate", "sinks", "q_norm_g", "kv_norm_g", "w_uq", "w_ukv", "sgu_ln_g", "sgu_ln_b", "sgu_w", "sgu_b",
           "w_proj_a", "w_proj_b", "w_proj_c", "w_o", "ln1_g", "ln1_b", "w_up", "conv_w", "conv_b", "w_down", "ln2_g", "ln2_b"]


def _pcall(body, **kw):
    return pl.pallas_call(body, **kw)


def _params(sem=None):
    return pltpu.CompilerParams(dimension_semantics=sem, vmem_limit_bytes=VMEM_LIMIT)


def _tile(dim, pref, align=LANES):
    t = (min(pref, dim) // align) * align
    while t >= align:
        if dim % t == 0:
            return t
        t -= align
    return dim


def _mm(a, b, mode, out_dtype, name, add=None, tm=1024, tn=512, tk=2048, col_chunks=1, carry=None):
    if mode == "nn":
        (M, K), (K2, N) = a.shape, b.shape
    elif mode == "nt":
        (M, K), (N, K2) = a.shape, b.shape
    else:
        (K, M), (K2, N) = a.shape, b.shape
    assert K == K2, (a.shape, b.shape, mode)
    assert N % col_chunks == 0
    tm, tn, tk = _tile(M, tm), _tile(N // col_chunks, tn), _tile(K, tk)
    assert (N // col_chunks) % tn == 0
    per_chunk = (N // col_chunks) // tn
    nk = K // tk
    if mode == "tn":
        a_spec = pl.BlockSpec((tk, tm), lambda i, j, k: (k, i))
    else:
        a_spec = pl.BlockSpec((tm, tk), lambda i, j, k: (i, k))
    if mode == "nt":
        b_spec = pl.BlockSpec((tn, tk), lambda i, j, k: (j, k))
    else:
        b_spec = pl.BlockSpec((tk, tn), lambda i, j, k: (k, j))
    dn = {"nn": (((1,), (0,)), ((), ())), "nt": (((1,), (1,)), ((), ())), "tn": (((0,), (0,)), ((), ()))}[mode]
    chunked = col_chunks > 1
    if chunked:
        assert add is None
        o_spec = pl.BlockSpec((1, tm, tn), lambda i, j, k: (lax.div(j, per_chunk), i, lax.rem(j, per_chunk)))
        out_shape = SDS((col_chunks, M, N // col_chunks), out_dtype)
    else:
        o_spec = pl.BlockSpec((tm, tn), lambda i, j, k: (i, j))
        out_shape = SDS((M, N), out_dtype)
    has_add = add is not None

    def body(*refs):
        if has_add:
            a_ref, b_ref, add_ref, o_ref, acc_ref = refs
        else:
            a_ref, b_ref, o_ref, acc_ref = refs
        k = pl.program_id(2)

        @pl.when(k == 0)
        def _():
            acc_ref[...] = jnp.zeros_like(acc_ref)

        acc_ref[...] += lax.dot_general(a_ref[...].astype(BF16), b_ref[...].astype(BF16), dn,
                                        preferred_element_type=F32)

        @pl.when(k == nk - 1)
        def _():
            r = acc_ref[...]
            if has_add:
                r = r + add_ref[...].astype(F32)
            if chunked:
                o_ref[0] = r.astype(o_ref.dtype)
            else:
                o_ref[...] = r.astype(o_ref.dtype)

    ins = [a, b] + ([add] if has_add else [])
    in_specs = [a_spec, b_spec] + ([o_spec] if has_add else [])
    (out,), carried = _call(body, name, (M // tm, N // tn, nk), 0, in_specs, [o_spec], [out_shape], [pltpu.VMEM((tm, tn), F32)],
                            ins, ("parallel", "parallel", "arbitrary"), carry)
    return out if carry is None else (out, carried)


def _rowwise(fn, rows, params, row_outs, acc_outs, tm, name):
    n_rows = rows[0][0].shape[0]
    assert n_rows % tm == 0
    nr, npar, no = len(rows), len(params), len(row_outs)

    def body(*refs):
        i = pl.program_id(0)
        r, p = refs[:nr], refs[nr:nr + npar]
        o, acc = refs[nr + npar:nr + npar + no], refs[nr + npar + no:]
        outs, sums = fn(i, [x[...] for x in r], [x[...] for x in p])
        for ref, val in zip(o, outs, strict=True):
            ref[...] = val.astype(ref.dtype)
        if acc:
            @pl.when(i == 0)
            def _():
                for ref in acc:
                    ref[...] = jnp.zeros_like(ref)
            for ref, val in zip(acc, sums, strict=True):
                ref[...] += val.astype(F32)

    def full(shape):
        nd = len(shape)
        return pl.BlockSpec(tuple(shape), lambda i: (0,) * nd)

    in_specs = [pl.BlockSpec((tm, w), (lambda i, cb=cb: (i, cb))) for (_, w, cb) in rows] + [full(p.shape) for p in params]
    out_specs = [pl.BlockSpec((tm, w), lambda i: (i, 0)) for (w, _) in row_outs] + [full(s) for s in acc_outs]
    out_shape = [SDS((n_rows, w), dt) for (w, dt) in row_outs] + [SDS(tuple(s), F32) for s in acc_outs]
    res = _pcall(body, name=name, out_shape=out_shape, grid=(n_rows // tm,), in_specs=in_specs, out_specs=out_specs,
                 compiler_params=_params(("arbitrary",)))(*[r[0] for r in rows], *params)
    return list(res[:no]), list(res[no:])


def _whole(a):
    return (a, a.shape[1], 0)


def _gelu(x):
    return 0.5 * x * (1.0 + lax.erf(x * (1.0 / math.sqrt(2.0))))


def _layer_norm(x, g, b):
    mu = x.mean(-1, keepdims=True)
    var = jnp.mean(jnp.square(x - mu), -1, keepdims=True)
    return (x - mu) * lax.rsqrt(var + EPS) * g + b


def _rms_norm(x, g):
    return x * lax.rsqrt(jnp.mean(jnp.square(x), -1, keepdims=True) + EPS) * g


def _sgu_math(hu, hv, ln_g, ln_b, ws, bs):
    u = _gelu(hu.astype(F32))
    vn = _layer_norm(_gelu(hv.astype(F32)), ln_g, ln_b)
    r = lax.broadcasted_iota(jnp.int32, (SGU_CHUNK, SGU_CHUNK), 0)
    c = lax.broadcasted_iota(jnp.int32, (SGU_CHUNK, SGU_CHUNK), 1)
    outs = []
    for g in range(SGU_GROUPS):
        w = jnp.where(r >= c, ws[g], 0.0).astype(BF16)
        vg = vn[:, g * SGU_GROUP_DIM:(g + 1) * SGU_GROUP_DIM].astype(BF16)
        outs.append(jnp.dot(w, vg, preferred_element_type=F32) + bs[g])
    return u * jnp.concatenate(outs, axis=1)


def _sgu_fwd(h, cu, cv, ln_g, ln_b, w, b3):
    def fn(i, rows, ps):
        g_, b_, w_, b3_ = ps
        y = _sgu_math(rows[0], rows[1], g_, b_, [w_[g] for g in range(SGU_GROUPS)], [b3_[g] for g in range(SGU_GROUPS)])
        return [y], []
    (y,), _ = _rowwise(fn, [(h, SGU_WIDTH, cu), (h, SGU_WIDTH, cv)], [ln_g, ln_b, w, b3], [(SGU_WIDTH, BF16)], [],
                       SGU_CHUNK, "sgu_fwd")
    return y


def _sgu_bwd(h, cu, cv, dy, ln_g, ln_b, w, b3):
    nd = 2 * SGU_WIDTH

    def body(hu_ref, hv_ref, dy_ref, g_ref, b_ref, w_ref, b3_ref, dh_ref, dg_ref, db_ref, dw_ref, db3_ref):
        i = pl.program_id(0)

        @pl.when(i == 0)
        def _():
            dg_ref[...] = jnp.zeros_like(dg_ref)
            db_ref[...] = jnp.zeros_like(db_ref)
            dw_ref[...] = jnp.zeros_like(dw_ref)
            db3_ref[...] = jnp.zeros_like(db3_ref)

        ws = [w_ref[g] for g in range(SGU_GROUPS)]
        bs = [b3_ref[g] for g in range(SGU_GROUPS)]
        _, vjp = jax.vjp(_sgu_math, hu_ref[...], hv_ref[...], g_ref[...], b_ref[...], ws, bs)
        dhu, dhv, dg, db, dws, dbs = vjp(dy_ref[...].astype(F32))
        dh_ref[...] = jnp.concatenate([dhu, dhv], axis=1).astype(dh_ref.dtype)
        dg_ref[...] += dg
        db_ref[...] += db
        for g in range(SGU_GROUPS):
            dw_ref[g] += dws[g]
            db3_ref[g] += dbs[g]

    n = h.shape[0]
    blk = lambda cb: pl.BlockSpec((SGU_CHUNK, SGU_WIDTH), lambda i, cb=cb: (i, cb))
    full = lambda s: pl.BlockSpec(tuple(s), lambda i: (0,) * len(s))
    return _pcall(
        body, name="sgu_bwd", grid=(n // SGU_CHUNK,),
        out_shape=[SDS((n, nd), BF16), SDS(ln_g.shape, F32), SDS(ln_b.shape, F32), SDS(w.shape, F32), SDS(b3.shape, F32)],
        in_specs=[blk(cu), blk(cv), blk(0), full(ln_g.shape), full(ln_b.shape), full(w.shape), full(b3.shape)],
        out_specs=[pl.BlockSpec((SGU_CHUNK, nd), lambda i: (i, 0)), full(ln_g.shape), full(ln_b.shape), full(w.shape),
                   full(b3.shape)],
        compiler_params=_params(("arbitrary",)))(h, h, dy, ln_g, ln_b, w, b3)


def _swa_math(q, kp, kc, vp, vc, sinks, not_first):
    kw = jnp.concatenate([kp, kc], axis=0).astype(BF16)
    vw = jnp.concatenate([vp, vc], axis=0).astype(BF16)
    qb = q.astype(BF16)
    q_off = lax.broadcasted_iota(jnp.int32, (SWA_BLOCK, 2 * SWA_BLOCK), 0) + SWA_BLOCK
    k_off = lax.broadcasted_iota(jnp.int32, (SWA_BLOCK, 2 * SWA_BLOCK), 1)
    rel = q_off - k_off
    valid = (rel >= 0) & (rel < SWA_BLOCK) & (not_first | (k_off >= SWA_BLOCK))
    G = SWA_Q_HEADS // SWA_KV_HEADS
    outs = []
    for head in range(SWA_Q_HEADS):
        hk = head // G
        qh = qb[:, head * SWA_HEAD_DIM:(head + 1) * SWA_HEAD_DIM]
        kh = kw[:, hk * SWA_HEAD_DIM:(hk + 1) * SWA_HEAD_DIM]
        vh = vw[:, hk * SWA_HEAD_DIM:(hk + 1) * SWA_HEAD_DIM]
        s = lax.dot_general(qh, kh, (((1,), (1,)), ((), ())), preferred_element_type=F32) * (SWA_HEAD_DIM ** -0.5)
        s = jnp.where(valid, s, MASK_VALUE)
        sink = sinks[:, head:head + 1]
        m = jnp.maximum(s.max(-1, keepdims=True), sink)
        p = jnp.exp(s - m)
        p = (p / (p.sum(-1, keepdims=True) + jnp.exp(sink - m))).astype(BF16)
        outs.append(jnp.dot(p, vh, preferred_element_type=F32))
    return jnp.concatenate(outs, axis=1)


def _swa_fwd(h, cq, ck, cv, sinks, carry=None):
    n = h.shape[0]
    nb = n // SWA_BLOCK

    def body(q_ref, kp_ref, kc_ref, vp_ref, vc_ref, s_ref, o_ref):
        i = pl.program_id(0)
        f = lambda x: x[...].astype(F32)
        o_ref[...] = _swa_math(f(q_ref), f(kp_ref), f(kc_ref), f(vp_ref), f(vc_ref), s_ref[...], i > 0).astype(o_ref.dtype)

    prev = lambda cb: pl.BlockSpec((SWA_BLOCK, A_KV), lambda i, cb=cb: (jnp.maximum(i - 1, 0), cb))
    cur = lambda cb: pl.BlockSpec((SWA_BLOCK, A_KV), lambda i, cb=cb: (i, cb))
    return _call(body, "swa_fwd", (nb,), 0,
                 [pl.BlockSpec((SWA_BLOCK, A_Q), lambda i: (i, cq)), prev(ck), cur(ck), prev(cv), cur(cv),
                  pl.BlockSpec((1, SWA_Q_HEADS), lambda i: (0, 0))],
                 [pl.BlockSpec((SWA_BLOCK, A_Q), lambda i: (i, 0))], [SDS((n, A_Q), BF16)], [],
                 [h, h, h, h, h, sinks], ("arbitrary",), carry)


def _swa_bwd(h, cq, ck, cv, sinks, dy, carry=None):
    n = h.shape[0]
    nb = n // SWA_BLOCK

    def body(q_ref, kp_ref, kc_ref, vp_ref, vc_ref, s_ref, dy_ref, dq_ref, dk_ref, dv_ref, ds_ref, ck_ref, cv_ref):
        r = pl.program_id(0)
        blk = nb - 1 - r

        @pl.when(r == 0)
        def _():
            ds_ref[...] = jnp.zeros_like(ds_ref)
            ck_ref[...] = jnp.zeros_like(ck_ref)
            cv_ref[...] = jnp.zeros_like(cv_ref)

        f = lambda x: x[...].astype(F32)
        not_first = blk > 0
        _, vjp = jax.vjp(lambda q, kp, kc, vp, vc, s: _swa_math(q, kp, kc, vp, vc, s, not_first),
                         f(q_ref), f(kp_ref), f(kc_ref), f(vp_ref), f(vc_ref), s_ref[...])
        dq, dkp, dkc, dvp, dvc, dsk = vjp(f(dy_ref))
        dq_ref[...] = dq.astype(dq_ref.dtype)
        dk_ref[...] = (dkc + ck_ref[...]).astype(dk_ref.dtype)
        dv_ref[...] = (dvc + cv_ref[...]).astype(dv_ref.dtype)
        ck_ref[...] = dkp
        cv_ref[...] = dvp
        ds_ref[...] += dsk

    rev = lambda i: nb - 1 - i
    prev = lambda cb: pl.BlockSpec((SWA_BLOCK, A_KV), lambda i, cb=cb: (jnp.maximum(rev(i) - 1, 0), cb))
    cur = lambda cb: pl.BlockSpec((SWA_BLOCK, A_KV), lambda i, cb=cb: (rev(i), cb))
    return _call(
        body, "swa_bwd", (nb,), 0,
        [pl.BlockSpec((SWA_BLOCK, A_Q), lambda i: (rev(i), cq)), prev(ck), cur(ck), prev(cv), cur(cv),
         pl.BlockSpec((1, SWA_Q_HEADS), lambda i: (0, 0)), pl.BlockSpec((SWA_BLOCK, A_Q), lambda i: (rev(i), 0))],
        [pl.BlockSpec((SWA_BLOCK, A_Q), lambda i: (rev(i), 0)), pl.BlockSpec((SWA_BLOCK, A_KV), lambda i: (rev(i), 0)),
         pl.BlockSpec((SWA_BLOCK, A_KV), lambda i: (rev(i), 0)), pl.BlockSpec((1, SWA_Q_HEADS), lambda i: (0, 0))],
        [SDS((n, A_Q), BF16), SDS((n, A_KV), BF16), SDS((n, A_KV), BF16), SDS((1, SWA_Q_HEADS), F32)],
        [pltpu.VMEM((SWA_BLOCK, A_KV), F32), pltpu.VMEM((SWA_BLOCK, A_KV), F32)],
        [h, h, h, h, h, sinks, dy], ("arbitrary",), carry)


def _rope(x, cos, sin, sign):
    w = x.shape[1]
    reps = w // LANES
    ct = jnp.tile(cos, (1, reps)) if reps > 1 else cos
    st = jnp.tile(sin, (1, reps)) if reps > 1 else sin
    fwd = pltpu.roll(x, MLA_ROPE // 2, axis=1)
    bwd = pltpu.roll(x, w - MLA_ROPE // 2, axis=1)
    lane = lax.broadcasted_iota(jnp.int32, x.shape, 1) % LANES
    rot = jnp.where(lane < MLA_ROPE // 2, -bwd, fwd)
    return x * ct + sign * (rot * st)


def _rope_call(a, wa, ca, b, wb, cb, cos, sin, sign, tm, name):
    def fn(i, rows, ps):
        xa, xb, c_, s_ = rows
        return [_rope(xa.astype(F32), c_, s_, sign), _rope(xb.astype(F32), c_, s_, sign)], []
    (ra, rb), _ = _rowwise(fn, [(a, wa, ca), (b, wb, cb), _whole(cos), _whole(sin)], [], [(wa, BF16), (wb, BF16)], [], tm, name)
    return ra, rb


MLA_SCALE = (MLA_NOPE + MLA_ROPE) ** -0.5
LOG2E = math.log2(math.e)


def _mla_scores(qn_ref, qr_ref, kn_ref, kr_ref, masked):
    q = jnp.concatenate([qn_ref[...], qr_ref[...]], axis=1)
    k = jnp.concatenate([kn_ref[...], kr_ref[...]], axis=1)
    s = lax.dot_general(q, k, (((1,), (1,)), ((), ())), preferred_element_type=F32)
    if masked:
        row = lax.broadcasted_iota(jnp.int32, s.shape, 0)
        col = lax.broadcasted_iota(jnp.int32, s.shape, 1)
        s = jnp.where(col <= row, s, MASK_VALUE)
    return s, q, k


def _causal_pairs(nq, by_query):
    if by_query:
        pairs = [(i, j) for i in range(nq) for j in range(i + 1)]
    else:
        pairs = [(i, j) for j in range(nq) for i in range(j, nq)]
    return (jnp.asarray(np.array([p[0] for p in pairs], np.int32)), jnp.asarray(np.array([p[1] for p in pairs], np.int32)),
            len(pairs))


def _mla_fwd(q_full, qr, kv, kr, T, carry=None):
    n = q_full.shape[0]
    nq = n // T
    H = MLA_HEADS
    qi, kj, npairs = _causal_pairs(nq, True)

    def body(qi_ref, kj_ref, qn_ref, qr_ref, kn_ref, v_ref, kr_ref, y_ref, lse_ref, m_ref, l_ref, acc_ref):
        t = pl.program_id(1)
        i, j = qi_ref[t], kj_ref[t]

        @pl.when(j == 0)
        def _():
            m_ref[...] = jnp.full_like(m_ref, MASK_VALUE)
            l_ref[...] = jnp.zeros_like(l_ref)
            acc_ref[...] = jnp.zeros_like(acc_ref)

        def update(masked):
            s, _, _ = _mla_scores(qn_ref, qr_ref, kn_ref, kr_ref, masked)
            m_prev = m_ref[...]
            m_new = jnp.maximum(m_prev, s.max(-1, keepdims=True))
            p = jnp.exp2((s - m_new[:, :1]) * (MLA_SCALE * LOG2E))
            alpha = jnp.exp2((m_prev - m_new) * (MLA_SCALE * LOG2E))
            l_ref[...] = alpha * l_ref[...] + p.sum(-1, keepdims=True)
            acc_ref[...] = alpha * acc_ref[...] + jnp.dot(p.astype(BF16), v_ref[...], preferred_element_type=F32)
            m_ref[...] = m_new

        @pl.when(j < i)
        def _():
            update(False)

        @pl.when(j == i)
        def _():
            update(True)
            y_ref[...] = (acc_ref[...] / l_ref[...]).astype(y_ref.dtype)
            lse_ref[0] = m_ref[...] * (MLA_SCALE * LOG2E) + jnp.log2(l_ref[...])

    qspec = lambda off: pl.BlockSpec((T, LANES), lambda h, t, qi, kj, off=off: (qi[t], off + h))
    kspec = lambda off: pl.BlockSpec((T, LANES), lambda h, t, qi, kj, off=off: (kj[t], off + h))
    return _call(
        body, "mla_fwd", (H, npairs), 2,
        [qspec(0), qspec(0), kspec(0), kspec(H), pl.BlockSpec((T, LANES), lambda h, t, qi, kj: (kj[t], 0))],
        [pl.BlockSpec((T, LANES), lambda h, t, qi, kj: (qi[t], h)), pl.BlockSpec((1, T, LANES), lambda h, t, qi, kj: (h, qi[t], 0))],
        [SDS((n, H * MLA_V), BF16), SDS((H, n, LANES), F32)], [pltpu.VMEM((T, LANES), F32)] * 3,
        [qi, kj, q_full, qr, kv, kv, kr], ("parallel", "arbitrary"), carry)


def _mla_delta(dy, y, T):
    n = y.shape[0]
    H = MLA_HEADS

    def body(dy_ref, y_ref, d_ref):
        d = jnp.sum(dy_ref[...].astype(F32) * y_ref[...].astype(F32), axis=-1, keepdims=True)
        d_ref[0] = jnp.broadcast_to(d, (T, LANES))

    spec = pl.BlockSpec((T, LANES), lambda h, i: (i, h))
    return _pcall(body, name="mla_delta", grid=(H, n // T), out_shape=SDS((H, n, LANES), F32), in_specs=[spec, spec],
                  out_specs=pl.BlockSpec((1, T, LANES), lambda h, i: (h, i, 0)),
                  compiler_params=_params(("parallel", "parallel")))(dy, y)


def _mla_bwd_dq(q_full, qr, kv, kr, dy, lse, delta, T, carry=None):
    n = q_full.shape[0]
    nq = n // T
    H = MLA_HEADS

    qi, kj, npairs = _causal_pairs(nq, True)

    def body(qi_ref, kj_ref, qn_ref, qr_ref, kn_ref, v_ref, kr_ref, dy_ref, lse_ref, dl_ref, dqn_ref, dqr_ref, acc_ref):
        t = pl.program_id(1)
        i, j = qi_ref[t], kj_ref[t]

        @pl.when(j == 0)
        def _():
            acc_ref[...] = jnp.zeros_like(acc_ref)

        def update(masked):
            s, _, k = _mla_scores(qn_ref, qr_ref, kn_ref, kr_ref, masked)
            p = jnp.exp2(s * (MLA_SCALE * LOG2E) - lse_ref[0][:, :1])
            dp = lax.dot_general(dy_ref[...], v_ref[...], (((1,), (1,)), ((), ())), preferred_element_type=F32)
            ds = p * (dp - dl_ref[0][:, :1])
            acc_ref[...] += jnp.dot(ds.astype(BF16), k, preferred_element_type=F32)

        @pl.when(j < i)
        def _():
            update(False)

        @pl.when(j == i)
        def _():
            update(True)
            dqn_ref[...] = (acc_ref[:, :LANES] * MLA_SCALE).astype(dqn_ref.dtype)
            dqr_ref[...] = (acc_ref[:, LANES:] * MLA_SCALE).astype(dqr_ref.dtype)

    qspec = lambda off: pl.BlockSpec((T, LANES), lambda h, t, qi, kj, off=off: (qi[t], off + h))
    kspec = lambda off: pl.BlockSpec((T, LANES), lambda h, t, qi, kj, off=off: (kj[t], off + h))
    stat = pl.BlockSpec((1, T, LANES), lambda h, t, qi, kj: (h, qi[t], 0))
    out = pl.BlockSpec((T, LANES), lambda h, t, qi, kj: (qi[t], h))
    return _call(
        body, "mla_bwd_dq", (H, npairs), 2,
        [qspec(0), qspec(0), kspec(0), kspec(H), pl.BlockSpec((T, LANES), lambda h, t, qi, kj: (kj[t], 0)), qspec(0), stat, stat],
        [out, out], [SDS((n, H * LANES), BF16), SDS((n, H * LANES), BF16)], [pltpu.VMEM((T, 2 * LANES), F32)],
        [qi, kj, q_full, qr, kv, kv, kr, dy, lse, delta], ("parallel", "arbitrary"), carry)


def _mla_bwd_dkv(q_full, qr, kv, kr, dy, lse, delta, T, carry=None):
    n = q_full.shape[0]
    nq = n // T
    H = MLA_HEADS

    qi, kj, npairs = _causal_pairs(nq, False)

    def body(qi_ref, kj_ref, qn_ref, qr_ref, kn_ref, v_ref, kr_ref, dy_ref, lse_ref, dl_ref, dkn_ref, dv_ref, dkr_ref,
             dk_acc, dv_acc):
        t = pl.program_id(1)
        i, j = qi_ref[t], kj_ref[t]

        @pl.when(i == j)
        def _():
            dk_acc[...] = jnp.zeros_like(dk_acc)
            dv_acc[...] = jnp.zeros_like(dv_acc)

        def update(masked):
            s, q, _ = _mla_scores(qn_ref, qr_ref, kn_ref, kr_ref, masked)
            p = jnp.exp2(s * (MLA_SCALE * LOG2E) - lse_ref[0][:, :1])
            dy = dy_ref[...]
            dv_acc[...] += lax.dot_general(p.astype(BF16), dy, (((0,), (0,)), ((), ())), preferred_element_type=F32)
            dp = lax.dot_general(dy, v_ref[...], (((1,), (1,)), ((), ())), preferred_element_type=F32)
            ds = p * (dp - dl_ref[0][:, :1])
            dk_acc[...] += lax.dot_general(ds.astype(BF16), q, (((0,), (0,)), ((), ())), preferred_element_type=F32)

        @pl.when(i == j)
        def _():
            update(True)

        @pl.when(i > j)
        def _():
            update(False)

        @pl.when(i == nq - 1)
        def _():
            dkn_ref[...] = (dk_acc[:, :LANES] * MLA_SCALE).astype(dkn_ref.dtype)
            dv_ref[...] = dv_acc[...].astype(dv_ref.dtype)
            dkr_ref[0] = dk_acc[:, LANES:] * MLA_SCALE

    qspec = lambda off: pl.BlockSpec((T, LANES), lambda h, t, qi, kj, off=off: (qi[t], off + h))
    kspec = lambda off: pl.BlockSpec((T, LANES), lambda h, t, qi, kj, off=off: (kj[t], off + h))
    stat = pl.BlockSpec((1, T, LANES), lambda h, t, qi, kj: (h, qi[t], 0))
    out = pl.BlockSpec((T, LANES), lambda h, t, qi, kj: (kj[t], h))
    return _call(
        body, "mla_bwd_dkv", (H, npairs), 2,
        [qspec(0), qspec(0), kspec(0), kspec(H), pl.BlockSpec((T, LANES), lambda h, t, qi, kj: (kj[t], 0)), qspec(0), stat, stat],
        [out, out, pl.BlockSpec((1, T, LANES), lambda h, t, qi, kj: (h, kj[t], 0))],
        [SDS((n, H * LANES), BF16), SDS((n, H * LANES), BF16), SDS((H, n, LANES), F32)],
        [pltpu.VMEM((T, 2 * LANES), F32), pltpu.VMEM((T, LANES), F32)],
        [qi, kj, q_full, qr, kv, kv, kr, dy, lse, delta], ("parallel", "arbitrary"), carry)


def _sum_heads(a, tm):
    H, n, _ = a.shape

    def body(a_ref, o_ref):
        o_ref[...] = jnp.sum(a_ref[...], axis=0)

    return _pcall(body, name="mla_sum_heads", grid=(n // tm,), out_shape=SDS((n, LANES), F32),
                  in_specs=[pl.BlockSpec((H, tm, LANES), lambda i: (0, i, 0))],
                  out_specs=pl.BlockSpec((tm, LANES), lambda i: (i, 0)), compiler_params=_params(("parallel",)))(a)


def _shift_down(x, k):
    row = lax.broadcasted_iota(jnp.int32, x.shape, 0)
    return jnp.where(row >= k, pltpu.roll(x, k, axis=0), 0.0)


def _shift_up(x, k):
    n = x.shape[0]
    row = lax.broadcasted_iota(jnp.int32, x.shape, 0)
    return jnp.where(row < n - k, pltpu.roll(x, n - k, axis=0), 0.0)


def _conv_fwd(up, w, b):
    n, c = up.shape

    def body(u_ref, w_ref, b_ref, o_ref):
        u = u_ref[...].astype(F32)
        wv = w_ref[...]
        o_ref[...] = (b_ref[...] + wv[0:1] * _shift_down(u, 2) + wv[1:2] * _shift_down(u, 1) + wv[2:3] * u).astype(o_ref.dtype)

    return _pcall(body, name="conv_fwd", grid=(c // LANES,), out_shape=SDS((n, c), BF16),
                  in_specs=[pl.BlockSpec((n, LANES), lambda j: (0, j)), pl.BlockSpec((3, LANES), lambda j: (0, j)),
                            pl.BlockSpec((1, LANES), lambda j: (0, j))],
                  out_specs=pl.BlockSpec((n, LANES), lambda j: (0, j)), compiler_params=_params(("parallel",)))(up, w, b)


def _conv_bwd(up, dc, w):
    n, c = up.shape

    def body(u_ref, d_ref, w_ref, du_ref, dw_ref, db_ref):
        u = u_ref[...].astype(F32)
        d = d_ref[...].astype(F32)
        wv = w_ref[...]
        du_ref[...] = (wv[2:3] * d + wv[1:2] * _shift_up(d, 1) + wv[0:1] * _shift_up(d, 2)).astype(du_ref.dtype)
        dw_ref[0:1, :] = jnp.sum(d * _shift_down(u, 2), axis=0, keepdims=True)
        dw_ref[1:2, :] = jnp.sum(d * _shift_down(u, 1), axis=0, keepdims=True)
        dw_ref[2:3, :] = jnp.sum(d * u, axis=0, keepdims=True)
        db_ref[...] = jnp.sum(d, axis=0, keepdims=True)

    col = pl.BlockSpec((n, LANES), lambda j: (0, j))
    return _pcall(body, name="conv_bwd", grid=(c // LANES,),
                  out_shape=[SDS((n, c), BF16), SDS((3, c), F32), SDS((1, c), F32)],
                  in_specs=[col, col, pl.BlockSpec((3, LANES), lambda j: (0, j))],
                  out_specs=[col, pl.BlockSpec((3, LANES), lambda j: (0, j)), pl.BlockSpec((1, LANES), lambda j: (0, j))],
                  compiler_params=_params(("parallel",)))(up, dc, w)


def _adamw_math(w, g, m, v):
    m = ADAM_B1 * m + (1.0 - ADAM_B1) * g
    v = ADAM_B2 * v + (1.0 - ADAM_B2) * jnp.square(g)
    m_hat = m / (1.0 - ADAM_B1 ** ADAM_STEP)
    v_hat = v / (1.0 - ADAM_B2 ** ADAM_STEP)
    delta = -ADAM_LR * (m_hat / (jnp.sqrt(v_hat) + ADAM_EPS) + ADAM_WD * w)
    return delta, m, v


def _adamw(w, g, m, v, name):
    r, c = w.shape
    tr = r
    budget = max(8, (1 << 20) // (4 * c))
    t = (min(budget, r) // 8) * 8
    while t >= 8:
        if r % t == 0:
            tr = t
            break
        t -= 8

    def fn(i, rows, ps):
        return list(_adamw_math(*rows)), []
    (d, nm, nv), _ = _rowwise(fn, [_whole(w), _whole(g), _whole(m), _whole(v)], [], [(c, F32)] * 3, [], tr, name)
    return d, nm, nv


def _coords():
    return lax.axis_index("x"), lax.axis_index("y"), lax.axis_index("c")


def _other_chips(x, y):
    return [(1 - x, y), (x, 1 - y), (1 - x, 1 - y)]


HBM_SPEC = pl.BlockSpec(memory_space=pltpu.HBM)


def _half(c, rows):
    return pl.ds(pl.multiple_of(c * rows, 16), rows)


def _rows_tile(rows, cols, budget_bytes=2 << 20, align=16):
    t = (min(max(align, budget_bytes // (4 * cols)), rows) // align) * align
    while t >= align:
        if rows % t == 0:
            return t
        t -= align
    return rows


def _scalar(v):
    return jnp.reshape(jnp.asarray(v, jnp.int32), (1,))


def _cast_into_slot(w3, layer, slot, name):
    _, R, C = w3.shape
    tr = _rows_tile(R, C)

    def body(s_ref, w_ref, o_ref):
        o_ref[0] = w_ref[0].astype(BF16)

    gs = pltpu.PrefetchScalarGridSpec(
        num_scalar_prefetch=1, grid=(R // tr,),
        in_specs=[pl.BlockSpec((1, tr, C), lambda i, s: (layer, i, 0))],
        out_specs=pl.BlockSpec((1, tr, C), lambda i, s: (s[0], i, 0)))
    return _pcall(body, name=name, grid_spec=gs, out_shape=SDS((N_CHIPS, R, C), BF16),
                  compiler_params=_params(("arbitrary",)))(_scalar(slot), w3)


class _Stage:
    def __init__(self, ins, out_shapes, aliases, n_sems, copies):
        self.ins, self.out_shapes, self.aliases, self.n_sems, self.copies = list(ins), out_shapes, aliases, n_sems, copies

    def start(self, ins, outs, send_sems, recv_sems):
        for cp in self.copies(ins, outs, send_sems, recv_sems)[0]:
            cp.start()

    def finish(self, ins, outs, send_sems, recv_sems):
        sends, arrivals = self.copies(ins, outs, send_sems, recv_sems)
        for cp in arrivals:
            cp.wait_recv()
        for cp in sends:
            cp.wait_send()


def _remote(src, dst, send_sems, recv_sems, k, to):
    return pltpu.make_async_remote_copy(src_ref=src, dst_ref=dst, send_sem=send_sems.at[k], recv_sem=recv_sems.at[k],
                                        device_id=to, device_id_type=MESH)


def _gather_stages(bufs):
    n = len(bufs)
    shapes = [SDS(b.shape, b.dtype) for b in bufs]
    same = {i: i for i in range(n)}

    def over_ici(ins, outs, send_sems, recv_sems):
        x, y, c = _coords()
        blk = lambda w, chip: outs[w].at[chip, _half(c, outs[w].shape[1] // 2), :]
        sends, arrivals = [], []
        for w in range(n):
            for j, (px, py) in enumerate(_other_chips(x, y)):
                sends.append(_remote(blk(w, 2 * x + y), blk(w, 2 * x + y), send_sems, recv_sems, 3 * w + j, (px, py, c)))
                arrivals.append(_remote(blk(w, 2 * px + py), blk(w, 2 * px + py), send_sems, recv_sems, 3 * w + j, (px, py, c)))
        return sends, arrivals

    def to_sibling(ins, outs, send_sems, recv_sems):
        x, y, c = _coords()
        blk = lambda w, chip, half: outs[w].at[chip, _half(half, outs[w].shape[1] // 2), :]
        sends, arrivals = [], []
        for w in range(n):
            for j, (px, py) in enumerate(_other_chips(x, y)):
                k = 2 * px + py
                sends.append(_remote(blk(w, k, c), blk(w, k, c), send_sems, recv_sems, 3 * w + j, (x, y, 1 - c)))
                arrivals.append(_remote(blk(w, k, 1 - c), blk(w, k, 1 - c), send_sems, recv_sems, 3 * w + j, (x, y, 1 - c)))
        return sends, arrivals

    return (lambda b: _Stage(b, shapes, same, 3 * n, over_ici)), (lambda b: _Stage(b, shapes, same, 3 * n, to_sibling))


def _swap_stage(gs_):
    n = len(gs_)

    def copies(ins, outs, send_sems, recv_sems):
        x, y, c = _coords()
        cps = [_remote(ins[w].at[:, _half(1 - c, ins[w].shape[1] // 2), :], outs[w], send_sems, recv_sems, w, (x, y, 1 - c))
               for w in range(n)]
        return cps, cps

    return _Stage(gs_, [SDS((N_CHIPS, g.shape[1] // 2, g.shape[2]), g.dtype) for g in gs_], {}, n, copies)


def _scatter_stage(ps):
    n = len(ps)

    def copies(ins, outs, send_sems, recv_sems):
        x, y, c = _coords()
        cps = [_remote(ins[w].at[2 * px + py], outs[w].at[j], send_sems, recv_sems, 3 * w + j, (px, py, c))
               for w in range(n) for j, (px, py) in enumerate(_other_chips(x, y))]
        return cps, cps

    return _Stage(ps, [SDS((3,) + p.shape[1:], p.dtype) for p in ps], {}, 3 * n, copies)


def _join_stage(bufs):
    n = len(bufs)
    L = bufs[0].shape[0]

    def copies(ins, outs, send_sems, recv_sems):
        x, y, c = _coords()
        blk = lambda w, l, half: outs[w].at[l, _half(half, outs[w].shape[1] // 2), :]
        sends = [_remote(blk(w, l, c), blk(w, l, c), send_sems, recv_sems, L * w + l, (x, y, 1 - c))
                 for w in range(n) for l in range(L)]
        arrivals = [_remote(blk(w, l, 1 - c), blk(w, l, 1 - c), send_sems, recv_sems, L * w + l, (x, y, 1 - c))
                    for w in range(n) for l in range(L)]
        return sends, arrivals

    return _Stage(bufs, [SDS(b.shape, b.dtype) for b in bufs], {i: i for i in range(n)}, L * n, copies)


def _stage_scratch(stage):
    return [pltpu.SemaphoreType.DMA((stage.n_sems,)), pltpu.SemaphoreType.DMA((stage.n_sems,))]


def _run_stage(stage, name):
    n_in, n_out = len(stage.ins), len(stage.out_shapes)

    def body(*refs):
        ins, outs, send_sems, recv_sems = refs[:n_in], refs[n_in:n_in + n_out], refs[n_in + n_out], refs[n_in + n_out + 1]
        stage.start(ins, outs, send_sems, recv_sems)
        stage.finish(ins, outs, send_sems, recv_sems)

    return _pcall(body, name=name, out_shape=stage.out_shapes, in_specs=[HBM_SPEC] * n_in, out_specs=[HBM_SPEC] * n_out,
                  input_output_aliases=stage.aliases, scratch_shapes=_stage_scratch(stage))(*stage.ins)


def _call(body, name, grid, n_prefetch, in_specs, out_specs, out_shape, scratch, operands, semantics, carry=None):
    n_in, n_out, n_sc = len(in_specs), len(out_specs), len(scratch)
    if carry is None:
        gs = pltpu.PrefetchScalarGridSpec(num_scalar_prefetch=n_prefetch, grid=grid, in_specs=in_specs, out_specs=out_specs,
                                          scratch_shapes=scratch)
        res = _pcall(body, name=name, grid_spec=gs, out_shape=out_shape, compiler_params=_params(semantics))(*operands)
        return list(res), []
    s_in, s_out = len(carry.ins), len(carry.out_shapes)

    def carrying(*refs):
        o = n_prefetch
        pre, ins = refs[:o], refs[o:o + n_in]
        o += n_in
        sins = refs[o:o + s_in]
        o += s_in
        outs = refs[o:o + n_out]
        o += n_out
        souts = refs[o:o + s_out]
        o += s_out
        sc, send_sems, recv_sems = refs[o:o + n_sc], refs[o + n_sc], refs[o + n_sc + 1]
        first = functools.reduce(jnp.logical_and, [pl.program_id(a) == 0 for a in range(len(grid))])
        last = functools.reduce(jnp.logical_and, [pl.program_id(a) == g - 1 for a, g in enumerate(grid)])

        @pl.when(first)
        def _():
            carry.start(sins, souts, send_sems, recv_sems)

        body(*pre, *ins, *outs, *sc)

        @pl.when(last)
        def _():
            carry.finish(sins, souts, send_sems, recv_sems)

    gs = pltpu.PrefetchScalarGridSpec(
        num_scalar_prefetch=n_prefetch, grid=grid, in_specs=list(in_specs) + [HBM_SPEC] * s_in,
        out_specs=list(out_specs) + [HBM_SPEC] * s_out, scratch_shapes=list(scratch) + _stage_scratch(carry))
    aliases = {n_prefetch + n_in + a: n_out + b for a, b in carry.aliases.items()}
    res = _pcall(carrying, name=name, grid_spec=gs, out_shape=list(out_shape) + list(carry.out_shapes),
                 input_output_aliases=aliases, compiler_params=_params(("arbitrary",) * len(grid)))(*operands, *carry.ins)
    return list(res[:n_out]), list(res[n_out:])


def _all_reduce_small(v):
    n = v.shape[0]

    def body(v_ref, out_ref, slots, send_sems, recv_sems):
        x, y, c = _coords()
        me = 4 * x + 2 * y + c
        cps = []
        for r in range(1, 8):
            t = (me + r) % 8
            cp = pltpu.make_async_remote_copy(src_ref=v_ref, dst_ref=slots.at[me], send_sem=send_sems.at[r - 1],
                                              recv_sem=recv_sems.at[me], device_id=(t // 4, (t // 2) % 2, t % 2),
                                              device_id_type=MESH)
            cp.start()
            cps.append(cp)
        slots[me] = v_ref[...]
        for r in range(1, 8):
            s = (me + r) % 8
            pltpu.make_async_remote_copy(src_ref=v_ref, dst_ref=slots.at[s], send_sem=send_sems.at[r - 1],
                                         recv_sem=recv_sems.at[s], device_id=(x, y, c), device_id_type=MESH).wait_recv()
        for cp in cps:
            cp.wait_send()
        acc = slots[0]
        for d in range(1, 8):
            acc = acc + slots[d]
        out_ref[...] = acc

    return _pcall(body, name="all_reduce_small", out_shape=SDS((n, LANES), F32),
                  in_specs=[pl.BlockSpec(memory_space=pltpu.VMEM)], out_specs=pl.BlockSpec(memory_space=pltpu.VMEM),
                  scratch_shapes=[pltpu.VMEM((8, n, LANES), F32), pltpu.SemaphoreType.DMA((7,)), pltpu.SemaphoreType.DMA((8,))],
                  compiler_params=pltpu.CompilerParams(vmem_limit_bytes=VMEM_LIMIT))(v)


def _add_half(g, recv, c, name):
    _, R, C = g.shape
    rows = R // 2
    tr = _rows_tile(rows, C, 1 << 20)
    nb = rows // tr

    def body(s_ref, g_ref, r_ref, o32_ref, o16_ref):
        s = g_ref[...] + r_ref[...]
        o32_ref[...] = s
        o16_ref[...] = s.astype(BF16)

    blk = lambda k, i, s: (k, i, 0)
    gs = pltpu.PrefetchScalarGridSpec(
        num_scalar_prefetch=1, grid=(N_CHIPS, nb),
        in_specs=[pl.BlockSpec((1, tr, C), lambda k, i, s: (k, s[0] * nb + i, 0)), pl.BlockSpec((1, tr, C), blk)],
        out_specs=[pl.BlockSpec((1, tr, C), blk), pl.BlockSpec((1, tr, C), blk)])
    return _pcall(body, name=name, grid_spec=gs, out_shape=[SDS((N_CHIPS, rows, C), F32), SDS((N_CHIPS, rows, C), BF16)],
                  compiler_params=_params(("arbitrary", "arbitrary")))(_scalar(c), g, recv)


def _sum_into(p32, arrived, chip, c, layer, n_layers, prev, name):
    _, rows, C = p32.shape
    tr = _rows_tile(rows, C, 1 << 20)
    nb = rows // tr

    def body(chip_ref, c_ref, p_ref, a_ref, *rest):
        o_ref = rest[-1]
        o_ref[0] = ((p_ref[0] + a_ref[0].astype(F32)) + a_ref[1].astype(F32)) + a_ref[2].astype(F32)

    in_specs = [pl.BlockSpec((1, tr, C), lambda i, chip_ref, c_ref: (chip_ref[0], i, 0)),
                pl.BlockSpec((3, tr, C), lambda i, chip_ref, c_ref: (0, i, 0))]
    ins = [p32, arrived]
    aliases = {}
    if prev is not None:
        in_specs.append(pl.BlockSpec(memory_space=pl.ANY))
        ins.append(prev)
        aliases = {4: 0}
    gs = pltpu.PrefetchScalarGridSpec(
        num_scalar_prefetch=2, grid=(nb,), in_specs=in_specs,
        out_specs=pl.BlockSpec((1, tr, C), lambda i, chip_ref, c_ref: (layer, c_ref[0] * nb + i, 0)))
    return _pcall(body, name=name, grid_spec=gs, out_shape=SDS((n_layers, 2 * rows, C), F32), input_output_aliases=aliases,
                  compiler_params=_params(("arbitrary",)))(_scalar(chip), _scalar(c), *ins)


def _h_layout(D):
    G = N_BRANCHES * D
    off, o = {}, 0
    for name, w in [("g", G), ("qa", A_Q), ("cq", MLA_Q_RANK), ("ckv", MLA_KV_RANK), ("hu", SGU_WIDTH), ("hv", SGU_WIDTH),
                    ("ka", A_KV), ("va", A_KV), ("kr", LANES)]:
        assert o % w == 0, (name, o, w)
        off[name] = (o, w)
        o += w
    off["total"] = -(-o // 512) * 512
    return off


def _perm_w_in(w, lay):
    s = np.cumsum([0, A_Q, A_KV, A_KV, MLA_Q_RANK, MLA_KV_RANK, MLA_ROPE, SGU_WIDTH, SGU_WIDTH])
    qa, ka, va, cq, ckv, kr, hu, hv = [w[:, s[i]:s[i + 1]] for i in range(8)]
    g = w[:, s[8]:]
    pad = jnp.zeros((w.shape[0], lay["total"] - lay["kr"][0] - MLA_ROPE), w.dtype)
    return jnp.concatenate([g, qa, cq, ckv, hu, hv, ka, va, kr, pad], axis=1)


def _unperm_w_in(wp, lay, D):
    take = lambda n, width=None: wp[:, lay[n][0]:lay[n][0] + (width or lay[n][1])]
    return jnp.concatenate([take("qa"), take("ka"), take("va"), take("cq"), take("ckv"), take("kr", MLA_ROPE), take("hu"),
                            take("hv"), take("g")], axis=1)


def _perm_w_uq(w):
    r = w.shape[0]
    w3 = w.reshape(r, MLA_HEADS, MLA_NOPE + MLA_ROPE)
    nope = w3[:, :, :MLA_NOPE].reshape(r, MLA_HEADS * MLA_NOPE)
    rope = jnp.pad(w3[:, :, MLA_NOPE:], ((0, 0), (0, 0), (0, LANES - MLA_ROPE))).reshape(r, MLA_HEADS * LANES)
    return jnp.concatenate([nope, rope], axis=1)


def _unperm_w_uq(wp):
    r = wp.shape[0]
    nope = wp[:, :MLA_HEADS * MLA_NOPE].reshape(r, MLA_HEADS, MLA_NOPE)
    rope = wp[:, MLA_HEADS * MLA_NOPE:].reshape(r, MLA_HEADS, LANES)[:, :, :MLA_ROPE]
    return jnp.concatenate([nope, rope], axis=2).reshape(r, MLA_HEADS * (MLA_NOPE + MLA_ROPE))


def _perm_w_ukv(w):
    r = w.shape[0]
    w3 = w.reshape(r, MLA_HEADS, MLA_NOPE + MLA_V)
    return jnp.concatenate([w3[:, :, :MLA_NOPE].reshape(r, -1), w3[:, :, MLA_NOPE:].reshape(r, -1)], axis=1)


def _unperm_w_ukv(wp):
    r = wp.shape[0]
    k = wp[:, :MLA_HEADS * MLA_NOPE].reshape(r, MLA_HEADS, MLA_NOPE)
    v = wp[:, MLA_HEADS * MLA_NOPE:].reshape(r, MLA_HEADS, MLA_V)
    return jnp.concatenate([k, v], axis=2).reshape(r, -1)


def _col_chunks(g):
    r, c4 = g.shape
    return jnp.transpose(g.reshape(r, N_CHIPS, c4 // N_CHIPS), (1, 0, 2))


def kernel(x, positions, w_in, b_gate, sinks, q_norm_g, kv_norm_g, w_uq, w_ukv, sgu_ln_g, sgu_ln_b, sgu_w, sgu_b, w_proj_a, w_proj_b, w_proj_c, w_o, ln1_g, ln1_b, w_up, conv_w, conv_b, w_down, ln2_g, ln2_b, loss_target, m_w_in, m_b_gate, m_sinks, m_q_norm_g, m_kv_norm_g, m_w_uq, m_w_ukv, m_sgu_ln_g, m_sgu_ln_b, m_sgu_w, m_sgu_b, m_w_proj_a, m_w_proj_b, m_w_proj_c, m_w_o, m_ln1_g, m_ln1_b, m_w_up, m_conv_w, m_conv_b, m_w_down, m_ln2_g, m_ln2_b, v_w_in, v_b_gate, v_sinks, v_q_norm_g, v_kv_norm_g, v_w_uq, v_w_ukv, v_sgu_ln_g, v_sgu_ln_b, v_sgu_w, v_sgu_b, v_w_proj_a, v_w_proj_b, v_w_proj_c, v_w_o, v_ln1_g, v_ln1_b, v_w_up, v_conv_w, v_conv_b, v_w_down, v_ln2_g, v_ln2_b):
    a = locals()
    W = {k: a[k] for k in WEIGHTS}
    Mo = {k: a["m_" + k] for k in WEIGHTS}
    Vo = {k: a["v_" + k] for k in WEIGHTS}
    S, D = x.shape[1], x.shape[2]
    FF2 = w_up.shape[2] * N_CHIPS
    FF = FF2 // 2
    L = DEPTH
    lay = _h_layout(D)
    NP = lay["total"]
    cx, cy, cc = _coords()
    chip = 2 * cx + cy
    T = _tile(S, 512)
    TM = _tile(S, 256, 16)
    TMW = _tile(S, 64, 16)

    shards = {k: tuple(W[k].shape) for k in BIG}
    full = {k: [None] * L for k in BIG}
    own = {(l, k): _cast_into_slot(W[k], l, chip, f"cast_{k}_l{l}") for l in range(L) for k in BIG}

    def over_ici(pairs):
        return _gather_stages([own[p_] for p_ in pairs])[0]([own[p_] for p_ in pairs]) if pairs else None

    def landed(pairs, arrived, tag):
        if not pairs:
            return
        gathered = _run_stage(_gather_stages(arrived)[1](arrived), f"gather_sibling_{tag}")
        for (l_, k), g in zip(pairs, gathered):
            _, r, c_ = shards[k]
            full[k][l_] = g.reshape(N_CHIPS * r, c_) if k in ROW_SHARDED else jnp.transpose(g, (1, 0, 2)).reshape(r, N_CHIPS * c_)

    MIX = ["w_uq", "w_ukv", "w_proj_a", "w_proj_b", "w_proj_c", "w_o"]
    nxt = lambda l, names: [(l + 1, k) for k in names] if l + 1 < L else []
    first = [(0, "w_in")]
    landed(first, _run_stage(over_ici(first), "gather_ici_w_in_l0"), "w_in_l0")

    small_sharded_full = {k: tuple(W[k].shape[:-1]) + (W[k].shape[-1] * N_CHIPS,) for k in SMALL_SHARDED}
    placed = []
    for k in ("b_gate", "conv_w"):
        z = jnp.zeros(small_sharded_full[k], F32)
        z = lax.dynamic_update_slice_in_dim(z, W[k], chip * W[k].shape[-1], axis=-1)
        placed.append(jnp.where(cc == 0, z, 0.0).reshape(-1))
    pv = jnp.concatenate(placed)
    n_pv = pv.shape[0]
    pv = jnp.pad(pv, (0, -n_pv % (8 * LANES))).reshape(-1, LANES)
    pv = _all_reduce_small(pv).reshape(-1)
    nb_ = int(np.prod(small_sharded_full["b_gate"]))
    b_gate_full = pv[:nb_].reshape(small_sharded_full["b_gate"])
    conv_w_full = pv[nb_:n_pv].reshape(small_sharded_full["conv_w"])

    inv_freq = ROPE_THETA ** (-jnp.arange(0, MLA_ROPE, 2, dtype=F32) / MLA_ROPE)
    ang = positions[0].astype(F32)[:, None] * inv_freq
    cos, sin = jnp.cos(ang), jnp.sin(ang)
    cos_t = jnp.concatenate([cos, cos, jnp.ones((S, LANES - MLA_ROPE), F32)], axis=1)
    sin_t = jnp.concatenate([sin, sin, jnp.zeros((S, LANES - MLA_ROPE), F32)], axis=1)

    row = lambda v: v.reshape(1, -1)
    cb = lambda name: lay[name][0] // lay[name][1]

    xs = x[0]
    saved = []
    for l in range(L):
        p = dict(
            w_in=_perm_w_in(full["w_in"][l], lay),
            sinks=row(sinks[l]), qg=row(q_norm_g[l]), kvg=row(kv_norm_g[l]), sg=row(sgu_ln_g[l]), sb=row(sgu_ln_b[l]),
            sw=sgu_w[l], sb3=sgu_b[l].reshape(SGU_GROUPS, SGU_CHUNK, 1),
            bg=b_gate_full[l], l1g=row(ln1_g[l]), l1b=row(ln1_b[l]), cw=conv_w_full[l], cbias=row(conv_b[l]),
            l2g=row(ln2_g[l]), l2b=row(ln2_b[l]))
        if l == 0:
            def fn_cast(i, rows, ps):
                return [rows[0]], []
            (xb,), _ = _rowwise(fn_cast, [_whole(xs)], [], [(D, BF16)], [], TM, "cast_x")
        own_mix = [(l, k) for k in MIX] if l == 0 else []
        own_up = [(l, "w_up")] if l == 0 else []
        own_down = [(l, "w_down")] if l == 0 else []
        if own_mix:
            h, arrived = _mm(xb, p["w_in"], "nn", BF16, "mm_h", carry=over_ici(own_mix))
            landed(own_mix, arrived, f"mix_l{l}")
        else:
            h = _mm(xb, p["w_in"], "nn", BF16, "mm_h")
        (y_a,), arrived = _swa_fwd(h, cb("qa"), cb("ka"), cb("va"), p["sinks"], carry=over_ici(own_up))
        landed(own_up, arrived, f"up_l{l}")
        p.update(w_uq=_perm_w_uq(full["w_uq"][l]), w_ukv=_perm_w_ukv(full["w_ukv"][l]), w_pa=full["w_proj_a"][l],
                 w_pb=full["w_proj_b"][l], w_pc=full["w_proj_c"][l], w_o=full["w_o"][l])
        def fn_rms(i, rows, ps):
            return [_rms_norm(rows[0].astype(F32), ps[0]), _rms_norm(rows[1].astype(F32), ps[1])], []
        (cqn, ckvn), _ = _rowwise(fn_rms, [(h, MLA_Q_RANK, cb("cq")), (h, MLA_KV_RANK, cb("ckv"))], [p["qg"], p["kvg"]],
                                  [(MLA_Q_RANK, BF16), (MLA_KV_RANK, BF16)], [], TM, "mla_rms")
        q_full = _mm(cqn, p["w_uq"], "nn", BF16, "mm_q")
        kv = _mm(ckvn, p["w_ukv"], "nn", BF16, "mm_kv")
        qr, kr = _rope_call(q_full, MLA_HEADS * LANES, 1, h, LANES, cb("kr"), cos_t, sin_t, 1.0, TM, "rope_fwd")
        behind_mla = own_down + nxt(l, ["w_in"] + MIX)
        (y_b, lse), arrived = _mla_fwd(q_full, qr, kv, kr, T, carry=over_ici(behind_mla))
        landed(behind_mla, arrived, f"behind_mla_l{l}")
        y_c = _sgu_fwd(h, cb("hu"), cb("hv"), p["sg"], p["sb"], p["sw"], p["sb3"])
        pa = _mm(y_a, p["w_pa"], "nn", F32, "mm_pa")
        pb = _mm(y_b, p["w_pb"], "nn", F32, "mm_pb")
        pc = _mm(y_c, p["w_pc"], "nn", F32, "mm_pc")

        def merge_math(pa_, pb_, pc_, g_, b0, b1, b2):
            out = 0.0
            for br, (pp, bb) in enumerate(zip((pa_, pb_, pc_), (b0, b1, b2))):
                gate = jax.nn.sigmoid(g_[:, br * D:(br + 1) * D].astype(F32) + bb)
                out = out + gate * pp
            return out

        def fn_merge(i, rows, ps):
            bgv = ps[0]
            return [merge_math(rows[0], rows[1], rows[2], rows[3], bgv[0:1], bgv[1:2], bgv[2:3])], []
        (merged,), _ = _rowwise(fn_merge, [_whole(pa), _whole(pb), _whole(pc), (h, N_BRANCHES * D, cb("g"))], [p["bg"]],
                                [(D, BF16)], [], TMW, "merge_fwd")
        o = _mm(merged, p["w_o"], "nn", F32, "mm_o")

        def ln_res_math(x_, o_, g_, b_):
            return _layer_norm(DN_ALPHA * x_ + o_, g_, b_)

        def fn_ln(i, rows, ps):
            y = ln_res_math(rows[0], rows[1], ps[0], ps[1])
            return [y, y], []
        (x1, x1b), _ = _rowwise(fn_ln, [_whole(xs), _whole(o)], [p["l1g"], p["l1b"]], [(D, F32), (D, BF16)], [], TM, "ln1_fwd")
        p.update(w_up=full["w_up"][l], w_down=full["w_down"][l])
        if nxt(l, ["w_up"]):
            up, arrived = _mm(x1b, p["w_up"], "nn", BF16, "mm_up", carry=over_ici(nxt(l, ["w_up"])))
            landed(nxt(l, ["w_up"]), arrived, f"up_l{l + 1}")
        else:
            up = _mm(x1b, p["w_up"], "nn", BF16, "mm_up")
        cv_ = _conv_fwd(up, p["cw"], p["cbias"])

        def glu_math(cg, cvv):
            return jax.nn.silu(cg.astype(F32)) * cvv.astype(F32)

        def fn_glu(i, rows, ps):
            return [glu_math(rows[0], rows[1])], []
        (act,), _ = _rowwise(fn_glu, [(cv_, FF, 0), (cv_, FF, 1)], [], [(FF, BF16)], [], TMW, "glu_fwd")
        if nxt(l, ["w_down"]):
            dn, arrived = _mm(act, p["w_down"], "nn", F32, "mm_down", carry=over_ici(nxt(l, ["w_down"])))
            landed(nxt(l, ["w_down"]), arrived, f"down_l{l + 1}")
        else:
            dn = _mm(act, p["w_down"], "nn", F32, "mm_down")
        (x2, x2b), _ = _rowwise(fn_ln, [_whole(x1), _whole(dn)], [p["l2g"], p["l2b"]], [(D, F32), (D, BF16)], [], TM, "ln2_fwd")
        saved.append(dict(p=p, x0=xs, x0b=xb, h=h, y_a=y_a, cqn=cqn, ckvn=ckvn, q_full=q_full, kv=kv, qr=qr, kr=kr, y_b=y_b,
                          lse=lse, y_c=y_c, pa=pa, pb=pb, pc=pc, merged=merged, o=o, x1=x1, x1b=x1b, up=up, cv=cv_, act=act,
                          dn=dn))
        xs, xb = x2, x2b

    def fn_loss(i, rows, ps):
        diff = rows[0] - rows[1]
        part = jnp.sum(jnp.mean(jnp.square(diff), axis=-1, keepdims=True), axis=0, keepdims=True)
        return [diff * (1.0 / D)], [jnp.broadcast_to(part, (8, LANES))]
    (dx,), (loss_acc,) = _rowwise(fn_loss, [_whole(xs), _whole(loss_target[0])], [], [(D, F32)], [(8, LANES)], TM, "loss")
    loss = lax.psum(0.5 * loss_acc[0, 0], ("x", "y", "c"))

    gbig = {k: [None] * L for k in BIG}
    gsmall = {k: [None] * L for k in SMALL}
    reduced = {}
    pending = None
    for l in reversed(range(L)):
        s = saved[l]
        p = s["p"]

        def fn_ln_bwd(i, rows, ps):
            _, vjp = jax.vjp(ln_res_math, rows[0], rows[1], ps[0], ps[1])
            dx_, do_, dg_, db_ = vjp(rows[2])
            return [dx_, do_], [dg_, db_]
        (dx1_res, ddn), (g_l2g, g_l2b) = _rowwise(fn_ln_bwd, [_whole(s["x1"]), _whole(s["dn"]), _whole(dx)], [p["l2g"], p["l2b"]],
                                                  [(D, F32), (D, BF16)], [(1, D), (1, D)], TM, "ln2_bwd")
        gsmall["ln2_g"][l], gsmall["ln2_b"][l] = g_l2g, g_l2b
        row_chunks = lambda g: g.reshape(N_CHIPS, g.shape[0] // N_CHIPS, g.shape[1])
        gbig["w_down"][l] = row_chunks(_mm(s["act"], ddn, "tn", F32, "mm_dw_down"))
        dact = _mm(ddn, p["w_down"], "nt", BF16, "mm_dact")

        def fn_glu_bwd(i, rows, ps):
            _, vjp = jax.vjp(glu_math, rows[0], rows[1])
            dcg, dcv = vjp(rows[2].astype(F32))
            return [jnp.concatenate([dcg, dcv], axis=1)], []
        (dc,), _ = _rowwise(fn_glu_bwd, [(s["cv"], FF, 0), (s["cv"], FF, 1), _whole(dact)], [], [(FF2, BF16)], [], TMW, "glu_bwd")
        dup, g_cw, g_cb = _conv_bwd(s["up"], dc, p["cw"])
        gsmall["conv_w"][l], gsmall["conv_b"][l] = g_cw, g_cb
        if pending is None:
            gbig["w_up"][l] = _mm(s["x1b"], dup, "tn", F32, "mm_dw_up", col_chunks=N_CHIPS, tm=2048)
        else:
            gbig["w_up"][l], from_sibling = _mm(s["x1b"], dup, "tn", F32, "mm_dw_up", col_chunks=N_CHIPS, tm=2048,
                                                carry=_swap_stage(pending))
            partial = [_add_half(g, r, cc, f"rs_add_{k}_l{l + 1}") for k, g, r in zip(BIG, pending, from_sibling)]
        dx1 = _mm(dup, p["w_up"], "nt", F32, "mm_dx1", add=dx1_res)
        (dx0_res, do_), (g_l1g, g_l1b) = _rowwise(fn_ln_bwd, [_whole(s["x0"]), _whole(s["o"]), _whole(dx1)], [p["l1g"], p["l1b"]],
                                                  [(D, F32), (D, BF16)], [(1, D), (1, D)], TM, "ln1_bwd")
        gsmall["ln1_g"][l], gsmall["ln1_b"][l] = g_l1g, g_l1b
        gbig["w_o"][l] = row_chunks(_mm(s["merged"], do_, "tn", F32, "mm_dw_o"))
        dmerged = _mm(do_, p["w_o"], "nt", F32, "mm_dmerged")

        def fn_merge_bwd(i, rows, ps):
            bgv = ps[0]
            _, vjp = jax.vjp(merge_math, rows[0], rows[1], rows[2], rows[3], bgv[0:1], bgv[1:2], bgv[2:3])
            dpa, dpb, dpc, dg_, db0, db1, db2 = vjp(rows[4])
            return [dpa, dpb, dpc, dg_], [db0, db1, db2]
        (dpa, dpb, dpc, dgl), (db0, db1, db2) = _rowwise(
            fn_merge_bwd, [_whole(s["pa"]), _whole(s["pb"]), _whole(s["pc"]), (s["h"], N_BRANCHES * D, cb("g")), _whole(dmerged)],
            [p["bg"]], [(D, BF16), (D, BF16), (D, BF16), (N_BRANCHES * D, BF16)], [(1, D)] * 3, TMW, "merge_bwd")
        gsmall["b_gate"][l] = jnp.concatenate([db0, db1, db2], axis=0)
        gbig["w_proj_a"][l] = _mm(s["y_a"], dpa, "tn", F32, "mm_dw_pa", col_chunks=N_CHIPS)
        gbig["w_proj_b"][l] = row_chunks(_mm(s["y_b"], dpb, "tn", F32, "mm_dw_pb"))
        gbig["w_proj_c"][l] = _mm(s["y_c"], dpc, "tn", F32, "mm_dw_pc", col_chunks=N_CHIPS)
        dy_a = _mm(dpa, p["w_pa"], "nt", BF16, "mm_dy_a")
        dy_b = _mm(dpb, p["w_pb"], "nt", BF16, "mm_dy_b")
        dy_c = _mm(dpc, p["w_pc"], "nt", BF16, "mm_dy_c")
        dh_c, g_sg, g_sb, g_sw, g_sb3 = _sgu_bwd(s["h"], cb("hu"), cb("hv"), dy_c, p["sg"], p["sb"], p["sw"], p["sb3"])
        gsmall["sgu_ln_g"][l], gsmall["sgu_ln_b"][l], gsmall["sgu_w"][l] = g_sg, g_sb, g_sw
        gsmall["sgu_b"][l] = g_sb3.reshape(SGU_GROUPS, SGU_CHUNK)
        hide_early = l == 0
        early = [gbig[k][l] for k in EARLY]
        (dqa, dka, dva, g_sinks), early_from_sibling = _swa_bwd(s["h"], cb("qa"), cb("ka"), cb("va"), p["sinks"], dy_a,
                                                                carry=_swap_stage(early) if hide_early else None)
        gsmall["sinks"][l] = g_sinks
        if hide_early:
            early_partial = [_add_half(g, r, cc, f"rs_add_{k}_l{l}") for k, g, r in zip(EARLY, early, early_from_sibling)]
        delta = _mla_delta(dy_b, s["y_b"], T)
        (dqn, dqr), early_arrived = _mla_bwd_dq(s["q_full"], s["qr"], s["kv"], s["kr"], dy_b, s["lse"], delta, T,
                                                carry=_scatter_stage([p16 for _, p16 in early_partial]) if hide_early else None)
        if hide_early:
            for k, (p32, _), arr in zip(EARLY, early_partial, early_arrived):
                reduced[k] = _sum_into(p32, arr, chip, cc, l, L, reduced.get(k), f"rs_sum_{k}_l{l}")
        (dkn, dv, dkr_heads), arrived = _mla_bwd_dkv(
            s["q_full"], s["qr"], s["kv"], s["kr"], dy_b, s["lse"], delta, T,
            carry=None if pending is None else _scatter_stage([p16 for _, p16 in partial]))
        if pending is not None:
            for k, (p32, _), arr in zip(BIG, partial, arrived):
                reduced[k] = _sum_into(p32, arr, chip, cc, l + 1, L, reduced.get(k), f"rs_sum_{k}_l{l + 1}")
        dkr = _sum_heads(dkr_heads, TM)
        dqr_raw, dkr_raw = _rope_call(dqr, MLA_HEADS * LANES, 0, dkr, LANES, 0, cos_t, sin_t, -1.0, TM, "rope_bwd")
        dq_full = jnp.concatenate([dqn, dqr_raw], axis=1)
        dkv = jnp.concatenate([dkn, dv], axis=1)
        gbig["w_uq"][l] = _col_chunks(_unperm_w_uq(_mm(s["cqn"], dq_full, "tn", F32, "mm_dw_uq")))
        gbig["w_ukv"][l] = _col_chunks(_unperm_w_ukv(_mm(s["ckvn"], dkv, "tn", F32, "mm_dw_ukv")))
        dcqn = _mm(dq_full, p["w_uq"], "nt", F32, "mm_dcqn")
        dckvn = _mm(dkv, p["w_ukv"], "nt", F32, "mm_dckvn")

        def fn_rms_bwd(i, rows, ps):
            _, vjp1 = jax.vjp(lambda c_, g_: _rms_norm(c_.astype(F32), g_), rows[0], ps[0])
            _, vjp2 = jax.vjp(lambda c_, g_: _rms_norm(c_.astype(F32), g_), rows[1], ps[1])
            d1, dg1 = vjp1(rows[2])
            d2, dg2 = vjp2(rows[3])
            return [d1, d2], [dg1, dg2]
        (dcq, dckv), (g_qg, g_kvg) = _rowwise(
            fn_rms_bwd, [(s["h"], MLA_Q_RANK, cb("cq")), (s["h"], MLA_KV_RANK, cb("ckv")), _whole(dcqn), _whole(dckvn)],
            [p["qg"], p["kvg"]], [(MLA_Q_RANK, BF16), (MLA_KV_RANK, BF16)], [(1, MLA_Q_RANK), (1, MLA_KV_RANK)], TM, "mla_rms_bwd")
        gsmall["q_norm_g"][l], gsmall["kv_norm_g"][l] = g_qg, g_kvg
        tail = jnp.zeros((S, NP - lay["kr"][0] - LANES), BF16)
        dh = jnp.concatenate([dgl, dqa, dcq, dckv, dh_c, dka, dva, dkr_raw, tail], axis=1)
        gbig["w_in"][l] = _col_chunks(_unperm_w_in(_mm(s["x0b"], dh, "tn", F32, "mm_dw_in"), lay, D))
        dx = _mm(dh, p["w_in"], "nt", F32, "mm_dx0", add=dx0_res)

        pending = [gbig[k][l] for k in BIG]

    late = [gbig[k][0] for k in LATE]
    from_sibling = _run_stage(_swap_stage(late), "rs_swap_halves_l0")
    partial = [_add_half(g, r, cc, f"rs_add_{k}_l0") for k, g, r in zip(LATE, late, from_sibling)]
    arrived = _run_stage(_scatter_stage([p16 for _, p16 in partial]), "rs_scatter_chips_l0")
    for k, (p32, _), arr in zip(LATE, partial, arrived):
        reduced[k] = _sum_into(p32, arr, chip, cc, 0, L, reduced.get(k), f"rs_sum_{k}_l0")

    grad_x = dx.reshape(x.shape)
    g_big = dict(zip(BIG, _run_stage(_join_stage([reduced[k] for k in BIG]), "rs_join_halves")))

    small_shapes = {k: (small_sharded_full[k] if k in SMALL_SHARDED else tuple(W[k].shape)) for k in SMALL}
    sv = jnp.concatenate([jnp.stack(gsmall[k]).reshape(-1) for k in SMALL])
    n_sv = sv.shape[0]
    sv = jnp.pad(sv, (0, -n_sv % (8 * LANES))).reshape(-1, LANES)
    sv = _all_reduce_small(sv).reshape(-1)
    g_small, o_ = {}, 0
    for k in SMALL:
        n = int(np.prod(small_shapes[k]))
        g = sv[o_:o_ + n].reshape(small_shapes[k])
        if k in SMALL_SHARDED:
            g = lax.dynamic_slice_in_dim(g, chip * W[k].shape[-1], W[k].shape[-1], axis=-1)
        g_small[k] = g
        o_ += n

    delta, new_m, new_v = {}, {}, {}
    for k in BIG:
        shp = shards[k]
        v2 = lambda t: t.reshape(shp[0] * shp[1], shp[2])
        d_, m_, v_ = _adamw(v2(W[k]), v2(g_big[k]), v2(Mo[k]), v2(Vo[k]), "adamw_" + k)
        delta[k], new_m[k], new_v[k] = d_.reshape(shp), m_.reshape(shp), v_.reshape(shp)
    pack = lambda t: jnp.concatenate([t[k].reshape(-1) for k in SMALL])
    n_small = sum(int(np.prod(W[k].shape)) for k in SMALL)
    pad2 = lambda t: jnp.pad(t, (0, -n_small % (8 * LANES))).reshape(-1, LANES)
    d_, m_, v_ = _adamw(pad2(pack(W)), pad2(pack(g_small)), pad2(pack(Mo)), pad2(pack(Vo)), "adamw_small")
    o_ = 0
    for k in SMALL:
        n = int(np.prod(W[k].shape))
        take = lambda t: t.reshape(-1)[o_:o_ + n].reshape(W[k].shape)
        delta[k], new_m[k], new_v[k] = take(d_), take(m_), take(v_)
        o_ += n

    grads = {**g_big, **g_small}
    return (loss, grad_x, *[grads[k] for k in WEIGHTS], *[delta[k] for k in WEIGHTS], *[new_m[k] for k in WEIGHTS],
            *[new_v[k] for k in WEIGHTS])
```

```python
import functools
import math

import jax
import jax.numpy as jnp
import numpy as np
from jax import lax
from jax.experimental import pallas as pl
from jax.experimental.pallas import tpu as pltpu

F32, BF16 = jnp.float32, jnp.bfloat16
SDS = jax.ShapeDtypeStruct
MESH = pl.DeviceIdType.MESH

SWA_Q_HEADS, SWA_KV_HEADS, SWA_HEAD_DIM, SWA_BLOCK = 16, 2, 64, 128
MLA_HEADS, MLA_NOPE, MLA_ROPE, MLA_V = 16, 128, 64, 128
MLA_Q_RANK, MLA_KV_RANK = 512, 512
ROPE_THETA = 10000.0
SGU_GROUPS, SGU_GROUP_DIM, SGU_CHUNK = 8, 128, 128
SGU_WIDTH = SGU_GROUPS * SGU_GROUP_DIM
A_Q = SWA_Q_HEADS * SWA_HEAD_DIM
A_KV = SWA_KV_HEADS * SWA_HEAD_DIM
N_BRANCHES = 3
DEPTH = 2
EPS = 1e-5
MASK_VALUE = -1e30
DN_ALPHA = (2 * DEPTH) ** 0.25
ADAM_LR, ADAM_B1, ADAM_B2, ADAM_EPS, ADAM_WD, ADAM_STEP = 0.001, 0.9, 0.999, 1e-08, 0.01, 10
N_CHIPS = 4

LANES = 128
VMEM_LIMIT = 48 * 1024 * 1024

BIG = ["w_in", "w_uq", "w_ukv", "w_proj_a", "w_proj_b", "w_proj_c", "w_o", "w_up", "w_down"]
ROW_SHARDED = {"w_proj_b", "w_o", "w_down"}
LATE = ["w_in", "w_uq", "w_ukv"]
EARLY = [k for k in BIG if k not in LATE]
SMALL = ["b_gate", "sinks", "q_norm_g", "kv_norm_g", "sgu_ln_g", "sgu_ln_b", "sgu_w", "sgu_b", "ln1_g", "ln1_b",
         "conv_w", "conv_b", "ln2_g", "ln2_b"]
SMALL_SHARDED = {"b_gate", "conv_w"}
WEIGHTS = ["w_in", "b_gate", "sinks", "q_norm_g", "kv_norm_g", "w_uq", "w_ukv", "sgu_ln_g", "sgu_ln_b", "sgu_w", "sgu_b",
           "w_proj_a", "w_proj_b", "w_proj_c", "w_o", "ln1_g", "ln1_b", "w_up", "conv_w", "conv_b", "w_down", "ln2_g", "ln2_b"]


def _pcall(body, **kw):
    return pl.pallas_call(body, **kw)


def _params(sem=None):
    return pltpu.CompilerParams(dimension_semantics=sem, vmem_limit_bytes=VMEM_LIMIT)


def _tile(dim, pref, align=LANES):
    t = (min(pref, dim) // align) * align
    while t >= align:
        if dim % t == 0:
            return t
        t -= align
    return dim


def _mm(a, b, mode, out_dtype, name, add=None, tm=1024, tn=512, tk=2048, col_chunks=1, carry=None):
    if mode == "nn":
        (M, K), (K2, N) = a.shape, b.shape
    elif mode == "nt":
        (M, K), (N, K2) = a.shape, b.shape
    else:
        (K, M), (K2, N) = a.shape, b.shape
    assert K == K2, (a.shape, b.shape, mode)
    assert N % col_chunks == 0
    tm, tn, tk = _tile(M, tm), _tile(N // col_chunks, tn), _tile(K, tk)
    assert (N // col_chunks) % tn == 0
    per_chunk = (N // col_chunks) // tn
    nk = K // tk
    if mode == "tn":
        a_spec = pl.BlockSpec((tk, tm), lambda i, j, k: (k, i))
    else:
        a_spec = pl.BlockSpec((tm, tk), lambda i, j, k: (i, k))
    if mode == "nt":
        b_spec = pl.BlockSpec((tn, tk), lambda i, j, k: (j, k))
    else:
        b_spec = pl.BlockSpec((tk, tn), lambda i, j, k: (k, j))
    dn = {"nn": (((1,), (0,)), ((), ())), "nt": (((1,), (1,)), ((), ())), "tn": (((0,), (0,)), ((), ()))}[mode]
    chunked = col_chunks > 1
    if chunked:
        assert add is None
        o_spec = pl.BlockSpec((1, tm, tn), lambda i, j, k: (lax.div(j, per_chunk), i, lax.rem(j, per_chunk)))
        out_shape = SDS((col_chunks, M, N // col_chunks), out_dtype)
    else:
        o_spec = pl.BlockSpec((tm, tn), lambda i, j, k: (i, j))
        out_shape = SDS((M, N), out_dtype)
    has_add = add is not None

    def body(*refs):
        if has_add:
            a_ref, b_ref, add_ref, o_ref, acc_ref = refs
        else:
            a_ref, b_ref, o_ref, acc_ref = refs
        k = pl.program_id(2)

        @pl.when(k == 0)
        def _():
            acc_ref[...] = jnp.zeros_like(acc_ref)

        acc_ref[...] += lax.dot_general(a_ref[...].astype(BF16), b_ref[...].astype(BF16), dn,
                                        preferred_element_type=F32)

        @pl.when(k == nk - 1)
        def _():
            r = acc_ref[...]
            if has_add:
                r = r + add_ref[...].astype(F32)
            if chunked:
                o_ref[0] = r.astype(o_ref.dtype)
            else:
                o_ref[...] = r.astype(o_ref.dtype)

    ins = [a, b] + ([add] if has_add else [])
    in_specs = [a_spec, b_spec] + ([o_spec] if has_add else [])
    (out,), carried = _call(body, name, (M // tm, N // tn, nk), 0, in_specs, [o_spec], [out_shape], [pltpu.VMEM((tm, tn), F32)],
                            ins, ("parallel", "parallel", "arbitrary"), carry)
    return out if carry is None else (out, carried)


def _rowwise(fn, rows, params, row_outs, acc_outs, tm, name):
    n_rows = rows[0][0].shape[0]
    assert n_rows % tm == 0
    nr, npar, no = len(rows), len(params), len(row_outs)

    def body(*refs):
        i = pl.program_id(0)
        r, p = refs[:nr], refs[nr:nr + npar]
        o, acc = refs[nr + npar:nr + npar + no], refs[nr + npar + no:]
        outs, sums = fn(i, [x[...] for x in r], [x[...] for x in p])
        for ref, val in zip(o, outs, strict=True):
            ref[...] = val.astype(ref.dtype)
        if acc:
            @pl.when(i == 0)
            def _():
                for ref in acc:
                    ref[...] = jnp.zeros_like(ref)
            for ref, val in zip(acc, sums, strict=True):
                ref[...] += val.astype(F32)

    def full(shape):
        nd = len(shape)
        return pl.BlockSpec(tuple(shape), lambda i: (0,) * nd)

    in_specs = [pl.BlockSpec((tm, w), (lambda i, cb=cb: (i, cb))) for (_, w, cb) in rows] + [full(p.shape) for p in params]
    out_specs = [pl.BlockSpec((tm, w), lambda i: (i, 0)) for (w, _) in row_outs] + [full(s) for s in acc_outs]
    out_shape = [SDS((n_rows, w), dt) for (w, dt) in row_outs] + [SDS(tuple(s), F32) for s in acc_outs]
    res = _pcall(body, name=name, out_shape=out_shape, grid=(n_rows // tm,), in_specs=in_specs, out_specs=out_specs,
                 compiler_params=_params(("arbitrary",)))(*[r[0] for r in rows], *params)
    return list(res[:no]), list(res[no:])


def _whole(a):
    return (a, a.shape[1], 0)


def _gelu(x):
    return 0.5 * x * (1.0 + lax.erf(x * (1.0 / math.sqrt(2.0))))


def _layer_norm(x, g, b):
    mu = x.mean(-1, keepdims=True)
    var = jnp.mean(jnp.square(x - mu), -1, keepdims=True)
    return (x - mu) * lax.rsqrt(var + EPS) * g + b


def _rms_norm(x, g):
    return x * lax.rsqrt(jnp.mean(jnp.square(x), -1, keepdims=True) + EPS) * g


def _sgu_math(hu, hv, ln_g, ln_b, ws, bs):
    u = _gelu(hu.astype(F32))
    vn = _layer_norm(_gelu(hv.astype(F32)), ln_g, ln_b)
    r = lax.broadcasted_iota(jnp.int32, (SGU_CHUNK, SGU_CHUNK), 0)
    c = lax.broadcasted_iota(jnp.int32, (SGU_CHUNK, SGU_CHUNK), 1)
    outs = []
    for g in range(SGU_GROUPS):
        w = jnp.where(r >= c, ws[g], 0.0).astype(BF16)
        vg = vn[:, g * SGU_GROUP_DIM:(g + 1) * SGU_GROUP_DIM].astype(BF16)
        outs.append(jnp.dot(w, vg, preferred_element_type=F32) + bs[g])
    return u * jnp.concatenate(outs, axis=1)


def _sgu_fwd(h, cu, cv, ln_g, ln_b, w, b3):
    def fn(i, rows, ps):
        g_, b_, w_, b3_ = ps
        y = _sgu_math(rows[0], rows[1], g_, b_, [w_[g] for g in range(SGU_GROUPS)], [b3_[g] for g in range(SGU_GROUPS)])
        return [y], []
    (y,), _ = _rowwise(fn, [(h, SGU_WIDTH, cu), (h, SGU_WIDTH, cv)], [ln_g, ln_b, w, b3], [(SGU_WIDTH, BF16)], [],
                       SGU_CHUNK, "sgu_fwd")
    return y


def _sgu_bwd(h, cu, cv, dy, ln_g, ln_b, w, b3):
    nd = 2 * SGU_WIDTH

    def body(hu_ref, hv_ref, dy_ref, g_ref, b_ref, w_ref, b3_ref, dh_ref, dg_ref, db_ref, dw_ref, db3_ref):
        i = pl.program_id(0)

        @pl.when(i == 0)
        def _():
            dg_ref[...] = jnp.zeros_like(dg_ref)
            db_ref[...] = jnp.zeros_like(db_ref)
            dw_ref[...] = jnp.zeros_like(dw_ref)
            db3_ref[...] = jnp.zeros_like(db3_ref)

        ws = [w_ref[g] for g in range(SGU_GROUPS)]
        bs = [b3_ref[g] for g in range(SGU_GROUPS)]
        _, vjp = jax.vjp(_sgu_math, hu_ref[...], hv_ref[...], g_ref[...], b_ref[...], ws, bs)
        dhu, dhv, dg, db, dws, dbs = vjp(dy_ref[...].astype(F32))
        dh_ref[...] = jnp.concatenate([dhu, dhv], axis=1).astype(dh_ref.dtype)
        dg_ref[...] += dg
        db_ref[...] += db
        for g in range(SGU_GROUPS):
            dw_ref[g] += dws[g]
            db3_ref[g] += dbs[g]

    n = h.shape[0]
    blk = lambda cb: pl.BlockSpec((SGU_CHUNK, SGU_WIDTH), lambda i, cb=cb: (i, cb))
    full = lambda s: pl.BlockSpec(tuple(s), lambda i: (0,) * len(s))
    return _pcall(
        body, name="sgu_bwd", grid=(n // SGU_CHUNK,),
        out_shape=[SDS((n, nd), BF16), SDS(ln_g.shape, F32), SDS(ln_b.shape, F32), SDS(w.shape, F32), SDS(b3.shape, F32)],
        in_specs=[blk(cu), blk(cv), blk(0), full(ln_g.shape), full(ln_b.shape), full(w.shape), full(b3.shape)],
        out_specs=[pl.BlockSpec((SGU_CHUNK, nd), lambda i: (i, 0)), full(ln_g.shape), full(ln_b.shape), full(w.shape),
                   full(b3.shape)],
        compiler_params=_params(("arbitrary",)))(h, h, dy, ln_g, ln_b, w, b3)


def _swa_math(q, kp, kc, vp, vc, sinks, not_first):
    kw = jnp.concatenate([kp, kc], axis=0).astype(BF16)
    vw = jnp.concatenate([vp, vc], axis=0).astype(BF16)
    qb = q.astype(BF16)
    q_off = lax.broadcasted_iota(jnp.int32, (SWA_BLOCK, 2 * SWA_BLOCK), 0) + SWA_BLOCK
    k_off = lax.broadcasted_iota(jnp.int32, (SWA_BLOCK, 2 * SWA_BLOCK), 1)
    rel = q_off - k_off
    valid = (rel >= 0) & (rel < SWA_BLOCK) & (not_first | (k_off >= SWA_BLOCK))
    G = SWA_Q_HEADS // SWA_KV_HEADS
    outs = []
    for head in range(SWA_Q_HEADS):
        hk = head // G
        qh = qb[:, head * SWA_HEAD_DIM:(head + 1) * SWA_HEAD_DIM]
        kh = kw[:, hk * SWA_HEAD_DIM:(hk + 1) * SWA_HEAD_DIM]
        vh = vw[:, hk * SWA_HEAD_DIM:(hk + 1) * SWA_HEAD_DIM]
        s = lax.dot_general(qh, kh, (((1,), (1,)), ((), ())), preferred_element_type=F32) * (SWA_HEAD_DIM ** -0.5)
        s = jnp.where(valid, s, MASK_VALUE)
        sink = sinks[:, head:head + 1]
        m = jnp.maximum(s.max(-1, keepdims=True), sink)
        p = jnp.exp(s - m)
        p = (p / (p.sum(-1, keepdims=True) + jnp.exp(sink - m))).astype(BF16)
        outs.append(jnp.dot(p, vh, preferred_element_type=F32))
    return jnp.concatenate(outs, axis=1)


def _swa_fwd(h, cq, ck, cv, sinks, carry=None):
    n = h.shape[0]
    nb = n // SWA_BLOCK

    def body(q_ref, kp_ref, kc_ref, vp_ref, vc_ref, s_ref, o_ref):
        i = pl.program_id(0)
        f = lambda x: x[...].astype(F32)
        o_ref[...] = _swa_math(f(q_ref), f(kp_ref), f(kc_ref), f(vp_ref), f(vc_ref), s_ref[...], i > 0).astype(o_ref.dtype)

    prev = lambda cb: pl.BlockSpec((SWA_BLOCK, A_KV), lambda i, cb=cb: (jnp.maximum(i - 1, 0), cb))
    cur = lambda cb: pl.BlockSpec((SWA_BLOCK, A_KV), lambda i, cb=cb: (i, cb))
    return _call(body, "swa_fwd", (nb,), 0,
                 [pl.BlockSpec((SWA_BLOCK, A_Q), lambda i: (i, cq)), prev(ck), cur(ck), prev(cv), cur(cv),
                  pl.BlockSpec((1, SWA_Q_HEADS), lambda i: (0, 0))],
                 [pl.BlockSpec((SWA_BLOCK, A_Q), lambda i: (i, 0))], [SDS((n, A_Q), BF16)], [],
                 [h, h, h, h, h, sinks], ("arbitrary",), carry)


def _swa_bwd(h, cq, ck, cv, sinks, dy, carry=None):
    n = h.shape[0]
    nb = n // SWA_BLOCK

    def body(q_ref, kp_ref, kc_ref, vp_ref, vc_ref, s_ref, dy_ref, dq_ref, dk_ref, dv_ref, ds_ref, ck_ref, cv_ref):
        r = pl.program_id(0)
        blk = nb - 1 - r

        @pl.when(r == 0)
        def _():
            ds_ref[...] = jnp.zeros_like(ds_ref)
            ck_ref[...] = jnp.zeros_like(ck_ref)
            cv_ref[...] = jnp.zeros_like(cv_ref)

        f = lambda x: x[...].astype(F32)
        not_first = blk > 0
        _, vjp = jax.vjp(lambda q, kp, kc, vp, vc, s: _swa_math(q, kp, kc, vp, vc, s, not_first),
                         f(q_ref), f(kp_ref), f(kc_ref), f(vp_ref), f(vc_ref), s_ref[...])
        dq, dkp, dkc, dvp, dvc, dsk = vjp(f(dy_ref))
        dq_ref[...] = dq.astype(dq_ref.dtype)
        dk_ref[...] = (dkc + ck_ref[...]).astype(dk_ref.dtype)
        dv_ref[...] = (dvc + cv_ref[...]).astype(dv_ref.dtype)
        ck_ref[...] = dkp
        cv_ref[...] = dvp
        ds_ref[...] += dsk

    rev = lambda i: nb - 1 - i
    prev = lambda cb: pl.BlockSpec((SWA_BLOCK, A_KV), lambda i, cb=cb: (jnp.maximum(rev(i) - 1, 0), cb))
    cur = lambda cb: pl.BlockSpec((SWA_BLOCK, A_KV), lambda i, cb=cb: (rev(i), cb))
    return _call(
        body, "swa_bwd", (nb,), 0,
        [pl.BlockSpec((SWA_BLOCK, A_Q), lambda i: (rev(i), cq)), prev(ck), cur(ck), prev(cv), cur(cv),
         pl.BlockSpec((1, SWA_Q_HEADS), lambda i: (0, 0)), pl.BlockSpec((SWA_BLOCK, A_Q), lambda i: (rev(i), 0))],
        [pl.BlockSpec((SWA_BLOCK, A_Q), lambda i: (rev(i), 0)), pl.BlockSpec((SWA_BLOCK, A_KV), lambda i: (rev(i), 0)),
         pl.BlockSpec((SWA_BLOCK, A_KV), lambda i: (rev(i), 0)), pl.BlockSpec((1, SWA_Q_HEADS), lambda i: (0, 0))],
        [SDS((n, A_Q), BF16), SDS((n, A_KV), BF16), SDS((n, A_KV), BF16), SDS((1, SWA_Q_HEADS), F32)],
        [pltpu.VMEM((SWA_BLOCK, A_KV), F32), pltpu.VMEM((SWA_BLOCK, A_KV), F32)],
        [h, h, h, h, h, sinks, dy], ("arbitrary",), carry)


def _rope(x, cos, sin, sign):
    w = x.shape[1]
    reps = w // LANES
    ct = jnp.tile(cos, (1, reps)) if reps > 1 else cos
    st = jnp.tile(sin, (1, reps)) if reps > 1 else sin
    fwd = pltpu.roll(x, MLA_ROPE // 2, axis=1)
    bwd = pltpu.roll(x, w - MLA_ROPE // 2, axis=1)
    lane = lax.broadcasted_iota(jnp.int32, x.shape, 1) % LANES
    rot = jnp.where(lane < MLA_ROPE // 2, -bwd, fwd)
    return x * ct + sign * (rot * st)


def _rope_call(a, wa, ca, b, wb, cb, cos, sin, sign, tm, name):
    def fn(i, rows, ps):
        xa, xb, c_, s_ = rows
        return [_rope(xa.astype(F32), c_, s_, sign), _rope(xb.astype(F32), c_, s_, sign)], []
    (ra, rb), _ = _rowwise(fn, [(a, wa, ca), (b, wb, cb), _whole(cos), _whole(sin)], [], [(wa, BF16), (wb, BF16)], [], tm, name)
    return ra, rb


MLA_SCALE = (MLA_NOPE + MLA_ROPE) ** -0.5
LOG2E = math.log2(math.e)


MLA_HP = 2
MLA_W = MLA_HP * LANES


def _mla_scores(qn_ref, qr_ref, kn_ref, kr_ref, hh, masked):
    cols = slice(hh * LANES, (hh + 1) * LANES)
    q = jnp.concatenate([qn_ref[:, cols], qr_ref[:, cols]], axis=1)
    k = jnp.concatenate([kn_ref[:, cols], kr_ref[...]], axis=1)
    s = lax.dot_general(q, k, (((1,), (1,)), ((), ())), preferred_element_type=F32)
    if masked:
        row = lax.broadcasted_iota(jnp.int32, s.shape, 0)
        col = lax.broadcasted_iota(jnp.int32, s.shape, 1)
        s = jnp.where(col <= row, s, MASK_VALUE)
    return s, q, k


def _causal_pairs(nq, by_query):
    if by_query:
        pairs = [(i, j) for i in range(nq) for j in range(i + 1)]
    else:
        pairs = [(i, j) for j in range(nq) for i in range(j, nq)]
    return (jnp.asarray(np.array([p[0] for p in pairs], np.int32)), jnp.asarray(np.array([p[1] for p in pairs], np.int32)),
            len(pairs))


def _mla_fwd(q_full, qr, kv, kr, T, carry=None):
    n = q_full.shape[0]
    nq = n // T
    H = MLA_HEADS
    qi, kj, npairs = _causal_pairs(nq, True)

    def body(qi_ref, kj_ref, qn_ref, qr_ref, kn_ref, v_ref, kr_ref, y_ref, lse_ref, m_ref, l_ref, acc_ref):
        t = pl.program_id(1)
        i, j = qi_ref[t], kj_ref[t]

        @pl.when(j == 0)
        def _():
            m_ref[...] = jnp.full_like(m_ref, MASK_VALUE)
            l_ref[...] = jnp.zeros_like(l_ref)
            acc_ref[...] = jnp.zeros_like(acc_ref)

        def update(masked):
            for hh in range(MLA_HP):
                s, _, _ = _mla_scores(qn_ref, qr_ref, kn_ref, kr_ref, hh, masked)
                m_prev = m_ref[hh]
                m_new = jnp.maximum(m_prev, s.max(-1, keepdims=True))
                p = jnp.exp2((s - m_new[:, :1]) * (MLA_SCALE * LOG2E))
                alpha = jnp.exp2((m_prev - m_new) * (MLA_SCALE * LOG2E))
                l_ref[hh] = alpha * l_ref[hh] + p.sum(-1, keepdims=True)
                acc_ref[hh] = alpha * acc_ref[hh] + jnp.dot(p.astype(BF16), v_ref[:, hh * LANES:(hh + 1) * LANES],
                                                            preferred_element_type=F32)
                m_ref[hh] = m_new

        @pl.when(j < i)
        def _():
            update(False)

        @pl.when(j == i)
        def _():
            update(True)
            for hh in range(MLA_HP):
                y_ref[:, hh * LANES:(hh + 1) * LANES] = (acc_ref[hh] / l_ref[hh]).astype(y_ref.dtype)
                lse_ref[hh] = m_ref[hh] * (MLA_SCALE * LOG2E) + jnp.log2(l_ref[hh])

    G = H // MLA_HP
    qspec = lambda off: pl.BlockSpec((T, MLA_W), lambda h, t, qi, kj, off=off: (qi[t], off + h))
    kspec = lambda off: pl.BlockSpec((T, MLA_W), lambda h, t, qi, kj, off=off: (kj[t], off + h))
    return _call(
        body, "mla_fwd", (G, npairs), 2,
        [qspec(0), qspec(0), kspec(0), kspec(G), pl.BlockSpec((T, LANES), lambda h, t, qi, kj: (kj[t], 0))],
        [pl.BlockSpec((T, MLA_W), lambda h, t, qi, kj: (qi[t], h)),
         pl.BlockSpec((MLA_HP, T, LANES), lambda h, t, qi, kj: (h, qi[t], 0))],
        [SDS((n, H * MLA_V), BF16), SDS((H, n, LANES), F32)], [pltpu.VMEM((MLA_HP, T, LANES), F32)] * 3,
        [qi, kj, q_full, qr, kv, kv, kr], ("parallel", "arbitrary"), carry)


def _mla_delta(dy, y, T):
    n = y.shape[0]
    H = MLA_HEADS

    def body(dy_ref, y_ref, d_ref):
        d = jnp.sum(dy_ref[...].astype(F32) * y_ref[...].astype(F32), axis=-1, keepdims=True)
        d_ref[0] = jnp.broadcast_to(d, (T, LANES))

    spec = pl.BlockSpec((T, LANES), lambda h, i: (i, h))
    return _pcall(body, name="mla_delta", grid=(H, n // T), out_shape=SDS((H, n, LANES), F32), in_specs=[spec, spec],
                  out_specs=pl.BlockSpec((1, T, LANES), lambda h, i: (h, i, 0)),
                  compiler_params=_params(("parallel", "parallel")))(dy, y)


def _mla_bwd_dq(q_full, qr, kv, kr, dy, lse, delta, T, carry=None):
    n = q_full.shape[0]
    nq = n // T
    H = MLA_HEADS

    qi, kj, npairs = _causal_pairs(nq, True)

    def body(qi_ref, kj_ref, qn_ref, qr_ref, kn_ref, v_ref, kr_ref, dy_ref, lse_ref, dl_ref, dqn_ref, dqr_ref, acc_ref):
        t = pl.program_id(1)
        i, j = qi_ref[t], kj_ref[t]

        @pl.when(j == 0)
        def _():
            acc_ref[...] = jnp.zeros_like(acc_ref)

        def update(masked):
            for hh in range(MLA_HP):
                cols = slice(hh * LANES, (hh + 1) * LANES)
                s, _, k = _mla_scores(qn_ref, qr_ref, kn_ref, kr_ref, hh, masked)
                p = jnp.exp2(s * (MLA_SCALE * LOG2E) - lse_ref[hh][:, :1])
                dp = lax.dot_general(dy_ref[:, cols], v_ref[:, cols], (((1,), (1,)), ((), ())), preferred_element_type=F32)
                ds = p * (dp - dl_ref[hh][:, :1])
                acc_ref[hh] += jnp.dot(ds.astype(BF16), k, preferred_element_type=F32)

        @pl.when(j < i)
        def _():
            update(False)

        @pl.when(j == i)
        def _():
            update(True)
            for hh in range(MLA_HP):
                cols = slice(hh * LANES, (hh + 1) * LANES)
                dqn_ref[:, cols] = (acc_ref[hh][:, :LANES] * MLA_SCALE).astype(dqn_ref.dtype)
                dqr_ref[:, cols] = (acc_ref[hh][:, LANES:] * MLA_SCALE).astype(dqr_ref.dtype)

    G = H // MLA_HP
    qspec = lambda off: pl.BlockSpec((T, MLA_W), lambda h, t, qi, kj, off=off: (qi[t], off + h))
    kspec = lambda off: pl.BlockSpec((T, MLA_W), lambda h, t, qi, kj, off=off: (kj[t], off + h))
    stat = pl.BlockSpec((MLA_HP, T, LANES), lambda h, t, qi, kj: (h, qi[t], 0))
    out = pl.BlockSpec((T, MLA_W), lambda h, t, qi, kj: (qi[t], h))
    return _call(
        body, "mla_bwd_dq", (G, npairs), 2,
        [qspec(0), qspec(0), kspec(0), kspec(G), pl.BlockSpec((T, LANES), lambda h, t, qi, kj: (kj[t], 0)), qspec(0), stat, stat],
        [out, out], [SDS((n, H * LANES), BF16), SDS((n, H * LANES), BF16)], [pltpu.VMEM((MLA_HP, T, 2 * LANES), F32)],
        [qi, kj, q_full, qr, kv, kv, kr, dy, lse, delta], ("parallel", "arbitrary"), carry)


def _mla_bwd_dkv(q_full, qr, kv, kr, dy, lse, delta, T, carry=None):
    n = q_full.shape[0]
    nq = n // T
    H = MLA_HEADS

    qi, kj, npairs = _causal_pairs(nq, False)

    def body(qi_ref, kj_ref, qn_ref, qr_ref, kn_ref, v_ref, kr_ref, dy_ref, lse_ref, dl_ref, dkn_ref, dv_ref, dkr_ref,
             dk_acc, dv_acc):
        t = pl.program_id(1)
        i, j = qi_ref[t], kj_ref[t]

        @pl.when(i == j)
        def _():
            dk_acc[...] = jnp.zeros_like(dk_acc)
            dv_acc[...] = jnp.zeros_like(dv_acc)

        def update(masked):
            for hh in range(MLA_HP):
                cols = slice(hh * LANES, (hh + 1) * LANES)
                s, q, _ = _mla_scores(qn_ref, qr_ref, kn_ref, kr_ref, hh, masked)
                p = jnp.exp2(s * (MLA_SCALE * LOG2E) - lse_ref[hh][:, :1])
                dy = dy_ref[:, cols]
                dv_acc[hh] += lax.dot_general(p.astype(BF16), dy, (((0,), (0,)), ((), ())), preferred_element_type=F32)
                dp = lax.dot_general(dy, v_ref[:, cols], (((1,), (1,)), ((), ())), preferred_element_type=F32)
                ds = p * (dp - dl_ref[hh][:, :1])
                dk_acc[hh] += lax.dot_general(ds.astype(BF16), q, (((0,), (0,)), ((), ())), preferred_element_type=F32)

        @pl.when(i == j)
        def _():
            update(True)

        @pl.when(i > j)
        def _():
            update(False)

        @pl.when(i == nq - 1)
        def _():
            for hh in range(MLA_HP):
                cols = slice(hh * LANES, (hh + 1) * LANES)
                dkn_ref[:, cols] = (dk_acc[hh][:, :LANES] * MLA_SCALE).astype(dkn_ref.dtype)
                dv_ref[:, cols] = dv_acc[hh].astype(dv_ref.dtype)
                dkr_ref[hh] = dk_acc[hh][:, LANES:] * MLA_SCALE

    G = H // MLA_HP
    qspec = lambda off: pl.BlockSpec((T, MLA_W), lambda h, t, qi, kj, off=off: (qi[t], off + h))
    kspec = lambda off: pl.BlockSpec((T, MLA_W), lambda h, t, qi, kj, off=off: (kj[t], off + h))
    stat = pl.BlockSpec((MLA_HP, T, LANES), lambda h, t, qi, kj: (h, qi[t], 0))
    out = pl.BlockSpec((T, MLA_W), lambda h, t, qi, kj: (kj[t], h))
    return _call(
        body, "mla_bwd_dkv", (G, npairs), 2,
        [qspec(0), qspec(0), kspec(0), kspec(G), pl.BlockSpec((T, LANES), lambda h, t, qi, kj: (kj[t], 0)), qspec(0), stat, stat],
        [out, out, pl.BlockSpec((MLA_HP, T, LANES), lambda h, t, qi, kj: (h, kj[t], 0))],
        [SDS((n, H * LANES), BF16), SDS((n, H * LANES), BF16), SDS((H, n, LANES), F32)],
        [pltpu.VMEM((MLA_HP, T, 2 * LANES), F32), pltpu.VMEM((MLA_HP, T, LANES), F32)],
        [qi, kj, q_full, qr, kv, kv, kr, dy, lse, delta], ("parallel", "arbitrary"), carry)


def _sum_heads(a, tm):
    H, n, _ = a.shape

    def body(a_ref, o_ref):
        o_ref[...] = jnp.sum(a_ref[...], axis=0)

    return _pcall(body, name="mla_sum_heads", grid=(n // tm,), out_shape=SDS((n, LANES), F32),
                  in_specs=[pl.BlockSpec((H, tm, LANES), lambda i: (0, i, 0))],
                  out_specs=pl.BlockSpec((tm, LANES), lambda i: (i, 0)), compiler_params=_params(("parallel",)))(a)


def _shift_down(x, k):
    row = lax.broadcasted_iota(jnp.int32, x.shape, 0)
    return jnp.where(row >= k, pltpu.roll(x, k, axis=0), 0.0)


def _shift_up(x, k):
    n = x.shape[0]
    row = lax.broadcasted_iota(jnp.int32, x.shape, 0)
    return jnp.where(row < n - k, pltpu.roll(x, n - k, axis=0), 0.0)


def _conv_fwd(up, w, b):
    n, c = up.shape

    def body(u_ref, w_ref, b_ref, o_ref):
        u = u_ref[...].astype(F32)
        wv = w_ref[...]
        o_ref[...] = (b_ref[...] + wv[0:1] * _shift_down(u, 2) + wv[1:2] * _shift_down(u, 1) + wv[2:3] * u).astype(o_ref.dtype)

    return _pcall(body, name="conv_fwd", grid=(c // LANES,), out_shape=SDS((n, c), BF16),
                  in_specs=[pl.BlockSpec((n, LANES), lambda j: (0, j)), pl.BlockSpec((3, LANES), lambda j: (0, j)),
                            pl.BlockSpec((1, LANES), lambda j: (0, j))],
                  out_specs=pl.BlockSpec((n, LANES), lambda j: (0, j)), compiler_params=_params(("parallel",)))(up, w, b)


def _conv_bwd(up, dc, w):
    n, c = up.shape

    def body(u_ref, d_ref, w_ref, du_ref, dw_ref, db_ref):
        u = u_ref[...].astype(F32)
        d = d_ref[...].astype(F32)
        wv = w_ref[...]
        du_ref[...] = (wv[2:3] * d + wv[1:2] * _shift_up(d, 1) + wv[0:1] * _shift_up(d, 2)).astype(du_ref.dtype)
        dw_ref[0:1, :] = jnp.sum(d * _shift_down(u, 2), axis=0, keepdims=True)
        dw_ref[1:2, :] = jnp.sum(d * _shift_down(u, 1), axis=0, keepdims=True)
        dw_ref[2:3, :] = jnp.sum(d * u, axis=0, keepdims=True)
        db_ref[...] = jnp.sum(d, axis=0, keepdims=True)

    col = pl.BlockSpec((n, LANES), lambda j: (0, j))
    return _pcall(body, name="conv_bwd", grid=(c // LANES,),
                  out_shape=[SDS((n, c), BF16), SDS((3, c), F32), SDS((1, c), F32)],
                  in_specs=[col, col, pl.BlockSpec((3, LANES), lambda j: (0, j))],
                  out_specs=[col, pl.BlockSpec((3, LANES), lambda j: (0, j)), pl.BlockSpec((1, LANES), lambda j: (0, j))],
                  compiler_params=_params(("parallel",)))(up, dc, w)


def _adamw_math(w, g, m, v):
    m = ADAM_B1 * m + (1.0 - ADAM_B1) * g
    v = ADAM_B2 * v + (1.0 - ADAM_B2) * jnp.square(g)
    m_hat = m / (1.0 - ADAM_B1 ** ADAM_STEP)
    v_hat = v / (1.0 - ADAM_B2 ** ADAM_STEP)
    delta = -ADAM_LR * (m_hat / (jnp.sqrt(v_hat) + ADAM_EPS) + ADAM_WD * w)
    return delta, m, v


def _adamw(w, g, m, v, name):
    r, c = w.shape
    tr = r
    budget = max(8, (1 << 20) // (4 * c))
    t = (min(budget, r) // 8) * 8
    while t >= 8:
        if r % t == 0:
            tr = t
            break
        t -= 8

    def fn(i, rows, ps):
        return list(_adamw_math(*rows)), []
    (d, nm, nv), _ = _rowwise(fn, [_whole(w), _whole(g), _whole(m), _whole(v)], [], [(c, F32)] * 3, [], tr, name)
    return d, nm, nv


def _coords():
    return lax.axis_index("x"), lax.axis_index("y"), lax.axis_index("c")


def _other_chips(x, y):
    return [(1 - x, y), (x, 1 - y), (1 - x, 1 - y)]


HBM_SPEC = pl.BlockSpec(memory_space=pltpu.HBM)


def _half(c, rows):
    return pl.ds(pl.multiple_of(c * rows, 16), rows)


def _rows_tile(rows, cols, budget_bytes=2 << 20, align=16):
    t = (min(max(align, budget_bytes // (4 * cols)), rows) // align) * align
    while t >= align:
        if rows % t == 0:
            return t
        t -= align
    return rows


def _scalar(v):
    return jnp.reshape(jnp.asarray(v, jnp.int32), (1,))


def _cast_into_slot(w3, layer, slot, name):
    _, R, C = w3.shape
    tr = _rows_tile(R, C)

    def body(s_ref, w_ref, o_ref):
        o_ref[0] = w_ref[0].astype(BF16)

    gs = pltpu.PrefetchScalarGridSpec(
        num_scalar_prefetch=1, grid=(R // tr,),
        in_specs=[pl.BlockSpec((1, tr, C), lambda i, s: (layer, i, 0))],
        out_specs=pl.BlockSpec((1, tr, C), lambda i, s: (s[0], i, 0)))
    return _pcall(body, name=name, grid_spec=gs, out_shape=SDS((N_CHIPS, R, C), BF16),
                  compiler_params=_params(("arbitrary",)))(_scalar(slot), w3)


class _Stage:
    def __init__(self, ins, out_shapes, aliases, n_sems, copies):
        self.ins, self.out_shapes, self.aliases, self.n_sems, self.copies = list(ins), out_shapes, aliases, n_sems, copies

    def start(self, ins, outs, send_sems, recv_sems):
        for cp in self.copies(ins, outs, send_sems, recv_sems)[0]:
            cp.start()

    def finish(self, ins, outs, send_sems, recv_sems):
        sends, arrivals = self.copies(ins, outs, send_sems, recv_sems)
        for cp in arrivals:
            cp.wait_recv()
        for cp in sends:
            cp.wait_send()


def _remote(src, dst, send_sems, recv_sems, k, to):
    return pltpu.make_async_remote_copy(src_ref=src, dst_ref=dst, send_sem=send_sems.at[k], recv_sem=recv_sems.at[k],
                                        device_id=to, device_id_type=MESH)


def _gather_stages(bufs):
    n = len(bufs)
    shapes = [SDS(b.shape, b.dtype) for b in bufs]
    same = {i: i for i in range(n)}

    def over_ici(ins, outs, send_sems, recv_sems):
        x, y, c = _coords()
        blk = lambda w, chip: outs[w].at[chip, _half(c, outs[w].shape[1] // 2), :]
        sends, arrivals = [], []
        for w in range(n):
            for j, (px, py) in enumerate(_other_chips(x, y)):
                sends.append(_remote(blk(w, 2 * x + y), blk(w, 2 * x + y), send_sems, recv_sems, 3 * w + j, (px, py, c)))
                arrivals.append(_remote(blk(w, 2 * px + py), blk(w, 2 * px + py), send_sems, recv_sems, 3 * w + j, (px, py, c)))
        return sends, arrivals

    def to_sibling(ins, outs, send_sems, recv_sems):
        x, y, c = _coords()
        blk = lambda w, chip, half: outs[w].at[chip, _half(half, outs[w].shape[1] // 2), :]
        sends, arrivals = [], []
        for w in range(n):
            for j, (px, py) in enumerate(_other_chips(x, y)):
                k = 2 * px + py
                sends.append(_remote(blk(w, k, c), blk(w, k, c), send_sems, recv_sems, 3 * w + j, (x, y, 1 - c)))
                arrivals.append(_remote(blk(w, k, 1 - c), blk(w, k, 1 - c), send_sems, recv_sems, 3 * w + j, (x, y, 1 - c)))
        return sends, arrivals

    return (lambda b: _Stage(b, shapes, same, 3 * n, over_ici)), (lambda b: _Stage(b, shapes, same, 3 * n, to_sibling))


def _swap_stage(gs_):
    n = len(gs_)

    def copies(ins, outs, send_sems, recv_sems):
        x, y, c = _coords()
        cps = [_remote(ins[w].at[:, _half(1 - c, ins[w].shape[1] // 2), :], outs[w], send_sems, recv_sems, w, (x, y, 1 - c))
               for w in range(n)]
        return cps, cps

    return _Stage(gs_, [SDS((N_CHIPS, g.shape[1] // 2, g.shape[2]), g.dtype) for g in gs_], {}, n, copies)


def _scatter_stage(ps):
    n = len(ps)

    def copies(ins, outs, send_sems, recv_sems):
        x, y, c = _coords()
        cps = [_remote(ins[w].at[2 * px + py], outs[w].at[j], send_sems, recv_sems, 3 * w + j, (px, py, c))
               for w in range(n) for j, (px, py) in enumerate(_other_chips(x, y))]
        return cps, cps

    return _Stage(ps, [SDS((3,) + p.shape[1:], p.dtype) for p in ps], {}, 3 * n, copies)


def _join_stage(bufs):
    n = len(bufs)
    L = bufs[0].shape[0]

    def copies(ins, outs, send_sems, recv_sems):
        x, y, c = _coords()
        blk = lambda w, l, half: outs[w].at[l, _half(half, outs[w].shape[1] // 2), :]
        sends = [_remote(blk(w, l, c), blk(w, l, c), send_sems, recv_sems, L * w + l, (x, y, 1 - c))
                 for w in range(n) for l in range(L)]
        arrivals = [_remote(blk(w, l, 1 - c), blk(w, l, 1 - c), send_sems, recv_sems, L * w + l, (x, y, 1 - c))
                    for w in range(n) for l in range(L)]
        return sends, arrivals

    return _Stage(bufs, [SDS(b.shape, b.dtype) for b in bufs], {i: i for i in range(n)}, L * n, copies)


def _stage_scratch(stage):
    return [pltpu.SemaphoreType.DMA((stage.n_sems,)), pltpu.SemaphoreType.DMA((stage.n_sems,))]


def _run_stage(stage, name):
    n_in, n_out = len(stage.ins), len(stage.out_shapes)

    def body(*refs):
        ins, outs, send_sems, recv_sems = refs[:n_in], refs[n_in:n_in + n_out], refs[n_in + n_out], refs[n_in + n_out + 1]
        stage.start(ins, outs, send_sems, recv_sems)
        stage.finish(ins, outs, send_sems, recv_sems)

    return _pcall(body, name=name, out_shape=stage.out_shapes, in_specs=[HBM_SPEC] * n_in, out_specs=[HBM_SPEC] * n_out,
                  input_output_aliases=stage.aliases, scratch_shapes=_stage_scratch(stage))(*stage.ins)


def _call(body, name, grid, n_prefetch, in_specs, out_specs, out_shape, scratch, operands, semantics, carry=None):
    n_in, n_out, n_sc = len(in_specs), len(out_specs), len(scratch)
    if carry is None:
        gs = pltpu.PrefetchScalarGridSpec(num_scalar_prefetch=n_prefetch, grid=grid, in_specs=in_specs, out_specs=out_specs,
                                          scratch_shapes=scratch)
        res = _pcall(body, name=name, grid_spec=gs, out_shape=out_shape, compiler_params=_params(semantics))(*operands)
        return list(res), []
    s_in, s_out = len(carry.ins), len(carry.out_shapes)

    def carrying(*refs):
        o = n_prefetch
        pre, ins = refs[:o], refs[o:o + n_in]
        o += n_in
        sins = refs[o:o + s_in]
        o += s_in
        outs = refs[o:o + n_out]
        o += n_out
        souts = refs[o:o + s_out]
        o += s_out
        sc, send_sems, recv_sems = refs[o:o + n_sc], refs[o + n_sc], refs[o + n_sc + 1]
        first = functools.reduce(jnp.logical_and, [pl.program_id(a) == 0 for a in range(len(grid))])
        last = functools.reduce(jnp.logical_and, [pl.program_id(a) == g - 1 for a, g in enumerate(grid)])

        @pl.when(first)
        def _():
            carry.start(sins, souts, send_sems, recv_sems)

        body(*pre, *ins, *outs, *sc)

        @pl.when(last)
        def _():
            carry.finish(sins, souts, send_sems, recv_sems)

    gs = pltpu.PrefetchScalarGridSpec(
        num_scalar_prefetch=n_prefetch, grid=grid, in_specs=list(in_specs) + [HBM_SPEC] * s_in,
        out_specs=list(out_specs) + [HBM_SPEC] * s_out, scratch_shapes=list(scratch) + _stage_scratch(carry))
    aliases = {n_prefetch + n_in + a: n_out + b for a, b in carry.aliases.items()}
    res = _pcall(carrying, name=name, grid_spec=gs, out_shape=list(out_shape) + list(carry.out_shapes),
                 input_output_aliases=aliases, compiler_params=_params(("arbitrary",) * len(grid)))(*operands, *carry.ins)
    return list(res[:n_out]), list(res[n_out:])


def _all_reduce_small(v):
    n = v.shape[0]

    def body(v_ref, out_ref, slots, send_sems, recv_sems):
        x, y, c = _coords()
        me = 4 * x + 2 * y + c
        cps = []
        for r in range(1, 8):
            t = (me + r) % 8
            cp = pltpu.make_async_remote_copy(src_ref=v_ref, dst_ref=slots.at[me], send_sem=send_sems.at[r - 1],
                                              recv_sem=recv_sems.at[me], device_id=(t // 4, (t // 2) % 2, t % 2),
                                              device_id_type=MESH)
            cp.start()
            cps.append(cp)
        slots[me] = v_ref[...]
        for r in range(1, 8):
            s = (me + r) % 8
            pltpu.make_async_remote_copy(src_ref=v_ref, dst_ref=slots.at[s], send_sem=send_sems.at[r - 1],
                                         recv_sem=recv_sems.at[s], device_id=(x, y, c), device_id_type=MESH).wait_recv()
        for cp in cps:
            cp.wait_send()
        acc = slots[0]
        for d in range(1, 8):
            acc = acc + slots[d]
        out_ref[...] = acc

    return _pcall(body, name="all_reduce_small", out_shape=SDS((n, LANES), F32),
                  in_specs=[pl.BlockSpec(memory_space=pltpu.VMEM)], out_specs=pl.BlockSpec(memory_space=pltpu.VMEM),
                  scratch_shapes=[pltpu.VMEM((8, n, LANES), F32), pltpu.SemaphoreType.DMA((7,)), pltpu.SemaphoreType.DMA((8,))],
                  compiler_params=pltpu.CompilerParams(vmem_limit_bytes=VMEM_LIMIT))(v)


def _add_half(g, recv, c, name):
    _, R, C = g.shape
    rows = R // 2
    tr = _rows_tile(rows, C, 1 << 20)
    nb = rows // tr

    def body(s_ref, g_ref, r_ref, o32_ref, o16_ref):
        s = g_ref[...] + r_ref[...]
        o32_ref[...] = s
        o16_ref[...] = s.astype(BF16)

    blk = lambda k, i, s: (k, i, 0)
    gs = pltpu.PrefetchScalarGridSpec(
        num_scalar_prefetch=1, grid=(N_CHIPS, nb),
        in_specs=[pl.BlockSpec((1, tr, C), lambda k, i, s: (k, s[0] * nb + i, 0)), pl.BlockSpec((1, tr, C), blk)],
        out_specs=[pl.BlockSpec((1, tr, C), blk), pl.BlockSpec((1, tr, C), blk)])
    return _pcall(body, name=name, grid_spec=gs, out_shape=[SDS((N_CHIPS, rows, C), F32), SDS((N_CHIPS, rows, C), BF16)],
                  compiler_params=_params(("arbitrary", "arbitrary")))(_scalar(c), g, recv)


def _sum_into(p32, arrived, chip, c, layer, n_layers, prev, name):
    _, rows, C = p32.shape
    tr = _rows_tile(rows, C, 1 << 20)
    nb = rows // tr

    def body(chip_ref, c_ref, p_ref, a_ref, *rest):
        o_ref = rest[-1]
        o_ref[0] = ((p_ref[0] + a_ref[0].astype(F32)) + a_ref[1].astype(F32)) + a_ref[2].astype(F32)

    in_specs = [pl.BlockSpec((1, tr, C), lambda i, chip_ref, c_ref: (chip_ref[0], i, 0)),
                pl.BlockSpec((3, tr, C), lambda i, chip_ref, c_ref: (0, i, 0))]
    ins = [p32, arrived]
    aliases = {}
    if prev is not None:
        in_specs.append(pl.BlockSpec(memory_space=pl.ANY))
        ins.append(prev)
        aliases = {4: 0}
    gs = pltpu.PrefetchScalarGridSpec(
        num_scalar_prefetch=2, grid=(nb,), in_specs=in_specs,
        out_specs=pl.BlockSpec((1, tr, C), lambda i, chip_ref, c_ref: (layer, c_ref[0] * nb + i, 0)))
    return _pcall(body, name=name, grid_spec=gs, out_shape=SDS((n_layers, 2 * rows, C), F32), input_output_aliases=aliases,
                  compiler_params=_params(("arbitrary",)))(_scalar(chip), _scalar(c), *ins)


def _h_layout(D):
    G = N_BRANCHES * D
    off, o = {}, 0
    for name, w in [("g", G), ("qa", A_Q), ("cq", MLA_Q_RANK), ("ckv", MLA_KV_RANK), ("hu", SGU_WIDTH), ("hv", SGU_WIDTH),
                    ("ka", A_KV), ("va", A_KV), ("kr", LANES)]:
        assert o % w == 0, (name, o, w)
        off[name] = (o, w)
        o += w
    off["total"] = -(-o // 512) * 512
    return off


def _perm_w_in(w, lay):
    s = np.cumsum([0, A_Q, A_KV, A_KV, MLA_Q_RANK, MLA_KV_RANK, MLA_ROPE, SGU_WIDTH, SGU_WIDTH])
    qa, ka, va, cq, ckv, kr, hu, hv = [w[:, s[i]:s[i + 1]] for i in range(8)]
    g = w[:, s[8]:]
    pad = jnp.zeros((w.shape[0], lay["total"] - lay["kr"][0] - MLA_ROPE), w.dtype)
    return jnp.concatenate([g, qa, cq, ckv, hu, hv, ka, va, kr, pad], axis=1)


def _unperm_w_in(wp, lay, D):
    take = lambda n, width=None: wp[:, lay[n][0]:lay[n][0] + (width or lay[n][1])]
    return jnp.concatenate([take("qa"), take("ka"), take("va"), take("cq"), take("ckv"), take("kr", MLA_ROPE), take("hu"),
                            take("hv"), take("g")], axis=1)


def _perm_w_uq(w):
    r = w.shape[0]
    w3 = w.reshape(r, MLA_HEADS, MLA_NOPE + MLA_ROPE)
    nope = w3[:, :, :MLA_NOPE].reshape(r, MLA_HEADS * MLA_NOPE)
    rope = jnp.pad(w3[:, :, MLA_NOPE:], ((0, 0), (0, 0), (0, LANES - MLA_ROPE))).reshape(r, MLA_HEADS * LANES)
    return jnp.concatenate([nope, rope], axis=1)


def _unperm_w_uq(wp):
    r = wp.shape[0]
    nope = wp[:, :MLA_HEADS * MLA_NOPE].reshape(r, MLA_HEADS, MLA_NOPE)
    rope = wp[:, MLA_HEADS * MLA_NOPE:].reshape(r, MLA_HEADS, LANES)[:, :, :MLA_ROPE]
    return jnp.concatenate([nope, rope], axis=2).reshape(r, MLA_HEADS * (MLA_NOPE + MLA_ROPE))


def _perm_w_ukv(w):
    r = w.shape[0]
    w3 = w.reshape(r, MLA_HEADS, MLA_NOPE + MLA_V)
    return jnp.concatenate([w3[:, :, :MLA_NOPE].reshape(r, -1), w3[:, :, MLA_NOPE:].reshape(r, -1)], axis=1)


def _unperm_w_ukv(wp):
    r = wp.shape[0]
    k = wp[:, :MLA_HEADS * MLA_NOPE].reshape(r, MLA_HEADS, MLA_NOPE)
    v = wp[:, MLA_HEADS * MLA_NOPE:].reshape(r, MLA_HEADS, MLA_V)
    return jnp.concatenate([k, v], axis=2).reshape(r, -1)


def _col_chunks(g):
    r, c4 = g.shape
    return jnp.transpose(g.reshape(r, N_CHIPS, c4 // N_CHIPS), (1, 0, 2))


def kernel(x, positions, w_in, b_gate, sinks, q_norm_g, kv_norm_g, w_uq, w_ukv, sgu_ln_g, sgu_ln_b, sgu_w, sgu_b, w_proj_a, w_proj_b, w_proj_c, w_o, ln1_g, ln1_b, w_up, conv_w, conv_b, w_down, ln2_g, ln2_b, loss_target, m_w_in, m_b_gate, m_sinks, m_q_norm_g, m_kv_norm_g, m_w_uq, m_w_ukv, m_sgu_ln_g, m_sgu_ln_b, m_sgu_w, m_sgu_b, m_w_proj_a, m_w_proj_b, m_w_proj_c, m_w_o, m_ln1_g, m_ln1_b, m_w_up, m_conv_w, m_conv_b, m_w_down, m_ln2_g, m_ln2_b, v_w_in, v_b_gate, v_sinks, v_q_norm_g, v_kv_norm_g, v_w_uq, v_w_ukv, v_sgu_ln_g, v_sgu_ln_b, v_sgu_w, v_sgu_b, v_w_proj_a, v_w_proj_b, v_w_proj_c, v_w_o, v_ln1_g, v_ln1_b, v_w_up, v_conv_w, v_conv_b, v_w_down, v_ln2_g, v_ln2_b):
    a = locals()
    W = {k: a[k] for k in WEIGHTS}
    Mo = {k: a["m_" + k] for k in WEIGHTS}
    Vo = {k: a["v_" + k] for k in WEIGHTS}
    S, D = x.shape[1], x.shape[2]
    FF2 = w_up.shape[2] * N_CHIPS
    FF = FF2 // 2
    L = DEPTH
    lay = _h_layout(D)
    NP = lay["total"]
    cx, cy, cc = _coords()
    chip = 2 * cx + cy
    T = _tile(S, 512)
    TM = _tile(S, 256, 16)
    TMW = _tile(S, 64, 16)

    shards = {k: tuple(W[k].shape) for k in BIG}
    full = {k: [None] * L for k in BIG}
    own = {(l, k): _cast_into_slot(W[k], l, chip, f"cast_{k}_l{l}") for l in range(L) for k in BIG}

    def over_ici(pairs):
        return _gather_stages([own[p_] for p_ in pairs])[0]([own[p_] for p_ in pairs]) if pairs else None

    def landed(pairs, arrived, tag):
        if not pairs:
            return
        gathered = _run_stage(_gather_stages(arrived)[1](arrived), f"gather_sibling_{tag}")
        for (l_, k), g in zip(pairs, gathered):
            _, r, c_ = shards[k]
            full[k][l_] = g.reshape(N_CHIPS * r, c_) if k in ROW_SHARDED else jnp.transpose(g, (1, 0, 2)).reshape(r, N_CHIPS * c_)

    MIX = ["w_uq", "w_ukv", "w_proj_a", "w_proj_b", "w_proj_c", "w_o"]
    nxt = lambda l, names: [(l + 1, k) for k in names] if l + 1 < L else []
    first = [(0, "w_in")]
    landed(first, _run_stage(over_ici(first), "gather_ici_w_in_l0"), "w_in_l0")

    small_sharded_full = {k: tuple(W[k].shape[:-1]) + (W[k].shape[-1] * N_CHIPS,) for k in SMALL_SHARDED}
    placed = []
    for k in ("b_gate", "conv_w"):
        z = jnp.zeros(small_sharded_full[k], F32)
        z = lax.dynamic_update_slice_in_dim(z, W[k], chip * W[k].shape[-1], axis=-1)
        placed.append(jnp.where(cc == 0, z, 0.0).reshape(-1))
    pv = jnp.concatenate(placed)
    n_pv = pv.shape[0]
    pv = jnp.pad(pv, (0, -n_pv % (8 * LANES))).reshape(-1, LANES)
    pv = _all_reduce_small(pv).reshape(-1)
    nb_ = int(np.prod(small_sharded_full["b_gate"]))
    b_gate_full = pv[:nb_].reshape(small_sharded_full["b_gate"])
    conv_w_full = pv[nb_:n_pv].reshape(small_sharded_full["conv_w"])

    inv_freq = ROPE_THETA ** (-jnp.arange(0, MLA_ROPE, 2, dtype=F32) / MLA_ROPE)
    ang = positions[0].astype(F32)[:, None] * inv_freq
    cos, sin = jnp.cos(ang), jnp.sin(ang)
    cos_t = jnp.concatenate([cos, cos, jnp.ones((S, LANES - MLA_ROPE), F32)], axis=1)
    sin_t = jnp.concatenate([sin, sin, jnp.zeros((S, LANES - MLA_ROPE), F32)], axis=1)

    row = lambda v: v.reshape(1, -1)
    cb = lambda name: lay[name][0] // lay[name][1]

    xs = x[0]
    saved = []
    for l in range(L):
        p = dict(
            w_in=_perm_w_in(full["w_in"][l], lay),
            sinks=row(sinks[l]), qg=row(q_norm_g[l]), kvg=row(kv_norm_g[l]), sg=row(sgu_ln_g[l]), sb=row(sgu_ln_b[l]),
            sw=sgu_w[l], sb3=sgu_b[l].reshape(SGU_GROUPS, SGU_CHUNK, 1),
            bg=b_gate_full[l], l1g=row(ln1_g[l]), l1b=row(ln1_b[l]), cw=conv_w_full[l], cbias=row(conv_b[l]),
            l2g=row(ln2_g[l]), l2b=row(ln2_b[l]))
        if l == 0:
            def fn_cast(i, rows, ps):
                return [rows[0]], []
            (xb,), _ = _rowwise(fn_cast, [_whole(xs)], [], [(D, BF16)], [], TM, "cast_x")
        own_mix = [(l, k) for k in MIX] if l == 0 else []
        own_up = [(l, "w_up")] if l == 0 else []
        own_down = [(l, "w_down")] if l == 0 else []
        if own_mix:
            h, arrived = _mm(xb, p["w_in"], "nn", BF16, "mm_h", carry=over_ici(own_mix))
            landed(own_mix, arrived, f"mix_l{l}")
        else:
            h = _mm(xb, p["w_in"], "nn", BF16, "mm_h")
        (y_a,), arrived = _swa_fwd(h, cb("qa"), cb("ka"), cb("va"), p["sinks"], carry=over_ici(own_up))
        landed(own_up, arrived, f"up_l{l}")
        p.update(w_uq=_perm_w_uq(full["w_uq"][l]), w_ukv=_perm_w_ukv(full["w_ukv"][l]), w_pa=full["w_proj_a"][l],
                 w_pb=full["w_proj_b"][l], w_pc=full["w_proj_c"][l], w_o=full["w_o"][l])
        def fn_rms(i, rows, ps):
            return [_rms_norm(rows[0].astype(F32), ps[0]), _rms_norm(rows[1].astype(F32), ps[1])], []
        (cqn, ckvn), _ = _rowwise(fn_rms, [(h, MLA_Q_RANK, cb("cq")), (h, MLA_KV_RANK, cb("ckv"))], [p["qg"], p["kvg"]],
                                  [(MLA_Q_RANK, BF16), (MLA_KV_RANK, BF16)], [], TM, "mla_rms")
        q_full = _mm(cqn, p["w_uq"], "nn", BF16, "mm_q")
        kv = _mm(ckvn, p["w_ukv"], "nn", BF16, "mm_kv")
        qr, kr = _rope_call(q_full, MLA_HEADS * LANES, 1, h, LANES, cb("kr"), cos_t, sin_t, 1.0, TM, "rope_fwd")
        behind_mla = own_down + nxt(l, ["w_in"] + MIX)
        (y_b, lse), arrived = _mla_fwd(q_full, qr, kv, kr, T, carry=over_ici(behind_mla))
        landed(behind_mla, arrived, f"behind_mla_l{l}")
        y_c = _sgu_fwd(h, cb("hu"), cb("hv"), p["sg"], p["sb"], p["sw"], p["sb3"])
        pa = _mm(y_a, p["w_pa"], "nn", F32, "mm_pa")
        pb = _mm(y_b, p["w_pb"], "nn", F32, "mm_pb")
        pc = _mm(y_c, p["w_pc"], "nn", F32, "mm_pc")

        def merge_math(pa_, pb_, pc_, g_, b0, b1, b2):
            out = 0.0
            for br, (pp, bb) in enumerate(zip((pa_, pb_, pc_), (b0, b1, b2))):
                gate = jax.nn.sigmoid(g_[:, br * D:(br + 1) * D].astype(F32) + bb)
                out = out + gate * pp
            return out

        def fn_merge(i, rows, ps):
            bgv = ps[0]
            return [merge_math(rows[0], rows[1], rows[2], rows[3], bgv[0:1], bgv[1:2], bgv[2:3])], []
        (merged,), _ = _rowwise(fn_merge, [_whole(pa), _whole(pb), _whole(pc), (h, N_BRANCHES * D, cb("g"))], [p["bg"]],
                                [(D, BF16)], [], TMW, "merge_fwd")
        o = _mm(merged, p["w_o"], "nn", F32, "mm_o")

        def ln_res_math(x_, o_, g_, b_):
            return _layer_norm(DN_ALPHA * x_ + o_, g_, b_)

        def fn_ln(i, rows, ps):
            y = ln_res_math(rows[0], rows[1], ps[0], ps[1])
            return [y, y], []
        (x1, x1b), _ = _rowwise(fn_ln, [_whole(xs), _whole(o)], [p["l1g"], p["l1b"]], [(D, F32), (D, BF16)], [], TM, "ln1_fwd")
        p.update(w_up=full["w_up"][l], w_down=full["w_down"][l])
        if nxt(l, ["w_up"]):
            up, arrived = _mm(x1b, p["w_up"], "nn", BF16, "mm_up", carry=over_ici(nxt(l, ["w_up"])))
            landed(nxt(l, ["w_up"]), arrived, f"up_l{l + 1}")
        else:
            up = _mm(x1b, p["w_up"], "nn", BF16, "mm_up")
        cv_ = _conv_fwd(up, p["cw"], p["cbias"])

        def glu_math(cg, cvv):
            return jax.nn.silu(cg.astype(F32)) * cvv.astype(F32)

        def fn_glu(i, rows, ps):
            return [glu_math(rows[0], rows[1])], []
        (act,), _ = _rowwise(fn_glu, [(cv_, FF, 0), (cv_, FF, 1)], [], [(FF, BF16)], [], TMW, "glu_fwd")
        if nxt(l, ["w_down"]):
            dn, arrived = _mm(act, p["w_down"], "nn", F32, "mm_down", carry=over_ici(nxt(l, ["w_down"])))
            landed(nxt(l, ["w_down"]), arrived, f"down_l{l + 1}")
        else:
            dn = _mm(act, p["w_down"], "nn", F32, "mm_down")
        (x2, x2b), _ = _rowwise(fn_ln, [_whole(x1), _whole(dn)], [p["l2g"], p["l2b"]], [(D, F32), (D, BF16)], [], TM, "ln2_fwd")
        saved.append(dict(p=p, x0=xs, x0b=xb, h=h, y_a=y_a, cqn=cqn, ckvn=ckvn, q_full=q_full, kv=kv, qr=qr, kr=kr, y_b=y_b,
                          lse=lse, y_c=y_c, pa=pa, pb=pb, pc=pc, merged=merged, o=o, x1=x1, x1b=x1b, up=up, cv=cv_, act=act,
                          dn=dn))
        xs, xb = x2, x2b

    def fn_loss(i, rows, ps):
        diff = rows[0] - rows[1]
        part = jnp.sum(jnp.mean(jnp.square(diff), axis=-1, keepdims=True), axis=0, keepdims=True)
        return [diff * (1.0 / D)], [jnp.broadcast_to(part, (8, LANES))]
    (dx,), (loss_acc,) = _rowwise(fn_loss, [_whole(xs), _whole(loss_target[0])], [], [(D, F32)], [(8, LANES)], TM, "loss")
    loss = lax.psum(0.5 * loss_acc[0, 0], ("x", "y", "c"))

    gbig = {k: [None] * L for k in BIG}
    gsmall = {k: [None] * L for k in SMALL}
    reduced = {}
    pending = None
    for l in reversed(range(L)):
        s = saved[l]
        p = s["p"]

        def fn_ln_bwd(i, rows, ps):
            _, vjp = jax.vjp(ln_res_math, rows[0], rows[1], ps[0], ps[1])
            dx_, do_, dg_, db_ = vjp(rows[2])
            return [dx_, do_], [dg_, db_]
        (dx1_res, ddn), (g_l2g, g_l2b) = _rowwise(fn_ln_bwd, [_whole(s["x1"]), _whole(s["dn"]), _whole(dx)], [p["l2g"], p["l2b"]],
                                                  [(D, F32), (D, BF16)], [(1, D), (1, D)], TM, "ln2_bwd")
        gsmall["ln2_g"][l], gsmall["ln2_b"][l] = g_l2g, g_l2b
        row_chunks = lambda g: g.reshape(N_CHIPS, g.shape[0] // N_CHIPS, g.shape[1])
        gbig["w_down"][l] = row_chunks(_mm(s["act"], ddn, "tn", F32, "mm_dw_down"))
        dact = _mm(ddn, p["w_down"], "nt", BF16, "mm_dact")

        def fn_glu_bwd(i, rows, ps):
            _, vjp = jax.vjp(glu_math, rows[0], rows[1])
            dcg, dcv = vjp(rows[2].astype(F32))
            return [jnp.concatenate([dcg, dcv], axis=1)], []
        (dc,), _ = _rowwise(fn_glu_bwd, [(s["cv"], FF, 0), (s["cv"], FF, 1), _whole(dact)], [], [(FF2, BF16)], [], TMW, "glu_bwd")
        dup, g_cw, g_cb = _conv_bwd(s["up"], dc, p["cw"])
        gsmall["conv_w"][l], gsmall["conv_b"][l] = g_cw, g_cb
        if pending is None:
            gbig["w_up"][l] = _mm(s["x1b"], dup, "tn", F32, "mm_dw_up", col_chunks=N_CHIPS, tm=2048)
        else:
            gbig["w_up"][l], from_sibling = _mm(s["x1b"], dup, "tn", F32, "mm_dw_up", col_chunks=N_CHIPS, tm=2048,
                                                carry=_swap_stage(pending))
            partial = [_add_half(g, r, cc, f"rs_add_{k}_l{l + 1}") for k, g, r in zip(BIG, pending, from_sibling)]
        dx1 = _mm(dup, p["w_up"], "nt", F32, "mm_dx1", add=dx1_res)
        (dx0_res, do_), (g_l1g, g_l1b) = _rowwise(fn_ln_bwd, [_whole(s["x0"]), _whole(s["o"]), _whole(dx1)], [p["l1g"], p["l1b"]],
                                                  [(D, F32), (D, BF16)], [(1, D), (1, D)], TM, "ln1_bwd")
        gsmall["ln1_g"][l], gsmall["ln1_b"][l] = g_l1g, g_l1b
        gbig["w_o"][l] = row_chunks(_mm(s["merged"], do_, "tn", F32, "mm_dw_o"))
        dmerged = _mm(do_, p["w_o"], "nt", F32, "mm_dmerged")

        def fn_merge_bwd(i, rows, ps):
            bgv = ps[0]
            _, vjp = jax.vjp(merge_math, rows[0], rows[1], rows[2], rows[3], bgv[0:1], bgv[1:2], bgv[2:3])
            dpa, dpb, dpc, dg_, db0, db1, db2 = vjp(rows[4])
            return [dpa, dpb, dpc, dg_], [db0, db1, db2]
        (dpa, dpb, dpc, dgl), (db0, db1, db2) = _rowwise(
            fn_merge_bwd, [_whole(s["pa"]), _whole(s["pb"]), _whole(s["pc"]), (s["h"], N_BRANCHES * D, cb("g")), _whole(dmerged)],
            [p["bg"]], [(D, BF16), (D, BF16), (D, BF16), (N_BRANCHES * D, BF16)], [(1, D)] * 3, TMW, "merge_bwd")
        gsmall["b_gate"][l] = jnp.concatenate([db0, db1, db2], axis=0)
        gbig["w_proj_a"][l] = _mm(s["y_a"], dpa, "tn", F32, "mm_dw_pa", col_chunks=N_CHIPS)
        gbig["w_proj_b"][l] = row_chunks(_mm(s["y_b"], dpb, "tn", F32, "mm_dw_pb"))
        gbig["w_proj_c"][l] = _mm(s["y_c"], dpc, "tn", F32, "mm_dw_pc", col_chunks=N_CHIPS)
        dy_a = _mm(dpa, p["w_pa"], "nt", BF16, "mm_dy_a")
        dy_b = _mm(dpb, p["w_pb"], "nt", BF16, "mm_dy_b")
        dy_c = _mm(dpc, p["w_pc"], "nt", BF16, "mm_dy_c")
        dh_c, g_sg, g_sb, g_sw, g_sb3 = _sgu_bwd(s["h"], cb("hu"), cb("hv"), dy_c, p["sg"], p["sb"], p["sw"], p["sb3"])
        gsmall["sgu_ln_g"][l], gsmall["sgu_ln_b"][l], gsmall["sgu_w"][l] = g_sg, g_sb, g_sw
        gsmall["sgu_b"][l] = g_sb3.reshape(SGU_GROUPS, SGU_CHUNK)
        hide_early = l == 0
        early = [gbig[k][l] for k in EARLY]
        (dqa, dka, dva, g_sinks), early_from_sibling = _swa_bwd(s["h"], cb("qa"), cb("ka"), cb("va"), p["sinks"], dy_a,
                                                                carry=_swap_stage(early) if hide_early else None)
        gsmall["sinks"][l] = g_sinks
        if hide_early:
            early_partial = [_add_half(g, r, cc, f"rs_add_{k}_l{l}") for k, g, r in zip(EARLY, early, early_from_sibling)]
        delta = _mla_delta(dy_b, s["y_b"], T)
        (dqn, dqr), early_arrived = _mla_bwd_dq(s["q_full"], s["qr"], s["kv"], s["kr"], dy_b, s["lse"], delta, T,
                                                carry=_scatter_stage([p16 for _, p16 in early_partial]) if hide_early else None)
        if hide_early:
            for k, (p32, _), arr in zip(EARLY, early_partial, early_arrived):
                reduced[k] = _sum_into(p32, arr, chip, cc, l, L, reduced.get(k), f"rs_sum_{k}_l{l}")
        (dkn, dv, dkr_heads), arrived = _mla_bwd_dkv(
            s["q_full"], s["qr"], s["kv"], s["kr"], dy_b, s["lse"], delta, T,
            carry=None if pending is None else _scatter_stage([p16 for _, p16 in partial]))
        if pending is not None:
            for k, (p32, _), arr in zip(BIG, partial, arrived):
                reduced[k] = _sum_into(p32, arr, chip, cc, l + 1, L, reduced.get(k), f"rs_sum_{k}_l{l + 1}")
        dkr = _sum_heads(dkr_heads, TM)
        dqr_raw, dkr_raw = _rope_call(dqr, MLA_HEADS * LANES, 0, dkr, LANES, 0, cos_t, sin_t, -1.0, TM, "rope_bwd")
        dq_full = jnp.concatenate([dqn, dqr_raw], axis=1)
        dkv = jnp.concatenate([dkn, dv], axis=1)
        gbig["w_uq"][l] = _col_chunks(_unperm_w_uq(_mm(s["cqn"], dq_full, "tn", F32, "mm_dw_uq")))
        gbig["w_ukv"][l] = _col_chunks(_unperm_w_ukv(_mm(s["ckvn"], dkv, "tn", F32, "mm_dw_ukv")))
        dcqn = _mm(dq_full, p["w_uq"], "nt", F32, "mm_dcqn")
        dckvn = _mm(dkv, p["w_ukv"], "nt", F32, "mm_dckvn")

        def fn_rms_bwd(i, rows, ps):
            _, vjp1 = jax.vjp(lambda c_, g_: _rms_norm(c_.astype(F32), g_), rows[0], ps[0])
            _, vjp2 = jax.vjp(lambda c_, g_: _rms_norm(c_.astype(F32), g_), rows[1], ps[1])
            d1, dg1 = vjp1(rows[2])
            d2, dg2 = vjp2(rows[3])
            return [d1, d2], [dg1, dg2]
        (dcq, dckv), (g_qg, g_kvg) = _rowwise(
            fn_rms_bwd, [(s["h"], MLA_Q_RANK, cb("cq")), (s["h"], MLA_KV_RANK, cb("ckv")), _whole(dcqn), _whole(dckvn)],
            [p["qg"], p["kvg"]], [(MLA_Q_RANK, BF16), (MLA_KV_RANK, BF16)], [(1, MLA_Q_RANK), (1, MLA_KV_RANK)], TM, "mla_rms_bwd")
        gsmall["q_norm_g"][l], gsmall["kv_norm_g"][l] = g_qg, g_kvg
        tail = jnp.zeros((S, NP - lay["kr"][0] - LANES), BF16)
        dh = jnp.concatenate([dgl, dqa, dcq, dckv, dh_c, dka, dva, dkr_raw, tail], axis=1)
        gbig["w_in"][l] = _col_chunks(_unperm_w_in(_mm(s["x0b"], dh, "tn", F32, "mm_dw_in"), lay, D))
        dx = _mm(dh, p["w_in"], "nt", F32, "mm_dx0", add=dx0_res)

        pending = [gbig[k][l] for k in BIG]

    late = [gbig[k][0] for k in LATE]
    from_sibling = _run_stage(_swap_stage(late), "rs_swap_halves_l0")
    partial = [_add_half(g, r, cc, f"rs_add_{k}_l0") for k, g, r in zip(LATE, late, from_sibling)]
    arrived = _run_stage(_scatter_stage([p16 for _, p16 in partial]), "rs_scatter_chips_l0")
    for k, (p32, _), arr in zip(LATE, partial, arrived):
        reduced[k] = _sum_into(p32, arr, chip, cc, 0, L, reduced.get(k), f"rs_sum_{k}_l0")

    grad_x = dx.reshape(x.shape)
    g_big = dict(zip(BIG, _run_stage(_join_stage([reduced[k] for k in BIG]), "rs_join_halves")))

    small_shapes = {k: (small_sharded_full[k] if k in SMALL_SHARDED else tuple(W[k].shape)) for k in SMALL}
    sv = jnp.concatenate([jnp.stack(gsmall[k]).reshape(-1) for k in SMALL])
    n_sv = sv.shape[0]
    sv = jnp.pad(sv, (0, -n_sv % (8 * LANES))).reshape(-1, LANES)
    sv = _all_reduce_small(sv).reshape(-1)
    g_small, o_ = {}, 0
    for k in SMALL:
        n = int(np.prod(small_shapes[k]))
        g = sv[o_:o_ + n].reshape(small_shapes[k])
        if k in SMALL_SHARDED:
            g = lax.dynamic_slice_in_dim(g, chip * W[k].shape[-1], W[k].shape[-1], axis=-1)
        g_small[k] = g
        o_ += n

    delta, new_m, new_v = {}, {}, {}
    for k in BIG:
        shp = shards[k]
        v2 = lambda t: t.reshape(shp[0] * shp[1], shp[2])
        d_, m_, v_ = _adamw(v2(W[k]), v2(g_big[k]), v2(Mo[k]), v2(Vo[k]), "adamw_" + k)
        delta[k], new_m[k], new_v[k] = d_.reshape(shp), m_.reshape(shp), v_.reshape(shp)
    pack = lambda t: jnp.concatenate([t[k].reshape(-1) for k in SMALL])
    n_small = sum(int(np.prod(W[k].shape)) for k in SMALL)
    pad2 = lambda t: jnp.pad(t, (0, -n_small % (8 * LANES))).reshape(-1, LANES)
    d_, m_, v_ = _adamw(pad2(pack(W)), pad2(pack(g_small)), pad2(pack(Mo)), pad2(pack(Vo)), "adamw_small")
    o_ = 0
    for k in SMALL:
        n = int(np.prod(W[k].shape))
        take = lambda t: t.reshape(-1)[o_:o_ + n].reshape(W[k].shape)
        delta[k], new_m[k], new_v[k] = take(d_), take(m_), take(v_)
        o_ += n

    grads = {**g_big, **g_small}
    return (loss, grad_x, *[grads[k] for k in WEIGHTS], *[delta[k] for k in WEIGHTS], *[new_m[k] for k in WEIGHTS],
            *[new_v[k] for k in WEIGHTS])
```

```python
import functools
import math

import jax
import jax.numpy as jnp
import numpy as np
from jax import lax
from jax.experimental import pallas as pl
from jax.experimental.pallas import tpu as pltpu

F32, BF16 = jnp.float32, jnp.bfloat16
SDS = jax.ShapeDtypeStruct
MESH = pl.DeviceIdType.MESH

SWA_Q_HEADS, SWA_KV_HEADS, SWA_HEAD_DIM, SWA_BLOCK = 16, 2, 64, 128
MLA_HEADS, MLA_NOPE, MLA_ROPE, MLA_V = 16, 128, 64, 128
MLA_Q_RANK, MLA_KV_RANK = 512, 512
ROPE_THETA = 10000.0
SGU_GROUPS, SGU_GROUP_DIM, SGU_CHUNK = 8, 128, 128
SGU_WIDTH = SGU_GROUPS * SGU_GROUP_DIM
A_Q = SWA_Q_HEADS * SWA_HEAD_DIM
A_KV = SWA_KV_HEADS * SWA_HEAD_DIM
N_BRANCHES = 3
DEPTH = 2
EPS = 1e-5
MASK_VALUE = -1e30
DN_ALPHA = (2 * DEPTH) ** 0.25
ADAM_LR, ADAM_B1, ADAM_B2, ADAM_EPS, ADAM_WD, ADAM_STEP = 0.001, 0.9, 0.999, 1e-08, 0.01, 10
N_CHIPS = 4

LANES = 128
VMEM_LIMIT = 48 * 1024 * 1024

BIG = ["w_in", "w_uq", "w_ukv", "w_proj_a", "w_proj_b", "w_proj_c", "w_o", "w_up", "w_down"]
ROW_SHARDED = {"w_proj_b", "w_o", "w_down"}
LATE = ["w_in", "w_uq", "w_ukv"]
EARLY = [k for k in BIG if k not in LATE]
SMALL = ["b_gate", "sinks", "q_norm_g", "kv_norm_g", "sgu_ln_g", "sgu_ln_b", "sgu_w", "sgu_b", "ln1_g", "ln1_b",
         "conv_w", "conv_b", "ln2_g", "ln2_b"]
SMALL_SHARDED = {"b_gate", "conv_w"}
WEIGHTS = ["w_in", "b_gate", "sinks", "q_norm_g", "kv_norm_g", "w_uq", "w_ukv", "sgu_ln_g", "sgu_ln_b", "sgu_w", "sgu_b",
           "w_proj_a", "w_proj_b", "w_proj_c", "w_o", "ln1_g", "ln1_b", "w_up", "conv_w", "conv_b", "w_down", "ln2_g", "ln2_b"]


def _pcall(body, **kw):
    return pl.pallas_call(body, **kw)


def _params(sem=None):
    return pltpu.CompilerParams(dimension_semantics=sem, vmem_limit_bytes=VMEM_LIMIT)


def _tile(dim, pref, align=LANES):
    t = (min(pref, dim) // align) * align
    while t >= align:
        if dim % t == 0:
            return t
        t -= align
    return dim


def _mm(a, b, mode, out_dtype, name, add=None, tm=1024, tn=512, tk=2048, col_chunks=1, carry=None):
    if mode == "nn":
        (M, K), (K2, N) = a.shape, b.shape
    elif mode == "nt":
        (M, K), (N, K2) = a.shape, b.shape
    else:
        (K, M), (K2, N) = a.shape, b.shape
    assert K == K2, (a.shape, b.shape, mode)
    assert N % col_chunks == 0
    tm, tn, tk = _tile(M, tm), _tile(N // col_chunks, tn), _tile(K, tk)
    assert (N // col_chunks) % tn == 0
    per_chunk = (N // col_chunks) // tn
    nk = K // tk
    if mode == "tn":
        a_spec = pl.BlockSpec((tk, tm), lambda i, j, k: (k, i))
    else:
        a_spec = pl.BlockSpec((tm, tk), lambda i, j, k: (i, k))
    if mode == "nt":
        b_spec = pl.BlockSpec((tn, tk), lambda i, j, k: (j, k))
    else:
        b_spec = pl.BlockSpec((tk, tn), lambda i, j, k: (k, j))
    dn = {"nn": (((1,), (0,)), ((), ())), "nt": (((1,), (1,)), ((), ())), "tn": (((0,), (0,)), ((), ()))}[mode]
    chunked = col_chunks > 1
    if chunked:
        assert add is None
        o_spec = pl.BlockSpec((1, tm, tn), lambda i, j, k: (lax.div(j, per_chunk), i, lax.rem(j, per_chunk)))
        out_shape = SDS((col_chunks, M, N // col_chunks), out_dtype)
    else:
        o_spec = pl.BlockSpec((tm, tn), lambda i, j, k: (i, j))
        out_shape = SDS((M, N), out_dtype)
    has_add = add is not None

    def body(*refs):
        if has_add:
            a_ref, b_ref, add_ref, o_ref, acc_ref = refs
        else:
            a_ref, b_ref, o_ref, acc_ref = refs
        k = pl.program_id(2)

        @pl.when(k == 0)
        def _():
            acc_ref[...] = jnp.zeros_like(acc_ref)

        acc_ref[...] += lax.dot_general(a_ref[...].astype(BF16), b_ref[...].astype(BF16), dn,
                                        preferred_element_type=F32)

        @pl.when(k == nk - 1)
        def _():
            r = acc_ref[...]
            if has_add:
                r = r + add_ref[...].astype(F32)
            if chunked:
                o_ref[0] = r.astype(o_ref.dtype)
            else:
                o_ref[...] = r.astype(o_ref.dtype)

    ins = [a, b] + ([add] if has_add else [])
    in_specs = [a_spec, b_spec] + ([o_spec] if has_add else [])
    (out,), carried = _call(body, name, (M // tm, N // tn, nk), 0, in_specs, [o_spec], [out_shape], [pltpu.VMEM((tm, tn), F32)],
                            ins, ("parallel", "parallel", "arbitrary"), carry)
    return out if carry is None else (out, carried)


def _rowwise(fn, rows, params, row_outs, acc_outs, tm, name):
    n_rows = rows[0][0].shape[0]
    assert n_rows % tm == 0
    nr, npar, no = len(rows), len(params), len(row_outs)

    def body(*refs):
        i = pl.program_id(0)
        r, p = refs[:nr], refs[nr:nr + npar]
        o, acc = refs[nr + npar:nr + npar + no], refs[nr + npar + no:]
        outs, sums = fn(i, [x[...] for x in r], [x[...] for x in p])
        for ref, val in zip(o, outs, strict=True):
            ref[...] = val.astype(ref.dtype)
        if acc:
            @pl.when(i == 0)
            def _():
                for ref in acc:
                    ref[...] = jnp.zeros_like(ref)
            for ref, val in zip(acc, sums, strict=True):
                ref[...] += val.astype(F32)

    def full(shape):
        nd = len(shape)
        return pl.BlockSpec(tuple(shape), lambda i: (0,) * nd)

    in_specs = [pl.BlockSpec((tm, w), (lambda i, cb=cb: (i, cb))) for (_, w, cb) in rows] + [full(p.shape) for p in params]
    out_specs = [pl.BlockSpec((tm, w), lambda i: (i, 0)) for (w, _) in row_outs] + [full(s) for s in acc_outs]
    out_shape = [SDS((n_rows, w), dt) for (w, dt) in row_outs] + [SDS(tuple(s), F32) for s in acc_outs]
    res = _pcall(body, name=name, out_shape=out_shape, grid=(n_rows // tm,), in_specs=in_specs, out_specs=out_specs,
                 compiler_params=_params(("arbitrary",)))(*[r[0] for r in rows], *params)
    return list(res[:no]), list(res[no:])


def _whole(a):
    return (a, a.shape[1], 0)


def _gelu(x):
    return 0.5 * x * (1.0 + lax.erf(x * (1.0 / math.sqrt(2.0))))


def _layer_norm(x, g, b):
    mu = x.mean(-1, keepdims=True)
    var = jnp.mean(jnp.square(x - mu), -1, keepdims=True)
    return (x - mu) * lax.rsqrt(var + EPS) * g + b


def _rms_norm(x, g):
    return x * lax.rsqrt(jnp.mean(jnp.square(x), -1, keepdims=True) + EPS) * g


def _sgu_math(hu, hv, ln_g, ln_b, ws, bs):
    u = _gelu(hu.astype(F32))
    vn = _layer_norm(_gelu(hv.astype(F32)), ln_g, ln_b)
    r = lax.broadcasted_iota(jnp.int32, (SGU_CHUNK, SGU_CHUNK), 0)
    c = lax.broadcasted_iota(jnp.int32, (SGU_CHUNK, SGU_CHUNK), 1)
    outs = []
    for g in range(SGU_GROUPS):
        w = jnp.where(r >= c, ws[g], 0.0).astype(BF16)
        vg = vn[:, g * SGU_GROUP_DIM:(g + 1) * SGU_GROUP_DIM].astype(BF16)
        outs.append(jnp.dot(w, vg, preferred_element_type=F32) + bs[g])
    return u * jnp.concatenate(outs, axis=1)


def _sgu_fwd(h, cu, cv, ln_g, ln_b, w, b3):
    def fn(i, rows, ps):
        g_, b_, w_, b3_ = ps
        y = _sgu_math(rows[0], rows[1], g_, b_, [w_[g] for g in range(SGU_GROUPS)], [b3_[g] for g in range(SGU_GROUPS)])
        return [y], []
    (y,), _ = _rowwise(fn, [(h, SGU_WIDTH, cu), (h, SGU_WIDTH, cv)], [ln_g, ln_b, w, b3], [(SGU_WIDTH, BF16)], [],
                       SGU_CHUNK, "sgu_fwd")
    return y


def _sgu_bwd(h, cu, cv, dy, ln_g, ln_b, w, b3):
    nd = 2 * SGU_WIDTH

    def body(hu_ref, hv_ref, dy_ref, g_ref, b_ref, w_ref, b3_ref, dh_ref, dg_ref, db_ref, dw_ref, db3_ref):
        i = pl.program_id(0)

        @pl.when(i == 0)
        def _():
            dg_ref[...] = jnp.zeros_like(dg_ref)
            db_ref[...] = jnp.zeros_like(db_ref)
            dw_ref[...] = jnp.zeros_like(dw_ref)
            db3_ref[...] = jnp.zeros_like(db3_ref)

        ws = [w_ref[g] for g in range(SGU_GROUPS)]
        bs = [b3_ref[g] for g in range(SGU_GROUPS)]
        _, vjp = jax.vjp(_sgu_math, hu_ref[...], hv_ref[...], g_ref[...], b_ref[...], ws, bs)
        dhu, dhv, dg, db, dws, dbs = vjp(dy_ref[...].astype(F32))
        dh_ref[...] = jnp.concatenate([dhu, dhv], axis=1).astype(dh_ref.dtype)
        dg_ref[...] += dg
        db_ref[...] += db
        for g in range(SGU_GROUPS):
            dw_ref[g] += dws[g]
            db3_ref[g] += dbs[g]

    n = h.shape[0]
    blk = lambda cb: pl.BlockSpec((SGU_CHUNK, SGU_WIDTH), lambda i, cb=cb: (i, cb))
    full = lambda s: pl.BlockSpec(tuple(s), lambda i: (0,) * len(s))
    return _pcall(
        body, name="sgu_bwd", grid=(n // SGU_CHUNK,),
        out_shape=[SDS((n, nd), BF16), SDS(ln_g.shape, F32), SDS(ln_b.shape, F32), SDS(w.shape, F32), SDS(b3.shape, F32)],
        in_specs=[blk(cu), blk(cv), blk(0), full(ln_g.shape), full(ln_b.shape), full(w.shape), full(b3.shape)],
        out_specs=[pl.BlockSpec((SGU_CHUNK, nd), lambda i: (i, 0)), full(ln_g.shape), full(ln_b.shape), full(w.shape),
                   full(b3.shape)],
        compiler_params=_params(("arbitrary",)))(h, h, dy, ln_g, ln_b, w, b3)


def _swa_math(q, kp, kc, vp, vc, sinks, not_first):
    kw = jnp.concatenate([kp, kc], axis=0).astype(BF16)
    vw = jnp.concatenate([vp, vc], axis=0).astype(BF16)
    qb = q.astype(BF16)
    q_off = lax.broadcasted_iota(jnp.int32, (SWA_BLOCK, 2 * SWA_BLOCK), 0) + SWA_BLOCK
    k_off = lax.broadcasted_iota(jnp.int32, (SWA_BLOCK, 2 * SWA_BLOCK), 1)
    rel = q_off - k_off
    valid = (rel >= 0) & (rel < SWA_BLOCK) & (not_first | (k_off >= SWA_BLOCK))
    G = SWA_Q_HEADS // SWA_KV_HEADS
    outs = []
    for head in range(SWA_Q_HEADS):
        hk = head // G
        qh = qb[:, head * SWA_HEAD_DIM:(head + 1) * SWA_HEAD_DIM]
        kh = kw[:, hk * SWA_HEAD_DIM:(hk + 1) * SWA_HEAD_DIM]
        vh = vw[:, hk * SWA_HEAD_DIM:(hk + 1) * SWA_HEAD_DIM]
        s = lax.dot_general(qh, kh, (((1,), (1,)), ((), ())), preferred_element_type=F32) * (SWA_HEAD_DIM ** -0.5)
        s = jnp.where(valid, s, MASK_VALUE)
        sink = sinks[:, head:head + 1]
        m = jnp.maximum(s.max(-1, keepdims=True), sink)
        p = jnp.exp(s - m)
        p = (p / (p.sum(-1, keepdims=True) + jnp.exp(sink - m))).astype(BF16)
        outs.append(jnp.dot(p, vh, preferred_element_type=F32))
    return jnp.concatenate(outs, axis=1)


def _swa_fwd(h, cq, ck, cv, sinks, carry=None):
    n = h.shape[0]
    nb = n // SWA_BLOCK

    def body(q_ref, kp_ref, kc_ref, vp_ref, vc_ref, s_ref, o_ref):
        i = pl.program_id(0)
        f = lambda x: x[...].astype(F32)
        o_ref[...] = _swa_math(f(q_ref), f(kp_ref), f(kc_ref), f(vp_ref), f(vc_ref), s_ref[...], i > 0).astype(o_ref.dtype)

    prev = lambda cb: pl.BlockSpec((SWA_BLOCK, A_KV), lambda i, cb=cb: (jnp.maximum(i - 1, 0), cb))
    cur = lambda cb: pl.BlockSpec((SWA_BLOCK, A_KV), lambda i, cb=cb: (i, cb))
    return _call(body, "swa_fwd", (nb,), 0,
                 [pl.BlockSpec((SWA_BLOCK, A_Q), lambda i: (i, cq)), prev(ck), cur(ck), prev(cv), cur(cv),
                  pl.BlockSpec((1, SWA_Q_HEADS), lambda i: (0, 0))],
                 [pl.BlockSpec((SWA_BLOCK, A_Q), lambda i: (i, 0))], [SDS((n, A_Q), BF16)], [],
                 [h, h, h, h, h, sinks], ("arbitrary",), carry)


def _swa_bwd(h, cq, ck, cv, sinks, dy, carry=None):
    n = h.shape[0]
    nb = n // SWA_BLOCK

    def body(q_ref, kp_ref, kc_ref, vp_ref, vc_ref, s_ref, dy_ref, dq_ref, dk_ref, dv_ref, ds_ref, ck_ref, cv_ref):
        r = pl.program_id(0)
        blk = nb - 1 - r

        @pl.when(r == 0)
        def _():
            ds_ref[...] = jnp.zeros_like(ds_ref)
            ck_ref[...] = jnp.zeros_like(ck_ref)
            cv_ref[...] = jnp.zeros_like(cv_ref)

        f = lambda x: x[...].astype(F32)
        not_first = blk > 0
        _, vjp = jax.vjp(lambda q, kp, kc, vp, vc, s: _swa_math(q, kp, kc, vp, vc, s, not_first),
                         f(q_ref), f(kp_ref), f(kc_ref), f(vp_ref), f(vc_ref), s_ref[...])
        dq, dkp, dkc, dvp, dvc, dsk = vjp(f(dy_ref))
        dq_ref[...] = dq.astype(dq_ref.dtype)
        dk_ref[...] = (dkc + ck_ref[...]).astype(dk_ref.dtype)
        dv_ref[...] = (dvc + cv_ref[...]).astype(dv_ref.dtype)
        ck_ref[...] = dkp
        cv_ref[...] = dvp
        ds_ref[...] += dsk

    rev = lambda i: nb - 1 - i
    prev = lambda cb: pl.BlockSpec((SWA_BLOCK, A_KV), lambda i, cb=cb: (jnp.maximum(rev(i) - 1, 0), cb))
    cur = lambda cb: pl.BlockSpec((SWA_BLOCK, A_KV), lambda i, cb=cb: (rev(i), cb))
    return _call(
        body, "swa_bwd", (nb,), 0,
        [pl.BlockSpec((SWA_BLOCK, A_Q), lambda i: (rev(i), cq)), prev(ck), cur(ck), prev(cv), cur(cv),
         pl.BlockSpec((1, SWA_Q_HEADS), lambda i: (0, 0)), pl.BlockSpec((SWA_BLOCK, A_Q), lambda i: (rev(i), 0))],
        [pl.BlockSpec((SWA_BLOCK, A_Q), lambda i: (rev(i), 0)), pl.BlockSpec((SWA_BLOCK, A_KV), lambda i: (rev(i), 0)),
         pl.BlockSpec((SWA_BLOCK, A_KV), lambda i: (rev(i), 0)), pl.BlockSpec((1, SWA_Q_HEADS), lambda i: (0, 0))],
        [SDS((n, A_Q), BF16), SDS((n, A_KV), BF16), SDS((n, A_KV), BF16), SDS((1, SWA_Q_HEADS), F32)],
        [pltpu.VMEM((SWA_BLOCK, A_KV), F32), pltpu.VMEM((SWA_BLOCK, A_KV), F32)],
        [h, h, h, h, h, sinks, dy], ("arbitrary",), carry)


def _rope(x, cos, sin, sign):
    w = x.shape[1]
    reps = w // LANES
    ct = jnp.tile(cos, (1, reps)) if reps > 1 else cos
    st = jnp.tile(sin, (1, reps)) if reps > 1 else sin
    fwd = pltpu.roll(x, MLA_ROPE // 2, axis=1)
    bwd = pltpu.roll(x, w - MLA_ROPE // 2, axis=1)
    lane = lax.broadcasted_iota(jnp.int32, x.shape, 1) % LANES
    rot = jnp.where(lane < MLA_ROPE // 2, -bwd, fwd)
    return x * ct + sign * (rot * st)


def _rope_call(a, wa, ca, b, wb, cb, cos, sin, sign, tm, name):
    def fn(i, rows, ps):
        xa, xb, c_, s_ = rows
        return [_rope(xa.astype(F32), c_, s_, sign), _rope(xb.astype(F32), c_, s_, sign)], []
    (ra, rb), _ = _rowwise(fn, [(a, wa, ca), (b, wb, cb), _whole(cos), _whole(sin)], [], [(wa, BF16), (wb, BF16)], [], tm, name)
    return ra, rb


MLA_SCALE = (MLA_NOPE + MLA_ROPE) ** -0.5
LOG2E = math.log2(math.e)


MLA_HP = 2
MLA_W = MLA_HP * LANES


def _mla_scores(qn_ref, qr_ref, kn_ref, kr_ref, hh, masked):
    cols = slice(hh * LANES, (hh + 1) * LANES)
    q = jnp.concatenate([qn_ref[:, cols], qr_ref[:, cols]], axis=1)
    k = jnp.concatenate([kn_ref[:, cols], kr_ref[...]], axis=1)
    s = lax.dot_general(q, k, (((1,), (1,)), ((), ())), preferred_element_type=F32)
    if masked:
        row = lax.broadcasted_iota(jnp.int32, s.shape, 0)
        col = lax.broadcasted_iota(jnp.int32, s.shape, 1)
        s = jnp.where(col <= row, s, MASK_VALUE)
    return s, q, k


def _causal_pairs(nq, by_query):
    if by_query:
        pairs = [(i, j) for i in range(nq) for j in range(i + 1)]
    else:
        pairs = [(i, j) for j in range(nq) for i in range(j, nq)]
    return (jnp.asarray(np.array([p[0] for p in pairs], np.int32)), jnp.asarray(np.array([p[1] for p in pairs], np.int32)),
            len(pairs))


def _mla_fwd(q_full, qr, kv, kr, T, carry=None):
    n = q_full.shape[0]
    nq = n // T
    H = MLA_HEADS
    qi, kj, npairs = _causal_pairs(nq, True)

    def body(qi_ref, kj_ref, qn_ref, qr_ref, kn_ref, v_ref, kr_ref, y_ref, lse_ref, m_ref, l_ref, acc_ref):
        t = pl.program_id(1)
        i, j = qi_ref[t], kj_ref[t]

        @pl.when(j == 0)
        def _():
            m_ref[...] = jnp.full_like(m_ref, MASK_VALUE)
            l_ref[...] = jnp.zeros_like(l_ref)
            acc_ref[...] = jnp.zeros_like(acc_ref)

        def update(masked):
            for hh in range(MLA_HP):
                s, _, _ = _mla_scores(qn_ref, qr_ref, kn_ref, kr_ref, hh, masked)
                m_prev = m_ref[hh]
                m_new = jnp.maximum(m_prev, s.max(-1, keepdims=True))
                p = jnp.exp2((s - m_new[:, :1]) * (MLA_SCALE * LOG2E))
                alpha = jnp.exp2((m_prev - m_new) * (MLA_SCALE * LOG2E))
                l_ref[hh] = alpha * l_ref[hh] + p.sum(-1, keepdims=True)
                acc_ref[hh] = alpha * acc_ref[hh] + jnp.dot(p.astype(BF16), v_ref[:, hh * LANES:(hh + 1) * LANES],
                                                            preferred_element_type=F32)
                m_ref[hh] = m_new

        @pl.when(j < i)
        def _():
            update(False)

        @pl.when(j == i)
        def _():
            update(True)
            for hh in range(MLA_HP):
                y_ref[:, hh * LANES:(hh + 1) * LANES] = (acc_ref[hh] / l_ref[hh]).astype(y_ref.dtype)
                lse_ref[hh] = m_ref[hh] * (MLA_SCALE * LOG2E) + jnp.log2(l_ref[hh])

    G = H // MLA_HP
    qspec = lambda off: pl.BlockSpec((T, MLA_W), lambda h, t, qi, kj, off=off: (qi[t], off + h))
    kspec = lambda off: pl.BlockSpec((T, MLA_W), lambda h, t, qi, kj, off=off: (kj[t], off + h))
    return _call(
        body, "mla_fwd", (G, npairs), 2,
        [qspec(0), qspec(0), kspec(0), kspec(G), pl.BlockSpec((T, LANES), lambda h, t, qi, kj: (kj[t], 0))],
        [pl.BlockSpec((T, MLA_W), lambda h, t, qi, kj: (qi[t], h)),
         pl.BlockSpec((MLA_HP, T, LANES), lambda h, t, qi, kj: (h, qi[t], 0))],
        [SDS((n, H * MLA_V), BF16), SDS((H, n, LANES), F32)], [pltpu.VMEM((MLA_HP, T, LANES), F32)] * 3,
        [qi, kj, q_full, qr, kv, kv, kr], ("parallel", "arbitrary"), carry)


def _mla_delta(dy, y, T):
    n = y.shape[0]
    H = MLA_HEADS

    def body(dy_ref, y_ref, d_ref):
        d = jnp.sum(dy_ref[...].astype(F32) * y_ref[...].astype(F32), axis=-1, keepdims=True)
        d_ref[0] = jnp.broadcast_to(d, (T, LANES))

    spec = pl.BlockSpec((T, LANES), lambda h, i: (i, h))
    return _pcall(body, name="mla_delta", grid=(H, n // T), out_shape=SDS((H, n, LANES), F32), in_specs=[spec, spec],
                  out_specs=pl.BlockSpec((1, T, LANES), lambda h, i: (h, i, 0)),
                  compiler_params=_params(("parallel", "parallel")))(dy, y)


def _mla_bwd_dq(q_full, qr, kv, kr, dy, lse, delta, T, carry=None):
    n = q_full.shape[0]
    nq = n // T
    H = MLA_HEADS

    qi, kj, npairs = _causal_pairs(nq, True)

    def body(qi_ref, kj_ref, qn_ref, qr_ref, kn_ref, v_ref, kr_ref, dy_ref, lse_ref, dl_ref, dqn_ref, dqr_ref, acc_ref):
        t = pl.program_id(1)
        i, j = qi_ref[t], kj_ref[t]

        @pl.when(j == 0)
        def _():
            acc_ref[...] = jnp.zeros_like(acc_ref)

        def update(masked):
            for hh in range(MLA_HP):
                cols = slice(hh * LANES, (hh + 1) * LANES)
                s, _, k = _mla_scores(qn_ref, qr_ref, kn_ref, kr_ref, hh, masked)
                p = jnp.exp2(s * (MLA_SCALE * LOG2E) - lse_ref[hh][:, :1])
                dp = lax.dot_general(dy_ref[:, cols], v_ref[:, cols], (((1,), (1,)), ((), ())), preferred_element_type=F32)
                ds = p * (dp - dl_ref[hh][:, :1])
                acc_ref[hh] += jnp.dot(ds.astype(BF16), k, preferred_element_type=F32)

        @pl.when(j < i)
        def _():
            update(False)

        @pl.when(j == i)
        def _():
            update(True)
            for hh in range(MLA_HP):
                cols = slice(hh * LANES, (hh + 1) * LANES)
                dqn_ref[:, cols] = (acc_ref[hh][:, :LANES] * MLA_SCALE).astype(dqn_ref.dtype)
                dqr_ref[:, cols] = (acc_ref[hh][:, LANES:] * MLA_SCALE).astype(dqr_ref.dtype)

    G = H // MLA_HP
    qspec = lambda off: pl.BlockSpec((T, MLA_W), lambda h, t, qi, kj, off=off: (qi[t], off + h))
    kspec = lambda off: pl.BlockSpec((T, MLA_W), lambda h, t, qi, kj, off=off: (kj[t], off + h))
    stat = pl.BlockSpec((MLA_HP, T, LANES), lambda h, t, qi, kj: (h, qi[t], 0))
    out = pl.BlockSpec((T, MLA_W), lambda h, t, qi, kj: (qi[t], h))
    return _call(
        body, "mla_bwd_dq", (G, npairs), 2,
        [qspec(0), qspec(0), kspec(0), kspec(G), pl.BlockSpec((T, LANES), lambda h, t, qi, kj: (kj[t], 0)), qspec(0), stat, stat],
        [out, out], [SDS((n, H * LANES), BF16), SDS((n, H * LANES), BF16)], [pltpu.VMEM((MLA_HP, T, 2 * LANES), F32)],
        [qi, kj, q_full, qr, kv, kv, kr, dy, lse, delta], ("parallel", "arbitrary"), carry)


def _mla_bwd_dkv(q_full, qr, kv, kr, dy, lse, delta, T, carry=None):
    n = q_full.shape[0]
    nq = n // T
    H = MLA_HEADS

    qi, kj, npairs = _causal_pairs(nq, False)

    def body(qi_ref, kj_ref, qn_ref, qr_ref, kn_ref, v_ref, kr_ref, dy_ref, lse_ref, dl_ref, dkn_ref, dv_ref, dkr_ref,
             dk_acc, dv_acc):
        t = pl.program_id(1)
        i, j = qi_ref[t], kj_ref[t]

        @pl.when(i == j)
        def _():
            dk_acc[...] = jnp.zeros_like(dk_acc)
            dv_acc[...] = jnp.zeros_like(dv_acc)

        def update(masked):
            for hh in range(MLA_HP):
                cols = slice(hh * LANES, (hh + 1) * LANES)
                s, q, _ = _mla_scores(qn_ref, qr_ref, kn_ref, kr_ref, hh, masked)
                p = jnp.exp2(s * (MLA_SCALE * LOG2E) - lse_ref[hh][:, :1])
                dy = dy_ref[:, cols]
                dv_acc[hh] += lax.dot_general(p.astype(BF16), dy, (((0,), (0,)), ((), ())), preferred_element_type=F32)
                dp = lax.dot_general(dy, v_ref[:, cols], (((1,), (1,)), ((), ())), preferred_element_type=F32)
                ds = p * (dp - dl_ref[hh][:, :1])
                dk_acc[hh] += lax.dot_general(ds.astype(BF16), q, (((0,), (0,)), ((), ())), preferred_element_type=F32)

        @pl.when(i == j)
        def _():
            update(True)

        @pl.when(i > j)
        def _():
            update(False)

        @pl.when(i == nq - 1)
        def _():
            for hh in range(MLA_HP):
                cols = slice(hh * LANES, (hh + 1) * LANES)
                dkn_ref[:, cols] = (dk_acc[hh][:, :LANES] * MLA_SCALE).astype(dkn_ref.dtype)
                dv_ref[:, cols] = dv_acc[hh].astype(dv_ref.dtype)
                dkr_ref[hh] = dk_acc[hh][:, LANES:] * MLA_SCALE

    G = H // MLA_HP
    qspec = lambda off: pl.BlockSpec((T, MLA_W), lambda h, t, qi, kj, off=off: (qi[t], off + h))
    kspec = lambda off: pl.BlockSpec((T, MLA_W), lambda h, t, qi, kj, off=off: (kj[t], off + h))
    stat = pl.BlockSpec((MLA_HP, T, LANES), lambda h, t, qi, kj: (h, qi[t], 0))
    out = pl.BlockSpec((T, MLA_W), lambda h, t, qi, kj: (kj[t], h))
    return _call(
        body, "mla_bwd_dkv", (G, npairs), 2,
        [qspec(0), qspec(0), kspec(0), kspec(G), pl.BlockSpec((T, LANES), lambda h, t, qi, kj: (kj[t], 0)), qspec(0), stat, stat],
        [out, out, pl.BlockSpec((MLA_HP, T, LANES), lambda h, t, qi, kj: (h, kj[t], 0))],
        [SDS((n, H * LANES), BF16), SDS((n, H * LANES), BF16), SDS((H, n, LANES), F32)],
        [pltpu.VMEM((MLA_HP, T, 2 * LANES), F32), pltpu.VMEM((MLA_HP, T, LANES), F32)],
        [qi, kj, q_full, qr, kv, kv, kr, dy, lse, delta], ("parallel", "arbitrary"), carry)


def _sum_heads(a, tm):
    H, n, _ = a.shape

    def body(a_ref, o_ref):
        o_ref[...] = jnp.sum(a_ref[...], axis=0)

    return _pcall(body, name="mla_sum_heads", grid=(n // tm,), out_shape=SDS((n, LANES), F32),
                  in_specs=[pl.BlockSpec((H, tm, LANES), lambda i: (0, i, 0))],
                  out_specs=pl.BlockSpec((tm, LANES), lambda i: (i, 0)), compiler_params=_params(("parallel",)))(a)


def _shift_down(x, k):
    row = lax.broadcasted_iota(jnp.int32, x.shape, 0)
    return jnp.where(row >= k, pltpu.roll(x, k, axis=0), 0.0)


def _shift_up(x, k):
    n = x.shape[0]
    row = lax.broadcasted_iota(jnp.int32, x.shape, 0)
    return jnp.where(row < n - k, pltpu.roll(x, n - k, axis=0), 0.0)


def _conv_fwd(up, w, b):
    n, c = up.shape

    def body(u_ref, w_ref, b_ref, o_ref):
        u = u_ref[...].astype(F32)
        wv = w_ref[...]
        o_ref[...] = (b_ref[...] + wv[0:1] * _shift_down(u, 2) + wv[1:2] * _shift_down(u, 1) + wv[2:3] * u).astype(o_ref.dtype)

    return _pcall(body, name="conv_fwd", grid=(c // LANES,), out_shape=SDS((n, c), BF16),
                  in_specs=[pl.BlockSpec((n, LANES), lambda j: (0, j)), pl.BlockSpec((3, LANES), lambda j: (0, j)),
                            pl.BlockSpec((1, LANES), lambda j: (0, j))],
                  out_specs=pl.BlockSpec((n, LANES), lambda j: (0, j)), compiler_params=_params(("parallel",)))(up, w, b)


def _conv_bwd(up, dc, w):
    n, c = up.shape

    def body(u_ref, d_ref, w_ref, du_ref, dw_ref, db_ref):
        u = u_ref[...].astype(F32)
        d = d_ref[...].astype(F32)
        wv = w_ref[...]
        du_ref[...] = (wv[2:3] * d + wv[1:2] * _shift_up(d, 1) + wv[0:1] * _shift_up(d, 2)).astype(du_ref.dtype)
        dw_ref[0:1, :] = jnp.sum(d * _shift_down(u, 2), axis=0, keepdims=True)
        dw_ref[1:2, :] = jnp.sum(d * _shift_down(u, 1), axis=0, keepdims=True)
        dw_ref[2:3, :] = jnp.sum(d * u, axis=0, keepdims=True)
        db_ref[...] = jnp.sum(d, axis=0, keepdims=True)

    col = pl.BlockSpec((n, LANES), lambda j: (0, j))
    return _pcall(body, name="conv_bwd", grid=(c // LANES,),
                  out_shape=[SDS((n, c), BF16), SDS((3, c), F32), SDS((1, c), F32)],
                  in_specs=[col, col, pl.BlockSpec((3, LANES), lambda j: (0, j))],
                  out_specs=[col, pl.BlockSpec((3, LANES), lambda j: (0, j)), pl.BlockSpec((1, LANES), lambda j: (0, j))],
                  compiler_params=_params(("parallel",)))(up, dc, w)


def _adamw_math(w, g, m, v):
    m = ADAM_B1 * m + (1.0 - ADAM_B1) * g
    v = ADAM_B2 * v + (1.0 - ADAM_B2) * jnp.square(g)
    m_hat = m / (1.0 - ADAM_B1 ** ADAM_STEP)
    v_hat = v / (1.0 - ADAM_B2 ** ADAM_STEP)
    delta = -ADAM_LR * (m_hat / (jnp.sqrt(v_hat) + ADAM_EPS) + ADAM_WD * w)
    return delta, m, v


def _adamw(w, g, m, v, name):
    L, r, c = w.shape
    tr = _rows_tile(r, c, 1 << 20, 8)

    def body(w_ref, g_ref, m_ref, v_ref, d_ref, nm_ref, nv_ref):
        d, nm, nv = _adamw_math(w_ref[...], g_ref[...], m_ref[...], v_ref[...])
        d_ref[...] = d
        nm_ref[...] = nm
        nv_ref[...] = nv

    spec = pl.BlockSpec((1, tr, c), lambda l, i: (l, i, 0))
    return _pcall(body, name=name, grid=(L, r // tr), out_shape=[SDS(w.shape, F32)] * 3, in_specs=[spec] * 4,
                  out_specs=[spec] * 3, compiler_params=_params(("parallel", "parallel")))(w, g, m, v)


def _coords():
    return lax.axis_index("x"), lax.axis_index("y"), lax.axis_index("c")


def _other_chips(x, y):
    return [(1 - x, y), (x, 1 - y), (1 - x, 1 - y)]


HBM_SPEC = pl.BlockSpec(memory_space=pltpu.HBM)


def _half(c, rows):
    return pl.ds(pl.multiple_of(c * rows, 16), rows)


def _rows_tile(rows, cols, budget_bytes=2 << 20, align=16):
    t = (min(max(align, budget_bytes // (4 * cols)), rows) // align) * align
    while t >= align:
        if rows % t == 0:
            return t
        t -= align
    return rows


def _scalar(v):
    return jnp.reshape(jnp.asarray(v, jnp.int32), (1,))


def _cast_into_slot(w3, layer, slot, name):
    _, R, C = w3.shape
    tr = _rows_tile(R, C)

    def body(s_ref, w_ref, o_ref):
        o_ref[0] = w_ref[0].astype(BF16)

    gs = pltpu.PrefetchScalarGridSpec(
        num_scalar_prefetch=1, grid=(R // tr,),
        in_specs=[pl.BlockSpec((1, tr, C), lambda i, s: (layer, i, 0))],
        out_specs=pl.BlockSpec((1, tr, C), lambda i, s: (s[0], i, 0)))
    return _pcall(body, name=name, grid_spec=gs, out_shape=SDS((N_CHIPS, R, C), BF16),
                  compiler_params=_params(("arbitrary",)))(_scalar(slot), w3)


class _Stage:
    def __init__(self, ins, out_shapes, aliases, n_sems, copies):
        self.ins, self.out_shapes, self.aliases, self.n_sems, self.copies = list(ins), out_shapes, aliases, n_sems, copies

    def start(self, ins, outs, send_sems, recv_sems):
        for cp in self.copies(ins, outs, send_sems, recv_sems)[0]:
            cp.start()

    def finish(self, ins, outs, send_sems, recv_sems):
        sends, arrivals = self.copies(ins, outs, send_sems, recv_sems)
        for cp in arrivals:
            cp.wait_recv()
        for cp in sends:
            cp.wait_send()


class _SemsFrom:
    def __init__(self, sems, base):
        self.sems, self.base = sems, base

    @property
    def at(self):
        return self

    def __getitem__(self, k):
        return self.sems.at[self.base + k]


def _both(a, b):
    if a is None or b is None:
        return a if b is None else b
    na, nao = len(a.ins), len(a.out_shapes)

    def copies(ins, outs, send_sems, recv_sems):
        sa, aa = a.copies(ins[:na], outs[:nao], send_sems, recv_sems)
        sb, ab = b.copies(ins[na:], outs[nao:], _SemsFrom(send_sems, a.n_sems), _SemsFrom(recv_sems, a.n_sems))
        return sa + sb, aa + ab

    aliases = {**a.aliases, **{na + i: nao + o for i, o in b.aliases.items()}}
    return _Stage(a.ins + b.ins, list(a.out_shapes) + list(b.out_shapes), aliases, a.n_sems + b.n_sems, copies)


def _remote(src, dst, send_sems, recv_sems, k, to):
    return pltpu.make_async_remote_copy(src_ref=src, dst_ref=dst, send_sem=send_sems.at[k], recv_sem=recv_sems.at[k],
                                        device_id=to, device_id_type=MESH)


def _gather_stages(bufs):
    n = len(bufs)
    shapes = [SDS(b.shape, b.dtype) for b in bufs]
    same = {i: i for i in range(n)}

    def over_ici(ins, outs, send_sems, recv_sems):
        x, y, c = _coords()
        blk = lambda w, chip: outs[w].at[chip, _half(c, outs[w].shape[1] // 2), :]
        sends, arrivals = [], []
        for w in range(n):
            for j, (px, py) in enumerate(_other_chips(x, y)):
                sends.append(_remote(blk(w, 2 * x + y), blk(w, 2 * x + y), send_sems, recv_sems, 3 * w + j, (px, py, c)))
                arrivals.append(_remote(blk(w, 2 * px + py), blk(w, 2 * px + py), send_sems, recv_sems, 3 * w + j, (px, py, c)))
        return sends, arrivals

    def to_sibling(ins, outs, send_sems, recv_sems):
        x, y, c = _coords()
        blk = lambda w, chip, half: outs[w].at[chip, _half(half, outs[w].shape[1] // 2), :]
        sends, arrivals = [], []
        for w in range(n):
            for j, (px, py) in enumerate(_other_chips(x, y)):
                k = 2 * px + py
                sends.append(_remote(blk(w, k, c), blk(w, k, c), send_sems, recv_sems, 3 * w + j, (x, y, 1 - c)))
                arrivals.append(_remote(blk(w, k, 1 - c), blk(w, k, 1 - c), send_sems, recv_sems, 3 * w + j, (x, y, 1 - c)))
        return sends, arrivals

    return (lambda b: _Stage(b, shapes, same, 3 * n, over_ici)), (lambda b: _Stage(b, shapes, same, 3 * n, to_sibling))


def _swap_stage(gs_):
    n = len(gs_)

    def copies(ins, outs, send_sems, recv_sems):
        x, y, c = _coords()
        cps = [_remote(ins[w].at[:, _half(1 - c, ins[w].shape[1] // 2), :], outs[w], send_sems, recv_sems, w, (x, y, 1 - c))
               for w in range(n)]
        return cps, cps

    return _Stage(gs_, [SDS((N_CHIPS, g.shape[1] // 2, g.shape[2]), g.dtype) for g in gs_], {}, n, copies)


def _scatter_stage(ps):
    n = len(ps)

    def copies(ins, outs, send_sems, recv_sems):
        x, y, c = _coords()
        cps = [_remote(ins[w].at[2 * px + py], outs[w].at[j], send_sems, recv_sems, 3 * w + j, (px, py, c))
               for w in range(n) for j, (px, py) in enumerate(_other_chips(x, y))]
        return cps, cps

    return _Stage(ps, [SDS((3,) + p.shape[1:], p.dtype) for p in ps], {}, 3 * n, copies)


def _join_stage(bufs):
    n = len(bufs)
    L = bufs[0].shape[0]

    def copies(ins, outs, send_sems, recv_sems):
        x, y, c = _coords()
        blk = lambda w, l, half: outs[w].at[l, _half(half, outs[w].shape[1] // 2), :]
        sends = [_remote(blk(w, l, c), blk(w, l, c), send_sems, recv_sems, L * w + l, (x, y, 1 - c))
                 for w in range(n) for l in range(L)]
        arrivals = [_remote(blk(w, l, 1 - c), blk(w, l, 1 - c), send_sems, recv_sems, L * w + l, (x, y, 1 - c))
                    for w in range(n) for l in range(L)]
        return sends, arrivals

    return _Stage(bufs, [SDS(b.shape, b.dtype) for b in bufs], {i: i for i in range(n)}, L * n, copies)


def _stage_scratch(stage):
    return [pltpu.SemaphoreType.DMA((stage.n_sems,)), pltpu.SemaphoreType.DMA((stage.n_sems,))]


def _run_stage(stage, name):
    n_in, n_out = len(stage.ins), len(stage.out_shapes)

    def body(*refs):
        ins, outs, send_sems, recv_sems = refs[:n_in], refs[n_in:n_in + n_out], refs[n_in + n_out], refs[n_in + n_out + 1]
        stage.start(ins, outs, send_sems, recv_sems)
        stage.finish(ins, outs, send_sems, recv_sems)

    return _pcall(body, name=name, out_shape=stage.out_shapes, in_specs=[HBM_SPEC] * n_in, out_specs=[HBM_SPEC] * n_out,
                  input_output_aliases=stage.aliases, scratch_shapes=_stage_scratch(stage))(*stage.ins)


def _call(body, name, grid, n_prefetch, in_specs, out_specs, out_shape, scratch, operands, semantics, carry=None):
    n_in, n_out, n_sc = len(in_specs), len(out_specs), len(scratch)
    if carry is None:
        gs = pltpu.PrefetchScalarGridSpec(num_scalar_prefetch=n_prefetch, grid=grid, in_specs=in_specs, out_specs=out_specs,
                                          scratch_shapes=scratch)
        res = _pcall(body, name=name, grid_spec=gs, out_shape=out_shape, compiler_params=_params(semantics))(*operands)
        return list(res), []
    s_in, s_out = len(carry.ins), len(carry.out_shapes)

    def carrying(*refs):
        o = n_prefetch
        pre, ins = refs[:o], refs[o:o + n_in]
        o += n_in
        sins = refs[o:o + s_in]
        o += s_in
        outs = refs[o:o + n_out]
        o += n_out
        souts = refs[o:o + s_out]
        o += s_out
        sc, send_sems, recv_sems = refs[o:o + n_sc], refs[o + n_sc], refs[o + n_sc + 1]
        first = functools.reduce(jnp.logical_and, [pl.program_id(a) == 0 for a in range(len(grid))])
        last = functools.reduce(jnp.logical_and, [pl.program_id(a) == g - 1 for a, g in enumerate(grid)])

        @pl.when(first)
        def _():
            carry.start(sins, souts, send_sems, recv_sems)

        body(*pre, *ins, *outs, *sc)

        @pl.when(last)
        def _():
            carry.finish(sins, souts, send_sems, recv_sems)

    gs = pltpu.PrefetchScalarGridSpec(
        num_scalar_prefetch=n_prefetch, grid=grid, in_specs=list(in_specs) + [HBM_SPEC] * s_in,
        out_specs=list(out_specs) + [HBM_SPEC] * s_out, scratch_shapes=list(scratch) + _stage_scratch(carry))
    aliases = {n_prefetch + n_in + a: n_out + b for a, b in carry.aliases.items()}
    res = _pcall(carrying, name=name, grid_spec=gs, out_shape=list(out_shape) + list(carry.out_shapes),
                 input_output_aliases=aliases, compiler_params=_params(("arbitrary",) * len(grid)))(*operands, *carry.ins)
    return list(res[:n_out]), list(res[n_out:])


def _all_reduce_small(v):
    n = v.shape[0]

    def body(v_ref, out_ref, slots, send_sems, recv_sems):
        x, y, c = _coords()
        me = 4 * x + 2 * y + c
        cps = []
        for r in range(1, 8):
            t = (me + r) % 8
            cp = pltpu.make_async_remote_copy(src_ref=v_ref, dst_ref=slots.at[me], send_sem=send_sems.at[r - 1],
                                              recv_sem=recv_sems.at[me], device_id=(t // 4, (t // 2) % 2, t % 2),
                                              device_id_type=MESH)
            cp.start()
            cps.append(cp)
        slots[me] = v_ref[...]
        for r in range(1, 8):
            s = (me + r) % 8
            pltpu.make_async_remote_copy(src_ref=v_ref, dst_ref=slots.at[s], send_sem=send_sems.at[r - 1],
                                         recv_sem=recv_sems.at[s], device_id=(x, y, c), device_id_type=MESH).wait_recv()
        for cp in cps:
            cp.wait_send()
        acc = slots[0]
        for d in range(1, 8):
            acc = acc + slots[d]
        out_ref[...] = acc

    return _pcall(body, name="all_reduce_small", out_shape=SDS((n, LANES), F32),
                  in_specs=[pl.BlockSpec(memory_space=pltpu.VMEM)], out_specs=pl.BlockSpec(memory_space=pltpu.VMEM),
                  scratch_shapes=[pltpu.VMEM((8, n, LANES), F32), pltpu.SemaphoreType.DMA((7,)), pltpu.SemaphoreType.DMA((8,))],
                  compiler_params=pltpu.CompilerParams(vmem_limit_bytes=VMEM_LIMIT))(v)


def _add_half(g, recv, c, name):
    _, R, C = g.shape
    rows = R // 2
    tr = _rows_tile(rows, C, 1 << 20)
    nb = rows // tr

    def body(s_ref, g_ref, r_ref, o32_ref, o16_ref):
        s = g_ref[...] + r_ref[...]
        o32_ref[...] = s
        o16_ref[...] = s.astype(BF16)

    blk = lambda k, i, s: (k, i, 0)
    gs = pltpu.PrefetchScalarGridSpec(
        num_scalar_prefetch=1, grid=(N_CHIPS, nb),
        in_specs=[pl.BlockSpec((1, tr, C), lambda k, i, s: (k, s[0] * nb + i, 0)), pl.BlockSpec((1, tr, C), blk)],
        out_specs=[pl.BlockSpec((1, tr, C), blk), pl.BlockSpec((1, tr, C), blk)])
    return _pcall(body, name=name, grid_spec=gs, out_shape=[SDS((N_CHIPS, rows, C), F32), SDS((N_CHIPS, rows, C), BF16)],
                  compiler_params=_params(("arbitrary", "arbitrary")))(_scalar(c), g, recv)


def _sum_into(p32, arrived, chip, c, layer, n_layers, prev, name):
    _, rows, C = p32.shape
    tr = _rows_tile(rows, C, 1 << 20)
    nb = rows // tr

    def body(chip_ref, c_ref, p_ref, a_ref, *rest):
        o_ref = rest[-1]
        o_ref[0] = ((p_ref[0] + a_ref[0].astype(F32)) + a_ref[1].astype(F32)) + a_ref[2].astype(F32)

    in_specs = [pl.BlockSpec((1, tr, C), lambda i, chip_ref, c_ref: (chip_ref[0], i, 0)),
                pl.BlockSpec((3, tr, C), lambda i, chip_ref, c_ref: (0, i, 0))]
    ins = [p32, arrived]
    aliases = {}
    if prev is not None:
        in_specs.append(pl.BlockSpec(memory_space=pl.ANY))
        ins.append(prev)
        aliases = {4: 0}
    gs = pltpu.PrefetchScalarGridSpec(
        num_scalar_prefetch=2, grid=(nb,), in_specs=in_specs,
        out_specs=pl.BlockSpec((1, tr, C), lambda i, chip_ref, c_ref: (layer, c_ref[0] * nb + i, 0)))
    return _pcall(body, name=name, grid_spec=gs, out_shape=SDS((n_layers, 2 * rows, C), F32), input_output_aliases=aliases,
                  compiler_params=_params(("arbitrary",)))(_scalar(chip), _scalar(c), *ins)


def _h_layout(D):
    G = N_BRANCHES * D
    off, o = {}, 0
    for name, w in [("g", G), ("qa", A_Q), ("cq", MLA_Q_RANK), ("ckv", MLA_KV_RANK), ("hu", SGU_WIDTH), ("hv", SGU_WIDTH),
                    ("ka", A_KV), ("va", A_KV), ("kr", LANES)]:
        assert o % w == 0, (name, o, w)
        off[name] = (o, w)
        o += w
    off["total"] = -(-o // 512) * 512
    return off


def _perm_w_in(w, lay):
    s = np.cumsum([0, A_Q, A_KV, A_KV, MLA_Q_RANK, MLA_KV_RANK, MLA_ROPE, SGU_WIDTH, SGU_WIDTH])
    qa, ka, va, cq, ckv, kr, hu, hv = [w[:, s[i]:s[i + 1]] for i in range(8)]
    g = w[:, s[8]:]
    pad = jnp.zeros((w.shape[0], lay["total"] - lay["kr"][0] - MLA_ROPE), w.dtype)
    return jnp.concatenate([g, qa, cq, ckv, hu, hv, ka, va, kr, pad], axis=1)


def _unperm_w_in(wp, lay, D):
    take = lambda n, width=None: wp[:, lay[n][0]:lay[n][0] + (width or lay[n][1])]
    return jnp.concatenate([take("qa"), take("ka"), take("va"), take("cq"), take("ckv"), take("kr", MLA_ROPE), take("hu"),
                            take("hv"), take("g")], axis=1)


def _perm_w_uq(w):
    r = w.shape[0]
    w3 = w.reshape(r, MLA_HEADS, MLA_NOPE + MLA_ROPE)
    nope = w3[:, :, :MLA_NOPE].reshape(r, MLA_HEADS * MLA_NOPE)
    rope = jnp.pad(w3[:, :, MLA_NOPE:], ((0, 0), (0, 0), (0, LANES - MLA_ROPE))).reshape(r, MLA_HEADS * LANES)
    return jnp.concatenate([nope, rope], axis=1)


def _unperm_w_uq(wp):
    r = wp.shape[0]
    nope = wp[:, :MLA_HEADS * MLA_NOPE].reshape(r, MLA_HEADS, MLA_NOPE)
    rope = wp[:, MLA_HEADS * MLA_NOPE:].reshape(r, MLA_HEADS, LANES)[:, :, :MLA_ROPE]
    return jnp.concatenate([nope, rope], axis=2).reshape(r, MLA_HEADS * (MLA_NOPE + MLA_ROPE))


def _perm_w_ukv(w):
    r = w.shape[0]
    w3 = w.reshape(r, MLA_HEADS, MLA_NOPE + MLA_V)
    return jnp.concatenate([w3[:, :, :MLA_NOPE].reshape(r, -1), w3[:, :, MLA_NOPE:].reshape(r, -1)], axis=1)


def _unperm_w_ukv(wp):
    r = wp.shape[0]
    k = wp[:, :MLA_HEADS * MLA_NOPE].reshape(r, MLA_HEADS, MLA_NOPE)
    v = wp[:, MLA_HEADS * MLA_NOPE:].reshape(r, MLA_HEADS, MLA_V)
    return jnp.concatenate([k, v], axis=2).reshape(r, -1)


def _col_chunks(g):
    r, c4 = g.shape
    return jnp.transpose(g.reshape(r, N_CHIPS, c4 // N_CHIPS), (1, 0, 2))


def kernel(x, positions, w_in, b_gate, sinks, q_norm_g, kv_norm_g, w_uq, w_ukv, sgu_ln_g, sgu_ln_b, sgu_w, sgu_b, w_proj_a, w_proj_b, w_proj_c, w_o, ln1_g, ln1_b, w_up, conv_w, conv_b, w_down, ln2_g, ln2_b, loss_target, m_w_in, m_b_gate, m_sinks, m_q_norm_g, m_kv_norm_g, m_w_uq, m_w_ukv, m_sgu_ln_g, m_sgu_ln_b, m_sgu_w, m_sgu_b, m_w_proj_a, m_w_proj_b, m_w_proj_c, m_w_o, m_ln1_g, m_ln1_b, m_w_up, m_conv_w, m_conv_b, m_w_down, m_ln2_g, m_ln2_b, v_w_in, v_b_gate, v_sinks, v_q_norm_g, v_kv_norm_g, v_w_uq, v_w_ukv, v_sgu_ln_g, v_sgu_ln_b, v_sgu_w, v_sgu_b, v_w_proj_a, v_w_proj_b, v_w_proj_c, v_w_o, v_ln1_g, v_ln1_b, v_w_up, v_conv_w, v_conv_b, v_w_down, v_ln2_g, v_ln2_b):
    a = locals()
    W = {k: a[k] for k in WEIGHTS}
    Mo = {k: a["m_" + k] for k in WEIGHTS}
    Vo = {k: a["v_" + k] for k in WEIGHTS}
    S, D = x.shape[1], x.shape[2]
    FF2 = w_up.shape[2] * N_CHIPS
    FF = FF2 // 2
    L = DEPTH
    lay = _h_layout(D)
    NP = lay["total"]
    cx, cy, cc = _coords()
    chip = 2 * cx + cy
    T = _tile(S, 512)
    TM = _tile(S, 256, 16)
    TMW = _tile(S, 64, 16)

    shards = {k: tuple(W[k].shape) for k in BIG}
    full = {k: [None] * L for k in BIG}
    own = {(l, k): _cast_into_slot(W[k], l, chip, f"cast_{k}_l{l}") for l in range(L) for k in BIG}

    def over_ici(pairs):
        return _gather_stages([own[p_] for p_ in pairs])[0]([own[p_] for p_ in pairs]) if pairs else None

    def landed(pairs, arrived, tag):
        if not pairs:
            return
        gathered = _run_stage(_gather_stages(arrived)[1](arrived), f"gather_sibling_{tag}")
        for (l_, k), g in zip(pairs, gathered):
            _, r, c_ = shards[k]
            full[k][l_] = g.reshape(N_CHIPS * r, c_) if k in ROW_SHARDED else jnp.transpose(g, (1, 0, 2)).reshape(r, N_CHIPS * c_)

    MIX = ["w_uq", "w_ukv", "w_proj_a", "w_proj_b", "w_proj_c", "w_o"]
    nxt = lambda l, names: [(l + 1, k) for k in names] if l + 1 < L else []
    first = [(0, "w_in")]
    landed(first, _run_stage(over_ici(first), "gather_ici_w_in_l0"), "w_in_l0")

    small_sharded_full = {k: tuple(W[k].shape[:-1]) + (W[k].shape[-1] * N_CHIPS,) for k in SMALL_SHARDED}
    placed = []
    for k in ("b_gate", "conv_w"):
        z = jnp.zeros(small_sharded_full[k], F32)
        z = lax.dynamic_update_slice_in_dim(z, W[k], chip * W[k].shape[-1], axis=-1)
        placed.append(jnp.where(cc == 0, z, 0.0).reshape(-1))
    pv = jnp.concatenate(placed)
    n_pv = pv.shape[0]
    pv = jnp.pad(pv, (0, -n_pv % (8 * LANES))).reshape(-1, LANES)
    pv = _all_reduce_small(pv).reshape(-1)
    nb_ = int(np.prod(small_sharded_full["b_gate"]))
    b_gate_full = pv[:nb_].reshape(small_sharded_full["b_gate"])
    conv_w_full = pv[nb_:n_pv].reshape(small_sharded_full["conv_w"])

    inv_freq = ROPE_THETA ** (-jnp.arange(0, MLA_ROPE, 2, dtype=F32) / MLA_ROPE)
    ang = positions[0].astype(F32)[:, None] * inv_freq
    cos, sin = jnp.cos(ang), jnp.sin(ang)
    cos_t = jnp.concatenate([cos, cos, jnp.ones((S, LANES - MLA_ROPE), F32)], axis=1)
    sin_t = jnp.concatenate([sin, sin, jnp.zeros((S, LANES - MLA_ROPE), F32)], axis=1)

    row = lambda v: v.reshape(1, -1)
    cb = lambda name: lay[name][0] // lay[name][1]

    xs = x[0]
    saved = []
    for l in range(L):
        p = dict(
            w_in=_perm_w_in(full["w_in"][l], lay),
            sinks=row(sinks[l]), qg=row(q_norm_g[l]), kvg=row(kv_norm_g[l]), sg=row(sgu_ln_g[l]), sb=row(sgu_ln_b[l]),
            sw=sgu_w[l], sb3=sgu_b[l].reshape(SGU_GROUPS, SGU_CHUNK, 1),
            bg=b_gate_full[l], l1g=row(ln1_g[l]), l1b=row(ln1_b[l]), cw=conv_w_full[l], cbias=row(conv_b[l]),
            l2g=row(ln2_g[l]), l2b=row(ln2_b[l]))
        if l == 0:
            def fn_cast(i, rows, ps):
                return [rows[0]], []
            (xb,), _ = _rowwise(fn_cast, [_whole(xs)], [], [(D, BF16)], [], TM, "cast_x")
        own_mix = [(l, k) for k in MIX] if l == 0 else []
        own_up = [(l, "w_up")] if l == 0 else []
        own_down = [(l, "w_down")] if l == 0 else []
        if own_mix:
            h, arrived = _mm(xb, p["w_in"], "nn", BF16, "mm_h", carry=over_ici(own_mix))
            landed(own_mix, arrived, f"mix_l{l}")
        else:
            h = _mm(xb, p["w_in"], "nn", BF16, "mm_h")
        (y_a,), arrived = _swa_fwd(h, cb("qa"), cb("ka"), cb("va"), p["sinks"], carry=over_ici(own_up))
        landed(own_up, arrived, f"up_l{l}")
        p.update(w_uq=_perm_w_uq(full["w_uq"][l]), w_ukv=_perm_w_ukv(full["w_ukv"][l]), w_pa=full["w_proj_a"][l],
                 w_pb=full["w_proj_b"][l], w_pc=full["w_proj_c"][l], w_o=full["w_o"][l])
        def fn_rms(i, rows, ps):
            return [_rms_norm(rows[0].astype(F32), ps[0]), _rms_norm(rows[1].astype(F32), ps[1])], []
        (cqn, ckvn), _ = _rowwise(fn_rms, [(h, MLA_Q_RANK, cb("cq")), (h, MLA_KV_RANK, cb("ckv"))], [p["qg"], p["kvg"]],
                                  [(MLA_Q_RANK, BF16), (MLA_KV_RANK, BF16)], [], TM, "mla_rms")
        q_full = _mm(cqn, p["w_uq"], "nn", BF16, "mm_q")
        kv = _mm(ckvn, p["w_ukv"], "nn", BF16, "mm_kv")
        qr, kr = _rope_call(q_full, MLA_HEADS * LANES, 1, h, LANES, cb("kr"), cos_t, sin_t, 1.0, TM, "rope_fwd")
        behind_mla = own_down + nxt(l, ["w_in"] + MIX)
        (y_b, lse), arrived = _mla_fwd(q_full, qr, kv, kr, T, carry=over_ici(behind_mla))
        landed(behind_mla, arrived, f"behind_mla_l{l}")
        y_c = _sgu_fwd(h, cb("hu"), cb("hv"), p["sg"], p["sb"], p["sw"], p["sb3"])
        pa = _mm(y_a, p["w_pa"], "nn", F32, "mm_pa")
        pb = _mm(y_b, p["w_pb"], "nn", F32, "mm_pb")
        pc = _mm(y_c, p["w_pc"], "nn", F32, "mm_pc")

        def merge_math(pa_, pb_, pc_, g_, b0, b1, b2):
            out = 0.0
            for br, (pp, bb) in enumerate(zip((pa_, pb_, pc_), (b0, b1, b2))):
                gate = jax.nn.sigmoid(g_[:, br * D:(br + 1) * D].astype(F32) + bb)
                out = out + gate * pp
            return out

        def fn_merge(i, rows, ps):
            bgv = ps[0]
            return [merge_math(rows[0], rows[1], rows[2], rows[3], bgv[0:1], bgv[1:2], bgv[2:3])], []
        (merged,), _ = _rowwise(fn_merge, [_whole(pa), _whole(pb), _whole(pc), (h, N_BRANCHES * D, cb("g"))], [p["bg"]],
                                [(D, BF16)], [], TMW, "merge_fwd")
        o = _mm(merged, p["w_o"], "nn", F32, "mm_o")

        def ln_res_math(x_, o_, g_, b_):
            return _layer_norm(DN_ALPHA * x_ + o_, g_, b_)

        def fn_ln(i, rows, ps):
            y = ln_res_math(rows[0], rows[1], ps[0], ps[1])
            return [y, y], []
        (x1, x1b), _ = _rowwise(fn_ln, [_whole(xs), _whole(o)], [p["l1g"], p["l1b"]], [(D, F32), (D, BF16)], [], TM, "ln1_fwd")
        p.update(w_up=full["w_up"][l], w_down=full["w_down"][l])
        if nxt(l, ["w_up"]):
            up, arrived = _mm(x1b, p["w_up"], "nn", BF16, "mm_up", carry=over_ici(nxt(l, ["w_up"])))
            landed(nxt(l, ["w_up"]), arrived, f"up_l{l + 1}")
        else:
            up = _mm(x1b, p["w_up"], "nn", BF16, "mm_up")
        cv_ = _conv_fwd(up, p["cw"], p["cbias"])

        def glu_math(cg, cvv):
            return jax.nn.silu(cg.astype(F32)) * cvv.astype(F32)

        def fn_glu(i, rows, ps):
            return [glu_math(rows[0], rows[1])], []
        (act,), _ = _rowwise(fn_glu, [(cv_, FF, 0), (cv_, FF, 1)], [], [(FF, BF16)], [], TMW, "glu_fwd")
        if nxt(l, ["w_down"]):
            dn, arrived = _mm(act, p["w_down"], "nn", F32, "mm_down", carry=over_ici(nxt(l, ["w_down"])))
            landed(nxt(l, ["w_down"]), arrived, f"down_l{l + 1}")
        else:
            dn = _mm(act, p["w_down"], "nn", F32, "mm_down")
        (x2, x2b), _ = _rowwise(fn_ln, [_whole(x1), _whole(dn)], [p["l2g"], p["l2b"]], [(D, F32), (D, BF16)], [], TM, "ln2_fwd")
        saved.append(dict(p=p, x0=xs, x0b=xb, h=h, y_a=y_a, cqn=cqn, ckvn=ckvn, q_full=q_full, kv=kv, qr=qr, kr=kr, y_b=y_b,
                          lse=lse, y_c=y_c, pa=pa, pb=pb, pc=pc, merged=merged, o=o, x1=x1, x1b=x1b, up=up, cv=cv_, act=act,
                          dn=dn))
        xs, xb = x2, x2b

    def fn_loss(i, rows, ps):
        diff = rows[0] - rows[1]
        part = jnp.sum(jnp.mean(jnp.square(diff), axis=-1, keepdims=True), axis=0, keepdims=True)
        return [diff * (1.0 / D)], [jnp.broadcast_to(part, (8, LANES))]
    (dx,), (loss_acc,) = _rowwise(fn_loss, [_whole(xs), _whole(loss_target[0])], [], [(D, F32)], [(8, LANES)], TM, "loss")
    loss = lax.psum(0.5 * loss_acc[0, 0], ("x", "y", "c"))

    gbig = {k: [None] * L for k in BIG}
    gsmall = {k: [None] * L for k in SMALL}
    reduced = {}
    pending = None
    for l in reversed(range(L)):
        s = saved[l]
        p = s["p"]

        def fn_ln_bwd(i, rows, ps):
            _, vjp = jax.vjp(ln_res_math, rows[0], rows[1], ps[0], ps[1])
            dx_, do_, dg_, db_ = vjp(rows[2])
            return [dx_, do_], [dg_, db_]
        (dx1_res, ddn), (g_l2g, g_l2b) = _rowwise(fn_ln_bwd, [_whole(s["x1"]), _whole(s["dn"]), _whole(dx)], [p["l2g"], p["l2b"]],
                                                  [(D, F32), (D, BF16)], [(1, D), (1, D)], TM, "ln2_bwd")
        gsmall["ln2_g"][l], gsmall["ln2_b"][l] = g_l2g, g_l2b
        row_chunks = lambda g: g.reshape(N_CHIPS, g.shape[0] // N_CHIPS, g.shape[1])
        gbig["w_down"][l] = row_chunks(_mm(s["act"], ddn, "tn", F32, "mm_dw_down"))
        dact = _mm(ddn, p["w_down"], "nt", BF16, "mm_dact")

        def fn_glu_bwd(i, rows, ps):
            _, vjp = jax.vjp(glu_math, rows[0], rows[1])
            dcg, dcv = vjp(rows[2].astype(F32))
            return [jnp.concatenate([dcg, dcv], axis=1)], []
        (dc,), _ = _rowwise(fn_glu_bwd, [(s["cv"], FF, 0), (s["cv"], FF, 1), _whole(dact)], [], [(FF2, BF16)], [], TMW, "glu_bwd")
        dup, g_cw, g_cb = _conv_bwd(s["up"], dc, p["cw"])
        gsmall["conv_w"][l], gsmall["conv_b"][l] = g_cw, g_cb
        if pending is None:
            gbig["w_up"][l] = _mm(s["x1b"], dup, "tn", F32, "mm_dw_up", col_chunks=N_CHIPS, tm=2048)
        else:
            gbig["w_up"][l], from_sibling = _mm(s["x1b"], dup, "tn", F32, "mm_dw_up", col_chunks=N_CHIPS, tm=2048,
                                                carry=_swap_stage(pending))
            partial = [_add_half(g, r, cc, f"rs_add_{k}_l{l + 1}") for k, g, r in zip(BIG, pending, from_sibling)]
        dx1 = _mm(dup, p["w_up"], "nt", F32, "mm_dx1", add=dx1_res)
        (dx0_res, do_), (g_l1g, g_l1b) = _rowwise(fn_ln_bwd, [_whole(s["x0"]), _whole(s["o"]), _whole(dx1)], [p["l1g"], p["l1b"]],
                                                  [(D, F32), (D, BF16)], [(1, D), (1, D)], TM, "ln1_bwd")
        gsmall["ln1_g"][l], gsmall["ln1_b"][l] = g_l1g, g_l1b
        gbig["w_o"][l] = row_chunks(_mm(s["merged"], do_, "tn", F32, "mm_dw_o"))
        dmerged = _mm(do_, p["w_o"], "nt", F32, "mm_dmerged")

        def fn_merge_bwd(i, rows, ps):
            bgv = ps[0]
            _, vjp = jax.vjp(merge_math, rows[0], rows[1], rows[2], rows[3], bgv[0:1], bgv[1:2], bgv[2:3])
            dpa, dpb, dpc, dg_, db0, db1, db2 = vjp(rows[4])
            return [dpa, dpb, dpc, dg_], [db0, db1, db2]
        (dpa, dpb, dpc, dgl), (db0, db1, db2) = _rowwise(
            fn_merge_bwd, [_whole(s["pa"]), _whole(s["pb"]), _whole(s["pc"]), (s["h"], N_BRANCHES * D, cb("g")), _whole(dmerged)],
            [p["bg"]], [(D, BF16), (D, BF16), (D, BF16), (N_BRANCHES * D, BF16)], [(1, D)] * 3, TMW, "merge_bwd")
        gsmall["b_gate"][l] = jnp.concatenate([db0, db1, db2], axis=0)
        gbig["w_proj_a"][l] = _mm(s["y_a"], dpa, "tn", F32, "mm_dw_pa", col_chunks=N_CHIPS)
        gbig["w_proj_b"][l] = row_chunks(_mm(s["y_b"], dpb, "tn", F32, "mm_dw_pb"))
        gbig["w_proj_c"][l] = _mm(s["y_c"], dpc, "tn", F32, "mm_dw_pc", col_chunks=N_CHIPS)
        dy_a = _mm(dpa, p["w_pa"], "nt", BF16, "mm_dy_a")
        dy_b = _mm(dpb, p["w_pb"], "nt", BF16, "mm_dy_b")
        dy_c = _mm(dpc, p["w_pc"], "nt", BF16, "mm_dy_c")
        dh_c, g_sg, g_sb, g_sw, g_sb3 = _sgu_bwd(s["h"], cb("hu"), cb("hv"), dy_c, p["sg"], p["sb"], p["sw"], p["sb3"])
        gsmall["sgu_ln_g"][l], gsmall["sgu_ln_b"][l], gsmall["sgu_w"][l] = g_sg, g_sb, g_sw
        gsmall["sgu_b"][l] = g_sb3.reshape(SGU_GROUPS, SGU_CHUNK)
        part_x = ["w_up", "w_down"]
        part_y = [k for k in BIG if k not in part_x]
        scatter_of = lambda names: None if pending is None else _scatter_stage([partial[BIG.index(k)][1] for k in names])
        hide_early = l == 0
        early = [gbig[k][l] for k in EARLY]
        n_early = len(EARLY) if hide_early else 0
        (dqa, dka, dva, g_sinks), landed_ = _swa_bwd(s["h"], cb("qa"), cb("ka"), cb("va"), p["sinks"], dy_a,
                                                     carry=_both(_swap_stage(early) if hide_early else None, scatter_of(part_x)))
        early_from_sibling, arrived_x = landed_[:n_early], landed_[n_early:]
        gsmall["sinks"][l] = g_sinks
        if hide_early:
            early_partial = [_add_half(g, r, cc, f"rs_add_{k}_l{l}") for k, g, r in zip(EARLY, early, early_from_sibling)]
        delta = _mla_delta(dy_b, s["y_b"], T)
        (dqn, dqr), arrived_y = _mla_bwd_dq(s["q_full"], s["qr"], s["kv"], s["kr"], dy_b, s["lse"], delta, T,
                                            carry=scatter_of(part_y))
        if pending is not None:
            arrived = {**dict(zip(part_x, arrived_x)), **dict(zip(part_y, arrived_y))}
            for k, (p32, _) in zip(BIG, partial):
                reduced[k] = _sum_into(p32, arrived[k], chip, cc, l + 1, L, reduced.get(k), f"rs_sum_{k}_l{l + 1}")
        (dkn, dv, dkr_heads), early_arrived = _mla_bwd_dkv(
            s["q_full"], s["qr"], s["kv"], s["kr"], dy_b, s["lse"], delta, T,
            carry=_scatter_stage([p16 for _, p16 in early_partial]) if hide_early else None)
        if hide_early:
            for k, (p32, _), arr in zip(EARLY, early_partial, early_arrived):
                reduced[k] = _sum_into(p32, arr, chip, cc, l, L, reduced.get(k), f"rs_sum_{k}_l{l}")
        dkr = _sum_heads(dkr_heads, TM)
        dqr_raw, dkr_raw = _rope_call(dqr, MLA_HEADS * LANES, 0, dkr, LANES, 0, cos_t, sin_t, -1.0, TM, "rope_bwd")
        dq_full = jnp.concatenate([dqn, dqr_raw], axis=1)
        dkv = jnp.concatenate([dkn, dv], axis=1)
        gbig["w_uq"][l] = _col_chunks(_unperm_w_uq(_mm(s["cqn"], dq_full, "tn", F32, "mm_dw_uq")))
        gbig["w_ukv"][l] = _col_chunks(_unperm_w_ukv(_mm(s["ckvn"], dkv, "tn", F32, "mm_dw_ukv")))
        dcqn = _mm(dq_full, p["w_uq"], "nt", F32, "mm_dcqn")
        dckvn = _mm(dkv, p["w_ukv"], "nt", F32, "mm_dckvn")

        def fn_rms_bwd(i, rows, ps):
            _, vjp1 = jax.vjp(lambda c_, g_: _rms_norm(c_.astype(F32), g_), rows[0], ps[0])
            _, vjp2 = jax.vjp(lambda c_, g_: _rms_norm(c_.astype(F32), g_), rows[1], ps[1])
            d1, dg1 = vjp1(rows[2])
            d2, dg2 = vjp2(rows[3])
            return [d1, d2], [dg1, dg2]
        (dcq, dckv), (g_qg, g_kvg) = _rowwise(
            fn_rms_bwd, [(s["h"], MLA_Q_RANK, cb("cq")), (s["h"], MLA_KV_RANK, cb("ckv")), _whole(dcqn), _whole(dckvn)],
            [p["qg"], p["kvg"]], [(MLA_Q_RANK, BF16), (MLA_KV_RANK, BF16)], [(1, MLA_Q_RANK), (1, MLA_KV_RANK)], TM, "mla_rms_bwd")
        gsmall["q_norm_g"][l], gsmall["kv_norm_g"][l] = g_qg, g_kvg
        tail = jnp.zeros((S, NP - lay["kr"][0] - LANES), BF16)
        dh = jnp.concatenate([dgl, dqa, dcq, dckv, dh_c, dka, dva, dkr_raw, tail], axis=1)
        gbig["w_in"][l] = _col_chunks(_unperm_w_in(_mm(s["x0b"], dh, "tn", F32, "mm_dw_in"), lay, D))
        dx = _mm(dh, p["w_in"], "nt", F32, "mm_dx0", add=dx0_res)

        pending = [gbig[k][l] for k in BIG]

    late = [gbig[k][0] for k in LATE]
    from_sibling = _run_stage(_swap_stage(late), "rs_swap_halves_l0")
    partial = [_add_half(g, r, cc, f"rs_add_{k}_l0") for k, g, r in zip(LATE, late, from_sibling)]
    arrived = _run_stage(_scatter_stage([p16 for _, p16 in partial]), "rs_scatter_chips_l0")
    for k, (p32, _), arr in zip(LATE, partial, arrived):
        reduced[k] = _sum_into(p32, arr, chip, cc, 0, L, reduced.get(k), f"rs_sum_{k}_l0")

    grad_x = dx.reshape(x.shape)
    g_big = dict(zip(BIG, _run_stage(_join_stage([reduced[k] for k in BIG]), "rs_join_halves")))

    small_shapes = {k: (small_sharded_full[k] if k in SMALL_SHARDED else tuple(W[k].shape)) for k in SMALL}
    sv = jnp.concatenate([jnp.stack(gsmall[k]).reshape(-1) for k in SMALL])
    n_sv = sv.shape[0]
    sv = jnp.pad(sv, (0, -n_sv % (8 * LANES))).reshape(-1, LANES)
    sv = _all_reduce_small(sv).reshape(-1)
    g_small, o_ = {}, 0
    for k in SMALL:
        n = int(np.prod(small_shapes[k]))
        g = sv[o_:o_ + n].reshape(small_shapes[k])
        if k in SMALL_SHARDED:
            g = lax.dynamic_slice_in_dim(g, chip * W[k].shape[-1], W[k].shape[-1], axis=-1)
        g_small[k] = g
        o_ += n

    delta, new_m, new_v = {}, {}, {}
    for k in BIG:
        delta[k], new_m[k], new_v[k] = _adamw(W[k], g_big[k], Mo[k], Vo[k], "adamw_" + k)
    pack = lambda t: jnp.concatenate([t[k].reshape(-1) for k in SMALL])
    n_small = sum(int(np.prod(W[k].shape)) for k in SMALL)
    pad2 = lambda t: jnp.pad(t, (0, -n_small % (8 * LANES))).reshape(1, -1, LANES)
    d_, m_, v_ = _adamw(pad2(pack(W)), pad2(pack(g_small)), pad2(pack(Mo)), pad2(pack(Vo)), "adamw_small")
    o_ = 0
    for k in SMALL:
        n = int(np.prod(W[k].shape))
        take = lambda t: t.reshape(-1)[o_:o_ + n].reshape(W[k].shape)
        delta[k], new_m[k], new_v[k] = take(d_), take(m_), take(v_)
        o_ += n

    grads = {**g_big, **g_small}
    return (loss, grad_x, *[grads[k] for k in WEIGHTS], *[delta[k] for k in WEIGHTS], *[new_m[k] for k in WEIGHTS],
            *[new_v[k] for k in WEIGHTS])
```

```python
import functools
import math

import jax
import jax.numpy as jnp
import numpy as np
from jax import lax
from jax.experimental import pallas as pl
from jax.experimental.pallas import tpu as pltpu

F32, BF16 = jnp.float32, jnp.bfloat16
SDS = jax.ShapeDtypeStruct
MESH = pl.DeviceIdType.MESH

SWA_Q_HEADS, SWA_KV_HEADS, SWA_HEAD_DIM, SWA_BLOCK = 16, 2, 64, 128
MLA_HEADS, MLA_NOPE, MLA_ROPE, MLA_V = 16, 128, 64, 128
MLA_Q_RANK, MLA_KV_RANK = 512, 512
ROPE_THETA = 10000.0
SGU_GROUPS, SGU_GROUP_DIM, SGU_CHUNK = 8, 128, 128
SGU_WIDTH = SGU_GROUPS * SGU_GROUP_DIM
A_Q = SWA_Q_HEADS * SWA_HEAD_DIM
A_KV = SWA_KV_HEADS * SWA_HEAD_DIM
N_BRANCHES = 3
DEPTH = 2
EPS = 1e-5
MASK_VALUE = -1e30
DN_ALPHA = (2 * DEPTH) ** 0.25
ADAM_LR, ADAM_B1, ADAM_B2, ADAM_EPS, ADAM_WD, ADAM_STEP = 0.001, 0.9, 0.999, 1e-08, 0.01, 10
N_CHIPS = 4

LANES = 128
VMEM_LIMIT = 48 * 1024 * 1024

BIG = ["w_in", "w_uq", "w_ukv", "w_proj_a", "w_proj_b", "w_proj_c", "w_o", "w_up", "w_down"]
ROW_SHARDED = {"w_proj_b", "w_o", "w_down"}
LATE = ["w_in", "w_uq", "w_ukv"]
EARLY = [k for k in BIG if k not in LATE]
SMALL = ["b_gate", "sinks", "q_norm_g", "kv_norm_g", "sgu_ln_g", "sgu_ln_b", "sgu_w", "sgu_b", "ln1_g", "ln1_b",
         "conv_w", "conv_b", "ln2_g", "ln2_b"]
SMALL_SHARDED = {"b_gate", "conv_w"}
WEIGHTS = ["w_in", "b_gate", "sinks", "q_norm_g", "kv_norm_g", "w_uq", "w_ukv", "sgu_ln_g", "sgu_ln_b", "sgu_w", "sgu_b",
           "w_proj_a", "w_proj_b", "w_proj_c", "w_o", "ln1_g", "ln1_b", "w_up", "conv_w", "conv_b", "w_down", "ln2_g", "ln2_b"]


def _pcall(body, **kw):
    return pl.pallas_call(body, **kw)


def _params(sem=None):
    return pltpu.CompilerParams(dimension_semantics=sem, vmem_limit_bytes=VMEM_LIMIT)


def _tile(dim, pref, align=LANES):
    t = (min(pref, dim) // align) * align
    while t >= align:
        if dim % t == 0:
            return t
        t -= align
    return dim


def _mm(a, b, mode, out_dtype, name, add=None, tm=1024, tn=512, tk=2048, col_chunks=1, carry=None):
    if mode == "nn":
        (M, K), (K2, N) = a.shape, b.shape
    elif mode == "nt":
        (M, K), (N, K2) = a.shape, b.shape
    else:
        (K, M), (K2, N) = a.shape, b.shape
    assert K == K2, (a.shape, b.shape, mode)
    assert N % col_chunks == 0
    tm, tn, tk = _tile(M, tm), _tile(N // col_chunks, tn), _tile(K, tk)
    assert (N // col_chunks) % tn == 0
    per_chunk = (N // col_chunks) // tn
    nk = K // tk
    if mode == "tn":
        a_spec = pl.BlockSpec((tk, tm), lambda i, j, k: (k, i))
    else:
        a_spec = pl.BlockSpec((tm, tk), lambda i, j, k: (i, k))
    if mode == "nt":
        b_spec = pl.BlockSpec((tn, tk), lambda i, j, k: (j, k))
    else:
        b_spec = pl.BlockSpec((tk, tn), lambda i, j, k: (k, j))
    dn = {"nn": (((1,), (0,)), ((), ())), "nt": (((1,), (1,)), ((), ())), "tn": (((0,), (0,)), ((), ()))}[mode]
    chunked = col_chunks > 1
    if chunked:
        assert add is None
        o_spec = pl.BlockSpec((1, tm, tn), lambda i, j, k: (lax.div(j, per_chunk), i, lax.rem(j, per_chunk)))
        out_shape = SDS((col_chunks, M, N // col_chunks), out_dtype)
    else:
        o_spec = pl.BlockSpec((tm, tn), lambda i, j, k: (i, j))
        out_shape = SDS((M, N), out_dtype)
    has_add = add is not None

    def body(*refs):
        if has_add:
            a_ref, b_ref, add_ref, o_ref, acc_ref = refs
        else:
            a_ref, b_ref, o_ref, acc_ref = refs
        k = pl.program_id(2)

        @pl.when(k == 0)
        def _():
            acc_ref[...] = jnp.zeros_like(acc_ref)

        acc_ref[...] += lax.dot_general(a_ref[...].astype(BF16), b_ref[...].astype(BF16), dn,
                                        preferred_element_type=F32)

        @pl.when(k == nk - 1)
        def _():
            r = acc_ref[...]
            if has_add:
                r = r + add_ref[...].astype(F32)
            if chunked:
                o_ref[0] = r.astype(o_ref.dtype)
            else:
                o_ref[...] = r.astype(o_ref.dtype)

    ins = [a, b] + ([add] if has_add else [])
    in_specs = [a_spec, b_spec] + ([o_spec] if has_add else [])
    (out,), carried = _call(body, name, (M // tm, N // tn, nk), 0, in_specs, [o_spec], [out_shape], [pltpu.VMEM((tm, tn), F32)],
                            ins, ("parallel", "parallel", "arbitrary"), carry)
    return out if carry is None else (out, carried)


def _rowwise(fn, rows, params, row_outs, acc_outs, tm, name):
    n_rows = rows[0][0].shape[0]
    assert n_rows % tm == 0
    nr, npar, no = len(rows), len(params), len(row_outs)

    def body(*refs):
        i = pl.program_id(0)
        r, p = refs[:nr], refs[nr:nr + npar]
        o, acc = refs[nr + npar:nr + npar + no], refs[nr + npar + no:]
        outs, sums = fn(i, [x[...] for x in r], [x[...] for x in p])
        for ref, val in zip(o, outs, strict=True):
            ref[...] = val.astype(ref.dtype)
        if acc:
            @pl.when(i == 0)
            def _():
                for ref in acc:
                    ref[...] = jnp.zeros_like(ref)
            for ref, val in zip(acc, sums, strict=True):
                ref[...] += val.astype(F32)

    def full(shape):
        nd = len(shape)
        return pl.BlockSpec(tuple(shape), lambda i: (0,) * nd)

    in_specs = [pl.BlockSpec((tm, w), (lambda i, cb=cb: (i, cb))) for (_, w, cb) in rows] + [full(p.shape) for p in params]
    out_specs = [pl.BlockSpec((tm, w), lambda i: (i, 0)) for (w, _) in row_outs] + [full(s) for s in acc_outs]
    out_shape = [SDS((n_rows, w), dt) for (w, dt) in row_outs] + [SDS(tuple(s), F32) for s in acc_outs]
    res = _pcall(body, name=name, out_shape=out_shape, grid=(n_rows // tm,), in_specs=in_specs, out_specs=out_specs,
                 compiler_params=_params(("arbitrary",)))(*[r[0] for r in rows], *params)
    return list(res[:no]), list(res[no:])


def _whole(a):
    return (a, a.shape[1], 0)


def _gelu(x):
    return 0.5 * x * (1.0 + lax.erf(x * (1.0 / math.sqrt(2.0))))


def _layer_norm(x, g, b):
    mu = x.mean(-1, keepdims=True)
    var = jnp.mean(jnp.square(x - mu), -1, keepdims=True)
    return (x - mu) * lax.rsqrt(var + EPS) * g + b


def _rms_norm(x, g):
    return x * lax.rsqrt(jnp.mean(jnp.square(x), -1, keepdims=True) + EPS) * g


def _sgu_math(hu, hv, ln_g, ln_b, ws, bs):
    u = _gelu(hu.astype(F32))
    vn = _layer_norm(_gelu(hv.astype(F32)), ln_g, ln_b)
    r = lax.broadcasted_iota(jnp.int32, (SGU_CHUNK, SGU_CHUNK), 0)
    c = lax.broadcasted_iota(jnp.int32, (SGU_CHUNK, SGU_CHUNK), 1)
    outs = []
    for g in range(SGU_GROUPS):
        w = jnp.where(r >= c, ws[g], 0.0).astype(BF16)
        vg = vn[:, g * SGU_GROUP_DIM:(g + 1) * SGU_GROUP_DIM].astype(BF16)
        outs.append(jnp.dot(w, vg, preferred_element_type=F32) + bs[g])
    return u * jnp.concatenate(outs, axis=1)


def _sgu_fwd(h, cu, cv, ln_g, ln_b, w, b3):
    def fn(i, rows, ps):
        g_, b_, w_, b3_ = ps
        y = _sgu_math(rows[0], rows[1], g_, b_, [w_[g] for g in range(SGU_GROUPS)], [b3_[g] for g in range(SGU_GROUPS)])
        return [y], []
    (y,), _ = _rowwise(fn, [(h, SGU_WIDTH, cu), (h, SGU_WIDTH, cv)], [ln_g, ln_b, w, b3], [(SGU_WIDTH, BF16)], [],
                       SGU_CHUNK, "sgu_fwd")
    return y


def _sgu_bwd(h, cu, cv, dy, ln_g, ln_b, w, b3):
    nd = 2 * SGU_WIDTH

    def body(hu_ref, hv_ref, dy_ref, g_ref, b_ref, w_ref, b3_ref, dh_ref, dg_ref, db_ref, dw_ref, db3_ref):
        i = pl.program_id(0)

        @pl.when(i == 0)
        def _():
            dg_ref[...] = jnp.zeros_like(dg_ref)
            db_ref[...] = jnp.zeros_like(db_ref)
            dw_ref[...] = jnp.zeros_like(dw_ref)
            db3_ref[...] = jnp.zeros_like(db3_ref)

        ws = [w_ref[g] for g in range(SGU_GROUPS)]
        bs = [b3_ref[g] for g in range(SGU_GROUPS)]
        _, vjp = jax.vjp(_sgu_math, hu_ref[...], hv_ref[...], g_ref[...], b_ref[...], ws, bs)
        dhu, dhv, dg, db, dws, dbs = vjp(dy_ref[...].astype(F32))
        dh_ref[...] = jnp.concatenate([dhu, dhv], axis=1).astype(dh_ref.dtype)
        dg_ref[...] += dg
        db_ref[...] += db
        for g in range(SGU_GROUPS):
            dw_ref[g] += dws[g]
            db3_ref[g] += dbs[g]

    n = h.shape[0]
    blk = lambda cb: pl.BlockSpec((SGU_CHUNK, SGU_WIDTH), lambda i, cb=cb: (i, cb))
    full = lambda s: pl.BlockSpec(tuple(s), lambda i: (0,) * len(s))
    return _pcall(
        body, name="sgu_bwd", grid=(n // SGU_CHUNK,),
        out_shape=[SDS((n, nd), BF16), SDS(ln_g.shape, F32), SDS(ln_b.shape, F32), SDS(w.shape, F32), SDS(b3.shape, F32)],
        in_specs=[blk(cu), blk(cv), blk(0), full(ln_g.shape), full(ln_b.shape), full(w.shape), full(b3.shape)],
        out_specs=[pl.BlockSpec((SGU_CHUNK, nd), lambda i: (i, 0)), full(ln_g.shape), full(ln_b.shape), full(w.shape),
                   full(b3.shape)],
        compiler_params=_params(("arbitrary",)))(h, h, dy, ln_g, ln_b, w, b3)


def _swa_math(q, kp, kc, vp, vc, sinks, not_first):
    kw = jnp.concatenate([kp, kc], axis=0).astype(BF16)
    vw = jnp.concatenate([vp, vc], axis=0).astype(BF16)
    qb = q.astype(BF16)
    q_off = lax.broadcasted_iota(jnp.int32, (SWA_BLOCK, 2 * SWA_BLOCK), 0) + SWA_BLOCK
    k_off = lax.broadcasted_iota(jnp.int32, (SWA_BLOCK, 2 * SWA_BLOCK), 1)
    rel = q_off - k_off
    valid = (rel >= 0) & (rel < SWA_BLOCK) & (not_first | (k_off >= SWA_BLOCK))
    G = SWA_Q_HEADS // SWA_KV_HEADS
    outs = []
    for head in range(SWA_Q_HEADS):
        hk = head // G
        qh = qb[:, head * SWA_HEAD_DIM:(head + 1) * SWA_HEAD_DIM]
        kh = kw[:, hk * SWA_HEAD_DIM:(hk + 1) * SWA_HEAD_DIM]
        vh = vw[:, hk * SWA_HEAD_DIM:(hk + 1) * SWA_HEAD_DIM]
        s = lax.dot_general(qh, kh, (((1,), (1,)), ((), ())), preferred_element_type=F32) * (SWA_HEAD_DIM ** -0.5)
        s = jnp.where(valid, s, MASK_VALUE)
        sink = sinks[:, head:head + 1]
        m = jnp.maximum(s.max(-1, keepdims=True), sink)
        p = jnp.exp(s - m)
        p = (p / (p.sum(-1, keepdims=True) + jnp.exp(sink - m))).astype(BF16)
        outs.append(jnp.dot(p, vh, preferred_element_type=F32))
    return jnp.concatenate(outs, axis=1)


def _swa_fwd(h, cq, ck, cv, sinks, carry=None):
    n = h.shape[0]
    nb = n // SWA_BLOCK

    def body(q_ref, kp_ref, kc_ref, vp_ref, vc_ref, s_ref, o_ref):
        i = pl.program_id(0)
        f = lambda x: x[...].astype(F32)
        o_ref[...] = _swa_math(f(q_ref), f(kp_ref), f(kc_ref), f(vp_ref), f(vc_ref), s_ref[...], i > 0).astype(o_ref.dtype)

    prev = lambda cb: pl.BlockSpec((SWA_BLOCK, A_KV), lambda i, cb=cb: (jnp.maximum(i - 1, 0), cb))
    cur = lambda cb: pl.BlockSpec((SWA_BLOCK, A_KV), lambda i, cb=cb: (i, cb))
    return _call(body, "swa_fwd", (nb,), 0,
                 [pl.BlockSpec((SWA_BLOCK, A_Q), lambda i: (i, cq)), prev(ck), cur(ck), prev(cv), cur(cv),
                  pl.BlockSpec((1, SWA_Q_HEADS), lambda i: (0, 0))],
                 [pl.BlockSpec((SWA_BLOCK, A_Q), lambda i: (i, 0))], [SDS((n, A_Q), BF16)], [],
                 [h, h, h, h, h, sinks], ("arbitrary",), carry)


def _swa_bwd(h, cq, ck, cv, sinks, dy, carry=None):
    n = h.shape[0]
    nb = n // SWA_BLOCK

    def body(q_ref, kp_ref, kc_ref, vp_ref, vc_ref, s_ref, dy_ref, dq_ref, dk_ref, dv_ref, ds_ref, ck_ref, cv_ref):
        r = pl.program_id(0)
        blk = nb - 1 - r

        @pl.when(r == 0)
        def _():
            ds_ref[...] = jnp.zeros_like(ds_ref)
            ck_ref[...] = jnp.zeros_like(ck_ref)
            cv_ref[...] = jnp.zeros_like(cv_ref)

        f = lambda x: x[...].astype(F32)
        not_first = blk > 0
        _, vjp = jax.vjp(lambda q, kp, kc, vp, vc, s: _swa_math(q, kp, kc, vp, vc, s, not_first),
                         f(q_ref), f(kp_ref), f(kc_ref), f(vp_ref), f(vc_ref), s_ref[...])
        dq, dkp, dkc, dvp, dvc, dsk = vjp(f(dy_ref))
        dq_ref[...] = dq.astype(dq_ref.dtype)
        dk_ref[...] = (dkc + ck_ref[...]).astype(dk_ref.dtype)
        dv_ref[...] = (dvc + cv_ref[...]).astype(dv_ref.dtype)
        ck_ref[...] = dkp
        cv_ref[...] = dvp
        ds_ref[...] += dsk

    rev = lambda i: nb - 1 - i
    prev = lambda cb: pl.BlockSpec((SWA_BLOCK, A_KV), lambda i, cb=cb: (jnp.maximum(rev(i) - 1, 0), cb))
    cur = lambda cb: pl.BlockSpec((SWA_BLOCK, A_KV), lambda i, cb=cb: (rev(i), cb))
    return _call(
        body, "swa_bwd", (nb,), 0,
        [pl.BlockSpec((SWA_BLOCK, A_Q), lambda i: (rev(i), cq)), prev(ck), cur(ck), prev(cv), cur(cv),
         pl.BlockSpec((1, SWA_Q_HEADS), lambda i: (0, 0)), pl.BlockSpec((SWA_BLOCK, A_Q), lambda i: (rev(i), 0))],
        [pl.BlockSpec((SWA_BLOCK, A_Q), lambda i: (rev(i), 0)), pl.BlockSpec((SWA_BLOCK, A_KV), lambda i: (rev(i), 0)),
         pl.BlockSpec((SWA_BLOCK, A_KV), lambda i: (rev(i), 0)), pl.BlockSpec((1, SWA_Q_HEADS), lambda i: (0, 0))],
        [SDS((n, A_Q), BF16), SDS((n, A_KV), BF16), SDS((n, A_KV), BF16), SDS((1, SWA_Q_HEADS), F32)],
        [pltpu.VMEM((SWA_BLOCK, A_KV), F32), pltpu.VMEM((SWA_BLOCK, A_KV), F32)],
        [h, h, h, h, h, sinks, dy], ("arbitrary",), carry)


def _rope(x, cos, sin, sign):
    w = x.shape[1]
    reps = w // LANES
    ct = jnp.tile(cos, (1, reps)) if reps > 1 else cos
    st = jnp.tile(sin, (1, reps)) if reps > 1 else sin
    fwd = pltpu.roll(x, MLA_ROPE // 2, axis=1)
    bwd = pltpu.roll(x, w - MLA_ROPE // 2, axis=1)
    lane = lax.broadcasted_iota(jnp.int32, x.shape, 1) % LANES
    rot = jnp.where(lane < MLA_ROPE // 2, -bwd, fwd)
    return x * ct + sign * (rot * st)


def _rope_call(a, wa, ca, b, wb, cb, cos, sin, sign, tm, name):
    def fn(i, rows, ps):
        xa, xb, c_, s_ = rows
        return [_rope(xa.astype(F32), c_, s_, sign), _rope(xb.astype(F32), c_, s_, sign)], []
    (ra, rb), _ = _rowwise(fn, [(a, wa, ca), (b, wb, cb), _whole(cos), _whole(sin)], [], [(wa, BF16), (wb, BF16)], [], tm, name)
    return ra, rb


MLA_SCALE = (MLA_NOPE + MLA_ROPE) ** -0.5
LOG2E = math.log2(math.e)


MLA_HP = 2
MLA_W = MLA_HP * LANES


def _mla_scores(qn_ref, qr_ref, kn_ref, kr_ref, hh, masked):
    cols = slice(hh * LANES, (hh + 1) * LANES)
    q = jnp.concatenate([qn_ref[:, cols], qr_ref[:, cols]], axis=1)
    k = jnp.concatenate([kn_ref[:, cols], kr_ref[...]], axis=1)
    s = lax.dot_general(q, k, (((1,), (1,)), ((), ())), preferred_element_type=F32)
    if masked:
        row = lax.broadcasted_iota(jnp.int32, s.shape, 0)
        col = lax.broadcasted_iota(jnp.int32, s.shape, 1)
        s = jnp.where(col <= row, s, MASK_VALUE)
    return s, q, k


def _causal_pairs(nq, by_query):
    if by_query:
        pairs = [(i, j) for i in range(nq) for j in range(i + 1)]
    else:
        pairs = [(i, j) for j in range(nq) for i in range(j, nq)]
    return (jnp.asarray(np.array([p[0] for p in pairs], np.int32)), jnp.asarray(np.array([p[1] for p in pairs], np.int32)),
            len(pairs))


def _mla_fwd(q_full, qr, kv, kr, T, carry=None):
    n = q_full.shape[0]
    nq = n // T
    H = MLA_HEADS
    qi, kj, npairs = _causal_pairs(nq, True)

    def body(qi_ref, kj_ref, qn_ref, qr_ref, kn_ref, v_ref, kr_ref, y_ref, lse_ref, m_ref, l_ref, acc_ref):
        t = pl.program_id(1)
        i, j = qi_ref[t], kj_ref[t]

        @pl.when(j == 0)
        def _():
            m_ref[...] = jnp.full_like(m_ref, MASK_VALUE)
            l_ref[...] = jnp.zeros_like(l_ref)
            acc_ref[...] = jnp.zeros_like(acc_ref)

        def update(masked):
            for hh in range(MLA_HP):
                s, _, _ = _mla_scores(qn_ref, qr_ref, kn_ref, kr_ref, hh, masked)
                m_prev = m_ref[hh]
                m_new = jnp.maximum(m_prev, s.max(-1, keepdims=True))
                p = jnp.exp2((s - m_new[:, :1]) * (MLA_SCALE * LOG2E))
                alpha = jnp.exp2((m_prev - m_new) * (MLA_SCALE * LOG2E))
                l_ref[hh] = alpha * l_ref[hh] + p.sum(-1, keepdims=True)
                acc_ref[hh] = alpha * acc_ref[hh] + jnp.dot(p.astype(BF16), v_ref[:, hh * LANES:(hh + 1) * LANES],
                                                            preferred_element_type=F32)
                m_ref[hh] = m_new

        @pl.when(j < i)
        def _():
            update(False)

        @pl.when(j == i)
        def _():
            update(True)
            for hh in range(MLA_HP):
                y_ref[:, hh * LANES:(hh + 1) * LANES] = (acc_ref[hh] / l_ref[hh]).astype(y_ref.dtype)
                lse_ref[hh] = m_ref[hh] * (MLA_SCALE * LOG2E) + jnp.log2(l_ref[hh])

    G = H // MLA_HP
    qspec = lambda off: pl.BlockSpec((T, MLA_W), lambda h, t, qi, kj, off=off: (qi[t], off + h))
    kspec = lambda off: pl.BlockSpec((T, MLA_W), lambda h, t, qi, kj, off=off: (kj[t], off + h))
    return _call(
        body, "mla_fwd", (G, npairs), 2,
        [qspec(0), qspec(0), kspec(0), kspec(G), pl.BlockSpec((T, LANES), lambda h, t, qi, kj: (kj[t], 0))],
        [pl.BlockSpec((T, MLA_W), lambda h, t, qi, kj: (qi[t], h)),
         pl.BlockSpec((MLA_HP, T, LANES), lambda h, t, qi, kj: (h, qi[t], 0))],
        [SDS((n, H * MLA_V), BF16), SDS((H, n, LANES), F32)], [pltpu.VMEM((MLA_HP, T, LANES), F32)] * 3,
        [qi, kj, q_full, qr, kv, kv, kr], ("parallel", "arbitrary"), carry)


def _mla_delta(dy, y, T):
    n = y.shape[0]
    H = MLA_HEADS

    def body(dy_ref, y_ref, d_ref):
        d = jnp.sum(dy_ref[...].astype(F32) * y_ref[...].astype(F32), axis=-1, keepdims=True)
        d_ref[0] = jnp.broadcast_to(d, (T, LANES))

    spec = pl.BlockSpec((T, LANES), lambda h, i: (i, h))
    return _pcall(body, name="mla_delta", grid=(H, n // T), out_shape=SDS((H, n, LANES), F32), in_specs=[spec, spec],
                  out_specs=pl.BlockSpec((1, T, LANES), lambda h, i: (h, i, 0)),
                  compiler_params=_params(("parallel", "parallel")))(dy, y)


def _mla_bwd_dq(q_full, qr, kv, kr, dy, lse, delta, T, carry=None):
    n = q_full.shape[0]
    nq = n // T
    H = MLA_HEADS

    qi, kj, npairs = _causal_pairs(nq, True)

    def body(qi_ref, kj_ref, qn_ref, qr_ref, kn_ref, v_ref, kr_ref, dy_ref, lse_ref, dl_ref, dqn_ref, dqr_ref, acc_ref):
        t = pl.program_id(1)
        i, j = qi_ref[t], kj_ref[t]

        @pl.when(j == 0)
        def _():
            acc_ref[...] = jnp.zeros_like(acc_ref)

        def update(masked):
            for hh in range(MLA_HP):
                cols = slice(hh * LANES, (hh + 1) * LANES)
                s, _, k = _mla_scores(qn_ref, qr_ref, kn_ref, kr_ref, hh, masked)
                p = jnp.exp2(s * (MLA_SCALE * LOG2E) - lse_ref[hh][:, :1])
                dp = lax.dot_general(dy_ref[:, cols], v_ref[:, cols], (((1,), (1,)), ((), ())), preferred_element_type=F32)
                ds = p * (dp - dl_ref[hh][:, :1])
                acc_ref[hh] += jnp.dot(ds.astype(BF16), k, preferred_element_type=F32)

        @pl.when(j < i)
        def _():
            update(False)

        @pl.when(j == i)
        def _():
            update(True)
            for hh in range(MLA_HP):
                cols = slice(hh * LANES, (hh + 1) * LANES)
                dqn_ref[:, cols] = (acc_ref[hh][:, :LANES] * MLA_SCALE).astype(dqn_ref.dtype)
                dqr_ref[:, cols] = (acc_ref[hh][:, LANES:] * MLA_SCALE).astype(dqr_ref.dtype)

    G = H // MLA_HP
    qspec = lambda off: pl.BlockSpec((T, MLA_W), lambda h, t, qi, kj, off=off: (qi[t], off + h))
    kspec = lambda off: pl.BlockSpec((T, MLA_W), lambda h, t, qi, kj, off=off: (kj[t], off + h))
    stat = pl.BlockSpec((MLA_HP, T, LANES), lambda h, t, qi, kj: (h, qi[t], 0))
    out = pl.BlockSpec((T, MLA_W), lambda h, t, qi, kj: (qi[t], h))
    return _call(
        body, "mla_bwd_dq", (G, npairs), 2,
        [qspec(0), qspec(0), kspec(0), kspec(G), pl.BlockSpec((T, LANES), lambda h, t, qi, kj: (kj[t], 0)), qspec(0), stat, stat],
        [out, out], [SDS((n, H * LANES), BF16), SDS((n, H * LANES), BF16)], [pltpu.VMEM((MLA_HP, T, 2 * LANES), F32)],
        [qi, kj, q_full, qr, kv, kv, kr, dy, lse, delta], ("parallel", "arbitrary"), carry)


def _mla_bwd_dkv(q_full, qr, kv, kr, dy, lse, delta, T, carry=None):
    n = q_full.shape[0]
    nq = n // T
    H = MLA_HEADS

    qi, kj, npairs = _causal_pairs(nq, False)

    def body(qi_ref, kj_ref, qn_ref, qr_ref, kn_ref, v_ref, kr_ref, dy_ref, lse_ref, dl_ref, dkn_ref, dv_ref, dkr_ref,
             dk_acc, dv_acc):
        t = pl.program_id(1)
        i, j = qi_ref[t], kj_ref[t]

        @pl.when(i == j)
        def _():
            dk_acc[...] = jnp.zeros_like(dk_acc)
            dv_acc[...] = jnp.zeros_like(dv_acc)

        def update(masked):
            for hh in range(MLA_HP):
                cols = slice(hh * LANES, (hh + 1) * LANES)
                s, q, _ = _mla_scores(qn_ref, qr_ref, kn_ref, kr_ref, hh, masked)
                p = jnp.exp2(s * (MLA_SCALE * LOG2E) - lse_ref[hh][:, :1])
                dy = dy_ref[:, cols]
                dv_acc[hh] += lax.dot_general(p.astype(BF16), dy, (((0,), (0,)), ((), ())), preferred_element_type=F32)
                dp = lax.dot_general(dy, v_ref[:, cols], (((1,), (1,)), ((), ())), preferred_element_type=F32)
                ds = p * (dp - dl_ref[hh][:, :1])
                dk_acc[hh] += lax.dot_general(ds.astype(BF16), q, (((0,), (0,)), ((), ())), preferred_element_type=F32)

        @pl.when(i == j)
        def _():
            update(True)

        @pl.when(i > j)
        def _():
            update(False)

        @pl.when(i == nq - 1)
        def _():
            for hh in range(MLA_HP):
                cols = slice(hh * LANES, (hh + 1) * LANES)
                dkn_ref[:, cols] = (dk_acc[hh][:, :LANES] * MLA_SCALE).astype(dkn_ref.dtype)
                dv_ref[:, cols] = dv_acc[hh].astype(dv_ref.dtype)
                dkr_ref[hh] = dk_acc[hh][:, LANES:] * MLA_SCALE

    G = H // MLA_HP
    qspec = lambda off: pl.BlockSpec((T, MLA_W), lambda h, t, qi, kj, off=off: (qi[t], off + h))
    kspec = lambda off: pl.BlockSpec((T, MLA_W), lambda h, t, qi, kj, off=off: (kj[t], off + h))
    stat = pl.BlockSpec((MLA_HP, T, LANES), lambda h, t, qi, kj: (h, qi[t], 0))
    out = pl.BlockSpec((T, MLA_W), lambda h, t, qi, kj: (kj[t], h))
    return _call(
        body, "mla_bwd_dkv", (G, npairs), 2,
        [qspec(0), qspec(0), kspec(0), kspec(G), pl.BlockSpec((T, LANES), lambda h, t, qi, kj: (kj[t], 0)), qspec(0), stat, stat],
        [out, out, pl.BlockSpec((MLA_HP, T, LANES), lambda h, t, qi, kj: (h, kj[t], 0))],
        [SDS((n, H * LANES), BF16), SDS((n, H * LANES), BF16), SDS((H, n, LANES), F32)],
        [pltpu.VMEM((MLA_HP, T, 2 * LANES), F32), pltpu.VMEM((MLA_HP, T, LANES), F32)],
        [qi, kj, q_full, qr, kv, kv, kr, dy, lse, delta], ("parallel", "arbitrary"), carry)


def _sum_heads(a, tm):
    H, n, _ = a.shape

    def body(a_ref, o_ref):
        o_ref[...] = jnp.sum(a_ref[...], axis=0)

    return _pcall(body, name="mla_sum_heads", grid=(n // tm,), out_shape=SDS((n, LANES), F32),
                  in_specs=[pl.BlockSpec((H, tm, LANES), lambda i: (0, i, 0))],
                  out_specs=pl.BlockSpec((tm, LANES), lambda i: (i, 0)), compiler_params=_params(("parallel",)))(a)


def _shift_down(x, k):
    row = lax.broadcasted_iota(jnp.int32, x.shape, 0)
    return jnp.where(row >= k, pltpu.roll(x, k, axis=0), 0.0)


def _shift_up(x, k):
    n = x.shape[0]
    row = lax.broadcasted_iota(jnp.int32, x.shape, 0)
    return jnp.where(row < n - k, pltpu.roll(x, n - k, axis=0), 0.0)


def _conv_fwd(up, w, b):
    n, c = up.shape

    def body(u_ref, w_ref, b_ref, o_ref):
        u = u_ref[...].astype(F32)
        wv = w_ref[...]
        o_ref[...] = (b_ref[...] + wv[0:1] * _shift_down(u, 2) + wv[1:2] * _shift_down(u, 1) + wv[2:3] * u).astype(o_ref.dtype)

    return _pcall(body, name="conv_fwd", grid=(c // LANES,), out_shape=SDS((n, c), BF16),
                  in_specs=[pl.BlockSpec((n, LANES), lambda j: (0, j)), pl.BlockSpec((3, LANES), lambda j: (0, j)),
                            pl.BlockSpec((1, LANES), lambda j: (0, j))],
                  out_specs=pl.BlockSpec((n, LANES), lambda j: (0, j)), compiler_params=_params(("parallel",)))(up, w, b)


def _conv_bwd(up, dc, w):
    n, c = up.shape

    def body(u_ref, d_ref, w_ref, du_ref, dw_ref, db_ref):
        u = u_ref[...].astype(F32)
        d = d_ref[...].astype(F32)
        wv = w_ref[...]
        du_ref[...] = (wv[2:3] * d + wv[1:2] * _shift_up(d, 1) + wv[0:1] * _shift_up(d, 2)).astype(du_ref.dtype)
        dw_ref[0:1, :] = jnp.sum(d * _shift_down(u, 2), axis=0, keepdims=True)
        dw_ref[1:2, :] = jnp.sum(d * _shift_down(u, 1), axis=0, keepdims=True)
        dw_ref[2:3, :] = jnp.sum(d * u, axis=0, keepdims=True)
        db_ref[...] = jnp.sum(d, axis=0, keepdims=True)

    col = pl.BlockSpec((n, LANES), lambda j: (0, j))
    return _pcall(body, name="conv_bwd", grid=(c // LANES,),
                  out_shape=[SDS((n, c), BF16), SDS((3, c), F32), SDS((1, c), F32)],
                  in_specs=[col, col, pl.BlockSpec((3, LANES), lambda j: (0, j))],
                  out_specs=[col, pl.BlockSpec((3, LANES), lambda j: (0, j)), pl.BlockSpec((1, LANES), lambda j: (0, j))],
                  compiler_params=_params(("parallel",)))(up, dc, w)


def _adamw_math(w, g, m, v):
    m = ADAM_B1 * m + (1.0 - ADAM_B1) * g
    v = ADAM_B2 * v + (1.0 - ADAM_B2) * jnp.square(g)
    m_hat = m / (1.0 - ADAM_B1 ** ADAM_STEP)
    v_hat = v / (1.0 - ADAM_B2 ** ADAM_STEP)
    delta = -ADAM_LR * (m_hat / (jnp.sqrt(v_hat) + ADAM_EPS) + ADAM_WD * w)
    return delta, m, v


def _adamw(w, g, m, v, name):
    L, r, c = w.shape
    tr = _rows_tile(r, c, 1 << 20, 8)

    def body(w_ref, g_ref, m_ref, v_ref, d_ref, nm_ref, nv_ref):
        d, nm, nv = _adamw_math(w_ref[...], g_ref[...], m_ref[...], v_ref[...])
        d_ref[...] = d
        nm_ref[...] = nm
        nv_ref[...] = nv

    spec = pl.BlockSpec((1, tr, c), lambda l, i: (l, i, 0))
    return _pcall(body, name=name, grid=(L, r // tr), out_shape=[SDS(w.shape, F32)] * 3, in_specs=[spec] * 4,
                  out_specs=[spec] * 3, compiler_params=_params(("parallel", "parallel")))(w, g, m, v)


def _coords():
    return lax.axis_index("x"), lax.axis_index("y"), lax.axis_index("c")


def _other_chips(x, y):
    return [(1 - x, y), (x, 1 - y), (1 - x, 1 - y)]


HBM_SPEC = pl.BlockSpec(memory_space=pltpu.HBM)


def _half(c, rows):
    return pl.ds(pl.multiple_of(c * rows, 16), rows)


def _rows_tile(rows, cols, budget_bytes=2 << 20, align=16):
    t = (min(max(align, budget_bytes // (4 * cols)), rows) // align) * align
    while t >= align:
        if rows % t == 0:
            return t
        t -= align
    return rows


def _scalar(v):
    return jnp.reshape(jnp.asarray(v, jnp.int32), (1,))


def _cast_into_slot(w3, layer, slot, name):
    _, R, C = w3.shape
    tr = _rows_tile(R, C)

    def body(s_ref, w_ref, o_ref):
        o_ref[0] = w_ref[0].astype(BF16)

    gs = pltpu.PrefetchScalarGridSpec(
        num_scalar_prefetch=1, grid=(R // tr,),
        in_specs=[pl.BlockSpec((1, tr, C), lambda i, s: (layer, i, 0))],
        out_specs=pl.BlockSpec((1, tr, C), lambda i, s: (s[0], i, 0)))
    return _pcall(body, name=name, grid_spec=gs, out_shape=SDS((N_CHIPS, R, C), BF16),
                  compiler_params=_params(("arbitrary",)))(_scalar(slot), w3)


class _Stage:
    def __init__(self, ins, out_shapes, aliases, n_sems, copies):
        self.ins, self.out_shapes, self.aliases, self.n_sems, self.copies = list(ins), out_shapes, aliases, n_sems, copies

    def start(self, ins, outs, send_sems, recv_sems):
        for cp in self.copies(ins, outs, send_sems, recv_sems)[0]:
            cp.start()

    def finish(self, ins, outs, send_sems, recv_sems):
        sends, arrivals = self.copies(ins, outs, send_sems, recv_sems)
        for cp in arrivals:
            cp.wait_recv()
        for cp in sends:
            cp.wait_send()


class _SemsFrom:
    def __init__(self, sems, base):
        self.sems, self.base = sems, base

    @property
    def at(self):
        return self

    def __getitem__(self, k):
        return self.sems.at[self.base + k]


def _both(a, b):
    if a is None or b is None:
        return a if b is None else b
    na, nao = len(a.ins), len(a.out_shapes)

    def copies(ins, outs, send_sems, recv_sems):
        sa, aa = a.copies(ins[:na], outs[:nao], send_sems, recv_sems)
        sb, ab = b.copies(ins[na:], outs[nao:], _SemsFrom(send_sems, a.n_sems), _SemsFrom(recv_sems, a.n_sems))
        return sa + sb, aa + ab

    aliases = {**a.aliases, **{na + i: nao + o for i, o in b.aliases.items()}}
    return _Stage(a.ins + b.ins, list(a.out_shapes) + list(b.out_shapes), aliases, a.n_sems + b.n_sems, copies)


def _remote(src, dst, send_sems, recv_sems, k, to):
    return pltpu.make_async_remote_copy(src_ref=src, dst_ref=dst, send_sem=send_sems.at[k], recv_sem=recv_sems.at[k],
                                        device_id=to, device_id_type=MESH)


def _gather_stages(bufs):
    n = len(bufs)
    shapes = [SDS(b.shape, b.dtype) for b in bufs]
    same = {i: i for i in range(n)}

    def over_ici(ins, outs, send_sems, recv_sems):
        x, y, c = _coords()
        blk = lambda w, chip: outs[w].at[chip, _half(c, outs[w].shape[1] // 2), :]
        sends, arrivals = [], []
        for w in range(n):
            for j, (px, py) in enumerate(_other_chips(x, y)):
                sends.append(_remote(blk(w, 2 * x + y), blk(w, 2 * x + y), send_sems, recv_sems, 3 * w + j, (px, py, c)))
                arrivals.append(_remote(blk(w, 2 * px + py), blk(w, 2 * px + py), send_sems, recv_sems, 3 * w + j, (px, py, c)))
        return sends, arrivals

    def to_sibling(ins, outs, send_sems, recv_sems):
        x, y, c = _coords()
        blk = lambda w, chip, half: outs[w].at[chip, _half(half, outs[w].shape[1] // 2), :]
        sends, arrivals = [], []
        for w in range(n):
            for j, (px, py) in enumerate(_other_chips(x, y)):
                k = 2 * px + py
                sends.append(_remote(blk(w, k, c), blk(w, k, c), send_sems, recv_sems, 3 * w + j, (x, y, 1 - c)))
                arrivals.append(_remote(blk(w, k, 1 - c), blk(w, k, 1 - c), send_sems, recv_sems, 3 * w + j, (x, y, 1 - c)))
        return sends, arrivals

    return (lambda b: _Stage(b, shapes, same, 3 * n, over_ici)), (lambda b: _Stage(b, shapes, same, 3 * n, to_sibling))


def _swap_stage(gs_):
    n = len(gs_)

    def copies(ins, outs, send_sems, recv_sems):
        x, y, c = _coords()
        cps = [_remote(ins[w].at[:, _half(1 - c, ins[w].shape[1] // 2), :], outs[w], send_sems, recv_sems, w, (x, y, 1 - c))
               for w in range(n)]
        return cps, cps

    return _Stage(gs_, [SDS((N_CHIPS, g.shape[1] // 2, g.shape[2]), g.dtype) for g in gs_], {}, n, copies)


def _scatter_stage(ps):
    n = len(ps)

    def copies(ins, outs, send_sems, recv_sems):
        x, y, c = _coords()
        cps = [_remote(ins[w].at[2 * px + py], outs[w].at[j], send_sems, recv_sems, 3 * w + j, (px, py, c))
               for w in range(n) for j, (px, py) in enumerate(_other_chips(x, y))]
        return cps, cps

    return _Stage(ps, [SDS((3,) + p.shape[1:], p.dtype) for p in ps], {}, 3 * n, copies)


def _join_stage(bufs, layers):
    n = len(bufs)
    slots = [(w, l) for w in range(n) for l in layers[w]]

    def copies(ins, outs, send_sems, recv_sems):
        x, y, c = _coords()
        blk = lambda w, l, half: outs[w].at[l, _half(half, outs[w].shape[1] // 2), :]
        sends = [_remote(blk(w, l, c), blk(w, l, c), send_sems, recv_sems, k, (x, y, 1 - c)) for k, (w, l) in enumerate(slots)]
        arrivals = [_remote(blk(w, l, 1 - c), blk(w, l, 1 - c), send_sems, recv_sems, k, (x, y, 1 - c))
                    for k, (w, l) in enumerate(slots)]
        return sends, arrivals

    return _Stage(bufs, [SDS(b.shape, b.dtype) for b in bufs], {i: i for i in range(n)}, len(slots), copies)


def _stage_scratch(stage):
    return [pltpu.SemaphoreType.DMA((stage.n_sems,)), pltpu.SemaphoreType.DMA((stage.n_sems,))]


def _run_stage(stage, name):
    n_in, n_out = len(stage.ins), len(stage.out_shapes)

    def body(*refs):
        ins, outs, send_sems, recv_sems = refs[:n_in], refs[n_in:n_in + n_out], refs[n_in + n_out], refs[n_in + n_out + 1]
        stage.start(ins, outs, send_sems, recv_sems)
        stage.finish(ins, outs, send_sems, recv_sems)

    return _pcall(body, name=name, out_shape=stage.out_shapes, in_specs=[HBM_SPEC] * n_in, out_specs=[HBM_SPEC] * n_out,
                  input_output_aliases=stage.aliases, scratch_shapes=_stage_scratch(stage))(*stage.ins)


def _call(body, name, grid, n_prefetch, in_specs, out_specs, out_shape, scratch, operands, semantics, carry=None):
    n_in, n_out, n_sc = len(in_specs), len(out_specs), len(scratch)
    if carry is None:
        gs = pltpu.PrefetchScalarGridSpec(num_scalar_prefetch=n_prefetch, grid=grid, in_specs=in_specs, out_specs=out_specs,
                                          scratch_shapes=scratch)
        res = _pcall(body, name=name, grid_spec=gs, out_shape=out_shape, compiler_params=_params(semantics))(*operands)
        return list(res), []
    s_in, s_out = len(carry.ins), len(carry.out_shapes)

    def carrying(*refs):
        o = n_prefetch
        pre, ins = refs[:o], refs[o:o + n_in]
        o += n_in
        sins = refs[o:o + s_in]
        o += s_in
        outs = refs[o:o + n_out]
        o += n_out
        souts = refs[o:o + s_out]
        o += s_out
        sc, send_sems, recv_sems = refs[o:o + n_sc], refs[o + n_sc], refs[o + n_sc + 1]
        first = functools.reduce(jnp.logical_and, [pl.program_id(a) == 0 for a in range(len(grid))])
        last = functools.reduce(jnp.logical_and, [pl.program_id(a) == g - 1 for a, g in enumerate(grid)])

        @pl.when(first)
        def _():
            carry.start(sins, souts, send_sems, recv_sems)

        body(*pre, *ins, *outs, *sc)

        @pl.when(last)
        def _():
            carry.finish(sins, souts, send_sems, recv_sems)

    gs = pltpu.PrefetchScalarGridSpec(
        num_scalar_prefetch=n_prefetch, grid=grid, in_specs=list(in_specs) + [HBM_SPEC] * s_in,
        out_specs=list(out_specs) + [HBM_SPEC] * s_out, scratch_shapes=list(scratch) + _stage_scratch(carry))
    aliases = {n_prefetch + n_in + a: n_out + b for a, b in carry.aliases.items()}
    res = _pcall(carrying, name=name, grid_spec=gs, out_shape=list(out_shape) + list(carry.out_shapes),
                 input_output_aliases=aliases, compiler_params=_params(("arbitrary",) * len(grid)))(*operands, *carry.ins)
    return list(res[:n_out]), list(res[n_out:])


def _all_reduce_small(v):
    n = v.shape[0]

    def body(v_ref, out_ref, slots, send_sems, recv_sems):
        x, y, c = _coords()
        me = 4 * x + 2 * y + c
        cps = []
        for r in range(1, 8):
            t = (me + r) % 8
            cp = pltpu.make_async_remote_copy(src_ref=v_ref, dst_ref=slots.at[me], send_sem=send_sems.at[r - 1],
                                              recv_sem=recv_sems.at[me], device_id=(t // 4, (t // 2) % 2, t % 2),
                                              device_id_type=MESH)
            cp.start()
            cps.append(cp)
        slots[me] = v_ref[...]
        for r in range(1, 8):
            s = (me + r) % 8
            pltpu.make_async_remote_copy(src_ref=v_ref, dst_ref=slots.at[s], send_sem=send_sems.at[r - 1],
                                         recv_sem=recv_sems.at[s], device_id=(x, y, c), device_id_type=MESH).wait_recv()
        for cp in cps:
            cp.wait_send()
        acc = slots[0]
        for d in range(1, 8):
            acc = acc + slots[d]
        out_ref[...] = acc

    return _pcall(body, name="all_reduce_small", out_shape=SDS((n, LANES), F32),
                  in_specs=[pl.BlockSpec(memory_space=pltpu.VMEM)], out_specs=pl.BlockSpec(memory_space=pltpu.VMEM),
                  scratch_shapes=[pltpu.VMEM((8, n, LANES), F32), pltpu.SemaphoreType.DMA((7,)), pltpu.SemaphoreType.DMA((8,))],
                  compiler_params=pltpu.CompilerParams(vmem_limit_bytes=VMEM_LIMIT))(v)


def _add_half(g, recv, c, name):
    _, R, C = g.shape
    rows = R // 2
    tr = _rows_tile(rows, C, 1 << 20)
    nb = rows // tr

    def body(s_ref, g_ref, r_ref, o32_ref, o16_ref):
        s = g_ref[...] + r_ref[...]
        o32_ref[...] = s
        o16_ref[...] = s.astype(BF16)

    blk = lambda k, i, s: (k, i, 0)
    gs = pltpu.PrefetchScalarGridSpec(
        num_scalar_prefetch=1, grid=(N_CHIPS, nb),
        in_specs=[pl.BlockSpec((1, tr, C), lambda k, i, s: (k, s[0] * nb + i, 0)), pl.BlockSpec((1, tr, C), blk)],
        out_specs=[pl.BlockSpec((1, tr, C), blk), pl.BlockSpec((1, tr, C), blk)])
    return _pcall(body, name=name, grid_spec=gs, out_shape=[SDS((N_CHIPS, rows, C), F32), SDS((N_CHIPS, rows, C), BF16)],
                  compiler_params=_params(("arbitrary", "arbitrary")))(_scalar(c), g, recv)


def _sum_into(p32, arrived, chip, c, layer, n_layers, prev, name):
    _, rows, C = p32.shape
    tr = _rows_tile(rows, C, 1 << 20)
    nb = rows // tr

    def body(chip_ref, c_ref, p_ref, a_ref, *rest):
        o_ref = rest[-1]
        o_ref[0] = ((p_ref[0] + a_ref[0].astype(F32)) + a_ref[1].astype(F32)) + a_ref[2].astype(F32)

    in_specs = [pl.BlockSpec((1, tr, C), lambda i, chip_ref, c_ref: (chip_ref[0], i, 0)),
                pl.BlockSpec((3, tr, C), lambda i, chip_ref, c_ref: (0, i, 0))]
    ins = [p32, arrived]
    aliases = {}
    if prev is not None:
        in_specs.append(pl.BlockSpec(memory_space=pl.ANY))
        ins.append(prev)
        aliases = {4: 0}
    gs = pltpu.PrefetchScalarGridSpec(
        num_scalar_prefetch=2, grid=(nb,), in_specs=in_specs,
        out_specs=pl.BlockSpec((1, tr, C), lambda i, chip_ref, c_ref: (layer, c_ref[0] * nb + i, 0)))
    return _pcall(body, name=name, grid_spec=gs, out_shape=SDS((n_layers, 2 * rows, C), F32), input_output_aliases=aliases,
                  compiler_params=_params(("arbitrary",)))(_scalar(chip), _scalar(c), *ins)


def _h_layout(D):
    G = N_BRANCHES * D
    off, o = {}, 0
    for name, w in [("g", G), ("qa", A_Q), ("cq", MLA_Q_RANK), ("ckv", MLA_KV_RANK), ("hu", SGU_WIDTH), ("hv", SGU_WIDTH),
                    ("ka", A_KV), ("va", A_KV), ("kr", LANES)]:
        assert o % w == 0, (name, o, w)
        off[name] = (o, w)
        o += w
    off["total"] = -(-o // 512) * 512
    return off


def _perm_w_in(w, lay):
    s = np.cumsum([0, A_Q, A_KV, A_KV, MLA_Q_RANK, MLA_KV_RANK, MLA_ROPE, SGU_WIDTH, SGU_WIDTH])
    qa, ka, va, cq, ckv, kr, hu, hv = [w[:, s[i]:s[i + 1]] for i in range(8)]
    g = w[:, s[8]:]
    pad = jnp.zeros((w.shape[0], lay["total"] - lay["kr"][0] - MLA_ROPE), w.dtype)
    return jnp.concatenate([g, qa, cq, ckv, hu, hv, ka, va, kr, pad], axis=1)


def _unperm_w_in(wp, lay, D):
    take = lambda n, width=None: wp[:, lay[n][0]:lay[n][0] + (width or lay[n][1])]
    return jnp.concatenate([take("qa"), take("ka"), take("va"), take("cq"), take("ckv"), take("kr", MLA_ROPE), take("hu"),
                            take("hv"), take("g")], axis=1)


def _perm_w_uq(w):
    r = w.shape[0]
    w3 = w.reshape(r, MLA_HEADS, MLA_NOPE + MLA_ROPE)
    nope = w3[:, :, :MLA_NOPE].reshape(r, MLA_HEADS * MLA_NOPE)
    rope = jnp.pad(w3[:, :, MLA_NOPE:], ((0, 0), (0, 0), (0, LANES - MLA_ROPE))).reshape(r, MLA_HEADS * LANES)
    return jnp.concatenate([nope, rope], axis=1)


def _unperm_w_uq(wp):
    r = wp.shape[0]
    nope = wp[:, :MLA_HEADS * MLA_NOPE].reshape(r, MLA_HEADS, MLA_NOPE)
    rope = wp[:, MLA_HEADS * MLA_NOPE:].reshape(r, MLA_HEADS, LANES)[:, :, :MLA_ROPE]
    return jnp.concatenate([nope, rope], axis=2).reshape(r, MLA_HEADS * (MLA_NOPE + MLA_ROPE))


def _perm_w_ukv(w):
    r = w.shape[0]
    w3 = w.reshape(r, MLA_HEADS, MLA_NOPE + MLA_V)
    return jnp.concatenate([w3[:, :, :MLA_NOPE].reshape(r, -1), w3[:, :, MLA_NOPE:].reshape(r, -1)], axis=1)


def _unperm_w_ukv(wp):
    r = wp.shape[0]
    k = wp[:, :MLA_HEADS * MLA_NOPE].reshape(r, MLA_HEADS, MLA_NOPE)
    v = wp[:, MLA_HEADS * MLA_NOPE:].reshape(r, MLA_HEADS, MLA_V)
    return jnp.concatenate([k, v], axis=2).reshape(r, -1)


def _col_chunks(g):
    r, c4 = g.shape
    return jnp.transpose(g.reshape(r, N_CHIPS, c4 // N_CHIPS), (1, 0, 2))


def kernel(x, positions, w_in, b_gate, sinks, q_norm_g, kv_norm_g, w_uq, w_ukv, sgu_ln_g, sgu_ln_b, sgu_w, sgu_b, w_proj_a, w_proj_b, w_proj_c, w_o, ln1_g, ln1_b, w_up, conv_w, conv_b, w_down, ln2_g, ln2_b, loss_target, m_w_in, m_b_gate, m_sinks, m_q_norm_g, m_kv_norm_g, m_w_uq, m_w_ukv, m_sgu_ln_g, m_sgu_ln_b, m_sgu_w, m_sgu_b, m_w_proj_a, m_w_proj_b, m_w_proj_c, m_w_o, m_ln1_g, m_ln1_b, m_w_up, m_conv_w, m_conv_b, m_w_down, m_ln2_g, m_ln2_b, v_w_in, v_b_gate, v_sinks, v_q_norm_g, v_kv_norm_g, v_w_uq, v_w_ukv, v_sgu_ln_g, v_sgu_ln_b, v_sgu_w, v_sgu_b, v_w_proj_a, v_w_proj_b, v_w_proj_c, v_w_o, v_ln1_g, v_ln1_b, v_w_up, v_conv_w, v_conv_b, v_w_down, v_ln2_g, v_ln2_b):
    a = locals()
    W = {k: a[k] for k in WEIGHTS}
    Mo = {k: a["m_" + k] for k in WEIGHTS}
    Vo = {k: a["v_" + k] for k in WEIGHTS}
    S, D = x.shape[1], x.shape[2]
    FF2 = w_up.shape[2] * N_CHIPS
    FF = FF2 // 2
    L = DEPTH
    lay = _h_layout(D)
    NP = lay["total"]
    cx, cy, cc = _coords()
    chip = 2 * cx + cy
    T = _tile(S, 512)
    TM = _tile(S, 256, 16)
    TMW = _tile(S, 64, 16)

    shards = {k: tuple(W[k].shape) for k in BIG}
    full = {k: [None] * L for k in BIG}
    own = {(l, k): _cast_into_slot(W[k], l, chip, f"cast_{k}_l{l}") for l in range(L) for k in BIG}

    on_way = {"pairs": [], "arrived": []}

    def lay_out(pairs, gathered):
        for (l_, k), g in zip(pairs, gathered):
            _, r, c_ = shards[k]
            full[k][l_] = g.reshape(N_CHIPS * r, c_) if k in ROW_SHARDED else jnp.transpose(g, (1, 0, 2)).reshape(r, N_CHIPS * c_)

    def gather_behind(pairs):
        to_sib = _gather_stages(on_way["arrived"])[1](on_way["arrived"]) if on_way["pairs"] else None
        bufs = [own[p_] for p_ in pairs]
        return _both(to_sib, _gather_stages(bufs)[0](bufs) if pairs else None)

    def gathered_behind(pairs, outs):
        n_done = len(on_way["pairs"])
        lay_out(on_way["pairs"], outs[:n_done])
        on_way["pairs"], on_way["arrived"] = list(pairs), list(outs[n_done:])

    def mm_behind(a_, b_, mode, dt, name, pairs):
        stage = gather_behind(pairs)
        if stage is None:
            return _mm(a_, b_, mode, dt, name)
        out, outs = _mm(a_, b_, mode, dt, name, carry=stage)
        gathered_behind(pairs, outs)
        return out

    MIX = ["w_uq", "w_ukv", "w_proj_a", "w_proj_b", "w_proj_c", "w_o"]
    nxt = lambda l, names: [(l + 1, k) for k in names] if l + 1 < L else []
    first = [(0, "w_in")]
    gathered_behind(first, _run_stage(gather_behind(first), "gather_ici_w_in_l0"))
    gathered_behind([], _run_stage(gather_behind([]), "gather_sibling_w_in_l0"))

    small_sharded_full = {k: tuple(W[k].shape[:-1]) + (W[k].shape[-1] * N_CHIPS,) for k in SMALL_SHARDED}
    placed = []
    for k in ("b_gate", "conv_w"):
        z = jnp.zeros(small_sharded_full[k], F32)
        z = lax.dynamic_update_slice_in_dim(z, W[k], chip * W[k].shape[-1], axis=-1)
        placed.append(jnp.where(cc == 0, z, 0.0).reshape(-1))
    pv = jnp.concatenate(placed)
    n_pv = pv.shape[0]
    pv = jnp.pad(pv, (0, -n_pv % (8 * LANES))).reshape(-1, LANES)
    pv = _all_reduce_small(pv).reshape(-1)
    nb_ = int(np.prod(small_sharded_full["b_gate"]))
    b_gate_full = pv[:nb_].reshape(small_sharded_full["b_gate"])
    conv_w_full = pv[nb_:n_pv].reshape(small_sharded_full["conv_w"])

    inv_freq = ROPE_THETA ** (-jnp.arange(0, MLA_ROPE, 2, dtype=F32) / MLA_ROPE)
    ang = positions[0].astype(F32)[:, None] * inv_freq
    cos, sin = jnp.cos(ang), jnp.sin(ang)
    cos_t = jnp.concatenate([cos, cos, jnp.ones((S, LANES - MLA_ROPE), F32)], axis=1)
    sin_t = jnp.concatenate([sin, sin, jnp.zeros((S, LANES - MLA_ROPE), F32)], axis=1)

    row = lambda v: v.reshape(1, -1)
    cb = lambda name: lay[name][0] // lay[name][1]

    xs = x[0]
    saved = []
    for l in range(L):
        p = dict(
            w_in=_perm_w_in(full["w_in"][l], lay),
            sinks=row(sinks[l]), qg=row(q_norm_g[l]), kvg=row(kv_norm_g[l]), sg=row(sgu_ln_g[l]), sb=row(sgu_ln_b[l]),
            sw=sgu_w[l], sb3=sgu_b[l].reshape(SGU_GROUPS, SGU_CHUNK, 1),
            bg=b_gate_full[l], l1g=row(ln1_g[l]), l1b=row(ln1_b[l]), cw=conv_w_full[l], cbias=row(conv_b[l]),
            l2g=row(ln2_g[l]), l2b=row(ln2_b[l]))
        if l == 0:
            def fn_cast(i, rows, ps):
                return [rows[0]], []
            (xb,), _ = _rowwise(fn_cast, [_whole(xs)], [], [(D, BF16)], [], TM, "cast_x")
        own_mix = [(l, k) for k in MIX] if l == 0 else []
        own_up = [(l, "w_up")] if l == 0 else []
        own_down = [(l, "w_down")] if l == 0 else []
        h = mm_behind(xb, p["w_in"], "nn", BF16, "mm_h", own_mix)
        (y_a,), outs = _swa_fwd(h, cb("qa"), cb("ka"), cb("va"), p["sinks"], carry=gather_behind(own_up))
        gathered_behind(own_up, outs)
        p.update(w_uq=_perm_w_uq(full["w_uq"][l]), w_ukv=_perm_w_ukv(full["w_ukv"][l]), w_pa=full["w_proj_a"][l],
                 w_pb=full["w_proj_b"][l], w_pc=full["w_proj_c"][l], w_o=full["w_o"][l])
        def fn_rms(i, rows, ps):
            return [_rms_norm(rows[0].astype(F32), ps[0]), _rms_norm(rows[1].astype(F32), ps[1])], []
        (cqn, ckvn), _ = _rowwise(fn_rms, [(h, MLA_Q_RANK, cb("cq")), (h, MLA_KV_RANK, cb("ckv"))], [p["qg"], p["kvg"]],
                                  [(MLA_Q_RANK, BF16), (MLA_KV_RANK, BF16)], [], TM, "mla_rms")
        q_full = _mm(cqn, p["w_uq"], "nn", BF16, "mm_q")
        kv = _mm(ckvn, p["w_ukv"], "nn", BF16, "mm_kv")
        qr, kr = _rope_call(q_full, MLA_HEADS * LANES, 1, h, LANES, cb("kr"), cos_t, sin_t, 1.0, TM, "rope_fwd")
        behind_mla = own_down + nxt(l, ["w_in"] + MIX)
        (y_b, lse), outs = _mla_fwd(q_full, qr, kv, kr, T, carry=gather_behind(behind_mla))
        gathered_behind(behind_mla, outs)
        y_c = _sgu_fwd(h, cb("hu"), cb("hv"), p["sg"], p["sb"], p["sw"], p["sb3"])
        pa = _mm(y_a, p["w_pa"], "nn", F32, "mm_pa")
        pb = _mm(y_b, p["w_pb"], "nn", F32, "mm_pb")
        pc = _mm(y_c, p["w_pc"], "nn", F32, "mm_pc")

        def merge_math(pa_, pb_, pc_, g_, b0, b1, b2):
            out = 0.0
            for br, (pp, bb) in enumerate(zip((pa_, pb_, pc_), (b0, b1, b2))):
                gate = jax.nn.sigmoid(g_[:, br * D:(br + 1) * D].astype(F32) + bb)
                out = out + gate * pp
            return out

        def fn_merge(i, rows, ps):
            bgv = ps[0]
            return [merge_math(rows[0], rows[1], rows[2], rows[3], bgv[0:1], bgv[1:2], bgv[2:3])], []
        (merged,), _ = _rowwise(fn_merge, [_whole(pa), _whole(pb), _whole(pc), (h, N_BRANCHES * D, cb("g"))], [p["bg"]],
                                [(D, BF16)], [], TMW, "merge_fwd")
        o = _mm(merged, p["w_o"], "nn", F32, "mm_o")

        def ln_res_math(x_, o_, g_, b_):
            return _layer_norm(DN_ALPHA * x_ + o_, g_, b_)

        def fn_ln(i, rows, ps):
            y = ln_res_math(rows[0], rows[1], ps[0], ps[1])
            return [y, y], []
        (x1, x1b), _ = _rowwise(fn_ln, [_whole(xs), _whole(o)], [p["l1g"], p["l1b"]], [(D, F32), (D, BF16)], [], TM, "ln1_fwd")
        p.update(w_up=full["w_up"][l])
        up = mm_behind(x1b, p["w_up"], "nn", BF16, "mm_up", nxt(l, ["w_up"]))
        p.update(w_down=full["w_down"][l])
        cv_ = _conv_fwd(up, p["cw"], p["cbias"])

        def glu_math(cg, cvv):
            return jax.nn.silu(cg.astype(F32)) * cvv.astype(F32)

        def fn_glu(i, rows, ps):
            return [glu_math(rows[0], rows[1])], []
        (act,), _ = _rowwise(fn_glu, [(cv_, FF, 0), (cv_, FF, 1)], [], [(FF, BF16)], [], TMW, "glu_fwd")
        dn = mm_behind(act, p["w_down"], "nn", F32, "mm_down", nxt(l, ["w_down"]))
        (x2, x2b), _ = _rowwise(fn_ln, [_whole(x1), _whole(dn)], [p["l2g"], p["l2b"]], [(D, F32), (D, BF16)], [], TM, "ln2_fwd")
        saved.append(dict(p=p, x0=xs, x0b=xb, h=h, y_a=y_a, cqn=cqn, ckvn=ckvn, q_full=q_full, kv=kv, qr=qr, kr=kr, y_b=y_b,
                          lse=lse, y_c=y_c, pa=pa, pb=pb, pc=pc, merged=merged, o=o, x1=x1, x1b=x1b, up=up, cv=cv_, act=act,
                          dn=dn))
        xs, xb = x2, x2b

    def fn_loss(i, rows, ps):
        diff = rows[0] - rows[1]
        part = jnp.sum(jnp.mean(jnp.square(diff), axis=-1, keepdims=True), axis=0, keepdims=True)
        return [diff * (1.0 / D)], [jnp.broadcast_to(part, (8, LANES))]
    (dx,), (loss_acc,) = _rowwise(fn_loss, [_whole(xs), _whole(loss_target[0])], [], [(D, F32)], [(8, LANES)], TM, "loss")
    loss = lax.psum(0.5 * loss_acc[0, 0], ("x", "y", "c"))

    gbig = {k: [None] * L for k in BIG}
    gsmall = {k: [None] * L for k in SMALL}
    reduced = {}
    pending = None
    for l in reversed(range(L)):
        s = saved[l]
        p = s["p"]

        def fn_ln_bwd(i, rows, ps):
            _, vjp = jax.vjp(ln_res_math, rows[0], rows[1], ps[0], ps[1])
            dx_, do_, dg_, db_ = vjp(rows[2])
            return [dx_, do_], [dg_, db_]
        (dx1_res, ddn), (g_l2g, g_l2b) = _rowwise(fn_ln_bwd, [_whole(s["x1"]), _whole(s["dn"]), _whole(dx)], [p["l2g"], p["l2b"]],
                                                  [(D, F32), (D, BF16)], [(1, D), (1, D)], TM, "ln2_bwd")
        gsmall["ln2_g"][l], gsmall["ln2_b"][l] = g_l2g, g_l2b
        row_chunks = lambda g: g.reshape(N_CHIPS, g.shape[0] // N_CHIPS, g.shape[1])
        gbig["w_down"][l] = row_chunks(_mm(s["act"], ddn, "tn", F32, "mm_dw_down"))
        dact = _mm(ddn, p["w_down"], "nt", BF16, "mm_dact")

        def fn_glu_bwd(i, rows, ps):
            _, vjp = jax.vjp(glu_math, rows[0], rows[1])
            dcg, dcv = vjp(rows[2].astype(F32))
            return [jnp.concatenate([dcg, dcv], axis=1)], []
        (dc,), _ = _rowwise(fn_glu_bwd, [(s["cv"], FF, 0), (s["cv"], FF, 1), _whole(dact)], [], [(FF2, BF16)], [], TMW, "glu_bwd")
        dup, g_cw, g_cb = _conv_bwd(s["up"], dc, p["cw"])
        gsmall["conv_w"][l], gsmall["conv_b"][l] = g_cw, g_cb
        if pending is None:
            gbig["w_up"][l] = _mm(s["x1b"], dup, "tn", F32, "mm_dw_up", col_chunks=N_CHIPS, tm=2048)
        else:
            gbig["w_up"][l], from_sibling = _mm(s["x1b"], dup, "tn", F32, "mm_dw_up", col_chunks=N_CHIPS, tm=2048,
                                                carry=_swap_stage(pending))
            partial = [_add_half(g, r, cc, f"rs_add_{k}_l{l + 1}") for k, g, r in zip(BIG, pending, from_sibling)]
        dx1 = _mm(dup, p["w_up"], "nt", F32, "mm_dx1", add=dx1_res)
        (dx0_res, do_), (g_l1g, g_l1b) = _rowwise(fn_ln_bwd, [_whole(s["x0"]), _whole(s["o"]), _whole(dx1)], [p["l1g"], p["l1b"]],
                                                  [(D, F32), (D, BF16)], [(1, D), (1, D)], TM, "ln1_bwd")
        gsmall["ln1_g"][l], gsmall["ln1_b"][l] = g_l1g, g_l1b
        gbig["w_o"][l] = row_chunks(_mm(s["merged"], do_, "tn", F32, "mm_dw_o"))
        dmerged = _mm(do_, p["w_o"], "nt", F32, "mm_dmerged")

        def fn_merge_bwd(i, rows, ps):
            bgv = ps[0]
            _, vjp = jax.vjp(merge_math, rows[0], rows[1], rows[2], rows[3], bgv[0:1], bgv[1:2], bgv[2:3])
            dpa, dpb, dpc, dg_, db0, db1, db2 = vjp(rows[4])
            return [dpa, dpb, dpc, dg_], [db0, db1, db2]
        (dpa, dpb, dpc, dgl), (db0, db1, db2) = _rowwise(
            fn_merge_bwd, [_whole(s["pa"]), _whole(s["pb"]), _whole(s["pc"]), (s["h"], N_BRANCHES * D, cb("g")), _whole(dmerged)],
            [p["bg"]], [(D, BF16), (D, BF16), (D, BF16), (N_BRANCHES * D, BF16)], [(1, D)] * 3, TMW, "merge_bwd")
        gsmall["b_gate"][l] = jnp.concatenate([db0, db1, db2], axis=0)
        gbig["w_proj_a"][l] = _mm(s["y_a"], dpa, "tn", F32, "mm_dw_pa", col_chunks=N_CHIPS)
        gbig["w_proj_b"][l] = row_chunks(_mm(s["y_b"], dpb, "tn", F32, "mm_dw_pb"))
        gbig["w_proj_c"][l] = _mm(s["y_c"], dpc, "tn", F32, "mm_dw_pc", col_chunks=N_CHIPS)
        dy_a = _mm(dpa, p["w_pa"], "nt", BF16, "mm_dy_a")
        dy_b = _mm(dpb, p["w_pb"], "nt", BF16, "mm_dy_b")
        dy_c = _mm(dpc, p["w_pc"], "nt", BF16, "mm_dy_c")
        dh_c, g_sg, g_sb, g_sw, g_sb3 = _sgu_bwd(s["h"], cb("hu"), cb("hv"), dy_c, p["sg"], p["sb"], p["sw"], p["sb3"])
        gsmall["sgu_ln_g"][l], gsmall["sgu_ln_b"][l], gsmall["sgu_w"][l] = g_sg, g_sb, g_sw
        gsmall["sgu_b"][l] = g_sb3.reshape(SGU_GROUPS, SGU_CHUNK)
        part_x = ["w_up", "w_down"]
        part_y = [k for k in BIG if k not in part_x]
        scatter_of = lambda names: None if pending is None else _scatter_stage([partial[BIG.index(k)][1] for k in names])
        hide_early = l == 0
        early = [gbig[k][l] for k in EARLY]
        n_early = len(EARLY) if hide_early else 0
        (dqa, dka, dva, g_sinks), landed_ = _swa_bwd(s["h"], cb("qa"), cb("ka"), cb("va"), p["sinks"], dy_a,
                                                     carry=_both(_swap_stage(early) if hide_early else None, scatter_of(part_x)))
        early_from_sibling, arrived_x = landed_[:n_early], landed_[n_early:]
        gsmall["sinks"][l] = g_sinks
        if hide_early:
            early_partial = [_add_half(g, r, cc, f"rs_add_{k}_l{l}") for k, g, r in zip(EARLY, early, early_from_sibling)]
        delta = _mla_delta(dy_b, s["y_b"], T)
        (dqn, dqr), arrived_y = _mla_bwd_dq(s["q_full"], s["qr"], s["kv"], s["kr"], dy_b, s["lse"], delta, T,
                                            carry=scatter_of(part_y))
        if pending is not None:
            arrived = {**dict(zip(part_x, arrived_x)), **dict(zip(part_y, arrived_y))}
            for k, (p32, _) in zip(BIG, partial):
                reduced[k] = _sum_into(p32, arrived[k], chip, cc, l + 1, L, reduced.get(k), f"rs_sum_{k}_l{l + 1}")
        (dkn, dv, dkr_heads), early_arrived = _mla_bwd_dkv(
            s["q_full"], s["qr"], s["kv"], s["kr"], dy_b, s["lse"], delta, T,
            carry=_scatter_stage([p16 for _, p16 in early_partial]) if hide_early else None)
        if hide_early:
            for k, (p32, _), arr in zip(EARLY, early_partial, early_arrived):
                reduced[k] = _sum_into(p32, arr, chip, cc, l, L, reduced.get(k), f"rs_sum_{k}_l{l}")
        dkr = _sum_heads(dkr_heads, TM)
        dqr_raw, dkr_raw = _rope_call(dqr, MLA_HEADS * LANES, 0, dkr, LANES, 0, cos_t, sin_t, -1.0, TM, "rope_bwd")
        dq_full = jnp.concatenate([dqn, dqr_raw], axis=1)
        dkv = jnp.concatenate([dkn, dv], axis=1)
        gbig["w_uq"][l] = _col_chunks(_unperm_w_uq(_mm(s["cqn"], dq_full, "tn", F32, "mm_dw_uq")))
        gbig["w_ukv"][l] = _col_chunks(_unperm_w_ukv(_mm(s["ckvn"], dkv, "tn", F32, "mm_dw_ukv")))
        dcqn = _mm(dq_full, p["w_uq"], "nt", F32, "mm_dcqn")
        dckvn = _mm(dkv, p["w_ukv"], "nt", F32, "mm_dckvn")

        def fn_rms_bwd(i, rows, ps):
            _, vjp1 = jax.vjp(lambda c_, g_: _rms_norm(c_.astype(F32), g_), rows[0], ps[0])
            _, vjp2 = jax.vjp(lambda c_, g_: _rms_norm(c_.astype(F32), g_), rows[1], ps[1])
            d1, dg1 = vjp1(rows[2])
            d2, dg2 = vjp2(rows[3])
            return [d1, d2], [dg1, dg2]
        (dcq, dckv), (g_qg, g_kvg) = _rowwise(
            fn_rms_bwd, [(s["h"], MLA_Q_RANK, cb("cq")), (s["h"], MLA_KV_RANK, cb("ckv")), _whole(dcqn), _whole(dckvn)],
            [p["qg"], p["kvg"]], [(MLA_Q_RANK, BF16), (MLA_KV_RANK, BF16)], [(1, MLA_Q_RANK), (1, MLA_KV_RANK)], TM, "mla_rms_bwd")
        gsmall["q_norm_g"][l], gsmall["kv_norm_g"][l] = g_qg, g_kvg
        tail = jnp.zeros((S, NP - lay["kr"][0] - LANES), BF16)
        dh = jnp.concatenate([dgl, dqa, dcq, dckv, dh_c, dka, dva, dkr_raw, tail], axis=1)
        if l == 0:
            done = [list(range(L)) if k in EARLY else list(range(1, L)) for k in BIG]
            dw_in, joined = _mm(s["x0b"], dh, "tn", F32, "mm_dw_in", carry=_join_stage([reduced[k] for k in BIG], done))
            reduced.update(zip(BIG, joined))
        else:
            dw_in = _mm(s["x0b"], dh, "tn", F32, "mm_dw_in")
        gbig["w_in"][l] = _col_chunks(_unperm_w_in(dw_in, lay, D))
        dx = _mm(dh, p["w_in"], "nt", F32, "mm_dx0", add=dx0_res)

        pending = [gbig[k][l] for k in BIG]

    late = [gbig[k][0] for k in LATE]
    from_sibling = _run_stage(_swap_stage(late), "rs_swap_halves_l0")
    partial = [_add_half(g, r, cc, f"rs_add_{k}_l0") for k, g, r in zip(LATE, late, from_sibling)]
    arrived = _run_stage(_scatter_stage([p16 for _, p16 in partial]), "rs_scatter_chips_l0")
    for k, (p32, _), arr in zip(LATE, partial, arrived):
        reduced[k] = _sum_into(p32, arr, chip, cc, 0, L, reduced.get(k), f"rs_sum_{k}_l0")

    grad_x = dx.reshape(x.shape)
    g_big = {**reduced, **dict(zip(LATE, _run_stage(_join_stage([reduced[k] for k in LATE], [[0]] * len(LATE)), "rs_join_halves")))}

    small_shapes = {k: (small_sharded_full[k] if k in SMALL_SHARDED else tuple(W[k].shape)) for k in SMALL}
    sv = jnp.concatenate([jnp.stack(gsmall[k]).reshape(-1) for k in SMALL])
    n_sv = sv.shape[0]
    sv = jnp.pad(sv, (0, -n_sv % (8 * LANES))).reshape(-1, LANES)
    sv = _all_reduce_small(sv).reshape(-1)
    g_small, o_ = {}, 0
    for k in SMALL:
        n = int(np.prod(small_shapes[k]))
        g = sv[o_:o_ + n].reshape(small_shapes[k])
        if k in SMALL_SHARDED:
            g = lax.dynamic_slice_in_dim(g, chip * W[k].shape[-1], W[k].shape[-1], axis=-1)
        g_small[k] = g
        o_ += n

    delta, new_m, new_v = {}, {}, {}
    swap_minor = lambda t: jnp.transpose(t, (0, 2, 1))
    for k in BIG:
        if shards[k][2] % LANES and not shards[k][1] % LANES:
            out = _adamw(swap_minor(W[k]), swap_minor(g_big[k]), swap_minor(Mo[k]), swap_minor(Vo[k]), "adamw_" + k)
            delta[k], new_m[k], new_v[k] = [swap_minor(t) for t in out]
        else:
            delta[k], new_m[k], new_v[k] = _adamw(W[k], g_big[k], Mo[k], Vo[k], "adamw_" + k)
    pack = lambda t: jnp.concatenate([t[k].reshape(-1) for k in SMALL])
    n_small = sum(int(np.prod(W[k].shape)) for k in SMALL)
    pad2 = lambda t: jnp.pad(t, (0, -n_small % (8 * LANES))).reshape(1, -1, LANES)
    d_, m_, v_ = _adamw(pad2(pack(W)), pad2(pack(g_small)), pad2(pack(Mo)), pad2(pack(Vo)), "adamw_small")
    o_ = 0
    for k in SMALL:
        n = int(np.prod(W[k].shape))
        take = lambda t: t.reshape(-1)[o_:o_ + n].reshape(W[k].shape)
        delta[k], new_m[k], new_v[k] = take(d_), take(m_), take(v_)
        o_ += n

    grads = {**g_big, **g_small}
    return (loss, grad_x, *[grads[k] for k in WEIGHTS], *[delta[k] for k in WEIGHTS], *[new_m[k] for k in WEIGHTS],
            *[new_v[k] for k in WEIGHTS])
```

```python
import functools
import math

import jax
import jax.numpy as jnp
import numpy as np
from jax import lax
from jax.experimental import pallas as pl
from jax.experimental.pallas import tpu as pltpu

F32, BF16 = jnp.float32, jnp.bfloat16
SDS = jax.ShapeDtypeStruct
MESH = pl.DeviceIdType.MESH

SWA_Q_HEADS, SWA_KV_HEADS, SWA_HEAD_DIM, SWA_BLOCK = 16, 2, 64, 128
MLA_HEADS, MLA_NOPE, MLA_ROPE, MLA_V = 16, 128, 64, 128
MLA_Q_RANK, MLA_KV_RANK = 512, 512
ROPE_THETA = 10000.0
SGU_GROUPS, SGU_GROUP_DIM, SGU_CHUNK = 8, 128, 128
SGU_WIDTH = SGU_GROUPS * SGU_GROUP_DIM
A_Q = SWA_Q_HEADS * SWA_HEAD_DIM
A_KV = SWA_KV_HEADS * SWA_HEAD_DIM
N_BRANCHES = 3
DEPTH = 2
EPS = 1e-5
MASK_VALUE = -1e30
DN_ALPHA = (2 * DEPTH) ** 0.25
ADAM_LR, ADAM_B1, ADAM_B2, ADAM_EPS, ADAM_WD, ADAM_STEP = 0.001, 0.9, 0.999, 1e-08, 0.01, 10
N_CHIPS = 4

LANES = 128
VMEM_LIMIT = 48 * 1024 * 1024

BIG = ["w_in", "w_uq", "w_ukv", "w_proj_a", "w_proj_b", "w_proj_c", "w_o", "w_up", "w_down"]
ROW_SHARDED = {"w_proj_b", "w_o", "w_down"}
LATE = ["w_in", "w_uq", "w_ukv"]
EARLY = [k for k in BIG if k not in LATE]
SMALL = ["b_gate", "sinks", "q_norm_g", "kv_norm_g", "sgu_ln_g", "sgu_ln_b", "sgu_w", "sgu_b", "ln1_g", "ln1_b",
         "conv_w", "conv_b", "ln2_g", "ln2_b"]
SMALL_SHARDED = {"b_gate", "conv_w"}
WEIGHTS = ["w_in", "b_gate", "sinks", "q_norm_g", "kv_norm_g", "w_uq", "w_ukv", "sgu_ln_g", "sgu_ln_b", "sgu_w", "sgu_b",
           "w_proj_a", "w_proj_b", "w_proj_c", "w_o", "ln1_g", "ln1_b", "w_up", "conv_w", "conv_b", "w_down", "ln2_g", "ln2_b"]


def _pcall(body, **kw):
    return pl.pallas_call(body, **kw)


def _params(sem=None):
    return pltpu.CompilerParams(dimension_semantics=sem, vmem_limit_bytes=VMEM_LIMIT)


def _tile(dim, pref, align=LANES):
    t = (min(pref, dim) // align) * align
    while t >= align:
        if dim % t == 0:
            return t
        t -= align
    return dim


def _mm(a, b, mode, out_dtype, name, add=None, tm=1024, tn=512, tk=2048, col_chunks=1, carry=None):
    if mode == "nn":
        (M, K), (K2, N) = a.shape, b.shape
    elif mode == "nt":
        (M, K), (N, K2) = a.shape, b.shape
    else:
        (K, M), (K2, N) = a.shape, b.shape
    assert K == K2, (a.shape, b.shape, mode)
    assert N % col_chunks == 0
    tm, tn, tk = _tile(M, tm), _tile(N // col_chunks, tn), _tile(K, tk)
    assert (N // col_chunks) % tn == 0
    per_chunk = (N // col_chunks) // tn
    nk = K // tk
    if mode == "tn":
        a_spec = pl.BlockSpec((tk, tm), lambda i, j, k: (k, i))
    else:
        a_spec = pl.BlockSpec((tm, tk), lambda i, j, k: (i, k))
    if mode == "nt":
        b_spec = pl.BlockSpec((tn, tk), lambda i, j, k: (j, k))
    else:
        b_spec = pl.BlockSpec((tk, tn), lambda i, j, k: (k, j))
    dn = {"nn": (((1,), (0,)), ((), ())), "nt": (((1,), (1,)), ((), ())), "tn": (((0,), (0,)), ((), ()))}[mode]
    chunked = col_chunks > 1
    if chunked:
        assert add is None
        o_spec = pl.BlockSpec((1, tm, tn), lambda i, j, k: (lax.div(j, per_chunk), i, lax.rem(j, per_chunk)))
        out_shape = SDS((col_chunks, M, N // col_chunks), out_dtype)
    else:
        o_spec = pl.BlockSpec((tm, tn), lambda i, j, k: (i, j))
        out_shape = SDS((M, N), out_dtype)
    has_add = add is not None

    def body(*refs):
        if has_add:
            a_ref, b_ref, add_ref, o_ref, acc_ref = refs
        else:
            a_ref, b_ref, o_ref, acc_ref = refs
        k = pl.program_id(2)

        @pl.when(k == 0)
        def _():
            acc_ref[...] = jnp.zeros_like(acc_ref)

        acc_ref[...] += lax.dot_general(a_ref[...].astype(BF16), b_ref[...].astype(BF16), dn,
                                        preferred_element_type=F32)

        @pl.when(k == nk - 1)
        def _():
            r = acc_ref[...]
            if has_add:
                r = r + add_ref[...].astype(F32)
            if chunked:
                o_ref[0] = r.astype(o_ref.dtype)
            else:
                o_ref[...] = r.astype(o_ref.dtype)

    ins = [a, b] + ([add] if has_add else [])
    in_specs = [a_spec, b_spec] + ([o_spec] if has_add else [])
    (out,), carried = _call(body, name, (M // tm, N // tn, nk), 0, in_specs, [o_spec], [out_shape], [pltpu.VMEM((tm, tn), F32)],
                            ins, ("parallel", "parallel", "arbitrary"), carry)
    return out if carry is None else (out, carried)


def _rowwise(fn, rows, params, row_outs, acc_outs, tm, name):
    n_rows = rows[0][0].shape[0]
    assert n_rows % tm == 0
    nr, npar, no = len(rows), len(params), len(row_outs)

    def body(*refs):
        i = pl.program_id(0)
        r, p = refs[:nr], refs[nr:nr + npar]
        o, acc = refs[nr + npar:nr + npar + no], refs[nr + npar + no:]
        outs, sums = fn(i, [x[...] for x in r], [x[...] for x in p])
        for ref, val in zip(o, outs, strict=True):
            ref[...] = val.astype(ref.dtype)
        if acc:
            @pl.when(i == 0)
            def _():
                for ref in acc:
                    ref[...] = jnp.zeros_like(ref)
            for ref, val in zip(acc, sums, strict=True):
                ref[...] += val.astype(F32)

    def full(shape):
        nd = len(shape)
        return pl.BlockSpec(tuple(shape), lambda i: (0,) * nd)

    in_specs = [pl.BlockSpec((tm, w), (lambda i, cb=cb: (i, cb))) for (_, w, cb) in rows] + [full(p.shape) for p in params]
    out_specs = [pl.BlockSpec((tm, w), lambda i: (i, 0)) for (w, _) in row_outs] + [full(s) for s in acc_outs]
    out_shape = [SDS((n_rows, w), dt) for (w, dt) in row_outs] + [SDS(tuple(s), F32) for s in acc_outs]
    res = _pcall(body, name=name, out_shape=out_shape, grid=(n_rows // tm,), in_specs=in_specs, out_specs=out_specs,
                 compiler_params=_params(("arbitrary",)))(*[r[0] for r in rows], *params)
    return list(res[:no]), list(res[no:])


def _whole(a):
    return (a, a.shape[1], 0)


def _gelu(x):
    return 0.5 * x * (1.0 + lax.erf(x * (1.0 / math.sqrt(2.0))))


def _layer_norm(x, g, b):
    mu = x.mean(-1, keepdims=True)
    var = jnp.mean(jnp.square(x - mu), -1, keepdims=True)
    return (x - mu) * lax.rsqrt(var + EPS) * g + b


def _rms_norm(x, g):
    return x * lax.rsqrt(jnp.mean(jnp.square(x), -1, keepdims=True) + EPS) * g


def _sgu_math(hu, hv, ln_g, ln_b, ws, bs):
    u = _gelu(hu.astype(F32))
    vn = _layer_norm(_gelu(hv.astype(F32)), ln_g, ln_b)
    r = lax.broadcasted_iota(jnp.int32, (SGU_CHUNK, SGU_CHUNK), 0)
    c = lax.broadcasted_iota(jnp.int32, (SGU_CHUNK, SGU_CHUNK), 1)
    outs = []
    for g in range(SGU_GROUPS):
        w = jnp.where(r >= c, ws[g], 0.0).astype(BF16)
        vg = vn[:, g * SGU_GROUP_DIM:(g + 1) * SGU_GROUP_DIM].astype(BF16)
        outs.append(jnp.dot(w, vg, preferred_element_type=F32) + bs[g])
    return u * jnp.concatenate(outs, axis=1)


def _sgu_fwd(h, cu, cv, ln_g, ln_b, w, b3):
    def fn(i, rows, ps):
        g_, b_, w_, b3_ = ps
        y = _sgu_math(rows[0], rows[1], g_, b_, [w_[g] for g in range(SGU_GROUPS)], [b3_[g] for g in range(SGU_GROUPS)])
        return [y], []
    (y,), _ = _rowwise(fn, [(h, SGU_WIDTH, cu), (h, SGU_WIDTH, cv)], [ln_g, ln_b, w, b3], [(SGU_WIDTH, BF16)], [],
                       SGU_CHUNK, "sgu_fwd")
    return y


def _sgu_bwd(h, cu, cv, dy, ln_g, ln_b, w, b3):
    nd = 2 * SGU_WIDTH

    def body(hu_ref, hv_ref, dy_ref, g_ref, b_ref, w_ref, b3_ref, dh_ref, dg_ref, db_ref, dw_ref, db3_ref):
        i = pl.program_id(0)

        @pl.when(i == 0)
        def _():
            dg_ref[...] = jnp.zeros_like(dg_ref)
            db_ref[...] = jnp.zeros_like(db_ref)
            dw_ref[...] = jnp.zeros_like(dw_ref)
            db3_ref[...] = jnp.zeros_like(db3_ref)

        ws = [w_ref[g] for g in range(SGU_GROUPS)]
        bs = [b3_ref[g] for g in range(SGU_GROUPS)]
        _, vjp = jax.vjp(_sgu_math, hu_ref[...], hv_ref[...], g_ref[...], b_ref[...], ws, bs)
        dhu, dhv, dg, db, dws, dbs = vjp(dy_ref[...].astype(F32))
        dh_ref[...] = jnp.concatenate([dhu, dhv], axis=1).astype(dh_ref.dtype)
        dg_ref[...] += dg
        db_ref[...] += db
        for g in range(SGU_GROUPS):
            dw_ref[g] += dws[g]
            db3_ref[g] += dbs[g]

    n = h.shape[0]
    blk = lambda cb: pl.BlockSpec((SGU_CHUNK, SGU_WIDTH), lambda i, cb=cb: (i, cb))
    full = lambda s: pl.BlockSpec(tuple(s), lambda i: (0,) * len(s))
    return _pcall(
        body, name="sgu_bwd", grid=(n // SGU_CHUNK,),
        out_shape=[SDS((n, nd), BF16), SDS(ln_g.shape, F32), SDS(ln_b.shape, F32), SDS(w.shape, F32), SDS(b3.shape, F32)],
        in_specs=[blk(cu), blk(cv), blk(0), full(ln_g.shape), full(ln_b.shape), full(w.shape), full(b3.shape)],
        out_specs=[pl.BlockSpec((SGU_CHUNK, nd), lambda i: (i, 0)), full(ln_g.shape), full(ln_b.shape), full(w.shape),
                   full(b3.shape)],
        compiler_params=_params(("arbitrary",)))(h, h, dy, ln_g, ln_b, w, b3)


def _swa_math(q, kp, kc, vp, vc, sinks, not_first):
    kw = jnp.concatenate([kp, kc], axis=0).astype(BF16)
    vw = jnp.concatenate([vp, vc], axis=0).astype(BF16)
    qb = q.astype(BF16)
    q_off = lax.broadcasted_iota(jnp.int32, (SWA_BLOCK, 2 * SWA_BLOCK), 0) + SWA_BLOCK
    k_off = lax.broadcasted_iota(jnp.int32, (SWA_BLOCK, 2 * SWA_BLOCK), 1)
    rel = q_off - k_off
    valid = (rel >= 0) & (rel < SWA_BLOCK) & (not_first | (k_off >= SWA_BLOCK))
    G = SWA_Q_HEADS // SWA_KV_HEADS
    outs = []
    for head in range(SWA_Q_HEADS):
        hk = head // G
        qh = qb[:, head * SWA_HEAD_DIM:(head + 1) * SWA_HEAD_DIM]
        kh = kw[:, hk * SWA_HEAD_DIM:(hk + 1) * SWA_HEAD_DIM]
        vh = vw[:, hk * SWA_HEAD_DIM:(hk + 1) * SWA_HEAD_DIM]
        s = lax.dot_general(qh, kh, (((1,), (1,)), ((), ())), preferred_element_type=F32) * (SWA_HEAD_DIM ** -0.5)
        s = jnp.where(valid, s, MASK_VALUE)
        sink = sinks[:, head:head + 1]
        m = jnp.maximum(s.max(-1, keepdims=True), sink)
        p = jnp.exp(s - m)
        p = (p / (p.sum(-1, keepdims=True) + jnp.exp(sink - m))).astype(BF16)
        outs.append(jnp.dot(p, vh, preferred_element_type=F32))
    return jnp.concatenate(outs, axis=1)


def _swa_fwd(h, cq, ck, cv, sinks, carry=None):
    n = h.shape[0]
    nb = n // SWA_BLOCK

    def body(q_ref, kp_ref, kc_ref, vp_ref, vc_ref, s_ref, o_ref):
        i = pl.program_id(0)
        f = lambda x: x[...].astype(F32)
        o_ref[...] = _swa_math(f(q_ref), f(kp_ref), f(kc_ref), f(vp_ref), f(vc_ref), s_ref[...], i > 0).astype(o_ref.dtype)

    prev = lambda cb: pl.BlockSpec((SWA_BLOCK, A_KV), lambda i, cb=cb: (jnp.maximum(i - 1, 0), cb))
    cur = lambda cb: pl.BlockSpec((SWA_BLOCK, A_KV), lambda i, cb=cb: (i, cb))
    return _call(body, "swa_fwd", (nb,), 0,
                 [pl.BlockSpec((SWA_BLOCK, A_Q), lambda i: (i, cq)), prev(ck), cur(ck), prev(cv), cur(cv),
                  pl.BlockSpec((1, SWA_Q_HEADS), lambda i: (0, 0))],
                 [pl.BlockSpec((SWA_BLOCK, A_Q), lambda i: (i, 0))], [SDS((n, A_Q), BF16)], [],
                 [h, h, h, h, h, sinks], ("arbitrary",), carry)


def _swa_bwd(h, cq, ck, cv, sinks, dy, carry=None):
    n = h.shape[0]
    nb = n // SWA_BLOCK

    def body(q_ref, kp_ref, kc_ref, vp_ref, vc_ref, s_ref, dy_ref, dq_ref, dk_ref, dv_ref, ds_ref, ck_ref, cv_ref):
        r = pl.program_id(0)
        blk = nb - 1 - r

        @pl.when(r == 0)
        def _():
            ds_ref[...] = jnp.zeros_like(ds_ref)
            ck_ref[...] = jnp.zeros_like(ck_ref)
            cv_ref[...] = jnp.zeros_like(cv_ref)

        f = lambda x: x[...].astype(F32)
        not_first = blk > 0
        _, vjp = jax.vjp(lambda q, kp, kc, vp, vc, s: _swa_math(q, kp, kc, vp, vc, s, not_first),
                         f(q_ref), f(kp_ref), f(kc_ref), f(vp_ref), f(vc_ref), s_ref[...])
        dq, dkp, dkc, dvp, dvc, dsk = vjp(f(dy_ref))
        dq_ref[...] = dq.astype(dq_ref.dtype)
        dk_ref[...] = (dkc + ck_ref[...]).astype(dk_ref.dtype)
        dv_ref[...] = (dvc + cv_ref[...]).astype(dv_ref.dtype)
        ck_ref[...] = dkp
        cv_ref[...] = dvp
        ds_ref[...] += dsk

    rev = lambda i: nb - 1 - i
    prev = lambda cb: pl.BlockSpec((SWA_BLOCK, A_KV), lambda i, cb=cb: (jnp.maximum(rev(i) - 1, 0), cb))
    cur = lambda cb: pl.BlockSpec((SWA_BLOCK, A_KV), lambda i, cb=cb: (rev(i), cb))
    return _call(
        body, "swa_bwd", (nb,), 0,
        [pl.BlockSpec((SWA_BLOCK, A_Q), lambda i: (rev(i), cq)), prev(ck), cur(ck), prev(cv), cur(cv),
         pl.BlockSpec((1, SWA_Q_HEADS), lambda i: (0, 0)), pl.BlockSpec((SWA_BLOCK, A_Q), lambda i: (rev(i), 0))],
        [pl.BlockSpec((SWA_BLOCK, A_Q), lambda i: (rev(i), 0)), pl.BlockSpec((SWA_BLOCK, A_KV), lambda i: (rev(i), 0)),
         pl.BlockSpec((SWA_BLOCK, A_KV), lambda i: (rev(i), 0)), pl.BlockSpec((1, SWA_Q_HEADS), lambda i: (0, 0))],
        [SDS((n, A_Q), BF16), SDS((n, A_KV), BF16), SDS((n, A_KV), BF16), SDS((1, SWA_Q_HEADS), F32)],
        [pltpu.VMEM((SWA_BLOCK, A_KV), F32), pltpu.VMEM((SWA_BLOCK, A_KV), F32)],
        [h, h, h, h, h, sinks, dy], ("arbitrary",), carry)


def _rope(x, cos, sin, sign):
    w = x.shape[1]
    reps = w // LANES
    ct = jnp.tile(cos, (1, reps)) if reps > 1 else cos
    st = jnp.tile(sin, (1, reps)) if reps > 1 else sin
    fwd = pltpu.roll(x, MLA_ROPE // 2, axis=1)
    bwd = pltpu.roll(x, w - MLA_ROPE // 2, axis=1)
    lane = lax.broadcasted_iota(jnp.int32, x.shape, 1) % LANES
    rot = jnp.where(lane < MLA_ROPE // 2, -bwd, fwd)
    return x * ct + sign * (rot * st)


def _rope_call(a, wa, ca, b, wb, cb, cos, sin, sign, tm, name):
    def fn(i, rows, ps):
        xa, xb, c_, s_ = rows
        return [_rope(xa.astype(F32), c_, s_, sign), _rope(xb.astype(F32), c_, s_, sign)], []
    (ra, rb), _ = _rowwise(fn, [(a, wa, ca), (b, wb, cb), _whole(cos), _whole(sin)], [], [(wa, BF16), (wb, BF16)], [], tm, name)
    return ra, rb


MLA_SCALE = (MLA_NOPE + MLA_ROPE) ** -0.5
LOG2E = math.log2(math.e)


MLA_HP = 4
MLA_W = MLA_HP * LANES


def _mla_scores(qn_ref, qr_ref, kn_ref, kr_ref, hh, masked):
    cols = slice(hh * LANES, (hh + 1) * LANES)
    q = jnp.concatenate([qn_ref[:, cols], qr_ref[:, cols]], axis=1)
    k = jnp.concatenate([kn_ref[:, cols], kr_ref[...]], axis=1)
    s = lax.dot_general(q, k, (((1,), (1,)), ((), ())), preferred_element_type=F32)
    if masked:
        row = lax.broadcasted_iota(jnp.int32, s.shape, 0)
        col = lax.broadcasted_iota(jnp.int32, s.shape, 1)
        s = jnp.where(col <= row, s, MASK_VALUE)
    return s, q, k


def _causal_pairs(nq, by_query):
    if by_query:
        pairs = [(i, j) for i in range(nq) for j in range(i + 1)]
    else:
        pairs = [(i, j) for j in range(nq) for i in range(j, nq)]
    return (jnp.asarray(np.array([p[0] for p in pairs], np.int32)), jnp.asarray(np.array([p[1] for p in pairs], np.int32)),
            len(pairs))


def _mla_fwd(q_full, qr, kv, kr, T, carry=None):
    n = q_full.shape[0]
    nq = n // T
    H = MLA_HEADS
    qi, kj, npairs = _causal_pairs(nq, True)

    def body(qi_ref, kj_ref, qn_ref, qr_ref, kn_ref, v_ref, kr_ref, y_ref, lse_ref, m_ref, l_ref, acc_ref):
        t = pl.program_id(1)
        i, j = qi_ref[t], kj_ref[t]

        @pl.when(j == 0)
        def _():
            m_ref[...] = jnp.full_like(m_ref, MASK_VALUE)
            l_ref[...] = jnp.zeros_like(l_ref)
            acc_ref[...] = jnp.zeros_like(acc_ref)

        def update(masked):
            for hh in range(MLA_HP):
                s, _, _ = _mla_scores(qn_ref, qr_ref, kn_ref, kr_ref, hh, masked)
                m_prev = m_ref[hh]
                m_new = jnp.maximum(m_prev, s.max(-1, keepdims=True))
                p = jnp.exp2((s - m_new[:, :1]) * (MLA_SCALE * LOG2E))
                alpha = jnp.exp2((m_prev - m_new) * (MLA_SCALE * LOG2E))
                l_ref[hh] = alpha * l_ref[hh] + p.sum(-1, keepdims=True)
                acc_ref[hh] = alpha * acc_ref[hh] + jnp.dot(p.astype(BF16), v_ref[:, hh * LANES:(hh + 1) * LANES],
                                                            preferred_element_type=F32)
                m_ref[hh] = m_new

        @pl.when(j < i)
        def _():
            update(False)

        @pl.when(j == i)
        def _():
            update(True)
            for hh in range(MLA_HP):
                y_ref[:, hh * LANES:(hh + 1) * LANES] = (acc_ref[hh] / l_ref[hh]).astype(y_ref.dtype)
                lse_ref[hh] = m_ref[hh] * (MLA_SCALE * LOG2E) + jnp.log2(l_ref[hh])

    G = H // MLA_HP
    qspec = lambda off: pl.BlockSpec((T, MLA_W), lambda h, t, qi, kj, off=off: (qi[t], off + h))
    kspec = lambda off: pl.BlockSpec((T, MLA_W), lambda h, t, qi, kj, off=off: (kj[t], off + h))
    return _call(
        body, "mla_fwd", (G, npairs), 2,
        [qspec(0), qspec(0), kspec(0), kspec(G), pl.BlockSpec((T, LANES), lambda h, t, qi, kj: (kj[t], 0))],
        [pl.BlockSpec((T, MLA_W), lambda h, t, qi, kj: (qi[t], h)),
         pl.BlockSpec((MLA_HP, T, LANES), lambda h, t, qi, kj: (h, qi[t], 0))],
        [SDS((n, H * MLA_V), BF16), SDS((H, n, LANES), F32)], [pltpu.VMEM((MLA_HP, T, LANES), F32)] * 3,
        [qi, kj, q_full, qr, kv, kv, kr], ("parallel", "arbitrary"), carry)


def _mla_delta(dy, y, T):
    n = y.shape[0]
    H = MLA_HEADS

    def body(dy_ref, y_ref, d_ref):
        d = jnp.sum(dy_ref[...].astype(F32) * y_ref[...].astype(F32), axis=-1, keepdims=True)
        d_ref[0] = jnp.broadcast_to(d, (T, LANES))

    spec = pl.BlockSpec((T, LANES), lambda h, i: (i, h))
    return _pcall(body, name="mla_delta", grid=(H, n // T), out_shape=SDS((H, n, LANES), F32), in_specs=[spec, spec],
                  out_specs=pl.BlockSpec((1, T, LANES), lambda h, i: (h, i, 0)),
                  compiler_params=_params(("parallel", "parallel")))(dy, y)


def _mla_bwd_dq(q_full, qr, kv, kr, dy, lse, delta, T, carry=None):
    n = q_full.shape[0]
    nq = n // T
    H = MLA_HEADS

    qi, kj, npairs = _causal_pairs(nq, True)

    def body(qi_ref, kj_ref, qn_ref, qr_ref, kn_ref, v_ref, kr_ref, dy_ref, lse_ref, dl_ref, dqn_ref, dqr_ref, acc_ref):
        t = pl.program_id(1)
        i, j = qi_ref[t], kj_ref[t]

        @pl.when(j == 0)
        def _():
            acc_ref[...] = jnp.zeros_like(acc_ref)

        def update(masked):
            for hh in range(MLA_HP):
                cols = slice(hh * LANES, (hh + 1) * LANES)
                s, _, k = _mla_scores(qn_ref, qr_ref, kn_ref, kr_ref, hh, masked)
                p = jnp.exp2(s * (MLA_SCALE * LOG2E) - lse_ref[hh][:, :1])
                dp = lax.dot_general(dy_ref[:, cols], v_ref[:, cols], (((1,), (1,)), ((), ())), preferred_element_type=F32)
                ds = p * (dp - dl_ref[hh][:, :1])
                acc_ref[hh] += jnp.dot(ds.astype(BF16), k, preferred_element_type=F32)

        @pl.when(j < i)
        def _():
            update(False)

        @pl.when(j == i)
        def _():
            update(True)
            for hh in range(MLA_HP):
                cols = slice(hh * LANES, (hh + 1) * LANES)
                dqn_ref[:, cols] = (acc_ref[hh][:, :LANES] * MLA_SCALE).astype(dqn_ref.dtype)
                dqr_ref[:, cols] = (acc_ref[hh][:, LANES:] * MLA_SCALE).astype(dqr_ref.dtype)

    G = H // MLA_HP
    qspec = lambda off: pl.BlockSpec((T, MLA_W), lambda h, t, qi, kj, off=off: (qi[t], off + h))
    kspec = lambda off: pl.BlockSpec((T, MLA_W), lambda h, t, qi, kj, off=off: (kj[t], off + h))
    stat = pl.BlockSpec((MLA_HP, T, LANES), lambda h, t, qi, kj: (h, qi[t], 0))
    out = pl.BlockSpec((T, MLA_W), lambda h, t, qi, kj: (qi[t], h))
    return _call(
        body, "mla_bwd_dq", (G, npairs), 2,
        [qspec(0), qspec(0), kspec(0), kspec(G), pl.BlockSpec((T, LANES), lambda h, t, qi, kj: (kj[t], 0)), qspec(0), stat, stat],
        [out, out], [SDS((n, H * LANES), BF16), SDS((n, H * LANES), BF16)], [pltpu.VMEM((MLA_HP, T, 2 * LANES), F32)],
        [qi, kj, q_full, qr, kv, kv, kr, dy, lse, delta], ("parallel", "arbitrary"), carry)


def _mla_bwd_dkv(q_full, qr, kv, kr, dy, lse, delta, T, carry=None):
    n = q_full.shape[0]
    nq = n // T
    H = MLA_HEADS

    qi, kj, npairs = _causal_pairs(nq, False)

    def body(qi_ref, kj_ref, qn_ref, qr_ref, kn_ref, v_ref, kr_ref, dy_ref, lse_ref, dl_ref, dkn_ref, dv_ref, dkr_ref,
             dk_acc, dv_acc):
        t = pl.program_id(1)
        i, j = qi_ref[t], kj_ref[t]

        @pl.when(i == j)
        def _():
            dk_acc[...] = jnp.zeros_like(dk_acc)
            dv_acc[...] = jnp.zeros_like(dv_acc)

        def update(masked):
            for hh in range(MLA_HP):
                cols = slice(hh * LANES, (hh + 1) * LANES)
                s, q, _ = _mla_scores(qn_ref, qr_ref, kn_ref, kr_ref, hh, masked)
                p = jnp.exp2(s * (MLA_SCALE * LOG2E) - lse_ref[hh][:, :1])
                dy = dy_ref[:, cols]
                dv_acc[hh] += lax.dot_general(p.astype(BF16), dy, (((0,), (0,)), ((), ())), preferred_element_type=F32)
                dp = lax.dot_general(dy, v_ref[:, cols], (((1,), (1,)), ((), ())), preferred_element_type=F32)
                ds = p * (dp - dl_ref[hh][:, :1])
                dk_acc[hh] += lax.dot_general(ds.astype(BF16), q, (((0,), (0,)), ((), ())), preferred_element_type=F32)

        @pl.when(i == j)
        def _():
            update(True)

        @pl.when(i > j)
        def _():
            update(False)

        @pl.when(i == nq - 1)
        def _():
            for hh in range(MLA_HP):
                cols = slice(hh * LANES, (hh + 1) * LANES)
                dkn_ref[:, cols] = (dk_acc[hh][:, :LANES] * MLA_SCALE).astype(dkn_ref.dtype)
                dv_ref[:, cols] = dv_acc[hh].astype(dv_ref.dtype)
                dkr_ref[hh] = dk_acc[hh][:, LANES:] * MLA_SCALE

    G = H // MLA_HP
    qspec = lambda off: pl.BlockSpec((T, MLA_W), lambda h, t, qi, kj, off=off: (qi[t], off + h))
    kspec = lambda off: pl.BlockSpec((T, MLA_W), lambda h, t, qi, kj, off=off: (kj[t], off + h))
    stat = pl.BlockSpec((MLA_HP, T, LANES), lambda h, t, qi, kj: (h, qi[t], 0))
    out = pl.BlockSpec((T, MLA_W), lambda h, t, qi, kj: (kj[t], h))
    return _call(
        body, "mla_bwd_dkv", (G, npairs), 2,
        [qspec(0), qspec(0), kspec(0), kspec(G), pl.BlockSpec((T, LANES), lambda h, t, qi, kj: (kj[t], 0)), qspec(0), stat, stat],
        [out, out, pl.BlockSpec((MLA_HP, T, LANES), lambda h, t, qi, kj: (h, kj[t], 0))],
        [SDS((n, H * LANES), BF16), SDS((n, H * LANES), BF16), SDS((H, n, LANES), F32)],
        [pltpu.VMEM((MLA_HP, T, 2 * LANES), F32), pltpu.VMEM((MLA_HP, T, LANES), F32)],
        [qi, kj, q_full, qr, kv, kv, kr, dy, lse, delta], ("parallel", "arbitrary"), carry)


def _sum_heads(a, tm):
    H, n, _ = a.shape

    def body(a_ref, o_ref):
        o_ref[...] = jnp.sum(a_ref[...], axis=0)

    return _pcall(body, name="mla_sum_heads", grid=(n // tm,), out_shape=SDS((n, LANES), F32),
                  in_specs=[pl.BlockSpec((H, tm, LANES), lambda i: (0, i, 0))],
                  out_specs=pl.BlockSpec((tm, LANES), lambda i: (i, 0)), compiler_params=_params(("parallel",)))(a)


def _shift_down(x, k):
    row = lax.broadcasted_iota(jnp.int32, x.shape, 0)
    return jnp.where(row >= k, pltpu.roll(x, k, axis=0), 0.0)


def _shift_up(x, k):
    n = x.shape[0]
    row = lax.broadcasted_iota(jnp.int32, x.shape, 0)
    return jnp.where(row < n - k, pltpu.roll(x, n - k, axis=0), 0.0)


def _conv_fwd(up, w, b):
    n, c = up.shape

    def body(u_ref, w_ref, b_ref, o_ref):
        u = u_ref[...].astype(F32)
        wv = w_ref[...]
        o_ref[...] = (b_ref[...] + wv[0:1] * _shift_down(u, 2) + wv[1:2] * _shift_down(u, 1) + wv[2:3] * u).astype(o_ref.dtype)

    return _pcall(body, name="conv_fwd", grid=(c // LANES,), out_shape=SDS((n, c), BF16),
                  in_specs=[pl.BlockSpec((n, LANES), lambda j: (0, j)), pl.BlockSpec((3, LANES), lambda j: (0, j)),
                            pl.BlockSpec((1, LANES), lambda j: (0, j))],
                  out_specs=pl.BlockSpec((n, LANES), lambda j: (0, j)), compiler_params=_params(("parallel",)))(up, w, b)


def _conv_bwd(up, dc, w):
    n, c = up.shape

    def body(u_ref, d_ref, w_ref, du_ref, dw_ref, db_ref):
        u = u_ref[...].astype(F32)
        d = d_ref[...].astype(F32)
        wv = w_ref[...]
        du_ref[...] = (wv[2:3] * d + wv[1:2] * _shift_up(d, 1) + wv[0:1] * _shift_up(d, 2)).astype(du_ref.dtype)
        dw_ref[0:1, :] = jnp.sum(d * _shift_down(u, 2), axis=0, keepdims=True)
        dw_ref[1:2, :] = jnp.sum(d * _shift_down(u, 1), axis=0, keepdims=True)
        dw_ref[2:3, :] = jnp.sum(d * u, axis=0, keepdims=True)
        db_ref[...] = jnp.sum(d, axis=0, keepdims=True)

    col = pl.BlockSpec((n, LANES), lambda j: (0, j))
    return _pcall(body, name="conv_bwd", grid=(c // LANES,),
                  out_shape=[SDS((n, c), BF16), SDS((3, c), F32), SDS((1, c), F32)],
                  in_specs=[col, col, pl.BlockSpec((3, LANES), lambda j: (0, j))],
                  out_specs=[col, pl.BlockSpec((3, LANES), lambda j: (0, j)), pl.BlockSpec((1, LANES), lambda j: (0, j))],
                  compiler_params=_params(("parallel",)))(up, dc, w)


def _adamw_math(w, g, m, v):
    m = ADAM_B1 * m + (1.0 - ADAM_B1) * g
    v = ADAM_B2 * v + (1.0 - ADAM_B2) * jnp.square(g)
    m_hat = m / (1.0 - ADAM_B1 ** ADAM_STEP)
    v_hat = v / (1.0 - ADAM_B2 ** ADAM_STEP)
    delta = -ADAM_LR * (m_hat / (jnp.sqrt(v_hat) + ADAM_EPS) + ADAM_WD * w)
    return delta, m, v


def _adamw(w, g, m, v, name):
    L, r, c = w.shape
    tr = _rows_tile(r, c, 1 << 20, 8)

    def body(w_ref, g_ref, m_ref, v_ref, d_ref, nm_ref, nv_ref):
        d, nm, nv = _adamw_math(w_ref[...], g_ref[...], m_ref[...], v_ref[...])
        d_ref[...] = d
        nm_ref[...] = nm
        nv_ref[...] = nv

    spec = pl.BlockSpec((1, tr, c), lambda l, i: (l, i, 0))
    return _pcall(body, name=name, grid=(L, r // tr), out_shape=[SDS(w.shape, F32)] * 3, in_specs=[spec] * 4,
                  out_specs=[spec] * 3, compiler_params=_params(("parallel", "parallel")))(w, g, m, v)


def _coords():
    return lax.axis_index("x"), lax.axis_index("y"), lax.axis_index("c")


def _other_chips(x, y):
    return [(1 - x, y), (x, 1 - y), (1 - x, 1 - y)]


HBM_SPEC = pl.BlockSpec(memory_space=pltpu.HBM)


def _half(c, rows):
    return pl.ds(pl.multiple_of(c * rows, 16), rows)


def _rows_tile(rows, cols, budget_bytes=2 << 20, align=16):
    t = (min(max(align, budget_bytes // (4 * cols)), rows) // align) * align
    while t >= align:
        if rows % t == 0:
            return t
        t -= align
    return rows


def _scalar(v):
    return jnp.reshape(jnp.asarray(v, jnp.int32), (1,))


def _cast_into_slot(w3, layer, slot, name):
    _, R, C = w3.shape
    tr = _rows_tile(R, C)

    def body(s_ref, w_ref, o_ref):
        o_ref[0] = w_ref[0].astype(BF16)

    gs = pltpu.PrefetchScalarGridSpec(
        num_scalar_prefetch=1, grid=(R // tr,),
        in_specs=[pl.BlockSpec((1, tr, C), lambda i, s: (layer, i, 0))],
        out_specs=pl.BlockSpec((1, tr, C), lambda i, s: (s[0], i, 0)))
    return _pcall(body, name=name, grid_spec=gs, out_shape=SDS((N_CHIPS, R, C), BF16),
                  compiler_params=_params(("arbitrary",)))(_scalar(slot), w3)


class _Stage:
    def __init__(self, ins, out_shapes, aliases, n_sems, copies):
        self.ins, self.out_shapes, self.aliases, self.n_sems, self.copies = list(ins), out_shapes, aliases, n_sems, copies

    def start(self, ins, outs, send_sems, recv_sems):
        for cp in self.copies(ins, outs, send_sems, recv_sems)[0]:
            cp.start()

    def finish(self, ins, outs, send_sems, recv_sems):
        sends, arrivals = self.copies(ins, outs, send_sems, recv_sems)
        for cp in arrivals:
            cp.wait_recv()
        for cp in sends:
            cp.wait_send()


class _SemsFrom:
    def __init__(self, sems, base):
        self.sems, self.base = sems, base

    @property
    def at(self):
        return self

    def __getitem__(self, k):
        return self.sems.at[self.base + k]


def _both(a, b):
    if a is None or b is None:
        return a if b is None else b
    na, nao = len(a.ins), len(a.out_shapes)

    def copies(ins, outs, send_sems, recv_sems):
        sa, aa = a.copies(ins[:na], outs[:nao], send_sems, recv_sems)
        sb, ab = b.copies(ins[na:], outs[nao:], _SemsFrom(send_sems, a.n_sems), _SemsFrom(recv_sems, a.n_sems))
        return sa + sb, aa + ab

    aliases = {**a.aliases, **{na + i: nao + o for i, o in b.aliases.items()}}
    return _Stage(a.ins + b.ins, list(a.out_shapes) + list(b.out_shapes), aliases, a.n_sems + b.n_sems, copies)


def _remote(src, dst, send_sems, recv_sems, k, to):
    return pltpu.make_async_remote_copy(src_ref=src, dst_ref=dst, send_sem=send_sems.at[k], recv_sem=recv_sems.at[k],
                                        device_id=to, device_id_type=MESH)


def _gather_stages(bufs):
    n = len(bufs)
    shapes = [SDS(b.shape, b.dtype) for b in bufs]
    same = {i: i for i in range(n)}

    def over_ici(ins, outs, send_sems, recv_sems):
        x, y, c = _coords()
        blk = lambda w, chip: outs[w].at[chip, _half(c, outs[w].shape[1] // 2), :]
        sends, arrivals = [], []
        for w in range(n):
            for j, (px, py) in enumerate(_other_chips(x, y)):
                sends.append(_remote(blk(w, 2 * x + y), blk(w, 2 * x + y), send_sems, recv_sems, 3 * w + j, (px, py, c)))
                arrivals.append(_remote(blk(w, 2 * px + py), blk(w, 2 * px + py), send_sems, recv_sems, 3 * w + j, (px, py, c)))
        return sends, arrivals

    def to_sibling(ins, outs, send_sems, recv_sems):
        x, y, c = _coords()
        blk = lambda w, chip, half: outs[w].at[chip, _half(half, outs[w].shape[1] // 2), :]
        sends, arrivals = [], []
        for w in range(n):
            for j, (px, py) in enumerate(_other_chips(x, y)):
                k = 2 * px + py
                sends.append(_remote(blk(w, k, c), blk(w, k, c), send_sems, recv_sems, 3 * w + j, (x, y, 1 - c)))
                arrivals.append(_remote(blk(w, k, 1 - c), blk(w, k, 1 - c), send_sems, recv_sems, 3 * w + j, (x, y, 1 - c)))
        return sends, arrivals

    return (lambda b: _Stage(b, shapes, same, 3 * n, over_ici)), (lambda b: _Stage(b, shapes, same, 3 * n, to_sibling))


def _swap_stage(gs_):
    n = len(gs_)

    def copies(ins, outs, send_sems, recv_sems):
        x, y, c = _coords()
        cps = [_remote(ins[w].at[:, _half(1 - c, ins[w].shape[1] // 2), :], outs[w], send_sems, recv_sems, w, (x, y, 1 - c))
               for w in range(n)]
        return cps, cps

    return _Stage(gs_, [SDS((N_CHIPS, g.shape[1] // 2, g.shape[2]), g.dtype) for g in gs_], {}, n, copies)


def _scatter_stage(ps):
    n = len(ps)

    def copies(ins, outs, send_sems, recv_sems):
        x, y, c = _coords()
        cps = [_remote(ins[w].at[2 * px + py], outs[w].at[j], send_sems, recv_sems, 3 * w + j, (px, py, c))
               for w in range(n) for j, (px, py) in enumerate(_other_chips(x, y))]
        return cps, cps

    return _Stage(ps, [SDS((3,) + p.shape[1:], p.dtype) for p in ps], {}, 3 * n, copies)


def _join_stage(bufs, layers):
    n = len(bufs)
    slots = [(w, l) for w in range(n) for l in layers[w]]

    def copies(ins, outs, send_sems, recv_sems):
        x, y, c = _coords()
        blk = lambda w, l, half: outs[w].at[l, _half(half, outs[w].shape[1] // 2), :]
        sends = [_remote(blk(w, l, c), blk(w, l, c), send_sems, recv_sems, k, (x, y, 1 - c)) for k, (w, l) in enumerate(slots)]
        arrivals = [_remote(blk(w, l, 1 - c), blk(w, l, 1 - c), send_sems, recv_sems, k, (x, y, 1 - c))
                    for k, (w, l) in enumerate(slots)]
        return sends, arrivals

    return _Stage(bufs, [SDS(b.shape, b.dtype) for b in bufs], {i: i for i in range(n)}, len(slots), copies)


def _stage_scratch(stage):
    return [pltpu.SemaphoreType.DMA((stage.n_sems,)), pltpu.SemaphoreType.DMA((stage.n_sems,))]


def _run_stage(stage, name):
    n_in, n_out = len(stage.ins), len(stage.out_shapes)

    def body(*refs):
        ins, outs, send_sems, recv_sems = refs[:n_in], refs[n_in:n_in + n_out], refs[n_in + n_out], refs[n_in + n_out + 1]
        stage.start(ins, outs, send_sems, recv_sems)
        stage.finish(ins, outs, send_sems, recv_sems)

    return _pcall(body, name=name, out_shape=stage.out_shapes, in_specs=[HBM_SPEC] * n_in, out_specs=[HBM_SPEC] * n_out,
                  input_output_aliases=stage.aliases, scratch_shapes=_stage_scratch(stage))(*stage.ins)


def _call(body, name, grid, n_prefetch, in_specs, out_specs, out_shape, scratch, operands, semantics, carry=None):
    n_in, n_out, n_sc = len(in_specs), len(out_specs), len(scratch)
    if carry is None:
        gs = pltpu.PrefetchScalarGridSpec(num_scalar_prefetch=n_prefetch, grid=grid, in_specs=in_specs, out_specs=out_specs,
                                          scratch_shapes=scratch)
        res = _pcall(body, name=name, grid_spec=gs, out_shape=out_shape, compiler_params=_params(semantics))(*operands)
        return list(res), []
    s_in, s_out = len(carry.ins), len(carry.out_shapes)

    def carrying(*refs):
        o = n_prefetch
        pre, ins = refs[:o], refs[o:o + n_in]
        o += n_in
        sins = refs[o:o + s_in]
        o += s_in
        outs = refs[o:o + n_out]
        o += n_out
        souts = refs[o:o + s_out]
        o += s_out
        sc, send_sems, recv_sems = refs[o:o + n_sc], refs[o + n_sc], refs[o + n_sc + 1]
        first = functools.reduce(jnp.logical_and, [pl.program_id(a) == 0 for a in range(len(grid))])
        last = functools.reduce(jnp.logical_and, [pl.program_id(a) == g - 1 for a, g in enumerate(grid)])

        @pl.when(first)
        def _():
            carry.start(sins, souts, send_sems, recv_sems)

        body(*pre, *ins, *outs, *sc)

        @pl.when(last)
        def _():
            carry.finish(sins, souts, send_sems, recv_sems)

    gs = pltpu.PrefetchScalarGridSpec(
        num_scalar_prefetch=n_prefetch, grid=grid, in_specs=list(in_specs) + [HBM_SPEC] * s_in,
        out_specs=list(out_specs) + [HBM_SPEC] * s_out, scratch_shapes=list(scratch) + _stage_scratch(carry))
    aliases = {n_prefetch + n_in + a: n_out + b for a, b in carry.aliases.items()}
    res = _pcall(carrying, name=name, grid_spec=gs, out_shape=list(out_shape) + list(carry.out_shapes),
                 input_output_aliases=aliases, compiler_params=_params(("arbitrary",) * len(grid)))(*operands, *carry.ins)
    return list(res[:n_out]), list(res[n_out:])


def _all_reduce_small(v):
    n = v.shape[0]

    def body(v_ref, out_ref, slots, send_sems, recv_sems):
        x, y, c = _coords()
        me = 4 * x + 2 * y + c
        cps = []
        for r in range(1, 8):
            t = (me + r) % 8
            cp = pltpu.make_async_remote_copy(src_ref=v_ref, dst_ref=slots.at[me], send_sem=send_sems.at[r - 1],
                                              recv_sem=recv_sems.at[me], device_id=(t // 4, (t // 2) % 2, t % 2),
                                              device_id_type=MESH)
            cp.start()
            cps.append(cp)
        slots[me] = v_ref[...]
        for r in range(1, 8):
            s = (me + r) % 8
            pltpu.make_async_remote_copy(src_ref=v_ref, dst_ref=slots.at[s], send_sem=send_sems.at[r - 1],
                                         recv_sem=recv_sems.at[s], device_id=(x, y, c), device_id_type=MESH).wait_recv()
        for cp in cps:
            cp.wait_send()
        acc = slots[0]
        for d in range(1, 8):
            acc = acc + slots[d]
        out_ref[...] = acc

    return _pcall(body, name="all_reduce_small", out_shape=SDS((n, LANES), F32),
                  in_specs=[pl.BlockSpec(memory_space=pltpu.VMEM)], out_specs=pl.BlockSpec(memory_space=pltpu.VMEM),
                  scratch_shapes=[pltpu.VMEM((8, n, LANES), F32), pltpu.SemaphoreType.DMA((7,)), pltpu.SemaphoreType.DMA((8,))],
                  compiler_params=pltpu.CompilerParams(vmem_limit_bytes=VMEM_LIMIT))(v)


def _add_half(g, recv, c, name):
    _, R, C = g.shape
    rows = R // 2
    tr = _rows_tile(rows, C, 1 << 20)
    nb = rows // tr

    def body(s_ref, g_ref, r_ref, o32_ref, o16_ref):
        s = g_ref[...] + r_ref[...]
        o32_ref[...] = s
        o16_ref[...] = s.astype(BF16)

    blk = lambda k, i, s: (k, i, 0)
    gs = pltpu.PrefetchScalarGridSpec(
        num_scalar_prefetch=1, grid=(N_CHIPS, nb),
        in_specs=[pl.BlockSpec((1, tr, C), lambda k, i, s: (k, s[0] * nb + i, 0)), pl.BlockSpec((1, tr, C), blk)],
        out_specs=[pl.BlockSpec((1, tr, C), blk), pl.BlockSpec((1, tr, C), blk)])
    return _pcall(body, name=name, grid_spec=gs, out_shape=[SDS((N_CHIPS, rows, C), F32), SDS((N_CHIPS, rows, C), BF16)],
                  compiler_params=_params(("arbitrary", "arbitrary")))(_scalar(c), g, recv)


def _sum_into(p32, arrived, chip, c, layer, n_layers, prev, name):
    _, rows, C = p32.shape
    tr = _rows_tile(rows, C, 1 << 20)
    nb = rows // tr

    def body(chip_ref, c_ref, p_ref, a_ref, *rest):
        o_ref = rest[-1]
        o_ref[0] = ((p_ref[0] + a_ref[0].astype(F32)) + a_ref[1].astype(F32)) + a_ref[2].astype(F32)

    in_specs = [pl.BlockSpec((1, tr, C), lambda i, chip_ref, c_ref: (chip_ref[0], i, 0)),
                pl.BlockSpec((3, tr, C), lambda i, chip_ref, c_ref: (0, i, 0))]
    ins = [p32, arrived]
    aliases = {}
    if prev is not None:
        in_specs.append(pl.BlockSpec(memory_space=pl.ANY))
        ins.append(prev)
        aliases = {4: 0}
    gs = pltpu.PrefetchScalarGridSpec(
        num_scalar_prefetch=2, grid=(nb,), in_specs=in_specs,
        out_specs=pl.BlockSpec((1, tr, C), lambda i, chip_ref, c_ref: (layer, c_ref[0] * nb + i, 0)))
    return _pcall(body, name=name, grid_spec=gs, out_shape=SDS((n_layers, 2 * rows, C), F32), input_output_aliases=aliases,
                  compiler_params=_params(("arbitrary",)))(_scalar(chip), _scalar(c), *ins)


def _h_layout(D):
    G = N_BRANCHES * D
    off, o = {}, 0
    for name, w in [("g", G), ("qa", A_Q), ("cq", MLA_Q_RANK), ("ckv", MLA_KV_RANK), ("hu", SGU_WIDTH), ("hv", SGU_WIDTH),
                    ("ka", A_KV), ("va", A_KV), ("kr", LANES)]:
        assert o % w == 0, (name, o, w)
        off[name] = (o, w)
        o += w
    off["total"] = -(-o // 512) * 512
    return off


def _perm_w_in(w, lay):
    s = np.cumsum([0, A_Q, A_KV, A_KV, MLA_Q_RANK, MLA_KV_RANK, MLA_ROPE, SGU_WIDTH, SGU_WIDTH])
    qa, ka, va, cq, ckv, kr, hu, hv = [w[:, s[i]:s[i + 1]] for i in range(8)]
    g = w[:, s[8]:]
    pad = jnp.zeros((w.shape[0], lay["total"] - lay["kr"][0] - MLA_ROPE), w.dtype)
    return jnp.concatenate([g, qa, cq, ckv, hu, hv, ka, va, kr, pad], axis=1)


def _unperm_w_in(wp, lay, D):
    take = lambda n, width=None: wp[:, lay[n][0]:lay[n][0] + (width or lay[n][1])]
    return jnp.concatenate([take("qa"), take("ka"), take("va"), take("cq"), take("ckv"), take("kr", MLA_ROPE), take("hu"),
                            take("hv"), take("g")], axis=1)


def _perm_w_uq(w):
    r = w.shape[0]
    w3 = w.reshape(r, MLA_HEADS, MLA_NOPE + MLA_ROPE)
    nope = w3[:, :, :MLA_NOPE].reshape(r, MLA_HEADS * MLA_NOPE)
    rope = jnp.pad(w3[:, :, MLA_NOPE:], ((0, 0), (0, 0), (0, LANES - MLA_ROPE))).reshape(r, MLA_HEADS * LANES)
    return jnp.concatenate([nope, rope], axis=1)


def _unperm_w_uq(wp):
    r = wp.shape[0]
    nope = wp[:, :MLA_HEADS * MLA_NOPE].reshape(r, MLA_HEADS, MLA_NOPE)
    rope = wp[:, MLA_HEADS * MLA_NOPE:].reshape(r, MLA_HEADS, LANES)[:, :, :MLA_ROPE]
    return jnp.concatenate([nope, rope], axis=2).reshape(r, MLA_HEADS * (MLA_NOPE + MLA_ROPE))


def _perm_w_ukv(w):
    r = w.shape[0]
    w3 = w.reshape(r, MLA_HEADS, MLA_NOPE + MLA_V)
    return jnp.concatenate([w3[:, :, :MLA_NOPE].reshape(r, -1), w3[:, :, MLA_NOPE:].reshape(r, -1)], axis=1)


def _unperm_w_ukv(wp):
    r = wp.shape[0]
    k = wp[:, :MLA_HEADS * MLA_NOPE].reshape(r, MLA_HEADS, MLA_NOPE)
    v = wp[:, MLA_HEADS * MLA_NOPE:].reshape(r, MLA_HEADS, MLA_V)
    return jnp.concatenate([k, v], axis=2).reshape(r, -1)


def _col_chunks(g):
    r, c4 = g.shape
    return jnp.transpose(g.reshape(r, N_CHIPS, c4 // N_CHIPS), (1, 0, 2))


def kernel(x, positions, w_in, b_gate, sinks, q_norm_g, kv_norm_g, w_uq, w_ukv, sgu_ln_g, sgu_ln_b, sgu_w, sgu_b, w_proj_a, w_proj_b, w_proj_c, w_o, ln1_g, ln1_b, w_up, conv_w, conv_b, w_down, ln2_g, ln2_b, loss_target, m_w_in, m_b_gate, m_sinks, m_q_norm_g, m_kv_norm_g, m_w_uq, m_w_ukv, m_sgu_ln_g, m_sgu_ln_b, m_sgu_w, m_sgu_b, m_w_proj_a, m_w_proj_b, m_w_proj_c, m_w_o, m_ln1_g, m_ln1_b, m_w_up, m_conv_w, m_conv_b, m_w_down, m_ln2_g, m_ln2_b, v_w_in, v_b_gate, v_sinks, v_q_norm_g, v_kv_norm_g, v_w_uq, v_w_ukv, v_sgu_ln_g, v_sgu_ln_b, v_sgu_w, v_sgu_b, v_w_proj_a, v_w_proj_b, v_w_proj_c, v_w_o, v_ln1_g, v_ln1_b, v_w_up, v_conv_w, v_conv_b, v_w_down, v_ln2_g, v_ln2_b):
    a = locals()
    W = {k: a[k] for k in WEIGHTS}
    Mo = {k: a["m_" + k] for k in WEIGHTS}
    Vo = {k: a["v_" + k] for k in WEIGHTS}
    S, D = x.shape[1], x.shape[2]
    FF2 = w_up.shape[2] * N_CHIPS
    FF = FF2 // 2
    L = DEPTH
    lay = _h_layout(D)
    NP = lay["total"]
    cx, cy, cc = _coords()
    chip = 2 * cx + cy
    T = _tile(S, 512)
    TM = _tile(S, 256, 16)
    TMW = _tile(S, 64, 16)

    shards = {k: tuple(W[k].shape) for k in BIG}
    full = {k: [None] * L for k in BIG}
    own = {(l, k): _cast_into_slot(W[k], l, chip, f"cast_{k}_l{l}") for l in range(L) for k in BIG}

    on_way = {"pairs": [], "arrived": []}

    def lay_out(pairs, gathered):
        for (l_, k), g in zip(pairs, gathered):
            _, r, c_ = shards[k]
            full[k][l_] = g.reshape(N_CHIPS * r, c_) if k in ROW_SHARDED else jnp.transpose(g, (1, 0, 2)).reshape(r, N_CHIPS * c_)

    def gather_behind(pairs):
        to_sib = _gather_stages(on_way["arrived"])[1](on_way["arrived"]) if on_way["pairs"] else None
        bufs = [own[p_] for p_ in pairs]
        return _both(to_sib, _gather_stages(bufs)[0](bufs) if pairs else None)

    def gathered_behind(pairs, outs):
        n_done = len(on_way["pairs"])
        lay_out(on_way["pairs"], outs[:n_done])
        on_way["pairs"], on_way["arrived"] = list(pairs), list(outs[n_done:])

    def mm_behind(a_, b_, mode, dt, name, pairs):
        stage = gather_behind(pairs)
        if stage is None:
            return _mm(a_, b_, mode, dt, name)
        out, outs = _mm(a_, b_, mode, dt, name, carry=stage)
        gathered_behind(pairs, outs)
        return out

    MIX = ["w_uq", "w_ukv", "w_proj_a", "w_proj_b", "w_proj_c", "w_o"]
    nxt = lambda l, names: [(l + 1, k) for k in names] if l + 1 < L else []
    first = [(0, "w_in")]
    gathered_behind(first, _run_stage(gather_behind(first), "gather_ici_w_in_l0"))
    gathered_behind([], _run_stage(gather_behind([]), "gather_sibling_w_in_l0"))

    small_sharded_full = {k: tuple(W[k].shape[:-1]) + (W[k].shape[-1] * N_CHIPS,) for k in SMALL_SHARDED}
    placed = []
    for k in ("b_gate", "conv_w"):
        z = jnp.zeros(small_sharded_full[k], F32)
        z = lax.dynamic_update_slice_in_dim(z, W[k], chip * W[k].shape[-1], axis=-1)
        placed.append(jnp.where(cc == 0, z, 0.0).reshape(-1))
    pv = jnp.concatenate(placed)
    n_pv = pv.shape[0]
    pv = jnp.pad(pv, (0, -n_pv % (8 * LANES))).reshape(-1, LANES)
    pv = _all_reduce_small(pv).reshape(-1)
    nb_ = int(np.prod(small_sharded_full["b_gate"]))
    b_gate_full = pv[:nb_].reshape(small_sharded_full["b_gate"])
    conv_w_full = pv[nb_:n_pv].reshape(small_sharded_full["conv_w"])

    inv_freq = ROPE_THETA ** (-jnp.arange(0, MLA_ROPE, 2, dtype=F32) / MLA_ROPE)
    ang = positions[0].astype(F32)[:, None] * inv_freq
    cos, sin = jnp.cos(ang), jnp.sin(ang)
    cos_t = jnp.concatenate([cos, cos, jnp.ones((S, LANES - MLA_ROPE), F32)], axis=1)
    sin_t = jnp.concatenate([sin, sin, jnp.zeros((S, LANES - MLA_ROPE), F32)], axis=1)

    row = lambda v: v.reshape(1, -1)
    cb = lambda name: lay[name][0] // lay[name][1]

    xs = x[0]
    saved = []
    for l in range(L):
        p = dict(
            w_in=_perm_w_in(full["w_in"][l], lay),
            sinks=row(sinks[l]), qg=row(q_norm_g[l]), kvg=row(kv_norm_g[l]), sg=row(sgu_ln_g[l]), sb=row(sgu_ln_b[l]),
            sw=sgu_w[l], sb3=sgu_b[l].reshape(SGU_GROUPS, SGU_CHUNK, 1),
            bg=b_gate_full[l], l1g=row(ln1_g[l]), l1b=row(ln1_b[l]), cw=conv_w_full[l], cbias=row(conv_b[l]),
            l2g=row(ln2_g[l]), l2b=row(ln2_b[l]))
        if l == 0:
            def fn_cast(i, rows, ps):
                return [rows[0]], []
            (xb,), _ = _rowwise(fn_cast, [_whole(xs)], [], [(D, BF16)], [], TM, "cast_x")
        own_mix = [(l, k) for k in MIX] if l == 0 else []
        own_up = [(l, "w_up")] if l == 0 else []
        own_down = [(l, "w_down")] if l == 0 else []
        h = mm_behind(xb, p["w_in"], "nn", BF16, "mm_h", own_mix)
        (y_a,), outs = _swa_fwd(h, cb("qa"), cb("ka"), cb("va"), p["sinks"], carry=gather_behind(own_up))
        gathered_behind(own_up, outs)
        p.update(w_uq=_perm_w_uq(full["w_uq"][l]), w_ukv=_perm_w_ukv(full["w_ukv"][l]), w_pa=full["w_proj_a"][l],
                 w_pb=full["w_proj_b"][l], w_pc=full["w_proj_c"][l], w_o=full["w_o"][l])
        def fn_rms(i, rows, ps):
            return [_rms_norm(rows[0].astype(F32), ps[0]), _rms_norm(rows[1].astype(F32), ps[1])], []
        (cqn, ckvn), _ = _rowwise(fn_rms, [(h, MLA_Q_RANK, cb("cq")), (h, MLA_KV_RANK, cb("ckv"))], [p["qg"], p["kvg"]],
                                  [(MLA_Q_RANK, BF16), (MLA_KV_RANK, BF16)], [], TM, "mla_rms")
        q_full = _mm(cqn, p["w_uq"], "nn", BF16, "mm_q")
        kv = _mm(ckvn, p["w_ukv"], "nn", BF16, "mm_kv")
        qr, kr = _rope_call(q_full, MLA_HEADS * LANES, 1, h, LANES, cb("kr"), cos_t, sin_t, 1.0, TM, "rope_fwd")
        behind_mla = own_down + nxt(l, ["w_in"] + MIX)
        (y_b, lse), outs = _mla_fwd(q_full, qr, kv, kr, T, carry=gather_behind(behind_mla))
        gathered_behind(behind_mla, outs)
        y_c = _sgu_fwd(h, cb("hu"), cb("hv"), p["sg"], p["sb"], p["sw"], p["sb3"])
        pa = _mm(y_a, p["w_pa"], "nn", F32, "mm_pa")
        pb = _mm(y_b, p["w_pb"], "nn", F32, "mm_pb")
        pc = _mm(y_c, p["w_pc"], "nn", F32, "mm_pc")

        def merge_math(pa_, pb_, pc_, g_, b0, b1, b2):
            out = 0.0
            for br, (pp, bb) in enumerate(zip((pa_, pb_, pc_), (b0, b1, b2))):
                gate = jax.nn.sigmoid(g_[:, br * D:(br + 1) * D].astype(F32) + bb)
                out = out + gate * pp
            return out

        def fn_merge(i, rows, ps):
            bgv = ps[0]
            return [merge_math(rows[0], rows[1], rows[2], rows[3], bgv[0:1], bgv[1:2], bgv[2:3])], []
        (merged,), _ = _rowwise(fn_merge, [_whole(pa), _whole(pb), _whole(pc), (h, N_BRANCHES * D, cb("g"))], [p["bg"]],
                                [(D, BF16)], [], TMW, "merge_fwd")
        o = _mm(merged, p["w_o"], "nn", F32, "mm_o")

        def ln_res_math(x_, o_, g_, b_):
            return _layer_norm(DN_ALPHA * x_ + o_, g_, b_)

        def fn_ln(i, rows, ps):
            y = ln_res_math(rows[0], rows[1], ps[0], ps[1])
            return [y, y], []
        (x1, x1b), _ = _rowwise(fn_ln, [_whole(xs), _whole(o)], [p["l1g"], p["l1b"]], [(D, F32), (D, BF16)], [], TM, "ln1_fwd")
        p.update(w_up=full["w_up"][l])
        up = mm_behind(x1b, p["w_up"], "nn", BF16, "mm_up", nxt(l, ["w_up"]))
        p.update(w_down=full["w_down"][l])
        cv_ = _conv_fwd(up, p["cw"], p["cbias"])

        def glu_math(cg, cvv):
            return jax.nn.silu(cg.astype(F32)) * cvv.astype(F32)

        def fn_glu(i, rows, ps):
            return [glu_math(rows[0], rows[1])], []
        (act,), _ = _rowwise(fn_glu, [(cv_, FF, 0), (cv_, FF, 1)], [], [(FF, BF16)], [], TMW, "glu_fwd")
        dn = mm_behind(act, p["w_down"], "nn", F32, "mm_down", nxt(l, ["w_down"]))
        (x2, x2b), _ = _rowwise(fn_ln, [_whole(x1), _whole(dn)], [p["l2g"], p["l2b"]], [(D, F32), (D, BF16)], [], TM, "ln2_fwd")
        saved.append(dict(p=p, x0=xs, x0b=xb, h=h, y_a=y_a, cqn=cqn, ckvn=ckvn, q_full=q_full, kv=kv, qr=qr, kr=kr, y_b=y_b,
                          lse=lse, y_c=y_c, pa=pa, pb=pb, pc=pc, merged=merged, o=o, x1=x1, x1b=x1b, up=up, cv=cv_, act=act,
                          dn=dn))
        xs, xb = x2, x2b

    def fn_loss(i, rows, ps):
        diff = rows[0] - rows[1]
        part = jnp.sum(jnp.mean(jnp.square(diff), axis=-1, keepdims=True), axis=0, keepdims=True)
        return [diff * (1.0 / D)], [jnp.broadcast_to(part, (8, LANES))]
    (dx,), (loss_acc,) = _rowwise(fn_loss, [_whole(xs), _whole(loss_target[0])], [], [(D, F32)], [(8, LANES)], TM, "loss")
    loss = lax.psum(0.5 * loss_acc[0, 0], ("x", "y", "c"))

    gbig = {k: [None] * L for k in BIG}
    gsmall = {k: [None] * L for k in SMALL}
    reduced = {}
    pending = None
    for l in reversed(range(L)):
        s = saved[l]
        p = s["p"]

        def fn_ln_bwd(i, rows, ps):
            _, vjp = jax.vjp(ln_res_math, rows[0], rows[1], ps[0], ps[1])
            dx_, do_, dg_, db_ = vjp(rows[2])
            return [dx_, do_], [dg_, db_]
        (dx1_res, ddn), (g_l2g, g_l2b) = _rowwise(fn_ln_bwd, [_whole(s["x1"]), _whole(s["dn"]), _whole(dx)], [p["l2g"], p["l2b"]],
                                                  [(D, F32), (D, BF16)], [(1, D), (1, D)], TM, "ln2_bwd")
        gsmall["ln2_g"][l], gsmall["ln2_b"][l] = g_l2g, g_l2b
        row_chunks = lambda g: g.reshape(N_CHIPS, g.shape[0] // N_CHIPS, g.shape[1])
        gbig["w_down"][l] = row_chunks(_mm(s["act"], ddn, "tn", F32, "mm_dw_down"))
        dact = _mm(ddn, p["w_down"], "nt", BF16, "mm_dact")

        def fn_glu_bwd(i, rows, ps):
            _, vjp = jax.vjp(glu_math, rows[0], rows[1])
            dcg, dcv = vjp(rows[2].astype(F32))
            return [jnp.concatenate([dcg, dcv], axis=1)], []
        (dc,), _ = _rowwise(fn_glu_bwd, [(s["cv"], FF, 0), (s["cv"], FF, 1), _whole(dact)], [], [(FF2, BF16)], [], TMW, "glu_bwd")
        dup, g_cw, g_cb = _conv_bwd(s["up"], dc, p["cw"])
        gsmall["conv_w"][l], gsmall["conv_b"][l] = g_cw, g_cb
        if pending is None:
            gbig["w_up"][l] = _mm(s["x1b"], dup, "tn", F32, "mm_dw_up", col_chunks=N_CHIPS, tm=2048)
        else:
            gbig["w_up"][l], from_sibling = _mm(s["x1b"], dup, "tn", F32, "mm_dw_up", col_chunks=N_CHIPS, tm=2048,
                                                carry=_swap_stage(pending))
            partial = [_add_half(g, r, cc, f"rs_add_{k}_l{l + 1}") for k, g, r in zip(BIG, pending, from_sibling)]
        dx1 = _mm(dup, p["w_up"], "nt", F32, "mm_dx1", add=dx1_res)
        (dx0_res, do_), (g_l1g, g_l1b) = _rowwise(fn_ln_bwd, [_whole(s["x0"]), _whole(s["o"]), _whole(dx1)], [p["l1g"], p["l1b"]],
                                                  [(D, F32), (D, BF16)], [(1, D), (1, D)], TM, "ln1_bwd")
        gsmall["ln1_g"][l], gsmall["ln1_b"][l] = g_l1g, g_l1b
        gbig["w_o"][l] = row_chunks(_mm(s["merged"], do_, "tn", F32, "mm_dw_o"))
        dmerged = _mm(do_, p["w_o"], "nt", F32, "mm_dmerged")

        def fn_merge_bwd(i, rows, ps):
            bgv = ps[0]
            _, vjp = jax.vjp(merge_math, rows[0], rows[1], rows[2], rows[3], bgv[0:1], bgv[1:2], bgv[2:3])
            dpa, dpb, dpc, dg_, db0, db1, db2 = vjp(rows[4])
            return [dpa, dpb, dpc, dg_], [db0, db1, db2]
        (dpa, dpb, dpc, dgl), (db0, db1, db2) = _rowwise(
            fn_merge_bwd, [_whole(s["pa"]), _whole(s["pb"]), _whole(s["pc"]), (s["h"], N_BRANCHES * D, cb("g")), _whole(dmerged)],
            [p["bg"]], [(D, BF16), (D, BF16), (D, BF16), (N_BRANCHES * D, BF16)], [(1, D)] * 3, TMW, "merge_bwd")
        gsmall["b_gate"][l] = jnp.concatenate([db0, db1, db2], axis=0)
        gbig["w_proj_a"][l] = _mm(s["y_a"], dpa, "tn", F32, "mm_dw_pa", col_chunks=N_CHIPS)
        gbig["w_proj_b"][l] = row_chunks(_mm(s["y_b"], dpb, "tn", F32, "mm_dw_pb"))
        gbig["w_proj_c"][l] = _mm(s["y_c"], dpc, "tn", F32, "mm_dw_pc", col_chunks=N_CHIPS)
        dy_a = _mm(dpa, p["w_pa"], "nt", BF16, "mm_dy_a")
        dy_b = _mm(dpb, p["w_pb"], "nt", BF16, "mm_dy_b")
        dy_c = _mm(dpc, p["w_pc"], "nt", BF16, "mm_dy_c")
        dh_c, g_sg, g_sb, g_sw, g_sb3 = _sgu_bwd(s["h"], cb("hu"), cb("hv"), dy_c, p["sg"], p["sb"], p["sw"], p["sb3"])
        gsmall["sgu_ln_g"][l], gsmall["sgu_ln_b"][l], gsmall["sgu_w"][l] = g_sg, g_sb, g_sw
        gsmall["sgu_b"][l] = g_sb3.reshape(SGU_GROUPS, SGU_CHUNK)
        part_x = ["w_up", "w_down"]
        part_y = [k for k in BIG if k not in part_x]
        scatter_of = lambda names: None if pending is None else _scatter_stage([partial[BIG.index(k)][1] for k in names])
        hide_early = l == 0
        early = [gbig[k][l] for k in EARLY]
        n_early = len(EARLY) if hide_early else 0
        (dqa, dka, dva, g_sinks), landed_ = _swa_bwd(s["h"], cb("qa"), cb("ka"), cb("va"), p["sinks"], dy_a,
                                                     carry=_both(_swap_stage(early) if hide_early else None, scatter_of(part_x)))
        early_from_sibling, arrived_x = landed_[:n_early], landed_[n_early:]
        gsmall["sinks"][l] = g_sinks
        if hide_early:
            early_partial = [_add_half(g, r, cc, f"rs_add_{k}_l{l}") for k, g, r in zip(EARLY, early, early_from_sibling)]
        delta = _mla_delta(dy_b, s["y_b"], T)
        (dqn, dqr), arrived_y = _mla_bwd_dq(s["q_full"], s["qr"], s["kv"], s["kr"], dy_b, s["lse"], delta, T,
                                            carry=scatter_of(part_y))
        if pending is not None:
            arrived = {**dict(zip(part_x, arrived_x)), **dict(zip(part_y, arrived_y))}
            for k, (p32, _) in zip(BIG, partial):
                reduced[k] = _sum_into(p32, arrived[k], chip, cc, l + 1, L, reduced.get(k), f"rs_sum_{k}_l{l + 1}")
        (dkn, dv, dkr_heads), early_arrived = _mla_bwd_dkv(
            s["q_full"], s["qr"], s["kv"], s["kr"], dy_b, s["lse"], delta, T,
            carry=_scatter_stage([p16 for _, p16 in early_partial]) if hide_early else None)
        if hide_early:
            for k, (p32, _), arr in zip(EARLY, early_partial, early_arrived):
                reduced[k] = _sum_into(p32, arr, chip, cc, l, L, reduced.get(k), f"rs_sum_{k}_l{l}")
        dkr = _sum_heads(dkr_heads, TM)
        dqr_raw, dkr_raw = _rope_call(dqr, MLA_HEADS * LANES, 0, dkr, LANES, 0, cos_t, sin_t, -1.0, TM, "rope_bwd")
        dq_full = jnp.concatenate([dqn, dqr_raw], axis=1)
        dkv = jnp.concatenate([dkn, dv], axis=1)
        gbig["w_uq"][l] = _col_chunks(_unperm_w_uq(_mm(s["cqn"], dq_full, "tn", F32, "mm_dw_uq")))
        gbig["w_ukv"][l] = _col_chunks(_unperm_w_ukv(_mm(s["ckvn"], dkv, "tn", F32, "mm_dw_ukv")))
        dcqn = _mm(dq_full, p["w_uq"], "nt", F32, "mm_dcqn")
        dckvn = _mm(dkv, p["w_ukv"], "nt", F32, "mm_dckvn")

        def fn_rms_bwd(i, rows, ps):
            _, vjp1 = jax.vjp(lambda c_, g_: _rms_norm(c_.astype(F32), g_), rows[0], ps[0])
            _, vjp2 = jax.vjp(lambda c_, g_: _rms_norm(c_.astype(F32), g_), rows[1], ps[1])
            d1, dg1 = vjp1(rows[2])
            d2, dg2 = vjp2(rows[3])
            return [d1, d2], [dg1, dg2]
        (dcq, dckv), (g_qg, g_kvg) = _rowwise(
            fn_rms_bwd, [(s["h"], MLA_Q_RANK, cb("cq")), (s["h"], MLA_KV_RANK, cb("ckv")), _whole(dcqn), _whole(dckvn)],
            [p["qg"], p["kvg"]], [(MLA_Q_RANK, BF16), (MLA_KV_RANK, BF16)], [(1, MLA_Q_RANK), (1, MLA_KV_RANK)], TM, "mla_rms_bwd")
        gsmall["q_norm_g"][l], gsmall["kv_norm_g"][l] = g_qg, g_kvg
        tail = jnp.zeros((S, NP - lay["kr"][0] - LANES), BF16)
        dh = jnp.concatenate([dgl, dqa, dcq, dckv, dh_c, dka, dva, dkr_raw, tail], axis=1)
        if l == 0:
            done = [list(range(L)) if k in EARLY else list(range(1, L)) for k in BIG]
            dw_in, joined = _mm(s["x0b"], dh, "tn", F32, "mm_dw_in", carry=_join_stage([reduced[k] for k in BIG], done))
            reduced.update(zip(BIG, joined))
        else:
            dw_in = _mm(s["x0b"], dh, "tn", F32, "mm_dw_in")
        gbig["w_in"][l] = _col_chunks(_unperm_w_in(dw_in, lay, D))
        if l > 0:
            dx = _mm(dh, p["w_in"], "nt", F32, "mm_dx0", add=dx0_res)

        pending = [gbig[k][l] for k in BIG]

    late = [gbig[k][0] for k in LATE]
    from_sibling = _run_stage(_swap_stage(late), "rs_swap_halves_l0")
    partial = [_add_half(g, r, cc, f"rs_add_{k}_l0") for k, g, r in zip(LATE, late, from_sibling)]
    dx, arrived = _mm(dh, p["w_in"], "nt", F32, "mm_dx0", add=dx0_res, carry=_scatter_stage([p16 for _, p16 in partial]))
    for k, (p32, _), arr in zip(LATE, partial, arrived):
        reduced[k] = _sum_into(p32, arr, chip, cc, 0, L, reduced.get(k), f"rs_sum_{k}_l0")

    grad_x = dx.reshape(x.shape)
    g_big = {**reduced, **dict(zip(LATE, _run_stage(_join_stage([reduced[k] for k in LATE], [[0]] * len(LATE)), "rs_join_halves")))}

    small_shapes = {k: (small_sharded_full[k] if k in SMALL_SHARDED else tuple(W[k].shape)) for k in SMALL}
    sv = jnp.concatenate([jnp.stack(gsmall[k]).reshape(-1) for k in SMALL])
    n_sv = sv.shape[0]
    sv = jnp.pad(sv, (0, -n_sv % (8 * LANES))).reshape(-1, LANES)
    sv = _all_reduce_small(sv).reshape(-1)
    g_small, o_ = {}, 0
    for k in SMALL:
        n = int(np.prod(small_shapes[k]))
        g = sv[o_:o_ + n].reshape(small_shapes[k])
        if k in SMALL_SHARDED:
            g = lax.dynamic_slice_in_dim(g, chip * W[k].shape[-1], W[k].shape[-1], axis=-1)
        g_small[k] = g
        o_ += n

    delta, new_m, new_v = {}, {}, {}
    swap_minor = lambda t: jnp.transpose(t, (0, 2, 1))
    for k in BIG:
        if shards[k][2] % LANES and not shards[k][1] % LANES:
            out = _adamw(swap_minor(W[k]), swap_minor(g_big[k]), swap_minor(Mo[k]), swap_minor(Vo[k]), "adamw_" + k)
            delta[k], new_m[k], new_v[k] = [swap_minor(t) for t in out]
        else:
            delta[k], new_m[k], new_v[k] = _adamw(W[k], g_big[k], Mo[k], Vo[k], "adamw_" + k)
    pack = lambda t: jnp.concatenate([t[k].reshape(-1) for k in SMALL])
    n_small = sum(int(np.prod(W[k].shape)) for k in SMALL)
    pad2 = lambda t: jnp.pad(t, (0, -n_small % (8 * LANES))).reshape(1, -1, LANES)
    d_, m_, v_ = _adamw(pad2(pack(W)), pad2(pack(g_small)), pad2(pack(Mo)), pad2(pack(Vo)), "adamw_small")
    o_ = 0
    for k in SMALL:
        n = int(np.prod(W[k].shape))
        take = lambda t: t.reshape(-1)[o_:o_ + n].reshape(W[k].shape)
        delta[k], new_m[k], new_v[k] = take(d_), take(m_), take(v_)
        o_ += n

    grads = {**g_big, **g_small}
    return (loss, grad_x, *[grads[k] for k in WEIGHTS], *[delta[k] for k in WEIGHTS], *[new_m[k] for k in WEIGHTS],
            *[new_v[k] for k in WEIGHTS])
```

```python
import functools
import math

import jax
import jax.numpy as jnp
import numpy as np
from jax import lax
from jax.experimental import pallas as pl
from jax.experimental.pallas import tpu as pltpu

F32, BF16 = jnp.float32, jnp.bfloat16
SDS = jax.ShapeDtypeStruct
MESH = pl.DeviceIdType.MESH

SWA_Q_HEADS, SWA_KV_HEADS, SWA_HEAD_DIM, SWA_BLOCK = 16, 2, 64, 128
MLA_HEADS, MLA_NOPE, MLA_ROPE, MLA_V = 16, 128, 64, 128
MLA_Q_RANK, MLA_KV_RANK = 512, 512
ROPE_THETA = 10000.0
SGU_GROUPS, SGU_GROUP_DIM, SGU_CHUNK = 8, 128, 128
SGU_WIDTH = SGU_GROUPS * SGU_GROUP_DIM
A_Q = SWA_Q_HEADS * SWA_HEAD_DIM
A_KV = SWA_KV_HEADS * SWA_HEAD_DIM
N_BRANCHES = 3
DEPTH = 2
EPS = 1e-5
MASK_VALUE = -1e30
DN_ALPHA = (2 * DEPTH) ** 0.25
ADAM_LR, ADAM_B1, ADAM_B2, ADAM_EPS, ADAM_WD, ADAM_STEP = 0.001, 0.9, 0.999, 1e-08, 0.01, 10
N_CHIPS = 4

LANES = 128
VMEM_LIMIT = 48 * 1024 * 1024

BIG = ["w_in", "w_uq", "w_ukv", "w_proj_a", "w_proj_b", "w_proj_c", "w_o", "w_up", "w_down"]
ROW_SHARDED = {"w_proj_b", "w_o", "w_down"}
LATE = ["w_in", "w_uq", "w_ukv"]
EARLY = [k for k in BIG if k not in LATE]
SMALL = ["b_gate", "sinks", "q_norm_g", "kv_norm_g", "sgu_ln_g", "sgu_ln_b", "sgu_w", "sgu_b", "ln1_g", "ln1_b",
         "conv_w", "conv_b", "ln2_g", "ln2_b"]
SMALL_SHARDED = {"b_gate", "conv_w"}
WEIGHTS = ["w_in", "b_gate", "sinks", "q_norm_g", "kv_norm_g", "w_uq", "w_ukv", "sgu_ln_g", "sgu_ln_b", "sgu_w", "sgu_b",
           "w_proj_a", "w_proj_b", "w_proj_c", "w_o", "ln1_g", "ln1_b", "w_up", "conv_w", "conv_b", "w_down", "ln2_g", "ln2_b"]


def _pcall(body, **kw):
    return pl.pallas_call(body, **kw)


def _params(sem=None):
    return pltpu.CompilerParams(dimension_semantics=sem, vmem_limit_bytes=VMEM_LIMIT)


def _tile(dim, pref, align=LANES):
    t = (min(pref, dim) // align) * align
    while t >= align:
        if dim % t == 0:
            return t
        t -= align
    return dim


def _mm(a, b, mode, out_dtype, name, add=None, tm=1024, tn=512, tk=2048, col_chunks=1, carry=None):
    if mode == "nn":
        (M, K), (K2, N) = a.shape, b.shape
    elif mode == "nt":
        (M, K), (N, K2) = a.shape, b.shape
    else:
        (K, M), (K2, N) = a.shape, b.shape
    assert K == K2, (a.shape, b.shape, mode)
    assert N % col_chunks == 0
    tm, tn, tk = _tile(M, tm), _tile(N // col_chunks, tn), _tile(K, tk)
    assert (N // col_chunks) % tn == 0
    per_chunk = (N // col_chunks) // tn
    nk = K // tk
    if mode == "tn":
        a_spec = pl.BlockSpec((tk, tm), lambda i, j, k: (k, i))
    else:
        a_spec = pl.BlockSpec((tm, tk), lambda i, j, k: (i, k))
    if mode == "nt":
        b_spec = pl.BlockSpec((tn, tk), lambda i, j, k: (j, k))
    else:
        b_spec = pl.BlockSpec((tk, tn), lambda i, j, k: (k, j))
    dn = {"nn": (((1,), (0,)), ((), ())), "nt": (((1,), (1,)), ((), ())), "tn": (((0,), (0,)), ((), ()))}[mode]
    chunked = col_chunks > 1
    if chunked:
        assert add is None
        o_spec = pl.BlockSpec((1, tm, tn), lambda i, j, k: (lax.div(j, per_chunk), i, lax.rem(j, per_chunk)))
        out_shape = SDS((col_chunks, M, N // col_chunks), out_dtype)
    else:
        o_spec = pl.BlockSpec((tm, tn), lambda i, j, k: (i, j))
        out_shape = SDS((M, N), out_dtype)
    has_add = add is not None

    def body(*refs):
        if has_add:
            a_ref, b_ref, add_ref, o_ref, acc_ref = refs
        else:
            a_ref, b_ref, o_ref, acc_ref = refs
        k = pl.program_id(2)

        @pl.when(k == 0)
        def _():
            acc_ref[...] = jnp.zeros_like(acc_ref)

        acc_ref[...] += lax.dot_general(a_ref[...].astype(BF16), b_ref[...].astype(BF16), dn,
                                        preferred_element_type=F32)

        @pl.when(k == nk - 1)
        def _():
            r = acc_ref[...]
            if has_add:
                r = r + add_ref[...].astype(F32)
            if chunked:
                o_ref[0] = r.astype(o_ref.dtype)
            else:
                o_ref[...] = r.astype(o_ref.dtype)

    ins = [a, b] + ([add] if has_add else [])
    in_specs = [a_spec, b_spec] + ([o_spec] if has_add else [])
    (out,), carried = _call(body, name, (M // tm, N // tn, nk), 0, in_specs, [o_spec], [out_shape], [pltpu.VMEM((tm, tn), F32)],
                            ins, ("parallel", "parallel", "arbitrary"), carry)
    return out if carry is None else (out, carried)


def _mm_tn(a, b, out_dtype, name, tm=512, tn=512, col_chunks=1, carry=None):
    (K, M), (K2, N) = a.shape, b.shape
    assert K == K2 and N % col_chunks == 0
    tm, tn = _tile(M, tm), _tile(N // col_chunks, tn)
    per_chunk = (N // col_chunks) // tn
    chunked = col_chunks > 1
    if chunked:
        o_spec = pl.BlockSpec((1, tm, tn), lambda i, j: (lax.div(j, per_chunk), i, lax.rem(j, per_chunk)))
        out_shape = SDS((col_chunks, M, N // col_chunks), out_dtype)
    else:
        o_spec = pl.BlockSpec((tm, tn), lambda i, j: (i, j))
        out_shape = SDS((M, N), out_dtype)

    def body(a_ref, b_ref, o_ref, at_ref):
        @pl.when(pl.program_id(1) == 0)
        def _():
            at_ref[...] = a_ref[...].astype(BF16).T

        r = jnp.dot(at_ref[...], b_ref[...].astype(BF16), preferred_element_type=F32)
        if chunked:
            o_ref[0] = r.astype(o_ref.dtype)
        else:
            o_ref[...] = r.astype(o_ref.dtype)

    (out,), carried = _call(body, name, (M // tm, N // tn), 0,
                            [pl.BlockSpec((K, tm), lambda i, j: (0, i)), pl.BlockSpec((K, tn), lambda i, j: (0, j))],
                            [o_spec], [out_shape], [pltpu.VMEM((tm, K), BF16)], [a, b], ("parallel", "arbitrary"), carry)
    return out if carry is None else (out, carried)


def _rowwise(fn, rows, params, row_outs, acc_outs, tm, name):
    n_rows = rows[0][0].shape[0]
    assert n_rows % tm == 0
    nr, npar, no = len(rows), len(params), len(row_outs)

    def body(*refs):
        i = pl.program_id(0)
        r, p = refs[:nr], refs[nr:nr + npar]
        o, acc = refs[nr + npar:nr + npar + no], refs[nr + npar + no:]
        outs, sums = fn(i, [x[...] for x in r], [x[...] for x in p])
        for ref, val in zip(o, outs, strict=True):
            ref[...] = val.astype(ref.dtype)
        if acc:
            @pl.when(i == 0)
            def _():
                for ref in acc:
                    ref[...] = jnp.zeros_like(ref)
            for ref, val in zip(acc, sums, strict=True):
                ref[...] += val.astype(F32)

    def full(shape):
        nd = len(shape)
        return pl.BlockSpec(tuple(shape), lambda i: (0,) * nd)

    in_specs = [pl.BlockSpec((tm, w), (lambda i, cb=cb: (i, cb))) for (_, w, cb) in rows] + [full(p.shape) for p in params]
    out_specs = [pl.BlockSpec((tm, w), lambda i: (i, 0)) for (w, _) in row_outs] + [full(s) for s in acc_outs]
    out_shape = [SDS((n_rows, w), dt) for (w, dt) in row_outs] + [SDS(tuple(s), F32) for s in acc_outs]
    res = _pcall(body, name=name, out_shape=out_shape, grid=(n_rows // tm,), in_specs=in_specs, out_specs=out_specs,
                 compiler_params=_params(("arbitrary",)))(*[r[0] for r in rows], *params)
    return list(res[:no]), list(res[no:])


def _whole(a):
    return (a, a.shape[1], 0)


def _gelu(x):
    return 0.5 * x * (1.0 + lax.erf(x * (1.0 / math.sqrt(2.0))))


def _layer_norm(x, g, b):
    mu = x.mean(-1, keepdims=True)
    var = jnp.mean(jnp.square(x - mu), -1, keepdims=True)
    return (x - mu) * lax.rsqrt(var + EPS) * g + b


def _rms_norm(x, g):
    return x * lax.rsqrt(jnp.mean(jnp.square(x), -1, keepdims=True) + EPS) * g


def _sgu_math(hu, hv, ln_g, ln_b, ws, bs):
    u = _gelu(hu.astype(F32))
    vn = _layer_norm(_gelu(hv.astype(F32)), ln_g, ln_b)
    r = lax.broadcasted_iota(jnp.int32, (SGU_CHUNK, SGU_CHUNK), 0)
    c = lax.broadcasted_iota(jnp.int32, (SGU_CHUNK, SGU_CHUNK), 1)
    outs = []
    for g in range(SGU_GROUPS):
        w = jnp.where(r >= c, ws[g], 0.0).astype(BF16)
        vg = vn[:, g * SGU_GROUP_DIM:(g + 1) * SGU_GROUP_DIM].astype(BF16)
        outs.append(jnp.dot(w, vg, preferred_element_type=F32) + bs[g])
    return u * jnp.concatenate(outs, axis=1)


def _sgu_fwd(h, cu, cv, ln_g, ln_b, w, b3):
    def fn(i, rows, ps):
        g_, b_, w_, b3_ = ps
        y = _sgu_math(rows[0], rows[1], g_, b_, [w_[g] for g in range(SGU_GROUPS)], [b3_[g] for g in range(SGU_GROUPS)])
        return [y], []
    (y,), _ = _rowwise(fn, [(h, SGU_WIDTH, cu), (h, SGU_WIDTH, cv)], [ln_g, ln_b, w, b3], [(SGU_WIDTH, BF16)], [],
                       SGU_CHUNK, "sgu_fwd")
    return y


def _sgu_bwd(h, cu, cv, dy, ln_g, ln_b, w, b3):
    nd = 2 * SGU_WIDTH

    def body(hu_ref, hv_ref, dy_ref, g_ref, b_ref, w_ref, b3_ref, dh_ref, dg_ref, db_ref, dw_ref, db3_ref):
        i = pl.program_id(0)

        @pl.when(i == 0)
        def _():
            dg_ref[...] = jnp.zeros_like(dg_ref)
            db_ref[...] = jnp.zeros_like(db_ref)
            dw_ref[...] = jnp.zeros_like(dw_ref)
            db3_ref[...] = jnp.zeros_like(db3_ref)

        ws = [w_ref[g] for g in range(SGU_GROUPS)]
        bs = [b3_ref[g] for g in range(SGU_GROUPS)]
        _, vjp = jax.vjp(_sgu_math, hu_ref[...], hv_ref[...], g_ref[...], b_ref[...], ws, bs)
        dhu, dhv, dg, db, dws, dbs = vjp(dy_ref[...].astype(F32))
        dh_ref[...] = jnp.concatenate([dhu, dhv], axis=1).astype(dh_ref.dtype)
        dg_ref[...] += dg
        db_ref[...] += db
        for g in range(SGU_GROUPS):
            dw_ref[g] += dws[g]
            db3_ref[g] += dbs[g]

    n = h.shape[0]
    blk = lambda cb: pl.BlockSpec((SGU_CHUNK, SGU_WIDTH), lambda i, cb=cb: (i, cb))
    full = lambda s: pl.BlockSpec(tuple(s), lambda i: (0,) * len(s))
    return _pcall(
        body, name="sgu_bwd", grid=(n // SGU_CHUNK,),
        out_shape=[SDS((n, nd), BF16), SDS(ln_g.shape, F32), SDS(ln_b.shape, F32), SDS(w.shape, F32), SDS(b3.shape, F32)],
        in_specs=[blk(cu), blk(cv), blk(0), full(ln_g.shape), full(ln_b.shape), full(w.shape), full(b3.shape)],
        out_specs=[pl.BlockSpec((SGU_CHUNK, nd), lambda i: (i, 0)), full(ln_g.shape), full(ln_b.shape), full(w.shape),
                   full(b3.shape)],
        compiler_params=_params(("arbitrary",)))(h, h, dy, ln_g, ln_b, w, b3)


def _swa_math(q, kp, kc, vp, vc, sinks, not_first):
    kw = jnp.concatenate([kp, kc], axis=0).astype(BF16)
    vw = jnp.concatenate([vp, vc], axis=0).astype(BF16)
    qb = q.astype(BF16)
    q_off = lax.broadcasted_iota(jnp.int32, (SWA_BLOCK, 2 * SWA_BLOCK), 0) + SWA_BLOCK
    k_off = lax.broadcasted_iota(jnp.int32, (SWA_BLOCK, 2 * SWA_BLOCK), 1)
    rel = q_off - k_off
    valid = (rel >= 0) & (rel < SWA_BLOCK) & (not_first | (k_off >= SWA_BLOCK))
    G = SWA_Q_HEADS // SWA_KV_HEADS
    outs = []
    for head in range(SWA_Q_HEADS):
        hk = head // G
        qh = qb[:, head * SWA_HEAD_DIM:(head + 1) * SWA_HEAD_DIM]
        kh = kw[:, hk * SWA_HEAD_DIM:(hk + 1) * SWA_HEAD_DIM]
        vh = vw[:, hk * SWA_HEAD_DIM:(hk + 1) * SWA_HEAD_DIM]
        s = lax.dot_general(qh, kh, (((1,), (1,)), ((), ())), preferred_element_type=F32) * (SWA_HEAD_DIM ** -0.5)
        s = jnp.where(valid, s, MASK_VALUE)
        sink = sinks[:, head:head + 1]
        m = jnp.maximum(s.max(-1, keepdims=True), sink)
        p = jnp.exp(s - m)
        p = (p / (p.sum(-1, keepdims=True) + jnp.exp(sink - m))).astype(BF16)
        outs.append(jnp.dot(p, vh, preferred_element_type=F32))
    return jnp.concatenate(outs, axis=1)


def _swa_fwd(h, cq, ck, cv, sinks, carry=None):
    n = h.shape[0]
    nb = n // SWA_BLOCK

    def body(q_ref, kp_ref, kc_ref, vp_ref, vc_ref, s_ref, o_ref):
        i = pl.program_id(0)
        f = lambda x: x[...].astype(F32)
        o_ref[...] = _swa_math(f(q_ref), f(kp_ref), f(kc_ref), f(vp_ref), f(vc_ref), s_ref[...], i > 0).astype(o_ref.dtype)

    prev = lambda cb: pl.BlockSpec((SWA_BLOCK, A_KV), lambda i, cb=cb: (jnp.maximum(i - 1, 0), cb))
    cur = lambda cb: pl.BlockSpec((SWA_BLOCK, A_KV), lambda i, cb=cb: (i, cb))
    return _call(body, "swa_fwd", (nb,), 0,
                 [pl.BlockSpec((SWA_BLOCK, A_Q), lambda i: (i, cq)), prev(ck), cur(ck), prev(cv), cur(cv),
                  pl.BlockSpec((1, SWA_Q_HEADS), lambda i: (0, 0))],
                 [pl.BlockSpec((SWA_BLOCK, A_Q), lambda i: (i, 0))], [SDS((n, A_Q), BF16)], [],
                 [h, h, h, h, h, sinks], ("arbitrary",), carry)


def _swa_bwd(h, cq, ck, cv, sinks, dy, carry=None):
    n = h.shape[0]
    nb = n // SWA_BLOCK

    def body(q_ref, kp_ref, kc_ref, vp_ref, vc_ref, s_ref, dy_ref, dq_ref, dk_ref, dv_ref, ds_ref, ck_ref, cv_ref):
        r = pl.program_id(0)
        blk = nb - 1 - r

        @pl.when(r == 0)
        def _():
            ds_ref[...] = jnp.zeros_like(ds_ref)
            ck_ref[...] = jnp.zeros_like(ck_ref)
            cv_ref[...] = jnp.zeros_like(cv_ref)

        f = lambda x: x[...].astype(F32)
        not_first = blk > 0
        _, vjp = jax.vjp(lambda q, kp, kc, vp, vc, s: _swa_math(q, kp, kc, vp, vc, s, not_first),
                         f(q_ref), f(kp_ref), f(kc_ref), f(vp_ref), f(vc_ref), s_ref[...])
        dq, dkp, dkc, dvp, dvc, dsk = vjp(f(dy_ref))
        dq_ref[...] = dq.astype(dq_ref.dtype)
        dk_ref[...] = (dkc + ck_ref[...]).astype(dk_ref.dtype)
        dv_ref[...] = (dvc + cv_ref[...]).astype(dv_ref.dtype)
        ck_ref[...] = dkp
        cv_ref[...] = dvp
        ds_ref[...] += dsk

    rev = lambda i: nb - 1 - i
    prev = lambda cb: pl.BlockSpec((SWA_BLOCK, A_KV), lambda i, cb=cb: (jnp.maximum(rev(i) - 1, 0), cb))
    cur = lambda cb: pl.BlockSpec((SWA_BLOCK, A_KV), lambda i, cb=cb: (rev(i), cb))
    return _call(
        body, "swa_bwd", (nb,), 0,
        [pl.BlockSpec((SWA_BLOCK, A_Q), lambda i: (rev(i), cq)), prev(ck), cur(ck), prev(cv), cur(cv),
         pl.BlockSpec((1, SWA_Q_HEADS), lambda i: (0, 0)), pl.BlockSpec((SWA_BLOCK, A_Q), lambda i: (rev(i), 0))],
        [pl.BlockSpec((SWA_BLOCK, A_Q), lambda i: (rev(i), 0)), pl.BlockSpec((SWA_BLOCK, A_KV), lambda i: (rev(i), 0)),
         pl.BlockSpec((SWA_BLOCK, A_KV), lambda i: (rev(i), 0)), pl.BlockSpec((1, SWA_Q_HEADS), lambda i: (0, 0))],
        [SDS((n, A_Q), BF16), SDS((n, A_KV), BF16), SDS((n, A_KV), BF16), SDS((1, SWA_Q_HEADS), F32)],
        [pltpu.VMEM((SWA_BLOCK, A_KV), F32), pltpu.VMEM((SWA_BLOCK, A_KV), F32)],
        [h, h, h, h, h, sinks, dy], ("arbitrary",), carry)


def _rope(x, cos, sin, sign):
    w = x.shape[1]
    reps = w // LANES
    ct = jnp.tile(cos, (1, reps)) if reps > 1 else cos
    st = jnp.tile(sin, (1, reps)) if reps > 1 else sin
    fwd = pltpu.roll(x, MLA_ROPE // 2, axis=1)
    bwd = pltpu.roll(x, w - MLA_ROPE // 2, axis=1)
    lane = lax.broadcasted_iota(jnp.int32, x.shape, 1) % LANES
    rot = jnp.where(lane < MLA_ROPE // 2, -bwd, fwd)
    return x * ct + sign * (rot * st)


def _rope_call(a, wa, ca, b, wb, cb, cos, sin, sign, tm, name):
    def fn(i, rows, ps):
        xa, xb, c_, s_ = rows
        return [_rope(xa.astype(F32), c_, s_, sign), _rope(xb.astype(F32), c_, s_, sign)], []
    (ra, rb), _ = _rowwise(fn, [(a, wa, ca), (b, wb, cb), _whole(cos), _whole(sin)], [], [(wa, BF16), (wb, BF16)], [], tm, name)
    return ra, rb


MLA_SCALE = (MLA_NOPE + MLA_ROPE) ** -0.5
LOG2E = math.log2(math.e)


MLA_HP = 8
MLA_W = MLA_HP * LANES


def _mla_scores(qn_ref, qr_ref, kn_ref, kr_ref, hh, masked):
    cols = slice(hh * LANES, (hh + 1) * LANES)
    q = jnp.concatenate([qn_ref[:, cols], qr_ref[:, cols]], axis=1)
    k = jnp.concatenate([kn_ref[:, cols], kr_ref[...]], axis=1)
    s = lax.dot_general(q, k, (((1,), (1,)), ((), ())), preferred_element_type=F32)
    if masked:
        row = lax.broadcasted_iota(jnp.int32, s.shape, 0)
        col = lax.broadcasted_iota(jnp.int32, s.shape, 1)
        s = jnp.where(col <= row, s, MASK_VALUE)
    return s, q, k


def _causal_pairs(nq, by_query):
    if by_query:
        pairs = [(i, j) for i in range(nq) for j in range(i + 1)]
    else:
        pairs = [(i, j) for j in range(nq) for i in range(j, nq)]
    return (jnp.asarray(np.array([p[0] for p in pairs], np.int32)), jnp.asarray(np.array([p[1] for p in pairs], np.int32)),
            len(pairs))


def _mla_fwd(q_full, qr, kv, kr, T, carry=None):
    n = q_full.shape[0]
    nq = n // T
    H = MLA_HEADS
    qi, kj, npairs = _causal_pairs(nq, True)

    def body(qi_ref, kj_ref, qn_ref, qr_ref, kn_ref, v_ref, kr_ref, y_ref, lse_ref, m_ref, l_ref, acc_ref):
        t = pl.program_id(1)
        i, j = qi_ref[t], kj_ref[t]

        @pl.when(j == 0)
        def _():
            m_ref[...] = jnp.full_like(m_ref, MASK_VALUE)
            l_ref[...] = jnp.zeros_like(l_ref)
            acc_ref[...] = jnp.zeros_like(acc_ref)

        def update(masked):
            for hh in range(MLA_HP):
                s, _, _ = _mla_scores(qn_ref, qr_ref, kn_ref, kr_ref, hh, masked)
                m_prev = m_ref[hh]
                m_new = jnp.maximum(m_prev, s.max(-1, keepdims=True))
                p = jnp.exp2((s - m_new[:, :1]) * (MLA_SCALE * LOG2E))
                alpha = jnp.exp2((m_prev - m_new) * (MLA_SCALE * LOG2E))
                l_ref[hh] = alpha * l_ref[hh] + p.sum(-1, keepdims=True)
                acc_ref[hh] = alpha * acc_ref[hh] + jnp.dot(p.astype(BF16), v_ref[:, hh * LANES:(hh + 1) * LANES],
                                                            preferred_element_type=F32)
                m_ref[hh] = m_new

        @pl.when(j < i)
        def _():
            update(False)

        @pl.when(j == i)
        def _():
            update(True)
            for hh in range(MLA_HP):
                y_ref[:, hh * LANES:(hh + 1) * LANES] = (acc_ref[hh] / l_ref[hh]).astype(y_ref.dtype)
                lse_ref[hh] = m_ref[hh] * (MLA_SCALE * LOG2E) + jnp.log2(l_ref[hh])

    G = H // MLA_HP
    qspec = lambda off: pl.BlockSpec((T, MLA_W), lambda h, t, qi, kj, off=off: (qi[t], off + h))
    kspec = lambda off: pl.BlockSpec((T, MLA_W), lambda h, t, qi, kj, off=off: (kj[t], off + h))
    return _call(
        body, "mla_fwd", (G, npairs), 2,
        [qspec(0), qspec(0), kspec(0), kspec(G), pl.BlockSpec((T, LANES), lambda h, t, qi, kj: (kj[t], 0))],
        [pl.BlockSpec((T, MLA_W), lambda h, t, qi, kj: (qi[t], h)),
         pl.BlockSpec((MLA_HP, T, LANES), lambda h, t, qi, kj: (h, qi[t], 0))],
        [SDS((n, H * MLA_V), BF16), SDS((H, n, LANES), F32)], [pltpu.VMEM((MLA_HP, T, LANES), F32)] * 3,
        [qi, kj, q_full, qr, kv, kv, kr], ("parallel", "arbitrary"), carry)


def _mla_delta(dy, y, T):
    n = y.shape[0]
    H = MLA_HEADS

    def body(dy_ref, y_ref, d_ref):
        d = jnp.sum(dy_ref[...].astype(F32) * y_ref[...].astype(F32), axis=-1, keepdims=True)
        d_ref[0] = jnp.broadcast_to(d, (T, LANES))

    spec = pl.BlockSpec((T, LANES), lambda h, i: (i, h))
    return _pcall(body, name="mla_delta", grid=(H, n // T), out_shape=SDS((H, n, LANES), F32), in_specs=[spec, spec],
                  out_specs=pl.BlockSpec((1, T, LANES), lambda h, i: (h, i, 0)),
                  compiler_params=_params(("parallel", "parallel")))(dy, y)


def _mla_bwd_dq(q_full, qr, kv, kr, dy, lse, delta, T, carry=None):
    n = q_full.shape[0]
    nq = n // T
    H = MLA_HEADS

    qi, kj, npairs = _causal_pairs(nq, True)

    def body(qi_ref, kj_ref, qn_ref, qr_ref, kn_ref, v_ref, kr_ref, dy_ref, lse_ref, dl_ref, dqn_ref, dqr_ref, acc_ref):
        t = pl.program_id(1)
        i, j = qi_ref[t], kj_ref[t]

        @pl.when(j == 0)
        def _():
            acc_ref[...] = jnp.zeros_like(acc_ref)

        def update(masked):
            for hh in range(MLA_HP):
                cols = slice(hh * LANES, (hh + 1) * LANES)
                s, _, k = _mla_scores(qn_ref, qr_ref, kn_ref, kr_ref, hh, masked)
                p = jnp.exp2(s * (MLA_SCALE * LOG2E) - lse_ref[hh][:, :1])
                dp = lax.dot_general(dy_ref[:, cols], v_ref[:, cols], (((1,), (1,)), ((), ())), preferred_element_type=F32)
                ds = p * (dp - dl_ref[hh][:, :1])
                acc_ref[hh] += jnp.dot(ds.astype(BF16), k, preferred_element_type=F32)

        @pl.when(j < i)
        def _():
            update(False)

        @pl.when(j == i)
        def _():
            update(True)
            for hh in range(MLA_HP):
                cols = slice(hh * LANES, (hh + 1) * LANES)
                dqn_ref[:, cols] = (acc_ref[hh][:, :LANES] * MLA_SCALE).astype(dqn_ref.dtype)
                dqr_ref[:, cols] = (acc_ref[hh][:, LANES:] * MLA_SCALE).astype(dqr_ref.dtype)

    G = H // MLA_HP
    qspec = lambda off: pl.BlockSpec((T, MLA_W), lambda h, t, qi, kj, off=off: (qi[t], off + h))
    kspec = lambda off: pl.BlockSpec((T, MLA_W), lambda h, t, qi, kj, off=off: (kj[t], off + h))
    stat = pl.BlockSpec((MLA_HP, T, LANES), lambda h, t, qi, kj: (h, qi[t], 0))
    out = pl.BlockSpec((T, MLA_W), lambda h, t, qi, kj: (qi[t], h))
    return _call(
        body, "mla_bwd_dq", (G, npairs), 2,
        [qspec(0), qspec(0), kspec(0), kspec(G), pl.BlockSpec((T, LANES), lambda h, t, qi, kj: (kj[t], 0)), qspec(0), stat, stat],
        [out, out], [SDS((n, H * LANES), BF16), SDS((n, H * LANES), BF16)], [pltpu.VMEM((MLA_HP, T, 2 * LANES), F32)],
        [qi, kj, q_full, qr, kv, kv, kr, dy, lse, delta], ("parallel", "arbitrary"), carry)


def _mla_bwd_dkv(q_full, qr, kv, kr, dy, lse, delta, T, carry=None):
    n = q_full.shape[0]
    nq = n // T
    H = MLA_HEADS

    qi, kj, npairs = _causal_pairs(nq, False)

    def body(qi_ref, kj_ref, qn_ref, qr_ref, kn_ref, v_ref, kr_ref, dy_ref, lse_ref, dl_ref, dkn_ref, dv_ref, dkr_ref,
             dk_acc, dv_acc):
        t = pl.program_id(1)
        i, j = qi_ref[t], kj_ref[t]

        @pl.when(i == j)
        def _():
            dk_acc[...] = jnp.zeros_like(dk_acc)
            dv_acc[...] = jnp.zeros_like(dv_acc)

        def update(masked):
            for hh in range(MLA_HP):
                cols = slice(hh * LANES, (hh + 1) * LANES)
                s, q, _ = _mla_scores(qn_ref, qr_ref, kn_ref, kr_ref, hh, masked)
                p = jnp.exp2(s * (MLA_SCALE * LOG2E) - lse_ref[hh][:, :1])
                dy = dy_ref[:, cols]
                dv_acc[hh] += lax.dot_general(p.astype(BF16), dy, (((0,), (0,)), ((), ())), preferred_element_type=F32)
                dp = lax.dot_general(dy, v_ref[:, cols], (((1,), (1,)), ((), ())), preferred_element_type=F32)
                ds = p * (dp - dl_ref[hh][:, :1])
                dk_acc[hh] += lax.dot_general(ds.astype(BF16), q, (((0,), (0,)), ((), ())), preferred_element_type=F32)

        @pl.when(i == j)
        def _():
            update(True)

        @pl.when(i > j)
        def _():
            update(False)

        @pl.when(i == nq - 1)
        def _():
            for hh in range(MLA_HP):
                cols = slice(hh * LANES, (hh + 1) * LANES)
                dkn_ref[:, cols] = (dk_acc[hh][:, :LANES] * MLA_SCALE).astype(dkn_ref.dtype)
                dv_ref[:, cols] = dv_acc[hh].astype(dv_ref.dtype)
                dkr_ref[hh] = dk_acc[hh][:, LANES:] * MLA_SCALE

    G = H // MLA_HP
    qspec = lambda off: pl.BlockSpec((T, MLA_W), lambda h, t, qi, kj, off=off: (qi[t], off + h))
    kspec = lambda off: pl.BlockSpec((T, MLA_W), lambda h, t, qi, kj, off=off: (kj[t], off + h))
    stat = pl.BlockSpec((MLA_HP, T, LANES), lambda h, t, qi, kj: (h, qi[t], 0))
    out = pl.BlockSpec((T, MLA_W), lambda h, t, qi, kj: (kj[t], h))
    return _call(
        body, "mla_bwd_dkv", (G, npairs), 2,
        [qspec(0), qspec(0), kspec(0), kspec(G), pl.BlockSpec((T, LANES), lambda h, t, qi, kj: (kj[t], 0)), qspec(0), stat, stat],
        [out, out, pl.BlockSpec((MLA_HP, T, LANES), lambda h, t, qi, kj: (h, kj[t], 0))],
        [SDS((n, H * LANES), BF16), SDS((n, H * LANES), BF16), SDS((H, n, LANES), F32)],
        [pltpu.VMEM((MLA_HP, T, 2 * LANES), F32), pltpu.VMEM((MLA_HP, T, LANES), F32)],
        [qi, kj, q_full, qr, kv, kv, kr, dy, lse, delta], ("parallel", "arbitrary"), carry)


def _sum_heads(a, tm):
    H, n, _ = a.shape

    def body(a_ref, o_ref):
        o_ref[...] = jnp.sum(a_ref[...], axis=0)

    return _pcall(body, name="mla_sum_heads", grid=(n // tm,), out_shape=SDS((n, LANES), F32),
                  in_specs=[pl.BlockSpec((H, tm, LANES), lambda i: (0, i, 0))],
                  out_specs=pl.BlockSpec((tm, LANES), lambda i: (i, 0)), compiler_params=_params(("parallel",)))(a)


def _shift_down(x, k):
    row = lax.broadcasted_iota(jnp.int32, x.shape, 0)
    return jnp.where(row >= k, pltpu.roll(x, k, axis=0), 0.0)


def _shift_up(x, k):
    n = x.shape[0]
    row = lax.broadcasted_iota(jnp.int32, x.shape, 0)
    return jnp.where(row < n - k, pltpu.roll(x, n - k, axis=0), 0.0)


def _conv_fwd(up, w, b):
    n, c = up.shape

    def body(u_ref, w_ref, b_ref, o_ref):
        u = u_ref[...].astype(F32)
        wv = w_ref[...]
        o_ref[...] = (b_ref[...] + wv[0:1] * _shift_down(u, 2) + wv[1:2] * _shift_down(u, 1) + wv[2:3] * u).astype(o_ref.dtype)

    return _pcall(body, name="conv_fwd", grid=(c // LANES,), out_shape=SDS((n, c), BF16),
                  in_specs=[pl.BlockSpec((n, LANES), lambda j: (0, j)), pl.BlockSpec((3, LANES), lambda j: (0, j)),
                            pl.BlockSpec((1, LANES), lambda j: (0, j))],
                  out_specs=pl.BlockSpec((n, LANES), lambda j: (0, j)), compiler_params=_params(("parallel",)))(up, w, b)


def _conv_bwd(up, dc, w):
    n, c = up.shape

    def body(u_ref, d_ref, w_ref, du_ref, dw_ref, db_ref):
        u = u_ref[...].astype(F32)
        d = d_ref[...].astype(F32)
        wv = w_ref[...]
        du_ref[...] = (wv[2:3] * d + wv[1:2] * _shift_up(d, 1) + wv[0:1] * _shift_up(d, 2)).astype(du_ref.dtype)
        dw_ref[0:1, :] = jnp.sum(d * _shift_down(u, 2), axis=0, keepdims=True)
        dw_ref[1:2, :] = jnp.sum(d * _shift_down(u, 1), axis=0, keepdims=True)
        dw_ref[2:3, :] = jnp.sum(d * u, axis=0, keepdims=True)
        db_ref[...] = jnp.sum(d, axis=0, keepdims=True)

    col = pl.BlockSpec((n, LANES), lambda j: (0, j))
    return _pcall(body, name="conv_bwd", grid=(c // LANES,),
                  out_shape=[SDS((n, c), BF16), SDS((3, c), F32), SDS((1, c), F32)],
                  in_specs=[col, col, pl.BlockSpec((3, LANES), lambda j: (0, j))],
                  out_specs=[col, pl.BlockSpec((3, LANES), lambda j: (0, j)), pl.BlockSpec((1, LANES), lambda j: (0, j))],
                  compiler_params=_params(("parallel",)))(up, dc, w)


def _adamw_math(w, g, m, v):
    m = ADAM_B1 * m + (1.0 - ADAM_B1) * g
    v = ADAM_B2 * v + (1.0 - ADAM_B2) * jnp.square(g)
    m_hat = m / (1.0 - ADAM_B1 ** ADAM_STEP)
    v_hat = v / (1.0 - ADAM_B2 ** ADAM_STEP)
    delta = -ADAM_LR * (m_hat / (jnp.sqrt(v_hat) + ADAM_EPS) + ADAM_WD * w)
    return delta, m, v


def _adamw(w, g, m, v, name):
    L, r, c = w.shape
    tr = _rows_tile(r, c, 1 << 20, 8)

    def body(w_ref, g_ref, m_ref, v_ref, d_ref, nm_ref, nv_ref):
        d, nm, nv = _adamw_math(w_ref[...], g_ref[...], m_ref[...], v_ref[...])
        d_ref[...] = d
        nm_ref[...] = nm
        nv_ref[...] = nv

    spec = pl.BlockSpec((1, tr, c), lambda l, i: (l, i, 0))
    return _pcall(body, name=name, grid=(L, r // tr), out_shape=[SDS(w.shape, F32)] * 3, in_specs=[spec] * 4,
                  out_specs=[spec] * 3, compiler_params=_params(("parallel", "parallel")))(w, g, m, v)


def _coords():
    return lax.axis_index("x"), lax.axis_index("y"), lax.axis_index("c")


def _other_chips(x, y):
    return [(1 - x, y), (x, 1 - y), (1 - x, 1 - y)]


HBM_SPEC = pl.BlockSpec(memory_space=pltpu.HBM)


def _half(c, rows):
    return pl.ds(pl.multiple_of(c * rows, 16), rows)


def _rows_tile(rows, cols, budget_bytes=2 << 20, align=16):
    t = (min(max(align, budget_bytes // (4 * cols)), rows) // align) * align
    while t >= align:
        if rows % t == 0:
            return t
        t -= align
    return rows


def _scalar(v):
    return jnp.reshape(jnp.asarray(v, jnp.int32), (1,))


def _cast_into_slot(w3, layer, slot, name):
    _, R, C = w3.shape
    tr = _rows_tile(R, C)

    def body(s_ref, w_ref, o_ref):
        o_ref[0] = w_ref[0].astype(BF16)

    gs = pltpu.PrefetchScalarGridSpec(
        num_scalar_prefetch=1, grid=(R // tr,),
        in_specs=[pl.BlockSpec((1, tr, C), lambda i, s: (layer, i, 0))],
        out_specs=pl.BlockSpec((1, tr, C), lambda i, s: (s[0], i, 0)))
    return _pcall(body, name=name, grid_spec=gs, out_shape=SDS((N_CHIPS, R, C), BF16),
                  compiler_params=_params(("arbitrary",)))(_scalar(slot), w3)


class _Stage:
    def __init__(self, ins, out_shapes, aliases, n_sems, copies):
        self.ins, self.out_shapes, self.aliases, self.n_sems, self.copies = list(ins), out_shapes, aliases, n_sems, copies

    def start(self, ins, outs, send_sems, recv_sems):
        for cp in self.copies(ins, outs, send_sems, recv_sems)[0]:
            cp.start()

    def finish(self, ins, outs, send_sems, recv_sems):
        sends, arrivals = self.copies(ins, outs, send_sems, recv_sems)
        for cp in arrivals:
            cp.wait_recv()
        for cp in sends:
            cp.wait_send()


class _SemsFrom:
    def __init__(self, sems, base):
        self.sems, self.base = sems, base

    @property
    def at(self):
        return self

    def __getitem__(self, k):
        return self.sems.at[self.base + k]


def _both(a, b):
    if a is None or b is None:
        return a if b is None else b
    na, nao = len(a.ins), len(a.out_shapes)

    def copies(ins, outs, send_sems, recv_sems):
        sa, aa = a.copies(ins[:na], outs[:nao], send_sems, recv_sems)
        sb, ab = b.copies(ins[na:], outs[nao:], _SemsFrom(send_sems, a.n_sems), _SemsFrom(recv_sems, a.n_sems))
        return sa + sb, aa + ab

    aliases = {**a.aliases, **{na + i: nao + o for i, o in b.aliases.items()}}
    return _Stage(a.ins + b.ins, list(a.out_shapes) + list(b.out_shapes), aliases, a.n_sems + b.n_sems, copies)


def _remote(src, dst, send_sems, recv_sems, k, to):
    return pltpu.make_async_remote_copy(src_ref=src, dst_ref=dst, send_sem=send_sems.at[k], recv_sem=recv_sems.at[k],
                                        device_id=to, device_id_type=MESH)


def _gather_stages(bufs):
    n = len(bufs)
    shapes = [SDS(b.shape, b.dtype) for b in bufs]
    same = {i: i for i in range(n)}

    def over_ici(ins, outs, send_sems, recv_sems):
        x, y, c = _coords()
        blk = lambda w, chip: outs[w].at[chip, _half(c, outs[w].shape[1] // 2), :]
        sends, arrivals = [], []
        for w in range(n):
            for j, (px, py) in enumerate(_other_chips(x, y)):
                sends.append(_remote(blk(w, 2 * x + y), blk(w, 2 * x + y), send_sems, recv_sems, 3 * w + j, (px, py, c)))
                arrivals.append(_remote(blk(w, 2 * px + py), blk(w, 2 * px + py), send_sems, recv_sems, 3 * w + j, (px, py, c)))
        return sends, arrivals

    def to_sibling(ins, outs, send_sems, recv_sems):
        x, y, c = _coords()
        blk = lambda w, chip, half: outs[w].at[chip, _half(half, outs[w].shape[1] // 2), :]
        sends, arrivals = [], []
        for w in range(n):
            for j, (px, py) in enumerate(_other_chips(x, y)):
                k = 2 * px + py
                sends.append(_remote(blk(w, k, c), blk(w, k, c), send_sems, recv_sems, 3 * w + j, (x, y, 1 - c)))
                arrivals.append(_remote(blk(w, k, 1 - c), blk(w, k, 1 - c), send_sems, recv_sems, 3 * w + j, (x, y, 1 - c)))
        return sends, arrivals

    return (lambda b: _Stage(b, shapes, same, 3 * n, over_ici)), (lambda b: _Stage(b, shapes, same, 3 * n, to_sibling))


def _swap_stage(gs_):
    n = len(gs_)

    def copies(ins, outs, send_sems, recv_sems):
        x, y, c = _coords()
        cps = [_remote(ins[w].at[:, _half(1 - c, ins[w].shape[1] // 2), :], outs[w], send_sems, recv_sems, w, (x, y, 1 - c))
               for w in range(n)]
        return cps, cps

    return _Stage(gs_, [SDS((N_CHIPS, g.shape[1] // 2, g.shape[2]), g.dtype) for g in gs_], {}, n, copies)


def _scatter_stage(ps):
    n = len(ps)

    def copies(ins, outs, send_sems, recv_sems):
        x, y, c = _coords()
        cps = [_remote(ins[w].at[2 * px + py], outs[w].at[j], send_sems, recv_sems, 3 * w + j, (px, py, c))
               for w in range(n) for j, (px, py) in enumerate(_other_chips(x, y))]
        return cps, cps

    return _Stage(ps, [SDS((3,) + p.shape[1:], p.dtype) for p in ps], {}, 3 * n, copies)


def _join_stage(bufs, layers):
    n = len(bufs)
    slots = [(w, l) for w in range(n) for l in layers[w]]

    def copies(ins, outs, send_sems, recv_sems):
        x, y, c = _coords()
        blk = lambda w, l, half: outs[w].at[l, _half(half, outs[w].shape[1] // 2), :]
        sends = [_remote(blk(w, l, c), blk(w, l, c), send_sems, recv_sems, k, (x, y, 1 - c)) for k, (w, l) in enumerate(slots)]
        arrivals = [_remote(blk(w, l, 1 - c), blk(w, l, 1 - c), send_sems, recv_sems, k, (x, y, 1 - c))
                    for k, (w, l) in enumerate(slots)]
        return sends, arrivals

    return _Stage(bufs, [SDS(b.shape, b.dtype) for b in bufs], {i: i for i in range(n)}, len(slots), copies)


def _stage_scratch(stage):
    return [pltpu.SemaphoreType.DMA((stage.n_sems,)), pltpu.SemaphoreType.DMA((stage.n_sems,))]


def _run_stage(stage, name):
    n_in, n_out = len(stage.ins), len(stage.out_shapes)

    def body(*refs):
        ins, outs, send_sems, recv_sems = refs[:n_in], refs[n_in:n_in + n_out], refs[n_in + n_out], refs[n_in + n_out + 1]
        stage.start(ins, outs, send_sems, recv_sems)
        stage.finish(ins, outs, send_sems, recv_sems)

    return _pcall(body, name=name, out_shape=stage.out_shapes, in_specs=[HBM_SPEC] * n_in, out_specs=[HBM_SPEC] * n_out,
                  input_output_aliases=stage.aliases, scratch_shapes=_stage_scratch(stage))(*stage.ins)


def _call(body, name, grid, n_prefetch, in_specs, out_specs, out_shape, scratch, operands, semantics, carry=None):
    n_in, n_out, n_sc = len(in_specs), len(out_specs), len(scratch)
    if carry is None:
        gs = pltpu.PrefetchScalarGridSpec(num_scalar_prefetch=n_prefetch, grid=grid, in_specs=in_specs, out_specs=out_specs,
                                          scratch_shapes=scratch)
        res = _pcall(body, name=name, grid_spec=gs, out_shape=out_shape, compiler_params=_params(semantics))(*operands)
        return list(res), []
    s_in, s_out = len(carry.ins), len(carry.out_shapes)

    def carrying(*refs):
        o = n_prefetch
        pre, ins = refs[:o], refs[o:o + n_in]
        o += n_in
        sins = refs[o:o + s_in]
        o += s_in
        outs = refs[o:o + n_out]
        o += n_out
        souts = refs[o:o + s_out]
        o += s_out
        sc, send_sems, recv_sems = refs[o:o + n_sc], refs[o + n_sc], refs[o + n_sc + 1]
        first = functools.reduce(jnp.logical_and, [pl.program_id(a) == 0 for a in range(len(grid))])
        last = functools.reduce(jnp.logical_and, [pl.program_id(a) == g - 1 for a, g in enumerate(grid)])

        @pl.when(first)
        def _():
            carry.start(sins, souts, send_sems, recv_sems)

        body(*pre, *ins, *outs, *sc)

        @pl.when(last)
        def _():
            carry.finish(sins, souts, send_sems, recv_sems)

    gs = pltpu.PrefetchScalarGridSpec(
        num_scalar_prefetch=n_prefetch, grid=grid, in_specs=list(in_specs) + [HBM_SPEC] * s_in,
        out_specs=list(out_specs) + [HBM_SPEC] * s_out, scratch_shapes=list(scratch) + _stage_scratch(carry))
    aliases = {n_prefetch + n_in + a: n_out + b for a, b in carry.aliases.items()}
    res = _pcall(carrying, name=name, grid_spec=gs, out_shape=list(out_shape) + list(carry.out_shapes),
                 input_output_aliases=aliases, compiler_params=_params(("arbitrary",) * len(grid)))(*operands, *carry.ins)
    return list(res[:n_out]), list(res[n_out:])


def _all_reduce_small(v):
    n = v.shape[0]

    def body(v_ref, out_ref, slots, send_sems, recv_sems):
        x, y, c = _coords()
        me = 4 * x + 2 * y + c
        cps = []
        for r in range(1, 8):
            t = (me + r) % 8
            cp = pltpu.make_async_remote_copy(src_ref=v_ref, dst_ref=slots.at[me], send_sem=send_sems.at[r - 1],
                                              recv_sem=recv_sems.at[me], device_id=(t // 4, (t // 2) % 2, t % 2),
                                              device_id_type=MESH)
            cp.start()
            cps.append(cp)
        slots[me] = v_ref[...]
        for r in range(1, 8):
            s = (me + r) % 8
            pltpu.make_async_remote_copy(src_ref=v_ref, dst_ref=slots.at[s], send_sem=send_sems.at[r - 1],
                                         recv_sem=recv_sems.at[s], device_id=(x, y, c), device_id_type=MESH).wait_recv()
        for cp in cps:
            cp.wait_send()
        acc = slots[0]
        for d in range(1, 8):
            acc = acc + slots[d]
        out_ref[...] = acc

    return _pcall(body, name="all_reduce_small", out_shape=SDS((n, LANES), F32),
                  in_specs=[pl.BlockSpec(memory_space=pltpu.VMEM)], out_specs=pl.BlockSpec(memory_space=pltpu.VMEM),
                  scratch_shapes=[pltpu.VMEM((8, n, LANES), F32), pltpu.SemaphoreType.DMA((7,)), pltpu.SemaphoreType.DMA((8,))],
                  compiler_params=pltpu.CompilerParams(vmem_limit_bytes=VMEM_LIMIT))(v)


def _add_half(g, recv, c, name):
    _, R, C = g.shape
    rows = R // 2
    tr = _rows_tile(rows, C, 1 << 20)
    nb = rows // tr

    def body(s_ref, g_ref, r_ref, o32_ref, o16_ref):
        s = g_ref[...] + r_ref[...]
        o32_ref[...] = s
        o16_ref[...] = s.astype(BF16)

    blk = lambda k, i, s: (k, i, 0)
    gs = pltpu.PrefetchScalarGridSpec(
        num_scalar_prefetch=1, grid=(N_CHIPS, nb),
        in_specs=[pl.BlockSpec((1, tr, C), lambda k, i, s: (k, s[0] * nb + i, 0)), pl.BlockSpec((1, tr, C), blk)],
        out_specs=[pl.BlockSpec((1, tr, C), blk), pl.BlockSpec((1, tr, C), blk)])
    return _pcall(body, name=name, grid_spec=gs, out_shape=[SDS((N_CHIPS, rows, C), F32), SDS((N_CHIPS, rows, C), BF16)],
                  compiler_params=_params(("arbitrary", "arbitrary")))(_scalar(c), g, recv)


def _sum_into(p32, arrived, chip, c, layer, n_layers, prev, name):
    _, rows, C = p32.shape
    tr = _rows_tile(rows, C, 1 << 20)
    nb = rows // tr

    def body(chip_ref, c_ref, p_ref, a_ref, *rest):
        o_ref = rest[-1]
        o_ref[0] = ((p_ref[0] + a_ref[0].astype(F32)) + a_ref[1].astype(F32)) + a_ref[2].astype(F32)

    in_specs = [pl.BlockSpec((1, tr, C), lambda i, chip_ref, c_ref: (chip_ref[0], i, 0)),
                pl.BlockSpec((3, tr, C), lambda i, chip_ref, c_ref: (0, i, 0))]
    ins = [p32, arrived]
    aliases = {}
    if prev is not None:
        in_specs.append(pl.BlockSpec(memory_space=pl.ANY))
        ins.append(prev)
        aliases = {4: 0}
    gs = pltpu.PrefetchScalarGridSpec(
        num_scalar_prefetch=2, grid=(nb,), in_specs=in_specs,
        out_specs=pl.BlockSpec((1, tr, C), lambda i, chip_ref, c_ref: (layer, c_ref[0] * nb + i, 0)))
    return _pcall(body, name=name, grid_spec=gs, out_shape=SDS((n_layers, 2 * rows, C), F32), input_output_aliases=aliases,
                  compiler_params=_params(("arbitrary",)))(_scalar(chip), _scalar(c), *ins)


def _h_layout(D):
    G = N_BRANCHES * D
    off, o = {}, 0
    for name, w in [("g", G), ("qa", A_Q), ("cq", MLA_Q_RANK), ("ckv", MLA_KV_RANK), ("hu", SGU_WIDTH), ("hv", SGU_WIDTH),
                    ("ka", A_KV), ("va", A_KV), ("kr", LANES)]:
        assert o % w == 0, (name, o, w)
        off[name] = (o, w)
        o += w
    off["total"] = -(-o // 512) * 512
    return off


def _perm_w_in(w, lay):
    s = np.cumsum([0, A_Q, A_KV, A_KV, MLA_Q_RANK, MLA_KV_RANK, MLA_ROPE, SGU_WIDTH, SGU_WIDTH])
    qa, ka, va, cq, ckv, kr, hu, hv = [w[:, s[i]:s[i + 1]] for i in range(8)]
    g = w[:, s[8]:]
    pad = jnp.zeros((w.shape[0], lay["total"] - lay["kr"][0] - MLA_ROPE), w.dtype)
    return jnp.concatenate([g, qa, cq, ckv, hu, hv, ka, va, kr, pad], axis=1)


def _unperm_w_in(wp, lay, D):
    take = lambda n, width=None: wp[:, lay[n][0]:lay[n][0] + (width or lay[n][1])]
    return jnp.concatenate([take("qa"), take("ka"), take("va"), take("cq"), take("ckv"), take("kr", MLA_ROPE), take("hu"),
                            take("hv"), take("g")], axis=1)


def _perm_w_uq(w):
    r = w.shape[0]
    w3 = w.reshape(r, MLA_HEADS, MLA_NOPE + MLA_ROPE)
    nope = w3[:, :, :MLA_NOPE].reshape(r, MLA_HEADS * MLA_NOPE)
    rope = jnp.pad(w3[:, :, MLA_NOPE:], ((0, 0), (0, 0), (0, LANES - MLA_ROPE))).reshape(r, MLA_HEADS * LANES)
    return jnp.concatenate([nope, rope], axis=1)


def _unperm_w_uq(wp):
    r = wp.shape[0]
    nope = wp[:, :MLA_HEADS * MLA_NOPE].reshape(r, MLA_HEADS, MLA_NOPE)
    rope = wp[:, MLA_HEADS * MLA_NOPE:].reshape(r, MLA_HEADS, LANES)[:, :, :MLA_ROPE]
    return jnp.concatenate([nope, rope], axis=2).reshape(r, MLA_HEADS * (MLA_NOPE + MLA_ROPE))


def _perm_w_ukv(w):
    r = w.shape[0]
    w3 = w.reshape(r, MLA_HEADS, MLA_NOPE + MLA_V)
    return jnp.concatenate([w3[:, :, :MLA_NOPE].reshape(r, -1), w3[:, :, MLA_NOPE:].reshape(r, -1)], axis=1)


def _unperm_w_ukv(wp):
    r = wp.shape[0]
    k = wp[:, :MLA_HEADS * MLA_NOPE].reshape(r, MLA_HEADS, MLA_NOPE)
    v = wp[:, MLA_HEADS * MLA_NOPE:].reshape(r, MLA_HEADS, MLA_V)
    return jnp.concatenate([k, v], axis=2).reshape(r, -1)


def _col_chunks(g):
    r, c4 = g.shape
    return jnp.transpose(g.reshape(r, N_CHIPS, c4 // N_CHIPS), (1, 0, 2))


def kernel(x, positions, w_in, b_gate, sinks, q_norm_g, kv_norm_g, w_uq, w_ukv, sgu_ln_g, sgu_ln_b, sgu_w, sgu_b, w_proj_a, w_proj_b, w_proj_c, w_o, ln1_g, ln1_b, w_up, conv_w, conv_b, w_down, ln2_g, ln2_b, loss_target, m_w_in, m_b_gate, m_sinks, m_q_norm_g, m_kv_norm_g, m_w_uq, m_w_ukv, m_sgu_ln_g, m_sgu_ln_b, m_sgu_w, m_sgu_b, m_w_proj_a, m_w_proj_b, m_w_proj_c, m_w_o, m_ln1_g, m_ln1_b, m_w_up, m_conv_w, m_conv_b, m_w_down, m_ln2_g, m_ln2_b, v_w_in, v_b_gate, v_sinks, v_q_norm_g, v_kv_norm_g, v_w_uq, v_w_ukv, v_sgu_ln_g, v_sgu_ln_b, v_sgu_w, v_sgu_b, v_w_proj_a, v_w_proj_b, v_w_proj_c, v_w_o, v_ln1_g, v_ln1_b, v_w_up, v_conv_w, v_conv_b, v_w_down, v_ln2_g, v_ln2_b):
    a = locals()
    W = {k: a[k] for k in WEIGHTS}
    Mo = {k: a["m_" + k] for k in WEIGHTS}
    Vo = {k: a["v_" + k] for k in WEIGHTS}
    S, D = x.shape[1], x.shape[2]
    FF2 = w_up.shape[2] * N_CHIPS
    FF = FF2 // 2
    L = DEPTH
    lay = _h_layout(D)
    NP = lay["total"]
    cx, cy, cc = _coords()
    chip = 2 * cx + cy
    T = _tile(S, 512)
    TM = _tile(S, 256, 16)
    TMW = _tile(S, 64, 16)

    shards = {k: tuple(W[k].shape) for k in BIG}
    full = {k: [None] * L for k in BIG}
    own = {(l, k): _cast_into_slot(W[k], l, chip, f"cast_{k}_l{l}") for l in range(L) for k in BIG}

    on_way = {"pairs": [], "arrived": []}

    def lay_out(pairs, gathered):
        for (l_, k), g in zip(pairs, gathered):
            _, r, c_ = shards[k]
            full[k][l_] = g.reshape(N_CHIPS * r, c_) if k in ROW_SHARDED else jnp.transpose(g, (1, 0, 2)).reshape(r, N_CHIPS * c_)

    def gather_behind(pairs):
        to_sib = _gather_stages(on_way["arrived"])[1](on_way["arrived"]) if on_way["pairs"] else None
        bufs = [own[p_] for p_ in pairs]
        return _both(to_sib, _gather_stages(bufs)[0](bufs) if pairs else None)

    def gathered_behind(pairs, outs):
        n_done = len(on_way["pairs"])
        lay_out(on_way["pairs"], outs[:n_done])
        on_way["pairs"], on_way["arrived"] = list(pairs), list(outs[n_done:])

    def mm_behind(a_, b_, mode, dt, name, pairs):
        stage = gather_behind(pairs)
        if stage is None:
            return _mm(a_, b_, mode, dt, name)
        out, outs = _mm(a_, b_, mode, dt, name, carry=stage)
        gathered_behind(pairs, outs)
        return out

    MIX = ["w_uq", "w_ukv", "w_proj_a", "w_proj_b", "w_proj_c", "w_o"]
    nxt = lambda l, names: [(l + 1, k) for k in names] if l + 1 < L else []
    first = [(0, "w_in")]
    gathered_behind(first, _run_stage(gather_behind(first), "gather_ici_w_in_l0"))
    gathered_behind([], _run_stage(gather_behind([]), "gather_sibling_w_in_l0"))

    small_sharded_full = {k: tuple(W[k].shape[:-1]) + (W[k].shape[-1] * N_CHIPS,) for k in SMALL_SHARDED}
    placed = []
    for k in ("b_gate", "conv_w"):
        z = jnp.zeros(small_sharded_full[k], F32)
        z = lax.dynamic_update_slice_in_dim(z, W[k], chip * W[k].shape[-1], axis=-1)
        placed.append(jnp.where(cc == 0, z, 0.0).reshape(-1))
    pv = jnp.concatenate(placed)
    n_pv = pv.shape[0]
    pv = jnp.pad(pv, (0, -n_pv % (8 * LANES))).reshape(-1, LANES)
    pv = _all_reduce_small(pv).reshape(-1)
    nb_ = int(np.prod(small_sharded_full["b_gate"]))
    b_gate_full = pv[:nb_].reshape(small_sharded_full["b_gate"])
    conv_w_full = pv[nb_:n_pv].reshape(small_sharded_full["conv_w"])

    inv_freq = ROPE_THETA ** (-jnp.arange(0, MLA_ROPE, 2, dtype=F32) / MLA_ROPE)
    ang = positions[0].astype(F32)[:, None] * inv_freq
    cos, sin = jnp.cos(ang), jnp.sin(ang)
    cos_t = jnp.concatenate([cos, cos, jnp.ones((S, LANES - MLA_ROPE), F32)], axis=1)
    sin_t = jnp.concatenate([sin, sin, jnp.zeros((S, LANES - MLA_ROPE), F32)], axis=1)

    row = lambda v: v.reshape(1, -1)
    cb = lambda name: lay[name][0] // lay[name][1]

    xs = x[0]
    saved = []
    for l in range(L):
        p = dict(
            w_in=_perm_w_in(full["w_in"][l], lay),
            sinks=row(sinks[l]), qg=row(q_norm_g[l]), kvg=row(kv_norm_g[l]), sg=row(sgu_ln_g[l]), sb=row(sgu_ln_b[l]),
            sw=sgu_w[l], sb3=sgu_b[l].reshape(SGU_GROUPS, SGU_CHUNK, 1),
            bg=b_gate_full[l], l1g=row(ln1_g[l]), l1b=row(ln1_b[l]), cw=conv_w_full[l], cbias=row(conv_b[l]),
            l2g=row(ln2_g[l]), l2b=row(ln2_b[l]))
        if l == 0:
            def fn_cast(i, rows, ps):
                return [rows[0]], []
            (xb,), _ = _rowwise(fn_cast, [_whole(xs)], [], [(D, BF16)], [], TM, "cast_x")
        own_mix = [(l, k) for k in MIX] if l == 0 else []
        own_up = [(l, "w_up")] if l == 0 else []
        own_down = [(l, "w_down")] if l == 0 else []
        h = mm_behind(xb, p["w_in"], "nn", BF16, "mm_h", own_mix)
        (y_a,), outs = _swa_fwd(h, cb("qa"), cb("ka"), cb("va"), p["sinks"], carry=gather_behind(own_up))
        gathered_behind(own_up, outs)
        p.update(w_uq=_perm_w_uq(full["w_uq"][l]), w_ukv=_perm_w_ukv(full["w_ukv"][l]), w_pa=full["w_proj_a"][l],
                 w_pb=full["w_proj_b"][l], w_pc=full["w_proj_c"][l], w_o=full["w_o"][l])
        def fn_rms(i, rows, ps):
            return [_rms_norm(rows[0].astype(F32), ps[0]), _rms_norm(rows[1].astype(F32), ps[1])], []
        (cqn, ckvn), _ = _rowwise(fn_rms, [(h, MLA_Q_RANK, cb("cq")), (h, MLA_KV_RANK, cb("ckv"))], [p["qg"], p["kvg"]],
                                  [(MLA_Q_RANK, BF16), (MLA_KV_RANK, BF16)], [], TM, "mla_rms")
        q_full = _mm(cqn, p["w_uq"], "nn", BF16, "mm_q")
        kv = _mm(ckvn, p["w_ukv"], "nn", BF16, "mm_kv")
        qr, kr = _rope_call(q_full, MLA_HEADS * LANES, 1, h, LANES, cb("kr"), cos_t, sin_t, 1.0, TM, "rope_fwd")
        behind_mla = own_down + nxt(l, ["w_in"] + MIX)
        (y_b, lse), outs = _mla_fwd(q_full, qr, kv, kr, T, carry=gather_behind(behind_mla))
        gathered_behind(behind_mla, outs)
        y_c = _sgu_fwd(h, cb("hu"), cb("hv"), p["sg"], p["sb"], p["sw"], p["sb3"])
        pa = _mm(y_a, p["w_pa"], "nn", F32, "mm_pa")
        pb = _mm(y_b, p["w_pb"], "nn", F32, "mm_pb")
        pc = _mm(y_c, p["w_pc"], "nn", F32, "mm_pc")

        def merge_math(pa_, pb_, pc_, g_, b0, b1, b2):
            out = 0.0
            for br, (pp, bb) in enumerate(zip((pa_, pb_, pc_), (b0, b1, b2))):
                gate = jax.nn.sigmoid(g_[:, br * D:(br + 1) * D].astype(F32) + bb)
                out = out + gate * pp
            return out

        def fn_merge(i, rows, ps):
            bgv = ps[0]
            return [merge_math(rows[0], rows[1], rows[2], rows[3], bgv[0:1], bgv[1:2], bgv[2:3])], []
        (merged,), _ = _rowwise(fn_merge, [_whole(pa), _whole(pb), _whole(pc), (h, N_BRANCHES * D, cb("g"))], [p["bg"]],
                                [(D, BF16)], [], TMW, "merge_fwd")
        o = _mm(merged, p["w_o"], "nn", F32, "mm_o")

        def ln_res_math(x_, o_, g_, b_):
            return _layer_norm(DN_ALPHA * x_ + o_, g_, b_)

        def fn_ln(i, rows, ps):
            y = ln_res_math(rows[0], rows[1], ps[0], ps[1])
            return [y, y], []
        (x1, x1b), _ = _rowwise(fn_ln, [_whole(xs), _whole(o)], [p["l1g"], p["l1b"]], [(D, F32), (D, BF16)], [], TM, "ln1_fwd")
        p.update(w_up=full["w_up"][l])
        up = mm_behind(x1b, p["w_up"], "nn", BF16, "mm_up", nxt(l, ["w_up"]))
        p.update(w_down=full["w_down"][l])
        cv_ = _conv_fwd(up, p["cw"], p["cbias"])

        def glu_math(cg, cvv):
            return jax.nn.silu(cg.astype(F32)) * cvv.astype(F32)

        def fn_glu(i, rows, ps):
            return [glu_math(rows[0], rows[1])], []
        (act,), _ = _rowwise(fn_glu, [(cv_, FF, 0), (cv_, FF, 1)], [], [(FF, BF16)], [], TMW, "glu_fwd")
        dn = mm_behind(act, p["w_down"], "nn", F32, "mm_down", nxt(l, ["w_down"]))
        (x2, x2b), _ = _rowwise(fn_ln, [_whole(x1), _whole(dn)], [p["l2g"], p["l2b"]], [(D, F32), (D, BF16)], [], TM, "ln2_fwd")
        saved.append(dict(p=p, x0=xs, x0b=xb, h=h, y_a=y_a, cqn=cqn, ckvn=ckvn, q_full=q_full, kv=kv, qr=qr, kr=kr, y_b=y_b,
                          lse=lse, y_c=y_c, pa=pa, pb=pb, pc=pc, merged=merged, o=o, x1=x1, x1b=x1b, up=up, cv=cv_, act=act,
                          dn=dn))
        xs, xb = x2, x2b

    def fn_loss(i, rows, ps):
        diff = rows[0] - rows[1]
        part = jnp.sum(jnp.mean(jnp.square(diff), axis=-1, keepdims=True), axis=0, keepdims=True)
        return [diff * (1.0 / D)], [jnp.broadcast_to(part, (8, LANES))]
    (dx,), (loss_acc,) = _rowwise(fn_loss, [_whole(xs), _whole(loss_target[0])], [], [(D, F32)], [(8, LANES)], TM, "loss")
    loss = lax.psum(0.5 * loss_acc[0, 0], ("x", "y", "c"))

    gbig = {k: [None] * L for k in BIG}
    gsmall = {k: [None] * L for k in SMALL}
    reduced = {}
    pending = None
    for l in reversed(range(L)):
        s = saved[l]
        p = s["p"]

        def fn_ln_bwd(i, rows, ps):
            _, vjp = jax.vjp(ln_res_math, rows[0], rows[1], ps[0], ps[1])
            dx_, do_, dg_, db_ = vjp(rows[2])
            return [dx_, do_], [dg_, db_]
        (dx1_res, ddn), (g_l2g, g_l2b) = _rowwise(fn_ln_bwd, [_whole(s["x1"]), _whole(s["dn"]), _whole(dx)], [p["l2g"], p["l2b"]],
                                                  [(D, F32), (D, BF16)], [(1, D), (1, D)], TM, "ln2_bwd")
        gsmall["ln2_g"][l], gsmall["ln2_b"][l] = g_l2g, g_l2b
        row_chunks = lambda g: g.reshape(N_CHIPS, g.shape[0] // N_CHIPS, g.shape[1])
        gbig["w_down"][l] = row_chunks(_mm_tn(s["act"], ddn, F32, "mm_dw_down"))
        dact = _mm(ddn, p["w_down"], "nt", BF16, "mm_dact")

        def fn_glu_bwd(i, rows, ps):
            _, vjp = jax.vjp(glu_math, rows[0], rows[1])
            dcg, dcv = vjp(rows[2].astype(F32))
            return [jnp.concatenate([dcg, dcv], axis=1)], []
        (dc,), _ = _rowwise(fn_glu_bwd, [(s["cv"], FF, 0), (s["cv"], FF, 1), _whole(dact)], [], [(FF2, BF16)], [], TMW, "glu_bwd")
        dup, g_cw, g_cb = _conv_bwd(s["up"], dc, p["cw"])
        gsmall["conv_w"][l], gsmall["conv_b"][l] = g_cw, g_cb
        if pending is None:
            gbig["w_up"][l] = _mm_tn(s["x1b"], dup, F32, "mm_dw_up", col_chunks=N_CHIPS)
        else:
            gbig["w_up"][l], from_sibling = _mm_tn(s["x1b"], dup, F32, "mm_dw_up", col_chunks=N_CHIPS,
                                                carry=_swap_stage(pending))
            partial = [_add_half(g, r, cc, f"rs_add_{k}_l{l + 1}") for k, g, r in zip(BIG, pending, from_sibling)]
        dx1 = _mm(dup, p["w_up"], "nt", F32, "mm_dx1", add=dx1_res)
        (dx0_res, do_), (g_l1g, g_l1b) = _rowwise(fn_ln_bwd, [_whole(s["x0"]), _whole(s["o"]), _whole(dx1)], [p["l1g"], p["l1b"]],
                                                  [(D, F32), (D, BF16)], [(1, D), (1, D)], TM, "ln1_bwd")
        gsmall["ln1_g"][l], gsmall["ln1_b"][l] = g_l1g, g_l1b
        gbig["w_o"][l] = row_chunks(_mm_tn(s["merged"], do_, F32, "mm_dw_o"))
        dmerged = _mm(do_, p["w_o"], "nt", F32, "mm_dmerged")

        def fn_merge_bwd(i, rows, ps):
            bgv = ps[0]
            _, vjp = jax.vjp(merge_math, rows[0], rows[1], rows[2], rows[3], bgv[0:1], bgv[1:2], bgv[2:3])
            dpa, dpb, dpc, dg_, db0, db1, db2 = vjp(rows[4])
            return [dpa, dpb, dpc, dg_], [db0, db1, db2]
        (dpa, dpb, dpc, dgl), (db0, db1, db2) = _rowwise(
            fn_merge_bwd, [_whole(s["pa"]), _whole(s["pb"]), _whole(s["pc"]), (s["h"], N_BRANCHES * D, cb("g")), _whole(dmerged)],
            [p["bg"]], [(D, BF16), (D, BF16), (D, BF16), (N_BRANCHES * D, BF16)], [(1, D)] * 3, TMW, "merge_bwd")
        gsmall["b_gate"][l] = jnp.concatenate([db0, db1, db2], axis=0)
        gbig["w_proj_a"][l] = _mm_tn(s["y_a"], dpa, F32, "mm_dw_pa", col_chunks=N_CHIPS)
        gbig["w_proj_b"][l] = row_chunks(_mm_tn(s["y_b"], dpb, F32, "mm_dw_pb"))
        gbig["w_proj_c"][l] = _mm_tn(s["y_c"], dpc, F32, "mm_dw_pc", col_chunks=N_CHIPS)
        dy_a = _mm(dpa, p["w_pa"], "nt", BF16, "mm_dy_a")
        dy_b = _mm(dpb, p["w_pb"], "nt", BF16, "mm_dy_b")
        dy_c = _mm(dpc, p["w_pc"], "nt", BF16, "mm_dy_c")
        dh_c, g_sg, g_sb, g_sw, g_sb3 = _sgu_bwd(s["h"], cb("hu"), cb("hv"), dy_c, p["sg"], p["sb"], p["sw"], p["sb3"])
        gsmall["sgu_ln_g"][l], gsmall["sgu_ln_b"][l], gsmall["sgu_w"][l] = g_sg, g_sb, g_sw
        gsmall["sgu_b"][l] = g_sb3.reshape(SGU_GROUPS, SGU_CHUNK)
        part_x = ["w_up", "w_down"]
        part_y = [k for k in BIG if k not in part_x]
        scatter_of = lambda names: None if pending is None else _scatter_stage([partial[BIG.index(k)][1] for k in names])
        hide_early = l == 0
        early = [gbig[k][l] for k in EARLY]
        n_early = len(EARLY) if hide_early else 0
        (dqa, dka, dva, g_sinks), landed_ = _swa_bwd(s["h"], cb("qa"), cb("ka"), cb("va"), p["sinks"], dy_a,
                                                     carry=_both(_swap_stage(early) if hide_early else None, scatter_of(part_x)))
        early_from_sibling, arrived_x = landed_[:n_early], landed_[n_early:]
        gsmall["sinks"][l] = g_sinks
        if hide_early:
            early_partial = [_add_half(g, r, cc, f"rs_add_{k}_l{l}") for k, g, r in zip(EARLY, early, early_from_sibling)]
        delta = _mla_delta(dy_b, s["y_b"], T)
        (dqn, dqr), arrived_y = _mla_bwd_dq(s["q_full"], s["qr"], s["kv"], s["kr"], dy_b, s["lse"], delta, T,
                                            carry=scatter_of(part_y))
        if pending is not None:
            arrived = {**dict(zip(part_x, arrived_x)), **dict(zip(part_y, arrived_y))}
            for k, (p32, _) in zip(BIG, partial):
                reduced[k] = _sum_into(p32, arrived[k], chip, cc, l + 1, L, reduced.get(k), f"rs_sum_{k}_l{l + 1}")
        (dkn, dv, dkr_heads), early_arrived = _mla_bwd_dkv(
            s["q_full"], s["qr"], s["kv"], s["kr"], dy_b, s["lse"], delta, T,
            carry=_scatter_stage([p16 for _, p16 in early_partial]) if hide_early else None)
        if hide_early:
            for k, (p32, _), arr in zip(EARLY, early_partial, early_arrived):
                reduced[k] = _sum_into(p32, arr, chip, cc, l, L, reduced.get(k), f"rs_sum_{k}_l{l}")
        dkr = _sum_heads(dkr_heads, TM)
        dqr_raw, dkr_raw = _rope_call(dqr, MLA_HEADS * LANES, 0, dkr, LANES, 0, cos_t, sin_t, -1.0, TM, "rope_bwd")
        dq_full = jnp.concatenate([dqn, dqr_raw], axis=1)
        dkv = jnp.concatenate([dkn, dv], axis=1)
        gbig["w_uq"][l] = _col_chunks(_unperm_w_uq(_mm_tn(s["cqn"], dq_full, F32, "mm_dw_uq")))
        gbig["w_ukv"][l] = _col_chunks(_unperm_w_ukv(_mm_tn(s["ckvn"], dkv, F32, "mm_dw_ukv")))
        dcqn = _mm(dq_full, p["w_uq"], "nt", F32, "mm_dcqn")
        dckvn = _mm(dkv, p["w_ukv"], "nt", F32, "mm_dckvn")

        def fn_rms_bwd(i, rows, ps):
            _, vjp1 = jax.vjp(lambda c_, g_: _rms_norm(c_.astype(F32), g_), rows[0], ps[0])
            _, vjp2 = jax.vjp(lambda c_, g_: _rms_norm(c_.astype(F32), g_), rows[1], ps[1])
            d1, dg1 = vjp1(rows[2])
            d2, dg2 = vjp2(rows[3])
            return [d1, d2], [dg1, dg2]
        (dcq, dckv), (g_qg, g_kvg) = _rowwise(
            fn_rms_bwd, [(s["h"], MLA_Q_RANK, cb("cq")), (s["h"], MLA_KV_RANK, cb("ckv")), _whole(dcqn), _whole(dckvn)],
            [p["qg"], p["kvg"]], [(MLA_Q_RANK, BF16), (MLA_KV_RANK, BF16)], [(1, MLA_Q_RANK), (1, MLA_KV_RANK)], TM, "mla_rms_bwd")
        gsmall["q_norm_g"][l], gsmall["kv_norm_g"][l] = g_qg, g_kvg
        tail = jnp.zeros((S, NP - lay["kr"][0] - LANES), BF16)
        dh = jnp.concatenate([dgl, dqa, dcq, dckv, dh_c, dka, dva, dkr_raw, tail], axis=1)
        if l == 0:
            done = [list(range(L)) if k in EARLY else list(range(1, L)) for k in BIG]
            dw_in, joined = _mm_tn(s["x0b"], dh, F32, "mm_dw_in", carry=_join_stage([reduced[k] for k in BIG], done))
            reduced.update(zip(BIG, joined))
        else:
            dw_in = _mm_tn(s["x0b"], dh, F32, "mm_dw_in")
        gbig["w_in"][l] = _col_chunks(_unperm_w_in(dw_in, lay, D))
        if l > 0:
            dx = _mm(dh, p["w_in"], "nt", F32, "mm_dx0", add=dx0_res)

        pending = [gbig[k][l] for k in BIG]

    late = [gbig[k][0] for k in LATE]
    from_sibling = _run_stage(_swap_stage(late), "rs_swap_halves_l0")
    partial = [_add_half(g, r, cc, f"rs_add_{k}_l0") for k, g, r in zip(LATE, late, from_sibling)]
    dx, arrived = _mm(dh, p["w_in"], "nt", F32, "mm_dx0", add=dx0_res, carry=_scatter_stage([p16 for _, p16 in partial]))
    for k, (p32, _), arr in zip(LATE, partial, arrived):
        reduced[k] = _sum_into(p32, arr, chip, cc, 0, L, reduced.get(k), f"rs_sum_{k}_l0")

    grad_x = dx.reshape(x.shape)
    g_big = {**reduced, **dict(zip(LATE, _run_stage(_join_stage([reduced[k] for k in LATE], [[0]] * len(LATE)), "rs_join_halves")))}

    small_shapes = {k: (small_sharded_full[k] if k in SMALL_SHARDED else tuple(W[k].shape)) for k in SMALL}
    sv = jnp.concatenate([jnp.stack(gsmall[k]).reshape(-1) for k in SMALL])
    n_sv = sv.shape[0]
    sv = jnp.pad(sv, (0, -n_sv % (8 * LANES))).reshape(-1, LANES)
    sv = _all_reduce_small(sv).reshape(-1)
    g_small, o_ = {}, 0
    for k in SMALL:
        n = int(np.prod(small_shapes[k]))
        g = sv[o_:o_ + n].reshape(small_shapes[k])
        if k in SMALL_SHARDED:
            g = lax.dynamic_slice_in_dim(g, chip * W[k].shape[-1], W[k].shape[-1], axis=-1)
        g_small[k] = g
        o_ += n

    delta, new_m, new_v = {}, {}, {}
    swap_minor = lambda t: jnp.transpose(t, (0, 2, 1))
    for k in BIG:
        if shards[k][2] % LANES and not shards[k][1] % LANES:
            out = _adamw(swap_minor(W[k]), swap_minor(g_big[k]), swap_minor(Mo[k]), swap_minor(Vo[k]), "adamw_" + k)
            delta[k], new_m[k], new_v[k] = [swap_minor(t) for t in out]
        else:
            delta[k], new_m[k], new_v[k] = _adamw(W[k], g_big[k], Mo[k], Vo[k], "adamw_" + k)
    pack = lambda t: jnp.concatenate([t[k].reshape(-1) for k in SMALL])
    n_small = sum(int(np.prod(W[k].shape)) for k in SMALL)
    pad2 = lambda t: jnp.pad(t, (0, -n_small % (8 * LANES))).reshape(1, -1, LANES)
    d_, m_, v_ = _adamw(pad2(pack(W)), pad2(pack(g_small)), pad2(pack(Mo)), pad2(pack(Vo)), "adamw_small")
    o_ = 0
    for k in SMALL:
        n = int(np.prod(W[k].shape))
        take = lambda t: t.reshape(-1)[o_:o_ + n].reshape(W[k].shape)
        delta[k], new_m[k], new_v[k] = take(d_), take(m_), take(v_)
        o_ += n

    grads = {**g_big, **g_small}
    return (loss, grad_x, *[grads[k] for k in WEIGHTS], *[delta[k] for k in WEIGHTS], *[new_m[k] for k in WEIGHTS],
            *[new_v[k] for k in WEIGHTS])
```

```python
import functools
import math

import jax
import jax.numpy as jnp
import numpy as np
from jax import lax
from jax.experimental import pallas as pl
from jax.experimental.pallas import tpu as pltpu

F32, BF16 = jnp.float32, jnp.bfloat16
SDS = jax.ShapeDtypeStruct
MESH = pl.DeviceIdType.MESH

SWA_Q_HEADS, SWA_KV_HEADS, SWA_HEAD_DIM, SWA_BLOCK = 16, 2, 64, 128
MLA_HEADS, MLA_NOPE, MLA_ROPE, MLA_V = 16, 128, 64, 128
MLA_Q_RANK, MLA_KV_RANK = 512, 512
ROPE_THETA = 10000.0
SGU_GROUPS, SGU_GROUP_DIM, SGU_CHUNK = 8, 128, 128
SGU_WIDTH = SGU_GROUPS * SGU_GROUP_DIM
A_Q = SWA_Q_HEADS * SWA_HEAD_DIM
A_KV = SWA_KV_HEADS * SWA_HEAD_DIM
N_BRANCHES = 3
DEPTH = 2
EPS = 1e-5
MASK_VALUE = -1e30
DN_ALPHA = (2 * DEPTH) ** 0.25
ADAM_LR, ADAM_B1, ADAM_B2, ADAM_EPS, ADAM_WD, ADAM_STEP = 0.001, 0.9, 0.999, 1e-08, 0.01, 10
N_CHIPS = 4

LANES = 128
VMEM_LIMIT = 48 * 1024 * 1024

BIG = ["w_in", "w_uq", "w_ukv", "w_proj_a", "w_proj_b", "w_proj_c", "w_o", "w_up", "w_down"]
ROW_SHARDED = {"w_proj_b", "w_o", "w_down"}
LATE = ["w_in", "w_uq", "w_ukv"]
EARLY = [k for k in BIG if k not in LATE]
SMALL = ["b_gate", "sinks", "q_norm_g", "kv_norm_g", "sgu_ln_g", "sgu_ln_b", "sgu_w", "sgu_b", "ln1_g", "ln1_b",
         "conv_w", "conv_b", "ln2_g", "ln2_b"]
SMALL_SHARDED = {"b_gate", "conv_w"}
WEIGHTS = ["w_in", "b_gate", "sinks", "q_norm_g", "kv_norm_g", "w_uq", "w_ukv", "sgu_ln_g", "sgu_ln_b", "sgu_w", "sgu_b",
           "w_proj_a", "w_proj_b", "w_proj_c", "w_o", "ln1_g", "ln1_b", "w_up", "conv_w", "conv_b", "w_down", "ln2_g", "ln2_b"]


def _pcall(body, **kw):
    return pl.pallas_call(body, **kw)


def _params(sem=None):
    return pltpu.CompilerParams(dimension_semantics=sem, vmem_limit_bytes=VMEM_LIMIT)


def _tile(dim, pref, align=LANES):
    t = (min(pref, dim) // align) * align
    while t >= align:
        if dim % t == 0:
            return t
        t -= align
    return dim


def _mm(a, b, mode, out_dtype, name, add=None, tm=1024, tn=512, tk=2048, col_chunks=1, carry=None):
    if mode == "nn":
        (M, K), (K2, N) = a.shape, b.shape
    elif mode == "nt":
        (M, K), (N, K2) = a.shape, b.shape
    else:
        (K, M), (K2, N) = a.shape, b.shape
    assert K == K2, (a.shape, b.shape, mode)
    assert N % col_chunks == 0
    tm, tn, tk = _tile(M, tm), _tile(N // col_chunks, tn), _tile(K, tk)
    assert (N // col_chunks) % tn == 0
    per_chunk = (N // col_chunks) // tn
    nk = K // tk
    if mode == "tn":
        a_spec = pl.BlockSpec((tk, tm), lambda i, j, k: (k, i))
    else:
        a_spec = pl.BlockSpec((tm, tk), lambda i, j, k: (i, k))
    if mode == "nt":
        b_spec = pl.BlockSpec((tn, tk), lambda i, j, k: (j, k))
    else:
        b_spec = pl.BlockSpec((tk, tn), lambda i, j, k: (k, j))
    dn = {"nn": (((1,), (0,)), ((), ())), "nt": (((1,), (1,)), ((), ())), "tn": (((0,), (0,)), ((), ()))}[mode]
    chunked = col_chunks > 1
    if chunked:
        assert add is None
        o_spec = pl.BlockSpec((1, tm, tn), lambda i, j, k: (lax.div(j, per_chunk), i, lax.rem(j, per_chunk)))
        out_shape = SDS((col_chunks, M, N // col_chunks), out_dtype)
    else:
        o_spec = pl.BlockSpec((tm, tn), lambda i, j, k: (i, j))
        out_shape = SDS((M, N), out_dtype)
    has_add = add is not None

    def body(*refs):
        if has_add:
            a_ref, b_ref, add_ref, o_ref, acc_ref = refs
        else:
            a_ref, b_ref, o_ref, acc_ref = refs
        k = pl.program_id(2)

        @pl.when(k == 0)
        def _():
            acc_ref[...] = jnp.zeros_like(acc_ref)

        acc_ref[...] += lax.dot_general(a_ref[...].astype(BF16), b_ref[...].astype(BF16), dn,
                                        preferred_element_type=F32)

        @pl.when(k == nk - 1)
        def _():
            r = acc_ref[...]
            if has_add:
                r = r + add_ref[...].astype(F32)
            if chunked:
                o_ref[0] = r.astype(o_ref.dtype)
            else:
                o_ref[...] = r.astype(o_ref.dtype)

    ins = [a, b] + ([add] if has_add else [])
    in_specs = [a_spec, b_spec] + ([o_spec] if has_add else [])
    (out,), carried = _call(body, name, (M // tm, N // tn, nk), 0, in_specs, [o_spec], [out_shape], [pltpu.VMEM((tm, tn), F32)],
                            ins, ("parallel", "parallel", "arbitrary"), carry)
    return out if carry is None else (out, carried)


def _mm_tn(a, b, out_dtype, name, tm=512, tn=512, col_chunks=1, carry=None):
    (K, M), (K2, N) = a.shape, b.shape
    assert K == K2 and N % col_chunks == 0
    tm, tn = _tile(M, tm), _tile(N // col_chunks, tn)
    per_chunk = (N // col_chunks) // tn
    chunked = col_chunks > 1
    if chunked:
        o_spec = pl.BlockSpec((1, tm, tn), lambda i, j: (lax.div(j, per_chunk), i, lax.rem(j, per_chunk)))
        out_shape = SDS((col_chunks, M, N // col_chunks), out_dtype)
    else:
        o_spec = pl.BlockSpec((tm, tn), lambda i, j: (i, j))
        out_shape = SDS((M, N), out_dtype)

    def body(a_ref, b_ref, o_ref, at_ref):
        @pl.when(pl.program_id(1) == 0)
        def _():
            at_ref[...] = a_ref[...].astype(BF16).T

        r = jnp.dot(at_ref[...], b_ref[...].astype(BF16), preferred_element_type=F32)
        if chunked:
            o_ref[0] = r.astype(o_ref.dtype)
        else:
            o_ref[...] = r.astype(o_ref.dtype)

    (out,), carried = _call(body, name, (M // tm, N // tn), 0,
                            [pl.BlockSpec((K, tm), lambda i, j: (0, i)), pl.BlockSpec((K, tn), lambda i, j: (0, j))],
                            [o_spec], [out_shape], [pltpu.VMEM((tm, K), BF16)], [a, b], ("parallel", "arbitrary"), carry)
    return out if carry is None else (out, carried)


def _rowwise(fn, rows, params, row_outs, acc_outs, tm, name):
    n_rows = rows[0][0].shape[0]
    assert n_rows % tm == 0
    nr, npar, no = len(rows), len(params), len(row_outs)

    def body(*refs):
        i = pl.program_id(0)
        r, p = refs[:nr], refs[nr:nr + npar]
        o, acc = refs[nr + npar:nr + npar + no], refs[nr + npar + no:]
        outs, sums = fn(i, [x[...] for x in r], [x[...] for x in p])
        for ref, val in zip(o, outs, strict=True):
            ref[...] = val.astype(ref.dtype)
        if acc:
            @pl.when(i == 0)
            def _():
                for ref in acc:
                    ref[...] = jnp.zeros_like(ref)
            for ref, val in zip(acc, sums, strict=True):
                ref[...] += val.astype(F32)

    def full(shape):
        nd = len(shape)
        return pl.BlockSpec(tuple(shape), lambda i: (0,) * nd)

    in_specs = [pl.BlockSpec((tm, w), (lambda i, cb=cb: (i, cb))) for (_, w, cb) in rows] + [full(p.shape) for p in params]
    out_specs = [pl.BlockSpec((tm, w), lambda i: (i, 0)) for (w, _) in row_outs] + [full(s) for s in acc_outs]
    out_shape = [SDS((n_rows, w), dt) for (w, dt) in row_outs] + [SDS(tuple(s), F32) for s in acc_outs]
    res = _pcall(body, name=name, out_shape=out_shape, grid=(n_rows // tm,), in_specs=in_specs, out_specs=out_specs,
                 compiler_params=_params(("arbitrary",)))(*[r[0] for r in rows], *params)
    return list(res[:no]), list(res[no:])


def _whole(a):
    return (a, a.shape[1], 0)


def _gelu(x):
    return 0.5 * x * (1.0 + lax.erf(x * (1.0 / math.sqrt(2.0))))


def _layer_norm(x, g, b):
    mu = x.mean(-1, keepdims=True)
    var = jnp.mean(jnp.square(x - mu), -1, keepdims=True)
    return (x - mu) * lax.rsqrt(var + EPS) * g + b


def _rms_norm(x, g):
    return x * lax.rsqrt(jnp.mean(jnp.square(x), -1, keepdims=True) + EPS) * g


def _sgu_math(hu, hv, ln_g, ln_b, ws, bs):
    u = _gelu(hu.astype(F32))
    vn = _layer_norm(_gelu(hv.astype(F32)), ln_g, ln_b)
    r = lax.broadcasted_iota(jnp.int32, (SGU_CHUNK, SGU_CHUNK), 0)
    c = lax.broadcasted_iota(jnp.int32, (SGU_CHUNK, SGU_CHUNK), 1)
    outs = []
    for g in range(SGU_GROUPS):
        w = jnp.where(r >= c, ws[g], 0.0).astype(BF16)
        vg = vn[:, g * SGU_GROUP_DIM:(g + 1) * SGU_GROUP_DIM].astype(BF16)
        outs.append(jnp.dot(w, vg, preferred_element_type=F32) + bs[g])
    return u * jnp.concatenate(outs, axis=1)


def _sgu_fwd(h, cu, cv, ln_g, ln_b, w, b3):
    def fn(i, rows, ps):
        g_, b_, w_, b3_ = ps
        y = _sgu_math(rows[0], rows[1], g_, b_, [w_[g] for g in range(SGU_GROUPS)], [b3_[g] for g in range(SGU_GROUPS)])
        return [y], []
    (y,), _ = _rowwise(fn, [(h, SGU_WIDTH, cu), (h, SGU_WIDTH, cv)], [ln_g, ln_b, w, b3], [(SGU_WIDTH, BF16)], [],
                       SGU_CHUNK, "sgu_fwd")
    return y


def _sgu_bwd(h, cu, cv, dy, ln_g, ln_b, w, b3):
    nd = 2 * SGU_WIDTH

    def body(hu_ref, hv_ref, dy_ref, g_ref, b_ref, w_ref, b3_ref, dh_ref, dg_ref, db_ref, dw_ref, db3_ref):
        i = pl.program_id(0)

        @pl.when(i == 0)
        def _():
            dg_ref[...] = jnp.zeros_like(dg_ref)
            db_ref[...] = jnp.zeros_like(db_ref)
            dw_ref[...] = jnp.zeros_like(dw_ref)
            db3_ref[...] = jnp.zeros_like(db3_ref)

        ws = [w_ref[g] for g in range(SGU_GROUPS)]
        bs = [b3_ref[g] for g in range(SGU_GROUPS)]
        _, vjp = jax.vjp(_sgu_math, hu_ref[...], hv_ref[...], g_ref[...], b_ref[...], ws, bs)
        dhu, dhv, dg, db, dws, dbs = vjp(dy_ref[...].astype(F32))
        dh_ref[...] = jnp.concatenate([dhu, dhv], axis=1).astype(dh_ref.dtype)
        dg_ref[...] += dg
        db_ref[...] += db
        for g in range(SGU_GROUPS):
            dw_ref[g] += dws[g]
            db3_ref[g] += dbs[g]

    n = h.shape[0]
    blk = lambda cb: pl.BlockSpec((SGU_CHUNK, SGU_WIDTH), lambda i, cb=cb: (i, cb))
    full = lambda s: pl.BlockSpec(tuple(s), lambda i: (0,) * len(s))
    return _pcall(
        body, name="sgu_bwd", grid=(n // SGU_CHUNK,),
        out_shape=[SDS((n, nd), BF16), SDS(ln_g.shape, F32), SDS(ln_b.shape, F32), SDS(w.shape, F32), SDS(b3.shape, F32)],
        in_specs=[blk(cu), blk(cv), blk(0), full(ln_g.shape), full(ln_b.shape), full(w.shape), full(b3.shape)],
        out_specs=[pl.BlockSpec((SGU_CHUNK, nd), lambda i: (i, 0)), full(ln_g.shape), full(ln_b.shape), full(w.shape),
                   full(b3.shape)],
        compiler_params=_params(("arbitrary",)))(h, h, dy, ln_g, ln_b, w, b3)


def _swa_math(q, kp, kc, vp, vc, sinks, not_first):
    kw = jnp.concatenate([kp, kc], axis=0).astype(BF16)
    vw = jnp.concatenate([vp, vc], axis=0).astype(BF16)
    qb = q.astype(BF16)
    q_off = lax.broadcasted_iota(jnp.int32, (SWA_BLOCK, 2 * SWA_BLOCK), 0) + SWA_BLOCK
    k_off = lax.broadcasted_iota(jnp.int32, (SWA_BLOCK, 2 * SWA_BLOCK), 1)
    rel = q_off - k_off
    valid = (rel >= 0) & (rel < SWA_BLOCK) & (not_first | (k_off >= SWA_BLOCK))
    G = SWA_Q_HEADS // SWA_KV_HEADS
    outs = []
    for head in range(SWA_Q_HEADS):
        hk = head // G
        qh = qb[:, head * SWA_HEAD_DIM:(head + 1) * SWA_HEAD_DIM]
        kh = kw[:, hk * SWA_HEAD_DIM:(hk + 1) * SWA_HEAD_DIM]
        vh = vw[:, hk * SWA_HEAD_DIM:(hk + 1) * SWA_HEAD_DIM]
        s = lax.dot_general(qh, kh, (((1,), (1,)), ((), ())), preferred_element_type=F32) * (SWA_HEAD_DIM ** -0.5)
        s = jnp.where(valid, s, MASK_VALUE)
        sink = sinks[:, head:head + 1]
        m = jnp.maximum(s.max(-1, keepdims=True), sink)
        p = jnp.exp(s - m)
        p = (p / (p.sum(-1, keepdims=True) + jnp.exp(sink - m))).astype(BF16)
        outs.append(jnp.dot(p, vh, preferred_element_type=F32))
    return jnp.concatenate(outs, axis=1)


def _swa_fwd(h, cq, ck, cv, sinks, carry=None):
    n = h.shape[0]
    nb = n // SWA_BLOCK

    def body(q_ref, kp_ref, kc_ref, vp_ref, vc_ref, s_ref, o_ref):
        i = pl.program_id(0)
        f = lambda x: x[...].astype(F32)
        o_ref[...] = _swa_math(f(q_ref), f(kp_ref), f(kc_ref), f(vp_ref), f(vc_ref), s_ref[...], i > 0).astype(o_ref.dtype)

    prev = lambda cb: pl.BlockSpec((SWA_BLOCK, A_KV), lambda i, cb=cb: (jnp.maximum(i - 1, 0), cb))
    cur = lambda cb: pl.BlockSpec((SWA_BLOCK, A_KV), lambda i, cb=cb: (i, cb))
    return _call(body, "swa_fwd", (nb,), 0,
                 [pl.BlockSpec((SWA_BLOCK, A_Q), lambda i: (i, cq)), prev(ck), cur(ck), prev(cv), cur(cv),
                  pl.BlockSpec((1, SWA_Q_HEADS), lambda i: (0, 0))],
                 [pl.BlockSpec((SWA_BLOCK, A_Q), lambda i: (i, 0))], [SDS((n, A_Q), BF16)], [],
                 [h, h, h, h, h, sinks], ("arbitrary",), carry)


def _swa_bwd(h, cq, ck, cv, sinks, dy, carry=None):
    n = h.shape[0]
    nb = n // SWA_BLOCK

    def body(q_ref, kp_ref, kc_ref, vp_ref, vc_ref, s_ref, dy_ref, dq_ref, dk_ref, dv_ref, ds_ref, ck_ref, cv_ref):
        r = pl.program_id(0)
        blk = nb - 1 - r

        @pl.when(r == 0)
        def _():
            ds_ref[...] = jnp.zeros_like(ds_ref)
            ck_ref[...] = jnp.zeros_like(ck_ref)
            cv_ref[...] = jnp.zeros_like(cv_ref)

        f = lambda x: x[...].astype(F32)
        not_first = blk > 0
        _, vjp = jax.vjp(lambda q, kp, kc, vp, vc, s: _swa_math(q, kp, kc, vp, vc, s, not_first),
                         f(q_ref), f(kp_ref), f(kc_ref), f(vp_ref), f(vc_ref), s_ref[...])
        dq, dkp, dkc, dvp, dvc, dsk = vjp(f(dy_ref))
        dq_ref[...] = dq.astype(dq_ref.dtype)
        dk_ref[...] = (dkc + ck_ref[...]).astype(dk_ref.dtype)
        dv_ref[...] = (dvc + cv_ref[...]).astype(dv_ref.dtype)
        ck_ref[...] = dkp
        cv_ref[...] = dvp
        ds_ref[...] += dsk

    rev = lambda i: nb - 1 - i
    prev = lambda cb: pl.BlockSpec((SWA_BLOCK, A_KV), lambda i, cb=cb: (jnp.maximum(rev(i) - 1, 0), cb))
    cur = lambda cb: pl.BlockSpec((SWA_BLOCK, A_KV), lambda i, cb=cb: (rev(i), cb))
    return _call(
        body, "swa_bwd", (nb,), 0,
        [pl.BlockSpec((SWA_BLOCK, A_Q), lambda i: (rev(i), cq)), prev(ck), cur(ck), prev(cv), cur(cv),
         pl.BlockSpec((1, SWA_Q_HEADS), lambda i: (0, 0)), pl.BlockSpec((SWA_BLOCK, A_Q), lambda i: (rev(i), 0))],
        [pl.BlockSpec((SWA_BLOCK, A_Q), lambda i: (rev(i), 0)), pl.BlockSpec((SWA_BLOCK, A_KV), lambda i: (rev(i), 0)),
         pl.BlockSpec((SWA_BLOCK, A_KV), lambda i: (rev(i), 0)), pl.BlockSpec((1, SWA_Q_HEADS), lambda i: (0, 0))],
        [SDS((n, A_Q), BF16), SDS((n, A_KV), BF16), SDS((n, A_KV), BF16), SDS((1, SWA_Q_HEADS), F32)],
        [pltpu.VMEM((SWA_BLOCK, A_KV), F32), pltpu.VMEM((SWA_BLOCK, A_KV), F32)],
        [h, h, h, h, h, sinks, dy], ("arbitrary",), carry)


def _rope(x, cos, sin, sign):
    w = x.shape[1]
    reps = w // LANES
    ct = jnp.tile(cos, (1, reps)) if reps > 1 else cos
    st = jnp.tile(sin, (1, reps)) if reps > 1 else sin
    fwd = pltpu.roll(x, MLA_ROPE // 2, axis=1)
    bwd = pltpu.roll(x, w - MLA_ROPE // 2, axis=1)
    lane = lax.broadcasted_iota(jnp.int32, x.shape, 1) % LANES
    rot = jnp.where(lane < MLA_ROPE // 2, -bwd, fwd)
    return x * ct + sign * (rot * st)


def _rope_call(a, wa, ca, b, wb, cb, cos, sin, sign, tm, name):
    def fn(i, rows, ps):
        xa, xb, c_, s_ = rows
        return [_rope(xa.astype(F32), c_, s_, sign), _rope(xb.astype(F32), c_, s_, sign)], []
    (ra, rb), _ = _rowwise(fn, [(a, wa, ca), (b, wb, cb), _whole(cos), _whole(sin)], [], [(wa, BF16), (wb, BF16)], [], tm, name)
    return ra, rb


MLA_SCALE = (MLA_NOPE + MLA_ROPE) ** -0.5
LOG2E = math.log2(math.e)


MLA_HP = 8
MLA_W = MLA_HP * LANES


def _mla_scores(qn_ref, qr_ref, kn_ref, kr_ref, hh, masked):
    cols = slice(hh * LANES, (hh + 1) * LANES)
    q = jnp.concatenate([qn_ref[:, cols], qr_ref[:, cols]], axis=1)
    k = jnp.concatenate([kn_ref[:, cols], kr_ref[...]], axis=1)
    s = lax.dot_general(q, k, (((1,), (1,)), ((), ())), preferred_element_type=F32)
    if masked:
        row = lax.broadcasted_iota(jnp.int32, s.shape, 0)
        col = lax.broadcasted_iota(jnp.int32, s.shape, 1)
        s = jnp.where(col <= row, s, MASK_VALUE)
    return s, q, k


def _causal_pairs(nq, by_query):
    if by_query:
        pairs = [(i, j) for i in range(nq) for j in range(i + 1)]
    else:
        pairs = [(i, j) for j in range(nq) for i in range(j, nq)]
    return (jnp.asarray(np.array([p[0] for p in pairs], np.int32)), jnp.asarray(np.array([p[1] for p in pairs], np.int32)),
            len(pairs))


def _mla_fwd(q_full, qr, kv, kr, T, carry=None):
    n = q_full.shape[0]
    nq = n // T
    H = MLA_HEADS
    qi, kj, npairs = _causal_pairs(nq, True)

    def body(qi_ref, kj_ref, qn_ref, qr_ref, kn_ref, v_ref, kr_ref, y_ref, lse_ref, m_ref, l_ref, acc_ref):
        t = pl.program_id(1)
        i, j = qi_ref[t], kj_ref[t]

        @pl.when(j == 0)
        def _():
            m_ref[...] = jnp.full_like(m_ref, MASK_VALUE)
            l_ref[...] = jnp.zeros_like(l_ref)
            acc_ref[...] = jnp.zeros_like(acc_ref)

        def update(masked):
            for hh in range(MLA_HP):
                s, _, _ = _mla_scores(qn_ref, qr_ref, kn_ref, kr_ref, hh, masked)
                m_prev = m_ref[hh]
                m_new = jnp.maximum(m_prev, s.max(-1, keepdims=True))
                p = jnp.exp2((s - m_new[:, :1]) * (MLA_SCALE * LOG2E))
                alpha = jnp.exp2((m_prev - m_new) * (MLA_SCALE * LOG2E))
                l_ref[hh] = alpha * l_ref[hh] + p.sum(-1, keepdims=True)
                acc_ref[hh] = alpha * acc_ref[hh] + jnp.dot(p.astype(BF16), v_ref[:, hh * LANES:(hh + 1) * LANES],
                                                            preferred_element_type=F32)
                m_ref[hh] = m_new

        @pl.when(j < i)
        def _():
            update(False)

        @pl.when(j == i)
        def _():
            update(True)
            for hh in range(MLA_HP):
                y_ref[:, hh * LANES:(hh + 1) * LANES] = (acc_ref[hh] / l_ref[hh]).astype(y_ref.dtype)
                lse_ref[hh] = m_ref[hh] * (MLA_SCALE * LOG2E) + jnp.log2(l_ref[hh])

    G = H // MLA_HP
    qspec = lambda off: pl.BlockSpec((T, MLA_W), lambda h, t, qi, kj, off=off: (qi[t], off + h))
    kspec = lambda off: pl.BlockSpec((T, MLA_W), lambda h, t, qi, kj, off=off: (kj[t], off + h))
    return _call(
        body, "mla_fwd", (G, npairs), 2,
        [qspec(0), qspec(0), kspec(0), kspec(G), pl.BlockSpec((T, LANES), lambda h, t, qi, kj: (kj[t], 0))],
        [pl.BlockSpec((T, MLA_W), lambda h, t, qi, kj: (qi[t], h)),
         pl.BlockSpec((MLA_HP, T, LANES), lambda h, t, qi, kj: (h, qi[t], 0))],
        [SDS((n, H * MLA_V), BF16), SDS((H, n, LANES), F32)], [pltpu.VMEM((MLA_HP, T, LANES), F32)] * 3,
        [qi, kj, q_full, qr, kv, kv, kr], ("parallel", "arbitrary"), carry)


def _mla_delta(dy, y, T):
    n = y.shape[0]
    H = MLA_HEADS

    def body(dy_ref, y_ref, d_ref):
        d = jnp.sum(dy_ref[...].astype(F32) * y_ref[...].astype(F32), axis=-1, keepdims=True)
        d_ref[0] = jnp.broadcast_to(d, (T, LANES))

    spec = pl.BlockSpec((T, LANES), lambda h, i: (i, h))
    return _pcall(body, name="mla_delta", grid=(H, n // T), out_shape=SDS((H, n, LANES), F32), in_specs=[spec, spec],
                  out_specs=pl.BlockSpec((1, T, LANES), lambda h, i: (h, i, 0)),
                  compiler_params=_params(("parallel", "parallel")))(dy, y)


def _mla_bwd_dq(q_full, qr, kv, kr, dy, lse, delta, T, carry=None):
    n = q_full.shape[0]
    nq = n // T
    H = MLA_HEADS

    qi, kj, npairs = _causal_pairs(nq, True)

    def body(qi_ref, kj_ref, qn_ref, qr_ref, kn_ref, v_ref, kr_ref, dy_ref, lse_ref, dl_ref, dqn_ref, dqr_ref, acc_ref):
        t = pl.program_id(1)
        i, j = qi_ref[t], kj_ref[t]

        @pl.when(j == 0)
        def _():
            acc_ref[...] = jnp.zeros_like(acc_ref)

        def update(masked):
            for hh in range(MLA_HP):
                cols = slice(hh * LANES, (hh + 1) * LANES)
                s, _, k = _mla_scores(qn_ref, qr_ref, kn_ref, kr_ref, hh, masked)
                p = jnp.exp2(s * (MLA_SCALE * LOG2E) - lse_ref[hh][:, :1])
                dp = lax.dot_general(dy_ref[:, cols], v_ref[:, cols], (((1,), (1,)), ((), ())), preferred_element_type=F32)
                ds = p * (dp - dl_ref[hh][:, :1])
                acc_ref[hh] += jnp.dot(ds.astype(BF16), k, preferred_element_type=F32)

        @pl.when(j < i)
        def _():
            update(False)

        @pl.when(j == i)
        def _():
            update(True)
            for hh in range(MLA_HP):
                cols = slice(hh * LANES, (hh + 1) * LANES)
                dqn_ref[:, cols] = (acc_ref[hh][:, :LANES] * MLA_SCALE).astype(dqn_ref.dtype)
                dqr_ref[:, cols] = (acc_ref[hh][:, LANES:] * MLA_SCALE).astype(dqr_ref.dtype)

    G = H // MLA_HP
    qspec = lambda off: pl.BlockSpec((T, MLA_W), lambda h, t, qi, kj, off=off: (qi[t], off + h))
    kspec = lambda off: pl.BlockSpec((T, MLA_W), lambda h, t, qi, kj, off=off: (kj[t], off + h))
    stat = pl.BlockSpec((MLA_HP, T, LANES), lambda h, t, qi, kj: (h, qi[t], 0))
    out = pl.BlockSpec((T, MLA_W), lambda h, t, qi, kj: (qi[t], h))
    return _call(
        body, "mla_bwd_dq", (G, npairs), 2,
        [qspec(0), qspec(0), kspec(0), kspec(G), pl.BlockSpec((T, LANES), lambda h, t, qi, kj: (kj[t], 0)), qspec(0), stat, stat],
        [out, out], [SDS((n, H * LANES), BF16), SDS((n, H * LANES), BF16)], [pltpu.VMEM((MLA_HP, T, 2 * LANES), F32)],
        [qi, kj, q_full, qr, kv, kv, kr, dy, lse, delta], ("parallel", "arbitrary"), carry)


def _mla_bwd_dkv(q_full, qr, kv, kr, dy, lse, delta, T, carry=None):
    n = q_full.shape[0]
    nq = n // T
    H = MLA_HEADS

    qi, kj, npairs = _causal_pairs(nq, False)

    def body(qi_ref, kj_ref, qn_ref, qr_ref, kn_ref, v_ref, kr_ref, dy_ref, lse_ref, dl_ref, dkn_ref, dv_ref, dkr_ref,
             dk_acc, dv_acc):
        t = pl.program_id(1)
        i, j = qi_ref[t], kj_ref[t]

        @pl.when(i == j)
        def _():
            dk_acc[...] = jnp.zeros_like(dk_acc)
            dv_acc[...] = jnp.zeros_like(dv_acc)

        def update(masked):
            for hh in range(MLA_HP):
                cols = slice(hh * LANES, (hh + 1) * LANES)
                s, q, _ = _mla_scores(qn_ref, qr_ref, kn_ref, kr_ref, hh, masked)
                p = jnp.exp2(s * (MLA_SCALE * LOG2E) - lse_ref[hh][:, :1])
                dy = dy_ref[:, cols]
                dv_acc[hh] += lax.dot_general(p.astype(BF16), dy, (((0,), (0,)), ((), ())), preferred_element_type=F32)
                dp = lax.dot_general(dy, v_ref[:, cols], (((1,), (1,)), ((), ())), preferred_element_type=F32)
                ds = p * (dp - dl_ref[hh][:, :1])
                dk_acc[hh] += lax.dot_general(ds.astype(BF16), q, (((0,), (0,)), ((), ())), preferred_element_type=F32)

        @pl.when(i == j)
        def _():
            update(True)

        @pl.when(i > j)
        def _():
            update(False)

        @pl.when(i == nq - 1)
        def _():
            for hh in range(MLA_HP):
                cols = slice(hh * LANES, (hh + 1) * LANES)
                dkn_ref[:, cols] = (dk_acc[hh][:, :LANES] * MLA_SCALE).astype(dkn_ref.dtype)
                dv_ref[:, cols] = dv_acc[hh].astype(dv_ref.dtype)
                dkr_ref[hh] = dk_acc[hh][:, LANES:] * MLA_SCALE

    G = H // MLA_HP
    qspec = lambda off: pl.BlockSpec((T, MLA_W), lambda h, t, qi, kj, off=off: (qi[t], off + h))
    kspec = lambda off: pl.BlockSpec((T, MLA_W), lambda h, t, qi, kj, off=off: (kj[t], off + h))
    stat = pl.BlockSpec((MLA_HP, T, LANES), lambda h, t, qi, kj: (h, qi[t], 0))
    out = pl.BlockSpec((T, MLA_W), lambda h, t, qi, kj: (kj[t], h))
    return _call(
        body, "mla_bwd_dkv", (G, npairs), 2,
        [qspec(0), qspec(0), kspec(0), kspec(G), pl.BlockSpec((T, LANES), lambda h, t, qi, kj: (kj[t], 0)), qspec(0), stat, stat],
        [out, out, pl.BlockSpec((MLA_HP, T, LANES), lambda h, t, qi, kj: (h, kj[t], 0))],
        [SDS((n, H * LANES), BF16), SDS((n, H * LANES), BF16), SDS((H, n, LANES), F32)],
        [pltpu.VMEM((MLA_HP, T, 2 * LANES), F32), pltpu.VMEM((MLA_HP, T, LANES), F32)],
        [qi, kj, q_full, qr, kv, kv, kr, dy, lse, delta], ("parallel", "arbitrary"), carry)


def _sum_heads(a, tm):
    H, n, _ = a.shape

    def body(a_ref, o_ref):
        o_ref[...] = jnp.sum(a_ref[...], axis=0)

    return _pcall(body, name="mla_sum_heads", grid=(n // tm,), out_shape=SDS((n, LANES), F32),
                  in_specs=[pl.BlockSpec((H, tm, LANES), lambda i: (0, i, 0))],
                  out_specs=pl.BlockSpec((tm, LANES), lambda i: (i, 0)), compiler_params=_params(("parallel",)))(a)


def _shift_down(x, k):
    row = lax.broadcasted_iota(jnp.int32, x.shape, 0)
    return jnp.where(row >= k, pltpu.roll(x, k, axis=0), 0.0)


def _shift_up(x, k):
    n = x.shape[0]
    row = lax.broadcasted_iota(jnp.int32, x.shape, 0)
    return jnp.where(row < n - k, pltpu.roll(x, n - k, axis=0), 0.0)


def _conv_fwd(up, w, b):
    n, c = up.shape

    def body(u_ref, w_ref, b_ref, o_ref):
        u = u_ref[...].astype(F32)
        wv = w_ref[...]
        o_ref[...] = (b_ref[...] + wv[0:1] * _shift_down(u, 2) + wv[1:2] * _shift_down(u, 1) + wv[2:3] * u).astype(o_ref.dtype)

    return _pcall(body, name="conv_fwd", grid=(c // LANES,), out_shape=SDS((n, c), BF16),
                  in_specs=[pl.BlockSpec((n, LANES), lambda j: (0, j)), pl.BlockSpec((3, LANES), lambda j: (0, j)),
                            pl.BlockSpec((1, LANES), lambda j: (0, j))],
                  out_specs=pl.BlockSpec((n, LANES), lambda j: (0, j)), compiler_params=_params(("parallel",)))(up, w, b)


def _conv_bwd(up, dc, w):
    n, c = up.shape

    def body(u_ref, d_ref, w_ref, du_ref, dw_ref, db_ref):
        u = u_ref[...].astype(F32)
        d = d_ref[...].astype(F32)
        wv = w_ref[...]
        du_ref[...] = (wv[2:3] * d + wv[1:2] * _shift_up(d, 1) + wv[0:1] * _shift_up(d, 2)).astype(du_ref.dtype)
        dw_ref[0:1, :] = jnp.sum(d * _shift_down(u, 2), axis=0, keepdims=True)
        dw_ref[1:2, :] = jnp.sum(d * _shift_down(u, 1), axis=0, keepdims=True)
        dw_ref[2:3, :] = jnp.sum(d * u, axis=0, keepdims=True)
        db_ref[...] = jnp.sum(d, axis=0, keepdims=True)

    col = pl.BlockSpec((n, LANES), lambda j: (0, j))
    return _pcall(body, name="conv_bwd", grid=(c // LANES,),
                  out_shape=[SDS((n, c), BF16), SDS((3, c), F32), SDS((1, c), F32)],
                  in_specs=[col, col, pl.BlockSpec((3, LANES), lambda j: (0, j))],
                  out_specs=[col, pl.BlockSpec((3, LANES), lambda j: (0, j)), pl.BlockSpec((1, LANES), lambda j: (0, j))],
                  compiler_params=_params(("parallel",)))(up, dc, w)


def _adamw_math(w, g, m, v):
    m = ADAM_B1 * m + (1.0 - ADAM_B1) * g
    v = ADAM_B2 * v + (1.0 - ADAM_B2) * jnp.square(g)
    m_hat = m / (1.0 - ADAM_B1 ** ADAM_STEP)
    v_hat = v / (1.0 - ADAM_B2 ** ADAM_STEP)
    delta = -ADAM_LR * (m_hat / (jnp.sqrt(v_hat) + ADAM_EPS) + ADAM_WD * w)
    return delta, m, v


def _adamw(w, g, m, v, name):
    L, r, c = w.shape
    tr = _rows_tile(r, c, 1 << 20, 8)

    def body(w_ref, g_ref, m_ref, v_ref, d_ref, nm_ref, nv_ref):
        d, nm, nv = _adamw_math(w_ref[...], g_ref[...], m_ref[...], v_ref[...])
        d_ref[...] = d
        nm_ref[...] = nm
        nv_ref[...] = nv

    spec = pl.BlockSpec((1, tr, c), lambda l, i: (l, i, 0))
    return _pcall(body, name=name, grid=(L, r // tr), out_shape=[SDS(w.shape, F32)] * 3, in_specs=[spec] * 4,
                  out_specs=[spec] * 3, compiler_params=_params(("parallel", "parallel")))(w, g, m, v)


def _coords():
    return lax.axis_index("x"), lax.axis_index("y"), lax.axis_index("c")


def _other_chips(x, y):
    return [(1 - x, y), (x, 1 - y), (1 - x, 1 - y)]


HBM_SPEC = pl.BlockSpec(memory_space=pltpu.HBM)


def _half(c, rows):
    return pl.ds(pl.multiple_of(c * rows, 16), rows)


def _rows_tile(rows, cols, budget_bytes=2 << 20, align=16):
    t = (min(max(align, budget_bytes // (4 * cols)), rows) // align) * align
    while t >= align:
        if rows % t == 0:
            return t
        t -= align
    return rows


def _scalar(v):
    return jnp.reshape(jnp.asarray(v, jnp.int32), (1,))


def _cast_into_slot(w3, layer, slot, name):
    _, R, C = w3.shape
    tr = _rows_tile(R, C)

    def body(s_ref, w_ref, o_ref):
        o_ref[0] = w_ref[0].astype(BF16)

    gs = pltpu.PrefetchScalarGridSpec(
        num_scalar_prefetch=1, grid=(R // tr,),
        in_specs=[pl.BlockSpec((1, tr, C), lambda i, s: (layer, i, 0))],
        out_specs=pl.BlockSpec((1, tr, C), lambda i, s: (s[0], i, 0)))
    return _pcall(body, name=name, grid_spec=gs, out_shape=SDS((N_CHIPS, R, C), BF16),
                  compiler_params=_params(("arbitrary",)))(_scalar(slot), w3)


class _Stage:
    def __init__(self, ins, out_shapes, aliases, n_sems, copies):
        self.ins, self.out_shapes, self.aliases, self.n_sems, self.copies = list(ins), out_shapes, aliases, n_sems, copies

    def start(self, ins, outs, send_sems, recv_sems):
        for cp in self.copies(ins, outs, send_sems, recv_sems)[0]:
            cp.start()

    def finish(self, ins, outs, send_sems, recv_sems):
        sends, arrivals = self.copies(ins, outs, send_sems, recv_sems)
        for cp in arrivals:
            cp.wait_recv()
        for cp in sends:
            cp.wait_send()


class _SemsFrom:
    def __init__(self, sems, base):
        self.sems, self.base = sems, base

    @property
    def at(self):
        return self

    def __getitem__(self, k):
        return self.sems.at[self.base + k]


def _both(a, b):
    if a is None or b is None:
        return a if b is None else b
    na, nao = len(a.ins), len(a.out_shapes)

    def copies(ins, outs, send_sems, recv_sems):
        sa, aa = a.copies(ins[:na], outs[:nao], send_sems, recv_sems)
        sb, ab = b.copies(ins[na:], outs[nao:], _SemsFrom(send_sems, a.n_sems), _SemsFrom(recv_sems, a.n_sems))
        return sa + sb, aa + ab

    aliases = {**a.aliases, **{na + i: nao + o for i, o in b.aliases.items()}}
    return _Stage(a.ins + b.ins, list(a.out_shapes) + list(b.out_shapes), aliases, a.n_sems + b.n_sems, copies)


def _remote(src, dst, send_sems, recv_sems, k, to):
    return pltpu.make_async_remote_copy(src_ref=src, dst_ref=dst, send_sem=send_sems.at[k], recv_sem=recv_sems.at[k],
                                        device_id=to, device_id_type=MESH)


def _gather_stages(bufs):
    n = len(bufs)
    shapes = [SDS(b.shape, b.dtype) for b in bufs]
    same = {i: i for i in range(n)}

    def over_ici(ins, outs, send_sems, recv_sems):
        x, y, c = _coords()
        blk = lambda w, chip: outs[w].at[chip, _half(c, outs[w].shape[1] // 2), :]
        sends, arrivals = [], []
        for w in range(n):
            for j, (px, py) in enumerate(_other_chips(x, y)):
                sends.append(_remote(blk(w, 2 * x + y), blk(w, 2 * x + y), send_sems, recv_sems, 3 * w + j, (px, py, c)))
                arrivals.append(_remote(blk(w, 2 * px + py), blk(w, 2 * px + py), send_sems, recv_sems, 3 * w + j, (px, py, c)))
        return sends, arrivals

    def to_sibling(ins, outs, send_sems, recv_sems):
        x, y, c = _coords()
        blk = lambda w, chip, half: outs[w].at[chip, _half(half, outs[w].shape[1] // 2), :]
        sends, arrivals = [], []
        for w in range(n):
            for j, (px, py) in enumerate(_other_chips(x, y)):
                k = 2 * px + py
                sends.append(_remote(blk(w, k, c), blk(w, k, c), send_sems, recv_sems, 3 * w + j, (x, y, 1 - c)))
                arrivals.append(_remote(blk(w, k, 1 - c), blk(w, k, 1 - c), send_sems, recv_sems, 3 * w + j, (x, y, 1 - c)))
        return sends, arrivals

    return (lambda b: _Stage(b, shapes, same, 3 * n, over_ici)), (lambda b: _Stage(b, shapes, same, 3 * n, to_sibling))


def _swap_stage(gs_):
    n = len(gs_)

    def copies(ins, outs, send_sems, recv_sems):
        x, y, c = _coords()
        cps = [_remote(ins[w].at[:, _half(1 - c, ins[w].shape[1] // 2), :], outs[w], send_sems, recv_sems, w, (x, y, 1 - c))
               for w in range(n)]
        return cps, cps

    return _Stage(gs_, [SDS((N_CHIPS, g.shape[1] // 2, g.shape[2]), g.dtype) for g in gs_], {}, n, copies)


def _scatter_stage(ps):
    n = len(ps)

    def copies(ins, outs, send_sems, recv_sems):
        x, y, c = _coords()
        cps = [_remote(ins[w].at[2 * px + py], outs[w].at[j], send_sems, recv_sems, 3 * w + j, (px, py, c))
               for w in range(n) for j, (px, py) in enumerate(_other_chips(x, y))]
        return cps, cps

    return _Stage(ps, [SDS((3,) + p.shape[1:], p.dtype) for p in ps], {}, 3 * n, copies)


def _join_stage(bufs, layers):
    n = len(bufs)
    slots = [(w, l) for w in range(n) for l in layers[w]]

    def copies(ins, outs, send_sems, recv_sems):
        x, y, c = _coords()
        blk = lambda w, l, half: outs[w].at[l, _half(half, outs[w].shape[1] // 2), :]
        sends = [_remote(blk(w, l, c), blk(w, l, c), send_sems, recv_sems, k, (x, y, 1 - c)) for k, (w, l) in enumerate(slots)]
        arrivals = [_remote(blk(w, l, 1 - c), blk(w, l, 1 - c), send_sems, recv_sems, k, (x, y, 1 - c))
                    for k, (w, l) in enumerate(slots)]
        return sends, arrivals

    return _Stage(bufs, [SDS(b.shape, b.dtype) for b in bufs], {i: i for i in range(n)}, len(slots), copies)


def _stage_scratch(stage):
    return [pltpu.SemaphoreType.DMA((stage.n_sems,)), pltpu.SemaphoreType.DMA((stage.n_sems,))]


def _run_stage(stage, name):
    n_in, n_out = len(stage.ins), len(stage.out_shapes)

    def body(*refs):
        ins, outs, send_sems, recv_sems = refs[:n_in], refs[n_in:n_in + n_out], refs[n_in + n_out], refs[n_in + n_out + 1]
        stage.start(ins, outs, send_sems, recv_sems)
        stage.finish(ins, outs, send_sems, recv_sems)

    return _pcall(body, name=name, out_shape=stage.out_shapes, in_specs=[HBM_SPEC] * n_in, out_specs=[HBM_SPEC] * n_out,
                  input_output_aliases=stage.aliases, scratch_shapes=_stage_scratch(stage))(*stage.ins)


def _call(body, name, grid, n_prefetch, in_specs, out_specs, out_shape, scratch, operands, semantics, carry=None):
    n_in, n_out, n_sc = len(in_specs), len(out_specs), len(scratch)
    if carry is None:
        gs = pltpu.PrefetchScalarGridSpec(num_scalar_prefetch=n_prefetch, grid=grid, in_specs=in_specs, out_specs=out_specs,
                                          scratch_shapes=scratch)
        res = _pcall(body, name=name, grid_spec=gs, out_shape=out_shape, compiler_params=_params(semantics))(*operands)
        return list(res), []
    s_in, s_out = len(carry.ins), len(carry.out_shapes)

    def carrying(*refs):
        o = n_prefetch
        pre, ins = refs[:o], refs[o:o + n_in]
        o += n_in
        sins = refs[o:o + s_in]
        o += s_in
        outs = refs[o:o + n_out]
        o += n_out
        souts = refs[o:o + s_out]
        o += s_out
        sc, send_sems, recv_sems = refs[o:o + n_sc], refs[o + n_sc], refs[o + n_sc + 1]
        first = functools.reduce(jnp.logical_and, [pl.program_id(a) == 0 for a in range(len(grid))])
        last = functools.reduce(jnp.logical_and, [pl.program_id(a) == g - 1 for a, g in enumerate(grid)])

        @pl.when(first)
        def _():
            carry.start(sins, souts, send_sems, recv_sems)

        body(*pre, *ins, *outs, *sc)

        @pl.when(last)
        def _():
            carry.finish(sins, souts, send_sems, recv_sems)

    gs = pltpu.PrefetchScalarGridSpec(
        num_scalar_prefetch=n_prefetch, grid=grid, in_specs=list(in_specs) + [HBM_SPEC] * s_in,
        out_specs=list(out_specs) + [HBM_SPEC] * s_out, scratch_shapes=list(scratch) + _stage_scratch(carry))
    aliases = {n_prefetch + n_in + a: n_out + b for a, b in carry.aliases.items()}
    res = _pcall(carrying, name=name, grid_spec=gs, out_shape=list(out_shape) + list(carry.out_shapes),
                 input_output_aliases=aliases, compiler_params=_params(("arbitrary",) * len(grid)))(*operands, *carry.ins)
    return list(res[:n_out]), list(res[n_out:])


def _all_reduce_small(v):
    n = v.shape[0]

    def body(v_ref, out_ref, slots, send_sems, recv_sems):
        x, y, c = _coords()
        me = 4 * x + 2 * y + c
        cps = []
        for r in range(1, 8):
            t = (me + r) % 8
            cp = pltpu.make_async_remote_copy(src_ref=v_ref, dst_ref=slots.at[me], send_sem=send_sems.at[r - 1],
                                              recv_sem=recv_sems.at[me], device_id=(t // 4, (t // 2) % 2, t % 2),
                                              device_id_type=MESH)
            cp.start()
            cps.append(cp)
        slots[me] = v_ref[...]
        for r in range(1, 8):
            s = (me + r) % 8
            pltpu.make_async_remote_copy(src_ref=v_ref, dst_ref=slots.at[s], send_sem=send_sems.at[r - 1],
                                         recv_sem=recv_sems.at[s], device_id=(x, y, c), device_id_type=MESH).wait_recv()
        for cp in cps:
            cp.wait_send()
        acc = slots[0]
        for d in range(1, 8):
            acc = acc + slots[d]
        out_ref[...] = acc

    return _pcall(body, name="all_reduce_small", out_shape=SDS((n, LANES), F32),
                  in_specs=[pl.BlockSpec(memory_space=pltpu.VMEM)], out_specs=pl.BlockSpec(memory_space=pltpu.VMEM),
                  scratch_shapes=[pltpu.VMEM((8, n, LANES), F32), pltpu.SemaphoreType.DMA((7,)), pltpu.SemaphoreType.DMA((8,))],
                  compiler_params=pltpu.CompilerParams(vmem_limit_bytes=VMEM_LIMIT))(v)


def _add_half(g, recv, c, name):
    _, R, C = g.shape
    rows = R // 2
    tr = _rows_tile(rows, C, 1 << 20)
    nb = rows // tr

    def body(s_ref, g_ref, r_ref, o32_ref, o16_ref):
        s = g_ref[...] + r_ref[...]
        o32_ref[...] = s
        o16_ref[...] = s.astype(BF16)

    blk = lambda k, i, s: (k, i, 0)
    gs = pltpu.PrefetchScalarGridSpec(
        num_scalar_prefetch=1, grid=(N_CHIPS, nb),
        in_specs=[pl.BlockSpec((1, tr, C), lambda k, i, s: (k, s[0] * nb + i, 0)), pl.BlockSpec((1, tr, C), blk)],
        out_specs=[pl.BlockSpec((1, tr, C), blk), pl.BlockSpec((1, tr, C), blk)])
    return _pcall(body, name=name, grid_spec=gs, out_shape=[SDS((N_CHIPS, rows, C), F32), SDS((N_CHIPS, rows, C), BF16)],
                  compiler_params=_params(("arbitrary", "arbitrary")))(_scalar(c), g, recv)


def _sum_into(p32, arrived, chip, c, layer, n_layers, prev, name):
    _, rows, C = p32.shape
    tr = _rows_tile(rows, C, 1 << 20)
    nb = rows // tr

    def body(chip_ref, c_ref, p_ref, a_ref, *rest):
        o_ref = rest[-1]
        o_ref[0] = ((p_ref[0] + a_ref[0].astype(F32)) + a_ref[1].astype(F32)) + a_ref[2].astype(F32)

    in_specs = [pl.BlockSpec((1, tr, C), lambda i, chip_ref, c_ref: (chip_ref[0], i, 0)),
                pl.BlockSpec((3, tr, C), lambda i, chip_ref, c_ref: (0, i, 0))]
    ins = [p32, arrived]
    aliases = {}
    if prev is not None:
        in_specs.append(pl.BlockSpec(memory_space=pl.ANY))
        ins.append(prev)
        aliases = {4: 0}
    gs = pltpu.PrefetchScalarGridSpec(
        num_scalar_prefetch=2, grid=(nb,), in_specs=in_specs,
        out_specs=pl.BlockSpec((1, tr, C), lambda i, chip_ref, c_ref: (layer, c_ref[0] * nb + i, 0)))
    return _pcall(body, name=name, grid_spec=gs, out_shape=SDS((n_layers, 2 * rows, C), F32), input_output_aliases=aliases,
                  compiler_params=_params(("arbitrary",)))(_scalar(chip), _scalar(c), *ins)


def _h_layout(D):
    G = N_BRANCHES * D
    off, o = {}, 0
    for name, w in [("g", G), ("qa", A_Q), ("cq", MLA_Q_RANK), ("ckv", MLA_KV_RANK), ("hu", SGU_WIDTH), ("hv", SGU_WIDTH),
                    ("ka", A_KV), ("va", A_KV), ("kr", LANES)]:
        assert o % w == 0, (name, o, w)
        off[name] = (o, w)
        o += w
    off["total"] = -(-o // 512) * 512
    return off


def _perm_w_in(w, lay):
    s = np.cumsum([0, A_Q, A_KV, A_KV, MLA_Q_RANK, MLA_KV_RANK, MLA_ROPE, SGU_WIDTH, SGU_WIDTH])
    qa, ka, va, cq, ckv, kr, hu, hv = [w[:, s[i]:s[i + 1]] for i in range(8)]
    g = w[:, s[8]:]
    pad = jnp.zeros((w.shape[0], lay["total"] - lay["kr"][0] - MLA_ROPE), w.dtype)
    return jnp.concatenate([g, qa, cq, ckv, hu, hv, ka, va, kr, pad], axis=1)


def _unperm_w_in(wp, lay, D):
    take = lambda n, width=None: wp[:, lay[n][0]:lay[n][0] + (width or lay[n][1])]
    return jnp.concatenate([take("qa"), take("ka"), take("va"), take("cq"), take("ckv"), take("kr", MLA_ROPE), take("hu"),
                            take("hv"), take("g")], axis=1)


def _perm_w_uq(w):
    r = w.shape[0]
    w3 = w.reshape(r, MLA_HEADS, MLA_NOPE + MLA_ROPE)
    nope = w3[:, :, :MLA_NOPE].reshape(r, MLA_HEADS * MLA_NOPE)
    rope = jnp.pad(w3[:, :, MLA_NOPE:], ((0, 0), (0, 0), (0, LANES - MLA_ROPE))).reshape(r, MLA_HEADS * LANES)
    return jnp.concatenate([nope, rope], axis=1)


def _unperm_w_uq(wp):
    r = wp.shape[0]
    nope = wp[:, :MLA_HEADS * MLA_NOPE].reshape(r, MLA_HEADS, MLA_NOPE)
    rope = wp[:, MLA_HEADS * MLA_NOPE:].reshape(r, MLA_HEADS, LANES)[:, :, :MLA_ROPE]
    return jnp.concatenate([nope, rope], axis=2).reshape(r, MLA_HEADS * (MLA_NOPE + MLA_ROPE))


def _perm_w_ukv(w):
    r = w.shape[0]
    w3 = w.reshape(r, MLA_HEADS, MLA_NOPE + MLA_V)
    return jnp.concatenate([w3[:, :, :MLA_NOPE].reshape(r, -1), w3[:, :, MLA_NOPE:].reshape(r, -1)], axis=1)


def _unperm_w_ukv(wp):
    r = wp.shape[0]
    k = wp[:, :MLA_HEADS * MLA_NOPE].reshape(r, MLA_HEADS, MLA_NOPE)
    v = wp[:, MLA_HEADS * MLA_NOPE:].reshape(r, MLA_HEADS, MLA_V)
    return jnp.concatenate([k, v], axis=2).reshape(r, -1)


def _col_chunks(g):
    r, c4 = g.shape
    return jnp.transpose(g.reshape(r, N_CHIPS, c4 // N_CHIPS), (1, 0, 2))


def kernel(x, positions, w_in, b_gate, sinks, q_norm_g, kv_norm_g, w_uq, w_ukv, sgu_ln_g, sgu_ln_b, sgu_w, sgu_b, w_proj_a, w_proj_b, w_proj_c, w_o, ln1_g, ln1_b, w_up, conv_w, conv_b, w_down, ln2_g, ln2_b, loss_target, m_w_in, m_b_gate, m_sinks, m_q_norm_g, m_kv_norm_g, m_w_uq, m_w_ukv, m_sgu_ln_g, m_sgu_ln_b, m_sgu_w, m_sgu_b, m_w_proj_a, m_w_proj_b, m_w_proj_c, m_w_o, m_ln1_g, m_ln1_b, m_w_up, m_conv_w, m_conv_b, m_w_down, m_ln2_g, m_ln2_b, v_w_in, v_b_gate, v_sinks, v_q_norm_g, v_kv_norm_g, v_w_uq, v_w_ukv, v_sgu_ln_g, v_sgu_ln_b, v_sgu_w, v_sgu_b, v_w_proj_a, v_w_proj_b, v_w_proj_c, v_w_o, v_ln1_g, v_ln1_b, v_w_up, v_conv_w, v_conv_b, v_w_down, v_ln2_g, v_ln2_b):
    a = locals()
    W = {k: a[k] for k in WEIGHTS}
    Mo = {k: a["m_" + k] for k in WEIGHTS}
    Vo = {k: a["v_" + k] for k in WEIGHTS}
    S, D = x.shape[1], x.shape[2]
    FF2 = w_up.shape[2] * N_CHIPS
    FF = FF2 // 2
    L = DEPTH
    lay = _h_layout(D)
    NP = lay["total"]
    cx, cy, cc = _coords()
    chip = 2 * cx + cy
    T = _tile(S, 512)
    TM = _tile(S, 256, 16)
    TMW = _tile(S, 64, 16)

    shards = {k: tuple(W[k].shape) for k in BIG}
    full = {k: [None] * L for k in BIG}
    own = {(l, k): _cast_into_slot(W[k], l, chip, f"cast_{k}_l{l}") for l in range(L) for k in BIG}

    on_way = {"pairs": [], "arrived": []}

    def lay_out(pairs, gathered):
        for (l_, k), g in zip(pairs, gathered):
            _, r, c_ = shards[k]
            full[k][l_] = g.reshape(N_CHIPS * r, c_) if k in ROW_SHARDED else jnp.transpose(g, (1, 0, 2)).reshape(r, N_CHIPS * c_)

    def gather_behind(pairs):
        to_sib = _gather_stages(on_way["arrived"])[1](on_way["arrived"]) if on_way["pairs"] else None
        bufs = [own[p_] for p_ in pairs]
        return _both(to_sib, _gather_stages(bufs)[0](bufs) if pairs else None)

    def gathered_behind(pairs, outs):
        n_done = len(on_way["pairs"])
        lay_out(on_way["pairs"], outs[:n_done])
        on_way["pairs"], on_way["arrived"] = list(pairs), list(outs[n_done:])

    def mm_behind(a_, b_, mode, dt, name, pairs):
        stage = gather_behind(pairs)
        if stage is None:
            return _mm(a_, b_, mode, dt, name)
        out, outs = _mm(a_, b_, mode, dt, name, carry=stage)
        gathered_behind(pairs, outs)
        return out

    MIX = ["w_uq", "w_ukv", "w_proj_a", "w_proj_b", "w_proj_c", "w_o"]
    nxt = lambda l, names: [(l + 1, k) for k in names] if l + 1 < L else []
    first = [(0, "w_in")]
    gathered_behind(first, _run_stage(gather_behind(first), "gather_ici_w_in_l0"))
    gathered_behind([], _run_stage(gather_behind([]), "gather_sibling_w_in_l0"))

    small_sharded_full = {k: tuple(W[k].shape[:-1]) + (W[k].shape[-1] * N_CHIPS,) for k in SMALL_SHARDED}
    placed = []
    for k in ("b_gate", "conv_w"):
        z = jnp.zeros(small_sharded_full[k], F32)
        z = lax.dynamic_update_slice_in_dim(z, W[k], chip * W[k].shape[-1], axis=-1)
        placed.append(jnp.where(cc == 0, z, 0.0).reshape(-1))
    pv = jnp.concatenate(placed)
    n_pv = pv.shape[0]
    pv = jnp.pad(pv, (0, -n_pv % (8 * LANES))).reshape(-1, LANES)
    pv = _all_reduce_small(pv).reshape(-1)
    nb_ = int(np.prod(small_sharded_full["b_gate"]))
    b_gate_full = pv[:nb_].reshape(small_sharded_full["b_gate"])
    conv_w_full = pv[nb_:n_pv].reshape(small_sharded_full["conv_w"])

    inv_freq = ROPE_THETA ** (-jnp.arange(0, MLA_ROPE, 2, dtype=F32) / MLA_ROPE)
    ang = positions[0].astype(F32)[:, None] * inv_freq
    cos, sin = jnp.cos(ang), jnp.sin(ang)
    cos_t = jnp.concatenate([cos, cos, jnp.ones((S, LANES - MLA_ROPE), F32)], axis=1)
    sin_t = jnp.concatenate([sin, sin, jnp.zeros((S, LANES - MLA_ROPE), F32)], axis=1)

    row = lambda v: v.reshape(1, -1)
    cb = lambda name: lay[name][0] // lay[name][1]

    xs = x[0]
    saved = []
    for l in range(L):
        p = dict(
            w_in=_perm_w_in(full["w_in"][l], lay),
            sinks=row(sinks[l]), qg=row(q_norm_g[l]), kvg=row(kv_norm_g[l]), sg=row(sgu_ln_g[l]), sb=row(sgu_ln_b[l]),
            sw=sgu_w[l], sb3=sgu_b[l].reshape(SGU_GROUPS, SGU_CHUNK, 1),
            bg=b_gate_full[l], l1g=row(ln1_g[l]), l1b=row(ln1_b[l]), cw=conv_w_full[l], cbias=row(conv_b[l]),
            l2g=row(ln2_g[l]), l2b=row(ln2_b[l]))
        if l == 0:
            def fn_cast(i, rows, ps):
                return [rows[0]], []
            (xb,), _ = _rowwise(fn_cast, [_whole(xs)], [], [(D, BF16)], [], TM, "cast_x")
        own_mix = [(l, k) for k in MIX]
        own_up = [(l, "w_up")] if l == 0 else []
        own_down = [(l, "w_down")] if l == 0 else []
        h = mm_behind(xb, p["w_in"], "nn", BF16, "mm_h", own_mix)
        (y_a,), outs = _swa_fwd(h, cb("qa"), cb("ka"), cb("va"), p["sinks"], carry=gather_behind(own_up))
        gathered_behind(own_up, outs)
        p.update(w_uq=_perm_w_uq(full["w_uq"][l]), w_ukv=_perm_w_ukv(full["w_ukv"][l]), w_pa=full["w_proj_a"][l],
                 w_pb=full["w_proj_b"][l], w_pc=full["w_proj_c"][l], w_o=full["w_o"][l])
        def fn_rms(i, rows, ps):
            return [_rms_norm(rows[0].astype(F32), ps[0]), _rms_norm(rows[1].astype(F32), ps[1])], []
        (cqn, ckvn), _ = _rowwise(fn_rms, [(h, MLA_Q_RANK, cb("cq")), (h, MLA_KV_RANK, cb("ckv"))], [p["qg"], p["kvg"]],
                                  [(MLA_Q_RANK, BF16), (MLA_KV_RANK, BF16)], [], TM, "mla_rms")
        q_full = _mm(cqn, p["w_uq"], "nn", BF16, "mm_q")
        kv = _mm(ckvn, p["w_ukv"], "nn", BF16, "mm_kv")
        qr, kr = _rope_call(q_full, MLA_HEADS * LANES, 1, h, LANES, cb("kr"), cos_t, sin_t, 1.0, TM, "rope_fwd")
        behind_mla = own_down + nxt(l, ["w_in"])
        (y_b, lse), outs = _mla_fwd(q_full, qr, kv, kr, T, carry=gather_behind(behind_mla))
        gathered_behind(behind_mla, outs)
        y_c = _sgu_fwd(h, cb("hu"), cb("hv"), p["sg"], p["sb"], p["sw"], p["sb3"])
        pa = _mm(y_a, p["w_pa"], "nn", F32, "mm_pa")
        pb = _mm(y_b, p["w_pb"], "nn", F32, "mm_pb")
        pc = _mm(y_c, p["w_pc"], "nn", F32, "mm_pc")

        def merge_math(pa_, pb_, pc_, g_, b0, b1, b2):
            out = 0.0
            for br, (pp, bb) in enumerate(zip((pa_, pb_, pc_), (b0, b1, b2))):
                gate = jax.nn.sigmoid(g_[:, br * D:(br + 1) * D].astype(F32) + bb)
                out = out + gate * pp
            return out

        def fn_merge(i, rows, ps):
            bgv = ps[0]
            return [merge_math(rows[0], rows[1], rows[2], rows[3], bgv[0:1], bgv[1:2], bgv[2:3])], []
        (merged,), _ = _rowwise(fn_merge, [_whole(pa), _whole(pb), _whole(pc), (h, N_BRANCHES * D, cb("g"))], [p["bg"]],
                                [(D, BF16)], [], TMW, "merge_fwd")
        o = _mm(merged, p["w_o"], "nn", F32, "mm_o")

        def ln_res_math(x_, o_, g_, b_):
            return _layer_norm(DN_ALPHA * x_ + o_, g_, b_)

        def fn_ln(i, rows, ps):
            y = ln_res_math(rows[0], rows[1], ps[0], ps[1])
            return [y, y], []
        (x1, x1b), _ = _rowwise(fn_ln, [_whole(xs), _whole(o)], [p["l1g"], p["l1b"]], [(D, F32), (D, BF16)], [], TM, "ln1_fwd")
        p.update(w_up=full["w_up"][l])
        up = mm_behind(x1b, p["w_up"], "nn", BF16, "mm_up", nxt(l, ["w_up"]))
        p.update(w_down=full["w_down"][l])
        cv_ = _conv_fwd(up, p["cw"], p["cbias"])

        def glu_math(cg, cvv):
            return jax.nn.silu(cg.astype(F32)) * cvv.astype(F32)

        def fn_glu(i, rows, ps):
            return [glu_math(rows[0], rows[1])], []
        (act,), _ = _rowwise(fn_glu, [(cv_, FF, 0), (cv_, FF, 1)], [], [(FF, BF16)], [], TMW, "glu_fwd")
        dn = mm_behind(act, p["w_down"], "nn", F32, "mm_down", nxt(l, ["w_down"]))
        (x2, x2b), _ = _rowwise(fn_ln, [_whole(x1), _whole(dn)], [p["l2g"], p["l2b"]], [(D, F32), (D, BF16)], [], TM, "ln2_fwd")
        saved.append(dict(p=p, x0=xs, x0b=xb, h=h, y_a=y_a, cqn=cqn, ckvn=ckvn, q_full=q_full, kv=kv, qr=qr, kr=kr, y_b=y_b,
                          lse=lse, y_c=y_c, pa=pa, pb=pb, pc=pc, merged=merged, o=o, x1=x1, x1b=x1b, up=up, cv=cv_, act=act,
                          dn=dn))
        xs, xb = x2, x2b

    def fn_loss(i, rows, ps):
        diff = rows[0] - rows[1]
        part = jnp.sum(jnp.mean(jnp.square(diff), axis=-1, keepdims=True), axis=0, keepdims=True)
        return [diff * (1.0 / D)], [jnp.broadcast_to(part, (8, LANES))]
    (dx,), (loss_acc,) = _rowwise(fn_loss, [_whole(xs), _whole(loss_target[0])], [], [(D, F32)], [(8, LANES)], TM, "loss")
    loss = lax.psum(0.5 * loss_acc[0, 0], ("x", "y", "c"))

    gbig = {k: [None] * L for k in BIG}
    gsmall = {k: [None] * L for k in SMALL}
    reduced = {}
    pending = None
    for l in reversed(range(L)):
        s = saved[l]
        p = s["p"]

        def fn_ln_bwd(i, rows, ps):
            _, vjp = jax.vjp(ln_res_math, rows[0], rows[1], ps[0], ps[1])
            dx_, do_, dg_, db_ = vjp(rows[2])
            return [dx_, do_], [dg_, db_]
        (dx1_res, ddn), (g_l2g, g_l2b) = _rowwise(fn_ln_bwd, [_whole(s["x1"]), _whole(s["dn"]), _whole(dx)], [p["l2g"], p["l2b"]],
                                                  [(D, F32), (D, BF16)], [(1, D), (1, D)], TM, "ln2_bwd")
        gsmall["ln2_g"][l], gsmall["ln2_b"][l] = g_l2g, g_l2b
        row_chunks = lambda g: g.reshape(N_CHIPS, g.shape[0] // N_CHIPS, g.shape[1])
        gbig["w_down"][l] = row_chunks(_mm_tn(s["act"], ddn, F32, "mm_dw_down"))
        dact = _mm(ddn, p["w_down"], "nt", BF16, "mm_dact")

        def fn_glu_bwd(i, rows, ps):
            _, vjp = jax.vjp(glu_math, rows[0], rows[1])
            dcg, dcv = vjp(rows[2].astype(F32))
            return [jnp.concatenate([dcg, dcv], axis=1)], []
        (dc,), _ = _rowwise(fn_glu_bwd, [(s["cv"], FF, 0), (s["cv"], FF, 1), _whole(dact)], [], [(FF2, BF16)], [], TMW, "glu_bwd")
        dup, g_cw, g_cb = _conv_bwd(s["up"], dc, p["cw"])
        gsmall["conv_w"][l], gsmall["conv_b"][l] = g_cw, g_cb
        if pending is None:
            gbig["w_up"][l] = _mm_tn(s["x1b"], dup, F32, "mm_dw_up", tm=1024, col_chunks=N_CHIPS)
        else:
            gbig["w_up"][l], from_sibling = _mm_tn(s["x1b"], dup, F32, "mm_dw_up", tm=1024, col_chunks=N_CHIPS,
                                                carry=_swap_stage(pending))
            partial = [_add_half(g, r, cc, f"rs_add_{k}_l{l + 1}") for k, g, r in zip(BIG, pending, from_sibling)]
        dx1 = _mm(dup, p["w_up"], "nt", F32, "mm_dx1", add=dx1_res)
        (dx0_res, do_), (g_l1g, g_l1b) = _rowwise(fn_ln_bwd, [_whole(s["x0"]), _whole(s["o"]), _whole(dx1)], [p["l1g"], p["l1b"]],
                                                  [(D, F32), (D, BF16)], [(1, D), (1, D)], TM, "ln1_bwd")
        gsmall["ln1_g"][l], gsmall["ln1_b"][l] = g_l1g, g_l1b
        gbig["w_o"][l] = row_chunks(_mm_tn(s["merged"], do_, F32, "mm_dw_o"))
        dmerged = _mm(do_, p["w_o"], "nt", F32, "mm_dmerged")

        def fn_merge_bwd(i, rows, ps):
            bgv = ps[0]
            _, vjp = jax.vjp(merge_math, rows[0], rows[1], rows[2], rows[3], bgv[0:1], bgv[1:2], bgv[2:3])
            dpa, dpb, dpc, dg_, db0, db1, db2 = vjp(rows[4])
            return [dpa, dpb, dpc, dg_], [db0, db1, db2]
        (dpa, dpb, dpc, dgl), (db0, db1, db2) = _rowwise(
            fn_merge_bwd, [_whole(s["pa"]), _whole(s["pb"]), _whole(s["pc"]), (s["h"], N_BRANCHES * D, cb("g")), _whole(dmerged)],
            [p["bg"]], [(D, BF16), (D, BF16), (D, BF16), (N_BRANCHES * D, BF16)], [(1, D)] * 3, TMW, "merge_bwd")
        gsmall["b_gate"][l] = jnp.concatenate([db0, db1, db2], axis=0)
        gbig["w_proj_a"][l] = _mm_tn(s["y_a"], dpa, F32, "mm_dw_pa", col_chunks=N_CHIPS)
        gbig["w_proj_b"][l] = row_chunks(_mm_tn(s["y_b"], dpb, F32, "mm_dw_pb"))
        gbig["w_proj_c"][l] = _mm_tn(s["y_c"], dpc, F32, "mm_dw_pc", col_chunks=N_CHIPS)
        dy_a = _mm(dpa, p["w_pa"], "nt", BF16, "mm_dy_a")
        dy_b = _mm(dpb, p["w_pb"], "nt", BF16, "mm_dy_b")
        dy_c = _mm(dpc, p["w_pc"], "nt", BF16, "mm_dy_c")
        dh_c, g_sg, g_sb, g_sw, g_sb3 = _sgu_bwd(s["h"], cb("hu"), cb("hv"), dy_c, p["sg"], p["sb"], p["sw"], p["sb3"])
        gsmall["sgu_ln_g"][l], gsmall["sgu_ln_b"][l], gsmall["sgu_w"][l] = g_sg, g_sb, g_sw
        gsmall["sgu_b"][l] = g_sb3.reshape(SGU_GROUPS, SGU_CHUNK)
        part_x = ["w_up", "w_down"]
        part_z = ["w_in"] if l == 0 else []
        part_y = [k for k in BIG if k not in part_x + part_z]
        scatter_of = lambda names: None if pending is None else _scatter_stage([partial[BIG.index(k)][1] for k in names])
        hide_early = l == 0
        early = [gbig[k][l] for k in EARLY]
        n_early = len(EARLY) if hide_early else 0
        (dqa, dka, dva, g_sinks), landed_ = _swa_bwd(s["h"], cb("qa"), cb("ka"), cb("va"), p["sinks"], dy_a,
                                                     carry=_both(_swap_stage(early) if hide_early else None, scatter_of(part_x)))
        early_from_sibling, arrived_x = landed_[:n_early], landed_[n_early:]
        gsmall["sinks"][l] = g_sinks
        if hide_early:
            early_partial = [_add_half(g, r, cc, f"rs_add_{k}_l{l}") for k, g, r in zip(EARLY, early, early_from_sibling)]
        delta = _mla_delta(dy_b, s["y_b"], T)
        (dqn, dqr), arrived_y = _mla_bwd_dq(s["q_full"], s["qr"], s["kv"], s["kr"], dy_b, s["lse"], delta, T,
                                            carry=scatter_of(part_y))
        if pending is not None:
            arrived = {**dict(zip(part_x, arrived_x)), **dict(zip(part_y, arrived_y))}
            for k, (p32, _) in zip(BIG, partial):
                if k in arrived:
                    reduced[k] = _sum_into(p32, arrived[k], chip, cc, l + 1, L, reduced.get(k), f"rs_sum_{k}_l{l + 1}")
        (dkn, dv, dkr_heads), early_arrived = _mla_bwd_dkv(
            s["q_full"], s["qr"], s["kv"], s["kr"], dy_b, s["lse"], delta, T,
            carry=_scatter_stage([p16 for _, p16 in early_partial]) if hide_early else None)
        if hide_early:
            for k, (p32, _), arr in zip(EARLY, early_partial, early_arrived):
                reduced[k] = _sum_into(p32, arr, chip, cc, l, L, reduced.get(k), f"rs_sum_{k}_l{l}")
        dkr = _sum_heads(dkr_heads, TM)
        dqr_raw, dkr_raw = _rope_call(dqr, MLA_HEADS * LANES, 0, dkr, LANES, 0, cos_t, sin_t, -1.0, TM, "rope_bwd")
        dq_full = jnp.concatenate([dqn, dqr_raw], axis=1)
        dkv = jnp.concatenate([dkn, dv], axis=1)
        gbig["w_uq"][l] = _col_chunks(_unperm_w_uq(_mm_tn(s["cqn"], dq_full, F32, "mm_dw_uq")))
        gbig["w_ukv"][l] = _col_chunks(_unperm_w_ukv(_mm_tn(s["ckvn"], dkv, F32, "mm_dw_ukv")))
        dcqn = _mm(dq_full, p["w_uq"], "nt", F32, "mm_dcqn")
        dckvn = _mm(dkv, p["w_ukv"], "nt", F32, "mm_dckvn")

        def fn_rms_bwd(i, rows, ps):
            _, vjp1 = jax.vjp(lambda c_, g_: _rms_norm(c_.astype(F32), g_), rows[0], ps[0])
            _, vjp2 = jax.vjp(lambda c_, g_: _rms_norm(c_.astype(F32), g_), rows[1], ps[1])
            d1, dg1 = vjp1(rows[2])
            d2, dg2 = vjp2(rows[3])
            return [d1, d2], [dg1, dg2]
        (dcq, dckv), (g_qg, g_kvg) = _rowwise(
            fn_rms_bwd, [(s["h"], MLA_Q_RANK, cb("cq")), (s["h"], MLA_KV_RANK, cb("ckv")), _whole(dcqn), _whole(dckvn)],
            [p["qg"], p["kvg"]], [(MLA_Q_RANK, BF16), (MLA_KV_RANK, BF16)], [(1, MLA_Q_RANK), (1, MLA_KV_RANK)], TM, "mla_rms_bwd")
        gsmall["q_norm_g"][l], gsmall["kv_norm_g"][l] = g_qg, g_kvg
        tail = jnp.zeros((S, NP - lay["kr"][0] - LANES), BF16)
        dh = jnp.concatenate([dgl, dqa, dcq, dckv, dh_c, dka, dva, dkr_raw, tail], axis=1)
        if l == 0:
            ready = [k for k in BIG if k not in part_z]
            done = [list(range(L)) if k in EARLY else list(range(1, L)) for k in ready]
            dw_in, outs = _mm_tn(s["x0b"], dh, F32, "mm_dw_in",
                                 carry=_both(_join_stage([reduced[k] for k in ready], done), scatter_of(part_z)))
            reduced.update(zip(ready, outs[:len(ready)]))
            for k, arr in zip(part_z, outs[len(ready):]):
                reduced[k] = _sum_into(partial[BIG.index(k)][0], arr, chip, cc, l + 1, L, reduced.get(k), f"rs_sum_{k}_l{l + 1}")
        else:
            dw_in = _mm_tn(s["x0b"], dh, F32, "mm_dw_in")
        gbig["w_in"][l] = _col_chunks(_unperm_w_in(dw_in, lay, D))
        if l > 0:
            dx = _mm(dh, p["w_in"], "nt", F32, "mm_dx0", add=dx0_res)

        pending = [gbig[k][l] for k in BIG]

    late = [gbig[k][0] for k in LATE]
    from_sibling = _run_stage(_swap_stage(late), "rs_swap_halves_l0")
    partial = [_add_half(g, r, cc, f"rs_add_{k}_l0") for k, g, r in zip(LATE, late, from_sibling)]
    dx, arrived = _mm(dh, p["w_in"], "nt", F32, "mm_dx0", add=dx0_res, carry=_scatter_stage([p16 for _, p16 in partial]))
    for k, (p32, _), arr in zip(LATE, partial, arrived):
        reduced[k] = _sum_into(p32, arr, chip, cc, 0, L, reduced.get(k), f"rs_sum_{k}_l0")

    grad_x = dx.reshape(x.shape)
    unjoined = [[0] + (list(range(1, L)) if k in part_z else []) for k in LATE]
    g_big = {**reduced, **dict(zip(LATE, _run_stage(_join_stage([reduced[k] for k in LATE], unjoined), "rs_join_halves")))}

    small_shapes = {k: (small_sharded_full[k] if k in SMALL_SHARDED else tuple(W[k].shape)) for k in SMALL}
    sv = jnp.concatenate([jnp.stack(gsmall[k]).reshape(-1) for k in SMALL])
    n_sv = sv.shape[0]
    sv = jnp.pad(sv, (0, -n_sv % (8 * LANES))).reshape(-1, LANES)
    sv = _all_reduce_small(sv).reshape(-1)
    g_small, o_ = {}, 0
    for k in SMALL:
        n = int(np.prod(small_shapes[k]))
        g = sv[o_:o_ + n].reshape(small_shapes[k])
        if k in SMALL_SHARDED:
            g = lax.dynamic_slice_in_dim(g, chip * W[k].shape[-1], W[k].shape[-1], axis=-1)
        g_small[k] = g
        o_ += n

    delta, new_m, new_v = {}, {}, {}
    swap_minor = lambda t: jnp.transpose(t, (0, 2, 1))
    for k in BIG:
        if shards[k][2] % LANES and not shards[k][1] % LANES:
            out = _adamw(swap_minor(W[k]), swap_minor(g_big[k]), swap_minor(Mo[k]), swap_minor(Vo[k]), "adamw_" + k)
            delta[k], new_m[k], new_v[k] = [swap_minor(t) for t in out]
        else:
            delta[k], new_m[k], new_v[k] = _adamw(W[k], g_big[k], Mo[k], Vo[k], "adamw_" + k)
    pack = lambda t: jnp.concatenate([t[k].reshape(-1) for k in SMALL])
    n_small = sum(int(np.prod(W[k].shape)) for k in SMALL)
    pad2 = lambda t: jnp.pad(t, (0, -n_small % (8 * LANES))).reshape(1, -1, LANES)
    d_, m_, v_ = _adamw(pad2(pack(W)), pad2(pack(g_small)), pad2(pack(Mo)), pad2(pack(Vo)), "adamw_small")
    o_ = 0
    for k in SMALL:
        n = int(np.prod(W[k].shape))
        take = lambda t: t.reshape(-1)[o_:o_ + n].reshape(W[k].shape)
        delta[k], new_m[k], new_v[k] = take(d_), take(m_), take(v_)
        o_ += n

    grads = {**g_big, **g_small}
    return (loss, grad_x, *[grads[k] for k in WEIGHTS], *[delta[k] for k in WEIGHTS], *[new_m[k] for k in WEIGHTS],
            *[new_v[k] for k in WEIGHTS])
```

```python
import functools
import math

import jax
import jax.numpy as jnp
import numpy as np
from jax import lax
from jax.experimental import pallas as pl
from jax.experimental.pallas import tpu as pltpu

F32, BF16 = jnp.float32, jnp.bfloat16
SDS = jax.ShapeDtypeStruct
MESH = pl.DeviceIdType.MESH

SWA_Q_HEADS, SWA_KV_HEADS, SWA_HEAD_DIM, SWA_BLOCK = 16, 2, 64, 128
MLA_HEADS, MLA_NOPE, MLA_ROPE, MLA_V = 16, 128, 64, 128
MLA_Q_RANK, MLA_KV_RANK = 512, 512
ROPE_THETA = 10000.0
SGU_GROUPS, SGU_GROUP_DIM, SGU_CHUNK = 8, 128, 128
SGU_WIDTH = SGU_GROUPS * SGU_GROUP_DIM
A_Q = SWA_Q_HEADS * SWA_HEAD_DIM
A_KV = SWA_KV_HEADS * SWA_HEAD_DIM
N_BRANCHES = 3
DEPTH = 2
EPS = 1e-5
MASK_VALUE = -1e30
DN_ALPHA = (2 * DEPTH) ** 0.25
ADAM_LR, ADAM_B1, ADAM_B2, ADAM_EPS, ADAM_WD, ADAM_STEP = 0.001, 0.9, 0.999, 1e-08, 0.01, 10
N_CHIPS = 4

LANES = 128
VMEM_LIMIT = 48 * 1024 * 1024

BIG = ["w_in", "w_uq", "w_ukv", "w_proj_a", "w_proj_b", "w_proj_c", "w_o", "w_up", "w_down"]
ROW_SHARDED = {"w_proj_b", "w_o", "w_down"}
LATE = ["w_in", "w_uq", "w_ukv"]
EARLY = [k for k in BIG if k not in LATE]
SMALL = ["b_gate", "sinks", "q_norm_g", "kv_norm_g", "sgu_ln_g", "sgu_ln_b", "sgu_w", "sgu_b", "ln1_g", "ln1_b",
         "conv_w", "conv_b", "ln2_g", "ln2_b"]
SMALL_SHARDED = {"b_gate", "conv_w"}
WEIGHTS = ["w_in", "b_gate", "sinks", "q_norm_g", "kv_norm_g", "w_uq", "w_ukv", "sgu_ln_g", "sgu_ln_b", "sgu_w", "sgu_b",
           "w_proj_a", "w_proj_b", "w_proj_c", "w_o", "ln1_g", "ln1_b", "w_up", "conv_w", "conv_b", "w_down", "ln2_g", "ln2_b"]


def _pcall(body, **kw):
    return pl.pallas_call(body, **kw)


def _params(sem=None):
    return pltpu.CompilerParams(dimension_semantics=sem, vmem_limit_bytes=VMEM_LIMIT)


def _tile(dim, pref, align=LANES):
    t = (min(pref, dim) // align) * align
    while t >= align:
        if dim % t == 0:
            return t
        t -= align
    return dim


def _mm(a, b, mode, out_dtype, name, add=None, tm=1024, tn=512, tk=2048, col_chunks=1, carry=None):
    assert mode in ("nn", "nt")
    if mode == "nn":
        (M, K), (K2, N) = a.shape, b.shape
    else:
        (M, K), (N, K2) = a.shape, b.shape
    assert K == K2, (a.shape, b.shape, mode)
    assert N % col_chunks == 0
    tm, tn, tk = _tile(M, tm), _tile(N // col_chunks, tn), _tile(K, tk)
    assert (N // col_chunks) % tn == 0
    per_chunk = (N // col_chunks) // tn
    nk = K // tk
    a_spec = pl.BlockSpec((tm, tk), lambda i, j, k: (i, k))
    if mode == "nt":
        b_spec = pl.BlockSpec((tn, tk), lambda i, j, k: (j, k))
    else:
        b_spec = pl.BlockSpec((tk, tn), lambda i, j, k: (k, j))
    dn = {"nn": (((1,), (0,)), ((), ())), "nt": (((1,), (1,)), ((), ()))}[mode]
    chunked = col_chunks > 1
    if chunked:
        assert add is None
        o_spec = pl.BlockSpec((1, tm, tn), lambda i, j, k: (lax.div(j, per_chunk), i, lax.rem(j, per_chunk)))
        out_shape = SDS((col_chunks, M, N // col_chunks), out_dtype)
    else:
        o_spec = pl.BlockSpec((tm, tn), lambda i, j, k: (i, j))
        out_shape = SDS((M, N), out_dtype)
    has_add = add is not None

    def body(*refs):
        if has_add:
            a_ref, b_ref, add_ref, o_ref, acc_ref = refs
        else:
            a_ref, b_ref, o_ref, acc_ref = refs
        k = pl.program_id(2)

        @pl.when(k == 0)
        def _():
            acc_ref[...] = jnp.zeros_like(acc_ref)

        acc_ref[...] += lax.dot_general(a_ref[...].astype(BF16), b_ref[...].astype(BF16), dn,
                                        preferred_element_type=F32)

        @pl.when(k == nk - 1)
        def _():
            r = acc_ref[...]
            if has_add:
                r = r + add_ref[...].astype(F32)
            if chunked:
                o_ref[0] = r.astype(o_ref.dtype)
            else:
                o_ref[...] = r.astype(o_ref.dtype)

    ins = [a, b] + ([add] if has_add else [])
    in_specs = [a_spec, b_spec] + ([o_spec] if has_add else [])
    (out,), carried = _call(body, name, (M // tm, N // tn, nk), 0, in_specs, [o_spec], [out_shape], [pltpu.VMEM((tm, tn), F32)],
                            ins, ("parallel", "parallel", "arbitrary"), carry)
    return out if carry is None else (out, carried)


def _mm_tn(a, b, out_dtype, name, tm=512, tn=512, col_chunks=1, carry=None):
    (K, M), (K2, N) = a.shape, b.shape
    assert K == K2 and N % col_chunks == 0
    tm, tn = _tile(M, tm), _tile(N // col_chunks, tn)
    per_chunk = (N // col_chunks) // tn
    chunked = col_chunks > 1
    if chunked:
        o_spec = pl.BlockSpec((1, tm, tn), lambda i, j: (lax.div(j, per_chunk), i, lax.rem(j, per_chunk)))
        out_shape = SDS((col_chunks, M, N // col_chunks), out_dtype)
    else:
        o_spec = pl.BlockSpec((tm, tn), lambda i, j: (i, j))
        out_shape = SDS((M, N), out_dtype)

    def body(a_ref, b_ref, o_ref, at_ref):
        @pl.when(pl.program_id(1) == 0)
        def _():
            at_ref[...] = a_ref[...].astype(BF16).T

        r = jnp.dot(at_ref[...], b_ref[...].astype(BF16), preferred_element_type=F32)
        if chunked:
            o_ref[0] = r.astype(o_ref.dtype)
        else:
            o_ref[...] = r.astype(o_ref.dtype)

    (out,), carried = _call(body, name, (M // tm, N // tn), 0,
                            [pl.BlockSpec((K, tm), lambda i, j: (0, i)), pl.BlockSpec((K, tn), lambda i, j: (0, j))],
                            [o_spec], [out_shape], [pltpu.VMEM((tm, K), BF16)], [a, b], ("parallel", "arbitrary"), carry)
    return out if carry is None else (out, carried)


def _rowwise(fn, rows, params, row_outs, acc_outs, tm, name):
    n_rows = rows[0][0].shape[0]
    assert n_rows % tm == 0
    nr, npar, no = len(rows), len(params), len(row_outs)

    def body(*refs):
        i = pl.program_id(0)
        r, p = refs[:nr], refs[nr:nr + npar]
        o, acc = refs[nr + npar:nr + npar + no], refs[nr + npar + no:]
        outs, sums = fn(i, [x[...] for x in r], [x[...] for x in p])
        for ref, val in zip(o, outs, strict=True):
            ref[...] = val.astype(ref.dtype)
        if acc:
            @pl.when(i == 0)
            def _():
                for ref in acc:
                    ref[...] = jnp.zeros_like(ref)
            for ref, val in zip(acc, sums, strict=True):
                ref[...] += val.astype(F32)

    def full(shape):
        nd = len(shape)
        return pl.BlockSpec(tuple(shape), lambda i: (0,) * nd)

    in_specs = [pl.BlockSpec((tm, w), (lambda i, cb=cb: (i, cb))) for (_, w, cb) in rows] + [full(p.shape) for p in params]
    out_specs = [pl.BlockSpec((tm, w), lambda i: (i, 0)) for (w, _) in row_outs] + [full(s) for s in acc_outs]
    out_shape = [SDS((n_rows, w), dt) for (w, dt) in row_outs] + [SDS(tuple(s), F32) for s in acc_outs]
    res = _pcall(body, name=name, out_shape=out_shape, grid=(n_rows // tm,), in_specs=in_specs, out_specs=out_specs,
                 compiler_params=_params(("arbitrary",)))(*[r[0] for r in rows], *params)
    return list(res[:no]), list(res[no:])


def _whole(a):
    return (a, a.shape[1], 0)


def _gelu(x):
    return 0.5 * x * (1.0 + lax.erf(x * (1.0 / math.sqrt(2.0))))


def _layer_norm(x, g, b):
    mu = x.mean(-1, keepdims=True)
    var = jnp.mean(jnp.square(x - mu), -1, keepdims=True)
    return (x - mu) * lax.rsqrt(var + EPS) * g + b


def _rms_norm(x, g):
    return x * lax.rsqrt(jnp.mean(jnp.square(x), -1, keepdims=True) + EPS) * g


def _sgu_math(hu, hv, ln_g, ln_b, ws, bs):
    u = _gelu(hu.astype(F32))
    vn = _layer_norm(_gelu(hv.astype(F32)), ln_g, ln_b)
    r = lax.broadcasted_iota(jnp.int32, (SGU_CHUNK, SGU_CHUNK), 0)
    c = lax.broadcasted_iota(jnp.int32, (SGU_CHUNK, SGU_CHUNK), 1)
    outs = []
    for g in range(SGU_GROUPS):
        w = jnp.where(r >= c, ws[g], 0.0).astype(BF16)
        vg = vn[:, g * SGU_GROUP_DIM:(g + 1) * SGU_GROUP_DIM].astype(BF16)
        outs.append(jnp.dot(w, vg, preferred_element_type=F32) + bs[g])
    return u * jnp.concatenate(outs, axis=1)


def _sgu_fwd(h, cu, cv, ln_g, ln_b, w, b3):
    def fn(i, rows, ps):
        g_, b_, w_, b3_ = ps
        y = _sgu_math(rows[0], rows[1], g_, b_, [w_[g] for g in range(SGU_GROUPS)], [b3_[g] for g in range(SGU_GROUPS)])
        return [y], []
    (y,), _ = _rowwise(fn, [(h, SGU_WIDTH, cu), (h, SGU_WIDTH, cv)], [ln_g, ln_b, w, b3], [(SGU_WIDTH, BF16)], [],
                       SGU_CHUNK, "sgu_fwd")
    return y


def _sgu_bwd(h, cu, cv, dy, ln_g, ln_b, w, b3):
    nd = 2 * SGU_WIDTH

    def body(hu_ref, hv_ref, dy_ref, g_ref, b_ref, w_ref, b3_ref, dh_ref, dg_ref, db_ref, dw_ref, db3_ref):
        i = pl.program_id(0)

        @pl.when(i == 0)
        def _():
            dg_ref[...] = jnp.zeros_like(dg_ref)
            db_ref[...] = jnp.zeros_like(db_ref)
            dw_ref[...] = jnp.zeros_like(dw_ref)
            db3_ref[...] = jnp.zeros_like(db3_ref)

        ws = [w_ref[g] for g in range(SGU_GROUPS)]
        bs = [b3_ref[g] for g in range(SGU_GROUPS)]
        _, vjp = jax.vjp(_sgu_math, hu_ref[...], hv_ref[...], g_ref[...], b_ref[...], ws, bs)
        dhu, dhv, dg, db, dws, dbs = vjp(dy_ref[...].astype(F32))
        dh_ref[...] = jnp.concatenate([dhu, dhv], axis=1).astype(dh_ref.dtype)
        dg_ref[...] += dg
        db_ref[...] += db
        for g in range(SGU_GROUPS):
            dw_ref[g] += dws[g]
            db3_ref[g] += dbs[g]

    n = h.shape[0]
    blk = lambda cb: pl.BlockSpec((SGU_CHUNK, SGU_WIDTH), lambda i, cb=cb: (i, cb))
    full = lambda s: pl.BlockSpec(tuple(s), lambda i: (0,) * len(s))
    return _pcall(
        body, name="sgu_bwd", grid=(n // SGU_CHUNK,),
        out_shape=[SDS((n, nd), BF16), SDS(ln_g.shape, F32), SDS(ln_b.shape, F32), SDS(w.shape, F32), SDS(b3.shape, F32)],
        in_specs=[blk(cu), blk(cv), blk(0), full(ln_g.shape), full(ln_b.shape), full(w.shape), full(b3.shape)],
        out_specs=[pl.BlockSpec((SGU_CHUNK, nd), lambda i: (i, 0)), full(ln_g.shape), full(ln_b.shape), full(w.shape),
                   full(b3.shape)],
        compiler_params=_params(("arbitrary",)))(h, h, dy, ln_g, ln_b, w, b3)


def _swa_math(q, kp, kc, vp, vc, sinks, not_first):
    kw = jnp.concatenate([kp, kc], axis=0).astype(BF16)
    vw = jnp.concatenate([vp, vc], axis=0).astype(BF16)
    qb = q.astype(BF16)
    q_off = lax.broadcasted_iota(jnp.int32, (SWA_BLOCK, 2 * SWA_BLOCK), 0) + SWA_BLOCK
    k_off = lax.broadcasted_iota(jnp.int32, (SWA_BLOCK, 2 * SWA_BLOCK), 1)
    rel = q_off - k_off
    valid = (rel >= 0) & (rel < SWA_BLOCK) & (not_first | (k_off >= SWA_BLOCK))
    G = SWA_Q_HEADS // SWA_KV_HEADS
    outs = []
    for head in range(SWA_Q_HEADS):
        hk = head // G
        qh = qb[:, head * SWA_HEAD_DIM:(head + 1) * SWA_HEAD_DIM]
        kh = kw[:, hk * SWA_HEAD_DIM:(hk + 1) * SWA_HEAD_DIM]
        vh = vw[:, hk * SWA_HEAD_DIM:(hk + 1) * SWA_HEAD_DIM]
        s = lax.dot_general(qh, kh, (((1,), (1,)), ((), ())), preferred_element_type=F32) * (SWA_HEAD_DIM ** -0.5)
        s = jnp.where(valid, s, MASK_VALUE)
        sink = sinks[:, head:head + 1]
        m = jnp.maximum(s.max(-1, keepdims=True), sink)
        p = jnp.exp(s - m)
        p = (p / (p.sum(-1, keepdims=True) + jnp.exp(sink - m))).astype(BF16)
        outs.append(jnp.dot(p, vh, preferred_element_type=F32))
    return jnp.concatenate(outs, axis=1)


def _swa_fwd(h, cq, ck, cv, sinks, carry=None):
    n = h.shape[0]
    nb = n // SWA_BLOCK

    def body(q_ref, kp_ref, kc_ref, vp_ref, vc_ref, s_ref, o_ref):
        i = pl.program_id(0)
        f = lambda x: x[...].astype(F32)
        o_ref[...] = _swa_math(f(q_ref), f(kp_ref), f(kc_ref), f(vp_ref), f(vc_ref), s_ref[...], i > 0).astype(o_ref.dtype)

    prev = lambda cb: pl.BlockSpec((SWA_BLOCK, A_KV), lambda i, cb=cb: (jnp.maximum(i - 1, 0), cb))
    cur = lambda cb: pl.BlockSpec((SWA_BLOCK, A_KV), lambda i, cb=cb: (i, cb))
    return _call(body, "swa_fwd", (nb,), 0,
                 [pl.BlockSpec((SWA_BLOCK, A_Q), lambda i: (i, cq)), prev(ck), cur(ck), prev(cv), cur(cv),
                  pl.BlockSpec((1, SWA_Q_HEADS), lambda i: (0, 0))],
                 [pl.BlockSpec((SWA_BLOCK, A_Q), lambda i: (i, 0))], [SDS((n, A_Q), BF16)], [],
                 [h, h, h, h, h, sinks], ("arbitrary",), carry)


def _swa_bwd(h, cq, ck, cv, sinks, dy, carry=None):
    n = h.shape[0]
    nb = n // SWA_BLOCK

    def body(q_ref, kp_ref, kc_ref, vp_ref, vc_ref, s_ref, dy_ref, dq_ref, dk_ref, dv_ref, ds_ref, ck_ref, cv_ref):
        r = pl.program_id(0)
        blk = nb - 1 - r

        @pl.when(r == 0)
        def _():
            ds_ref[...] = jnp.zeros_like(ds_ref)
            ck_ref[...] = jnp.zeros_like(ck_ref)
            cv_ref[...] = jnp.zeros_like(cv_ref)

        f = lambda x: x[...].astype(F32)
        not_first = blk > 0
        _, vjp = jax.vjp(lambda q, kp, kc, vp, vc, s: _swa_math(q, kp, kc, vp, vc, s, not_first),
                         f(q_ref), f(kp_ref), f(kc_ref), f(vp_ref), f(vc_ref), s_ref[...])
        dq, dkp, dkc, dvp, dvc, dsk = vjp(f(dy_ref))
        dq_ref[...] = dq.astype(dq_ref.dtype)
        dk_ref[...] = (dkc + ck_ref[...]).astype(dk_ref.dtype)
        dv_ref[...] = (dvc + cv_ref[...]).astype(dv_ref.dtype)
        ck_ref[...] = dkp
        cv_ref[...] = dvp
        ds_ref[...] += dsk

    rev = lambda i: nb - 1 - i
    prev = lambda cb: pl.BlockSpec((SWA_BLOCK, A_KV), lambda i, cb=cb: (jnp.maximum(rev(i) - 1, 0), cb))
    cur = lambda cb: pl.BlockSpec((SWA_BLOCK, A_KV), lambda i, cb=cb: (rev(i), cb))
    return _call(
        body, "swa_bwd", (nb,), 0,
        [pl.BlockSpec((SWA_BLOCK, A_Q), lambda i: (rev(i), cq)), prev(ck), cur(ck), prev(cv), cur(cv),
         pl.BlockSpec((1, SWA_Q_HEADS), lambda i: (0, 0)), pl.BlockSpec((SWA_BLOCK, A_Q), lambda i: (rev(i), 0))],
        [pl.BlockSpec((SWA_BLOCK, A_Q), lambda i: (rev(i), 0)), pl.BlockSpec((SWA_BLOCK, A_KV), lambda i: (rev(i), 0)),
         pl.BlockSpec((SWA_BLOCK, A_KV), lambda i: (rev(i), 0)), pl.BlockSpec((1, SWA_Q_HEADS), lambda i: (0, 0))],
        [SDS((n, A_Q), BF16), SDS((n, A_KV), BF16), SDS((n, A_KV), BF16), SDS((1, SWA_Q_HEADS), F32)],
        [pltpu.VMEM((SWA_BLOCK, A_KV), F32), pltpu.VMEM((SWA_BLOCK, A_KV), F32)],
        [h, h, h, h, h, sinks, dy], ("arbitrary",), carry)


def _rope(x, cos, sin, sign):
    w = x.shape[1]
    reps = w // LANES
    ct = jnp.tile(cos, (1, reps)) if reps > 1 else cos
    st = jnp.tile(sin, (1, reps)) if reps > 1 else sin
    fwd = pltpu.roll(x, MLA_ROPE // 2, axis=1)
    bwd = pltpu.roll(x, w - MLA_ROPE // 2, axis=1)
    lane = lax.broadcasted_iota(jnp.int32, x.shape, 1) % LANES
    rot = jnp.where(lane < MLA_ROPE // 2, -bwd, fwd)
    return x * ct + sign * (rot * st)


def _rope_call(a, wa, ca, b, wb, cb, cos, sin, sign, tm, name):
    def fn(i, rows, ps):
        xa, xb, c_, s_ = rows
        return [_rope(xa.astype(F32), c_, s_, sign), _rope(xb.astype(F32), c_, s_, sign)], []
    (ra, rb), _ = _rowwise(fn, [(a, wa, ca), (b, wb, cb), _whole(cos), _whole(sin)], [], [(wa, BF16), (wb, BF16)], [], tm, name)
    return ra, rb


MLA_SCALE = (MLA_NOPE + MLA_ROPE) ** -0.5
LOG2E = math.log2(math.e)


MLA_HP = 8
MLA_W = MLA_HP * LANES


def _mla_scores(qn_ref, qr_ref, kn_ref, kr_ref, hh, masked):
    cols = slice(hh * LANES, (hh + 1) * LANES)
    q = jnp.concatenate([qn_ref[:, cols], qr_ref[:, cols]], axis=1)
    k = jnp.concatenate([kn_ref[:, cols], kr_ref[...]], axis=1)
    s = lax.dot_general(q, k, (((1,), (1,)), ((), ())), preferred_element_type=F32)
    if masked:
        row = lax.broadcasted_iota(jnp.int32, s.shape, 0)
        col = lax.broadcasted_iota(jnp.int32, s.shape, 1)
        s = jnp.where(col <= row, s, MASK_VALUE)
    return s, q, k


def _causal_pairs(nq, by_query):
    if by_query:
        pairs = [(i, j) for i in range(nq) for j in range(i + 1)]
    else:
        pairs = [(i, j) for j in range(nq) for i in range(j, nq)]
    return (jnp.asarray(np.array([p[0] for p in pairs], np.int32)), jnp.asarray(np.array([p[1] for p in pairs], np.int32)),
            len(pairs))


def _mla_fwd(q_full, qr, kv, kr, T, carry=None):
    n = q_full.shape[0]
    nq = n // T
    H = MLA_HEADS
    qi, kj, npairs = _causal_pairs(nq, True)

    def body(qi_ref, kj_ref, qn_ref, qr_ref, kn_ref, v_ref, kr_ref, y_ref, lse_ref, m_ref, l_ref, acc_ref):
        t = pl.program_id(1)
        i, j = qi_ref[t], kj_ref[t]

        @pl.when(j == 0)
        def _():
            m_ref[...] = jnp.full_like(m_ref, MASK_VALUE)
            l_ref[...] = jnp.zeros_like(l_ref)
            acc_ref[...] = jnp.zeros_like(acc_ref)

        def update(masked):
            for hh in range(MLA_HP):
                s, _, _ = _mla_scores(qn_ref, qr_ref, kn_ref, kr_ref, hh, masked)
                m_prev = m_ref[hh]
                m_new = jnp.maximum(m_prev, s.max(-1, keepdims=True))
                p = jnp.exp2((s - m_new[:, :1]) * (MLA_SCALE * LOG2E))
                alpha = jnp.exp2((m_prev - m_new) * (MLA_SCALE * LOG2E))
                l_ref[hh] = alpha * l_ref[hh] + p.sum(-1, keepdims=True)
                acc_ref[hh] = alpha * acc_ref[hh] + jnp.dot(p.astype(BF16), v_ref[:, hh * LANES:(hh + 1) * LANES],
                                                            preferred_element_type=F32)
                m_ref[hh] = m_new

        @pl.when(j < i)
        def _():
            update(False)

        @pl.when(j == i)
        def _():
            update(True)
            for hh in range(MLA_HP):
                y_ref[:, hh * LANES:(hh + 1) * LANES] = (acc_ref[hh] / l_ref[hh]).astype(y_ref.dtype)
                lse_ref[hh] = m_ref[hh] * (MLA_SCALE * LOG2E) + jnp.log2(l_ref[hh])

    G = H // MLA_HP
    qspec = lambda off: pl.BlockSpec((T, MLA_W), lambda h, t, qi, kj, off=off: (qi[t], off + h))
    kspec = lambda off: pl.BlockSpec((T, MLA_W), lambda h, t, qi, kj, off=off: (kj[t], off + h))
    return _call(
        body, "mla_fwd", (G, npairs), 2,
        [qspec(0), qspec(0), kspec(0), kspec(G), pl.BlockSpec((T, LANES), lambda h, t, qi, kj: (kj[t], 0))],
        [pl.BlockSpec((T, MLA_W), lambda h, t, qi, kj: (qi[t], h)),
         pl.BlockSpec((MLA_HP, T, LANES), lambda h, t, qi, kj: (h, qi[t], 0))],
        [SDS((n, H * MLA_V), BF16), SDS((H, n, LANES), F32)], [pltpu.VMEM((MLA_HP, T, LANES), F32)] * 3,
        [qi, kj, q_full, qr, kv, kv, kr], ("parallel", "arbitrary"), carry)


def _mla_delta(dy, y, T):
    n = y.shape[0]
    H = MLA_HEADS

    def body(dy_ref, y_ref, d_ref):
        d = jnp.sum(dy_ref[...].astype(F32) * y_ref[...].astype(F32), axis=-1, keepdims=True)
        d_ref[0] = jnp.broadcast_to(d, (T, LANES))

    spec = pl.BlockSpec((T, LANES), lambda h, i: (i, h))
    return _pcall(body, name="mla_delta", grid=(H, n // T), out_shape=SDS((H, n, LANES), F32), in_specs=[spec, spec],
                  out_specs=pl.BlockSpec((1, T, LANES), lambda h, i: (h, i, 0)),
                  compiler_params=_params(("parallel", "parallel")))(dy, y)


def _mla_bwd_dq(q_full, qr, kv, kr, dy, lse, delta, T, carry=None):
    n = q_full.shape[0]
    nq = n // T
    H = MLA_HEADS

    qi, kj, npairs = _causal_pairs(nq, True)

    def body(qi_ref, kj_ref, qn_ref, qr_ref, kn_ref, v_ref, kr_ref, dy_ref, lse_ref, dl_ref, dqn_ref, dqr_ref, acc_ref):
        t = pl.program_id(1)
        i, j = qi_ref[t], kj_ref[t]

        @pl.when(j == 0)
        def _():
            acc_ref[...] = jnp.zeros_like(acc_ref)

        def update(masked):
            for hh in range(MLA_HP):
                cols = slice(hh * LANES, (hh + 1) * LANES)
                s, _, k = _mla_scores(qn_ref, qr_ref, kn_ref, kr_ref, hh, masked)
                p = jnp.exp2(s * (MLA_SCALE * LOG2E) - lse_ref[hh][:, :1])
                dp = lax.dot_general(dy_ref[:, cols], v_ref[:, cols], (((1,), (1,)), ((), ())), preferred_element_type=F32)
                ds = p * (dp - dl_ref[hh][:, :1])
                acc_ref[hh] += jnp.dot(ds.astype(BF16), k, preferred_element_type=F32)

        @pl.when(j < i)
        def _():
            update(False)

        @pl.when(j == i)
        def _():
            update(True)
            for hh in range(MLA_HP):
                cols = slice(hh * LANES, (hh + 1) * LANES)
                dqn_ref[:, cols] = (acc_ref[hh][:, :LANES] * MLA_SCALE).astype(dqn_ref.dtype)
                dqr_ref[:, cols] = (acc_ref[hh][:, LANES:] * MLA_SCALE).astype(dqr_ref.dtype)

    G = H // MLA_HP
    qspec = lambda off: pl.BlockSpec((T, MLA_W), lambda h, t, qi, kj, off=off: (qi[t], off + h))
    kspec = lambda off: pl.BlockSpec((T, MLA_W), lambda h, t, qi, kj, off=off: (kj[t], off + h))
    stat = pl.BlockSpec((MLA_HP, T, LANES), lambda h, t, qi, kj: (h, qi[t], 0))
    out = pl.BlockSpec((T, MLA_W), lambda h, t, qi, kj: (qi[t], h))
    return _call(
        body, "mla_bwd_dq", (G, npairs), 2,
        [qspec(0), qspec(0), kspec(0), kspec(G), pl.BlockSpec((T, LANES), lambda h, t, qi, kj: (kj[t], 0)), qspec(0), stat, stat],
        [out, out], [SDS((n, H * LANES), BF16), SDS((n, H * LANES), BF16)], [pltpu.VMEM((MLA_HP, T, 2 * LANES), F32)],
        [qi, kj, q_full, qr, kv, kv, kr, dy, lse, delta], ("parallel", "arbitrary"), carry)


def _mla_bwd_dkv(q_full, qr, kv, kr, dy, lse, delta, T, carry=None):
    n = q_full.shape[0]
    nq = n // T
    H = MLA_HEADS

    qi, kj, npairs = _causal_pairs(nq, False)

    def body(qi_ref, kj_ref, qn_ref, qr_ref, kn_ref, v_ref, kr_ref, dy_ref, lse_ref, dl_ref, dkn_ref, dv_ref, dkr_ref,
             dk_acc, dv_acc):
        t = pl.program_id(1)
        i, j = qi_ref[t], kj_ref[t]

        @pl.when(i == j)
        def _():
            dk_acc[...] = jnp.zeros_like(dk_acc)
            dv_acc[...] = jnp.zeros_like(dv_acc)

        def update(masked):
            for hh in range(MLA_HP):
                cols = slice(hh * LANES, (hh + 1) * LANES)
                s, q, _ = _mla_scores(qn_ref, qr_ref, kn_ref, kr_ref, hh, masked)
                p = jnp.exp2(s * (MLA_SCALE * LOG2E) - lse_ref[hh][:, :1])
                dy = dy_ref[:, cols]
                dv_acc[hh] += lax.dot_general(p.astype(BF16), dy, (((0,), (0,)), ((), ())), preferred_element_type=F32)
                dp = lax.dot_general(dy, v_ref[:, cols], (((1,), (1,)), ((), ())), preferred_element_type=F32)
                ds = p * (dp - dl_ref[hh][:, :1])
                dk_acc[hh] += lax.dot_general(ds.astype(BF16), q, (((0,), (0,)), ((), ())), preferred_element_type=F32)

        @pl.when(i == j)
        def _():
            update(True)

        @pl.when(i > j)
        def _():
            update(False)

        @pl.when(i == nq - 1)
        def _():
            for hh in range(MLA_HP):
                cols = slice(hh * LANES, (hh + 1) * LANES)
                dkn_ref[:, cols] = (dk_acc[hh][:, :LANES] * MLA_SCALE).astype(dkn_ref.dtype)
                dv_ref[:, cols] = dv_acc[hh].astype(dv_ref.dtype)
                dkr_ref[hh] = dk_acc[hh][:, LANES:] * MLA_SCALE

    G = H // MLA_HP
    qspec = lambda off: pl.BlockSpec((T, MLA_W), lambda h, t, qi, kj, off=off: (qi[t], off + h))
    kspec = lambda off: pl.BlockSpec((T, MLA_W), lambda h, t, qi, kj, off=off: (kj[t], off + h))
    stat = pl.BlockSpec((MLA_HP, T, LANES), lambda h, t, qi, kj: (h, qi[t], 0))
    out = pl.BlockSpec((T, MLA_W), lambda h, t, qi, kj: (kj[t], h))
    return _call(
        body, "mla_bwd_dkv", (G, npairs), 2,
        [qspec(0), qspec(0), kspec(0), kspec(G), pl.BlockSpec((T, LANES), lambda h, t, qi, kj: (kj[t], 0)), qspec(0), stat, stat],
        [out, out, pl.BlockSpec((MLA_HP, T, LANES), lambda h, t, qi, kj: (h, kj[t], 0))],
        [SDS((n, H * LANES), BF16), SDS((n, H * LANES), BF16), SDS((H, n, LANES), F32)],
        [pltpu.VMEM((MLA_HP, T, 2 * LANES), F32), pltpu.VMEM((MLA_HP, T, LANES), F32)],
        [qi, kj, q_full, qr, kv, kv, kr, dy, lse, delta], ("parallel", "arbitrary"), carry)


def _sum_heads(a, tm):
    H, n, _ = a.shape

    def body(a_ref, o_ref):
        o_ref[...] = jnp.sum(a_ref[...], axis=0)

    return _pcall(body, name="mla_sum_heads", grid=(n // tm,), out_shape=SDS((n, LANES), F32),
                  in_specs=[pl.BlockSpec((H, tm, LANES), lambda i: (0, i, 0))],
                  out_specs=pl.BlockSpec((tm, LANES), lambda i: (i, 0)), compiler_params=_params(("parallel",)))(a)


def _shift_down(x, k):
    row = lax.broadcasted_iota(jnp.int32, x.shape, 0)
    return jnp.where(row >= k, pltpu.roll(x, k, axis=0), 0.0)


def _shift_up(x, k):
    n = x.shape[0]
    row = lax.broadcasted_iota(jnp.int32, x.shape, 0)
    return jnp.where(row < n - k, pltpu.roll(x, n - k, axis=0), 0.0)


def _conv_fwd(up, w, b):
    n, c = up.shape

    def body(u_ref, w_ref, b_ref, o_ref):
        u = u_ref[...].astype(F32)
        wv = w_ref[...]
        o_ref[...] = (b_ref[...] + wv[0:1] * _shift_down(u, 2) + wv[1:2] * _shift_down(u, 1) + wv[2:3] * u).astype(o_ref.dtype)

    return _pcall(body, name="conv_fwd", grid=(c // LANES,), out_shape=SDS((n, c), BF16),
                  in_specs=[pl.BlockSpec((n, LANES), lambda j: (0, j)), pl.BlockSpec((3, LANES), lambda j: (0, j)),
                            pl.BlockSpec((1, LANES), lambda j: (0, j))],
                  out_specs=pl.BlockSpec((n, LANES), lambda j: (0, j)), compiler_params=_params(("parallel",)))(up, w, b)


def _conv_bwd(up, dc, w):
    n, c = up.shape

    def body(u_ref, d_ref, w_ref, du_ref, dw_ref, db_ref):
        u = u_ref[...].astype(F32)
        d = d_ref[...].astype(F32)
        wv = w_ref[...]
        du_ref[...] = (wv[2:3] * d + wv[1:2] * _shift_up(d, 1) + wv[0:1] * _shift_up(d, 2)).astype(du_ref.dtype)
        dw_ref[0:1, :] = jnp.sum(d * _shift_down(u, 2), axis=0, keepdims=True)
        dw_ref[1:2, :] = jnp.sum(d * _shift_down(u, 1), axis=0, keepdims=True)
        dw_ref[2:3, :] = jnp.sum(d * u, axis=0, keepdims=True)
        db_ref[...] = jnp.sum(d, axis=0, keepdims=True)

    col = pl.BlockSpec((n, LANES), lambda j: (0, j))
    return _pcall(body, name="conv_bwd", grid=(c // LANES,),
                  out_shape=[SDS((n, c), BF16), SDS((3, c), F32), SDS((1, c), F32)],
                  in_specs=[col, col, pl.BlockSpec((3, LANES), lambda j: (0, j))],
                  out_specs=[col, pl.BlockSpec((3, LANES), lambda j: (0, j)), pl.BlockSpec((1, LANES), lambda j: (0, j))],
                  compiler_params=_params(("parallel",)))(up, dc, w)


def _adamw_math(w, g, m, v):
    m = ADAM_B1 * m + (1.0 - ADAM_B1) * g
    v = ADAM_B2 * v + (1.0 - ADAM_B2) * jnp.square(g)
    m_hat = m / (1.0 - ADAM_B1 ** ADAM_STEP)
    v_hat = v / (1.0 - ADAM_B2 ** ADAM_STEP)
    delta = -ADAM_LR * (m_hat / (jnp.sqrt(v_hat) + ADAM_EPS) + ADAM_WD * w)
    return delta, m, v


def _adamw(w, g, m, v, name):
    L, r, c = w.shape
    tr = _rows_tile(r, c, 1 << 20, 8)

    def body(w_ref, g_ref, m_ref, v_ref, d_ref, nm_ref, nv_ref):
        d, nm, nv = _adamw_math(w_ref[...], g_ref[...], m_ref[...], v_ref[...])
        d_ref[...] = d
        nm_ref[...] = nm
        nv_ref[...] = nv

    spec = pl.BlockSpec((1, tr, c), lambda l, i: (l, i, 0))
    return _pcall(body, name=name, grid=(L, r // tr), out_shape=[SDS(w.shape, F32)] * 3, in_specs=[spec] * 4,
                  out_specs=[spec] * 3, compiler_params=_params(("parallel", "parallel")))(w, g, m, v)


def _coords():
    return lax.axis_index("x"), lax.axis_index("y"), lax.axis_index("c")


def _other_chips(x, y):
    return [(1 - x, y), (x, 1 - y), (1 - x, 1 - y)]


HBM_SPEC = pl.BlockSpec(memory_space=pltpu.HBM)


def _half(c, rows):
    return pl.ds(pl.multiple_of(c * rows, 16), rows)


def _rows_tile(rows, cols, budget_bytes=2 << 20, align=16):
    t = (min(max(align, budget_bytes // (4 * cols)), rows) // align) * align
    while t >= align:
        if rows % t == 0:
            return t
        t -= align
    return rows


def _scalar(v):
    return jnp.reshape(jnp.asarray(v, jnp.int32), (1,))


def _cast_into_slot(w3, layer, slot, name):
    _, R, C = w3.shape
    tr = _rows_tile(R, C)

    def body(s_ref, w_ref, o_ref):
        o_ref[0] = w_ref[0].astype(BF16)

    gs = pltpu.PrefetchScalarGridSpec(
        num_scalar_prefetch=1, grid=(R // tr,),
        in_specs=[pl.BlockSpec((1, tr, C), lambda i, s: (layer, i, 0))],
        out_specs=pl.BlockSpec((1, tr, C), lambda i, s: (s[0], i, 0)))
    return _pcall(body, name=name, grid_spec=gs, out_shape=SDS((N_CHIPS, R, C), BF16),
                  compiler_params=_params(("arbitrary",)))(_scalar(slot), w3)


class _Stage:
    def __init__(self, ins, out_shapes, aliases, n_sems, copies):
        self.ins, self.out_shapes, self.aliases, self.n_sems, self.copies = list(ins), out_shapes, aliases, n_sems, copies

    def start(self, ins, outs, send_sems, recv_sems):
        for cp in self.copies(ins, outs, send_sems, recv_sems)[0]:
            cp.start()

    def finish(self, ins, outs, send_sems, recv_sems):
        sends, arrivals = self.copies(ins, outs, send_sems, recv_sems)
        for cp in arrivals:
            cp.wait_recv()
        for cp in sends:
            cp.wait_send()


class _SemsFrom:
    def __init__(self, sems, base):
        self.sems, self.base = sems, base

    @property
    def at(self):
        return self

    def __getitem__(self, k):
        return self.sems.at[self.base + k]


def _both(a, b):
    if a is None or b is None:
        return a if b is None else b
    na, nao = len(a.ins), len(a.out_shapes)

    def copies(ins, outs, send_sems, recv_sems):
        sa, aa = a.copies(ins[:na], outs[:nao], send_sems, recv_sems)
        sb, ab = b.copies(ins[na:], outs[nao:], _SemsFrom(send_sems, a.n_sems), _SemsFrom(recv_sems, a.n_sems))
        return sa + sb, aa + ab

    aliases = {**a.aliases, **{na + i: nao + o for i, o in b.aliases.items()}}
    return _Stage(a.ins + b.ins, list(a.out_shapes) + list(b.out_shapes), aliases, a.n_sems + b.n_sems, copies)


def _remote(src, dst, send_sems, recv_sems, k, to):
    return pltpu.make_async_remote_copy(src_ref=src, dst_ref=dst, send_sem=send_sems.at[k], recv_sem=recv_sems.at[k],
                                        device_id=to, device_id_type=MESH)


def _gather_stages(bufs):
    n = len(bufs)
    shapes = [SDS(b.shape, b.dtype) for b in bufs]
    same = {i: i for i in range(n)}

    def over_ici(ins, outs, send_sems, recv_sems):
        x, y, c = _coords()
        blk = lambda w, chip: outs[w].at[chip, _half(c, outs[w].shape[1] // 2), :]
        sends, arrivals = [], []
        for w in range(n):
            for j, (px, py) in enumerate(_other_chips(x, y)):
                sends.append(_remote(blk(w, 2 * x + y), blk(w, 2 * x + y), send_sems, recv_sems, 3 * w + j, (px, py, c)))
                arrivals.append(_remote(blk(w, 2 * px + py), blk(w, 2 * px + py), send_sems, recv_sems, 3 * w + j, (px, py, c)))
        return sends, arrivals

    def to_sibling(ins, outs, send_sems, recv_sems):
        x, y, c = _coords()
        blk = lambda w, chip, half: outs[w].at[chip, _half(half, outs[w].shape[1] // 2), :]
        sends, arrivals = [], []
        for w in range(n):
            for j, (px, py) in enumerate(_other_chips(x, y)):
                k = 2 * px + py
                sends.append(_remote(blk(w, k, c), blk(w, k, c), send_sems, recv_sems, 3 * w + j, (x, y, 1 - c)))
                arrivals.append(_remote(blk(w, k, 1 - c), blk(w, k, 1 - c), send_sems, recv_sems, 3 * w + j, (x, y, 1 - c)))
        return sends, arrivals

    return (lambda b: _Stage(b, shapes, same, 3 * n, over_ici)), (lambda b: _Stage(b, shapes, same, 3 * n, to_sibling))


def _swap_stage(gs_):
    n = len(gs_)

    def copies(ins, outs, send_sems, recv_sems):
        x, y, c = _coords()
        cps = [_remote(ins[w].at[:, _half(1 - c, ins[w].shape[1] // 2), :], outs[w], send_sems, recv_sems, w, (x, y, 1 - c))
               for w in range(n)]
        return cps, cps

    return _Stage(gs_, [SDS((N_CHIPS, g.shape[1] // 2, g.shape[2]), g.dtype) for g in gs_], {}, n, copies)


def _scatter_stage(ps):
    n = len(ps)

    def copies(ins, outs, send_sems, recv_sems):
        x, y, c = _coords()
        cps = [_remote(ins[w].at[2 * px + py], outs[w].at[j], send_sems, recv_sems, 3 * w + j, (px, py, c))
               for w in range(n) for j, (px, py) in enumerate(_other_chips(x, y))]
        return cps, cps

    return _Stage(ps, [SDS((3,) + p.shape[1:], p.dtype) for p in ps], {}, 3 * n, copies)


def _join_stage(bufs, layers):
    n = len(bufs)
    slots = [(w, l) for w in range(n) for l in layers[w]]

    def copies(ins, outs, send_sems, recv_sems):
        x, y, c = _coords()
        blk = lambda w, l, half: outs[w].at[l, _half(half, outs[w].shape[1] // 2), :]
        sends = [_remote(blk(w, l, c), blk(w, l, c), send_sems, recv_sems, k, (x, y, 1 - c)) for k, (w, l) in enumerate(slots)]
        arrivals = [_remote(blk(w, l, 1 - c), blk(w, l, 1 - c), send_sems, recv_sems, k, (x, y, 1 - c))
                    for k, (w, l) in enumerate(slots)]
        return sends, arrivals

    return _Stage(bufs, [SDS(b.shape, b.dtype) for b in bufs], {i: i for i in range(n)}, len(slots), copies)


def _stage_scratch(stage):
    return [pltpu.SemaphoreType.DMA((stage.n_sems,)), pltpu.SemaphoreType.DMA((stage.n_sems,))]


def _run_stage(stage, name):
    n_in, n_out = len(stage.ins), len(stage.out_shapes)

    def body(*refs):
        ins, outs, send_sems, recv_sems = refs[:n_in], refs[n_in:n_in + n_out], refs[n_in + n_out], refs[n_in + n_out + 1]
        stage.start(ins, outs, send_sems, recv_sems)
        stage.finish(ins, outs, send_sems, recv_sems)

    return _pcall(body, name=name, out_shape=stage.out_shapes, in_specs=[HBM_SPEC] * n_in, out_specs=[HBM_SPEC] * n_out,
                  input_output_aliases=stage.aliases, scratch_shapes=_stage_scratch(stage))(*stage.ins)


def _call(body, name, grid, n_prefetch, in_specs, out_specs, out_shape, scratch, operands, semantics, carry=None):
    n_in, n_out, n_sc = len(in_specs), len(out_specs), len(scratch)
    if carry is None:
        gs = pltpu.PrefetchScalarGridSpec(num_scalar_prefetch=n_prefetch, grid=grid, in_specs=in_specs, out_specs=out_specs,
                                          scratch_shapes=scratch)
        res = _pcall(body, name=name, grid_spec=gs, out_shape=out_shape, compiler_params=_params(semantics))(*operands)
        return list(res), []
    s_in, s_out = len(carry.ins), len(carry.out_shapes)

    def carrying(*refs):
        o = n_prefetch
        pre, ins = refs[:o], refs[o:o + n_in]
        o += n_in
        sins = refs[o:o + s_in]
        o += s_in
        outs = refs[o:o + n_out]
        o += n_out
        souts = refs[o:o + s_out]
        o += s_out
        sc, send_sems, recv_sems = refs[o:o + n_sc], refs[o + n_sc], refs[o + n_sc + 1]
        first = functools.reduce(jnp.logical_and, [pl.program_id(a) == 0 for a in range(len(grid))])
        last = functools.reduce(jnp.logical_and, [pl.program_id(a) == g - 1 for a, g in enumerate(grid)])

        @pl.when(first)
        def _():
            carry.start(sins, souts, send_sems, recv_sems)

        body(*pre, *ins, *outs, *sc)

        @pl.when(last)
        def _():
            carry.finish(sins, souts, send_sems, recv_sems)

    gs = pltpu.PrefetchScalarGridSpec(
        num_scalar_prefetch=n_prefetch, grid=grid, in_specs=list(in_specs) + [HBM_SPEC] * s_in,
        out_specs=list(out_specs) + [HBM_SPEC] * s_out, scratch_shapes=list(scratch) + _stage_scratch(carry))
    aliases = {n_prefetch + n_in + a: n_out + b for a, b in carry.aliases.items()}
    res = _pcall(carrying, name=name, grid_spec=gs, out_shape=list(out_shape) + list(carry.out_shapes),
                 input_output_aliases=aliases, compiler_params=_params(("arbitrary",) * len(grid)))(*operands, *carry.ins)
    return list(res[:n_out]), list(res[n_out:])


def _all_reduce_small(v):
    n = v.shape[0]

    def body(v_ref, out_ref, slots, send_sems, recv_sems):
        x, y, c = _coords()
        me = 4 * x + 2 * y + c
        cps = []
        for r in range(1, 8):
            t = (me + r) % 8
            cp = pltpu.make_async_remote_copy(src_ref=v_ref, dst_ref=slots.at[me], send_sem=send_sems.at[r - 1],
                                              recv_sem=recv_sems.at[me], device_id=(t // 4, (t // 2) % 2, t % 2),
                                              device_id_type=MESH)
            cp.start()
            cps.append(cp)
        slots[me] = v_ref[...]
        for r in range(1, 8):
            s = (me + r) % 8
            pltpu.make_async_remote_copy(src_ref=v_ref, dst_ref=slots.at[s], send_sem=send_sems.at[r - 1],
                                         recv_sem=recv_sems.at[s], device_id=(x, y, c), device_id_type=MESH).wait_recv()
        for cp in cps:
            cp.wait_send()
        acc = slots[0]
        for d in range(1, 8):
            acc = acc + slots[d]
        out_ref[...] = acc

    return _pcall(body, name="all_reduce_small", out_shape=SDS((n, LANES), F32),
                  in_specs=[pl.BlockSpec(memory_space=pltpu.VMEM)], out_specs=pl.BlockSpec(memory_space=pltpu.VMEM),
                  scratch_shapes=[pltpu.VMEM((8, n, LANES), F32), pltpu.SemaphoreType.DMA((7,)), pltpu.SemaphoreType.DMA((8,))],
                  compiler_params=pltpu.CompilerParams(vmem_limit_bytes=VMEM_LIMIT))(v)


def _add_half(g, recv, c, name):
    _, R, C = g.shape
    rows = R // 2
    tr = _rows_tile(rows, C, 1 << 20)
    nb = rows // tr

    def body(s_ref, g_ref, r_ref, o32_ref, o16_ref):
        s = g_ref[...] + r_ref[...]
        o32_ref[...] = s
        o16_ref[...] = s.astype(BF16)

    blk = lambda k, i, s: (k, i, 0)
    gs = pltpu.PrefetchScalarGridSpec(
        num_scalar_prefetch=1, grid=(N_CHIPS, nb),
        in_specs=[pl.BlockSpec((1, tr, C), lambda k, i, s: (k, s[0] * nb + i, 0)), pl.BlockSpec((1, tr, C), blk)],
        out_specs=[pl.BlockSpec((1, tr, C), blk), pl.BlockSpec((1, tr, C), blk)])
    return _pcall(body, name=name, grid_spec=gs, out_shape=[SDS((N_CHIPS, rows, C), F32), SDS((N_CHIPS, rows, C), BF16)],
                  compiler_params=_params(("arbitrary", "arbitrary")))(_scalar(c), g, recv)


def _sum_into(p32, arrived, chip, c, layer, n_layers, prev, name):
    _, rows, C = p32.shape
    tr = _rows_tile(rows, C, 1 << 20)
    nb = rows // tr

    def body(chip_ref, c_ref, p_ref, a_ref, *rest):
        o_ref = rest[-1]
        o_ref[0] = ((p_ref[0] + a_ref[0].astype(F32)) + a_ref[1].astype(F32)) + a_ref[2].astype(F32)

    in_specs = [pl.BlockSpec((1, tr, C), lambda i, chip_ref, c_ref: (chip_ref[0], i, 0)),
                pl.BlockSpec((3, tr, C), lambda i, chip_ref, c_ref: (0, i, 0))]
    ins = [p32, arrived]
    aliases = {}
    if prev is not None:
        in_specs.append(pl.BlockSpec(memory_space=pl.ANY))
        ins.append(prev)
        aliases = {4: 0}
    gs = pltpu.PrefetchScalarGridSpec(
        num_scalar_prefetch=2, grid=(nb,), in_specs=in_specs,
        out_specs=pl.BlockSpec((1, tr, C), lambda i, chip_ref, c_ref: (layer, c_ref[0] * nb + i, 0)))
    return _pcall(body, name=name, grid_spec=gs, out_shape=SDS((n_layers, 2 * rows, C), F32), input_output_aliases=aliases,
                  compiler_params=_params(("arbitrary",)))(_scalar(chip), _scalar(c), *ins)


def _h_layout(D):
    G = N_BRANCHES * D
    off, o = {}, 0
    for name, w in [("g", G), ("qa", A_Q), ("cq", MLA_Q_RANK), ("ckv", MLA_KV_RANK), ("hu", SGU_WIDTH), ("hv", SGU_WIDTH),
                    ("ka", A_KV), ("va", A_KV), ("kr", LANES)]:
        assert o % w == 0, (name, o, w)
        off[name] = (o, w)
        o += w
    off["total"] = -(-o // 512) * 512
    return off


def _perm_w_in(w, lay):
    s = np.cumsum([0, A_Q, A_KV, A_KV, MLA_Q_RANK, MLA_KV_RANK, MLA_ROPE, SGU_WIDTH, SGU_WIDTH])
    qa, ka, va, cq, ckv, kr, hu, hv = [w[:, s[i]:s[i + 1]] for i in range(8)]
    g = w[:, s[8]:]
    pad = jnp.zeros((w.shape[0], lay["total"] - lay["kr"][0] - MLA_ROPE), w.dtype)
    return jnp.concatenate([g, qa, cq, ckv, hu, hv, ka, va, kr, pad], axis=1)


def _unperm_w_in(wp, lay, D):
    take = lambda n, width=None: wp[:, lay[n][0]:lay[n][0] + (width or lay[n][1])]
    return jnp.concatenate([take("qa"), take("ka"), take("va"), take("cq"), take("ckv"), take("kr", MLA_ROPE), take("hu"),
                            take("hv"), take("g")], axis=1)


def _perm_w_uq(w):
    r = w.shape[0]
    w3 = w.reshape(r, MLA_HEADS, MLA_NOPE + MLA_ROPE)
    nope = w3[:, :, :MLA_NOPE].reshape(r, MLA_HEADS * MLA_NOPE)
    rope = jnp.pad(w3[:, :, MLA_NOPE:], ((0, 0), (0, 0), (0, LANES - MLA_ROPE))).reshape(r, MLA_HEADS * LANES)
    return jnp.concatenate([nope, rope], axis=1)


def _unperm_w_uq(wp):
    r = wp.shape[0]
    nope = wp[:, :MLA_HEADS * MLA_NOPE].reshape(r, MLA_HEADS, MLA_NOPE)
    rope = wp[:, MLA_HEADS * MLA_NOPE:].reshape(r, MLA_HEADS, LANES)[:, :, :MLA_ROPE]
    return jnp.concatenate([nope, rope], axis=2).reshape(r, MLA_HEADS * (MLA_NOPE + MLA_ROPE))


def _perm_w_ukv(w):
    r = w.shape[0]
    w3 = w.reshape(r, MLA_HEADS, MLA_NOPE + MLA_V)
    return jnp.concatenate([w3[:, :, :MLA_NOPE].reshape(r, -1), w3[:, :, MLA_NOPE:].reshape(r, -1)], axis=1)


def _unperm_w_ukv(wp):
    r = wp.shape[0]
    k = wp[:, :MLA_HEADS * MLA_NOPE].reshape(r, MLA_HEADS, MLA_NOPE)
    v = wp[:, MLA_HEADS * MLA_NOPE:].reshape(r, MLA_HEADS, MLA_V)
    return jnp.concatenate([k, v], axis=2).reshape(r, -1)


def _col_chunks(g):
    r, c4 = g.shape
    return jnp.transpose(g.reshape(r, N_CHIPS, c4 // N_CHIPS), (1, 0, 2))


def kernel(x, positions, w_in, b_gate, sinks, q_norm_g, kv_norm_g, w_uq, w_ukv, sgu_ln_g, sgu_ln_b, sgu_w, sgu_b, w_proj_a, w_proj_b, w_proj_c, w_o, ln1_g, ln1_b, w_up, conv_w, conv_b, w_down, ln2_g, ln2_b, loss_target, m_w_in, m_b_gate, m_sinks, m_q_norm_g, m_kv_norm_g, m_w_uq, m_w_ukv, m_sgu_ln_g, m_sgu_ln_b, m_sgu_w, m_sgu_b, m_w_proj_a, m_w_proj_b, m_w_proj_c, m_w_o, m_ln1_g, m_ln1_b, m_w_up, m_conv_w, m_conv_b, m_w_down, m_ln2_g, m_ln2_b, v_w_in, v_b_gate, v_sinks, v_q_norm_g, v_kv_norm_g, v_w_uq, v_w_ukv, v_sgu_ln_g, v_sgu_ln_b, v_sgu_w, v_sgu_b, v_w_proj_a, v_w_proj_b, v_w_proj_c, v_w_o, v_ln1_g, v_ln1_b, v_w_up, v_conv_w, v_conv_b, v_w_down, v_ln2_g, v_ln2_b):
    a = locals()
    W = {k: a[k] for k in WEIGHTS}
    Mo = {k: a["m_" + k] for k in WEIGHTS}
    Vo = {k: a["v_" + k] for k in WEIGHTS}
    S, D = x.shape[1], x.shape[2]
    FF2 = w_up.shape[2] * N_CHIPS
    FF = FF2 // 2
    L = DEPTH
    lay = _h_layout(D)
    NP = lay["total"]
    cx, cy, cc = _coords()
    chip = 2 * cx + cy
    T = _tile(S, 512)
    TM = _tile(S, 256, 16)
    TMW = _tile(S, 64, 16)

    shards = {k: tuple(W[k].shape) for k in BIG}
    full = {k: [None] * L for k in BIG}
    own = {(l, k): _cast_into_slot(W[k], l, chip, f"cast_{k}_l{l}") for l in range(L) for k in BIG}

    on_way = {"pairs": [], "arrived": []}

    def lay_out(pairs, gathered):
        for (l_, k), g in zip(pairs, gathered):
            _, r, c_ = shards[k]
            full[k][l_] = g.reshape(N_CHIPS * r, c_) if k in ROW_SHARDED else jnp.transpose(g, (1, 0, 2)).reshape(r, N_CHIPS * c_)

    def gather_behind(pairs):
        to_sib = _gather_stages(on_way["arrived"])[1](on_way["arrived"]) if on_way["pairs"] else None
        bufs = [own[p_] for p_ in pairs]
        return _both(to_sib, _gather_stages(bufs)[0](bufs) if pairs else None)

    def gathered_behind(pairs, outs):
        n_done = len(on_way["pairs"])
        lay_out(on_way["pairs"], outs[:n_done])
        on_way["pairs"], on_way["arrived"] = list(pairs), list(outs[n_done:])

    def mm_behind(a_, b_, mode, dt, name, pairs):
        stage = gather_behind(pairs)
        if stage is None:
            return _mm(a_, b_, mode, dt, name)
        out, outs = _mm(a_, b_, mode, dt, name, carry=stage)
        gathered_behind(pairs, outs)
        return out

    MIX = ["w_uq", "w_ukv", "w_proj_a", "w_proj_b", "w_proj_c", "w_o"]
    nxt = lambda l, names: [(l + 1, k) for k in names] if l + 1 < L else []
    first = [(0, "w_in")]
    gathered_behind(first, _run_stage(gather_behind(first), "gather_ici_w_in_l0"))
    gathered_behind([], _run_stage(gather_behind([]), "gather_sibling_w_in_l0"))

    small_sharded_full = {k: tuple(W[k].shape[:-1]) + (W[k].shape[-1] * N_CHIPS,) for k in SMALL_SHARDED}
    placed = []
    for k in ("b_gate", "conv_w"):
        z = jnp.zeros(small_sharded_full[k], F32)
        z = lax.dynamic_update_slice_in_dim(z, W[k], chip * W[k].shape[-1], axis=-1)
        placed.append(jnp.where(cc == 0, z, 0.0).reshape(-1))
    pv = jnp.concatenate(placed)
    n_pv = pv.shape[0]
    pv = jnp.pad(pv, (0, -n_pv % (8 * LANES))).reshape(-1, LANES)
    pv = _all_reduce_small(pv).reshape(-1)
    nb_ = int(np.prod(small_sharded_full["b_gate"]))
    b_gate_full = pv[:nb_].reshape(small_sharded_full["b_gate"])
    conv_w_full = pv[nb_:n_pv].reshape(small_sharded_full["conv_w"])

    inv_freq = ROPE_THETA ** (-jnp.arange(0, MLA_ROPE, 2, dtype=F32) / MLA_ROPE)
    ang = positions[0].astype(F32)[:, None] * inv_freq
    cos, sin = jnp.cos(ang), jnp.sin(ang)
    cos_t = jnp.concatenate([cos, cos, jnp.ones((S, LANES - MLA_ROPE), F32)], axis=1)
    sin_t = jnp.concatenate([sin, sin, jnp.zeros((S, LANES - MLA_ROPE), F32)], axis=1)

    row = lambda v: v.reshape(1, -1)
    cb = lambda name: lay[name][0] // lay[name][1]

    xs = x[0]
    saved = []
    for l in range(L):
        p = dict(
            w_in=_perm_w_in(full["w_in"][l], lay),
            sinks=row(sinks[l]), qg=row(q_norm_g[l]), kvg=row(kv_norm_g[l]), sg=row(sgu_ln_g[l]), sb=row(sgu_ln_b[l]),
            sw=sgu_w[l], sb3=sgu_b[l].reshape(SGU_GROUPS, SGU_CHUNK, 1),
            bg=b_gate_full[l], l1g=row(ln1_g[l]), l1b=row(ln1_b[l]), cw=conv_w_full[l], cbias=row(conv_b[l]),
            l2g=row(ln2_g[l]), l2b=row(ln2_b[l]))
        if l == 0:
            def fn_cast(i, rows, ps):
                return [rows[0]], []
            (xb,), _ = _rowwise(fn_cast, [_whole(xs)], [], [(D, BF16)], [], TM, "cast_x")
        own_mix = [(l, k) for k in MIX]
        own_up = [(l, "w_up")] if l == 0 else []
        own_down = [(l, "w_down")] if l == 0 else []
        h = mm_behind(xb, p["w_in"], "nn", BF16, "mm_h", own_mix)
        (y_a,), outs = _swa_fwd(h, cb("qa"), cb("ka"), cb("va"), p["sinks"], carry=gather_behind(own_up))
        gathered_behind(own_up, outs)
        p.update(w_uq=_perm_w_uq(full["w_uq"][l]), w_ukv=_perm_w_ukv(full["w_ukv"][l]), w_pa=full["w_proj_a"][l],
                 w_pb=full["w_proj_b"][l], w_pc=full["w_proj_c"][l], w_o=full["w_o"][l])
        def fn_rms(i, rows, ps):
            return [_rms_norm(rows[0].astype(F32), ps[0]), _rms_norm(rows[1].astype(F32), ps[1])], []
        (cqn, ckvn), _ = _rowwise(fn_rms, [(h, MLA_Q_RANK, cb("cq")), (h, MLA_KV_RANK, cb("ckv"))], [p["qg"], p["kvg"]],
                                  [(MLA_Q_RANK, BF16), (MLA_KV_RANK, BF16)], [], TM, "mla_rms")
        q_full = _mm(cqn, p["w_uq"], "nn", BF16, "mm_q")
        kv = _mm(ckvn, p["w_ukv"], "nn", BF16, "mm_kv")
        qr, kr = _rope_call(q_full, MLA_HEADS * LANES, 1, h, LANES, cb("kr"), cos_t, sin_t, 1.0, TM, "rope_fwd")
        behind_mla = own_down + nxt(l, ["w_in"])
        (y_b, lse), outs = _mla_fwd(q_full, qr, kv, kr, T, carry=gather_behind(behind_mla))
        gathered_behind(behind_mla, outs)
        y_c = _sgu_fwd(h, cb("hu"), cb("hv"), p["sg"], p["sb"], p["sw"], p["sb3"])
        pa = _mm(y_a, p["w_pa"], "nn", F32, "mm_pa")
        pb = _mm(y_b, p["w_pb"], "nn", F32, "mm_pb")
        pc = _mm(y_c, p["w_pc"], "nn", F32, "mm_pc")

        def merge_math(pa_, pb_, pc_, g_, b0, b1, b2):
            out = 0.0
            for br, (pp, bb) in enumerate(zip((pa_, pb_, pc_), (b0, b1, b2))):
                gate = jax.nn.sigmoid(g_[:, br * D:(br + 1) * D].astype(F32) + bb)
                out = out + gate * pp
            return out

        def fn_merge(i, rows, ps):
            bgv = ps[0]
            return [merge_math(rows[0], rows[1], rows[2], rows[3], bgv[0:1], bgv[1:2], bgv[2:3])], []
        (merged,), _ = _rowwise(fn_merge, [_whole(pa), _whole(pb), _whole(pc), (h, N_BRANCHES * D, cb("g"))], [p["bg"]],
                                [(D, BF16)], [], TMW, "merge_fwd")
        o = _mm(merged, p["w_o"], "nn", F32, "mm_o")

        def ln_res_math(x_, o_, g_, b_):
            return _layer_norm(DN_ALPHA * x_ + o_, g_, b_)

        def fn_ln(i, rows, ps):
            y = ln_res_math(rows[0], rows[1], ps[0], ps[1])
            return [y, y], []
        (x1, x1b), _ = _rowwise(fn_ln, [_whole(xs), _whole(o)], [p["l1g"], p["l1b"]], [(D, F32), (D, BF16)], [], TM, "ln1_fwd")
        p.update(w_up=full["w_up"][l])
        up = mm_behind(x1b, p["w_up"], "nn", BF16, "mm_up", nxt(l, ["w_up"]))
        p.update(w_down=full["w_down"][l])
        cv_ = _conv_fwd(up, p["cw"], p["cbias"])

        def glu_math(cg, cvv):
            return jax.nn.silu(cg.astype(F32)) * cvv.astype(F32)

        def fn_glu(i, rows, ps):
            return [glu_math(rows[0], rows[1])], []
        (act,), _ = _rowwise(fn_glu, [(cv_, FF, 0), (cv_, FF, 1)], [], [(FF, BF16)], [], TMW, "glu_fwd")
        dn = mm_behind(act, p["w_down"], "nn", F32, "mm_down", nxt(l, ["w_down"]))
        (x2, x2b), _ = _rowwise(fn_ln, [_whole(x1), _whole(dn)], [p["l2g"], p["l2b"]], [(D, F32), (D, BF16)], [], TM, "ln2_fwd")
        saved.append(dict(p=p, x0=xs, x0b=xb, h=h, y_a=y_a, cqn=cqn, ckvn=ckvn, q_full=q_full, kv=kv, qr=qr, kr=kr, y_b=y_b,
                          lse=lse, y_c=y_c, pa=pa, pb=pb, pc=pc, merged=merged, o=o, x1=x1, x1b=x1b, up=up, cv=cv_, act=act,
                          dn=dn))
        xs, xb = x2, x2b

    def fn_loss(i, rows, ps):
        diff = rows[0] - rows[1]
        part = jnp.sum(jnp.mean(jnp.square(diff), axis=-1, keepdims=True), axis=0, keepdims=True)
        return [diff * (1.0 / D)], [jnp.broadcast_to(part, (8, LANES))]
    (dx,), (loss_acc,) = _rowwise(fn_loss, [_whole(xs), _whole(loss_target[0])], [], [(D, F32)], [(8, LANES)], TM, "loss")
    loss = lax.psum(0.5 * loss_acc[0, 0], ("x", "y", "c"))

    gbig = {k: [None] * L for k in BIG}
    gsmall = {k: [None] * L for k in SMALL}
    reduced = {}
    pending = None
    for l in reversed(range(L)):
        s = saved[l]
        p = s["p"]

        def fn_ln_bwd(i, rows, ps):
            _, vjp = jax.vjp(ln_res_math, rows[0], rows[1], ps[0], ps[1])
            dx_, do_, dg_, db_ = vjp(rows[2])
            return [dx_, do_], [dg_, db_]
        (dx1_res, ddn), (g_l2g, g_l2b) = _rowwise(fn_ln_bwd, [_whole(s["x1"]), _whole(s["dn"]), _whole(dx)], [p["l2g"], p["l2b"]],
                                                  [(D, F32), (D, BF16)], [(1, D), (1, D)], TM, "ln2_bwd")
        gsmall["ln2_g"][l], gsmall["ln2_b"][l] = g_l2g, g_l2b
        row_chunks = lambda g: g.reshape(N_CHIPS, g.shape[0] // N_CHIPS, g.shape[1])
        gbig["w_down"][l] = row_chunks(_mm_tn(s["act"], ddn, F32, "mm_dw_down"))
        dact = _mm(ddn, p["w_down"], "nt", BF16, "mm_dact")

        def fn_glu_bwd(i, rows, ps):
            _, vjp = jax.vjp(glu_math, rows[0], rows[1])
            dcg, dcv = vjp(rows[2].astype(F32))
            return [jnp.concatenate([dcg, dcv], axis=1)], []
        (dc,), _ = _rowwise(fn_glu_bwd, [(s["cv"], FF, 0), (s["cv"], FF, 1), _whole(dact)], [], [(FF2, BF16)], [], TMW, "glu_bwd")
        dup, g_cw, g_cb = _conv_bwd(s["up"], dc, p["cw"])
        gsmall["conv_w"][l], gsmall["conv_b"][l] = g_cw, g_cb
        if pending is None:
            gbig["w_up"][l] = _mm_tn(s["x1b"], dup, F32, "mm_dw_up", tm=1024, col_chunks=N_CHIPS)
        else:
            gbig["w_up"][l], from_sibling = _mm_tn(s["x1b"], dup, F32, "mm_dw_up", tm=1024, col_chunks=N_CHIPS,
                                                carry=_swap_stage(pending))
            partial = [_add_half(g, r, cc, f"rs_add_{k}_l{l + 1}") for k, g, r in zip(BIG, pending, from_sibling)]
        dx1 = _mm(dup, p["w_up"], "nt", F32, "mm_dx1", add=dx1_res)
        (dx0_res, do_), (g_l1g, g_l1b) = _rowwise(fn_ln_bwd, [_whole(s["x0"]), _whole(s["o"]), _whole(dx1)], [p["l1g"], p["l1b"]],
                                                  [(D, F32), (D, BF16)], [(1, D), (1, D)], TM, "ln1_bwd")
        gsmall["ln1_g"][l], gsmall["ln1_b"][l] = g_l1g, g_l1b
        gbig["w_o"][l] = row_chunks(_mm_tn(s["merged"], do_, F32, "mm_dw_o"))
        dmerged = _mm(do_, p["w_o"], "nt", F32, "mm_dmerged")

        def fn_merge_bwd(i, rows, ps):
            bgv = ps[0]
            _, vjp = jax.vjp(merge_math, rows[0], rows[1], rows[2], rows[3], bgv[0:1], bgv[1:2], bgv[2:3])
            dpa, dpb, dpc, dg_, db0, db1, db2 = vjp(rows[4])
            return [dpa, dpb, dpc, dg_], [db0, db1, db2]
        (dpa, dpb, dpc, dgl), (db0, db1, db2) = _rowwise(
            fn_merge_bwd, [_whole(s["pa"]), _whole(s["pb"]), _whole(s["pc"]), (s["h"], N_BRANCHES * D, cb("g")), _whole(dmerged)],
            [p["bg"]], [(D, BF16), (D, BF16), (D, BF16), (N_BRANCHES * D, BF16)], [(1, D)] * 3, TMW, "merge_bwd")
        gsmall["b_gate"][l] = jnp.concatenate([db0, db1, db2], axis=0)
        gbig["w_proj_a"][l] = _mm_tn(s["y_a"], dpa, F32, "mm_dw_pa", col_chunks=N_CHIPS)
        gbig["w_proj_b"][l] = row_chunks(_mm_tn(s["y_b"], dpb, F32, "mm_dw_pb"))
        gbig["w_proj_c"][l] = _mm_tn(s["y_c"], dpc, F32, "mm_dw_pc", col_chunks=N_CHIPS)
        dy_a = _mm(dpa, p["w_pa"], "nt", BF16, "mm_dy_a")
        dy_b = _mm(dpb, p["w_pb"], "nt", BF16, "mm_dy_b")
        dy_c = _mm(dpc, p["w_pc"], "nt", BF16, "mm_dy_c")
        dh_c, g_sg, g_sb, g_sw, g_sb3 = _sgu_bwd(s["h"], cb("hu"), cb("hv"), dy_c, p["sg"], p["sb"], p["sw"], p["sb3"])
        gsmall["sgu_ln_g"][l], gsmall["sgu_ln_b"][l], gsmall["sgu_w"][l] = g_sg, g_sb, g_sw
        gsmall["sgu_b"][l] = g_sb3.reshape(SGU_GROUPS, SGU_CHUNK)
        part_x = ["w_up", "w_down"]
        part_z = ["w_in"] if l == 0 else []
        part_y = [k for k in BIG if k not in part_x + part_z]
        scatter_of = lambda names: None if pending is None else _scatter_stage([partial[BIG.index(k)][1] for k in names])
        hide_early = l == 0
        early = [gbig[k][l] for k in EARLY]
        n_early = len(EARLY) if hide_early else 0
        (dqa, dka, dva, g_sinks), landed_ = _swa_bwd(s["h"], cb("qa"), cb("ka"), cb("va"), p["sinks"], dy_a,
                                                     carry=_both(_swap_stage(early) if hide_early else None, scatter_of(part_x)))
        early_from_sibling, arrived_x = landed_[:n_early], landed_[n_early:]
        gsmall["sinks"][l] = g_sinks
        if hide_early:
            early_partial = [_add_half(g, r, cc, f"rs_add_{k}_l{l}") for k, g, r in zip(EARLY, early, early_from_sibling)]
        delta = _mla_delta(dy_b, s["y_b"], T)
        (dqn, dqr), arrived_y = _mla_bwd_dq(s["q_full"], s["qr"], s["kv"], s["kr"], dy_b, s["lse"], delta, T,
                                            carry=scatter_of(part_y))
        if pending is not None:
            arrived = {**dict(zip(part_x, arrived_x)), **dict(zip(part_y, arrived_y))}
            for k, (p32, _) in zip(BIG, partial):
                if k in arrived:
                    reduced[k] = _sum_into(p32, arrived[k], chip, cc, l + 1, L, reduced.get(k), f"rs_sum_{k}_l{l + 1}")
        (dkn, dv, dkr_heads), early_arrived = _mla_bwd_dkv(
            s["q_full"], s["qr"], s["kv"], s["kr"], dy_b, s["lse"], delta, T,
            carry=_scatter_stage([p16 for _, p16 in early_partial]) if hide_early else None)
        if hide_early:
            for k, (p32, _), arr in zip(EARLY, early_partial, early_arrived):
                reduced[k] = _sum_into(p32, arr, chip, cc, l, L, reduced.get(k), f"rs_sum_{k}_l{l}")
        dkr = _sum_heads(dkr_heads, TM)
        dqr_raw, dkr_raw = _rope_call(dqr, MLA_HEADS * LANES, 0, dkr, LANES, 0, cos_t, sin_t, -1.0, TM, "rope_bwd")
        dq_full = jnp.concatenate([dqn, dqr_raw], axis=1)
        dkv = jnp.concatenate([dkn, dv], axis=1)
        gbig["w_uq"][l] = _col_chunks(_unperm_w_uq(_mm_tn(s["cqn"], dq_full, F32, "mm_dw_uq")))
        gbig["w_ukv"][l] = _col_chunks(_unperm_w_ukv(_mm_tn(s["ckvn"], dkv, F32, "mm_dw_ukv")))
        dcqn = _mm(dq_full, p["w_uq"], "nt", F32, "mm_dcqn")
        dckvn = _mm(dkv, p["w_ukv"], "nt", F32, "mm_dckvn")

        def fn_rms_bwd(i, rows, ps):
            _, vjp1 = jax.vjp(lambda c_, g_: _rms_norm(c_.astype(F32), g_), rows[0], ps[0])
            _, vjp2 = jax.vjp(lambda c_, g_: _rms_norm(c_.astype(F32), g_), rows[1], ps[1])
            d1, dg1 = vjp1(rows[2])
            d2, dg2 = vjp2(rows[3])
            return [d1, d2], [dg1, dg2]
        (dcq, dckv), (g_qg, g_kvg) = _rowwise(
            fn_rms_bwd, [(s["h"], MLA_Q_RANK, cb("cq")), (s["h"], MLA_KV_RANK, cb("ckv")), _whole(dcqn), _whole(dckvn)],
            [p["qg"], p["kvg"]], [(MLA_Q_RANK, BF16), (MLA_KV_RANK, BF16)], [(1, MLA_Q_RANK), (1, MLA_KV_RANK)], TM, "mla_rms_bwd")
        gsmall["q_norm_g"][l], gsmall["kv_norm_g"][l] = g_qg, g_kvg
        tail = jnp.zeros((S, NP - lay["kr"][0] - LANES), BF16)
        dh = jnp.concatenate([dgl, dqa, dcq, dckv, dh_c, dka, dva, dkr_raw, tail], axis=1)
        if l == 0:
            ready = [k for k in BIG if k not in part_z]
            done = [list(range(L)) if k in EARLY else list(range(1, L)) for k in ready]
            dw_in, outs = _mm_tn(s["x0b"], dh, F32, "mm_dw_in", tm=1024,
                                 carry=_both(_join_stage([reduced[k] for k in ready], done), scatter_of(part_z)))
            reduced.update(zip(ready, outs[:len(ready)]))
            for k, arr in zip(part_z, outs[len(ready):]):
                reduced[k] = _sum_into(partial[BIG.index(k)][0], arr, chip, cc, l + 1, L, reduced.get(k), f"rs_sum_{k}_l{l + 1}")
        else:
            dw_in = _mm_tn(s["x0b"], dh, F32, "mm_dw_in", tm=1024)
        gbig["w_in"][l] = _col_chunks(_unperm_w_in(dw_in, lay, D))
        if l > 0:
            dx = _mm(dh, p["w_in"], "nt", F32, "mm_dx0", add=dx0_res)

        pending = [gbig[k][l] for k in BIG]

    late = [gbig[k][0] for k in LATE]
    from_sibling = _run_stage(_swap_stage(late), "rs_swap_halves_l0")
    partial = [_add_half(g, r, cc, f"rs_add_{k}_l0") for k, g, r in zip(LATE, late, from_sibling)]
    dx, arrived = _mm(dh, p["w_in"], "nt", F32, "mm_dx0", add=dx0_res, carry=_scatter_stage([p16 for _, p16 in partial]))
    for k, (p32, _), arr in zip(LATE, partial, arrived):
        reduced[k] = _sum_into(p32, arr, chip, cc, 0, L, reduced.get(k), f"rs_sum_{k}_l0")

    grad_x = dx.reshape(x.shape)
    unjoined = [[0] + (list(range(1, L)) if k in part_z else []) for k in LATE]
    g_big = {**reduced, **dict(zip(LATE, _run_stage(_join_stage([reduced[k] for k in LATE], unjoined), "rs_join_halves")))}

    small_shapes = {k: (small_sharded_full[k] if k in SMALL_SHARDED else tuple(W[k].shape)) for k in SMALL}
    sv = jnp.concatenate([jnp.stack(gsmall[k]).reshape(-1) for k in SMALL])
    n_sv = sv.shape[0]
    sv = jnp.pad(sv, (0, -n_sv % (8 * LANES))).reshape(-1, LANES)
    sv = _all_reduce_small(sv).reshape(-1)
    g_small, o_ = {}, 0
    for k in SMALL:
        n = int(np.prod(small_shapes[k]))
        g = sv[o_:o_ + n].reshape(small_shapes[k])
        if k in SMALL_SHARDED:
            g = lax.dynamic_slice_in_dim(g, chip * W[k].shape[-1], W[k].shape[-1], axis=-1)
        g_small[k] = g
        o_ += n

    delta, new_m, new_v = {}, {}, {}
    swap_minor = lambda t: jnp.transpose(t, (0, 2, 1))
    for k in BIG:
        if shards[k][2] % LANES and not shards[k][1] % LANES:
            out = _adamw(swap_minor(W[k]), swap_minor(g_big[k]), swap_minor(Mo[k]), swap_minor(Vo[k]), "adamw_" + k)
            delta[k], new_m[k], new_v[k] = [swap_minor(t) for t in out]
        else:
            delta[k], new_m[k], new_v[k] = _adamw(W[k], g_big[k], Mo[k], Vo[k], "adamw_" + k)
    pack = lambda t: jnp.concatenate([t[k].reshape(-1) for k in SMALL])
    n_small = sum(int(np.prod(W[k].shape)) for k in SMALL)
    pad2 = lambda t: jnp.pad(t, (0, -n_small % (8 * LANES))).reshape(1, -1, LANES)
    d_, m_, v_ = _adamw(pad2(pack(W)), pad2(pack(g_small)), pad2(pack(Mo)), pad2(pack(Vo)), "adamw_small")
    o_ = 0
    for k in SMALL:
        n = int(np.prod(W[k].shape))
        take = lambda t: t.reshape(-1)[o_:o_ + n].reshape(W[k].shape)
        delta[k], new_m[k], new_v[k] = take(d_), take(m_), take(v_)
        o_ += n

    grads = {**g_big, **g_small}
    return (loss, grad_x, *[grads[k] for k in WEIGHTS], *[delta[k] for k in WEIGHTS], *[new_m[k] for k in WEIGHTS],
            *[new_v[k] for k in WEIGHTS])
```

```python
import functools
import math

import jax
import jax.numpy as jnp
import numpy as np
from jax import lax
from jax.experimental import pallas as pl
from jax.experimental.pallas import tpu as pltpu

F32, BF16 = jnp.float32, jnp.bfloat16
SDS = jax.ShapeDtypeStruct
MESH = pl.DeviceIdType.MESH

SWA_Q_HEADS, SWA_KV_HEADS, SWA_HEAD_DIM, SWA_BLOCK = 16, 2, 64, 128
MLA_HEADS, MLA_NOPE, MLA_ROPE, MLA_V = 16, 128, 64, 128
MLA_Q_RANK, MLA_KV_RANK = 512, 512
ROPE_THETA = 10000.0
SGU_GROUPS, SGU_GROUP_DIM, SGU_CHUNK = 8, 128, 128
SGU_WIDTH = SGU_GROUPS * SGU_GROUP_DIM
A_Q = SWA_Q_HEADS * SWA_HEAD_DIM
A_KV = SWA_KV_HEADS * SWA_HEAD_DIM
N_BRANCHES = 3
DEPTH = 2
EPS = 1e-5
MASK_VALUE = -1e30
DN_ALPHA = (2 * DEPTH) ** 0.25
ADAM_LR, ADAM_B1, ADAM_B2, ADAM_EPS, ADAM_WD, ADAM_STEP = 0.001, 0.9, 0.999, 1e-08, 0.01, 10
N_CHIPS = 4

LANES = 128
VMEM_LIMIT = 48 * 1024 * 1024

BIG = ["w_in", "w_uq", "w_ukv", "w_proj_a", "w_proj_b", "w_proj_c", "w_o", "w_up", "w_down"]
ROW_SHARDED = {"w_proj_b", "w_o", "w_down"}
LATE = ["w_in", "w_uq", "w_ukv"]
EARLY = [k for k in BIG if k not in LATE]
SMALL = ["b_gate", "sinks", "q_norm_g", "kv_norm_g", "sgu_ln_g", "sgu_ln_b", "sgu_w", "sgu_b", "ln1_g", "ln1_b",
         "conv_w", "conv_b", "ln2_g", "ln2_b"]
SMALL_SHARDED = {"b_gate", "conv_w"}
WEIGHTS = ["w_in", "b_gate", "sinks", "q_norm_g", "kv_norm_g", "w_uq", "w_ukv", "sgu_ln_g", "sgu_ln_b", "sgu_w", "sgu_b",
           "w_proj_a", "w_proj_b", "w_proj_c", "w_o", "ln1_g", "ln1_b", "w_up", "conv_w", "conv_b", "w_down", "ln2_g", "ln2_b"]


def _pcall(body, **kw):
    return pl.pallas_call(body, **kw)


def _params(sem=None):
    return pltpu.CompilerParams(dimension_semantics=sem, vmem_limit_bytes=VMEM_LIMIT)


def _tile(dim, pref, align=LANES):
    t = (min(pref, dim) // align) * align
    while t >= align:
        if dim % t == 0:
            return t
        t -= align
    return dim


def _mm(a, b, mode, out_dtype, name, add=None, tm=1024, tn=512, tk=2048, col_chunks=1, carry=None):
    assert mode in ("nn", "nt")
    if mode == "nn":
        (M, K), (K2, N) = a.shape, b.shape
    else:
        (M, K), (N, K2) = a.shape, b.shape
    assert K == K2, (a.shape, b.shape, mode)
    assert N % col_chunks == 0
    tm, tn, tk = _tile(M, tm), _tile(N // col_chunks, tn), _tile(K, tk)
    assert (N // col_chunks) % tn == 0
    per_chunk = (N // col_chunks) // tn
    nk = K // tk
    a_spec = pl.BlockSpec((tm, tk), lambda i, j, k: (i, k))
    if mode == "nt":
        b_spec = pl.BlockSpec((tn, tk), lambda i, j, k: (j, k))
    else:
        b_spec = pl.BlockSpec((tk, tn), lambda i, j, k: (k, j))
    dn = {"nn": (((1,), (0,)), ((), ())), "nt": (((1,), (1,)), ((), ()))}[mode]
    chunked = col_chunks > 1
    if chunked:
        assert add is None
        o_spec = pl.BlockSpec((1, tm, tn), lambda i, j, k: (lax.div(j, per_chunk), i, lax.rem(j, per_chunk)))
        out_shape = SDS((col_chunks, M, N // col_chunks), out_dtype)
    else:
        o_spec = pl.BlockSpec((tm, tn), lambda i, j, k: (i, j))
        out_shape = SDS((M, N), out_dtype)
    has_add = add is not None

    def body(*refs):
        if has_add:
            a_ref, b_ref, add_ref, o_ref, acc_ref = refs
        else:
            a_ref, b_ref, o_ref, acc_ref = refs
        k = pl.program_id(2)

        @pl.when(k == 0)
        def _():
            acc_ref[...] = jnp.zeros_like(acc_ref)

        acc_ref[...] += lax.dot_general(a_ref[...].astype(BF16), b_ref[...].astype(BF16), dn,
                                        preferred_element_type=F32)

        @pl.when(k == nk - 1)
        def _():
            r = acc_ref[...]
            if has_add:
                r = r + add_ref[...].astype(F32)
            if chunked:
                o_ref[0] = r.astype(o_ref.dtype)
            else:
                o_ref[...] = r.astype(o_ref.dtype)

    ins = [a, b] + ([add] if has_add else [])
    in_specs = [a_spec, b_spec] + ([o_spec] if has_add else [])
    (out,), carried = _call(body, name, (M // tm, N // tn, nk), 0, in_specs, [o_spec], [out_shape], [pltpu.VMEM((tm, tn), F32)],
                            ins, ("parallel", "parallel", "arbitrary"), carry)
    return out if carry is None else (out, carried)


def _mm_tn(a, b, out_dtype, name, tm=512, tn=512, col_chunks=1, carry=None):
    (K, M), (K2, N) = a.shape, b.shape
    assert K == K2 and N % col_chunks == 0
    tm, tn = _tile(M, tm), _tile(N // col_chunks, tn)
    per_chunk = (N // col_chunks) // tn
    chunked = col_chunks > 1
    if chunked:
        o_spec = pl.BlockSpec((1, tm, tn), lambda i, j: (lax.div(j, per_chunk), i, lax.rem(j, per_chunk)))
        out_shape = SDS((col_chunks, M, N // col_chunks), out_dtype)
    else:
        o_spec = pl.BlockSpec((tm, tn), lambda i, j: (i, j))
        out_shape = SDS((M, N), out_dtype)

    def body(a_ref, b_ref, o_ref, at_ref):
        @pl.when(pl.program_id(1) == 0)
        def _():
            at_ref[...] = a_ref[...].astype(BF16).T

        r = jnp.dot(at_ref[...], b_ref[...].astype(BF16), preferred_element_type=F32)
        if chunked:
            o_ref[0] = r.astype(o_ref.dtype)
        else:
            o_ref[...] = r.astype(o_ref.dtype)

    (out,), carried = _call(body, name, (M // tm, N // tn), 0,
                            [pl.BlockSpec((K, tm), lambda i, j: (0, i)), pl.BlockSpec((K, tn), lambda i, j: (0, j))],
                            [o_spec], [out_shape], [pltpu.VMEM((tm, K), BF16)], [a, b], ("parallel", "arbitrary"), carry)
    return out if carry is None else (out, carried)


def _rowwise(fn, rows, params, row_outs, acc_outs, tm, name):
    n_rows = rows[0][0].shape[0]
    assert n_rows % tm == 0
    nr, npar, no = len(rows), len(params), len(row_outs)

    def body(*refs):
        i = pl.program_id(0)
        r, p = refs[:nr], refs[nr:nr + npar]
        o, acc = refs[nr + npar:nr + npar + no], refs[nr + npar + no:]
        outs, sums = fn(i, [x[...] for x in r], [x[...] for x in p])
        for ref, val in zip(o, outs, strict=True):
            ref[...] = val.astype(ref.dtype)
        if acc:
            @pl.when(i == 0)
            def _():
                for ref in acc:
                    ref[...] = jnp.zeros_like(ref)
            for ref, val in zip(acc, sums, strict=True):
                ref[...] += val.astype(F32)

    def full(shape):
        nd = len(shape)
        return pl.BlockSpec(tuple(shape), lambda i: (0,) * nd)

    in_specs = [pl.BlockSpec((tm, w), (lambda i, cb=cb: (i, cb))) for (_, w, cb) in rows] + [full(p.shape) for p in params]
    out_specs = [pl.BlockSpec((tm, w), lambda i: (i, 0)) for (w, _) in row_outs] + [full(s) for s in acc_outs]
    out_shape = [SDS((n_rows, w), dt) for (w, dt) in row_outs] + [SDS(tuple(s), F32) for s in acc_outs]
    res = _pcall(body, name=name, out_shape=out_shape, grid=(n_rows // tm,), in_specs=in_specs, out_specs=out_specs,
                 compiler_params=_params(("arbitrary",)))(*[r[0] for r in rows], *params)
    return list(res[:no]), list(res[no:])


def _whole(a):
    return (a, a.shape[1], 0)


def _gelu(x):
    return 0.5 * x * (1.0 + lax.erf(x * (1.0 / math.sqrt(2.0))))


def _layer_norm(x, g, b):
    mu = x.mean(-1, keepdims=True)
    var = jnp.mean(jnp.square(x - mu), -1, keepdims=True)
    return (x - mu) * lax.rsqrt(var + EPS) * g + b


def _rms_norm(x, g):
    return x * lax.rsqrt(jnp.mean(jnp.square(x), -1, keepdims=True) + EPS) * g


def _sgu_math(hu, hv, ln_g, ln_b, ws, bs):
    u = _gelu(hu.astype(F32))
    vn = _layer_norm(_gelu(hv.astype(F32)), ln_g, ln_b)
    r = lax.broadcasted_iota(jnp.int32, (SGU_CHUNK, SGU_CHUNK), 0)
    c = lax.broadcasted_iota(jnp.int32, (SGU_CHUNK, SGU_CHUNK), 1)
    outs = []
    for g in range(SGU_GROUPS):
        w = jnp.where(r >= c, ws[g], 0.0).astype(BF16)
        vg = vn[:, g * SGU_GROUP_DIM:(g + 1) * SGU_GROUP_DIM].astype(BF16)
        outs.append(jnp.dot(w, vg, preferred_element_type=F32) + bs[g])
    return u * jnp.concatenate(outs, axis=1)


def _sgu_fwd(h, cu, cv, ln_g, ln_b, w, b3):
    def fn(i, rows, ps):
        g_, b_, w_, b3_ = ps
        y = _sgu_math(rows[0], rows[1], g_, b_, [w_[g] for g in range(SGU_GROUPS)], [b3_[g] for g in range(SGU_GROUPS)])
        return [y], []
    (y,), _ = _rowwise(fn, [(h, SGU_WIDTH, cu), (h, SGU_WIDTH, cv)], [ln_g, ln_b, w, b3], [(SGU_WIDTH, BF16)], [],
                       SGU_CHUNK, "sgu_fwd")
    return y


def _sgu_bwd(h, cu, cv, dy, ln_g, ln_b, w, b3):
    nd = 2 * SGU_WIDTH

    def body(hu_ref, hv_ref, dy_ref, g_ref, b_ref, w_ref, b3_ref, dh_ref, dg_ref, db_ref, dw_ref, db3_ref):
        i = pl.program_id(0)

        @pl.when(i == 0)
        def _():
            dg_ref[...] = jnp.zeros_like(dg_ref)
            db_ref[...] = jnp.zeros_like(db_ref)
            dw_ref[...] = jnp.zeros_like(dw_ref)
            db3_ref[...] = jnp.zeros_like(db3_ref)

        ws = [w_ref[g] for g in range(SGU_GROUPS)]
        bs = [b3_ref[g] for g in range(SGU_GROUPS)]
        _, vjp = jax.vjp(_sgu_math, hu_ref[...], hv_ref[...], g_ref[...], b_ref[...], ws, bs)
        dhu, dhv, dg, db, dws, dbs = vjp(dy_ref[...].astype(F32))
        dh_ref[...] = jnp.concatenate([dhu, dhv], axis=1).astype(dh_ref.dtype)
        dg_ref[...] += dg
        db_ref[...] += db
        for g in range(SGU_GROUPS):
            dw_ref[g] += dws[g]
            db3_ref[g] += dbs[g]

    n = h.shape[0]
    blk = lambda cb: pl.BlockSpec((SGU_CHUNK, SGU_WIDTH), lambda i, cb=cb: (i, cb))
    full = lambda s: pl.BlockSpec(tuple(s), lambda i: (0,) * len(s))
    return _pcall(
        body, name="sgu_bwd", grid=(n // SGU_CHUNK,),
        out_shape=[SDS((n, nd), BF16), SDS(ln_g.shape, F32), SDS(ln_b.shape, F32), SDS(w.shape, F32), SDS(b3.shape, F32)],
        in_specs=[blk(cu), blk(cv), blk(0), full(ln_g.shape), full(ln_b.shape), full(w.shape), full(b3.shape)],
        out_specs=[pl.BlockSpec((SGU_CHUNK, nd), lambda i: (i, 0)), full(ln_g.shape), full(ln_b.shape), full(w.shape),
                   full(b3.shape)],
        compiler_params=_params(("arbitrary",)))(h, h, dy, ln_g, ln_b, w, b3)


def _swa_math(q, kp, kc, vp, vc, sinks, not_first):
    kw = jnp.concatenate([kp, kc], axis=0).astype(BF16)
    vw = jnp.concatenate([vp, vc], axis=0).astype(BF16)
    qb = q.astype(BF16)
    q_off = lax.broadcasted_iota(jnp.int32, (SWA_BLOCK, 2 * SWA_BLOCK), 0) + SWA_BLOCK
    k_off = lax.broadcasted_iota(jnp.int32, (SWA_BLOCK, 2 * SWA_BLOCK), 1)
    rel = q_off - k_off
    valid = (rel >= 0) & (rel < SWA_BLOCK) & (not_first | (k_off >= SWA_BLOCK))
    G = SWA_Q_HEADS // SWA_KV_HEADS
    outs = []
    for head in range(SWA_Q_HEADS):
        hk = head // G
        qh = qb[:, head * SWA_HEAD_DIM:(head + 1) * SWA_HEAD_DIM]
        kh = kw[:, hk * SWA_HEAD_DIM:(hk + 1) * SWA_HEAD_DIM]
        vh = vw[:, hk * SWA_HEAD_DIM:(hk + 1) * SWA_HEAD_DIM]
        s = lax.dot_general(qh, kh, (((1,), (1,)), ((), ())), preferred_element_type=F32) * (SWA_HEAD_DIM ** -0.5)
        s = jnp.where(valid, s, MASK_VALUE)
        sink = sinks[:, head:head + 1]
        m = jnp.maximum(s.max(-1, keepdims=True), sink)
        p = jnp.exp(s - m)
        p = (p / (p.sum(-1, keepdims=True) + jnp.exp(sink - m))).astype(BF16)
        outs.append(jnp.dot(p, vh, preferred_element_type=F32))
    return jnp.concatenate(outs, axis=1)


def _swa_fwd(h, cq, ck, cv, sinks, carry=None):
    n = h.shape[0]
    nb = n // SWA_BLOCK

    def body(q_ref, kp_ref, kc_ref, vp_ref, vc_ref, s_ref, o_ref):
        i = pl.program_id(0)
        f = lambda x: x[...].astype(F32)
        o_ref[...] = _swa_math(f(q_ref), f(kp_ref), f(kc_ref), f(vp_ref), f(vc_ref), s_ref[...], i > 0).astype(o_ref.dtype)

    prev = lambda cb: pl.BlockSpec((SWA_BLOCK, A_KV), lambda i, cb=cb: (jnp.maximum(i - 1, 0), cb))
    cur = lambda cb: pl.BlockSpec((SWA_BLOCK, A_KV), lambda i, cb=cb: (i, cb))
    return _call(body, "swa_fwd", (nb,), 0,
                 [pl.BlockSpec((SWA_BLOCK, A_Q), lambda i: (i, cq)), prev(ck), cur(ck), prev(cv), cur(cv),
                  pl.BlockSpec((1, SWA_Q_HEADS), lambda i: (0, 0))],
                 [pl.BlockSpec((SWA_BLOCK, A_Q), lambda i: (i, 0))], [SDS((n, A_Q), BF16)], [],
                 [h, h, h, h, h, sinks], ("arbitrary",), carry)


def _swa_bwd(h, cq, ck, cv, sinks, dy, carry=None):
    n = h.shape[0]
    nb = n // SWA_BLOCK

    def body(q_ref, kp_ref, kc_ref, vp_ref, vc_ref, s_ref, dy_ref, dq_ref, dk_ref, dv_ref, ds_ref, ck_ref, cv_ref):
        r = pl.program_id(0)
        blk = nb - 1 - r

        @pl.when(r == 0)
        def _():
            ds_ref[...] = jnp.zeros_like(ds_ref)
            ck_ref[...] = jnp.zeros_like(ck_ref)
            cv_ref[...] = jnp.zeros_like(cv_ref)

        f = lambda x: x[...].astype(F32)
        not_first = blk > 0
        _, vjp = jax.vjp(lambda q, kp, kc, vp, vc, s: _swa_math(q, kp, kc, vp, vc, s, not_first),
                         f(q_ref), f(kp_ref), f(kc_ref), f(vp_ref), f(vc_ref), s_ref[...])
        dq, dkp, dkc, dvp, dvc, dsk = vjp(f(dy_ref))
        dq_ref[...] = dq.astype(dq_ref.dtype)
        dk_ref[...] = (dkc + ck_ref[...]).astype(dk_ref.dtype)
        dv_ref[...] = (dvc + cv_ref[...]).astype(dv_ref.dtype)
        ck_ref[...] = dkp
        cv_ref[...] = dvp
        ds_ref[...] += dsk

    rev = lambda i: nb - 1 - i
    prev = lambda cb: pl.BlockSpec((SWA_BLOCK, A_KV), lambda i, cb=cb: (jnp.maximum(rev(i) - 1, 0), cb))
    cur = lambda cb: pl.BlockSpec((SWA_BLOCK, A_KV), lambda i, cb=cb: (rev(i), cb))
    return _call(
        body, "swa_bwd", (nb,), 0,
        [pl.BlockSpec((SWA_BLOCK, A_Q), lambda i: (rev(i), cq)), prev(ck), cur(ck), prev(cv), cur(cv),
         pl.BlockSpec((1, SWA_Q_HEADS), lambda i: (0, 0)), pl.BlockSpec((SWA_BLOCK, A_Q), lambda i: (rev(i), 0))],
        [pl.BlockSpec((SWA_BLOCK, A_Q), lambda i: (rev(i), 0)), pl.BlockSpec((SWA_BLOCK, A_KV), lambda i: (rev(i), 0)),
         pl.BlockSpec((SWA_BLOCK, A_KV), lambda i: (rev(i), 0)), pl.BlockSpec((1, SWA_Q_HEADS), lambda i: (0, 0))],
        [SDS((n, A_Q), BF16), SDS((n, A_KV), BF16), SDS((n, A_KV), BF16), SDS((1, SWA_Q_HEADS), F32)],
        [pltpu.VMEM((SWA_BLOCK, A_KV), F32), pltpu.VMEM((SWA_BLOCK, A_KV), F32)],
        [h, h, h, h, h, sinks, dy], ("arbitrary",), carry)


def _rope(x, cos, sin, sign):
    w = x.shape[1]
    reps = w // LANES
    ct = jnp.tile(cos, (1, reps)) if reps > 1 else cos
    st = jnp.tile(sin, (1, reps)) if reps > 1 else sin
    fwd = pltpu.roll(x, MLA_ROPE // 2, axis=1)
    bwd = pltpu.roll(x, w - MLA_ROPE // 2, axis=1)
    lane = lax.broadcasted_iota(jnp.int32, x.shape, 1) % LANES
    rot = jnp.where(lane < MLA_ROPE // 2, -bwd, fwd)
    return x * ct + sign * (rot * st)


def _rope_call(a, wa, ca, b, wb, cb, cos, sin, sign, tm, name):
    def fn(i, rows, ps):
        xa, xb, c_, s_ = rows
        return [_rope(xa.astype(F32), c_, s_, sign), _rope(xb.astype(F32), c_, s_, sign)], []
    (ra, rb), _ = _rowwise(fn, [(a, wa, ca), (b, wb, cb), _whole(cos), _whole(sin)], [], [(wa, BF16), (wb, BF16)], [], tm, name)
    return ra, rb


MLA_SCALE = (MLA_NOPE + MLA_ROPE) ** -0.5
LOG2E = math.log2(math.e)


MLA_HP = 8
MLA_W = MLA_HP * LANES


def _mla_scores(qn_ref, qr_ref, kn_ref, kr_ref, hh, masked):
    cols = slice(hh * LANES, (hh + 1) * LANES)
    q = jnp.concatenate([qn_ref[:, cols], qr_ref[:, cols]], axis=1)
    k = jnp.concatenate([kn_ref[:, cols], kr_ref[...]], axis=1)
    s = lax.dot_general(q, k, (((1,), (1,)), ((), ())), preferred_element_type=F32)
    if masked:
        row = lax.broadcasted_iota(jnp.int32, s.shape, 0)
        col = lax.broadcasted_iota(jnp.int32, s.shape, 1)
        s = jnp.where(col <= row, s, MASK_VALUE)
    return s, q, k


def _causal_pairs(nq, by_query):
    if by_query:
        pairs = [(i, j) for i in range(nq) for j in range(i + 1)]
    else:
        pairs = [(i, j) for j in range(nq) for i in range(j, nq)]
    return (jnp.asarray(np.array([p[0] for p in pairs], np.int32)), jnp.asarray(np.array([p[1] for p in pairs], np.int32)),
            len(pairs))


def _mla_fwd(q_full, qr, kv, kr, T, carry=None):
    n = q_full.shape[0]
    nq = n // T
    H = MLA_HEADS
    qi, kj, npairs = _causal_pairs(nq, True)

    def body(qi_ref, kj_ref, qn_ref, qr_ref, kn_ref, v_ref, kr_ref, y_ref, lse_ref, m_ref, l_ref, acc_ref):
        t = pl.program_id(1)
        i, j = qi_ref[t], kj_ref[t]

        @pl.when(j == 0)
        def _():
            m_ref[...] = jnp.full_like(m_ref, MASK_VALUE)
            l_ref[...] = jnp.zeros_like(l_ref)
            acc_ref[...] = jnp.zeros_like(acc_ref)

        def update(masked):
            for hh in range(MLA_HP):
                s, _, _ = _mla_scores(qn_ref, qr_ref, kn_ref, kr_ref, hh, masked)
                m_prev = m_ref[hh]
                m_new = jnp.maximum(m_prev, s.max(-1, keepdims=True))
                p = jnp.exp2((s - m_new[:, :1]) * (MLA_SCALE * LOG2E))
                alpha = jnp.exp2((m_prev - m_new) * (MLA_SCALE * LOG2E))
                l_ref[hh] = alpha * l_ref[hh] + p.sum(-1, keepdims=True)
                acc_ref[hh] = alpha * acc_ref[hh] + jnp.dot(p.astype(BF16), v_ref[:, hh * LANES:(hh + 1) * LANES],
                                                            preferred_element_type=F32)
                m_ref[hh] = m_new

        @pl.when(j < i)
        def _():
            update(False)

        @pl.when(j == i)
        def _():
            update(True)
            for hh in range(MLA_HP):
                y_ref[:, hh * LANES:(hh + 1) * LANES] = (acc_ref[hh] / l_ref[hh]).astype(y_ref.dtype)
                lse_ref[hh] = m_ref[hh] * (MLA_SCALE * LOG2E) + jnp.log2(l_ref[hh])

    G = H // MLA_HP
    qspec = lambda off: pl.BlockSpec((T, MLA_W), lambda h, t, qi, kj, off=off: (qi[t], off + h))
    kspec = lambda off: pl.BlockSpec((T, MLA_W), lambda h, t, qi, kj, off=off: (kj[t], off + h))
    return _call(
        body, "mla_fwd", (G, npairs), 2,
        [qspec(0), qspec(0), kspec(0), kspec(G), pl.BlockSpec((T, LANES), lambda h, t, qi, kj: (kj[t], 0))],
        [pl.BlockSpec((T, MLA_W), lambda h, t, qi, kj: (qi[t], h)),
         pl.BlockSpec((MLA_HP, T, LANES), lambda h, t, qi, kj: (h, qi[t], 0))],
        [SDS((n, H * MLA_V), BF16), SDS((H, n, LANES), F32)], [pltpu.VMEM((MLA_HP, T, LANES), F32)] * 3,
        [qi, kj, q_full, qr, kv, kv, kr], ("parallel", "arbitrary"), carry)


def _mla_delta(dy, y, T):
    n = y.shape[0]
    H = MLA_HEADS

    def body(dy_ref, y_ref, d_ref):
        d = jnp.sum(dy_ref[...].astype(F32) * y_ref[...].astype(F32), axis=-1, keepdims=True)
        d_ref[0] = jnp.broadcast_to(d, (T, LANES))

    spec = pl.BlockSpec((T, LANES), lambda h, i: (i, h))
    return _pcall(body, name="mla_delta", grid=(H, n // T), out_shape=SDS((H, n, LANES), F32), in_specs=[spec, spec],
                  out_specs=pl.BlockSpec((1, T, LANES), lambda h, i: (h, i, 0)),
                  compiler_params=_params(("parallel", "parallel")))(dy, y)


def _mla_bwd_dq(q_full, qr, kv, kr, dy, lse, delta, T, carry=None):
    n = q_full.shape[0]
    nq = n // T
    H = MLA_HEADS

    qi, kj, npairs = _causal_pairs(nq, True)

    def body(qi_ref, kj_ref, qn_ref, qr_ref, kn_ref, v_ref, kr_ref, dy_ref, lse_ref, dl_ref, dqn_ref, dqr_ref, acc_ref):
        t = pl.program_id(1)
        i, j = qi_ref[t], kj_ref[t]

        @pl.when(j == 0)
        def _():
            acc_ref[...] = jnp.zeros_like(acc_ref)

        def update(masked):
            for hh in range(MLA_HP):
                cols = slice(hh * LANES, (hh + 1) * LANES)
                s, _, k = _mla_scores(qn_ref, qr_ref, kn_ref, kr_ref, hh, masked)
                p = jnp.exp2(s * (MLA_SCALE * LOG2E) - lse_ref[hh][:, :1])
                dp = lax.dot_general(dy_ref[:, cols], v_ref[:, cols], (((1,), (1,)), ((), ())), preferred_element_type=F32)
                ds = p * (dp - dl_ref[hh][:, :1])
                acc_ref[hh] += jnp.dot(ds.astype(BF16), k, preferred_element_type=F32)

        @pl.when(j < i)
        def _():
            update(False)

        @pl.when(j == i)
        def _():
            update(True)
            for hh in range(MLA_HP):
                cols = slice(hh * LANES, (hh + 1) * LANES)
                dqn_ref[:, cols] = (acc_ref[hh][:, :LANES] * MLA_SCALE).astype(dqn_ref.dtype)
                dqr_ref[:, cols] = (acc_ref[hh][:, LANES:] * MLA_SCALE).astype(dqr_ref.dtype)

    G = H // MLA_HP
    qspec = lambda off: pl.BlockSpec((T, MLA_W), lambda h, t, qi, kj, off=off: (qi[t], off + h))
    kspec = lambda off: pl.BlockSpec((T, MLA_W), lambda h, t, qi, kj, off=off: (kj[t], off + h))
    stat = pl.BlockSpec((MLA_HP, T, LANES), lambda h, t, qi, kj: (h, qi[t], 0))
    out = pl.BlockSpec((T, MLA_W), lambda h, t, qi, kj: (qi[t], h))
    return _call(
        body, "mla_bwd_dq", (G, npairs), 2,
        [qspec(0), qspec(0), kspec(0), kspec(G), pl.BlockSpec((T, LANES), lambda h, t, qi, kj: (kj[t], 0)), qspec(0), stat, stat],
        [out, out], [SDS((n, H * LANES), BF16), SDS((n, H * LANES), BF16)], [pltpu.VMEM((MLA_HP, T, 2 * LANES), F32)],
        [qi, kj, q_full, qr, kv, kv, kr, dy, lse, delta], ("parallel", "arbitrary"), carry)


def _mla_bwd_dkv(q_full, qr, kv, kr, dy, lse, delta, T, carry=None):
    n = q_full.shape[0]
    nq = n // T
    H = MLA_HEADS

    qi, kj, npairs = _causal_pairs(nq, False)

    def body(qi_ref, kj_ref, qn_ref, qr_ref, kn_ref, v_ref, kr_ref, dy_ref, lse_ref, dl_ref, dkn_ref, dv_ref, dkr_ref,
             dk_acc, dv_acc):
        t = pl.program_id(1)
        i, j = qi_ref[t], kj_ref[t]

        @pl.when(i == j)
        def _():
            dk_acc[...] = jnp.zeros_like(dk_acc)
            dv_acc[...] = jnp.zeros_like(dv_acc)

        def update(masked):
            for hh in range(MLA_HP):
                cols = slice(hh * LANES, (hh + 1) * LANES)
                s, q, _ = _mla_scores(qn_ref, qr_ref, kn_ref, kr_ref, hh, masked)
                p = jnp.exp2(s * (MLA_SCALE * LOG2E) - lse_ref[hh][:, :1])
                dy = dy_ref[:, cols]
                dv_acc[hh] += lax.dot_general(p.astype(BF16), dy, (((0,), (0,)), ((), ())), preferred_element_type=F32)
                dp = lax.dot_general(dy, v_ref[:, cols], (((1,), (1,)), ((), ())), preferred_element_type=F32)
                ds = p * (dp - dl_ref[hh][:, :1])
                dk_acc[hh] += lax.dot_general(ds.astype(BF16), q, (((0,), (0,)), ((), ())), preferred_element_type=F32)

        @pl.when(i == j)
        def _():
            update(True)

        @pl.when(i > j)
        def _():
            update(False)

        @pl.when(i == nq - 1)
        def _():
            for hh in range(MLA_HP):
                cols = slice(hh * LANES, (hh + 1) * LANES)
                dkn_ref[:, cols] = (dk_acc[hh][:, :LANES] * MLA_SCALE).astype(dkn_ref.dtype)
                dv_ref[:, cols] = dv_acc[hh].astype(dv_ref.dtype)
                dkr_ref[hh] = dk_acc[hh][:, LANES:] * MLA_SCALE

    G = H // MLA_HP
    qspec = lambda off: pl.BlockSpec((T, MLA_W), lambda h, t, qi, kj, off=off: (qi[t], off + h))
    kspec = lambda off: pl.BlockSpec((T, MLA_W), lambda h, t, qi, kj, off=off: (kj[t], off + h))
    stat = pl.BlockSpec((MLA_HP, T, LANES), lambda h, t, qi, kj: (h, qi[t], 0))
    out = pl.BlockSpec((T, MLA_W), lambda h, t, qi, kj: (kj[t], h))
    return _call(
        body, "mla_bwd_dkv", (G, npairs), 2,
        [qspec(0), qspec(0), kspec(0), kspec(G), pl.BlockSpec((T, LANES), lambda h, t, qi, kj: (kj[t], 0)), qspec(0), stat, stat],
        [out, out, pl.BlockSpec((MLA_HP, T, LANES), lambda h, t, qi, kj: (h, kj[t], 0))],
        [SDS((n, H * LANES), BF16), SDS((n, H * LANES), BF16), SDS((H, n, LANES), F32)],
        [pltpu.VMEM((MLA_HP, T, 2 * LANES), F32), pltpu.VMEM((MLA_HP, T, LANES), F32)],
        [qi, kj, q_full, qr, kv, kv, kr, dy, lse, delta], ("parallel", "arbitrary"), carry)


def _sum_heads(a, tm):
    H, n, _ = a.shape

    def body(a_ref, o_ref):
        o_ref[...] = jnp.sum(a_ref[...], axis=0)

    return _pcall(body, name="mla_sum_heads", grid=(n // tm,), out_shape=SDS((n, LANES), F32),
                  in_specs=[pl.BlockSpec((H, tm, LANES), lambda i: (0, i, 0))],
                  out_specs=pl.BlockSpec((tm, LANES), lambda i: (i, 0)), compiler_params=_params(("parallel",)))(a)


def _shift_down(x, k):
    row = lax.broadcasted_iota(jnp.int32, x.shape, 0)
    return jnp.where(row >= k, pltpu.roll(x, k, axis=0), 0.0)


def _shift_up(x, k):
    n = x.shape[0]
    row = lax.broadcasted_iota(jnp.int32, x.shape, 0)
    return jnp.where(row < n - k, pltpu.roll(x, n - k, axis=0), 0.0)


def _conv_fwd(up, w, b):
    n, c = up.shape

    def body(u_ref, w_ref, b_ref, o_ref):
        u = u_ref[...].astype(F32)
        wv = w_ref[...]
        o_ref[...] = (b_ref[...] + wv[0:1] * _shift_down(u, 2) + wv[1:2] * _shift_down(u, 1) + wv[2:3] * u).astype(o_ref.dtype)

    return _pcall(body, name="conv_fwd", grid=(c // LANES,), out_shape=SDS((n, c), BF16),
                  in_specs=[pl.BlockSpec((n, LANES), lambda j: (0, j)), pl.BlockSpec((3, LANES), lambda j: (0, j)),
                            pl.BlockSpec((1, LANES), lambda j: (0, j))],
                  out_specs=pl.BlockSpec((n, LANES), lambda j: (0, j)), compiler_params=_params(("parallel",)))(up, w, b)


def _conv_bwd(up, dc, w):
    n, c = up.shape

    def body(u_ref, d_ref, w_ref, du_ref, dw_ref, db_ref):
        u = u_ref[...].astype(F32)
        d = d_ref[...].astype(F32)
        wv = w_ref[...]
        du_ref[...] = (wv[2:3] * d + wv[1:2] * _shift_up(d, 1) + wv[0:1] * _shift_up(d, 2)).astype(du_ref.dtype)
        dw_ref[0:1, :] = jnp.sum(d * _shift_down(u, 2), axis=0, keepdims=True)
        dw_ref[1:2, :] = jnp.sum(d * _shift_down(u, 1), axis=0, keepdims=True)
        dw_ref[2:3, :] = jnp.sum(d * u, axis=0, keepdims=True)
        db_ref[...] = jnp.sum(d, axis=0, keepdims=True)

    col = pl.BlockSpec((n, LANES), lambda j: (0, j))
    return _pcall(body, name="conv_bwd", grid=(c // LANES,),
                  out_shape=[SDS((n, c), BF16), SDS((3, c), F32), SDS((1, c), F32)],
                  in_specs=[col, col, pl.BlockSpec((3, LANES), lambda j: (0, j))],
                  out_specs=[col, pl.BlockSpec((3, LANES), lambda j: (0, j)), pl.BlockSpec((1, LANES), lambda j: (0, j))],
                  compiler_params=_params(("parallel",)))(up, dc, w)


def _adamw_math(w, g, m, v):
    m = ADAM_B1 * m + (1.0 - ADAM_B1) * g
    v = ADAM_B2 * v + (1.0 - ADAM_B2) * jnp.square(g)
    m_hat = m / (1.0 - ADAM_B1 ** ADAM_STEP)
    v_hat = v / (1.0 - ADAM_B2 ** ADAM_STEP)
    delta = -ADAM_LR * (m_hat / (jnp.sqrt(v_hat) + ADAM_EPS) + ADAM_WD * w)
    return delta, m, v


def _adamw(w, g, m, v, name):
    L, r, c = w.shape
    tr = _rows_tile(r, c, 1 << 20, 8)

    def body(w_ref, g_ref, m_ref, v_ref, d_ref, nm_ref, nv_ref):
        d, nm, nv = _adamw_math(w_ref[...], g_ref[...], m_ref[...], v_ref[...])
        d_ref[...] = d
        nm_ref[...] = nm
        nv_ref[...] = nv

    spec = pl.BlockSpec((1, tr, c), lambda l, i: (l, i, 0))
    return _pcall(body, name=name, grid=(L, r // tr), out_shape=[SDS(w.shape, F32)] * 3, in_specs=[spec] * 4,
                  out_specs=[spec] * 3, compiler_params=_params(("parallel", "parallel")))(w, g, m, v)


def _coords():
    return lax.axis_index("x"), lax.axis_index("y"), lax.axis_index("c")


def _other_chips(x, y):
    return [(1 - x, y), (x, 1 - y), (1 - x, 1 - y)]


HBM_SPEC = pl.BlockSpec(memory_space=pltpu.HBM)


def _half(c, rows):
    return pl.ds(pl.multiple_of(c * rows, 16), rows)


def _rows_tile(rows, cols, budget_bytes=2 << 20, align=16):
    t = (min(max(align, budget_bytes // (4 * cols)), rows) // align) * align
    while t >= align:
        if rows % t == 0:
            return t
        t -= align
    return rows


def _scalar(v):
    return jnp.reshape(jnp.asarray(v, jnp.int32), (1,))


def _cast_into_slot(w3, layer, slot, name):
    _, R, C = w3.shape
    tr = _rows_tile(R, C)

    def body(s_ref, w_ref, o_ref):
        o_ref[0] = w_ref[0].astype(BF16)

    gs = pltpu.PrefetchScalarGridSpec(
        num_scalar_prefetch=1, grid=(R // tr,),
        in_specs=[pl.BlockSpec((1, tr, C), lambda i, s: (layer, i, 0))],
        out_specs=pl.BlockSpec((1, tr, C), lambda i, s: (s[0], i, 0)))
    return _pcall(body, name=name, grid_spec=gs, out_shape=SDS((N_CHIPS, R, C), BF16),
                  compiler_params=_params(("arbitrary",)))(_scalar(slot), w3)


class _Stage:
    def __init__(self, ins, out_shapes, aliases, n_sems, copies):
        self.ins, self.out_shapes, self.aliases, self.n_sems, self.copies = list(ins), out_shapes, aliases, n_sems, copies

    def start(self, ins, outs, send_sems, recv_sems):
        for cp in self.copies(ins, outs, send_sems, recv_sems)[0]:
            cp.start()

    def finish(self, ins, outs, send_sems, recv_sems):
        sends, arrivals = self.copies(ins, outs, send_sems, recv_sems)
        for cp in arrivals:
            cp.wait_recv()
        for cp in sends:
            cp.wait_send()


class _SemsFrom:
    def __init__(self, sems, base):
        self.sems, self.base = sems, base

    @property
    def at(self):
        return self

    def __getitem__(self, k):
        return self.sems.at[self.base + k]


def _both(a, b):
    if a is None or b is None:
        return a if b is None else b
    na, nao = len(a.ins), len(a.out_shapes)

    def copies(ins, outs, send_sems, recv_sems):
        sa, aa = a.copies(ins[:na], outs[:nao], send_sems, recv_sems)
        sb, ab = b.copies(ins[na:], outs[nao:], _SemsFrom(send_sems, a.n_sems), _SemsFrom(recv_sems, a.n_sems))
        return sa + sb, aa + ab

    aliases = {**a.aliases, **{na + i: nao + o for i, o in b.aliases.items()}}
    return _Stage(a.ins + b.ins, list(a.out_shapes) + list(b.out_shapes), aliases, a.n_sems + b.n_sems, copies)


def _remote(src, dst, send_sems, recv_sems, k, to):
    return pltpu.make_async_remote_copy(src_ref=src, dst_ref=dst, send_sem=send_sems.at[k], recv_sem=recv_sems.at[k],
                                        device_id=to, device_id_type=MESH)


def _gather_stages(bufs):
    n = len(bufs)
    shapes = [SDS(b.shape, b.dtype) for b in bufs]
    same = {i: i for i in range(n)}

    def over_ici(ins, outs, send_sems, recv_sems):
        x, y, c = _coords()
        blk = lambda w, chip: outs[w].at[chip, _half(c, outs[w].shape[1] // 2), :]
        sends, arrivals = [], []
        for w in range(n):
            for j, (px, py) in enumerate(_other_chips(x, y)):
                sends.append(_remote(blk(w, 2 * x + y), blk(w, 2 * x + y), send_sems, recv_sems, 3 * w + j, (px, py, c)))
                arrivals.append(_remote(blk(w, 2 * px + py), blk(w, 2 * px + py), send_sems, recv_sems, 3 * w + j, (px, py, c)))
        return sends, arrivals

    def to_sibling(ins, outs, send_sems, recv_sems):
        x, y, c = _coords()
        blk = lambda w, chip, half: outs[w].at[chip, _half(half, outs[w].shape[1] // 2), :]
        sends, arrivals = [], []
        for w in range(n):
            for j, (px, py) in enumerate(_other_chips(x, y)):
                k = 2 * px + py
                sends.append(_remote(blk(w, k, c), blk(w, k, c), send_sems, recv_sems, 3 * w + j, (x, y, 1 - c)))
                arrivals.append(_remote(blk(w, k, 1 - c), blk(w, k, 1 - c), send_sems, recv_sems, 3 * w + j, (x, y, 1 - c)))
        return sends, arrivals

    return (lambda b: _Stage(b, shapes, same, 3 * n, over_ici)), (lambda b: _Stage(b, shapes, same, 3 * n, to_sibling))


def _swap_stage(gs_):
    n = len(gs_)

    def copies(ins, outs, send_sems, recv_sems):
        x, y, c = _coords()
        cps = [_remote(ins[w].at[:, _half(1 - c, ins[w].shape[1] // 2), :], outs[w], send_sems, recv_sems, w, (x, y, 1 - c))
               for w in range(n)]
        return cps, cps

    return _Stage(gs_, [SDS((N_CHIPS, g.shape[1] // 2, g.shape[2]), g.dtype) for g in gs_], {}, n, copies)


def _scatter_stage(ps):
    n = len(ps)

    def copies(ins, outs, send_sems, recv_sems):
        x, y, c = _coords()
        cps = [_remote(ins[w].at[2 * px + py], outs[w].at[j], send_sems, recv_sems, 3 * w + j, (px, py, c))
               for w in range(n) for j, (px, py) in enumerate(_other_chips(x, y))]
        return cps, cps

    return _Stage(ps, [SDS((3,) + p.shape[1:], p.dtype) for p in ps], {}, 3 * n, copies)


def _join_stage(bufs, layers):
    n = len(bufs)
    slots = [(w, l) for w in range(n) for l in layers[w]]

    def copies(ins, outs, send_sems, recv_sems):
        x, y, c = _coords()
        blk = lambda w, l, half: outs[w].at[l, _half(half, outs[w].shape[1] // 2), :]
        sends = [_remote(blk(w, l, c), blk(w, l, c), send_sems, recv_sems, k, (x, y, 1 - c)) for k, (w, l) in enumerate(slots)]
        arrivals = [_remote(blk(w, l, 1 - c), blk(w, l, 1 - c), send_sems, recv_sems, k, (x, y, 1 - c))
                    for k, (w, l) in enumerate(slots)]
        return sends, arrivals

    return _Stage(bufs, [SDS(b.shape, b.dtype) for b in bufs], {i: i for i in range(n)}, len(slots), copies)


def _stage_scratch(stage):
    return [pltpu.SemaphoreType.DMA((stage.n_sems,)), pltpu.SemaphoreType.DMA((stage.n_sems,))]


def _run_stage(stage, name):
    n_in, n_out = len(stage.ins), len(stage.out_shapes)

    def body(*refs):
        ins, outs, send_sems, recv_sems = refs[:n_in], refs[n_in:n_in + n_out], refs[n_in + n_out], refs[n_in + n_out + 1]
        stage.start(ins, outs, send_sems, recv_sems)
        stage.finish(ins, outs, send_sems, recv_sems)

    return _pcall(body, name=name, out_shape=stage.out_shapes, in_specs=[HBM_SPEC] * n_in, out_specs=[HBM_SPEC] * n_out,
                  input_output_aliases=stage.aliases, scratch_shapes=_stage_scratch(stage))(*stage.ins)


def _call(body, name, grid, n_prefetch, in_specs, out_specs, out_shape, scratch, operands, semantics, carry=None):
    n_in, n_out, n_sc = len(in_specs), len(out_specs), len(scratch)
    if carry is None:
        gs = pltpu.PrefetchScalarGridSpec(num_scalar_prefetch=n_prefetch, grid=grid, in_specs=in_specs, out_specs=out_specs,
                                          scratch_shapes=scratch)
        res = _pcall(body, name=name, grid_spec=gs, out_shape=out_shape, compiler_params=_params(semantics))(*operands)
        return list(res), []
    s_in, s_out = len(carry.ins), len(carry.out_shapes)

    def carrying(*refs):
        o = n_prefetch
        pre, ins = refs[:o], refs[o:o + n_in]
        o += n_in
        sins = refs[o:o + s_in]
        o += s_in
        outs = refs[o:o + n_out]
        o += n_out
        souts = refs[o:o + s_out]
        o += s_out
        sc, send_sems, recv_sems = refs[o:o + n_sc], refs[o + n_sc], refs[o + n_sc + 1]
        first = functools.reduce(jnp.logical_and, [pl.program_id(a) == 0 for a in range(len(grid))])
        last = functools.reduce(jnp.logical_and, [pl.program_id(a) == g - 1 for a, g in enumerate(grid)])

        @pl.when(first)
        def _():
            carry.start(sins, souts, send_sems, recv_sems)

        body(*pre, *ins, *outs, *sc)

        @pl.when(last)
        def _():
            carry.finish(sins, souts, send_sems, recv_sems)

    gs = pltpu.PrefetchScalarGridSpec(
        num_scalar_prefetch=n_prefetch, grid=grid, in_specs=list(in_specs) + [HBM_SPEC] * s_in,
        out_specs=list(out_specs) + [HBM_SPEC] * s_out, scratch_shapes=list(scratch) + _stage_scratch(carry))
    aliases = {n_prefetch + n_in + a: n_out + b for a, b in carry.aliases.items()}
    res = _pcall(carrying, name=name, grid_spec=gs, out_shape=list(out_shape) + list(carry.out_shapes),
                 input_output_aliases=aliases, compiler_params=_params(("arbitrary",) * len(grid)))(*operands, *carry.ins)
    return list(res[:n_out]), list(res[n_out:])


def _all_reduce_small(v):
    n0 = v.shape[0]
    n = -(-n0 // 16) * 16
    v = jnp.pad(v, ((0, n - n0), (0, 0)))
    h = n // 2

    def body(v_ref, out_ref, from_sib, chip_sum, slots, send_sems, recv_sems):
        x, y, c = _coords()
        me = 2 * x + y
        sib = (x, y, 1 - c)
        rows = lambda half: pl.ds(pl.multiple_of(half * h, 8), h)
        first = _remote(v_ref, from_sib, send_sems, recv_sems, 0, sib)
        first.start()
        first.wait()
        mine, theirs = v_ref[...], from_sib[...]
        chip_sum[...] = jnp.where(c == 0, mine, theirs) + jnp.where(c == 0, theirs, mine)
        chips = _other_chips(x, y)
        cps = [_remote(chip_sum.at[rows(c)], slots.at[me], send_sems, recv_sems, 1 + j, (px, py, c))
               for j, (px, py) in enumerate(chips)]
        for cp in cps:
            cp.start()
        slots[me] = chip_sum[rows(c)]
        for j, (px, py) in enumerate(chips):
            _remote(chip_sum.at[rows(c)], slots.at[2 * px + py], send_sems, recv_sems, 1 + j, (px, py, c)).wait_recv()
        for cp in cps:
            cp.wait_send()
        out_ref[rows(c)] = ((slots[0] + slots[1]) + slots[2]) + slots[3]
        last = _remote(out_ref.at[rows(c)], out_ref.at[rows(c)], send_sems, recv_sems, 4, sib)
        last.start()
        _remote(out_ref.at[rows(1 - c)], out_ref.at[rows(1 - c)], send_sems, recv_sems, 4, sib).wait_recv()
        last.wait_send()

    out = _pcall(body, name="all_reduce_small", out_shape=SDS((n, LANES), F32),
                 in_specs=[pl.BlockSpec(memory_space=pltpu.VMEM)], out_specs=pl.BlockSpec(memory_space=pltpu.VMEM),
                 scratch_shapes=[pltpu.VMEM((n, LANES), F32), pltpu.VMEM((n, LANES), F32), pltpu.VMEM((N_CHIPS, h, LANES), F32),
                                 pltpu.SemaphoreType.DMA((5,)), pltpu.SemaphoreType.DMA((5,))],
                 compiler_params=pltpu.CompilerParams(vmem_limit_bytes=VMEM_LIMIT))(v)
    return out[:n0]


def _add_half(g, recv, c, name):
    _, R, C = g.shape
    rows = R // 2
    tr = _rows_tile(rows, C, 1 << 20)
    nb = rows // tr

    def body(s_ref, g_ref, r_ref, o32_ref, o16_ref):
        s = g_ref[...] + r_ref[...]
        o32_ref[...] = s
        o16_ref[...] = s.astype(BF16)

    blk = lambda k, i, s: (k, i, 0)
    gs = pltpu.PrefetchScalarGridSpec(
        num_scalar_prefetch=1, grid=(N_CHIPS, nb),
        in_specs=[pl.BlockSpec((1, tr, C), lambda k, i, s: (k, s[0] * nb + i, 0)), pl.BlockSpec((1, tr, C), blk)],
        out_specs=[pl.BlockSpec((1, tr, C), blk), pl.BlockSpec((1, tr, C), blk)])
    return _pcall(body, name=name, grid_spec=gs, out_shape=[SDS((N_CHIPS, rows, C), F32), SDS((N_CHIPS, rows, C), BF16)],
                  compiler_params=_params(("arbitrary", "arbitrary")))(_scalar(c), g, recv)


def _sum_into(p32, arrived, chip, c, layer, n_layers, prev, name):
    _, rows, C = p32.shape
    tr = _rows_tile(rows, C, 1 << 20)
    nb = rows // tr

    def body(chip_ref, c_ref, p_ref, a_ref, *rest):
        o_ref = rest[-1]
        o_ref[0] = ((p_ref[0] + a_ref[0].astype(F32)) + a_ref[1].astype(F32)) + a_ref[2].astype(F32)

    in_specs = [pl.BlockSpec((1, tr, C), lambda i, chip_ref, c_ref: (chip_ref[0], i, 0)),
                pl.BlockSpec((3, tr, C), lambda i, chip_ref, c_ref: (0, i, 0))]
    ins = [p32, arrived]
    aliases = {}
    if prev is not None:
        in_specs.append(pl.BlockSpec(memory_space=pl.ANY))
        ins.append(prev)
        aliases = {4: 0}
    gs = pltpu.PrefetchScalarGridSpec(
        num_scalar_prefetch=2, grid=(nb,), in_specs=in_specs,
        out_specs=pl.BlockSpec((1, tr, C), lambda i, chip_ref, c_ref: (layer, c_ref[0] * nb + i, 0)))
    return _pcall(body, name=name, grid_spec=gs, out_shape=SDS((n_layers, 2 * rows, C), F32), input_output_aliases=aliases,
                  compiler_params=_params(("arbitrary",)))(_scalar(chip), _scalar(c), *ins)


def _h_layout(D):
    G = N_BRANCHES * D
    off, o = {}, 0
    for name, w in [("g", G), ("qa", A_Q), ("cq", MLA_Q_RANK), ("ckv", MLA_KV_RANK), ("hu", SGU_WIDTH), ("hv", SGU_WIDTH),
                    ("ka", A_KV), ("va", A_KV), ("kr", LANES)]:
        assert o % w == 0, (name, o, w)
        off[name] = (o, w)
        o += w
    off["total"] = -(-o // 512) * 512
    return off


def _perm_w_in(w, lay):
    s = np.cumsum([0, A_Q, A_KV, A_KV, MLA_Q_RANK, MLA_KV_RANK, MLA_ROPE, SGU_WIDTH, SGU_WIDTH])
    qa, ka, va, cq, ckv, kr, hu, hv = [w[:, s[i]:s[i + 1]] for i in range(8)]
    g = w[:, s[8]:]
    pad = jnp.zeros((w.shape[0], lay["total"] - lay["kr"][0] - MLA_ROPE), w.dtype)
    return jnp.concatenate([g, qa, cq, ckv, hu, hv, ka, va, kr, pad], axis=1)


def _unperm_w_in(wp, lay, D):
    take = lambda n, width=None: wp[:, lay[n][0]:lay[n][0] + (width or lay[n][1])]
    return jnp.concatenate([take("qa"), take("ka"), take("va"), take("cq"), take("ckv"), take("kr", MLA_ROPE), take("hu"),
                            take("hv"), take("g")], axis=1)


def _perm_w_uq(w):
    r = w.shape[0]
    w3 = w.reshape(r, MLA_HEADS, MLA_NOPE + MLA_ROPE)
    nope = w3[:, :, :MLA_NOPE].reshape(r, MLA_HEADS * MLA_NOPE)
    rope = jnp.pad(w3[:, :, MLA_NOPE:], ((0, 0), (0, 0), (0, LANES - MLA_ROPE))).reshape(r, MLA_HEADS * LANES)
    return jnp.concatenate([nope, rope], axis=1)


def _unperm_w_uq(wp):
    r = wp.shape[0]
    nope = wp[:, :MLA_HEADS * MLA_NOPE].reshape(r, MLA_HEADS, MLA_NOPE)
    rope = wp[:, MLA_HEADS * MLA_NOPE:].reshape(r, MLA_HEADS, LANES)[:, :, :MLA_ROPE]
    return jnp.concatenate([nope, rope], axis=2).reshape(r, MLA_HEADS * (MLA_NOPE + MLA_ROPE))


def _perm_w_ukv(w):
    r = w.shape[0]
    w3 = w.reshape(r, MLA_HEADS, MLA_NOPE + MLA_V)
    return jnp.concatenate([w3[:, :, :MLA_NOPE].reshape(r, -1), w3[:, :, MLA_NOPE:].reshape(r, -1)], axis=1)


def _unperm_w_ukv(wp):
    r = wp.shape[0]
    k = wp[:, :MLA_HEADS * MLA_NOPE].reshape(r, MLA_HEADS, MLA_NOPE)
    v = wp[:, MLA_HEADS * MLA_NOPE:].reshape(r, MLA_HEADS, MLA_V)
    return jnp.concatenate([k, v], axis=2).reshape(r, -1)


def _col_chunks(g):
    r, c4 = g.shape
    return jnp.transpose(g.reshape(r, N_CHIPS, c4 // N_CHIPS), (1, 0, 2))


def kernel(x, positions, w_in, b_gate, sinks, q_norm_g, kv_norm_g, w_uq, w_ukv, sgu_ln_g, sgu_ln_b, sgu_w, sgu_b, w_proj_a, w_proj_b, w_proj_c, w_o, ln1_g, ln1_b, w_up, conv_w, conv_b, w_down, ln2_g, ln2_b, loss_target, m_w_in, m_b_gate, m_sinks, m_q_norm_g, m_kv_norm_g, m_w_uq, m_w_ukv, m_sgu_ln_g, m_sgu_ln_b, m_sgu_w, m_sgu_b, m_w_proj_a, m_w_proj_b, m_w_proj_c, m_w_o, m_ln1_g, m_ln1_b, m_w_up, m_conv_w, m_conv_b, m_w_down, m_ln2_g, m_ln2_b, v_w_in, v_b_gate, v_sinks, v_q_norm_g, v_kv_norm_g, v_w_uq, v_w_ukv, v_sgu_ln_g, v_sgu_ln_b, v_sgu_w, v_sgu_b, v_w_proj_a, v_w_proj_b, v_w_proj_c, v_w_o, v_ln1_g, v_ln1_b, v_w_up, v_conv_w, v_conv_b, v_w_down, v_ln2_g, v_ln2_b):
    a = locals()
    W = {k: a[k] for k in WEIGHTS}
    Mo = {k: a["m_" + k] for k in WEIGHTS}
    Vo = {k: a["v_" + k] for k in WEIGHTS}
    S, D = x.shape[1], x.shape[2]
    FF2 = w_up.shape[2] * N_CHIPS
    FF = FF2 // 2
    L = DEPTH
    lay = _h_layout(D)
    NP = lay["total"]
    cx, cy, cc = _coords()
    chip = 2 * cx + cy
    T = _tile(S, 512)
    TM = _tile(S, 256, 16)
    TMW = _tile(S, 64, 16)

    shards = {k: tuple(W[k].shape) for k in BIG}
    full = {k: [None] * L for k in BIG}
    own = {(l, k): _cast_into_slot(W[k], l, chip, f"cast_{k}_l{l}") for l in range(L) for k in BIG}

    on_way = {"pairs": [], "arrived": []}

    def lay_out(pairs, gathered):
        for (l_, k), g in zip(pairs, gathered):
            _, r, c_ = shards[k]
            full[k][l_] = g.reshape(N_CHIPS * r, c_) if k in ROW_SHARDED else jnp.transpose(g, (1, 0, 2)).reshape(r, N_CHIPS * c_)

    def gather_behind(pairs):
        to_sib = _gather_stages(on_way["arrived"])[1](on_way["arrived"]) if on_way["pairs"] else None
        bufs = [own[p_] for p_ in pairs]
        return _both(to_sib, _gather_stages(bufs)[0](bufs) if pairs else None)

    def gathered_behind(pairs, outs):
        n_done = len(on_way["pairs"])
        lay_out(on_way["pairs"], outs[:n_done])
        on_way["pairs"], on_way["arrived"] = list(pairs), list(outs[n_done:])

    def mm_behind(a_, b_, mode, dt, name, pairs):
        stage = gather_behind(pairs)
        if stage is None:
            return _mm(a_, b_, mode, dt, name)
        out, outs = _mm(a_, b_, mode, dt, name, carry=stage)
        gathered_behind(pairs, outs)
        return out

    MIX = ["w_uq", "w_ukv", "w_proj_a", "w_proj_b", "w_proj_c", "w_o"]
    nxt = lambda l, names: [(l + 1, k) for k in names] if l + 1 < L else []
    first = [(0, "w_in")]
    gathered_behind(first, _run_stage(gather_behind(first), "gather_ici_w_in_l0"))
    gathered_behind([], _run_stage(gather_behind([]), "gather_sibling_w_in_l0"))

    small_sharded_full = {k: tuple(W[k].shape[:-1]) + (W[k].shape[-1] * N_CHIPS,) for k in SMALL_SHARDED}
    placed = []
    for k in ("b_gate", "conv_w"):
        z = jnp.zeros(small_sharded_full[k], F32)
        z = lax.dynamic_update_slice_in_dim(z, W[k], chip * W[k].shape[-1], axis=-1)
        placed.append(jnp.where(cc == 0, z, 0.0).reshape(-1))
    pv = jnp.concatenate(placed)
    n_pv = pv.shape[0]
    pv = jnp.pad(pv, (0, -n_pv % (8 * LANES))).reshape(-1, LANES)
    pv = _all_reduce_small(pv).reshape(-1)
    nb_ = int(np.prod(small_sharded_full["b_gate"]))
    b_gate_full = pv[:nb_].reshape(small_sharded_full["b_gate"])
    conv_w_full = pv[nb_:n_pv].reshape(small_sharded_full["conv_w"])

    inv_freq = ROPE_THETA ** (-jnp.arange(0, MLA_ROPE, 2, dtype=F32) / MLA_ROPE)
    ang = positions[0].astype(F32)[:, None] * inv_freq
    cos, sin = jnp.cos(ang), jnp.sin(ang)
    cos_t = jnp.concatenate([cos, cos, jnp.ones((S, LANES - MLA_ROPE), F32)], axis=1)
    sin_t = jnp.concatenate([sin, sin, jnp.zeros((S, LANES - MLA_ROPE), F32)], axis=1)

    row = lambda v: v.reshape(1, -1)
    cb = lambda name: lay[name][0] // lay[name][1]

    xs = x[0]
    saved = []
    for l in range(L):
        p = dict(
            w_in=_perm_w_in(full["w_in"][l], lay),
            sinks=row(sinks[l]), qg=row(q_norm_g[l]), kvg=row(kv_norm_g[l]), sg=row(sgu_ln_g[l]), sb=row(sgu_ln_b[l]),
            sw=sgu_w[l], sb3=sgu_b[l].reshape(SGU_GROUPS, SGU_CHUNK, 1),
            bg=b_gate_full[l], l1g=row(ln1_g[l]), l1b=row(ln1_b[l]), cw=conv_w_full[l], cbias=row(conv_b[l]),
            l2g=row(ln2_g[l]), l2b=row(ln2_b[l]))
        if l == 0:
            def fn_cast(i, rows, ps):
                return [rows[0]], []
            (xb,), _ = _rowwise(fn_cast, [_whole(xs)], [], [(D, BF16)], [], TM, "cast_x")
        own_mix = [(l, k) for k in MIX]
        own_up = [(l, "w_up")] if l == 0 else []
        own_down = [(l, "w_down")] if l == 0 else []
        h = mm_behind(xb, p["w_in"], "nn", BF16, "mm_h", own_mix)
        (y_a,), outs = _swa_fwd(h, cb("qa"), cb("ka"), cb("va"), p["sinks"], carry=gather_behind(own_up))
        gathered_behind(own_up, outs)
        p.update(w_uq=_perm_w_uq(full["w_uq"][l]), w_ukv=_perm_w_ukv(full["w_ukv"][l]), w_pa=full["w_proj_a"][l],
                 w_pb=full["w_proj_b"][l], w_pc=full["w_proj_c"][l], w_o=full["w_o"][l])
        def fn_rms(i, rows, ps):
            return [_rms_norm(rows[0].astype(F32), ps[0]), _rms_norm(rows[1].astype(F32), ps[1])], []
        (cqn, ckvn), _ = _rowwise(fn_rms, [(h, MLA_Q_RANK, cb("cq")), (h, MLA_KV_RANK, cb("ckv"))], [p["qg"], p["kvg"]],
                                  [(MLA_Q_RANK, BF16), (MLA_KV_RANK, BF16)], [], TM, "mla_rms")
        q_full = _mm(cqn, p["w_uq"], "nn", BF16, "mm_q")
        kv = _mm(ckvn, p["w_ukv"], "nn", BF16, "mm_kv")
        qr, kr = _rope_call(q_full, MLA_HEADS * LANES, 1, h, LANES, cb("kr"), cos_t, sin_t, 1.0, TM, "rope_fwd")
        behind_mla = own_down + nxt(l, ["w_in"])
        (y_b, lse), outs = _mla_fwd(q_full, qr, kv, kr, T, carry=gather_behind(behind_mla))
        gathered_behind(behind_mla, outs)
        y_c = _sgu_fwd(h, cb("hu"), cb("hv"), p["sg"], p["sb"], p["sw"], p["sb3"])
        pa = _mm(y_a, p["w_pa"], "nn", F32, "mm_pa")
        pb = _mm(y_b, p["w_pb"], "nn", F32, "mm_pb")
        pc = _mm(y_c, p["w_pc"], "nn", F32, "mm_pc")

        def merge_math(pa_, pb_, pc_, g_, b0, b1, b2):
            out = 0.0
            for br, (pp, bb) in enumerate(zip((pa_, pb_, pc_), (b0, b1, b2))):
                gate = jax.nn.sigmoid(g_[:, br * D:(br + 1) * D].astype(F32) + bb)
                out = out + gate * pp
            return out

        def fn_merge(i, rows, ps):
            bgv = ps[0]
            return [merge_math(rows[0], rows[1], rows[2], rows[3], bgv[0:1], bgv[1:2], bgv[2:3])], []
        (merged,), _ = _rowwise(fn_merge, [_whole(pa), _whole(pb), _whole(pc), (h, N_BRANCHES * D, cb("g"))], [p["bg"]],
                                [(D, BF16)], [], TMW, "merge_fwd")
        o = _mm(merged, p["w_o"], "nn", F32, "mm_o")

        def ln_res_math(x_, o_, g_, b_):
            return _layer_norm(DN_ALPHA * x_ + o_, g_, b_)

        def fn_ln(i, rows, ps):
            y = ln_res_math(rows[0], rows[1], ps[0], ps[1])
            return [y, y], []
        (x1, x1b), _ = _rowwise(fn_ln, [_whole(xs), _whole(o)], [p["l1g"], p["l1b"]], [(D, F32), (D, BF16)], [], TM, "ln1_fwd")
        p.update(w_up=full["w_up"][l])
        up = mm_behind(x1b, p["w_up"], "nn", BF16, "mm_up", nxt(l, ["w_up"]))
        p.update(w_down=full["w_down"][l])
        cv_ = _conv_fwd(up, p["cw"], p["cbias"])

        def glu_math(cg, cvv):
            return jax.nn.silu(cg.astype(F32)) * cvv.astype(F32)

        def fn_glu(i, rows, ps):
            return [glu_math(rows[0], rows[1])], []
        (act,), _ = _rowwise(fn_glu, [(cv_, FF, 0), (cv_, FF, 1)], [], [(FF, BF16)], [], TMW, "glu_fwd")
        dn = mm_behind(act, p["w_down"], "nn", F32, "mm_down", nxt(l, ["w_down"]))
        (x2, x2b), _ = _rowwise(fn_ln, [_whole(x1), _whole(dn)], [p["l2g"], p["l2b"]], [(D, F32), (D, BF16)], [], TM, "ln2_fwd")
        saved.append(dict(p=p, x0=xs, x0b=xb, h=h, y_a=y_a, cqn=cqn, ckvn=ckvn, q_full=q_full, kv=kv, qr=qr, kr=kr, y_b=y_b,
                          lse=lse, y_c=y_c, pa=pa, pb=pb, pc=pc, merged=merged, o=o, x1=x1, x1b=x1b, up=up, cv=cv_, act=act,
                          dn=dn))
        xs, xb = x2, x2b

    def fn_loss(i, rows, ps):
        diff = rows[0] - rows[1]
        part = jnp.sum(jnp.mean(jnp.square(diff), axis=-1, keepdims=True), axis=0, keepdims=True)
        return [diff * (1.0 / D)], [jnp.broadcast_to(part, (8, LANES))]
    (dx,), (loss_acc,) = _rowwise(fn_loss, [_whole(xs), _whole(loss_target[0])], [], [(D, F32)], [(8, LANES)], TM, "loss")
    loss = lax.psum(0.5 * loss_acc[0, 0], ("x", "y", "c"))

    gbig = {k: [None] * L for k in BIG}
    gsmall = {k: [None] * L for k in SMALL}
    reduced = {}
    pending = None
    for l in reversed(range(L)):
        s = saved[l]
        p = s["p"]

        def fn_ln_bwd(i, rows, ps):
            _, vjp = jax.vjp(ln_res_math, rows[0], rows[1], ps[0], ps[1])
            dx_, do_, dg_, db_ = vjp(rows[2])
            return [dx_, do_], [dg_, db_]
        (dx1_res, ddn), (g_l2g, g_l2b) = _rowwise(fn_ln_bwd, [_whole(s["x1"]), _whole(s["dn"]), _whole(dx)], [p["l2g"], p["l2b"]],
                                                  [(D, F32), (D, BF16)], [(1, D), (1, D)], TM, "ln2_bwd")
        gsmall["ln2_g"][l], gsmall["ln2_b"][l] = g_l2g, g_l2b
        row_chunks = lambda g: g.reshape(N_CHIPS, g.shape[0] // N_CHIPS, g.shape[1])
        gbig["w_down"][l] = row_chunks(_mm_tn(s["act"], ddn, F32, "mm_dw_down"))
        dact = _mm(ddn, p["w_down"], "nt", BF16, "mm_dact")

        def fn_glu_bwd(i, rows, ps):
            _, vjp = jax.vjp(glu_math, rows[0], rows[1])
            dcg, dcv = vjp(rows[2].astype(F32))
            return [jnp.concatenate([dcg, dcv], axis=1)], []
        (dc,), _ = _rowwise(fn_glu_bwd, [(s["cv"], FF, 0), (s["cv"], FF, 1), _whole(dact)], [], [(FF2, BF16)], [], TMW, "glu_bwd")
        dup, g_cw, g_cb = _conv_bwd(s["up"], dc, p["cw"])
        gsmall["conv_w"][l], gsmall["conv_b"][l] = g_cw, g_cb
        if pending is None:
            gbig["w_up"][l] = _mm_tn(s["x1b"], dup, F32, "mm_dw_up", tm=1024, col_chunks=N_CHIPS)
        else:
            gbig["w_up"][l], from_sibling = _mm_tn(s["x1b"], dup, F32, "mm_dw_up", tm=1024, col_chunks=N_CHIPS,
                                                carry=_swap_stage(pending))
            partial = [_add_half(g, r, cc, f"rs_add_{k}_l{l + 1}") for k, g, r in zip(BIG, pending, from_sibling)]
        dx1 = _mm(dup, p["w_up"], "nt", F32, "mm_dx1", add=dx1_res)
        (dx0_res, do_), (g_l1g, g_l1b) = _rowwise(fn_ln_bwd, [_whole(s["x0"]), _whole(s["o"]), _whole(dx1)], [p["l1g"], p["l1b"]],
                                                  [(D, F32), (D, BF16)], [(1, D), (1, D)], TM, "ln1_bwd")
        gsmall["ln1_g"][l], gsmall["ln1_b"][l] = g_l1g, g_l1b
        gbig["w_o"][l] = row_chunks(_mm_tn(s["merged"], do_, F32, "mm_dw_o"))
        dmerged = _mm(do_, p["w_o"], "nt", F32, "mm_dmerged")

        def fn_merge_bwd(i, rows, ps):
            bgv = ps[0]
            _, vjp = jax.vjp(merge_math, rows[0], rows[1], rows[2], rows[3], bgv[0:1], bgv[1:2], bgv[2:3])
            dpa, dpb, dpc, dg_, db0, db1, db2 = vjp(rows[4])
            return [dpa, dpb, dpc, dg_], [db0, db1, db2]
        (dpa, dpb, dpc, dgl), (db0, db1, db2) = _rowwise(
            fn_merge_bwd, [_whole(s["pa"]), _whole(s["pb"]), _whole(s["pc"]), (s["h"], N_BRANCHES * D, cb("g")), _whole(dmerged)],
            [p["bg"]], [(D, BF16), (D, BF16), (D, BF16), (N_BRANCHES * D, BF16)], [(1, D)] * 3, TMW, "merge_bwd")
        gsmall["b_gate"][l] = jnp.concatenate([db0, db1, db2], axis=0)
        gbig["w_proj_a"][l] = _mm_tn(s["y_a"], dpa, F32, "mm_dw_pa", col_chunks=N_CHIPS)
        gbig["w_proj_b"][l] = row_chunks(_mm_tn(s["y_b"], dpb, F32, "mm_dw_pb"))
        gbig["w_proj_c"][l] = _mm_tn(s["y_c"], dpc, F32, "mm_dw_pc", col_chunks=N_CHIPS)
        dy_a = _mm(dpa, p["w_pa"], "nt", BF16, "mm_dy_a")
        dy_b = _mm(dpb, p["w_pb"], "nt", BF16, "mm_dy_b")
        dy_c = _mm(dpc, p["w_pc"], "nt", BF16, "mm_dy_c")
        dh_c, g_sg, g_sb, g_sw, g_sb3 = _sgu_bwd(s["h"], cb("hu"), cb("hv"), dy_c, p["sg"], p["sb"], p["sw"], p["sb3"])
        gsmall["sgu_ln_g"][l], gsmall["sgu_ln_b"][l], gsmall["sgu_w"][l] = g_sg, g_sb, g_sw
        gsmall["sgu_b"][l] = g_sb3.reshape(SGU_GROUPS, SGU_CHUNK)
        part_x = ["w_up", "w_down"]
        part_z = ["w_in"] if l == 0 else []
        part_y = [k for k in BIG if k not in part_x + part_z]
        scatter_of = lambda names: None if pending is None else _scatter_stage([partial[BIG.index(k)][1] for k in names])
        hide_early = l == 0
        early = [gbig[k][l] for k in EARLY]
        n_early = len(EARLY) if hide_early else 0
        (dqa, dka, dva, g_sinks), landed_ = _swa_bwd(s["h"], cb("qa"), cb("ka"), cb("va"), p["sinks"], dy_a,
                                                     carry=_both(_swap_stage(early) if hide_early else None, scatter_of(part_x)))
        early_from_sibling, arrived_x = landed_[:n_early], landed_[n_early:]
        gsmall["sinks"][l] = g_sinks
        if hide_early:
            early_partial = [_add_half(g, r, cc, f"rs_add_{k}_l{l}") for k, g, r in zip(EARLY, early, early_from_sibling)]
        delta = _mla_delta(dy_b, s["y_b"], T)
        (dqn, dqr), arrived_y = _mla_bwd_dq(s["q_full"], s["qr"], s["kv"], s["kr"], dy_b, s["lse"], delta, T,
                                            carry=scatter_of(part_y))
        if pending is not None:
            arrived = {**dict(zip(part_x, arrived_x)), **dict(zip(part_y, arrived_y))}
            for k, (p32, _) in zip(BIG, partial):
                if k in arrived:
                    reduced[k] = _sum_into(p32, arrived[k], chip, cc, l + 1, L, reduced.get(k), f"rs_sum_{k}_l{l + 1}")
        (dkn, dv, dkr_heads), early_arrived = _mla_bwd_dkv(
            s["q_full"], s["qr"], s["kv"], s["kr"], dy_b, s["lse"], delta, T,
            carry=_scatter_stage([p16 for _, p16 in early_partial]) if hide_early else None)
        if hide_early:
            for k, (p32, _), arr in zip(EARLY, early_partial, early_arrived):
                reduced[k] = _sum_into(p32, arr, chip, cc, l, L, reduced.get(k), f"rs_sum_{k}_l{l}")
        dkr = _sum_heads(dkr_heads, TM)
        dqr_raw, dkr_raw = _rope_call(dqr, MLA_HEADS * LANES, 0, dkr, LANES, 0, cos_t, sin_t, -1.0, TM, "rope_bwd")
        dq_full = jnp.concatenate([dqn, dqr_raw], axis=1)
        dkv = jnp.concatenate([dkn, dv], axis=1)
        gbig["w_uq"][l] = _col_chunks(_unperm_w_uq(_mm_tn(s["cqn"], dq_full, F32, "mm_dw_uq")))
        gbig["w_ukv"][l] = _col_chunks(_unperm_w_ukv(_mm_tn(s["ckvn"], dkv, F32, "mm_dw_ukv")))
        dcqn = _mm(dq_full, p["w_uq"], "nt", F32, "mm_dcqn")
        dckvn = _mm(dkv, p["w_ukv"], "nt", F32, "mm_dckvn")

        def fn_rms_bwd(i, rows, ps):
            _, vjp1 = jax.vjp(lambda c_, g_: _rms_norm(c_.astype(F32), g_), rows[0], ps[0])
            _, vjp2 = jax.vjp(lambda c_, g_: _rms_norm(c_.astype(F32), g_), rows[1], ps[1])
            d1, dg1 = vjp1(rows[2])
            d2, dg2 = vjp2(rows[3])
            return [d1, d2], [dg1, dg2]
        (dcq, dckv), (g_qg, g_kvg) = _rowwise(
            fn_rms_bwd, [(s["h"], MLA_Q_RANK, cb("cq")), (s["h"], MLA_KV_RANK, cb("ckv")), _whole(dcqn), _whole(dckvn)],
            [p["qg"], p["kvg"]], [(MLA_Q_RANK, BF16), (MLA_KV_RANK, BF16)], [(1, MLA_Q_RANK), (1, MLA_KV_RANK)], TM, "mla_rms_bwd")
        gsmall["q_norm_g"][l], gsmall["kv_norm_g"][l] = g_qg, g_kvg
        tail = jnp.zeros((S, NP - lay["kr"][0] - LANES), BF16)
        dh = jnp.concatenate([dgl, dqa, dcq, dckv, dh_c, dka, dva, dkr_raw, tail], axis=1)
        if l == 0:
            ready = [k for k in BIG if k not in part_z]
            done = [list(range(L)) if k in EARLY else list(range(1, L)) for k in ready]
            dw_in, outs = _mm_tn(s["x0b"], dh, F32, "mm_dw_in", tm=1024,
                                 carry=_both(_join_stage([reduced[k] for k in ready], done), scatter_of(part_z)))
            reduced.update(zip(ready, outs[:len(ready)]))
            for k, arr in zip(part_z, outs[len(ready):]):
                reduced[k] = _sum_into(partial[BIG.index(k)][0], arr, chip, cc, l + 1, L, reduced.get(k), f"rs_sum_{k}_l{l + 1}")
        else:
            dw_in = _mm_tn(s["x0b"], dh, F32, "mm_dw_in", tm=1024)
        gbig["w_in"][l] = _col_chunks(_unperm_w_in(dw_in, lay, D))
        if l > 0:
            dx = _mm(dh, p["w_in"], "nt", F32, "mm_dx0", add=dx0_res)

        pending = [gbig[k][l] for k in BIG]

    late = [gbig[k][0] for k in LATE]
    from_sibling = _run_stage(_swap_stage(late), "rs_swap_halves_l0")
    partial = [_add_half(g, r, cc, f"rs_add_{k}_l0") for k, g, r in zip(LATE, late, from_sibling)]
    dx, arrived = _mm(dh, p["w_in"], "nt", F32, "mm_dx0", add=dx0_res, carry=_scatter_stage([p16 for _, p16 in partial]))
    for k, (p32, _), arr in zip(LATE, partial, arrived):
        reduced[k] = _sum_into(p32, arr, chip, cc, 0, L, reduced.get(k), f"rs_sum_{k}_l0")

    grad_x = dx.reshape(x.shape)
    unjoined = [[0] + (list(range(1, L)) if k in part_z else []) for k in LATE]
    g_big = {**reduced, **dict(zip(LATE, _run_stage(_join_stage([reduced[k] for k in LATE], unjoined), "rs_join_halves")))}

    small_shapes = {k: (small_sharded_full[k] if k in SMALL_SHARDED else tuple(W[k].shape)) for k in SMALL}
    sv = jnp.concatenate([jnp.stack(gsmall[k]).reshape(-1) for k in SMALL])
    n_sv = sv.shape[0]
    sv = jnp.pad(sv, (0, -n_sv % (8 * LANES))).reshape(-1, LANES)
    sv = _all_reduce_small(sv).reshape(-1)
    g_small, o_ = {}, 0
    for k in SMALL:
        n = int(np.prod(small_shapes[k]))
        g = sv[o_:o_ + n].reshape(small_shapes[k])
        if k in SMALL_SHARDED:
            g = lax.dynamic_slice_in_dim(g, chip * W[k].shape[-1], W[k].shape[-1], axis=-1)
        g_small[k] = g
        o_ += n

    delta, new_m, new_v = {}, {}, {}
    swap_minor = lambda t: jnp.transpose(t, (0, 2, 1))
    for k in BIG:
        if shards[k][2] % LANES and not shards[k][1] % LANES:
            out = _adamw(swap_minor(W[k]), swap_minor(g_big[k]), swap_minor(Mo[k]), swap_minor(Vo[k]), "adamw_" + k)
            delta[k], new_m[k], new_v[k] = [swap_minor(t) for t in out]
        else:
            delta[k], new_m[k], new_v[k] = _adamw(W[k], g_big[k], Mo[k], Vo[k], "adamw_" + k)
    pack = lambda t: jnp.concatenate([t[k].reshape(-1) for k in SMALL])
    n_small = sum(int(np.prod(W[k].shape)) for k in SMALL)
    pad2 = lambda t: jnp.pad(t, (0, -n_small % (8 * LANES))).reshape(1, -1, LANES)
    d_, m_, v_ = _adamw(pad2(pack(W)), pad2(pack(g_small)), pad2(pack(Mo)), pad2(pack(Vo)), "adamw_small")
    o_ = 0
    for k in SMALL:
        n = int(np.prod(W[k].shape))
        take = lambda t: t.reshape(-1)[o_:o_ + n].reshape(W[k].shape)
        delta[k], new_m[k], new_v[k] = take(d_), take(m_), take(v_)
        o_ += n

    grads = {**g_big, **g_small}
    return (loss, grad_x, *[grads[k] for k in WEIGHTS], *[delta[k] for k in WEIGHTS], *[new_m[k] for k in WEIGHTS],
            *[new_v[k] for k in WEIGHTS])
```
